```python
import math
import jax, jax.numpy as jnp
from jax import lax
import numpy as np

D_MODEL = 1024
BATCH = 8
SEQ = 4096
DEPTH = 2

N_A_LAYERS = DEPTH // 2
N_B_LAYERS = DEPTH - N_A_LAYERS
RMS_EPS = 1e-6
GATED_NORM_EPS = 1e-5

SSM_EXPAND = 2
SSM_D_INNER = SSM_EXPAND * D_MODEL
SSM_HEAD_DIM = 64
SSM_N_HEADS = SSM_D_INNER // SSM_HEAD_DIM
SSM_N_GROUPS = 8
SSM_D_STATE = 128
SSM_CONV = 4
SSM_CHUNK = 128
SSM_CONV_DIM = SSM_D_INNER + 2 * SSM_N_GROUPS * SSM_D_STATE
SSM_IN_DIM = SSM_D_INNER + SSM_CONV_DIM + SSM_N_HEADS

ATT_PATTERNS = ((128, 1), (512, 4), (2048, 16))
ATT_N_GROUPS = len(ATT_PATTERNS)
ATT_HEAD_DIM = 128
ATT_HEADS_PER_GROUP = 8
ATT_KV_HEADS_PER_GROUP = 2
ATT_Q_DIM = ATT_N_GROUPS * ATT_HEADS_PER_GROUP * ATT_HEAD_DIM
ATT_KV_DIM = ATT_N_GROUPS * ATT_KV_HEADS_PER_GROUP * ATT_HEAD_DIM
ATT_OUT_DIM = ATT_HEADS_PER_GROUP * ATT_HEAD_DIM
ROPE_DIM = ATT_HEAD_DIM // 4
ROPE_THETA = 500000.0

FFN_DIM = 2816
FFN_CONV = 3

kernel_name = 'yoco_mamba2_dilated_attn_hybrid'


def _rms_norm(x, w, eps=RMS_EPS):
    xf = x.astype(jnp.float32)
    y = xf * lax.rsqrt(jnp.mean(xf * xf, axis=-1, keepdims=True) + eps)
    return (y * w.astype(jnp.float32)).astype(x.dtype)


def _causal_depthwise_conv(x, w):
    width, ch = w.shape
    return lax.conv_general_dilated(
        x, w[:, None, :].astype(x.dtype), window_strides=(1,),
        padding=((width - 1, 0),), dimension_numbers=('NWC', 'WIO', 'NWC'),
        feature_group_count=ch)


def _partial_rotary(x, positions):
    half = ROPE_DIM // 2
    inv_freq = jnp.power(jnp.float32(ROPE_THETA), -jnp.arange(0, ROPE_DIM, 2, dtype=jnp.float32) / ROPE_DIM)
    ang = positions.astype(jnp.float32)[:, None] * inv_freq[None, :]
    cos = jnp.cos(ang)[None, :, None, :]
    sin = jnp.sin(ang)[None, :, None, :]
    xf = x.astype(jnp.float32)
    x1, x2, rest = xf[..., :half], xf[..., half:ROPE_DIM], xf[..., ROPE_DIM:]
    out = jnp.concatenate([x1 * cos - x2 * sin, x2 * cos + x1 * sin, rest], axis=-1)
    return out.astype(x.dtype)


def _ssd_chunked(x, dt, a, b_in, c_in):
    bsz, seq, nh, hp = x.shape
    ng, ns = b_in.shape[2], b_in.shape[3]
    hg = nh // ng
    nc, L = seq // SSM_CHUNK, SSM_CHUNK
    xc = (x.astype(jnp.float32) * dt[..., None]).reshape(bsz, nc, L, ng, hg, hp)
    adt = (dt * a).reshape(bsz, nc, L, ng, hg).transpose(0, 1, 3, 4, 2)
    a_cs = jnp.cumsum(adt, axis=-1)
    bc = b_in.astype(jnp.float32).reshape(bsz, nc, L, ng, ns)
    cc = c_in.astype(jnp.float32).reshape(bsz, nc, L, ng, ns)
    tril = jnp.tril(jnp.ones((L, L), dtype=bool))
    seg = a_cs[..., :, None] - a_cs[..., None, :]
    decay = jnp.exp(jnp.where(tril, seg, -jnp.inf))
    cb = jnp.einsum('bclgn,bcsgn->bcgls', cc, bc)
    y_diag = jnp.einsum('bcgls,bcghls,bcsghp->bclghp', cb, decay, xc)
    decay_states = jnp.exp(a_cs[..., -1:] - a_cs)
    states = jnp.einsum('bclgn,bcghl,bclghp->bcghpn', bc, decay_states, xc)
    chunk_decay = jnp.exp(a_cs[..., -1])

    def step(h, inp):
        st, dec = inp
        return h * dec[..., None, None] + st, h

    h0 = jnp.zeros((bsz, ng, hg, hp, ns), jnp.float32)
    _, prev = lax.scan(step, h0, (jnp.moveaxis(states, 1, 0), jnp.moveaxis(chunk_decay, 1, 0)))
    prev = jnp.moveaxis(prev, 0, 1)
    y_off = jnp.einsum('bclgn,bcghpn,bcghl->bclghp', cc, prev, jnp.exp(a_cs))
    return (y_diag + y_off).reshape(bsz, seq, nh, hp)


def _mamba2_mixer(h, w_in, conv_w, conv_b, dt_bias, a_log, d_skip, norm_w, w_out):
    bsz, seq, _ = h.shape
    zxbcdt = h @ w_in.astype(h.dtype)
    z = zxbcdt[..., :SSM_D_INNER]
    xbc = zxbcdt[..., SSM_D_INNER:SSM_D_INNER + SSM_CONV_DIM]
    dt_raw = zxbcdt[..., SSM_D_INNER + SSM_CONV_DIM:]
    xbc = jax.nn.silu(_causal_depthwise_conv(xbc, conv_w) + conv_b.astype(h.dtype))
    gn = SSM_N_GROUPS * SSM_D_STATE
    xs = xbc[..., :SSM_D_INNER].reshape(bsz, seq, SSM_N_HEADS, SSM_HEAD_DIM)
    b_in = xbc[..., SSM_D_INNER:SSM_D_INNER + gn].reshape(bsz, seq, SSM_N_GROUPS, SSM_D_STATE)
    c_in = xbc[..., SSM_D_INNER + gn:].reshape(bsz, seq, SSM_N_GROUPS, SSM_D_STATE)
    dt = jax.nn.softplus(dt_raw.astype(jnp.float32) + dt_bias.astype(jnp.float32))
    a = -jnp.exp(a_log.astype(jnp.float32))
    y = _ssd_chunked(xs, dt, a, b_in, c_in)
    y = y + xs.astype(jnp.float32) * d_skip.astype(jnp.float32)[:, None]
    y = y.reshape(bsz, seq, SSM_D_INNER) * jax.nn.silu(z.astype(jnp.float32))
    yg = y.reshape(bsz, seq, SSM_N_GROUPS, SSM_D_INNER // SSM_N_GROUPS)
    yg = yg * lax.rsqrt(jnp.mean(yg * yg, axis=-1, keepdims=True) + GATED_NORM_EPS)
    y = yg.reshape(bsz, seq, SSM_D_INNER) * norm_w.astype(jnp.float32)
    return y.astype(h.dtype) @ w_out.astype(h.dtype)


def _shared_kv(x, kv_norm, w_kv, positions):
    bsz, seq, _ = x.shape
    kv = (_rms_norm(x, kv_norm) @ w_kv.astype(x.dtype)).reshape(
        bsz, seq, 2, ATT_N_GROUPS * ATT_KV_HEADS_PER_GROUP, ATT_HEAD_DIM)
    k = _partial_rotary(kv[:, :, 0], positions).reshape(bsz, seq, ATT_N_GROUPS, ATT_KV_HEADS_PER_GROUP, ATT_HEAD_DIM)
    v = kv[:, :, 1].reshape(bsz, seq, ATT_N_GROUPS, ATT_KV_HEADS_PER_GROUP, ATT_HEAD_DIM)
    return k, v


def _dilated_group_attention(q, k, v, window, dilation):
    bsz, seq, n_heads, hd = q.shape
    n_kv = k.shape[2]
    rep = n_heads // n_kv
    blk = window // dilation
    n_sub = seq // dilation
    n_blk = -(-n_sub // blk)
    n_pad = n_blk * blk

    def to_blocks(t):
        t = jnp.moveaxis(t.reshape(bsz, n_sub, dilation, *t.shape[2:]), 2, 1)
        pad = [(0, 0)] * t.ndim
        pad[2] = (0, n_pad - n_sub)
        t = jnp.pad(t, pad)
        return t.reshape(bsz, dilation, n_blk, blk, *t.shape[3:])

    def with_prev(t):
        prev = jnp.pad(t[:, :, :-1], ((0, 0), (0, 0), (1, 0), (0, 0), (0, 0), (0, 0)))
        return jnp.concatenate([prev, t], axis=3)

    qb = to_blocks(q.astype(jnp.float32)).reshape(bsz, dilation, n_blk, blk, n_kv, rep, hd)
    kb = with_prev(to_blocks(k.astype(jnp.float32)))
    vb = with_prev(to_blocks(v.astype(jnp.float32)))
    scores = jnp.einsum('brnqhgd,brnshd->brnhgqs', qb, kb) * (hd ** -0.5)
    q_idx = jnp.arange(blk)[:, None]
    s_idx = jnp.arange(2 * blk)[None, :]
    rel = q_idx + blk - s_idx
    band = (rel >= 0) & (rel <= blk)
    has_prev = (jnp.arange(n_blk) > 0)[:, None, None] | (s_idx >= blk)[None]
    mask = band[None] & has_prev
    scores = jnp.where(mask[:, None, None], scores, -jnp.inf)
    m = jnp.max(scores, axis=-1, keepdims=True)
    p = jnp.exp(scores - m)
    denom = jnp.sum(p, axis=-1)
    lse = jnp.transpose(m[..., 0] + jnp.log(denom), (0, 1, 2, 5, 3, 4))
    out = jnp.einsum('brnhgqs,brnshd->brnqhgd', p, vb) / jnp.transpose(denom, (0, 1, 2, 5, 3, 4))[..., None]

    def from_blocks(t):
        t = t.reshape(bsz, dilation, n_pad, n_heads, *t.shape[6:])[:, :, :n_sub]
        return jnp.moveaxis(t, 1, 2).reshape(bsz, seq, n_heads, *t.shape[4:])

    return from_blocks(out), from_blocks(lse)


def _dilated_mixer(h, k_sh, v_sh, w_q, w_o, positions):
    bsz, seq, _ = h.shape
    q = (h @ w_q.astype(h.dtype)).reshape(bsz, seq, ATT_N_GROUPS * ATT_HEADS_PER_GROUP, ATT_HEAD_DIM)
    q = _partial_rotary(q, positions).reshape(bsz, seq, ATT_N_GROUPS, ATT_HEADS_PER_GROUP, ATT_HEAD_DIM)
    outs, lses = [], []
    for g, (window, dilation) in enumerate(ATT_PATTERNS):
        o_g, lse_g = _dilated_group_attention(q[:, :, g], k_sh[:, :, g], v_sh[:, :, g], window, dilation)
        outs.append(o_g)
        lses.append(lse_g)
    wts = jax.nn.softmax(jnp.stack(lses, axis=0), axis=0)
    o = jnp.einsum('gbsh,gbshd->bshd', wts, jnp.stack(outs, axis=0))
    return o.reshape(bsz, seq, ATT_OUT_DIM).astype(h.dtype) @ w_o.astype(h.dtype)


def _conv_ffn(h, w_up, conv_w, w_down):
    u = _causal_depthwise_conv(h @ w_up.astype(h.dtype), conv_w)
    gate, val = u[..., :FFN_DIM], u[..., FFN_DIM:]
    return (jax.nn.silu(gate) * val) @ w_down.astype(h.dtype)


def _fwd_setup_inputs(seed: int = 0) -> dict:
    key = jax.random.key(seed)
    ks = jax.random.split(key, 24)
    f32 = jnp.float32
    out_scale = (2.0 * DEPTH) ** -0.5

    def nrm(k, shape, scale):
        return jax.random.normal(k, shape, f32) * scale

    x = nrm(ks[0], (BATCH, SEQ, D_MODEL), 1.0)
    a_norm = 1.0 + nrm(ks[1], (N_A_LAYERS, D_MODEL), 0.02)
    ssm_w_in = nrm(ks[2], (N_A_LAYERS, D_MODEL, SSM_IN_DIM), D_MODEL ** -0.5)
    ssm_conv_w = nrm(ks[3], (N_A_LAYERS, SSM_CONV, SSM_CONV_DIM), SSM_CONV ** -0.5)
    ssm_conv_b = nrm(ks[4], (N_A_LAYERS, SSM_CONV_DIM), 0.02)
    dt0 = jnp.exp(jax.random.uniform(ks[5], (N_A_LAYERS, SSM_N_HEADS), f32, math.log(1e-3), math.log(1e-1)))
    ssm_dt_bias = dt0 + jnp.log(-jnp.expm1(-dt0))
    ssm_a_log = jnp.log(jax.random.uniform(ks[6], (N_A_LAYERS, SSM_N_HEADS), f32, 1.0, 16.0))
    ssm_d = 1.0 + nrm(ks[7], (N_A_LAYERS, SSM_N_HEADS), 0.02)
    ssm_norm = 1.0 + nrm(ks[8], (N_A_LAYERS, SSM_D_INNER), 0.02)
    ssm_w_out = nrm(ks[9], (N_A_LAYERS, SSM_D_INNER, D_MODEL), SSM_D_INNER ** -0.5 * out_scale)
    kv_norm = 1.0 + nrm(ks[10], (D_MODEL,), 0.02)
    w_kv = nrm(ks[11], (D_MODEL, 2 * ATT_KV_DIM), D_MODEL ** -0.5)
    b_norm = 1.0 + nrm(ks[12], (N_B_LAYERS, D_MODEL), 0.02)
    att_w_q = nrm(ks[13], (N_B_LAYERS, D_MODEL, ATT_Q_DIM), D_MODEL ** -0.5)
    att_w_o = nrm(ks[14], (N_B_LAYERS, ATT_OUT_DIM, D_MODEL), ATT_OUT_DIM ** -0.5 * out_scale)
    ffn_norm = 1.0 + nrm(ks[15], (DEPTH, D_MODEL), 0.02)
    ffn_w_up = nrm(ks[16], (DEPTH, D_MODEL, 2 * FFN_DIM), D_MODEL ** -0.5)
    ffn_conv_w = nrm(ks[17], (DEPTH, FFN_CONV, 2 * FFN_DIM), FFN_CONV ** -0.5)
    ffn_w_down = nrm(ks[18], (DEPTH, FFN_DIM, D_MODEL), FFN_DIM ** -0.5 * out_scale)
    final_norm = 1.0 + nrm(ks[19], (D_MODEL,), 0.02)
    return {'x': x, 'a_norm': a_norm, 'ssm_w_in': ssm_w_in, 'ssm_conv_w': ssm_conv_w,
            'ssm_conv_b': ssm_conv_b, 'ssm_dt_bias': ssm_dt_bias, 'ssm_a_log': ssm_a_log,
            'ssm_d': ssm_d, 'ssm_norm': ssm_norm, 'ssm_w_out': ssm_w_out, 'kv_norm': kv_norm,
            'w_kv': w_kv, 'b_norm': b_norm, 'att_w_q': att_w_q, 'att_w_o': att_w_o,
            'ffn_norm': ffn_norm, 'ffn_w_up': ffn_w_up, 'ffn_conv_w': ffn_conv_w,
            'ffn_w_down': ffn_w_down, 'final_norm': final_norm}


def _fwd_reference(x, a_norm, ssm_w_in, ssm_conv_w, ssm_conv_b, ssm_dt_bias, ssm_a_log, ssm_d,
              ssm_norm, ssm_w_out, kv_norm, w_kv, b_norm, att_w_q, att_w_o,
              ffn_norm, ffn_w_up, ffn_conv_w, ffn_w_down, final_norm):
    positions = jnp.arange(x.shape[1], dtype=jnp.int32)
    k_sh, v_sh = None, None
    for layer in range(DEPTH):
        if layer < N_A_LAYERS:
            i = layer
            x = x + _mamba2_mixer(_rms_norm(x, a_norm[i]), ssm_w_in[i], ssm_conv_w[i], ssm_conv_b[i],
                                  ssm_dt_bias[i], ssm_a_log[i], ssm_d[i], ssm_norm[i], ssm_w_out[i])
        else:
            if layer == N_A_LAYERS:
                k_sh, v_sh = _shared_kv(x, kv_norm, w_kv, positions)
            j = layer - N_A_LAYERS
            x = x + _dilated_mixer(_rms_norm(x, b_norm[j]), k_sh, v_sh, att_w_q[j], att_w_o[j], positions)
        x = x + _conv_ffn(_rms_norm(x, ffn_norm[layer]), ffn_w_up[layer], ffn_conv_w[layer], ffn_w_down[layer])
    return _rms_norm(x, final_norm)


import jax as _jax
import jax.numpy as _jnp

TWIN_FORMAT = 'train_step'
FWD_PARAMS = ['x', 'a_norm', 'ssm_w_in', 'ssm_conv_w', 'ssm_conv_b', 'ssm_dt_bias', 'ssm_a_log', 'ssm_d', 'ssm_norm', 'ssm_w_out', 'kv_norm', 'w_kv', 'b_norm', 'att_w_q', 'att_w_o', 'ffn_norm', 'ffn_w_up', 'ffn_conv_w', 'ffn_w_down', 'final_norm']
TWIN_WEIGHTS = ['a_norm', 'ssm_w_in', 'ssm_conv_w', 'ssm_conv_b', 'ssm_dt_bias', 'ssm_a_log', 'ssm_d', 'ssm_norm', 'ssm_w_out', 'kv_norm', 'w_kv', 'b_norm', 'att_w_q', 'att_w_o', 'ffn_norm', 'ffn_w_up', 'ffn_conv_w', 'ffn_w_down', 'final_norm']
TWIN_DIFF_INPUT = 'x'
TWIN_INPUTS = ['x', 'a_norm', 'ssm_w_in', 'ssm_conv_w', 'ssm_conv_b', 'ssm_dt_bias', 'ssm_a_log', 'ssm_d', 'ssm_norm', 'ssm_w_out', 'kv_norm', 'w_kv', 'b_norm', 'att_w_q', 'att_w_o', 'ffn_norm', 'ffn_w_up', 'ffn_conv_w', 'ffn_w_down', 'final_norm', 'loss_target', 'm_a_norm', 'm_ssm_w_in', 'm_ssm_conv_w', 'm_ssm_conv_b', 'm_ssm_dt_bias', 'm_ssm_a_log', 'm_ssm_d', 'm_ssm_norm', 'm_ssm_w_out', 'm_kv_norm', 'm_w_kv', 'm_b_norm', 'm_att_w_q', 'm_att_w_o', 'm_ffn_norm', 'm_ffn_w_up', 'm_ffn_conv_w', 'm_ffn_w_down', 'm_final_norm', 'v_a_norm', 'v_ssm_w_in', 'v_ssm_conv_w', 'v_ssm_conv_b', 'v_ssm_dt_bias', 'v_ssm_a_log', 'v_ssm_d', 'v_ssm_norm', 'v_ssm_w_out', 'v_kv_norm', 'v_w_kv', 'v_b_norm', 'v_att_w_q', 'v_att_w_o', 'v_ffn_norm', 'v_ffn_w_up', 'v_ffn_conv_w', 'v_ffn_w_down', 'v_final_norm']
TWIN_OUTPUTS = ['loss', 'grad_x', 'grad_a_norm', 'grad_ssm_w_in', 'grad_ssm_conv_w', 'grad_ssm_conv_b', 'grad_ssm_dt_bias', 'grad_ssm_a_log', 'grad_ssm_d', 'grad_ssm_norm', 'grad_ssm_w_out', 'grad_kv_norm', 'grad_w_kv', 'grad_b_norm', 'grad_att_w_q', 'grad_att_w_o', 'grad_ffn_norm', 'grad_ffn_w_up', 'grad_ffn_conv_w', 'grad_ffn_w_down', 'grad_final_norm', 'delta_a_norm', 'delta_ssm_w_in', 'delta_ssm_conv_w', 'delta_ssm_conv_b', 'delta_ssm_dt_bias', 'delta_ssm_a_log', 'delta_ssm_d', 'delta_ssm_norm', 'delta_ssm_w_out', 'delta_kv_norm', 'delta_w_kv', 'delta_b_norm', 'delta_att_w_q', 'delta_att_w_o', 'delta_ffn_norm', 'delta_ffn_w_up', 'delta_ffn_conv_w', 'delta_ffn_w_down', 'delta_final_norm', 'new_m_a_norm', 'new_m_ssm_w_in', 'new_m_ssm_conv_w', 'new_m_ssm_conv_b', 'new_m_ssm_dt_bias', 'new_m_ssm_a_log', 'new_m_ssm_d', 'new_m_ssm_norm', 'new_m_ssm_w_out', 'new_m_kv_norm', 'new_m_w_kv', 'new_m_b_norm', 'new_m_att_w_q', 'new_m_att_w_o', 'new_m_ffn_norm', 'new_m_ffn_w_up', 'new_m_ffn_conv_w', 'new_m_ffn_w_down', 'new_m_final_norm', 'new_v_a_norm', 'new_v_ssm_w_in', 'new_v_ssm_conv_w', 'new_v_ssm_conv_b', 'new_v_ssm_dt_bias', 'new_v_ssm_a_log', 'new_v_ssm_d', 'new_v_ssm_norm', 'new_v_ssm_w_out', 'new_v_kv_norm', 'new_v_w_kv', 'new_v_b_norm', 'new_v_att_w_q', 'new_v_att_w_o', 'new_v_ffn_norm', 'new_v_ffn_w_up', 'new_v_ffn_conv_w', 'new_v_ffn_w_down', 'new_v_final_norm']
TWIN_LEAF_KINDS = {'loss': 'loss', 'grad_x': 'grad_x', 'grad_a_norm': 'grad_w', 'grad_ssm_w_in': 'grad_w', 'grad_ssm_conv_w': 'grad_w', 'grad_ssm_conv_b': 'grad_w', 'grad_ssm_dt_bias': 'grad_w', 'grad_ssm_a_log': 'grad_w', 'grad_ssm_d': 'grad_w', 'grad_ssm_norm': 'grad_w', 'grad_ssm_w_out': 'grad_w', 'grad_kv_norm': 'grad_w', 'grad_w_kv': 'grad_w', 'grad_b_norm': 'grad_w', 'grad_att_w_q': 'grad_w', 'grad_att_w_o': 'grad_w', 'grad_ffn_norm': 'grad_w', 'grad_ffn_w_up': 'grad_w', 'grad_ffn_conv_w': 'grad_w', 'grad_ffn_w_down': 'grad_w', 'grad_final_norm': 'grad_w', 'delta_a_norm': 'delta_w', 'delta_ssm_w_in': 'delta_w', 'delta_ssm_conv_w': 'delta_w', 'delta_ssm_conv_b': 'delta_w', 'delta_ssm_dt_bias': 'delta_w', 'delta_ssm_a_log': 'delta_w', 'delta_ssm_d': 'delta_w', 'delta_ssm_norm': 'delta_w', 'delta_ssm_w_out': 'delta_w', 'delta_kv_norm': 'delta_w', 'delta_w_kv': 'delta_w', 'delta_b_norm': 'delta_w', 'delta_att_w_q': 'delta_w', 'delta_att_w_o': 'delta_w', 'delta_ffn_norm': 'delta_w', 'delta_ffn_w_up': 'delta_w', 'delta_ffn_conv_w': 'delta_w', 'delta_ffn_w_down': 'delta_w', 'delta_final_norm': 'delta_w', 'new_m_a_norm': 'new_m', 'new_m_ssm_w_in': 'new_m', 'new_m_ssm_conv_w': 'new_m', 'new_m_ssm_conv_b': 'new_m', 'new_m_ssm_dt_bias': 'new_m', 'new_m_ssm_a_log': 'new_m', 'new_m_ssm_d': 'new_m', 'new_m_ssm_norm': 'new_m', 'new_m_ssm_w_out': 'new_m', 'new_m_kv_norm': 'new_m', 'new_m_w_kv': 'new_m', 'new_m_b_norm': 'new_m', 'new_m_att_w_q': 'new_m', 'new_m_att_w_o': 'new_m', 'new_m_ffn_norm': 'new_m', 'new_m_ffn_w_up': 'new_m', 'new_m_ffn_conv_w': 'new_m', 'new_m_ffn_w_down': 'new_m', 'new_m_final_norm': 'new_m', 'new_v_a_norm': 'new_v', 'new_v_ssm_w_in': 'new_v', 'new_v_ssm_conv_w': 'new_v', 'new_v_ssm_conv_b': 'new_v', 'new_v_ssm_dt_bias': 'new_v', 'new_v_ssm_a_log': 'new_v', 'new_v_ssm_d': 'new_v', 'new_v_ssm_norm': 'new_v', 'new_v_ssm_w_out': 'new_v', 'new_v_kv_norm': 'new_v', 'new_v_w_kv': 'new_v', 'new_v_b_norm': 'new_v', 'new_v_att_w_q': 'new_v', 'new_v_att_w_o': 'new_v', 'new_v_ffn_norm': 'new_v', 'new_v_ffn_w_up': 'new_v', 'new_v_ffn_conv_w': 'new_v', 'new_v_ffn_w_down': 'new_v', 'new_v_final_norm': 'new_v'}


def _forward(args):
    return _fwd_reference(*[args[k] for k in FWD_PARAMS])


def _output_shape():
    out = _jax.eval_shape(lambda: _forward(_fwd_setup_inputs(0)))
    return out.shape, out.dtype

N_MICROBATCH = 1
ADAM_LR = 0.001
ADAM_B1 = 0.9
ADAM_B2 = 0.999
ADAM_EPS = 1e-08
ADAM_WD = 0.01
ADAM_STEP = 10
PER_EXAMPLE_BATCH_AXIS = {'x': 0, 'loss_target': 0}
SHARED_INPUTS = []
_WEIGHT_DTYPES = {'a_norm': _jnp.float32, 'ssm_w_in': _jnp.float32, 'ssm_conv_w': _jnp.float32, 'ssm_conv_b': _jnp.float32, 'ssm_dt_bias': _jnp.float32, 'ssm_a_log': _jnp.float32, 'ssm_d': _jnp.float32, 'ssm_norm': _jnp.float32, 'ssm_w_out': _jnp.float32, 'kv_norm': _jnp.float32, 'w_kv': _jnp.float32, 'b_norm': _jnp.float32, 'att_w_q': _jnp.float32, 'att_w_o': _jnp.float32, 'ffn_norm': _jnp.float32, 'ffn_w_up': _jnp.float32, 'ffn_conv_w': _jnp.float32, 'ffn_w_down': _jnp.float32, 'final_norm': _jnp.float32}
MOMENT_SCALE = {'a_norm': 1.285697e-01, 'ssm_w_in': 5.284512e-02, 'ssm_conv_w': 4.592161e-02, 'ssm_conv_b': 6.510374e-02, 'ssm_dt_bias': 1.254208e-01, 'ssm_a_log': 2.428088e-01, 'ssm_d': 3.167120e-01, 'ssm_norm': 6.692027e-02, 'ssm_w_out': 1.703739e-01, 'kv_norm': 2.062523e-02, 'w_kv': 1.653085e-02, 'b_norm': 1.281775e-02, 'att_w_q': 7.523823e-03, 'att_w_o': 3.069822e-02, 'ffn_norm': 7.191127e-02, 'ffn_w_up': 2.871723e-02, 'ffn_conv_w': 2.832497e-02, 'ffn_w_down': 9.346623e-02, 'final_norm': 3.194703e+01}


def _to_microbatches(a, axis):
    t = _jnp.moveaxis(a, axis, 0)
    t = t.reshape((N_MICROBATCH, t.shape[0] // N_MICROBATCH) + t.shape[1:])
    return _jnp.moveaxis(t, 1, axis + 1)


def setup_inputs(seed: int = 0) -> dict:
    inp = _fwd_setup_inputs(seed)
    key = _jax.random.fold_in(_jax.random.key(seed), 7919)
    shape, _ = _output_shape()
    out = dict(inp)
    out["loss_target"] = _jax.random.normal(_jax.random.fold_in(key, 0), shape, _jnp.float32)
    for i, name in enumerate(TWIN_WEIGHTS):
        w = inp[name].astype(_jnp.float32)
        if MOMENT_SCALE is None:
            s = _jnp.sqrt(_jnp.mean(_jnp.square(w)) + 1e-30)
        else:
            s = MOMENT_SCALE[name]
        km, kv = _jax.random.split(_jax.random.fold_in(key, i + 1))
        out[name] = w
        out["m_" + name] = s * _jax.random.normal(km, w.shape, _jnp.float32)
        out["v_" + name] = (s * s) * _jax.random.uniform(kv, w.shape, _jnp.float32, 0.5, 1.5)
    if N_MICROBATCH > 1:
        for name, axis in PER_EXAMPLE_BATCH_AXIS.items():
            out[name] = _to_microbatches(out[name], axis)
    return {'x': out['x'], 'a_norm': out['a_norm'], 'ssm_w_in': out['ssm_w_in'], 'ssm_conv_w': out['ssm_conv_w'], 'ssm_conv_b': out['ssm_conv_b'], 'ssm_dt_bias': out['ssm_dt_bias'], 'ssm_a_log': out['ssm_a_log'], 'ssm_d': out['ssm_d'], 'ssm_norm': out['ssm_norm'], 'ssm_w_out': out['ssm_w_out'], 'kv_norm': out['kv_norm'], 'w_kv': out['w_kv'], 'b_norm': out['b_norm'], 'att_w_q': out['att_w_q'], 'att_w_o': out['att_w_o'], 'ffn_norm': out['ffn_norm'], 'ffn_w_up': out['ffn_w_up'], 'ffn_conv_w': out['ffn_conv_w'], 'ffn_w_down': out['ffn_w_down'], 'final_norm': out['final_norm'], 'loss_target': out['loss_target'], 'm_a_norm': out['m_a_norm'], 'm_ssm_w_in': out['m_ssm_w_in'], 'm_ssm_conv_w': out['m_ssm_conv_w'], 'm_ssm_conv_b': out['m_ssm_conv_b'], 'm_ssm_dt_bias': out['m_ssm_dt_bias'], 'm_ssm_a_log': out['m_ssm_a_log'], 'm_ssm_d': out['m_ssm_d'], 'm_ssm_norm': out['m_ssm_norm'], 'm_ssm_w_out': out['m_ssm_w_out'], 'm_kv_norm': out['m_kv_norm'], 'm_w_kv': out['m_w_kv'], 'm_b_norm': out['m_b_norm'], 'm_att_w_q': out['m_att_w_q'], 'm_att_w_o': out['m_att_w_o'], 'm_ffn_norm': out['m_ffn_norm'], 'm_ffn_w_up': out['m_ffn_w_up'], 'm_ffn_conv_w': out['m_ffn_conv_w'], 'm_ffn_w_down': out['m_ffn_w_down'], 'm_final_norm': out['m_final_norm'], 'v_a_norm': out['v_a_norm'], 'v_ssm_w_in': out['v_ssm_w_in'], 'v_ssm_conv_w': out['v_ssm_conv_w'], 'v_ssm_conv_b': out['v_ssm_conv_b'], 'v_ssm_dt_bias': out['v_ssm_dt_bias'], 'v_ssm_a_log': out['v_ssm_a_log'], 'v_ssm_d': out['v_ssm_d'], 'v_ssm_norm': out['v_ssm_norm'], 'v_ssm_w_out': out['v_ssm_w_out'], 'v_kv_norm': out['v_kv_norm'], 'v_w_kv': out['v_w_kv'], 'v_b_norm': out['v_b_norm'], 'v_att_w_q': out['v_att_w_q'], 'v_att_w_o': out['v_att_w_o'], 'v_ffn_norm': out['v_ffn_norm'], 'v_ffn_w_up': out['v_ffn_w_up'], 'v_ffn_conv_w': out['v_ffn_conv_w'], 'v_ffn_w_down': out['v_ffn_w_down'], 'v_final_norm': out['v_final_norm']}


def _loss(weights, diff, rest, loss_target):
    with _jax.named_scope("forward"):
        args = {**rest, TWIN_DIFF_INPUT: diff, **{k: w.astype(_WEIGHT_DTYPES[k]) for k, w in weights.items()}}
        y = _forward(args)
    with _jax.named_scope("loss_head"):
        err = _jnp.square(y.astype(_jnp.float32) - loss_target)
        return 0.5 * _jnp.sum(_jnp.mean(err, axis=-1)) if err.ndim else 0.5 * err


def _adamw(w, g, m, v):
    m = ADAM_B1 * m + (1.0 - ADAM_B1) * g
    v = ADAM_B2 * v + (1.0 - ADAM_B2) * _jnp.square(g)
    m_hat = m / (1.0 - ADAM_B1 ** ADAM_STEP)
    v_hat = v / (1.0 - ADAM_B2 ** ADAM_STEP)
    delta = -ADAM_LR * (m_hat / (_jnp.sqrt(v_hat) + ADAM_EPS) + ADAM_WD * w)
    return delta, m, v


def reference(x, a_norm, ssm_w_in, ssm_conv_w, ssm_conv_b, ssm_dt_bias, ssm_a_log, ssm_d, ssm_norm, ssm_w_out, kv_norm, w_kv, b_norm, att_w_q, att_w_o, ffn_norm, ffn_w_up, ffn_conv_w, ffn_w_down, final_norm, loss_target, m_a_norm, m_ssm_w_in, m_ssm_conv_w, m_ssm_conv_b, m_ssm_dt_bias, m_ssm_a_log, m_ssm_d, m_ssm_norm, m_ssm_w_out, m_kv_norm, m_w_kv, m_b_norm, m_att_w_q, m_att_w_o, m_ffn_norm, m_ffn_w_up, m_ffn_conv_w, m_ffn_w_down, m_final_norm, v_a_norm, v_ssm_w_in, v_ssm_conv_w, v_ssm_conv_b, v_ssm_dt_bias, v_ssm_a_log, v_ssm_d, v_ssm_norm, v_ssm_w_out, v_kv_norm, v_w_kv, v_b_norm, v_att_w_q, v_att_w_o, v_ffn_norm, v_ffn_w_up, v_ffn_conv_w, v_ffn_w_down, v_final_norm):
    given = dict(x=x, a_norm=a_norm, ssm_w_in=ssm_w_in, ssm_conv_w=ssm_conv_w, ssm_conv_b=ssm_conv_b, ssm_dt_bias=ssm_dt_bias, ssm_a_log=ssm_a_log, ssm_d=ssm_d, ssm_norm=ssm_norm, ssm_w_out=ssm_w_out, kv_norm=kv_norm, w_kv=w_kv, b_norm=b_norm, att_w_q=att_w_q, att_w_o=att_w_o, ffn_norm=ffn_norm, ffn_w_up=ffn_w_up, ffn_conv_w=ffn_conv_w, ffn_w_down=ffn_w_down, final_norm=final_norm, loss_target=loss_target, m_a_norm=m_a_norm, m_ssm_w_in=m_ssm_w_in, m_ssm_conv_w=m_ssm_conv_w, m_ssm_conv_b=m_ssm_conv_b, m_ssm_dt_bias=m_ssm_dt_bias, m_ssm_a_log=m_ssm_a_log, m_ssm_d=m_ssm_d, m_ssm_norm=m_ssm_norm, m_ssm_w_out=m_ssm_w_out, m_kv_norm=m_kv_norm, m_w_kv=m_w_kv, m_b_norm=m_b_norm, m_att_w_q=m_att_w_q, m_att_w_o=m_att_w_o, m_ffn_norm=m_ffn_norm, m_ffn_w_up=m_ffn_w_up, m_ffn_conv_w=m_ffn_conv_w, m_ffn_w_down=m_ffn_w_down, m_final_norm=m_final_norm, v_a_norm=v_a_norm, v_ssm_w_in=v_ssm_w_in, v_ssm_conv_w=v_ssm_conv_w, v_ssm_conv_b=v_ssm_conv_b, v_ssm_dt_bias=v_ssm_dt_bias, v_ssm_a_log=v_ssm_a_log, v_ssm_d=v_ssm_d, v_ssm_norm=v_ssm_norm, v_ssm_w_out=v_ssm_w_out, v_kv_norm=v_kv_norm, v_w_kv=v_w_kv, v_b_norm=v_b_norm, v_att_w_q=v_att_w_q, v_att_w_o=v_att_w_o, v_ffn_norm=v_ffn_norm, v_ffn_w_up=v_ffn_w_up, v_ffn_conv_w=v_ffn_conv_w, v_ffn_w_down=v_ffn_w_down, v_final_norm=v_final_norm)
    weights = {n: given[n] for n in TWIN_WEIGHTS}
    shared = {n: given[n] for n in SHARED_INPUTS}
    per_example = {n: given[n] for n in ['x']}
    grad_fn = _jax.value_and_grad(_loss, argnums=(0, 1))

    def one_microbatch(ex, loss_target):
        ex = dict(ex)
        diff = ex.pop(TWIN_DIFF_INPUT)
        return grad_fn(weights, diff, {**shared, **ex}, loss_target)

    if N_MICROBATCH == 1:
        loss, (grad_w, grad_x) = one_microbatch(per_example, given["loss_target"])
    else:
        def body(carry, xs):
            loss_sum, grad_sum = carry
            l_k, (gw_k, gx_k) = one_microbatch(xs[0], xs[1])
            with _jax.named_scope("update"):
                return (loss_sum + l_k, _jax.tree.map(_jnp.add, grad_sum, gw_k)), gx_k

        init = (_jnp.zeros((), _jnp.float32), _jax.tree.map(_jnp.zeros_like, weights))
        (loss, grad_w), grad_x = _jax.lax.scan(body, init, (per_example, given["loss_target"]))
    with _jax.named_scope("update"):
        delta_w, new_m, new_v = {}, {}, {}
        for n in TWIN_WEIGHTS:
            delta_w[n], new_m[n], new_v[n] = _adamw(weights[n], grad_w[n], given["m_" + n], given["v_" + n])
    return (loss, grad_x, *[grad_w[n] for n in TWIN_WEIGHTS], *[delta_w[n] for n in TWIN_WEIGHTS],
            *[new_m[n] for n in TWIN_WEIGHTS], *[new_v[n] for n in TWIN_WEIGHTS])
```

```python
import functools
import math

import jax
import jax.numpy as jnp
from jax import lax
from jax.experimental import pallas as pl
from jax.experimental.pallas import tpu as pltpu

F32, BF16 = jnp.float32, jnp.bfloat16
AXES = ("x", "y", "c")
NDEV = 8
MESH = pl.DeviceIdType.MESH
HIGHEST = lax.Precision.HIGHEST

LANES = 128
VMEM_LIMIT_BYTES = 48 * 1024 * 1024

RMS_EPS = 1e-6
GATED_NORM_EPS = 1e-5
SSM_HEAD_DIM = 64
SSM_N_GROUPS = 8
SSM_D_STATE = 128
SSM_CONV = 4
SSM_CHUNK = 128
ATT_PATTERNS = ((128, 1), (512, 4), (2048, 16))
ATT_HEAD_DIM = 128
ATT_HEADS_PER_GROUP = 8
ATT_KV_HEADS_PER_GROUP = 2
ATT_BLOCK = 128
ROPE_DIM = ATT_HEAD_DIM // 4
ROPE_THETA = 500000.0
FFN_CONV = 3
ADAM_LR = 0.001
ADAM_B1 = 0.9
ADAM_B2 = 0.999
ADAM_EPS = 1e-08
ADAM_WD = 0.01
ADAM_STEP = 10
NEG = -1e30

WEIGHTS = ['a_norm', 'ssm_w_in', 'ssm_conv_w', 'ssm_conv_b', 'ssm_dt_bias', 'ssm_a_log', 'ssm_d', 'ssm_norm',
           'ssm_w_out', 'kv_norm', 'w_kv', 'b_norm', 'att_w_q', 'att_w_o', 'ffn_norm', 'ffn_w_up', 'ffn_conv_w',
           'ffn_w_down', 'final_norm']


def _params(sem=None):
    kw = dict(vmem_limit_bytes=VMEM_LIMIT_BYTES)
    if sem is not None:
        kw["dimension_semantics"] = sem
    return pltpu.CompilerParams(**kw)


def _pick(n, pref):
    if n <= pref:
        return n
    t = (pref // LANES) * LANES
    while t >= LANES:
        if n % t == 0:
            return t
        t -= LANES
    return n


def _dot(a, b, dims=(((1,), (0,)), ((), ())), precision=None):
    return lax.dot_general(a, b, dims, precision=precision, preferred_element_type=F32)


_NT = (((1,), (1,)), ((), ()))
_TN = (((0,), (0,)), ((), ()))


def _mm(a, b, *, ta=False, tb=False, res=None, out_dtype=F32, name, tm=512, tn=512, tk=1024):
    m = a.shape[1] if ta else a.shape[0]
    k = a.shape[0] if ta else a.shape[1]
    n = b.shape[0] if tb else b.shape[1]
    assert k == (b.shape[1] if tb else b.shape[0])
    tm, tn, tk = _pick(m, tm), _pick(n, tn), _pick(k, tk)
    nk = k // tk
    a_spec = pl.BlockSpec((tk, tm), lambda i, j, l: (l, i)) if ta else pl.BlockSpec((tm, tk), lambda i, j, l: (i, l))
    b_spec = pl.BlockSpec((tn, tk), lambda i, j, l: (j, l)) if tb else pl.BlockSpec((tk, tn), lambda i, j, l: (l, j))
    o_spec = pl.BlockSpec((tm, tn), lambda i, j, l: (i, j))
    dims = (((0 if ta else 1,), (1 if tb else 0,)), ((), ()))
    has_res = res is not None

    def body(*refs):
        if has_res:
            a_ref, b_ref, r_ref, o_ref, acc = refs
        else:
            a_ref, b_ref, o_ref, acc = refs
        l = pl.program_id(2)
        p = _dot(a_ref[...].astype(BF16), b_ref[...].astype(BF16), dims)

        @pl.when(l == 0)
        def _():
            acc[...] = p

        @pl.when(l > 0)
        def _():
            acc[...] += p

        @pl.when(l == nk - 1)
        def _():
            r = acc[...]
            if has_res:
                r = r + r_ref[...]
            o_ref[...] = r.astype(o_ref.dtype)

    ins = [a, b] + ([res] if has_res else [])
    in_specs = [a_spec, b_spec] + ([o_spec] if has_res else [])
    return pl.pallas_call(
        body, name=name, grid=(m // tm, n // tn, nk), in_specs=in_specs, out_specs=o_spec,
        out_shape=jax.ShapeDtypeStruct((m, n), out_dtype), scratch_shapes=[pltpu.VMEM((tm, tn), F32)],
        compiler_params=_params(("parallel", "parallel", "arbitrary")))(*ins)


def _rowwise(fn, rows, bcasts, outs, accs=(), *, tile, name):
    s = rows[0].shape[0]
    tile = min(tile, s)
    n_in, n_out, n_acc = len(rows) + len(bcasts), len(outs), len(accs)

    def body(*refs):
        vals = fn(*[r[...] for r in refs[:n_in]])
        o_refs = refs[n_in:n_in + n_out]
        a_refs = refs[n_in + n_out:]
        for r, v in zip(o_refs, vals[:n_out]):
            r[...] = v.astype(r.dtype)

        @pl.when(pl.program_id(0) == 0)
        def _():
            for r in a_refs:
                r[...] = jnp.zeros(r.shape, r.dtype)

        for r, v in zip(a_refs, vals[n_out:]):
            r[...] += v

    in_specs = [pl.BlockSpec((tile, r.shape[1]), lambda i: (i, 0)) for r in rows]
    in_specs += [pl.BlockSpec(b.shape, lambda i: (0, 0)) for b in bcasts]
    out_specs = [pl.BlockSpec((tile, c), lambda i: (i, 0)) for c, _ in outs]
    out_specs += [pl.BlockSpec(sh, lambda i: (0, 0)) for sh, _ in accs]
    out_shape = [jax.ShapeDtypeStruct((s, c), dt) for c, dt in outs]
    out_shape += [jax.ShapeDtypeStruct(sh, dt) for sh, dt in accs]
    return pl.pallas_call(body, name=name, grid=(s // tile,), in_specs=in_specs, out_specs=out_specs,
                          out_shape=out_shape, compiler_params=_params(("arbitrary",)))(*rows, *bcasts)


def _rms_fwd(x, w, name):
    def fn(x, w):
        r = lax.rsqrt(jnp.mean(x * x, axis=-1, keepdims=True) + RMS_EPS)
        return (x * r * w,)
    return _rowwise(fn, [x], [w], [(x.shape[1], BF16)], tile=256, name=name)[0]


def _rms_bwd(x, w, dh, dres, name):
    def fn(x, dh, dres, w):
        r = lax.rsqrt(jnp.mean(x * x, axis=-1, keepdims=True) + RMS_EPS)
        xh = x * r
        dxh = dh * w
        dx = r * (dxh - xh * jnp.mean(dxh * xh, axis=-1, keepdims=True))
        return dres + dx, jnp.sum(dh * xh, axis=0, keepdims=True)
    d = x.shape[1]
    return _rowwise(fn, [x, dh, dres], [w], [(d, F32)], [((1, d), F32)], tile=256, name=name)


def _final_loss(x, w, tgt, name):
    d = x.shape[1]

    def fn(x, t, w):
        r = lax.rsqrt(jnp.mean(x * x, axis=-1, keepdims=True) + RMS_EPS)
        xh = x * r
        err = xh * w - t
        part = jnp.sum(jnp.mean(err * err, axis=-1, keepdims=True), axis=0, keepdims=True) * 0.5
        dy = err * (1.0 / d)
        dxh = dy * w
        dx = r * (dxh - xh * jnp.mean(dxh * xh, axis=-1, keepdims=True))
        return dx, part, jnp.sum(dy * xh, axis=0, keepdims=True)
    dx, part, dw = _rowwise(fn, [x, tgt], [w], [(d, F32)], [((1, 1), F32), ((1, d), F32)], tile=256, name=name)
    return part, dx, dw


def _softplus_fwd(dtr, bias, name):
    def fn(r, b):
        v = r + b
        return (jnp.maximum(v, 0.0) + jnp.log(1.0 + jnp.exp(-jnp.abs(v))),)
    return _rowwise(fn, [dtr], [bias], [(LANES, F32)], tile=512, name=name)[0]


def _softplus_bwd(ddt, dtr, bias, n_heads, name):
    def fn(g, r, b):
        lane = lax.broadcasted_iota(jnp.int32, g.shape, 1)
        d = jnp.where(lane < n_heads, g * jax.nn.sigmoid(r + b), 0.0)
        return d, jnp.sum(d, axis=0, keepdims=True)
    return _rowwise(fn, [ddt, dtr], [bias], [(LANES, BF16)], [((1, LANES), F32)], tile=512, name=name)


def _gnorm_fwd(y, z, w, n_groups, name):
    di = y.shape[1]
    gs = di // n_groups

    def fn(y, z, w):
        y2 = y * (z * jax.nn.sigmoid(z))
        out = []
        for g in range(n_groups):
            sl = y2[:, g * gs:(g + 1) * gs]
            r = lax.rsqrt(jnp.mean(sl * sl, axis=-1, keepdims=True) + GATED_NORM_EPS)
            out.append(sl * r)
        return (jnp.concatenate(out, axis=1) * w,)
    return _rowwise(fn, [y, z], [w], [(di, BF16)], tile=256, name=name)[0]


def _gnorm_bwd(dyn, y, z, w, n_groups, name):
    di = y.shape[1]
    gs = di // n_groups

    def fn(dyn, y, z, w):
        sig = jax.nn.sigmoid(z)
        sz = z * sig
        y2 = y * sz
        d2n = dyn * w
        dy2, yhat = [], []
        for g in range(n_groups):
            sl = y2[:, g * gs:(g + 1) * gs]
            dg = d2n[:, g * gs:(g + 1) * gs]
            r = lax.rsqrt(jnp.mean(sl * sl, axis=-1, keepdims=True) + GATED_NORM_EPS)
            yh = sl * r
            dy2.append(r * (dg - yh * jnp.mean(dg * yh, axis=-1, keepdims=True)))
            yhat.append(yh)
        dy2 = jnp.concatenate(dy2, axis=1)
        yhat = jnp.concatenate(yhat, axis=1)
        dz = dy2 * y * (sig * (1.0 + z * (1.0 - sig)))
        return dy2 * sz, dz, jnp.sum(dyn * yhat, axis=0, keepdims=True)
    return _rowwise(fn, [dyn, y, z], [w], [(di, F32), (di, BF16)], [((1, di), F32)], tile=128, name=name)


def _merge_fwd(os_, lses, name):
    n = len(os_)

    def fn(*v):
        o, l = v[:n], v[n:]
        m = functools.reduce(jnp.maximum, l)
        e = [jnp.exp(li - m) for li in l]
        tot = functools.reduce(jnp.add, e)
        acc = functools.reduce(jnp.add, [ei * oi for ei, oi in zip(e, o)])
        return acc / tot, m + jnp.log(tot)
    c = os_[0].shape[1]
    return _rowwise(fn, list(os_) + list(lses), [], [(c, F32), (c, F32)], tile=256, name=name)


def _delta(do, o, name):
    c = o.shape[1]

    def fn(do, o):
        p = do * o
        out = [jnp.broadcast_to(jnp.sum(p[:, j:j + ATT_HEAD_DIM], axis=-1, keepdims=True), (p.shape[0], ATT_HEAD_DIM))
               for j in range(0, c, ATT_HEAD_DIM)]
        return (jnp.concatenate(out, axis=1),)
    return _rowwise(fn, [do, o], [], [(c, F32)], tile=256, name=name)[0]


def _sum_slabs(recv, name):
    def body(r_ref, o_ref):
        acc = r_ref[0]
        for k in range(1, NDEV):
            acc = acc + r_ref[k]
        o_ref[...] = acc
    return pl.pallas_call(body, name=name, out_shape=jax.ShapeDtypeStruct(recv.shape[1:], F32),
                          compiler_params=_params())(recv)


def _shift_down(x, k):
    if k == 0:
        return x
    row = lax.broadcasted_iota(jnp.int32, x.shape, 0)
    return jnp.where(row >= k, pltpu.roll(x, k, 0), 0.0)


def _shift_up(x, k):
    if k == 0:
        return x
    s = x.shape[0]
    row = lax.broadcasted_iota(jnp.int32, x.shape, 0)
    return jnp.where(row < s - k, pltpu.roll(x, s - k, 0), 0.0)


def _conv(x, w):
    kw = w.shape[0]
    return functools.reduce(jnp.add, [w[k:k + 1, :] * _shift_down(x, kw - 1 - k) for k in range(kw)])


def _conv_t(dy, w):
    kw = w.shape[0]
    return functools.reduce(jnp.add, [w[k:k + 1, :] * _shift_up(dy, kw - 1 - k) for k in range(kw)])


def _conv_dw(x, dy, dw_ref):
    kw = dw_ref.shape[0]
    for k in range(kw):
        dw_ref[k:k + 1, :] = jnp.sum(dy * _shift_down(x, kw - 1 - k), axis=0, keepdims=True)


def _dsilu(pre):
    sig = jax.nn.sigmoid(pre)
    return sig * (1.0 + pre * (1.0 - sig))


def _col_specs(s, c, kw, tc):
    return (pl.BlockSpec((s, tc), lambda j: (0, j)), pl.BlockSpec((kw, tc), lambda j: (0, j)),
            pl.BlockSpec((1, tc), lambda j: (0, j)))


def _conv_silu_fwd(x, w, b, name):
    s, c = x.shape
    tc = LANES
    xs, ws, bs = _col_specs(s, c, w.shape[0], tc)

    def body(x_ref, w_ref, b_ref, o_ref):
        pre = _conv(x_ref[...], w_ref[...]) + b_ref[...]
        o_ref[...] = pre * jax.nn.sigmoid(pre)
    return pl.pallas_call(body, name=name, grid=(c // tc,), in_specs=[xs, ws, bs], out_specs=xs,
                          out_shape=jax.ShapeDtypeStruct((s, c), F32), compiler_params=_params(("parallel",)))(x, w, b)


def _conv_silu_bwd(x, w, b, dy, name):
    s, c = x.shape
    tc = LANES
    xs, ws, bs = _col_specs(s, c, w.shape[0], tc)

    def body(x_ref, w_ref, b_ref, dy_ref, dx_ref, dw_ref, db_ref):
        xv, wv = x_ref[...], w_ref[...]
        pre = _conv(xv, wv) + b_ref[...]
        dpre = dy_ref[...] * _dsilu(pre)
        dx_ref[...] = _conv_t(dpre, wv).astype(dx_ref.dtype)
        _conv_dw(xv, dpre, dw_ref)
        db_ref[...] = jnp.sum(dpre, axis=0, keepdims=True)
    return pl.pallas_call(
        body, name=name, grid=(c // tc,), in_specs=[xs, ws, bs, xs], out_specs=[xs, ws, bs],
        out_shape=[jax.ShapeDtypeStruct((s, c), BF16), jax.ShapeDtypeStruct(w.shape, F32),
                   jax.ShapeDtypeStruct((1, c), F32)],
        compiler_params=_params(("parallel",)))(x, w, b, dy)


def _ffn_gate_fwd(ug, uv, wg, wv, name):
    s, c = ug.shape
    tc = LANES
    xs, ws, _ = _col_specs(s, c, wg.shape[0], tc)

    def body(g_ref, v_ref, wg_ref, wv_ref, o_ref):
        g = _conv(g_ref[...], wg_ref[...])
        v = _conv(v_ref[...], wv_ref[...])
        o_ref[...] = (g * jax.nn.sigmoid(g) * v).astype(o_ref.dtype)
    return pl.pallas_call(body, name=name, grid=(c // tc,), in_specs=[xs, xs, ws, ws], out_specs=xs,
                          out_shape=jax.ShapeDtypeStruct((s, c), BF16),
                          compiler_params=_params(("parallel",)))(ug, uv, wg, wv)


def _ffn_gate_bwd(ug, uv, wg, wv, df, name):
    s, c = ug.shape
    tc = LANES
    xs, ws, _ = _col_specs(s, c, wg.shape[0], tc)

    def body(g_ref, v_ref, wg_ref, wv_ref, df_ref, dg_ref, dv_ref, dwg_ref, dwv_ref):
        gp, vp, wgv, wvv = g_ref[...], v_ref[...], wg_ref[...], wv_ref[...]
        g = _conv(gp, wgv)
        v = _conv(vp, wvv)
        dfv = df_ref[...]
        dg = dfv * v * _dsilu(g)
        dv = dfv * (g * jax.nn.sigmoid(g))
        dg_ref[...] = _conv_t(dg, wgv).astype(dg_ref.dtype)
        dv_ref[...] = _conv_t(dv, wvv).astype(dv_ref.dtype)
        _conv_dw(gp, dg, dwg_ref)
        _conv_dw(vp, dv, dwv_ref)
    return pl.pallas_call(
        body, name=name, grid=(c // tc,), in_specs=[xs, xs, ws, ws, xs], out_specs=[xs, xs, ws, ws],
        out_shape=[jax.ShapeDtypeStruct((s, c), BF16), jax.ShapeDtypeStruct((s, c), BF16),
                   jax.ShapeDtypeStruct(wg.shape, F32), jax.ShapeDtypeStruct(wv.shape, F32)],
        compiler_params=_params(("parallel",)))(ug, uv, wg, wv, df)


def _ssd_common(dt, alog, n_heads):
    ln = dt.shape[0]
    lane = lax.broadcasted_iota(jnp.int32, (1, LANES), 1)
    a = jnp.where(lane < n_heads, -jnp.exp(alog), 0.0)
    row = lax.broadcasted_iota(jnp.int32, (ln, ln), 0)
    col = lax.broadcasted_iota(jnp.int32, (ln, ln), 1)
    tril = col <= row
    acs = _dot(tril.astype(F32), dt * a, precision=HIGHEST)
    return a, acs, acs.T, tril


def _ssd_fwd(xbc, dt, alog, dskip, di, n_heads, n_groups, name):
    s, convd = xbc.shape
    ln, p, ns = SSM_CHUNK, SSM_HEAD_DIM, SSM_D_STATE
    nc, hg = s // ln, n_heads // n_groups

    def body(x_ref, dt_ref, alog_ref, d_ref, y_ref, prev_ref, st):
        @pl.when(pl.program_id(0) == 0)
        def _():
            st[...] = jnp.zeros(st.shape, F32)

        dt = dt_ref[...]
        _, acs, acs_t, tril = _ssd_common(dt, alog_ref[...], n_heads)
        e_all = jnp.exp(acs)
        last = acs[ln - 1:ln, :]
        ds_all = jnp.exp(last - acs)
        t_all = jnp.exp(last)
        dsk = d_ref[...]
        for g in range(n_groups):
            bg = x_ref[:, di + g * ns:di + (g + 1) * ns].astype(BF16)
            cg = x_ref[:, di + (n_groups + g) * ns:di + (n_groups + g + 1) * ns].astype(BF16)
            gm = _dot(cg, bg, _NT)
            for j in range(hg):
                h = g * hg + j
                xh = x_ref[:, h * p:(h + 1) * p]
                xdt = xh * dt[:, h:h + 1]
                seg = acs[:, h:h + 1] - acs_t[h:h + 1, :]
                m = jnp.where(tril, gm * jnp.exp(jnp.where(tril, seg, 0.0)), 0.0)
                prev = st[h]
                prev_ref[0, h] = prev
                y = _dot(m.astype(BF16), xdt.astype(BF16))
                y = y + _dot(cg, prev.astype(BF16), _NT) * e_all[:, h:h + 1]
                y = y + xh * dsk[:, h:h + 1]
                snew = _dot((xdt * ds_all[:, h:h + 1]).astype(BF16), bg, _TN)
                st[h] = prev * t_all[:, h:h + 1] + snew
                y_ref[:, h * p:(h + 1) * p] = y

    vec = pl.BlockSpec((1, LANES), lambda c: (0, 0))
    return pl.pallas_call(
        body, name=name, grid=(nc,),
        in_specs=[pl.BlockSpec((ln, convd), lambda c: (c, 0)), pl.BlockSpec((ln, LANES), lambda c: (c, 0)), vec, vec],
        out_specs=[pl.BlockSpec((ln, di), lambda c: (c, 0)),
                   pl.BlockSpec((1, n_heads, p, ns), lambda c: (c, 0, 0, 0))],
        out_shape=[jax.ShapeDtypeStruct((s, di), F32), jax.ShapeDtypeStruct((nc, n_heads, p, ns), F32)],
        scratch_shapes=[pltpu.VMEM((n_heads, p, ns), F32)],
        compiler_params=_params(("arbitrary",)))(xbc, dt, alog, dskip)


def _ssd_bwd(xbc, dt, alog, dskip, prev_all, dy, di, n_heads, n_groups, name):
    s, convd = xbc.shape
    ln, p, ns = SSM_CHUNK, SSM_HEAD_DIM, SSM_D_STATE
    nc, hg = s // ln, n_heads // n_groups

    def body(x_ref, dt_ref, alog_ref, d_ref, prev_ref, dy_ref, dx_ref, ddt_ref, da_ref, dd_ref, dh):
        step = pl.program_id(0)

        @pl.when(step == 0)
        def _():
            dh[...] = jnp.zeros(dh.shape, F32)
            da_ref[...] = jnp.zeros(da_ref.shape, F32)
            dd_ref[...] = jnp.zeros(dd_ref.shape, F32)

        dt = dt_ref[...]
        a, acs, acs_t, tril = _ssd_common(dt, alog_ref[...], n_heads)
        e_all = jnp.exp(acs)
        last = acs[ln - 1:ln, :]
        ds_all = jnp.exp(last - acs)
        t_all = jnp.exp(last)
        dsk = d_ref[...]
        lane = lax.broadcasted_iota(jnp.int32, (ln, LANES), 1)
        lane1 = lax.broadcasted_iota(jnp.int32, (1, LANES), 1)
        sub = lax.broadcasted_iota(jnp.int32, (LANES, ln), 0)
        rowi = lax.broadcasted_iota(jnp.int32, (ln, LANES), 0)
        dacs_c = jnp.zeros((ln, LANES), F32)
        dacs_r = jnp.zeros((LANES, ln), F32)
        dlast = jnp.zeros((1, LANES), F32)
        ddt_x = jnp.zeros((ln, LANES), F32)
        dd = jnp.zeros((1, LANES), F32)

        def tot(v):
            return jnp.sum(jnp.sum(v, axis=1, keepdims=True), axis=0, keepdims=True)

        for g in range(n_groups):
            bg = x_ref[:, di + g * ns:di + (g + 1) * ns].astype(BF16)
            cg = x_ref[:, di + (n_groups + g) * ns:di + (n_groups + g + 1) * ns].astype(BF16)
            gm = _dot(cg, bg, _NT)
            dgm = jnp.zeros((ln, ln), F32)
            dcg = jnp.zeros((ln, ns), F32)
            dbg = jnp.zeros((ln, ns), F32)
            for j in range(hg):
                h = g * hg + j
                xh = x_ref[:, h * p:(h + 1) * p]
                dth = dt[:, h:h + 1]
                xdt = xh * dth
                dyh = dy_ref[:, h * p:(h + 1) * p]
                eh, dsh, th = e_all[:, h:h + 1], ds_all[:, h:h + 1], t_all[:, h:h + 1]
                seg = acs[:, h:h + 1] - acs_t[h:h + 1, :]
                dec = jnp.where(tril, jnp.exp(jnp.where(tril, seg, 0.0)), 0.0)
                m = gm * dec
                prev = prev_ref[0, h]
                dhn = dh[h]
                prevb, dhb, dyb, xdtb = prev.astype(BF16), dhn.astype(BF16), dyh.astype(BF16), xdt.astype(BF16)
                yo = _dot(cg, prevb, _NT)
                dyob = (dyh * eh).astype(BF16)
                c_col = jnp.sum(dyh * yo, axis=1, keepdims=True) * eh
                dcg = dcg + _dot(dyob, prevb)
                dprev = th * dhn + _dot(dyob, cg, _TN)
                dtt = tot(dhn * prev)
                w = _dot(bg, dhb, _NT)
                dxdt = w * dsh
                dds = jnp.sum(w * xdt, axis=1, keepdims=True)
                dbg = dbg + _dot((xdt * dsh).astype(BF16), dhb)
                dm = _dot(dyb, xdtb, _NT)
                dxdt = dxdt + _dot(m.astype(BF16), dyb, _TN)
                dgm = dgm + dm * dec
                q = dm * m
                c_col = c_col + jnp.sum(q, axis=1, keepdims=True) - dds * dsh
                r_row = -jnp.sum(q, axis=0, keepdims=True)
                dlast_h = tot(dds * dsh) + dtt * th
                dacs_c = dacs_c + jnp.where(lane == h, c_col, 0.0)
                dacs_r = dacs_r + jnp.where(sub == h, r_row, 0.0)
                dlast = dlast + jnp.where(lane1 == h, dlast_h, 0.0)
                ddt_x = ddt_x + jnp.where(lane == h, jnp.sum(dxdt * xh, axis=1, keepdims=True), 0.0)
                dd = dd + jnp.where(lane1 == h, tot(dyh * xh), 0.0)
                dx_ref[:, h * p:(h + 1) * p] = dxdt * dth + dyh * dsk[:, h:h + 1]
                dh[h] = dprev
            dgb = dgm.astype(BF16)
            dx_ref[:, di + g * ns:di + (g + 1) * ns] = dbg + _dot(dgb, cg, _TN)
            dx_ref[:, di + (n_groups + g) * ns:di + (n_groups + g + 1) * ns] = dcg + _dot(dgb, bg)

        dacs = dacs_c + dacs_r.T + jnp.where(rowi == ln - 1, dlast, 0.0)
        row = lax.broadcasted_iota(jnp.int32, (ln, ln), 0)
        col = lax.broadcasted_iota(jnp.int32, (ln, ln), 1)
        dadt = _dot((col >= row).astype(F32), dacs, precision=HIGHEST)
        ddt_ref[...] = dadt * a + ddt_x
        da_ref[...] += jnp.sum(dadt * dt, axis=0, keepdims=True)
        dd_ref[...] += dd

        @pl.when(step == nc - 1)
        def _():
            da_ref[...] = da_ref[...] * a

    vec = pl.BlockSpec((1, LANES), lambda c: (0, 0))
    rev = lambda c: (nc - 1 - c, 0)
    return pl.pallas_call(
        body, name=name, grid=(nc,),
        in_specs=[pl.BlockSpec((ln, convd), rev), pl.BlockSpec((ln, LANES), rev), vec, vec,
                  pl.BlockSpec((1, n_heads, p, ns), lambda c: (nc - 1 - c, 0, 0, 0)), pl.BlockSpec((ln, di), rev)],
        out_specs=[pl.BlockSpec((ln, convd), rev), pl.BlockSpec((ln, LANES), rev), vec, vec],
        out_shape=[jax.ShapeDtypeStruct((s, convd), F32), jax.ShapeDtypeStruct((s, LANES), F32),
                   jax.ShapeDtypeStruct((1, LANES), F32), jax.ShapeDtypeStruct((1, LANES), F32)],
        scratch_shapes=[pltpu.VMEM((n_heads, p, ns), F32)],
        compiler_params=_params(("arbitrary",)))(xbc, dt, alog, dskip, prev_all, dy)


def _perm(a, d):
    if d == 1:
        return a
    s = a.shape[0]
    return a.reshape(s // d, d, -1).transpose(1, 0, 2).reshape(s, -1)


def _unperm(a, d):
    if d == 1:
        return a
    s = a.shape[0]
    return a.reshape(d, s // d, -1).transpose(1, 0, 2).reshape(s, -1)


def _rot_tables(s, d):
    half = ROPE_DIM // 2
    inv_freq = jnp.power(jnp.float32(ROPE_THETA), -jnp.arange(0, ROPE_DIM, 2, dtype=F32) / ROPE_DIM)
    v = jnp.arange(s, dtype=jnp.int32)
    pos = (v % (s // d)) * d + v // (s // d)
    ang = pos.astype(F32)[:, None] * inv_freq[None, :]
    cos, sin = jnp.cos(ang), jnp.sin(ang)
    zero = jnp.zeros((s, ATT_HEAD_DIM - ROPE_DIM), F32)
    cf = jnp.concatenate([cos, cos, jnp.ones_like(zero)], axis=1)
    s1 = jnp.concatenate([-sin, jnp.zeros_like(sin), zero], axis=1)
    s2 = jnp.concatenate([jnp.zeros_like(sin), sin, zero], axis=1)
    assert half * 2 == ROPE_DIM
    return cf, s1, s2


def _rot(x, tabs, sign):
    cf, s1, s2 = tabs
    half = ROPE_DIM // 2
    left = pltpu.roll(x, ATT_HEAD_DIM - half, 1)
    right = pltpu.roll(x, half, 1)
    return x * cf + sign * (left * s1 + right * s2)


def _att_masks(n, n_blk, rep):
    b = ATT_BLOCK
    row = lax.broadcasted_iota(jnp.int32, (rep * b, b), 0) & (b - 1)
    col = lax.broadcasted_iota(jnp.int32, (rep * b, b), 1)
    off = jnp.where(n % n_blk != 0, 0, 2 * b)
    return col <= row, col >= row + off


def _stack(x, rep):
    return jnp.concatenate([x[:, j * ATT_HEAD_DIM:(j + 1) * ATT_HEAD_DIM] for j in range(rep)], axis=0)


def _att_specs(nb, rep, cur, prv):
    b, hd = ATT_BLOCK, ATT_HEAD_DIM
    q_spec = pl.BlockSpec((b, rep * hd), lambda h, n: (cur(n), h))
    kc_spec = pl.BlockSpec((b, hd), lambda h, n: (cur(n), h))
    kp_spec = pl.BlockSpec((b, hd), lambda h, n: (prv(n), h))
    tc_spec = pl.BlockSpec((b, hd), lambda h, n: (cur(n), 0))
    tp_spec = pl.BlockSpec((b, hd), lambda h, n: (prv(n), 0))
    return q_spec, kc_spec, kp_spec, tc_spec, tp_spec


def _attn_fwd(q, k, v, tabs, n_blk, name):
    s = q.shape[0]
    b, hd = ATT_BLOCK, ATT_HEAD_DIM
    nb = s // b
    n_kv = ATT_KV_HEADS_PER_GROUP
    rep = ATT_HEADS_PER_GROUP // n_kv
    scale = hd ** -0.5

    def body(q_ref, kc_ref, kp_ref, vc_ref, vp_ref, cfc, s1c, s2c, cfp, s1p, s2p, o_ref, lse_ref):
        n = pl.program_id(1)
        tc = (cfc[...], s1c[...], s2c[...])
        tp = (cfp[...], s1p[...], s2p[...])
        qv = q_ref[...]
        q4 = jnp.concatenate([_rot(qv[:, j * hd:(j + 1) * hd], tc, 1.0) for j in range(rep)], axis=0).astype(BF16)
        kc = _rot(kc_ref[...], tc, 1.0).astype(BF16)
        kp = _rot(kp_ref[...], tp, 1.0).astype(BF16)
        mc, mp = _att_masks(n, n_blk, rep)
        sc = jnp.where(mc, _dot(q4, kc, _NT) * scale, NEG)
        sp = jnp.where(mp, _dot(q4, kp, _NT) * scale, NEG)
        m = jnp.maximum(jnp.max(sc, axis=1, keepdims=True), jnp.max(sp, axis=1, keepdims=True))
        pc, pp = jnp.exp(sc - m), jnp.exp(sp - m)
        l = jnp.sum(pc, axis=1, keepdims=True) + jnp.sum(pp, axis=1, keepdims=True)
        o = (_dot(pc.astype(BF16), vc_ref[...].astype(BF16)) + _dot(pp.astype(BF16), vp_ref[...].astype(BF16))) / l
        lse = jnp.broadcast_to(m + jnp.log(l), (rep * b, hd))
        for j in range(rep):
            o_ref[:, j * hd:(j + 1) * hd] = o[j * b:(j + 1) * b]
            lse_ref[:, j * hd:(j + 1) * hd] = lse[j * b:(j + 1) * b]

    cur = lambda n: n
    prv = lambda n: jnp.maximum(n - 1, 0)
    q_spec, kc_spec, kp_spec, tc_spec, tp_spec = _att_specs(nb, rep, cur, prv)
    return pl.pallas_call(
        body, name=name, grid=(n_kv, nb),
        in_specs=[q_spec, kc_spec, kp_spec, kc_spec, kp_spec, tc_spec, tc_spec, tc_spec, tp_spec, tp_spec, tp_spec],
        out_specs=[q_spec, q_spec],
        out_shape=[jax.ShapeDtypeStruct(q.shape, F32), jax.ShapeDtypeStruct(q.shape, F32)],
        compiler_params=_params(("parallel", "arbitrary")))(q, k, k, v, v, *tabs, *tabs)


def _attn_bwd(q, k, v, do, lse, delta, tabs, n_blk, name):
    s = q.shape[0]
    b, hd = ATT_BLOCK, ATT_HEAD_DIM
    nb = s // b
    n_kv = ATT_KV_HEADS_PER_GROUP
    rep = ATT_HEADS_PER_GROUP // n_kv
    scale = hd ** -0.5

    def body(q_ref, do_ref, lse_ref, dl_ref, kc_ref, kp_ref, vc_ref, vp_ref, cfc, s1c, s2c, cfp, s1p, s2p,
             dq_ref, dk_ref, dv_ref, ck, cv):
        n = pl.program_id(1)
        tp = (cfp[...], s1p[...], s2p[...])

        @pl.when(n == 0)
        def _():
            ck[...] = jnp.zeros(ck.shape, F32)
            cv[...] = jnp.zeros(cv.shape, F32)

        @pl.when(n < nb)
        def _():
            tc = (cfc[...], s1c[...], s2c[...])
            qv = q_ref[...]
            q4 = jnp.concatenate([_rot(qv[:, j * hd:(j + 1) * hd], tc, 1.0) for j in range(rep)],
                                 axis=0).astype(BF16)
            do4 = _stack(do_ref[...], rep).astype(BF16)
            lse4 = _stack(lse_ref[...], rep)
            dl4 = _stack(dl_ref[...], rep)
            kc = _rot(kc_ref[...], tc, 1.0).astype(BF16)
            kp = _rot(kp_ref[...], tp, 1.0).astype(BF16)
            vc, vp = vc_ref[...].astype(BF16), vp_ref[...].astype(BF16)
            mc, mp = _att_masks(n, n_blk, rep)
            pc = jnp.where(mc, jnp.exp(jnp.where(mc, _dot(q4, kc, _NT) * scale - lse4, 0.0)), 0.0)
            pp = jnp.where(mp, jnp.exp(jnp.where(mp, _dot(q4, kp, _NT) * scale - lse4, 0.0)), 0.0)
            dsc = (pc * (_dot(do4, vc, _NT) - dl4)).astype(BF16)
            dsp = (pp * (_dot(do4, vp, _NT) - dl4)).astype(BF16)
            dq4 = (_dot(dsc, kc) + _dot(dsp, kp)) * scale
            for j in range(rep):
                dq_ref[:, j * hd:(j + 1) * hd] = _rot(dq4[j * b:(j + 1) * b], tc, -1.0).astype(dq_ref.dtype)
            dk_prev = ck[...] + _dot(dsp, q4, _TN) * scale
            dv_prev = cv[...] + _dot(pp.astype(BF16), do4, _TN)
            dk_ref[...] = _rot(dk_prev, tp, -1.0).astype(dk_ref.dtype)
            dv_ref[...] = dv_prev.astype(dv_ref.dtype)
            ck[...] = _dot(dsc, q4, _TN) * scale
            cv[...] = _dot(pc.astype(BF16), do4, _TN)

        @pl.when(n == nb)
        def _():
            dk_ref[...] = _rot(ck[...], tp, -1.0).astype(dk_ref.dtype)
            dv_ref[...] = cv[...].astype(dv_ref.dtype)

    cur = lambda n: jnp.minimum(n, nb - 1)
    prv = lambda n: jnp.maximum(n - 1, 0)
    q_spec, kc_spec, kp_spec, tc_spec, tp_spec = _att_specs(nb, rep, cur, prv)
    return pl.pallas_call(
        body, name=name, grid=(n_kv, nb + 1),
        in_specs=[q_spec, q_spec, q_spec, q_spec, kc_spec, kp_spec, kc_spec, kp_spec,
                  tc_spec, tc_spec, tc_spec, tp_spec, tp_spec, tp_spec],
        out_specs=[q_spec, kp_spec, kp_spec],
        out_shape=[jax.ShapeDtypeStruct(q.shape, BF16), jax.ShapeDtypeStruct(k.shape, BF16),
                   jax.ShapeDtypeStruct(k.shape, BF16)],
        scratch_shapes=[pltpu.VMEM((b, hd), F32), pltpu.VMEM((b, hd), F32)],
        compiler_params=_params(("parallel", "arbitrary")))(q, do, lse, delta, k, k, v, v, *tabs, *tabs)


def _adamw(g_slabs, w, m, v, name):
    kk, r, c = g_slabs.shape
    tile = r if r <= 256 else _pick_rows(r, 256)

    def body(g_ref, w_ref, m_ref, v_ref, go_ref, d_ref, mo_ref, vo_ref):
        g = g_ref[0].astype(F32)
        for k in range(1, kk):
            g = g + g_ref[k].astype(F32)
        m2 = ADAM_B1 * m_ref[...] + (1.0 - ADAM_B1) * g
        v2 = ADAM_B2 * v_ref[...] + (1.0 - ADAM_B2) * jnp.square(g)
        m_hat = m2 / (1.0 - ADAM_B1 ** ADAM_STEP)
        v_hat = v2 / (1.0 - ADAM_B2 ** ADAM_STEP)
        go_ref[...] = g
        d_ref[...] = -ADAM_LR * (m_hat / (jnp.sqrt(v_hat) + ADAM_EPS) + ADAM_WD * w_ref[...])
        mo_ref[...] = m2
        vo_ref[...] = v2

    spec = pl.BlockSpec((tile, c), lambda i: (i, 0))
    return pl.pallas_call(
        body, name=name, grid=(r // tile,), in_specs=[pl.BlockSpec((kk, tile, c), lambda i: (0, i, 0)), spec, spec, spec],
        out_specs=[spec] * 4, out_shape=[jax.ShapeDtypeStruct((r, c), F32)] * 4,
        compiler_params=_params(("parallel",)))(g_slabs, w, m, v)


def _pick_rows(r, pref):
    t = (pref // 16) * 16
    while t >= 16:
        if r % t == 0:
            return t
        t -= 16
    return r


def _coords():
    return lax.axis_index("x"), lax.axis_index("y"), lax.axis_index("c")


def _dev_index(px, py, pc):
    return 4 * px + 2 * py + pc


def _all_gather(shards, name):
    na = len(shards)

    def body(*refs):
        ins, outs = refs[:na], refs[na:2 * na]
        send_sems, recv_sems, local_sems = refs[2 * na:]
        x, y, c = _coords()
        me, sibling = (x, y, c), (x, y, 1 - c)
        chips = [(1 - x, y), (x, 1 - y), (1 - x, 1 - y)]

        def copy(a, k, block, to, src=None):
            dst = outs[a].at[_dev_index(*block)]
            return pltpu.make_async_remote_copy(
                src_ref=dst if src is None else src, dst_ref=dst, send_sem=send_sems.at[a * 7 + k],
                recv_sem=recv_sems.at[a * 7 + k], device_id=to, device_id_type=MESH)

        mine = [pltpu.make_async_copy(ins[a], outs[a].at[_dev_index(*me)], local_sems.at[a]) for a in range(na)]
        for cp in mine:
            cp.start()
        first = []
        for a in range(na):
            first.append(copy(a, 0, me, sibling, src=ins[a]))
            first += [copy(a, 1 + j, me, (*chip, c), src=ins[a]) for j, chip in enumerate(chips)]
        for cp in first:
            cp.start()
        passed = []
        for j, chip in enumerate(chips):
            for a in range(na):
                copy(a, 1 + j, (*chip, c), me).wait_recv()
                cp = copy(a, 4 + j, (*chip, c), sibling)
                cp.start()
                passed.append(cp)
        for a in range(na):
            copy(a, 0, sibling, me).wait_recv()
            for j, chip in enumerate(chips):
                copy(a, 4 + j, (*chip, 1 - c), me).wait_recv()
        for cp in first + passed:
            cp.wait_send()
        for cp in mine:
            cp.wait()

    hbm = pl.BlockSpec(memory_space=pl.ANY)
    return pl.pallas_call(
        body, name=name, in_specs=[hbm] * na, out_specs=[hbm] * na,
        out_shape=[jax.ShapeDtypeStruct((NDEV,) + s.shape, s.dtype) for s in shards],
        scratch_shapes=[pltpu.SemaphoreType.DMA((7 * na,)), pltpu.SemaphoreType.DMA((7 * na,)),
                        pltpu.SemaphoreType.DMA((na,))])(*shards)


def _exchange(slabs, whole, name):
    ns, nw = len(slabs), len(whole)
    na = ns + nw

    def body(*refs):
        ins, outs = refs[:na], refs[na:2 * na]
        send_sems, recv_sems, local_sems = refs[2 * na:]
        x, y, c = _coords()
        me = _dev_index(x, y, c)

        def src_of(a, p):
            return ins[a].at[p] if a < ns else ins[a]

        def copy(a, k, peer):
            p = _dev_index(*peer)
            return pltpu.make_async_remote_copy(
                src_ref=src_of(a, p), dst_ref=outs[a].at[me], send_sem=send_sems.at[a * 7 + k - 1],
                recv_sem=recv_sems.at[a * 7 + k - 1], device_id=peer, device_id_type=MESH)

        def arrival(a, k, peer):
            p = _dev_index(*peer)
            return pltpu.make_async_remote_copy(
                src_ref=src_of(a, p), dst_ref=outs[a].at[p], send_sem=send_sems.at[a * 7 + k - 1],
                recv_sem=recv_sems.at[a * 7 + k - 1], device_id=peer, device_id_type=MESH)

        mine = [pltpu.make_async_copy(src_of(a, me), outs[a].at[me], local_sems.at[a]) for a in range(na)]
        for cp in mine:
            cp.start()
        peers = [(k, (x ^ (k >> 2), y ^ ((k >> 1) & 1), c ^ (k & 1))) for k in range(1, NDEV)]
        sent = [copy(a, k, peer) for k, peer in peers for a in range(na)]
        for cp in sent:
            cp.start()
        for k, peer in peers:
            for a in range(na):
                arrival(a, k, peer).wait_recv()
        for cp in sent:
            cp.wait_send()
        for cp in mine:
            cp.wait()

    hbm = pl.BlockSpec(memory_space=pl.ANY)
    out_shape = [jax.ShapeDtypeStruct(s.shape, s.dtype) for s in slabs]
    out_shape += [jax.ShapeDtypeStruct((NDEV,) + w.shape, w.dtype) for w in whole]
    return pl.pallas_call(
        body, name=name, in_specs=[hbm] * na, out_specs=[hbm] * na, out_shape=out_shape,
        scratch_shapes=[pltpu.SemaphoreType.DMA((7 * na,)), pltpu.SemaphoreType.DMA((7 * na,)),
                        pltpu.SemaphoreType.DMA((na,))])(*slabs, *whole)


def _pack(vecs):
    parts, spans, off = [], [], 0
    for v in vecs:
        n = v.size
        pad = (-n) % LANES
        parts.append(jnp.pad(v.reshape(-1).astype(F32), (0, pad)))
        spans.append((off, n))
        off += n + pad
    return jnp.concatenate(parts).reshape(-1, LANES), spans


def _pad_lanes(v):
    v = v.reshape(1, -1)
    return jnp.pad(v, ((0, 0), (0, LANES - v.shape[1])))


def _cols_to_slabs(g):
    sh = g.shape
    g = g.reshape(sh[:-1] + (NDEV, sh[-1] // NDEV))
    return jnp.moveaxis(g, -2, 0)


def _rows_to_slabs(g):
    sh = g.shape
    g = g.reshape(sh[:-2] + (NDEV, sh[-2] // NDEV, sh[-1]))
    return jnp.moveaxis(g, -3, 0)


def _slabs_to_cols(a):
    a = jnp.moveaxis(a, 0, -2)
    return a.reshape(a.shape[:-2] + (a.shape[-2] * a.shape[-1],))


def _slabs_to_rows(a):
    a = jnp.moveaxis(a, 0, -3)
    return a.reshape(a.shape[:-3] + (a.shape[-3] * a.shape[-2], a.shape[-1]))


def _ffn_forward(x, norm_w, wup_g, wup_v, cw_g, cw_v, wdown, tag):
    h = _rms_fwd(x, norm_w, f"{tag}_norm")
    ug = _mm(h, wup_g, name=f"{tag}_up_gate")
    uv = _mm(h, wup_v, name=f"{tag}_up_val")
    f = _ffn_gate_fwd(ug, uv, cw_g, cw_v, f"{tag}_gate")
    return _mm(f, wdown, res=x, name=f"{tag}_down"), (h, ug, uv, f)


def _ffn_backward(x, saved, dout, norm_w, wup_g, wup_v, cw_g, cw_v, wdown, tag):
    h, ug, uv, f = saved
    dwdown = _mm(f, dout, ta=True, name=f"{tag}_dwdown")
    df = _mm(dout, wdown, tb=True, name=f"{tag}_df")
    dug, duv, dcg, dcv = _ffn_gate_bwd(ug, uv, cw_g, cw_v, df, f"{tag}_gate_bwd")
    dwg = _mm(h, dug, ta=True, name=f"{tag}_dwup_gate")
    dwv = _mm(h, duv, ta=True, name=f"{tag}_dwup_val")
    dh = _mm(dug, wup_g, tb=True, name=f"{tag}_dh_gate")
    dh = _mm(duv, wup_v, tb=True, res=dh, name=f"{tag}_dh_val")
    dx, dnorm = _rms_bwd(x, norm_w, dh, dout, f"{tag}_norm_bwd")
    return dx, (jnp.concatenate([dwg, dwv], axis=1), jnp.concatenate([dcg, dcv], axis=1), dwdown, dnorm)


def kernel(x, a_norm, ssm_w_in, ssm_conv_w, ssm_conv_b, ssm_dt_bias, ssm_a_log, ssm_d, ssm_norm, ssm_w_out, kv_norm, w_kv, b_norm, att_w_q, att_w_o, ffn_norm, ffn_w_up, ffn_conv_w, ffn_w_down, final_norm, loss_target, m_a_norm, m_ssm_w_in, m_ssm_conv_w, m_ssm_conv_b, m_ssm_dt_bias, m_ssm_a_log, m_ssm_d, m_ssm_norm, m_ssm_w_out, m_kv_norm, m_w_kv, m_b_norm, m_att_w_q, m_att_w_o, m_ffn_norm, m_ffn_w_up, m_ffn_conv_w, m_ffn_w_down, m_final_norm, v_a_norm, v_ssm_w_in, v_ssm_conv_w, v_ssm_conv_b, v_ssm_dt_bias, v_ssm_a_log, v_ssm_d, v_ssm_norm, v_ssm_w_out, v_kv_norm, v_w_kv, v_b_norm, v_att_w_q, v_att_w_o, v_ffn_norm, v_ffn_w_up, v_ffn_conv_w, v_ffn_w_down, v_final_norm):
    given = dict(locals())
    xs, tgt = x[0], loss_target[0]
    s, d = xs.shape
    di = ssm_w_out.shape[1] * NDEV
    nh = ssm_dt_bias.shape[1]
    ng = SSM_N_GROUPS
    convd = di + 2 * ng * SSM_D_STATE
    f = ffn_w_down.shape[1] * NDEV
    n_att = len(ATT_PATTERNS)
    qg = ATT_HEADS_PER_GROUP * ATT_HEAD_DIM
    kg = ATT_KV_HEADS_PER_GROUP * ATT_HEAD_DIM
    kvd = n_att * kg
    assert all(w // dil == ATT_BLOCK for w, dil in ATT_PATTERNS)

    small, _ = _pack([a_norm, ssm_conv_w, ssm_conv_b, ssm_norm, ffn_conv_w])
    big = [ssm_w_in[0], ssm_w_out[0], w_kv, att_w_q[0], att_w_o[0], ffn_w_up, ffn_w_down]
    gat = _all_gather([b.astype(BF16) for b in big] + [small], "gather_weights")
    w_in = _slabs_to_cols(gat[0])
    w_z, w_xbc = w_in[:, :di], w_in[:, di:di + convd]
    w_dt = jnp.pad(w_in[:, di + convd:], ((0, 0), (0, LANES - nh)))
    w_out = _slabs_to_rows(gat[1])
    w_kvf = _slabs_to_cols(gat[2])
    w_q = _slabs_to_cols(gat[3])
    w_o = _slabs_to_rows(gat[4])
    w_up = _slabs_to_cols(gat[5])
    w_up_g, w_up_v = w_up[:, :, :f], w_up[:, :, f:]
    w_down = _slabs_to_rows(gat[6])
    sm = gat[7].reshape(NDEV, -1)
    o0 = 0

    def take(shape):
        nonlocal o0
        n = math.prod(shape)
        out = sm[:, o0:o0 + n].reshape((NDEV,) + shape)
        o0 += n + (-n) % LANES
        return out
    a_norm_f = _slabs_to_cols(take(a_norm.shape))
    conv_w_f = _slabs_to_cols(take(ssm_conv_w.shape))[0]
    conv_b_f = _slabs_to_cols(take(ssm_conv_b.shape))
    ssm_norm_f = _slabs_to_cols(take(ssm_norm.shape))
    fcw = _slabs_to_cols(take(ffn_conv_w.shape))
    fcw_g, fcw_v = fcw[:, :, :f], fcw[:, :, f:]
    dtb, alog, dsk = _pad_lanes(ssm_dt_bias), _pad_lanes(ssm_a_log), _pad_lanes(ssm_d)
    kvn, fin = kv_norm.reshape(1, d), final_norm.reshape(1, d)

    h0 = _rms_fwd(xs, a_norm_f, "a_norm")
    z = _mm(h0, w_z, name="in_z")
    xbc_pre = _mm(h0, w_xbc, name="in_xbc")
    dtr = _mm(h0, w_dt, name="in_dt")
    xbc = _conv_silu_fwd(xbc_pre, conv_w_f, conv_b_f, "ssm_conv")
    dt = _softplus_fwd(dtr, dtb, "ssm_dt")
    y, prevs = _ssd_fwd(xbc, dt, alog, dsk, di, nh, ng, "ssd")
    yn = _gnorm_fwd(y, z, ssm_norm_f, ng, "ssm_gnorm")
    x1 = _mm(yn, w_out, res=xs, name="ssm_out")
    x2, ffn0 = _ffn_forward(x1, ffn_norm[0:1], w_up_g[0], w_up_v[0], fcw_g[0], fcw_v[0], w_down[0], "ffn0")
    hk = _rms_fwd(x2, kvn, "kv_norm")
    kv = _mm(hk, w_kvf, name="kv_proj")
    h2 = _rms_fwd(x2, b_norm, "b_norm")
    q = _mm(h2, w_q, name="q_proj")
    att = []
    for g, (window, dil) in enumerate(ATT_PATTERNS):
        tabs = _rot_tables(s, dil)
        qp = _perm(q[:, g * qg:(g + 1) * qg], dil)
        kp = _perm(kv[:, g * kg:(g + 1) * kg], dil)
        vp = _perm(kv[:, kvd + g * kg:kvd + (g + 1) * kg], dil)
        n_blk = s // dil // ATT_BLOCK
        og, lg = _attn_fwd(qp, kp, vp, tabs, n_blk, f"attn{g}")
        att.append((qp, kp, vp, tabs, n_blk, dil, _unperm(og, dil), _unperm(lg, dil)))
    o, lse = _merge_fwd([t[6] for t in att], [t[7] for t in att], "attn_merge")
    x3 = _mm(o, w_o, res=x2, name="attn_out")
    x4, ffn1 = _ffn_forward(x3, ffn_norm[1:2], w_up_g[1], w_up_v[1], fcw_g[1], fcw_v[1], w_down[1], "ffn1")
    loss_part, dx4, dfin = _final_loss(x4, fin, tgt, "loss_head")

    dx3, (dwup1, dfc1, dwdown1, dfn1) = _ffn_backward(
        x3, ffn1, dx4, ffn_norm[1:2], w_up_g[1], w_up_v[1], fcw_g[1], fcw_v[1], w_down[1], "ffn1")
    dw_o = _mm(o, dx3, ta=True, name="attn_dwo")
    do = _mm(dx3, w_o, tb=True, name="attn_do")
    delta = _delta(do, o, "attn_delta")
    dqs, dks, dvs = [], [], []
    for g, (qp, kp, vp, tabs, n_blk, dil, _, _) in enumerate(att):
        dqp, dkp, dvp = _attn_bwd(qp, kp, vp, _perm(do, dil), _perm(lse, dil), _perm(delta, dil), tabs, n_blk,
                                  f"attn{g}_bwd")
        dqs.append(_unperm(dqp, dil))
        dks.append(_unperm(dkp, dil))
        dvs.append(_unperm(dvp, dil))
    dq = jnp.concatenate(dqs, axis=1)
    dkv = jnp.concatenate(dks + dvs, axis=1)
    dw_q = _mm(h2, dq, ta=True, name="q_dw")
    dh2 = _mm(dq, w_q, tb=True, name="q_dh")
    dw_kv = _mm(hk, dkv, ta=True, name="kv_dw")
    dhk = _mm(dkv, w_kvf, tb=True, name="kv_dh")
    dx2, db_norm = _rms_bwd(x2, b_norm, dh2, dx3, "b_norm_bwd")
    dx2, dkv_norm = _rms_bwd(x2, kvn, dhk, dx2, "kv_norm_bwd")
    dx1, (dwup0, dfc0, dwdown0, dfn0) = _ffn_backward(
        x1, ffn0, dx2, ffn_norm[0:1], w_up_g[0], w_up_v[0], fcw_g[0], fcw_v[0], w_down[0], "ffn0")
    dw_out = _mm(yn, dx1, ta=True, name="ssm_dwout")
    dyn = _mm(dx1, w_out, tb=True, name="ssm_dyn")
    dy, dz, dssm_norm = _gnorm_bwd(dyn, y, z, ssm_norm_f, ng, "ssm_gnorm_bwd")
    dxbc, ddt, dalog, ddsk = _ssd_bwd(xbc, dt, alog, dsk, prevs, dy, di, nh, ng, "ssd_bwd")
    ddtr, ddtb = _softplus_bwd(ddt, dtr, dtb, nh, "ssm_dt_bwd")
    dxbc_pre, dconv_w, dconv_b = _conv_silu_bwd(xbc_pre, conv_w_f, conv_b_f, dxbc, "ssm_conv_bwd")
    dw_z = _mm(h0, dz, ta=True, name="in_dwz")
    dw_xbc = _mm(h0, dxbc_pre, ta=True, name="in_dwxbc")
    dw_dt = _mm(h0, ddtr, ta=True, name="in_dwdt")[:, :nh]
    dh0 = _mm(dz, w_z, tb=True, name="in_dh_z")
    dh0 = _mm(dxbc_pre, w_xbc, tb=True, res=dh0, name="in_dh_xbc")
    dh0 = _mm(ddtr, w_dt, tb=True, res=dh0, name="in_dh_dt")
    dx0, da_norm = _rms_bwd(xs, a_norm_f, dh0, dx1, "a_norm_bwd")

    full = {
        'ssm_w_in': jnp.concatenate([dw_z, dw_xbc, dw_dt], axis=1)[None], 'ssm_w_out': dw_out[None], 'w_kv': dw_kv,
        'att_w_q': dw_q[None], 'att_w_o': dw_o[None], 'ffn_w_up': jnp.stack([dwup0, dwup1]),
        'ffn_w_down': jnp.stack([dwdown0, dwdown1]),
    }
    by_cols = {'ssm_w_in', 'w_kv', 'att_w_q', 'ffn_w_up'}
    big_names = list(full)
    slabs = [(_cols_to_slabs if n in by_cols else _rows_to_slabs)(full[n]).astype(BF16) for n in big_names]
    small_full = {
        'a_norm': da_norm, 'ssm_conv_w': dconv_w[None], 'ssm_conv_b': dconv_b, 'ssm_dt_bias': ddtb[:, :nh],
        'ssm_a_log': dalog[:, :nh], 'ssm_d': ddsk[:, :nh], 'ssm_norm': dssm_norm, 'kv_norm': dkv_norm.reshape(d),
        'b_norm': db_norm, 'ffn_norm': jnp.concatenate([dfn0, dfn1], axis=0), 'ffn_conv_w': jnp.stack([dfc0, dfc1]),
        'final_norm': dfin.reshape(d),
    }
    small_names = list(small_full)
    packed, spans = _pack([small_full[n] for n in small_names])
    recv = _exchange(slabs, [packed], "exchange_grads")
    small_sum = _sum_slabs(recv[-1], "sum_small_grads").reshape(-1)

    me = _dev_index(*_coords())
    res = {}
    for n, r in zip(big_names, recv[:-1]):
        w = given[n]
        c = w.shape[-1]
        outs = _adamw(r.reshape(NDEV, -1, c), w.reshape(-1, c), given['m_' + n].reshape(-1, c),
                      given['v_' + n].reshape(-1, c), f"adamw_{n}")
        res[n] = [o_.reshape(w.shape) for o_ in outs]
    sharded_small = {'a_norm', 'ssm_conv_w', 'ssm_conv_b', 'ssm_norm', 'ffn_conv_w'}
    for n, (off, size) in zip(small_names, spans):
        w = given[n]
        gfull = small_sum[off:off + size].reshape(small_full[n].shape)
        if n in sharded_small:
            c = w.shape[-1]
            gfull = lax.dynamic_slice_in_dim(gfull, me * c, c, axis=gfull.ndim - 1)
        c = w.shape[-1]
        outs = _adamw(gfull.reshape(1, -1, c), w.reshape(-1, c), given['m_' + n].reshape(-1, c),
                      given['v_' + n].reshape(-1, c), f"adamw_{n}")
        res[n] = [o_.reshape(w.shape) for o_ in outs]

    loss = lax.psum(loss_part[0, 0], AXES)
    return (loss, dx0[None], *[res[n][0] for n in WEIGHTS], *[res[n][1] for n in WEIGHTS],
            *[res[n][2] for n in WEIGHTS], *[res[n][3] for n in WEIGHTS])
```

```python
import functools
import math

import jax
import jax.numpy as jnp
from jax import lax
from jax.experimental import pallas as pl
from jax.experimental.pallas import tpu as pltpu

F32, BF16 = jnp.float32, jnp.bfloat16
AXES = ("x", "y", "c")
NDEV = 8
MESH = pl.DeviceIdType.MESH
HIGHEST = lax.Precision.HIGHEST

LANES = 128
VMEM_LIMIT_BYTES = 48 * 1024 * 1024

RMS_EPS = 1e-6
GATED_NORM_EPS = 1e-5
SSM_HEAD_DIM = 64
SSM_N_GROUPS = 8
SSM_D_STATE = 128
SSM_CONV = 4
SSM_CHUNK = 128
ATT_PATTERNS = ((128, 1), (512, 4), (2048, 16))
ATT_HEAD_DIM = 128
ATT_HEADS_PER_GROUP = 8
ATT_KV_HEADS_PER_GROUP = 2
ATT_BLOCK = 128
ROPE_DIM = ATT_HEAD_DIM // 4
ROPE_THETA = 500000.0
FFN_CONV = 3
ADAM_LR = 0.001
ADAM_B1 = 0.9
ADAM_B2 = 0.999
ADAM_EPS = 1e-08
ADAM_WD = 0.01
ADAM_STEP = 10
NEG = -1e30

WEIGHTS = ['a_norm', 'ssm_w_in', 'ssm_conv_w', 'ssm_conv_b', 'ssm_dt_bias', 'ssm_a_log', 'ssm_d', 'ssm_norm',
           'ssm_w_out', 'kv_norm', 'w_kv', 'b_norm', 'att_w_q', 'att_w_o', 'ffn_norm', 'ffn_w_up', 'ffn_conv_w',
           'ffn_w_down', 'final_norm']


def _params(sem=None):
    kw = dict(vmem_limit_bytes=VMEM_LIMIT_BYTES)
    if sem is not None:
        kw["dimension_semantics"] = sem
    return pltpu.CompilerParams(**kw)


def _pick(n, pref):
    if n <= pref:
        return n
    t = (pref // LANES) * LANES
    while t >= LANES:
        if n % t == 0:
            return t
        t -= LANES
    return n


def _dot(a, b, dims=(((1,), (0,)), ((), ())), precision=None):
    return lax.dot_general(a, b, dims, precision=precision, preferred_element_type=F32)


_NT = (((1,), (1,)), ((), ()))
_TN = (((0,), (0,)), ((), ()))


def _mm(a, b, *, ta=False, tb=False, res=None, out_dtype=F32, name, tm=1024, tn=1408, tk=2048):
    m = a.shape[1] if ta else a.shape[0]
    k = a.shape[0] if ta else a.shape[1]
    n = b.shape[0] if tb else b.shape[1]
    assert k == (b.shape[1] if tb else b.shape[0])
    tm, tn, tk = _pick(m, tm), _pick(n, tn), _pick(k, tk)
    nk = k // tk
    a_spec = pl.BlockSpec((tk, tm), lambda i, j, l: (l, i)) if ta else pl.BlockSpec((tm, tk), lambda i, j, l: (i, l))
    b_spec = pl.BlockSpec((tn, tk), lambda i, j, l: (j, l)) if tb else pl.BlockSpec((tk, tn), lambda i, j, l: (l, j))
    o_spec = pl.BlockSpec((tm, tn), lambda i, j, l: (i, j))
    dims = (((0 if ta else 1,), (1 if tb else 0,)), ((), ()))
    has_res = res is not None

    def body(*refs):
        a_ref, b_ref = refs[:2]
        r_ref = refs[2] if has_res else None
        o_ref = refs[2 + has_res]
        p = _dot(a_ref[...].astype(BF16), b_ref[...].astype(BF16), dims)

        def finish(r):
            if has_res:
                r = r + r_ref[...]
            o_ref[...] = r.astype(o_ref.dtype)

        if nk == 1:
            finish(p)
            return
        acc = refs[3 + has_res]
        l = pl.program_id(2)

        @pl.when(l == 0)
        def _():
            acc[...] = p

        @pl.when(jnp.logical_and(l > 0, l < nk - 1))
        def _():
            acc[...] += p

        @pl.when(l == nk - 1)
        def _():
            finish(acc[...] + p)

    ins = [a, b] + ([res] if has_res else [])
    in_specs = [a_spec, b_spec] + ([o_spec] if has_res else [])
    return pl.pallas_call(
        body, name=name, grid=(m // tm, n // tn, nk), in_specs=in_specs, out_specs=o_spec,
        out_shape=jax.ShapeDtypeStruct((m, n), out_dtype),
        scratch_shapes=[pltpu.VMEM((tm, tn), F32)] if nk > 1 else [],
        compiler_params=_params(("parallel", "parallel", "arbitrary")))(*ins)


def _rowwise(fn, rows, bcasts, outs, accs=(), *, tile, name):
    s = rows[0].shape[0]
    tile = min(tile, s)
    n_in, n_out, n_acc = len(rows) + len(bcasts), len(outs), len(accs)

    def body(*refs):
        vals = fn(*[r[...] for r in refs[:n_in]])
        o_refs = refs[n_in:n_in + n_out]
        a_refs = refs[n_in + n_out:]
        for r, v in zip(o_refs, vals[:n_out]):
            r[...] = v.astype(r.dtype)

        @pl.when(pl.program_id(0) == 0)
        def _():
            for r in a_refs:
                r[...] = jnp.zeros(r.shape, r.dtype)

        for r, v in zip(a_refs, vals[n_out:]):
            r[...] += v

    in_specs = [pl.BlockSpec((tile, r.shape[1]), lambda i: (i, 0)) for r in rows]
    in_specs += [pl.BlockSpec(b.shape, lambda i: (0, 0)) for b in bcasts]
    out_specs = [pl.BlockSpec((tile, c), lambda i: (i, 0)) for c, _ in outs]
    out_specs += [pl.BlockSpec(sh, lambda i: (0, 0)) for sh, _ in accs]
    out_shape = [jax.ShapeDtypeStruct((s, c), dt) for c, dt in outs]
    out_shape += [jax.ShapeDtypeStruct(sh, dt) for sh, dt in accs]
    return pl.pallas_call(body, name=name, grid=(s // tile,), in_specs=in_specs, out_specs=out_specs,
                          out_shape=out_shape, compiler_params=_params(("arbitrary",)))(*rows, *bcasts)


def _rms_fwd(x, w, name):
    def fn(x, w):
        r = lax.rsqrt(jnp.mean(x * x, axis=-1, keepdims=True) + RMS_EPS)
        return (x * r * w,)
    return _rowwise(fn, [x], [w], [(x.shape[1], BF16)], tile=256, name=name)[0]


def _rms_bwd(x, w, dh, dres, name):
    def fn(x, dh, dres, w):
        r = lax.rsqrt(jnp.mean(x * x, axis=-1, keepdims=True) + RMS_EPS)
        xh = x * r
        dxh = dh * w
        dx = dres + r * (dxh - xh * jnp.mean(dxh * xh, axis=-1, keepdims=True))
        return dx, dx, jnp.sum(dh * xh, axis=0, keepdims=True)
    d = x.shape[1]
    return _rowwise(fn, [x, dh, dres], [w], [(d, F32), (d, BF16)], [((1, d), F32)], tile=256, name=name)


def _final_loss(x, w, tgt, name):
    d = x.shape[1]

    def fn(x, t, w):
        r = lax.rsqrt(jnp.mean(x * x, axis=-1, keepdims=True) + RMS_EPS)
        xh = x * r
        err = xh * w - t
        part = jnp.sum(jnp.mean(err * err, axis=-1, keepdims=True), axis=0, keepdims=True) * 0.5
        dy = err * (1.0 / d)
        dxh = dy * w
        dx = r * (dxh - xh * jnp.mean(dxh * xh, axis=-1, keepdims=True))
        return dx, dx, part, jnp.sum(dy * xh, axis=0, keepdims=True)
    dx, dxb, part, dw = _rowwise(fn, [x, tgt], [w], [(d, F32), (d, BF16)], [((1, 1), F32), ((1, d), F32)],
                                 tile=256, name=name)
    return part, dx, dxb, dw


def _softplus_fwd(dtr, bias, name):
    def fn(r, b):
        v = r + b
        return (jnp.maximum(v, 0.0) + jnp.log(1.0 + jnp.exp(-jnp.abs(v))),)
    return _rowwise(fn, [dtr], [bias], [(LANES, F32)], tile=512, name=name)[0]


def _softplus_bwd(ddt, dtr, bias, n_heads, name):
    def fn(g, r, b):
        lane = lax.broadcasted_iota(jnp.int32, g.shape, 1)
        d = jnp.where(lane < n_heads, g * jax.nn.sigmoid(r + b), 0.0)
        return d, jnp.sum(d, axis=0, keepdims=True)
    return _rowwise(fn, [ddt, dtr], [bias], [(LANES, BF16)], [((1, LANES), F32)], tile=512, name=name)


def _gnorm_fwd(y, z, w, n_groups, name):
    di = y.shape[1]
    gs = di // n_groups

    def fn(y, z, w):
        y2 = y * (z * jax.nn.sigmoid(z))
        out = []
        for g in range(n_groups):
            sl = y2[:, g * gs:(g + 1) * gs]
            r = lax.rsqrt(jnp.mean(sl * sl, axis=-1, keepdims=True) + GATED_NORM_EPS)
            out.append(sl * r)
        return (jnp.concatenate(out, axis=1) * w,)
    return _rowwise(fn, [y, z], [w], [(di, BF16)], tile=256, name=name)[0]


def _gnorm_bwd(dyn, y, z, w, n_groups, name):
    di = y.shape[1]
    gs = di // n_groups

    def fn(dyn, y, z, w):
        sig = jax.nn.sigmoid(z)
        sz = z * sig
        y2 = y * sz
        d2n = dyn * w
        dy2, yhat = [], []
        for g in range(n_groups):
            sl = y2[:, g * gs:(g + 1) * gs]
            dg = d2n[:, g * gs:(g + 1) * gs]
            r = lax.rsqrt(jnp.mean(sl * sl, axis=-1, keepdims=True) + GATED_NORM_EPS)
            yh = sl * r
            dy2.append(r * (dg - yh * jnp.mean(dg * yh, axis=-1, keepdims=True)))
            yhat.append(yh)
        dy2 = jnp.concatenate(dy2, axis=1)
        yhat = jnp.concatenate(yhat, axis=1)
        dz = dy2 * y * (sig * (1.0 + z * (1.0 - sig)))
        return dy2 * sz, dz, jnp.sum(dyn * yhat, axis=0, keepdims=True)
    return _rowwise(fn, [dyn, y, z], [w], [(di, F32), (di, BF16)], [((1, di), F32)], tile=128, name=name)


def _merge_fwd(os_, lses, name):
    n = len(os_)

    def fn(*v):
        o, l = v[:n], v[n:]
        m = functools.reduce(jnp.maximum, l)
        e = [jnp.exp(li - m) for li in l]
        tot = functools.reduce(jnp.add, e)
        acc = functools.reduce(jnp.add, [ei * oi for ei, oi in zip(e, o)]) / tot
        return acc, acc, m + jnp.log(tot)
    c = os_[0].shape[1]
    return _rowwise(fn, list(os_) + list(lses), [], [(c, F32), (c, BF16), (c, F32)], tile=256, name=name)


def _delta(do, o, name):
    c = o.shape[1]

    def fn(do, o):
        p = do * o
        out = [jnp.broadcast_to(jnp.sum(p[:, j:j + ATT_HEAD_DIM], axis=-1, keepdims=True), (p.shape[0], ATT_HEAD_DIM))
               for j in range(0, c, ATT_HEAD_DIM)]
        return (jnp.concatenate(out, axis=1),)
    return _rowwise(fn, [do, o], [], [(c, F32)], tile=256, name=name)[0]


def _sum_slabs(recv, name):
    def body(r_ref, o_ref):
        acc = r_ref[0]
        for k in range(1, NDEV):
            acc = acc + r_ref[k]
        o_ref[...] = acc
    return pl.pallas_call(body, name=name, out_shape=jax.ShapeDtypeStruct(recv.shape[1:], F32),
                          compiler_params=_params())(recv)


def _shift_down(x, k):
    if k == 0:
        return x
    row = lax.broadcasted_iota(jnp.int32, x.shape, 0)
    return jnp.where(row >= k, pltpu.roll(x, k, 0), 0.0)


def _shift_up(x, k):
    if k == 0:
        return x
    s = x.shape[0]
    row = lax.broadcasted_iota(jnp.int32, x.shape, 0)
    return jnp.where(row < s - k, pltpu.roll(x, s - k, 0), 0.0)


def _conv(x, w):
    kw = w.shape[0]
    return functools.reduce(jnp.add, [w[k:k + 1, :] * _shift_down(x, kw - 1 - k) for k in range(kw)])


def _conv_t(dy, w):
    kw = w.shape[0]
    return functools.reduce(jnp.add, [w[k:k + 1, :] * _shift_up(dy, kw - 1 - k) for k in range(kw)])


def _conv_dw(x, dy, dw_ref):
    kw = dw_ref.shape[0]
    for k in range(kw):
        dw_ref[k:k + 1, :] = jnp.sum(dy * _shift_down(x, kw - 1 - k), axis=0, keepdims=True)


def _dsilu(pre):
    sig = jax.nn.sigmoid(pre)
    return sig * (1.0 + pre * (1.0 - sig))


def _col_specs(s, c, kw, tc):
    return (pl.BlockSpec((s, tc), lambda j: (0, j)), pl.BlockSpec((kw, tc), lambda j: (0, j)),
            pl.BlockSpec((1, tc), lambda j: (0, j)))


def _conv_silu_fwd(x, w, b, name):
    s, c = x.shape
    tc = LANES
    xs, ws, bs = _col_specs(s, c, w.shape[0], tc)

    def body(x_ref, w_ref, b_ref, o_ref):
        pre = _conv(x_ref[...], w_ref[...]) + b_ref[...]
        o_ref[...] = pre * jax.nn.sigmoid(pre)
    return pl.pallas_call(body, name=name, grid=(c // tc,), in_specs=[xs, ws, bs], out_specs=xs,
                          out_shape=jax.ShapeDtypeStruct((s, c), F32), compiler_params=_params(("parallel",)))(x, w, b)


def _conv_silu_bwd(x, w, b, dy, name):
    s, c = x.shape
    tc = LANES
    xs, ws, bs = _col_specs(s, c, w.shape[0], tc)

    def body(x_ref, w_ref, b_ref, dy_ref, dx_ref, dw_ref, db_ref):
        xv, wv = x_ref[...], w_ref[...]
        pre = _conv(xv, wv) + b_ref[...]
        dpre = dy_ref[...] * _dsilu(pre)
        dx_ref[...] = _conv_t(dpre, wv).astype(dx_ref.dtype)
        _conv_dw(xv, dpre, dw_ref)
        db_ref[...] = jnp.sum(dpre, axis=0, keepdims=True)
    return pl.pallas_call(
        body, name=name, grid=(c // tc,), in_specs=[xs, ws, bs, xs], out_specs=[xs, ws, bs],
        out_shape=[jax.ShapeDtypeStruct((s, c), BF16), jax.ShapeDtypeStruct(w.shape, F32),
                   jax.ShapeDtypeStruct((1, c), F32)],
        compiler_params=_params(("parallel",)))(x, w, b, dy)


def _ffn_gate_fwd(ug, uv, wg, wv, name):
    s, c = ug.shape
    tc = LANES
    xs, ws, _ = _col_specs(s, c, wg.shape[0], tc)

    def body(g_ref, v_ref, wg_ref, wv_ref, o_ref):
        g = _conv(g_ref[...], wg_ref[...])
        v = _conv(v_ref[...], wv_ref[...])
        o_ref[...] = (g * jax.nn.sigmoid(g) * v).astype(o_ref.dtype)
    return pl.pallas_call(body, name=name, grid=(c // tc,), in_specs=[xs, xs, ws, ws], out_specs=xs,
                          out_shape=jax.ShapeDtypeStruct((s, c), BF16),
                          compiler_params=_params(("parallel",)))(ug, uv, wg, wv)


def _ffn_gate_bwd(ug, uv, wg, wv, df, name):
    s, c = ug.shape
    tc = LANES
    xs, ws, _ = _col_specs(s, c, wg.shape[0], tc)

    def body(g_ref, v_ref, wg_ref, wv_ref, df_ref, dg_ref, dv_ref, dwg_ref, dwv_ref):
        gp, vp, wgv, wvv = g_ref[...], v_ref[...], wg_ref[...], wv_ref[...]
        g = _conv(gp, wgv)
        v = _conv(vp, wvv)
        dfv = df_ref[...]
        dg = dfv * v * _dsilu(g)
        dv = dfv * (g * jax.nn.sigmoid(g))
        dg_ref[...] = _conv_t(dg, wgv).astype(dg_ref.dtype)
        dv_ref[...] = _conv_t(dv, wvv).astype(dv_ref.dtype)
        _conv_dw(gp, dg, dwg_ref)
        _conv_dw(vp, dv, dwv_ref)
    return pl.pallas_call(
        body, name=name, grid=(c // tc,), in_specs=[xs, xs, ws, ws, xs], out_specs=[xs, xs, ws, ws],
        out_shape=[jax.ShapeDtypeStruct((s, c), BF16), jax.ShapeDtypeStruct((s, c), BF16),
                   jax.ShapeDtypeStruct(wg.shape, F32), jax.ShapeDtypeStruct(wv.shape, F32)],
        compiler_params=_params(("parallel",)))(ug, uv, wg, wv, df)


def _ssd_common(dt, alog, n_heads):
    ln = dt.shape[0]
    lane = lax.broadcasted_iota(jnp.int32, (1, LANES), 1)
    a = jnp.where(lane < n_heads, -jnp.exp(alog), 0.0)
    row = lax.broadcasted_iota(jnp.int32, (ln, ln), 0)
    col = lax.broadcasted_iota(jnp.int32, (ln, ln), 1)
    tril = col <= row
    acs = _dot(tril.astype(F32), dt * a, precision=HIGHEST)
    return a, acs, acs.T, tril


def _ssd_fwd(xbc, dt, alog, dskip, di, n_heads, n_groups, name):
    s, convd = xbc.shape
    ln, p, ns = SSM_CHUNK, SSM_HEAD_DIM, SSM_D_STATE
    nc, hg = s // ln, n_heads // n_groups

    def body(x_ref, dt_ref, alog_ref, d_ref, y_ref, prev_ref, st):
        @pl.when(pl.program_id(0) == 0)
        def _():
            st[...] = jnp.zeros(st.shape, F32)

        dt = dt_ref[...]
        _, acs, acs_t, tril = _ssd_common(dt, alog_ref[...], n_heads)
        e_all = jnp.exp(acs)
        last = acs[ln - 1:ln, :]
        ds_all = jnp.exp(last - acs)
        t_all = jnp.exp(last)
        dsk = d_ref[...]
        for g in range(n_groups):
            bg = x_ref[:, di + g * ns:di + (g + 1) * ns].astype(BF16)
            cg = x_ref[:, di + (n_groups + g) * ns:di + (n_groups + g + 1) * ns].astype(BF16)
            gm = _dot(cg, bg, _NT)
            for j in range(hg):
                h = g * hg + j
                xh = x_ref[:, h * p:(h + 1) * p]
                xdt = xh * dt[:, h:h + 1]
                seg = acs[:, h:h + 1] - acs_t[h:h + 1, :]
                m = jnp.where(tril, gm * jnp.exp(jnp.where(tril, seg, 0.0)), 0.0)
                prev = st[h]
                prev_ref[0, h] = prev
                y = _dot(m.astype(BF16), xdt.astype(BF16))
                y = y + _dot(cg, prev.astype(BF16), _NT) * e_all[:, h:h + 1]
                y = y + xh * dsk[:, h:h + 1]
                snew = _dot((xdt * ds_all[:, h:h + 1]).astype(BF16), bg, _TN)
                st[h] = prev * t_all[:, h:h + 1] + snew
                y_ref[:, h * p:(h + 1) * p] = y

    vec = pl.BlockSpec((1, LANES), lambda c: (0, 0))
    return pl.pallas_call(
        body, name=name, grid=(nc,),
        in_specs=[pl.BlockSpec((ln, convd), lambda c: (c, 0)), pl.BlockSpec((ln, LANES), lambda c: (c, 0)), vec, vec],
        out_specs=[pl.BlockSpec((ln, di), lambda c: (c, 0)),
                   pl.BlockSpec((1, n_heads, p, ns), lambda c: (c, 0, 0, 0))],
        out_shape=[jax.ShapeDtypeStruct((s, di), F32), jax.ShapeDtypeStruct((nc, n_heads, p, ns), F32)],
        scratch_shapes=[pltpu.VMEM((n_heads, p, ns), F32)],
        compiler_params=_params(("arbitrary",)))(xbc, dt, alog, dskip)


def _ssd_bwd(xbc, dt, alog, dskip, prev_all, dy, di, n_heads, n_groups, name):
    s, convd = xbc.shape
    ln, p, ns = SSM_CHUNK, SSM_HEAD_DIM, SSM_D_STATE
    nc, hg = s // ln, n_heads // n_groups

    def body(x_ref, dt_ref, alog_ref, d_ref, prev_ref, dy_ref, dx_ref, ddt_ref, da_ref, dd_ref, dh):
        step = pl.program_id(0)

        @pl.when(step == 0)
        def _():
            dh[...] = jnp.zeros(dh.shape, F32)
            da_ref[...] = jnp.zeros(da_ref.shape, F32)
            dd_ref[...] = jnp.zeros(dd_ref.shape, F32)

        dt = dt_ref[...]
        a, acs, acs_t, tril = _ssd_common(dt, alog_ref[...], n_heads)
        e_all = jnp.exp(acs)
        last = acs[ln - 1:ln, :]
        ds_all = jnp.exp(last - acs)
        t_all = jnp.exp(last)
        dsk = d_ref[...]
        lane = lax.broadcasted_iota(jnp.int32, (ln, LANES), 1)
        lane1 = lax.broadcasted_iota(jnp.int32, (1, LANES), 1)
        sub = lax.broadcasted_iota(jnp.int32, (LANES, ln), 0)
        rowi = lax.broadcasted_iota(jnp.int32, (ln, LANES), 0)
        dacs_c = jnp.zeros((ln, LANES), F32)
        dacs_r = jnp.zeros((LANES, ln), F32)
        dlast = jnp.zeros((1, LANES), F32)
        ddt_x = jnp.zeros((ln, LANES), F32)
        dd = jnp.zeros((1, LANES), F32)

        def tot(v):
            return jnp.sum(jnp.sum(v, axis=1, keepdims=True), axis=0, keepdims=True)

        for g in range(n_groups):
            bg = x_ref[:, di + g * ns:di + (g + 1) * ns].astype(BF16)
            cg = x_ref[:, di + (n_groups + g) * ns:di + (n_groups + g + 1) * ns].astype(BF16)
            gm = _dot(cg, bg, _NT)
            dgm = jnp.zeros((ln, ln), F32)
            dcg = jnp.zeros((ln, ns), F32)
            dbg = jnp.zeros((ln, ns), F32)
            for j in range(hg):
                h = g * hg + j
                xh = x_ref[:, h * p:(h + 1) * p]
                dth = dt[:, h:h + 1]
                xdt = xh * dth
                dyh = dy_ref[:, h * p:(h + 1) * p]
                eh, dsh, th = e_all[:, h:h + 1], ds_all[:, h:h + 1], t_all[:, h:h + 1]
                seg = acs[:, h:h + 1] - acs_t[h:h + 1, :]
                dec = jnp.where(tril, jnp.exp(jnp.where(tril, seg, 0.0)), 0.0)
                m = gm * dec
                prev = prev_ref[0, h]
                dhn = dh[h]
                prevb, dhb, dyb, xdtb = prev.astype(BF16), dhn.astype(BF16), dyh.astype(BF16), xdt.astype(BF16)
                yo = _dot(cg, prevb, _NT)
                dyob = (dyh * eh).astype(BF16)
                c_col = jnp.sum(dyh * yo, axis=1, keepdims=True) * eh
                dcg = dcg + _dot(dyob, prevb)
                dprev = th * dhn + _dot(dyob, cg, _TN)
                dtt = tot(dhn * prev)
                w = _dot(bg, dhb, _NT)
                dxdt = w * dsh
                dds = jnp.sum(w * xdt, axis=1, keepdims=True)
                dbg = dbg + _dot((xdt * dsh).astype(BF16), dhb)
                dm = _dot(dyb, xdtb, _NT)
                dxdt = dxdt + _dot(m.astype(BF16), dyb, _TN)
                dgm = dgm + dm * dec
                q = dm * m
                c_col = c_col + jnp.sum(q, axis=1, keepdims=True) - dds * dsh
                r_row = -jnp.sum(q, axis=0, keepdims=True)
                dlast_h = tot(dds * dsh) + dtt * th
                dacs_c = dacs_c + jnp.where(lane == h, c_col, 0.0)
                dacs_r = dacs_r + jnp.where(sub == h, r_row, 0.0)
                dlast = dlast + jnp.where(lane1 == h, dlast_h, 0.0)
                ddt_x = ddt_x + jnp.where(lane == h, jnp.sum(dxdt * xh, axis=1, keepdims=True), 0.0)
                dd = dd + jnp.where(lane1 == h, tot(dyh * xh), 0.0)
                dx_ref[:, h * p:(h + 1) * p] = dxdt * dth + dyh * dsk[:, h:h + 1]
                dh[h] = dprev
            dgb = dgm.astype(BF16)
            dx_ref[:, di + g * ns:di + (g + 1) * ns] = dbg + _dot(dgb, cg, _TN)
            dx_ref[:, di + (n_groups + g) * ns:di + (n_groups + g + 1) * ns] = dcg + _dot(dgb, bg)

        dacs = dacs_c + dacs_r.T + jnp.where(rowi == ln - 1, dlast, 0.0)
        row = lax.broadcasted_iota(jnp.int32, (ln, ln), 0)
        col = lax.broadcasted_iota(jnp.int32, (ln, ln), 1)
        dadt = _dot((col >= row).astype(F32), dacs, precision=HIGHEST)
        ddt_ref[...] = dadt * a + ddt_x
        da_ref[...] += jnp.sum(dadt * dt, axis=0, keepdims=True)
        dd_ref[...] += dd

        @pl.when(step == nc - 1)
        def _():
            da_ref[...] = da_ref[...] * a

    vec = pl.BlockSpec((1, LANES), lambda c: (0, 0))
    rev = lambda c: (nc - 1 - c, 0)
    return pl.pallas_call(
        body, name=name, grid=(nc,),
        in_specs=[pl.BlockSpec((ln, convd), rev), pl.BlockSpec((ln, LANES), rev), vec, vec,
                  pl.BlockSpec((1, n_heads, p, ns), lambda c: (nc - 1 - c, 0, 0, 0)), pl.BlockSpec((ln, di), rev)],
        out_specs=[pl.BlockSpec((ln, convd), rev), pl.BlockSpec((ln, LANES), rev), vec, vec],
        out_shape=[jax.ShapeDtypeStruct((s, convd), F32), jax.ShapeDtypeStruct((s, LANES), F32),
                   jax.ShapeDtypeStruct((1, LANES), F32), jax.ShapeDtypeStruct((1, LANES), F32)],
        scratch_shapes=[pltpu.VMEM((n_heads, p, ns), F32)],
        compiler_params=_params(("arbitrary",)))(xbc, dt, alog, dskip, prev_all, dy)


def _perm(a, d):
    if d == 1:
        return a
    s = a.shape[0]
    return a.reshape(s // d, d, -1).transpose(1, 0, 2).reshape(s, -1)


def _unperm(a, d):
    if d == 1:
        return a
    s = a.shape[0]
    return a.reshape(d, s // d, -1).transpose(1, 0, 2).reshape(s, -1)


def _rot_tables(s, d):
    half = ROPE_DIM // 2
    inv_freq = jnp.power(jnp.float32(ROPE_THETA), -jnp.arange(0, ROPE_DIM, 2, dtype=F32) / ROPE_DIM)
    v = jnp.arange(s, dtype=jnp.int32)
    pos = (v % (s // d)) * d + v // (s // d)
    ang = pos.astype(F32)[:, None] * inv_freq[None, :]
    cos, sin = jnp.cos(ang), jnp.sin(ang)
    zero = jnp.zeros((s, ATT_HEAD_DIM - ROPE_DIM), F32)
    cf = jnp.concatenate([cos, cos, jnp.ones_like(zero)], axis=1)
    s1 = jnp.concatenate([-sin, jnp.zeros_like(sin), zero], axis=1)
    s2 = jnp.concatenate([jnp.zeros_like(sin), sin, zero], axis=1)
    assert half * 2 == ROPE_DIM
    return cf, s1, s2


def _rot(x, tabs, sign):
    cf, s1, s2 = tabs
    half = ROPE_DIM // 2
    left = pltpu.roll(x, ATT_HEAD_DIM - half, 1)
    right = pltpu.roll(x, half, 1)
    return x * cf + sign * (left * s1 + right * s2)


def _att_masks(n, n_blk, rep):
    b = ATT_BLOCK
    row = lax.broadcasted_iota(jnp.int32, (rep * b, b), 0) & (b - 1)
    col = lax.broadcasted_iota(jnp.int32, (rep * b, b), 1)
    off = jnp.where(n % n_blk != 0, 0, 2 * b)
    return col <= row, col >= row + off


def _stack(x, rep):
    return jnp.concatenate([x[:, j * ATT_HEAD_DIM:(j + 1) * ATT_HEAD_DIM] for j in range(rep)], axis=0)


def _att_specs(nb, rep, cur, prv):
    b, hd = ATT_BLOCK, ATT_HEAD_DIM
    q_spec = pl.BlockSpec((b, rep * hd), lambda h, n: (cur(n), h))
    kc_spec = pl.BlockSpec((b, hd), lambda h, n: (cur(n), h))
    kp_spec = pl.BlockSpec((b, hd), lambda h, n: (prv(n), h))
    tc_spec = pl.BlockSpec((b, hd), lambda h, n: (cur(n), 0))
    tp_spec = pl.BlockSpec((b, hd), lambda h, n: (prv(n), 0))
    return q_spec, kc_spec, kp_spec, tc_spec, tp_spec


def _attn_fwd(q, k, v, tabs, n_blk, name):
    s = q.shape[0]
    b, hd = ATT_BLOCK, ATT_HEAD_DIM
    nb = s // b
    n_kv = ATT_KV_HEADS_PER_GROUP
    rep = ATT_HEADS_PER_GROUP // n_kv
    scale = hd ** -0.5

    def body(q_ref, kc_ref, kp_ref, vc_ref, vp_ref, cfc, s1c, s2c, cfp, s1p, s2p, o_ref, lse_ref):
        n = pl.program_id(1)
        tc = (cfc[...], s1c[...], s2c[...])
        tp = (cfp[...], s1p[...], s2p[...])
        qv = q_ref[...]
        q4 = jnp.concatenate([_rot(qv[:, j * hd:(j + 1) * hd], tc, 1.0) for j in range(rep)], axis=0).astype(BF16)
        kc = _rot(kc_ref[...], tc, 1.0).astype(BF16)
        kp = _rot(kp_ref[...], tp, 1.0).astype(BF16)
        mc, mp = _att_masks(n, n_blk, rep)
        sc = jnp.where(mc, _dot(q4, kc, _NT) * scale, NEG)
        sp = jnp.where(mp, _dot(q4, kp, _NT) * scale, NEG)
        m = jnp.maximum(jnp.max(sc, axis=1, keepdims=True), jnp.max(sp, axis=1, keepdims=True))
        pc, pp = jnp.exp(sc - m), jnp.exp(sp - m)
        l = jnp.sum(pc, axis=1, keepdims=True) + jnp.sum(pp, axis=1, keepdims=True)
        o = (_dot(pc.astype(BF16), vc_ref[...].astype(BF16)) + _dot(pp.astype(BF16), vp_ref[...].astype(BF16))) / l
        lse = jnp.broadcast_to(m + jnp.log(l), (rep * b, hd))
        for j in range(rep):
            o_ref[:, j * hd:(j + 1) * hd] = o[j * b:(j + 1) * b]
            lse_ref[:, j * hd:(j + 1) * hd] = lse[j * b:(j + 1) * b]

    cur = lambda n: n
    prv = lambda n: jnp.maximum(n - 1, 0)
    q_spec, kc_spec, kp_spec, tc_spec, tp_spec = _att_specs(nb, rep, cur, prv)
    return pl.pallas_call(
        body, name=name, grid=(n_kv, nb),
        in_specs=[q_spec, kc_spec, kp_spec, kc_spec, kp_spec, tc_spec, tc_spec, tc_spec, tp_spec, tp_spec, tp_spec],
        out_specs=[q_spec, q_spec],
        out_shape=[jax.ShapeDtypeStruct(q.shape, F32), jax.ShapeDtypeStruct(q.shape, F32)],
        compiler_params=_params(("parallel", "arbitrary")))(q, k, k, v, v, *tabs, *tabs)


def _attn_bwd(q, k, v, do, lse, delta, tabs, n_blk, name):
    s = q.shape[0]
    b, hd = ATT_BLOCK, ATT_HEAD_DIM
    nb = s // b
    n_kv = ATT_KV_HEADS_PER_GROUP
    rep = ATT_HEADS_PER_GROUP // n_kv
    scale = hd ** -0.5

    def body(q_ref, do_ref, lse_ref, dl_ref, kc_ref, kp_ref, vc_ref, vp_ref, cfc, s1c, s2c, cfp, s1p, s2p,
             dq_ref, dk_ref, dv_ref, ck, cv):
        n = pl.program_id(1)
        tp = (cfp[...], s1p[...], s2p[...])

        @pl.when(n == 0)
        def _():
            ck[...] = jnp.zeros(ck.shape, F32)
            cv[...] = jnp.zeros(cv.shape, F32)

        @pl.when(n < nb)
        def _():
            tc = (cfc[...], s1c[...], s2c[...])
            qv = q_ref[...]
            q4 = jnp.concatenate([_rot(qv[:, j * hd:(j + 1) * hd], tc, 1.0) for j in range(rep)],
                                 axis=0).astype(BF16)
            do4 = _stack(do_ref[...], rep).astype(BF16)
            lse4 = _stack(lse_ref[...], rep)
            dl4 = _stack(dl_ref[...], rep)
            kc = _rot(kc_ref[...], tc, 1.0).astype(BF16)
            kp = _rot(kp_ref[...], tp, 1.0).astype(BF16)
            vc, vp = vc_ref[...].astype(BF16), vp_ref[...].astype(BF16)
            mc, mp = _att_masks(n, n_blk, rep)
            pc = jnp.where(mc, jnp.exp(jnp.where(mc, _dot(q4, kc, _NT) * scale - lse4, 0.0)), 0.0)
            pp = jnp.where(mp, jnp.exp(jnp.where(mp, _dot(q4, kp, _NT) * scale - lse4, 0.0)), 0.0)
            dsc = (pc * (_dot(do4, vc, _NT) - dl4)).astype(BF16)
            dsp = (pp * (_dot(do4, vp, _NT) - dl4)).astype(BF16)
            dq4 = (_dot(dsc, kc) + _dot(dsp, kp)) * scale
            for j in range(rep):
                dq_ref[:, j * hd:(j + 1) * hd] = _rot(dq4[j * b:(j + 1) * b], tc, -1.0).astype(dq_ref.dtype)
            dk_prev = ck[...] + _dot(dsp, q4, _TN) * scale
            dv_prev = cv[...] + _dot(pp.astype(BF16), do4, _TN)
            dk_ref[...] = _rot(dk_prev, tp, -1.0).astype(dk_ref.dtype)
            dv_ref[...] = dv_prev.astype(dv_ref.dtype)
            ck[...] = _dot(dsc, q4, _TN) * scale
            cv[...] = _dot(pc.astype(BF16), do4, _TN)

        @pl.when(n == nb)
        def _():
            dk_ref[...] = _rot(ck[...], tp, -1.0).astype(dk_ref.dtype)
            dv_ref[...] = cv[...].astype(dv_ref.dtype)

    cur = lambda n: jnp.minimum(n, nb - 1)
    prv = lambda n: jnp.maximum(n - 1, 0)
    q_spec, kc_spec, kp_spec, tc_spec, tp_spec = _att_specs(nb, rep, cur, prv)
    return pl.pallas_call(
        body, name=name, grid=(n_kv, nb + 1),
        in_specs=[q_spec, q_spec, q_spec, q_spec, kc_spec, kp_spec, kc_spec, kp_spec,
                  tc_spec, tc_spec, tc_spec, tp_spec, tp_spec, tp_spec],
        out_specs=[q_spec, kp_spec, kp_spec],
        out_shape=[jax.ShapeDtypeStruct(q.shape, BF16), jax.ShapeDtypeStruct(k.shape, BF16),
                   jax.ShapeDtypeStruct(k.shape, BF16)],
        scratch_shapes=[pltpu.VMEM((b, hd), F32), pltpu.VMEM((b, hd), F32)],
        compiler_params=_params(("parallel", "arbitrary")))(q, do, lse, delta, k, k, v, v, *tabs, *tabs)


def _adamw(g_slabs, w, m, v, name):
    kk, r, c = g_slabs.shape
    tile = r if r <= 256 else _pick_rows(r, 256)

    def body(g_ref, w_ref, m_ref, v_ref, go_ref, d_ref, mo_ref, vo_ref):
        g = g_ref[0].astype(F32)
        for k in range(1, kk):
            g = g + g_ref[k].astype(F32)
        m2 = ADAM_B1 * m_ref[...] + (1.0 - ADAM_B1) * g
        v2 = ADAM_B2 * v_ref[...] + (1.0 - ADAM_B2) * jnp.square(g)
        m_hat = m2 / (1.0 - ADAM_B1 ** ADAM_STEP)
        v_hat = v2 / (1.0 - ADAM_B2 ** ADAM_STEP)
        go_ref[...] = g
        d_ref[...] = -ADAM_LR * (m_hat / (jnp.sqrt(v_hat) + ADAM_EPS) + ADAM_WD * w_ref[...])
        mo_ref[...] = m2
        vo_ref[...] = v2

    spec = pl.BlockSpec((tile, c), lambda i: (i, 0))
    return pl.pallas_call(
        body, name=name, grid=(r // tile,), in_specs=[pl.BlockSpec((kk, tile, c), lambda i: (0, i, 0)), spec, spec, spec],
        out_specs=[spec] * 4, out_shape=[jax.ShapeDtypeStruct((r, c), F32)] * 4,
        compiler_params=_params(("parallel",)))(g_slabs, w, m, v)


def _pick_rows(r, pref):
    t = (pref // 16) * 16
    while t >= 16:
        if r % t == 0:
            return t
        t -= 16
    return r


def _coords():
    return lax.axis_index("x"), lax.axis_index("y"), lax.axis_index("c")


def _dev_index(px, py, pc):
    return 4 * px + 2 * py + pc


def _all_gather(shards, name):
    na = len(shards)

    def body(*refs):
        ins, outs = refs[:na], refs[na:2 * na]
        send_sems, recv_sems, local_sems = refs[2 * na:]
        x, y, c = _coords()
        me, sibling = (x, y, c), (x, y, 1 - c)
        chips = [(1 - x, y), (x, 1 - y), (1 - x, 1 - y)]

        def copy(a, k, block, to, src=None):
            dst = outs[a].at[_dev_index(*block)]
            return pltpu.make_async_remote_copy(
                src_ref=dst if src is None else src, dst_ref=dst, send_sem=send_sems.at[a * 7 + k],
                recv_sem=recv_sems.at[a * 7 + k], device_id=to, device_id_type=MESH)

        mine = [pltpu.make_async_copy(ins[a], outs[a].at[_dev_index(*me)], local_sems.at[a]) for a in range(na)]
        for cp in mine:
            cp.start()
        first = []
        for a in range(na):
            first.append(copy(a, 0, me, sibling, src=ins[a]))
            first += [copy(a, 1 + j, me, (*chip, c), src=ins[a]) for j, chip in enumerate(chips)]
        for cp in first:
            cp.start()
        passed = []
        for j, chip in enumerate(chips):
            for a in range(na):
                copy(a, 1 + j, (*chip, c), me).wait_recv()
                cp = copy(a, 4 + j, (*chip, c), sibling)
                cp.start()
                passed.append(cp)
        for a in range(na):
            copy(a, 0, sibling, me).wait_recv()
            for j, chip in enumerate(chips):
                copy(a, 4 + j, (*chip, 1 - c), me).wait_recv()
        for cp in first + passed:
            cp.wait_send()
        for cp in mine:
            cp.wait()

    hbm = pl.BlockSpec(memory_space=pl.ANY)
    return pl.pallas_call(
        body, name=name, in_specs=[hbm] * na, out_specs=[hbm] * na,
        out_shape=[jax.ShapeDtypeStruct((NDEV,) + s.shape, s.dtype) for s in shards],
        scratch_shapes=[pltpu.SemaphoreType.DMA((7 * na,)), pltpu.SemaphoreType.DMA((7 * na,)),
                        pltpu.SemaphoreType.DMA((na,))])(*shards)


def _exchange(slabs, whole, name):
    ns, nw = len(slabs), len(whole)
    na = ns + nw

    def body(*refs):
        ins, outs = refs[:na], refs[na:2 * na]
        send_sems, recv_sems, local_sems = refs[2 * na:]
        x, y, c = _coords()
        me = _dev_index(x, y, c)

        def src_of(a, p):
            return ins[a].at[p] if a < ns else ins[a]

        def copy(a, k, peer):
            p = _dev_index(*peer)
            return pltpu.make_async_remote_copy(
                src_ref=src_of(a, p), dst_ref=outs[a].at[me], send_sem=send_sems.at[a * 7 + k - 1],
                recv_sem=recv_sems.at[a * 7 + k - 1], device_id=peer, device_id_type=MESH)

        def arrival(a, k, peer):
            p = _dev_index(*peer)
            return pltpu.make_async_remote_copy(
                src_ref=src_of(a, p), dst_ref=outs[a].at[p], send_sem=send_sems.at[a * 7 + k - 1],
                recv_sem=recv_sems.at[a * 7 + k - 1], device_id=peer, device_id_type=MESH)

        mine = [pltpu.make_async_copy(src_of(a, me), outs[a].at[me], local_sems.at[a]) for a in range(na)]
        for cp in mine:
            cp.start()
        peers = [(k, (x ^ (k >> 2), y ^ ((k >> 1) & 1), c ^ (k & 1))) for k in range(1, NDEV)]
        sent = [copy(a, k, peer) for k, peer in peers for a in range(na)]
        for cp in sent:
            cp.start()
        for k, peer in peers:
            for a in range(na):
                arrival(a, k, peer).wait_recv()
        for cp in sent:
            cp.wait_send()
        for cp in mine:
            cp.wait()

    hbm = pl.BlockSpec(memory_space=pl.ANY)
    out_shape = [jax.ShapeDtypeStruct(s.shape, s.dtype) for s in slabs]
    out_shape += [jax.ShapeDtypeStruct((NDEV,) + w.shape, w.dtype) for w in whole]
    return pl.pallas_call(
        body, name=name, in_specs=[hbm] * na, out_specs=[hbm] * na, out_shape=out_shape,
        scratch_shapes=[pltpu.SemaphoreType.DMA((7 * na,)), pltpu.SemaphoreType.DMA((7 * na,)),
                        pltpu.SemaphoreType.DMA((na,))])(*slabs, *whole)


def _pack(vecs):
    parts, spans, off = [], [], 0
    for v in vecs:
        n = v.size
        pad = (-n) % LANES
        parts.append(jnp.pad(v.reshape(-1).astype(F32), (0, pad)))
        spans.append((off, n))
        off += n + pad
    return jnp.concatenate(parts).reshape(-1, LANES), spans


def _pad_lanes(v):
    v = v.reshape(1, -1)
    return jnp.pad(v, ((0, 0), (0, LANES - v.shape[1])))


def _cols_to_slabs(g):
    sh = g.shape
    g = g.reshape(sh[:-1] + (NDEV, sh[-1] // NDEV))
    return jnp.moveaxis(g, -2, 0)


def _rows_to_slabs(g):
    sh = g.shape
    g = g.reshape(sh[:-2] + (NDEV, sh[-2] // NDEV, sh[-1]))
    return jnp.moveaxis(g, -3, 0)


def _slabs_to_cols(a):
    a = jnp.moveaxis(a, 0, -2)
    return a.reshape(a.shape[:-2] + (a.shape[-2] * a.shape[-1],))


def _slabs_to_rows(a):
    a = jnp.moveaxis(a, 0, -3)
    return a.reshape(a.shape[:-3] + (a.shape[-3] * a.shape[-2], a.shape[-1]))


def _ffn_forward(x, norm_w, wup_g, wup_v, cw_g, cw_v, wdown, tag):
    h = _rms_fwd(x, norm_w, f"{tag}_norm")
    ug = _mm(h, wup_g, name=f"{tag}_up_gate")
    uv = _mm(h, wup_v, name=f"{tag}_up_val")
    f = _ffn_gate_fwd(ug, uv, cw_g, cw_v, f"{tag}_gate")
    return _mm(f, wdown, res=x, name=f"{tag}_down"), (h, ug, uv, f)


def _ffn_backward(x, saved, dout, dout_b, norm_w, wup_g, wup_v, cw_g, cw_v, wdown, tag):
    h, ug, uv, f = saved
    dwdown = _mm(f, dout_b, ta=True, name=f"{tag}_dwdown")
    df = _mm(dout_b, wdown, tb=True, name=f"{tag}_df")
    dug, duv, dcg, dcv = _ffn_gate_bwd(ug, uv, cw_g, cw_v, df, f"{tag}_gate_bwd")
    dwg = _mm(h, dug, ta=True, name=f"{tag}_dwup_gate")
    dwv = _mm(h, duv, ta=True, name=f"{tag}_dwup_val")
    dh = _mm(dug, wup_g, tb=True, name=f"{tag}_dh_gate")
    dh = _mm(duv, wup_v, tb=True, res=dh, name=f"{tag}_dh_val")
    dx, dxb, dnorm = _rms_bwd(x, norm_w, dh, dout, f"{tag}_norm_bwd")
    return dx, dxb, (jnp.concatenate([dwg, dwv], axis=1), jnp.concatenate([dcg, dcv], axis=1), dwdown, dnorm)


def kernel(x, a_norm, ssm_w_in, ssm_conv_w, ssm_conv_b, ssm_dt_bias, ssm_a_log, ssm_d, ssm_norm, ssm_w_out, kv_norm, w_kv, b_norm, att_w_q, att_w_o, ffn_norm, ffn_w_up, ffn_conv_w, ffn_w_down, final_norm, loss_target, m_a_norm, m_ssm_w_in, m_ssm_conv_w, m_ssm_conv_b, m_ssm_dt_bias, m_ssm_a_log, m_ssm_d, m_ssm_norm, m_ssm_w_out, m_kv_norm, m_w_kv, m_b_norm, m_att_w_q, m_att_w_o, m_ffn_norm, m_ffn_w_up, m_ffn_conv_w, m_ffn_w_down, m_final_norm, v_a_norm, v_ssm_w_in, v_ssm_conv_w, v_ssm_conv_b, v_ssm_dt_bias, v_ssm_a_log, v_ssm_d, v_ssm_norm, v_ssm_w_out, v_kv_norm, v_w_kv, v_b_norm, v_att_w_q, v_att_w_o, v_ffn_norm, v_ffn_w_up, v_ffn_conv_w, v_ffn_w_down, v_final_norm):
    given = dict(locals())
    xs, tgt = x[0], loss_target[0]
    s, d = xs.shape
    di = ssm_w_out.shape[1] * NDEV
    nh = ssm_dt_bias.shape[1]
    ng = SSM_N_GROUPS
    convd = di + 2 * ng * SSM_D_STATE
    f = ffn_w_down.shape[1] * NDEV
    n_att = len(ATT_PATTERNS)
    qg = ATT_HEADS_PER_GROUP * ATT_HEAD_DIM
    kg = ATT_KV_HEADS_PER_GROUP * ATT_HEAD_DIM
    kvd = n_att * kg
    assert all(w // dil == ATT_BLOCK for w, dil in ATT_PATTERNS)

    small, _ = _pack([a_norm, ssm_conv_w, ssm_conv_b, ssm_norm, ffn_conv_w])
    big = [ssm_w_in[0], ssm_w_out[0], w_kv, att_w_q[0], att_w_o[0], ffn_w_up, ffn_w_down]
    gat = _all_gather([b.astype(BF16) for b in big] + [small], "gather_weights")
    w_in = _slabs_to_cols(gat[0])
    w_z, w_xbc = w_in[:, :di], w_in[:, di:di + convd]
    w_dt = jnp.pad(w_in[:, di + convd:], ((0, 0), (0, LANES - nh)))
    w_out = _slabs_to_rows(gat[1])
    w_kvf = _slabs_to_cols(gat[2])
    w_q = _slabs_to_cols(gat[3])
    w_o = _slabs_to_rows(gat[4])
    w_up = _slabs_to_cols(gat[5])
    w_up_g, w_up_v = w_up[:, :, :f], w_up[:, :, f:]
    w_down = _slabs_to_rows(gat[6])
    sm = gat[7].reshape(NDEV, -1)
    o0 = 0

    def take(shape):
        nonlocal o0
        n = math.prod(shape)
        out = sm[:, o0:o0 + n].reshape((NDEV,) + shape)
        o0 += n + (-n) % LANES
        return out
    a_norm_f = _slabs_to_cols(take(a_norm.shape))
    conv_w_f = _slabs_to_cols(take(ssm_conv_w.shape))[0]
    conv_b_f = _slabs_to_cols(take(ssm_conv_b.shape))
    ssm_norm_f = _slabs_to_cols(take(ssm_norm.shape))
    fcw = _slabs_to_cols(take(ffn_conv_w.shape))
    fcw_g, fcw_v = fcw[:, :, :f], fcw[:, :, f:]
    dtb, alog, dsk = _pad_lanes(ssm_dt_bias), _pad_lanes(ssm_a_log), _pad_lanes(ssm_d)
    kvn, fin = kv_norm.reshape(1, d), final_norm.reshape(1, d)

    h0 = _rms_fwd(xs, a_norm_f, "a_norm")
    z = _mm(h0, w_z, name="in_z")
    xbc_pre = _mm(h0, w_xbc, name="in_xbc")
    dtr = _mm(h0, w_dt, name="in_dt")
    xbc = _conv_silu_fwd(xbc_pre, conv_w_f, conv_b_f, "ssm_conv")
    dt = _softplus_fwd(dtr, dtb, "ssm_dt")
    y, prevs = _ssd_fwd(xbc, dt, alog, dsk, di, nh, ng, "ssd")
    yn = _gnorm_fwd(y, z, ssm_norm_f, ng, "ssm_gnorm")
    x1 = _mm(yn, w_out, res=xs, name="ssm_out")
    x2, ffn0 = _ffn_forward(x1, ffn_norm[0:1], w_up_g[0], w_up_v[0], fcw_g[0], fcw_v[0], w_down[0], "ffn0")
    hk = _rms_fwd(x2, kvn, "kv_norm")
    kv = _mm(hk, w_kvf, name="kv_proj")
    h2 = _rms_fwd(x2, b_norm, "b_norm")
    q = _mm(h2, w_q, name="q_proj")
    att = []
    for g, (window, dil) in enumerate(ATT_PATTERNS):
        tabs = _rot_tables(s, dil)
        qp = _perm(q[:, g * qg:(g + 1) * qg], dil)
        kp = _perm(kv[:, g * kg:(g + 1) * kg], dil)
        vp = _perm(kv[:, kvd + g * kg:kvd + (g + 1) * kg], dil)
        n_blk = s // dil // ATT_BLOCK
        og, lg = _attn_fwd(qp, kp, vp, tabs, n_blk, f"attn{g}")
        att.append((qp, kp, vp, tabs, n_blk, dil, _unperm(og, dil), _unperm(lg, dil)))
    o, ob, lse = _merge_fwd([t[6] for t in att], [t[7] for t in att], "attn_merge")
    x3 = _mm(ob, w_o, res=x2, name="attn_out")
    x4, ffn1 = _ffn_forward(x3, ffn_norm[1:2], w_up_g[1], w_up_v[1], fcw_g[1], fcw_v[1], w_down[1], "ffn1")
    loss_part, dx4, dx4b, dfin = _final_loss(x4, fin, tgt, "loss_head")

    dx3, dx3b, (dwup1, dfc1, dwdown1, dfn1) = _ffn_backward(
        x3, ffn1, dx4, dx4b, ffn_norm[1:2], w_up_g[1], w_up_v[1], fcw_g[1], fcw_v[1], w_down[1], "ffn1")
    dw_o = _mm(ob, dx3b, ta=True, name="attn_dwo")
    do = _mm(dx3b, w_o, tb=True, name="attn_do")
    delta = _delta(do, o, "attn_delta")
    dqs, dks, dvs = [], [], []
    for g, (qp, kp, vp, tabs, n_blk, dil, _, _) in enumerate(att):
        dqp, dkp, dvp = _attn_bwd(qp, kp, vp, _perm(do, dil), _perm(lse, dil), _perm(delta, dil), tabs, n_blk,
                                  f"attn{g}_bwd")
        dqs.append(_unperm(dqp, dil))
        dks.append(_unperm(dkp, dil))
        dvs.append(_unperm(dvp, dil))
    dq = jnp.concatenate(dqs, axis=1)
    dkv = jnp.concatenate(dks + dvs, axis=1)
    dw_q = _mm(h2, dq, ta=True, name="q_dw")
    dh2 = _mm(dq, w_q, tb=True, name="q_dh")
    dw_kv = _mm(hk, dkv, ta=True, name="kv_dw")
    dhk = _mm(dkv, w_kvf, tb=True, name="kv_dh")
    dx2, _, db_norm = _rms_bwd(x2, b_norm, dh2, dx3, "b_norm_bwd")
    dx2, dx2b, dkv_norm = _rms_bwd(x2, kvn, dhk, dx2, "kv_norm_bwd")
    dx1, dx1b, (dwup0, dfc0, dwdown0, dfn0) = _ffn_backward(
        x1, ffn0, dx2, dx2b, ffn_norm[0:1], w_up_g[0], w_up_v[0], fcw_g[0], fcw_v[0], w_down[0], "ffn0")
    dw_out = _mm(yn, dx1b, ta=True, name="ssm_dwout")
    dyn = _mm(dx1b, w_out, tb=True, name="ssm_dyn")
    dy, dz, dssm_norm = _gnorm_bwd(dyn, y, z, ssm_norm_f, ng, "ssm_gnorm_bwd")
    dxbc, ddt, dalog, ddsk = _ssd_bwd(xbc, dt, alog, dsk, prevs, dy, di, nh, ng, "ssd_bwd")
    ddtr, ddtb = _softplus_bwd(ddt, dtr, dtb, nh, "ssm_dt_bwd")
    dxbc_pre, dconv_w, dconv_b = _conv_silu_bwd(xbc_pre, conv_w_f, conv_b_f, dxbc, "ssm_conv_bwd")
    dw_z = _mm(h0, dz, ta=True, name="in_dwz")
    dw_xbc = _mm(h0, dxbc_pre, ta=True, name="in_dwxbc")
    dw_dt = _mm(h0, ddtr, ta=True, name="in_dwdt")[:, :nh]
    dh0 = _mm(dz, w_z, tb=True, name="in_dh_z")
    dh0 = _mm(dxbc_pre, w_xbc, tb=True, res=dh0, name="in_dh_xbc")
    dh0 = _mm(ddtr, w_dt, tb=True, res=dh0, name="in_dh_dt")
    dx0, _, da_norm = _rms_bwd(xs, a_norm_f, dh0, dx1, "a_norm_bwd")

    full = {
        'ssm_w_in': jnp.concatenate([dw_z, dw_xbc, dw_dt], axis=1)[None], 'ssm_w_out': dw_out[None], 'w_kv': dw_kv,
        'att_w_q': dw_q[None], 'att_w_o': dw_o[None], 'ffn_w_up': jnp.stack([dwup0, dwup1]),
        'ffn_w_down': jnp.stack([dwdown0, dwdown1]),
    }
    by_cols = {'ssm_w_in', 'w_kv', 'att_w_q', 'ffn_w_up'}
    big_names = list(full)
    slabs = [(_cols_to_slabs if n in by_cols else _rows_to_slabs)(full[n]).astype(BF16) for n in big_names]
    small_full = {
        'a_norm': da_norm, 'ssm_conv_w': dconv_w[None], 'ssm_conv_b': dconv_b, 'ssm_dt_bias': ddtb[:, :nh],
        'ssm_a_log': dalog[:, :nh], 'ssm_d': ddsk[:, :nh], 'ssm_norm': dssm_norm, 'kv_norm': dkv_norm.reshape(d),
        'b_norm': db_norm, 'ffn_norm': jnp.concatenate([dfn0, dfn1], axis=0), 'ffn_conv_w': jnp.stack([dfc0, dfc1]),
        'final_norm': dfin.reshape(d),
    }
    small_names = list(small_full)
    packed, spans = _pack([small_full[n] for n in small_names])
    recv = _exchange(slabs, [packed], "exchange_grads")
    small_sum = _sum_slabs(recv[-1], "sum_small_grads").reshape(-1)

    me = _dev_index(*_coords())
    res = {}
    for n, r in zip(big_names, recv[:-1]):
        w = given[n]
        c = w.shape[-1]
        outs = _adamw(r.reshape(NDEV, -1, c), w.reshape(-1, c), given['m_' + n].reshape(-1, c),
                      given['v_' + n].reshape(-1, c), f"adamw_{n}")
        res[n] = [o_.reshape(w.shape) for o_ in outs]
    sharded_small = {'a_norm', 'ssm_conv_w', 'ssm_conv_b', 'ssm_norm', 'ffn_conv_w'}
    for n, (off, size) in zip(small_names, spans):
        w = given[n]
        gfull = small_sum[off:off + size].reshape(small_full[n].shape)
        if n in sharded_small:
            c = w.shape[-1]
            gfull = lax.dynamic_slice_in_dim(gfull, me * c, c, axis=gfull.ndim - 1)
        c = w.shape[-1]
        outs = _adamw(gfull.reshape(1, -1, c), w.reshape(-1, c), given['m_' + n].reshape(-1, c),
                      given['v_' + n].reshape(-1, c), f"adamw_{n}")
        res[n] = [o_.reshape(w.shape) for o_ in outs]

    loss = lax.psum(loss_part[0, 0], AXES)
    return (loss, dx0[None], *[res[n][0] for n in WEIGHTS], *[res[n][1] for n in WEIGHTS],
            *[res[n][2] for n in WEIGHTS], *[res[n][3] for n in WEIGHTS])
```

```python
import functools
import math

import jax
import jax.numpy as jnp
from jax import lax
from jax.experimental import pallas as pl
from jax.experimental.pallas import tpu as pltpu

F32, BF16 = jnp.float32, jnp.bfloat16
AXES = ("x", "y", "c")
NDEV = 8
MESH = pl.DeviceIdType.MESH
HIGHEST = lax.Precision.HIGHEST

LANES = 128
VMEM_LIMIT_BYTES = 48 * 1024 * 1024

RMS_EPS = 1e-6
GATED_NORM_EPS = 1e-5
SSM_HEAD_DIM = 64
SSM_N_GROUPS = 8
SSM_D_STATE = 128
SSM_CONV = 4
SSM_CHUNK = 128
ATT_PATTERNS = ((128, 1), (512, 4), (2048, 16))
ATT_HEAD_DIM = 128
ATT_HEADS_PER_GROUP = 8
ATT_KV_HEADS_PER_GROUP = 2
ATT_BLOCK = 128
ROPE_DIM = ATT_HEAD_DIM // 4
ROPE_THETA = 500000.0
FFN_CONV = 3
ADAM_LR = 0.001
ADAM_B1 = 0.9
ADAM_B2 = 0.999
ADAM_EPS = 1e-08
ADAM_WD = 0.01
ADAM_STEP = 10
NEG = -1e30

WEIGHTS = ['a_norm', 'ssm_w_in', 'ssm_conv_w', 'ssm_conv_b', 'ssm_dt_bias', 'ssm_a_log', 'ssm_d', 'ssm_norm',
           'ssm_w_out', 'kv_norm', 'w_kv', 'b_norm', 'att_w_q', 'att_w_o', 'ffn_norm', 'ffn_w_up', 'ffn_conv_w',
           'ffn_w_down', 'final_norm']


def _params(sem=None):
    kw = dict(vmem_limit_bytes=VMEM_LIMIT_BYTES)
    if sem is not None:
        kw["dimension_semantics"] = sem
    return pltpu.CompilerParams(**kw)


def _pick(n, pref):
    if n <= pref:
        return n
    t = (pref // LANES) * LANES
    while t >= LANES:
        if n % t == 0:
            return t
        t -= LANES
    return n


def _dot(a, b, dims=(((1,), (0,)), ((), ())), precision=None):
    return lax.dot_general(a, b, dims, precision=precision, preferred_element_type=F32)


_NT = (((1,), (1,)), ((), ()))
_TN = (((0,), (0,)), ((), ()))


def _mm(a, b, *, ta=False, tb=False, res=None, out_dtype=F32, name, tm=1024, tn=1408, tk=2048):
    m = a.shape[1] if ta else a.shape[0]
    k = a.shape[0] if ta else a.shape[1]
    n = b.shape[0] if tb else b.shape[1]
    assert k == (b.shape[1] if tb else b.shape[0])
    tm, tn, tk = _pick(m, tm), _pick(n, tn), _pick(k, tk)
    nk = k // tk
    a_spec = pl.BlockSpec((tk, tm), lambda i, j, l: (l, i)) if ta else pl.BlockSpec((tm, tk), lambda i, j, l: (i, l))
    b_spec = pl.BlockSpec((tn, tk), lambda i, j, l: (j, l)) if tb else pl.BlockSpec((tk, tn), lambda i, j, l: (l, j))
    o_spec = pl.BlockSpec((tm, tn), lambda i, j, l: (i, j))
    dims = (((0 if ta else 1,), (1 if tb else 0,)), ((), ()))
    has_res = res is not None

    def body(*refs):
        a_ref, b_ref = refs[:2]
        r_ref = refs[2] if has_res else None
        o_ref = refs[2 + has_res]
        p = _dot(a_ref[...].astype(BF16), b_ref[...].astype(BF16), dims)

        def finish(r):
            if has_res:
                r = r + r_ref[...]
            o_ref[...] = r.astype(o_ref.dtype)

        if nk == 1:
            finish(p)
            return
        acc = refs[3 + has_res]
        l = pl.program_id(2)

        @pl.when(l == 0)
        def _():
            acc[...] = p

        @pl.when(jnp.logical_and(l > 0, l < nk - 1))
        def _():
            acc[...] += p

        @pl.when(l == nk - 1)
        def _():
            finish(acc[...] + p)

    ins = [a, b] + ([res] if has_res else [])
    in_specs = [a_spec, b_spec] + ([o_spec] if has_res else [])
    return pl.pallas_call(
        body, name=name, grid=(m // tm, n // tn, nk), in_specs=in_specs, out_specs=o_spec,
        out_shape=jax.ShapeDtypeStruct((m, n), out_dtype),
        scratch_shapes=[pltpu.VMEM((tm, tn), F32)] if nk > 1 else [],
        compiler_params=_params(("parallel", "parallel", "arbitrary")))(*ins)


def _rowwise(fn, rows, bcasts, outs, accs=(), *, tile, name):
    s = rows[0].shape[0]
    tile = min(tile, s)
    n_in, n_out, n_acc = len(rows) + len(bcasts), len(outs), len(accs)

    def body(*refs):
        vals = fn(*[r[...] for r in refs[:n_in]])
        o_refs = refs[n_in:n_in + n_out]
        a_refs = refs[n_in + n_out:]
        for r, v in zip(o_refs, vals[:n_out]):
            r[...] = v.astype(r.dtype)

        @pl.when(pl.program_id(0) == 0)
        def _():
            for r in a_refs:
                r[...] = jnp.zeros(r.shape, r.dtype)

        for r, v in zip(a_refs, vals[n_out:]):
            r[...] += v

    in_specs = [pl.BlockSpec((tile, r.shape[1]), lambda i: (i, 0)) for r in rows]
    in_specs += [pl.BlockSpec(b.shape, lambda i: (0, 0)) for b in bcasts]
    out_specs = [pl.BlockSpec((tile, c), lambda i: (i, 0)) for c, _ in outs]
    out_specs += [pl.BlockSpec(sh, lambda i: (0, 0)) for sh, _ in accs]
    out_shape = [jax.ShapeDtypeStruct((s, c), dt) for c, dt in outs]
    out_shape += [jax.ShapeDtypeStruct(sh, dt) for sh, dt in accs]
    return pl.pallas_call(body, name=name, grid=(s // tile,), in_specs=in_specs, out_specs=out_specs,
                          out_shape=out_shape, compiler_params=_params(("arbitrary",)))(*rows, *bcasts)


def _rms_fwd(x, w, name):
    def fn(x, w):
        r = lax.rsqrt(jnp.mean(x * x, axis=-1, keepdims=True) + RMS_EPS)
        return (x * r * w,)
    return _rowwise(fn, [x], [w], [(x.shape[1], BF16)], tile=256, name=name)[0]


def _rms_bwd(x, w, dh, dres, name):
    def fn(x, dh, dres, w):
        r = lax.rsqrt(jnp.mean(x * x, axis=-1, keepdims=True) + RMS_EPS)
        xh = x * r
        dxh = dh * w
        dx = dres + r * (dxh - xh * jnp.mean(dxh * xh, axis=-1, keepdims=True))
        return dx, dx, jnp.sum(dh * xh, axis=0, keepdims=True)
    d = x.shape[1]
    return _rowwise(fn, [x, dh, dres], [w], [(d, F32), (d, BF16)], [((1, d), F32)], tile=256, name=name)


def _final_loss(x, w, tgt, name):
    d = x.shape[1]

    def fn(x, t, w):
        r = lax.rsqrt(jnp.mean(x * x, axis=-1, keepdims=True) + RMS_EPS)
        xh = x * r
        err = xh * w - t
        part = jnp.sum(jnp.mean(err * err, axis=-1, keepdims=True), axis=0, keepdims=True) * 0.5
        dy = err * (1.0 / d)
        dxh = dy * w
        dx = r * (dxh - xh * jnp.mean(dxh * xh, axis=-1, keepdims=True))
        return dx, dx, part, jnp.sum(dy * xh, axis=0, keepdims=True)
    dx, dxb, part, dw = _rowwise(fn, [x, tgt], [w], [(d, F32), (d, BF16)], [((1, 1), F32), ((1, d), F32)],
                                 tile=256, name=name)
    return part, dx, dxb, dw


def _softplus_fwd(dtr, bias, name):
    def fn(r, b):
        v = r + b
        return (jnp.maximum(v, 0.0) + jnp.log(1.0 + jnp.exp(-jnp.abs(v))),)
    return _rowwise(fn, [dtr], [bias], [(LANES, F32)], tile=512, name=name)[0]


def _softplus_bwd(ddt, dtr, bias, n_heads, name):
    def fn(g, r, b):
        lane = lax.broadcasted_iota(jnp.int32, g.shape, 1)
        d = jnp.where(lane < n_heads, g * jax.nn.sigmoid(r + b), 0.0)
        return d, jnp.sum(d, axis=0, keepdims=True)
    return _rowwise(fn, [ddt, dtr], [bias], [(LANES, BF16)], [((1, LANES), F32)], tile=512, name=name)


def _gnorm_fwd(y, z, w, n_groups, name):
    di = y.shape[1]
    gs = di // n_groups

    def fn(y, z, w):
        y2 = y * (z * jax.nn.sigmoid(z))
        out = []
        for g in range(n_groups):
            sl = y2[:, g * gs:(g + 1) * gs]
            r = lax.rsqrt(jnp.mean(sl * sl, axis=-1, keepdims=True) + GATED_NORM_EPS)
            out.append(sl * r)
        return (jnp.concatenate(out, axis=1) * w,)
    return _rowwise(fn, [y, z], [w], [(di, BF16)], tile=256, name=name)[0]


def _gnorm_bwd(dyn, y, z, w, n_groups, name):
    di = y.shape[1]
    gs = di // n_groups

    def fn(dyn, y, z, w):
        sig = jax.nn.sigmoid(z)
        sz = z * sig
        y2 = y * sz
        d2n = dyn * w
        dy2, yhat = [], []
        for g in range(n_groups):
            sl = y2[:, g * gs:(g + 1) * gs]
            dg = d2n[:, g * gs:(g + 1) * gs]
            r = lax.rsqrt(jnp.mean(sl * sl, axis=-1, keepdims=True) + GATED_NORM_EPS)
            yh = sl * r
            dy2.append(r * (dg - yh * jnp.mean(dg * yh, axis=-1, keepdims=True)))
            yhat.append(yh)
        dy2 = jnp.concatenate(dy2, axis=1)
        yhat = jnp.concatenate(yhat, axis=1)
        dz = dy2 * y * (sig * (1.0 + z * (1.0 - sig)))
        return dy2 * sz, dz, jnp.sum(dyn * yhat, axis=0, keepdims=True)
    return _rowwise(fn, [dyn, y, z], [w], [(di, F32), (di, BF16)], [((1, di), F32)], tile=128, name=name)


def _merge_fwd(os_, lses, name):
    n = len(os_)

    def fn(*v):
        o, l = v[:n], v[n:]
        m = functools.reduce(jnp.maximum, l)
        e = [jnp.exp(li - m) for li in l]
        tot = functools.reduce(jnp.add, e)
        acc = functools.reduce(jnp.add, [ei * oi for ei, oi in zip(e, o)]) / tot
        return acc, acc, m + jnp.log(tot)
    c = os_[0].shape[1]
    return _rowwise(fn, list(os_) + list(lses), [], [(c, F32), (c, BF16), (c, F32)], tile=256, name=name)


def _delta(do, o, name):
    c = o.shape[1]

    def fn(do, o):
        p = do * o
        out = [jnp.broadcast_to(jnp.sum(p[:, j:j + ATT_HEAD_DIM], axis=-1, keepdims=True), (p.shape[0], ATT_HEAD_DIM))
               for j in range(0, c, ATT_HEAD_DIM)]
        return (jnp.concatenate(out, axis=1),)
    return _rowwise(fn, [do, o], [], [(c, F32)], tile=256, name=name)[0]


def _sum_slabs(recv, name):
    def body(r_ref, o_ref):
        acc = r_ref[0]
        for k in range(1, NDEV):
            acc = acc + r_ref[k]
        o_ref[...] = acc
    return pl.pallas_call(body, name=name, out_shape=jax.ShapeDtypeStruct(recv.shape[1:], F32),
                          compiler_params=_params())(recv)


def _shift_down(x, k):
    if k == 0:
        return x
    row = lax.broadcasted_iota(jnp.int32, x.shape, 0)
    return jnp.where(row >= k, pltpu.roll(x, k, 0), 0.0)


def _shift_up(x, k):
    if k == 0:
        return x
    s = x.shape[0]
    row = lax.broadcasted_iota(jnp.int32, x.shape, 0)
    return jnp.where(row < s - k, pltpu.roll(x, s - k, 0), 0.0)


def _conv(x, w):
    kw = w.shape[0]
    return functools.reduce(jnp.add, [w[k:k + 1, :] * _shift_down(x, kw - 1 - k) for k in range(kw)])


def _conv_t(dy, w):
    kw = w.shape[0]
    return functools.reduce(jnp.add, [w[k:k + 1, :] * _shift_up(dy, kw - 1 - k) for k in range(kw)])


def _conv_dw(x, dy, dw_ref):
    kw = dw_ref.shape[0]
    for k in range(kw):
        dw_ref[k:k + 1, :] = jnp.sum(dy * _shift_down(x, kw - 1 - k), axis=0, keepdims=True)


def _dsilu(pre):
    sig = jax.nn.sigmoid(pre)
    return sig * (1.0 + pre * (1.0 - sig))


def _col_specs(s, c, kw, tc):
    return (pl.BlockSpec((s, tc), lambda j: (0, j)), pl.BlockSpec((kw, tc), lambda j: (0, j)),
            pl.BlockSpec((1, tc), lambda j: (0, j)))


def _conv_silu_fwd(x, w, b, name):
    s, c = x.shape
    tc = LANES
    xs, ws, bs = _col_specs(s, c, w.shape[0], tc)

    def body(x_ref, w_ref, b_ref, o_ref):
        pre = _conv(x_ref[...], w_ref[...]) + b_ref[...]
        o_ref[...] = pre * jax.nn.sigmoid(pre)
    return pl.pallas_call(body, name=name, grid=(c // tc,), in_specs=[xs, ws, bs], out_specs=xs,
                          out_shape=jax.ShapeDtypeStruct((s, c), F32), compiler_params=_params(("parallel",)))(x, w, b)


def _conv_silu_bwd(x, w, b, dy, name):
    s, c = x.shape
    tc = LANES
    xs, ws, bs = _col_specs(s, c, w.shape[0], tc)

    def body(x_ref, w_ref, b_ref, dy_ref, dx_ref, dw_ref, db_ref):
        xv, wv = x_ref[...], w_ref[...]
        pre = _conv(xv, wv) + b_ref[...]
        dpre = dy_ref[...] * _dsilu(pre)
        dx_ref[...] = _conv_t(dpre, wv).astype(dx_ref.dtype)
        _conv_dw(xv, dpre, dw_ref)
        db_ref[...] = jnp.sum(dpre, axis=0, keepdims=True)
    return pl.pallas_call(
        body, name=name, grid=(c // tc,), in_specs=[xs, ws, bs, xs], out_specs=[xs, ws, bs],
        out_shape=[jax.ShapeDtypeStruct((s, c), BF16), jax.ShapeDtypeStruct(w.shape, F32),
                   jax.ShapeDtypeStruct((1, c), F32)],
        compiler_params=_params(("parallel",)))(x, w, b, dy)


def _ffn_gate_fwd(ug, uv, wg, wv, name):
    s, c = ug.shape
    tc = LANES
    xs, ws, _ = _col_specs(s, c, wg.shape[0], tc)

    def body(g_ref, v_ref, wg_ref, wv_ref, o_ref):
        g = _conv(g_ref[...], wg_ref[...])
        v = _conv(v_ref[...], wv_ref[...])
        o_ref[...] = (g * jax.nn.sigmoid(g) * v).astype(o_ref.dtype)
    return pl.pallas_call(body, name=name, grid=(c // tc,), in_specs=[xs, xs, ws, ws], out_specs=xs,
                          out_shape=jax.ShapeDtypeStruct((s, c), BF16),
                          compiler_params=_params(("parallel",)))(ug, uv, wg, wv)


def _ffn_gate_bwd(ug, uv, wg, wv, df, name):
    s, c = ug.shape
    tc = LANES
    xs, ws, _ = _col_specs(s, c, wg.shape[0], tc)

    def body(g_ref, v_ref, wg_ref, wv_ref, df_ref, dg_ref, dv_ref, dwg_ref, dwv_ref):
        gp, vp, wgv, wvv = g_ref[...], v_ref[...], wg_ref[...], wv_ref[...]
        g = _conv(gp, wgv)
        v = _conv(vp, wvv)
        dfv = df_ref[...]
        dg = dfv * v * _dsilu(g)
        dv = dfv * (g * jax.nn.sigmoid(g))
        dg_ref[...] = _conv_t(dg, wgv).astype(dg_ref.dtype)
        dv_ref[...] = _conv_t(dv, wvv).astype(dv_ref.dtype)
        _conv_dw(gp, dg, dwg_ref)
        _conv_dw(vp, dv, dwv_ref)
    return pl.pallas_call(
        body, name=name, grid=(c // tc,), in_specs=[xs, xs, ws, ws, xs], out_specs=[xs, xs, ws, ws],
        out_shape=[jax.ShapeDtypeStruct((s, c), BF16), jax.ShapeDtypeStruct((s, c), BF16),
                   jax.ShapeDtypeStruct(wg.shape, F32), jax.ShapeDtypeStruct(wv.shape, F32)],
        compiler_params=_params(("parallel",)))(ug, uv, wg, wv, df)


def _ssd_common(dt, alog, n_heads):
    ln = dt.shape[0]
    lane = lax.broadcasted_iota(jnp.int32, (1, LANES), 1)
    a = jnp.where(lane < n_heads, -jnp.exp(alog), 0.0)
    row = lax.broadcasted_iota(jnp.int32, (ln, ln), 0)
    col = lax.broadcasted_iota(jnp.int32, (ln, ln), 1)
    tril = col <= row
    acs = _dot(tril.astype(F32), dt * a, precision=HIGHEST)
    return a, acs, acs.T, tril


def _ssd_fwd(xbc, dt, alog, dskip, di, n_heads, n_groups, name):
    s, convd = xbc.shape
    ln, p, ns = SSM_CHUNK, SSM_HEAD_DIM, SSM_D_STATE
    nc, hg = s // ln, n_heads // n_groups

    def body(x_ref, dt_ref, alog_ref, d_ref, y_ref, prev_ref, st):
        @pl.when(pl.program_id(0) == 0)
        def _():
            st[...] = jnp.zeros(st.shape, F32)

        dt = dt_ref[...]
        _, acs, acs_t, tril = _ssd_common(dt, alog_ref[...], n_heads)
        e_all = jnp.exp(acs)
        last = acs[ln - 1:ln, :]
        ds_all = jnp.exp(last - acs)
        t_all = jnp.exp(last)
        dsk = d_ref[...]
        for g in range(n_groups):
            bg = x_ref[:, di + g * ns:di + (g + 1) * ns].astype(BF16)
            cg = x_ref[:, di + (n_groups + g) * ns:di + (n_groups + g + 1) * ns].astype(BF16)
            gm = _dot(cg, bg, _NT)
            for j in range(hg):
                h = g * hg + j
                xh = x_ref[:, h * p:(h + 1) * p]
                xdt = xh * dt[:, h:h + 1]
                seg = acs[:, h:h + 1] - acs_t[h:h + 1, :]
                m = jnp.where(tril, gm * jnp.exp(jnp.where(tril, seg, 0.0)), 0.0)
                prev = st[h]
                prev_ref[0, h] = prev
                y = _dot(m.astype(BF16), xdt.astype(BF16))
                y = y + _dot(cg, prev.astype(BF16), _NT) * e_all[:, h:h + 1]
                y = y + xh * dsk[:, h:h + 1]
                snew = _dot((xdt * ds_all[:, h:h + 1]).astype(BF16), bg, _TN)
                st[h] = prev * t_all[:, h:h + 1] + snew
                y_ref[:, h * p:(h + 1) * p] = y

    vec = pl.BlockSpec((1, LANES), lambda c: (0, 0))
    return pl.pallas_call(
        body, name=name, grid=(nc,),
        in_specs=[pl.BlockSpec((ln, convd), lambda c: (c, 0)), pl.BlockSpec((ln, LANES), lambda c: (c, 0)), vec, vec],
        out_specs=[pl.BlockSpec((ln, di), lambda c: (c, 0)),
                   pl.BlockSpec((1, n_heads, p, ns), lambda c: (c, 0, 0, 0))],
        out_shape=[jax.ShapeDtypeStruct((s, di), F32), jax.ShapeDtypeStruct((nc, n_heads, p, ns), F32)],
        scratch_shapes=[pltpu.VMEM((n_heads, p, ns), F32)],
        compiler_params=_params(("arbitrary",)))(xbc, dt, alog, dskip)


def _ssd_bwd(xbc, dt, alog, dskip, prev_all, dy, di, n_heads, n_groups, name):
    s, convd = xbc.shape
    ln, p, ns = SSM_CHUNK, SSM_HEAD_DIM, SSM_D_STATE
    nc, hg = s // ln, n_heads // n_groups

    def body(x_ref, dt_ref, alog_ref, d_ref, prev_ref, dy_ref, dx_ref, ddt_ref, da_ref, dd_ref, dh):
        step = pl.program_id(0)

        @pl.when(step == 0)
        def _():
            dh[...] = jnp.zeros(dh.shape, F32)
            da_ref[...] = jnp.zeros(da_ref.shape, F32)
            dd_ref[...] = jnp.zeros(dd_ref.shape, F32)

        dt = dt_ref[...]
        a, acs, acs_t, tril = _ssd_common(dt, alog_ref[...], n_heads)
        e_all = jnp.exp(acs)
        last = acs[ln - 1:ln, :]
        ds_all = jnp.exp(last - acs)
        t_all = jnp.exp(last)
        dsk = d_ref[...]
        lane = lax.broadcasted_iota(jnp.int32, (ln, LANES), 1)
        lane1 = lax.broadcasted_iota(jnp.int32, (1, LANES), 1)
        sub = lax.broadcasted_iota(jnp.int32, (LANES, ln), 0)
        rowi = lax.broadcasted_iota(jnp.int32, (ln, LANES), 0)
        dacs_c = jnp.zeros((ln, LANES), F32)
        dacs_r = jnp.zeros((LANES, ln), F32)
        dlast = jnp.zeros((1, LANES), F32)
        ddt_x = jnp.zeros((ln, LANES), F32)
        dd = jnp.zeros((1, LANES), F32)

        def tot(v):
            return jnp.sum(jnp.sum(v, axis=1, keepdims=True), axis=0, keepdims=True)

        for g in range(n_groups):
            bg = x_ref[:, di + g * ns:di + (g + 1) * ns].astype(BF16)
            cg = x_ref[:, di + (n_groups + g) * ns:di + (n_groups + g + 1) * ns].astype(BF16)
            gm = _dot(cg, bg, _NT)
            dgm = jnp.zeros((ln, ln), F32)
            dcg = jnp.zeros((ln, ns), F32)
            dbg = jnp.zeros((ln, ns), F32)
            for j in range(hg):
                h = g * hg + j
                xh = x_ref[:, h * p:(h + 1) * p]
                dth = dt[:, h:h + 1]
                xdt = xh * dth
                dyh = dy_ref[:, h * p:(h + 1) * p]
                eh, dsh, th = e_all[:, h:h + 1], ds_all[:, h:h + 1], t_all[:, h:h + 1]
                seg = acs[:, h:h + 1] - acs_t[h:h + 1, :]
                dec = jnp.where(tril, jnp.exp(jnp.where(tril, seg, 0.0)), 0.0)
                m = gm * dec
                prev = prev_ref[0, h]
                dhn = dh[h]
                prevb, dhb, dyb, xdtb = prev.astype(BF16), dhn.astype(BF16), dyh.astype(BF16), xdt.astype(BF16)
                yo = _dot(cg, prevb, _NT)
                dyob = (dyh * eh).astype(BF16)
                c_col = jnp.sum(dyh * yo, axis=1, keepdims=True) * eh
                dcg = dcg + _dot(dyob, prevb)
                dprev = th * dhn + _dot(dyob, cg, _TN)
                dtt = tot(dhn * prev)
                w = _dot(bg, dhb, _NT)
                dxdt = w * dsh
                dds = jnp.sum(w * xdt, axis=1, keepdims=True)
                dbg = dbg + _dot((xdt * dsh).astype(BF16), dhb)
                dm = _dot(dyb, xdtb, _NT)
                dxdt = dxdt + _dot(m.astype(BF16), dyb, _TN)
                dgm = dgm + dm * dec
                q = dm * m
                c_col = c_col + jnp.sum(q, axis=1, keepdims=True) - dds * dsh
                r_row = -jnp.sum(q, axis=0, keepdims=True)
                dlast_h = tot(dds * dsh) + dtt * th
                dacs_c = dacs_c + jnp.where(lane == h, c_col, 0.0)
                dacs_r = dacs_r + jnp.where(sub == h, r_row, 0.0)
                dlast = dlast + jnp.where(lane1 == h, dlast_h, 0.0)
                ddt_x = ddt_x + jnp.where(lane == h, jnp.sum(dxdt * xh, axis=1, keepdims=True), 0.0)
                dd = dd + jnp.where(lane1 == h, tot(dyh * xh), 0.0)
                dx_ref[:, h * p:(h + 1) * p] = dxdt * dth + dyh * dsk[:, h:h + 1]
                dh[h] = dprev
            dgb = dgm.astype(BF16)
            dx_ref[:, di + g * ns:di + (g + 1) * ns] = dbg + _dot(dgb, cg, _TN)
            dx_ref[:, di + (n_groups + g) * ns:di + (n_groups + g + 1) * ns] = dcg + _dot(dgb, bg)

        dacs = dacs_c + dacs_r.T + jnp.where(rowi == ln - 1, dlast, 0.0)
        row = lax.broadcasted_iota(jnp.int32, (ln, ln), 0)
        col = lax.broadcasted_iota(jnp.int32, (ln, ln), 1)
        dadt = _dot((col >= row).astype(F32), dacs, precision=HIGHEST)
        ddt_ref[...] = dadt * a + ddt_x
        da_ref[...] += jnp.sum(dadt * dt, axis=0, keepdims=True)
        dd_ref[...] += dd

        @pl.when(step == nc - 1)
        def _():
            da_ref[...] = da_ref[...] * a

    vec = pl.BlockSpec((1, LANES), lambda c: (0, 0))
    rev = lambda c: (nc - 1 - c, 0)
    return pl.pallas_call(
        body, name=name, grid=(nc,),
        in_specs=[pl.BlockSpec((ln, convd), rev), pl.BlockSpec((ln, LANES), rev), vec, vec,
                  pl.BlockSpec((1, n_heads, p, ns), lambda c: (nc - 1 - c, 0, 0, 0)), pl.BlockSpec((ln, di), rev)],
        out_specs=[pl.BlockSpec((ln, convd), rev), pl.BlockSpec((ln, LANES), rev), vec, vec],
        out_shape=[jax.ShapeDtypeStruct((s, convd), F32), jax.ShapeDtypeStruct((s, LANES), F32),
                   jax.ShapeDtypeStruct((1, LANES), F32), jax.ShapeDtypeStruct((1, LANES), F32)],
        scratch_shapes=[pltpu.VMEM((n_heads, p, ns), F32)],
        compiler_params=_params(("arbitrary",)))(xbc, dt, alog, dskip, prev_all, dy)


def _split(x, n):
    out = []
    for _ in range(n):
        piece = x.astype(BF16)
        out.append(piece)
        x = x - piece.astype(F32)
    return out


def _spread(x, onehot, n=2):
    return functools.reduce(jnp.add, [_dot(piece, onehot) for piece in _split(x, n)])


def _head_maps(di, p):
    e = (jnp.arange(di, dtype=jnp.int32)[None, :] // p == jnp.arange(LANES, dtype=jnp.int32)[:, None]).astype(BF16)
    return e, e.T


def _ssd_wide(dt, acs, acs_t, dskip, e_ref, et_ref):
    ln = dt.shape[0]
    last = acs[ln - 1:ln, :]
    stack = jnp.concatenate([dt, jnp.exp(acs), jnp.exp(last - acs), jnp.broadcast_to(dskip, (8, LANES))], axis=0)
    wide = _spread(stack, e_ref[...])
    tb = jnp.exp(jnp.broadcast_to(acs_t[:, ln - 1:ln], (LANES, LANES)))
    texp = functools.reduce(jnp.add, [_dot(et_ref[...], piece) for piece in _split(tb, 3)])
    return wide[:ln], wide[ln:2 * ln], wide[2 * ln:3 * ln], wide[3 * ln:3 * ln + 1], texp


def _ssd_fwd2(xbc, dt, alog, dskip, di, n_heads, n_groups, name):
    s, convd = xbc.shape
    ln, p, ns = SSM_CHUNK, SSM_HEAD_DIM, SSM_D_STATE
    nc, hg = s // ln, n_heads // n_groups
    gw = hg * p
    e64, e64t = _head_maps(di, p)

    def body(x_ref, dt_ref, alog_ref, d_ref, e_ref, et_ref, y_ref, prev_ref, st):
        @pl.when(pl.program_id(0) == 0)
        def _():
            st[...] = jnp.zeros(st.shape, F32)

        dt = dt_ref[...]
        _, acs, acs_t, tril = _ssd_common(dt, alog_ref[...], n_heads)
        dte, ee, dse, dske, texp = _ssd_wide(dt, acs, acs_t, d_ref[...], e_ref, et_ref)
        x = x_ref[:, :di]
        xdt = x * dte
        xdtb = xdt.astype(BF16)
        xdsb = (xdt * dse).astype(BF16)
        for g in range(n_groups):
            rows = slice(g * gw, (g + 1) * gw)
            bg = x_ref[:, di + g * ns:di + (g + 1) * ns].astype(BF16)
            cg = x_ref[:, di + (n_groups + g) * ns:di + (n_groups + g + 1) * ns].astype(BF16)
            gm = _dot(cg, bg, _NT)
            prev = st[rows, :]
            prev_ref[0, rows, :] = prev
            yo = _dot(cg, prev.astype(BF16), _NT)
            for j in range(hg):
                h = g * hg + j
                seg = acs[:, h:h + 1] - acs_t[h:h + 1, :]
                m = jnp.where(tril, gm * jnp.exp(jnp.where(tril, seg, 0.0)), 0.0)
                y_ref[:, h * p:(h + 1) * p] = _dot(m.astype(BF16), xdtb[:, h * p:(h + 1) * p])
            y_ref[:, rows] = y_ref[:, rows] + yo * ee[:, rows] + x[:, rows] * dske[:, rows]
            st[rows, :] = prev * texp[rows, :] + _dot(xdsb[:, rows], bg, _TN)

    vec = pl.BlockSpec((1, LANES), lambda c: (0, 0))
    return pl.pallas_call(
        body, name=name, grid=(nc,),
        in_specs=[pl.BlockSpec((ln, convd), lambda c: (c, 0)), pl.BlockSpec((ln, LANES), lambda c: (c, 0)), vec, vec,
                  pl.BlockSpec(e64.shape, lambda c: (0, 0)), pl.BlockSpec(e64t.shape, lambda c: (0, 0))],
        out_specs=[pl.BlockSpec((ln, di), lambda c: (c, 0)), pl.BlockSpec((1, di, ns), lambda c: (c, 0, 0))],
        out_shape=[jax.ShapeDtypeStruct((s, di), F32), jax.ShapeDtypeStruct((nc, di, ns), F32)],
        scratch_shapes=[pltpu.VMEM((di, ns), F32)],
        compiler_params=_params(("arbitrary",)))(xbc, dt, alog, dskip, e64, e64t)


def _ssd_bwd2(xbc, dt, alog, dskip, prev_all, dy, di, n_heads, n_groups, name):
    s, convd = xbc.shape
    ln, p, ns = SSM_CHUNK, SSM_HEAD_DIM, SSM_D_STATE
    nc, hg = s // ln, n_heads // n_groups
    gw = hg * p
    e64, e64t = _head_maps(di, p)

    def body(x_ref, dt_ref, alog_ref, d_ref, e_ref, et_ref, prev_ref, dy_ref,
             dx_ref, ddt_ref, da_ref, dd_ref, dh, yo_ref, w_ref):
        step = pl.program_id(0)

        @pl.when(step == 0)
        def _():
            dh[...] = jnp.zeros(dh.shape, F32)
            da_ref[...] = jnp.zeros(da_ref.shape, F32)
            dd_ref[...] = jnp.zeros(dd_ref.shape, F32)

        dt = dt_ref[...]
        a, acs, acs_t, tril = _ssd_common(dt, alog_ref[...], n_heads)
        dte, ee, dse, dske, texp = _ssd_wide(dt, acs, acs_t, d_ref[...], e_ref, et_ref)
        row = lax.broadcasted_iota(jnp.int32, (ln, ln), 0)
        col = lax.broadcasted_iota(jnp.int32, (ln, ln), 1)
        triu = col >= row
        x = x_ref[:, :di]
        dy = dy_ref[...]
        xdt = x * dte
        xdtb = xdt.astype(BF16)
        xdsb = (xdt * dse).astype(BF16)
        dyb = dy.astype(BF16)
        dyob = (dy * ee).astype(BF16)
        dhn = dh[...]
        dhb = dhn.astype(BF16)
        per_head = functools.reduce(jnp.add, [_dot(e_ref[...], piece) for piece in _split(dhn * prev_ref[0], 2)])
        ones8 = jnp.ones((8, LANES), BF16)
        dtt = functools.reduce(jnp.add, [_dot(ones8, piece, _NT) for piece in _split(per_head, 2)])[0:1]
        dacs_c = jnp.zeros((ln, LANES), F32)
        dacs_r = jnp.zeros((LANES, ln), F32)
        for g in range(n_groups):
            rows = slice(g * gw, (g + 1) * gw)
            bg = x_ref[:, di + g * ns:di + (g + 1) * ns].astype(BF16)
            cg = x_ref[:, di + (n_groups + g) * ns:di + (n_groups + g + 1) * ns].astype(BF16)
            gmt = _dot(bg, cg, _NT)
            prevb = prev_ref[0, rows, :].astype(BF16)
            dcg = _dot(dyob[:, rows], prevb)
            dh[rows, :] = texp[rows, :] * dhn[rows, :] + _dot(dyob[:, rows], cg, _TN)
            w = _dot(bg, dhb[rows, :], _NT)
            dbg = _dot(xdsb[:, rows], dhb[rows, :])
            yo_ref[:, rows] = _dot(cg, prevb, _NT)
            w_ref[:, rows] = w
            dgmt = jnp.zeros((ln, ln), F32)
            q_hi, q_lo = [], []
            for j in range(hg):
                h = g * hg + j
                segt = acs_t[h:h + 1, :] - acs[:, h:h + 1]
                dect = jnp.where(triu, jnp.exp(jnp.where(triu, segt, 0.0)), 0.0)
                dyh, xh = dyb[:, h * p:(h + 1) * p], xdtb[:, h * p:(h + 1) * p]
                mt = gmt * dect
                dmt = _dot(xh, dyh, _NT)
                dx_ref[:, h * p:(h + 1) * p] = _dot(mt.astype(BF16), dyh)
                dgmt = dgmt + dmt * dect
                hi, lo = _split(dmt * mt, 2)
                q_hi.append(hi)
                q_lo.append(lo)
            sel_c = (lax.broadcasted_iota(jnp.int32, (hg * ln, LANES), 1)
                     == g * hg + lax.broadcasted_iota(jnp.int32, (hg * ln, LANES), 0) // ln).astype(BF16)
            sel_r = (lax.broadcasted_iota(jnp.int32, (LANES, hg * ln), 0)
                     == g * hg + lax.broadcasted_iota(jnp.int32, (LANES, hg * ln), 1) // ln).astype(BF16)
            for pieces in (q_hi, q_lo):
                dacs_c = dacs_c - _dot(jnp.concatenate(pieces, axis=1), sel_c)
                dacs_r = dacs_r + _dot(sel_r, jnp.concatenate(pieces, axis=0))
            dgb = dgmt.astype(BF16)
            dx_ref[:, di + g * ns:di + (g + 1) * ns] = dbg + _dot(dgb, cg)
            dx_ref[:, di + (n_groups + g) * ns:di + (n_groups + g + 1) * ns] = dcg + _dot(dgb, bg, _TN)

        wds = w_ref[...] * dse
        dxdt = dx_ref[:, :di] + wds
        red = _spread(jnp.concatenate([dxdt * x, dy * yo_ref[...] * ee, xdt * wds, dy * x], axis=0), et_ref[...])
        ddt_x, r_off, r_state, ddr = red[:ln], red[ln:2 * ln], red[2 * ln:3 * ln], red[3 * ln:]
        dx_ref[:, :di] = dxdt * dte + dy * dske
        rowi = lax.broadcasted_iota(jnp.int32, (ln, LANES), 0)
        dlast = jnp.sum(r_state, axis=0, keepdims=True) + dtt * jnp.exp(acs[ln - 1:ln, :])
        dacs = r_off - r_state + dacs_c + dacs_r.T + jnp.where(rowi == ln - 1, dlast, 0.0)
        dadt = _dot(triu.astype(F32), dacs, precision=HIGHEST)
        ddt_ref[...] = dadt * a + ddt_x
        da_ref[...] += jnp.sum(dadt * dt, axis=0, keepdims=True)
        dd_ref[...] += jnp.sum(ddr, axis=0, keepdims=True)

        @pl.when(step == nc - 1)
        def _():
            da_ref[...] = da_ref[...] * a

    vec = pl.BlockSpec((1, LANES), lambda c: (0, 0))
    rev = lambda c: (nc - 1 - c, 0)
    return pl.pallas_call(
        body, name=name, grid=(nc,),
        in_specs=[pl.BlockSpec((ln, convd), rev), pl.BlockSpec((ln, LANES), rev), vec, vec,
                  pl.BlockSpec(e64.shape, lambda c: (0, 0)), pl.BlockSpec(e64t.shape, lambda c: (0, 0)),
                  pl.BlockSpec((1, di, ns), lambda c: (nc - 1 - c, 0, 0)), pl.BlockSpec((ln, di), rev)],
        out_specs=[pl.BlockSpec((ln, convd), rev), pl.BlockSpec((ln, LANES), rev), vec, vec],
        out_shape=[jax.ShapeDtypeStruct((s, convd), F32), jax.ShapeDtypeStruct((s, LANES), F32),
                   jax.ShapeDtypeStruct((1, LANES), F32), jax.ShapeDtypeStruct((1, LANES), F32)],
        scratch_shapes=[pltpu.VMEM((di, ns), F32), pltpu.VMEM((ln, di), F32), pltpu.VMEM((ln, di), F32)],
        compiler_params=_params(("arbitrary",)))(xbc, dt, alog, dskip, e64, e64t, prev_all, dy)


def _perm(a, d):
    if d == 1:
        return a
    s = a.shape[0]
    return a.reshape(s // d, d, -1).transpose(1, 0, 2).reshape(s, -1)


def _unperm(a, d):
    if d == 1:
        return a
    s = a.shape[0]
    return a.reshape(d, s // d, -1).transpose(1, 0, 2).reshape(s, -1)


def _rot_tables(s, d):
    half = ROPE_DIM // 2
    inv_freq = jnp.power(jnp.float32(ROPE_THETA), -jnp.arange(0, ROPE_DIM, 2, dtype=F32) / ROPE_DIM)
    v = jnp.arange(s, dtype=jnp.int32)
    pos = (v % (s // d)) * d + v // (s // d)
    ang = pos.astype(F32)[:, None] * inv_freq[None, :]
    cos, sin = jnp.cos(ang), jnp.sin(ang)
    zero = jnp.zeros((s, ATT_HEAD_DIM - ROPE_DIM), F32)
    cf = jnp.concatenate([cos, cos, jnp.ones_like(zero)], axis=1)
    s1 = jnp.concatenate([-sin, jnp.zeros_like(sin), zero], axis=1)
    s2 = jnp.concatenate([jnp.zeros_like(sin), sin, zero], axis=1)
    assert half * 2 == ROPE_DIM
    return cf, s1, s2


def _rot(x, tabs, sign):
    cf, s1, s2 = tabs
    half = ROPE_DIM // 2
    left = pltpu.roll(x, ATT_HEAD_DIM - half, 1)
    right = pltpu.roll(x, half, 1)
    return x * cf + sign * (left * s1 + right * s2)


def _att_masks(n, n_blk, rep):
    b = ATT_BLOCK
    row = lax.broadcasted_iota(jnp.int32, (rep * b, b), 0) & (b - 1)
    col = lax.broadcasted_iota(jnp.int32, (rep * b, b), 1)
    off = jnp.where(n % n_blk != 0, 0, 2 * b)
    return col <= row, col >= row + off


def _stack(x, rep):
    return jnp.concatenate([x[:, j * ATT_HEAD_DIM:(j + 1) * ATT_HEAD_DIM] for j in range(rep)], axis=0)


def _att_specs(nb, rep, cur, prv):
    b, hd = ATT_BLOCK, ATT_HEAD_DIM
    q_spec = pl.BlockSpec((b, rep * hd), lambda h, n: (cur(n), h))
    kc_spec = pl.BlockSpec((b, hd), lambda h, n: (cur(n), h))
    kp_spec = pl.BlockSpec((b, hd), lambda h, n: (prv(n), h))
    tc_spec = pl.BlockSpec((b, hd), lambda h, n: (cur(n), 0))
    tp_spec = pl.BlockSpec((b, hd), lambda h, n: (prv(n), 0))
    return q_spec, kc_spec, kp_spec, tc_spec, tp_spec


def _attn_fwd(q, k, v, tabs, n_blk, name):
    s = q.shape[0]
    b, hd = ATT_BLOCK, ATT_HEAD_DIM
    nb = s // b
    n_kv = ATT_KV_HEADS_PER_GROUP
    rep = ATT_HEADS_PER_GROUP // n_kv
    scale = hd ** -0.5

    def body(q_ref, kc_ref, kp_ref, vc_ref, vp_ref, cfc, s1c, s2c, cfp, s1p, s2p, o_ref, lse_ref):
        n = pl.program_id(1)
        tc = (cfc[...], s1c[...], s2c[...])
        tp = (cfp[...], s1p[...], s2p[...])
        qv = q_ref[...]
        q4 = jnp.concatenate([_rot(qv[:, j * hd:(j + 1) * hd], tc, 1.0) for j in range(rep)], axis=0).astype(BF16)
        kc = _rot(kc_ref[...], tc, 1.0).astype(BF16)
        kp = _rot(kp_ref[...], tp, 1.0).astype(BF16)
        mc, mp = _att_masks(n, n_blk, rep)
        sc = jnp.where(mc, _dot(q4, kc, _NT) * scale, NEG)
        sp = jnp.where(mp, _dot(q4, kp, _NT) * scale, NEG)
        m = jnp.maximum(jnp.max(sc, axis=1, keepdims=True), jnp.max(sp, axis=1, keepdims=True))
        pc, pp = jnp.exp(sc - m), jnp.exp(sp - m)
        l = jnp.sum(pc, axis=1, keepdims=True) + jnp.sum(pp, axis=1, keepdims=True)
        o = (_dot(pc.astype(BF16), vc_ref[...].astype(BF16)) + _dot(pp.astype(BF16), vp_ref[...].astype(BF16))) / l
        lse = jnp.broadcast_to(m + jnp.log(l), (rep * b, hd))
        for j in range(rep):
            o_ref[:, j * hd:(j + 1) * hd] = o[j * b:(j + 1) * b]
            lse_ref[:, j * hd:(j + 1) * hd] = lse[j * b:(j + 1) * b]

    cur = lambda n: n
    prv = lambda n: jnp.maximum(n - 1, 0)
    q_spec, kc_spec, kp_spec, tc_spec, tp_spec = _att_specs(nb, rep, cur, prv)
    return pl.pallas_call(
        body, name=name, grid=(n_kv, nb),
        in_specs=[q_spec, kc_spec, kp_spec, kc_spec, kp_spec, tc_spec, tc_spec, tc_spec, tp_spec, tp_spec, tp_spec],
        out_specs=[q_spec, q_spec],
        out_shape=[jax.ShapeDtypeStruct(q.shape, F32), jax.ShapeDtypeStruct(q.shape, F32)],
        compiler_params=_params(("parallel", "arbitrary")))(q, k, k, v, v, *tabs, *tabs)


def _attn_bwd(q, k, v, do, lse, delta, tabs, n_blk, name):
    s = q.shape[0]
    b, hd = ATT_BLOCK, ATT_HEAD_DIM
    nb = s // b
    n_kv = ATT_KV_HEADS_PER_GROUP
    rep = ATT_HEADS_PER_GROUP // n_kv
    scale = hd ** -0.5

    def body(q_ref, do_ref, lse_ref, dl_ref, kc_ref, kp_ref, vc_ref, vp_ref, cfc, s1c, s2c, cfp, s1p, s2p,
             dq_ref, dk_ref, dv_ref, ck, cv):
        n = pl.program_id(1)
        tp = (cfp[...], s1p[...], s2p[...])

        @pl.when(n == 0)
        def _():
            ck[...] = jnp.zeros(ck.shape, F32)
            cv[...] = jnp.zeros(cv.shape, F32)

        @pl.when(n < nb)
        def _():
            tc = (cfc[...], s1c[...], s2c[...])
            qv = q_ref[...]
            q4 = jnp.concatenate([_rot(qv[:, j * hd:(j + 1) * hd], tc, 1.0) for j in range(rep)],
                                 axis=0).astype(BF16)
            do4 = _stack(do_ref[...], rep).astype(BF16)
            lse4 = _stack(lse_ref[...], rep)
            dl4 = _stack(dl_ref[...], rep)
            kc = _rot(kc_ref[...], tc, 1.0).astype(BF16)
            kp = _rot(kp_ref[...], tp, 1.0).astype(BF16)
            vc, vp = vc_ref[...].astype(BF16), vp_ref[...].astype(BF16)
            mc, mp = _att_masks(n, n_blk, rep)
            pc = jnp.where(mc, jnp.exp(jnp.where(mc, _dot(q4, kc, _NT) * scale - lse4, 0.0)), 0.0)
            pp = jnp.where(mp, jnp.exp(jnp.where(mp, _dot(q4, kp, _NT) * scale - lse4, 0.0)), 0.0)
            dsc = (pc * (_dot(do4, vc, _NT) - dl4)).astype(BF16)
            dsp = (pp * (_dot(do4, vp, _NT) - dl4)).astype(BF16)
            dq4 = (_dot(dsc, kc) + _dot(dsp, kp)) * scale
            for j in range(rep):
                dq_ref[:, j * hd:(j + 1) * hd] = _rot(dq4[j * b:(j + 1) * b], tc, -1.0).astype(dq_ref.dtype)
            dk_prev = ck[...] + _dot(dsp, q4, _TN) * scale
            dv_prev = cv[...] + _dot(pp.astype(BF16), do4, _TN)
            dk_ref[...] = _rot(dk_prev, tp, -1.0).astype(dk_ref.dtype)
            dv_ref[...] = dv_prev.astype(dv_ref.dtype)
            ck[...] = _dot(dsc, q4, _TN) * scale
            cv[...] = _dot(pc.astype(BF16), do4, _TN)

        @pl.when(n == nb)
        def _():
            dk_ref[...] = _rot(ck[...], tp, -1.0).astype(dk_ref.dtype)
            dv_ref[...] = cv[...].astype(dv_ref.dtype)

    cur = lambda n: jnp.minimum(n, nb - 1)
    prv = lambda n: jnp.maximum(n - 1, 0)
    q_spec, kc_spec, kp_spec, tc_spec, tp_spec = _att_specs(nb, rep, cur, prv)
    return pl.pallas_call(
        body, name=name, grid=(n_kv, nb + 1),
        in_specs=[q_spec, q_spec, q_spec, q_spec, kc_spec, kp_spec, kc_spec, kp_spec,
                  tc_spec, tc_spec, tc_spec, tp_spec, tp_spec, tp_spec],
        out_specs=[q_spec, kp_spec, kp_spec],
        out_shape=[jax.ShapeDtypeStruct(q.shape, BF16), jax.ShapeDtypeStruct(k.shape, BF16),
                   jax.ShapeDtypeStruct(k.shape, BF16)],
        scratch_shapes=[pltpu.VMEM((b, hd), F32), pltpu.VMEM((b, hd), F32)],
        compiler_params=_params(("parallel", "arbitrary")))(q, do, lse, delta, k, k, v, v, *tabs, *tabs)


def _adamw(g_slabs, w, m, v, name):
    kk, r, c = g_slabs.shape
    tile = r if r <= 256 else _pick_rows(r, 256)

    def body(g_ref, w_ref, m_ref, v_ref, go_ref, d_ref, mo_ref, vo_ref):
        g = g_ref[0].astype(F32)
        for k in range(1, kk):
            g = g + g_ref[k].astype(F32)
        m2 = ADAM_B1 * m_ref[...] + (1.0 - ADAM_B1) * g
        v2 = ADAM_B2 * v_ref[...] + (1.0 - ADAM_B2) * jnp.square(g)
        m_hat = m2 / (1.0 - ADAM_B1 ** ADAM_STEP)
        v_hat = v2 / (1.0 - ADAM_B2 ** ADAM_STEP)
        go_ref[...] = g
        d_ref[...] = -ADAM_LR * (m_hat / (jnp.sqrt(v_hat) + ADAM_EPS) + ADAM_WD * w_ref[...])
        mo_ref[...] = m2
        vo_ref[...] = v2

    spec = pl.BlockSpec((tile, c), lambda i: (i, 0))
    return pl.pallas_call(
        body, name=name, grid=(r // tile,), in_specs=[pl.BlockSpec((kk, tile, c), lambda i: (0, i, 0)), spec, spec, spec],
        out_specs=[spec] * 4, out_shape=[jax.ShapeDtypeStruct((r, c), F32)] * 4,
        compiler_params=_params(("parallel",)))(g_slabs, w, m, v)


def _pick_rows(r, pref):
    t = (pref // 16) * 16
    while t >= 16:
        if r % t == 0:
            return t
        t -= 16
    return r


def _coords():
    return lax.axis_index("x"), lax.axis_index("y"), lax.axis_index("c")


def _dev_index(px, py, pc):
    return 4 * px + 2 * py + pc


def _all_gather(shards, name):
    na = len(shards)

    def body(*refs):
        ins, outs = refs[:na], refs[na:2 * na]
        send_sems, recv_sems, local_sems = refs[2 * na:]
        x, y, c = _coords()
        me, sibling = (x, y, c), (x, y, 1 - c)
        chips = [(1 - x, y), (x, 1 - y), (1 - x, 1 - y)]

        def copy(a, k, block, to, src=None):
            dst = outs[a].at[_dev_index(*block)]
            return pltpu.make_async_remote_copy(
                src_ref=dst if src is None else src, dst_ref=dst, send_sem=send_sems.at[a * 7 + k],
                recv_sem=recv_sems.at[a * 7 + k], device_id=to, device_id_type=MESH)

        mine = [pltpu.make_async_copy(ins[a], outs[a].at[_dev_index(*me)], local_sems.at[a]) for a in range(na)]
        for cp in mine:
            cp.start()
        first = []
        for a in range(na):
            first.append(copy(a, 0, me, sibling, src=ins[a]))
            first += [copy(a, 1 + j, me, (*chip, c), src=ins[a]) for j, chip in enumerate(chips)]
        for cp in first:
            cp.start()
        passed = []
        for j, chip in enumerate(chips):
            for a in range(na):
                copy(a, 1 + j, (*chip, c), me).wait_recv()
                cp = copy(a, 4 + j, (*chip, c), sibling)
                cp.start()
                passed.append(cp)
        for a in range(na):
            copy(a, 0, sibling, me).wait_recv()
            for j, chip in enumerate(chips):
                copy(a, 4 + j, (*chip, 1 - c), me).wait_recv()
        for cp in first + passed:
            cp.wait_send()
        for cp in mine:
            cp.wait()

    hbm = pl.BlockSpec(memory_space=pl.ANY)
    return pl.pallas_call(
        body, name=name, in_specs=[hbm] * na, out_specs=[hbm] * na,
        out_shape=[jax.ShapeDtypeStruct((NDEV,) + s.shape, s.dtype) for s in shards],
        scratch_shapes=[pltpu.SemaphoreType.DMA((7 * na,)), pltpu.SemaphoreType.DMA((7 * na,)),
                        pltpu.SemaphoreType.DMA((na,))])(*shards)


def _exchange(slabs, whole, name):
    ns, nw = len(slabs), len(whole)
    na = ns + nw

    def body(*refs):
        ins, outs = refs[:na], refs[na:2 * na]
        send_sems, recv_sems, local_sems = refs[2 * na:]
        x, y, c = _coords()
        me = _dev_index(x, y, c)

        def src_of(a, p):
            return ins[a].at[p] if a < ns else ins[a]

        def copy(a, k, peer):
            p = _dev_index(*peer)
            return pltpu.make_async_remote_copy(
                src_ref=src_of(a, p), dst_ref=outs[a].at[me], send_sem=send_sems.at[a * 7 + k - 1],
                recv_sem=recv_sems.at[a * 7 + k - 1], device_id=peer, device_id_type=MESH)

        def arrival(a, k, peer):
            p = _dev_index(*peer)
            return pltpu.make_async_remote_copy(
                src_ref=src_of(a, p), dst_ref=outs[a].at[p], send_sem=send_sems.at[a * 7 + k - 1],
                recv_sem=recv_sems.at[a * 7 + k - 1], device_id=peer, device_id_type=MESH)

        mine = [pltpu.make_async_copy(src_of(a, me), outs[a].at[me], local_sems.at[a]) for a in range(na)]
        for cp in mine:
            cp.start()
        peers = [(k, (x ^ (k >> 2), y ^ ((k >> 1) & 1), c ^ (k & 1))) for k in range(1, NDEV)]
        sent = [copy(a, k, peer) for k, peer in peers for a in range(na)]
        for cp in sent:
            cp.start()
        for k, peer in peers:
            for a in range(na):
                arrival(a, k, peer).wait_recv()
        for cp in sent:
            cp.wait_send()
        for cp in mine:
            cp.wait()

    hbm = pl.BlockSpec(memory_space=pl.ANY)
    out_shape = [jax.ShapeDtypeStruct(s.shape, s.dtype) for s in slabs]
    out_shape += [jax.ShapeDtypeStruct((NDEV,) + w.shape, w.dtype) for w in whole]
    return pl.pallas_call(
        body, name=name, in_specs=[hbm] * na, out_specs=[hbm] * na, out_shape=out_shape,
        scratch_shapes=[pltpu.SemaphoreType.DMA((7 * na,)), pltpu.SemaphoreType.DMA((7 * na,)),
                        pltpu.SemaphoreType.DMA((na,))])(*slabs, *whole)


def _pack(vecs):
    parts, spans, off = [], [], 0
    for v in vecs:
        n = v.size
        pad = (-n) % LANES
        parts.append(jnp.pad(v.reshape(-1).astype(F32), (0, pad)))
        spans.append((off, n))
        off += n + pad
    return jnp.concatenate(parts).reshape(-1, LANES), spans


def _pad_lanes(v):
    v = v.reshape(1, -1)
    return jnp.pad(v, ((0, 0), (0, LANES - v.shape[1])))


def _cols_to_slabs(g):
    sh = g.shape
    g = g.reshape(sh[:-1] + (NDEV, sh[-1] // NDEV))
    return jnp.moveaxis(g, -2, 0)


def _rows_to_slabs(g):
    sh = g.shape
    g = g.reshape(sh[:-2] + (NDEV, sh[-2] // NDEV, sh[-1]))
    return jnp.moveaxis(g, -3, 0)


def _slabs_to_cols(a):
    a = jnp.moveaxis(a, 0, -2)
    return a.reshape(a.shape[:-2] + (a.shape[-2] * a.shape[-1],))


def _slabs_to_rows(a):
    a = jnp.moveaxis(a, 0, -3)
    return a.reshape(a.shape[:-3] + (a.shape[-3] * a.shape[-2], a.shape[-1]))


def _ffn_forward(x, norm_w, wup_g, wup_v, cw_g, cw_v, wdown, tag):
    h = _rms_fwd(x, norm_w, f"{tag}_norm")
    ug = _mm(h, wup_g, name=f"{tag}_up_gate")
    uv = _mm(h, wup_v, name=f"{tag}_up_val")
    f = _ffn_gate_fwd(ug, uv, cw_g, cw_v, f"{tag}_gate")
    return _mm(f, wdown, res=x, name=f"{tag}_down"), (h, ug, uv, f)


def _ffn_backward(x, saved, dout, dout_b, norm_w, wup_g, wup_v, cw_g, cw_v, wdown, tag):
    h, ug, uv, f = saved
    dwdown = _mm(f, dout_b, ta=True, name=f"{tag}_dwdown")
    df = _mm(dout_b, wdown, tb=True, name=f"{tag}_df")
    dug, duv, dcg, dcv = _ffn_gate_bwd(ug, uv, cw_g, cw_v, df, f"{tag}_gate_bwd")
    dwg = _mm(h, dug, ta=True, name=f"{tag}_dwup_gate")
    dwv = _mm(h, duv, ta=True, name=f"{tag}_dwup_val")
    dh = _mm(dug, wup_g, tb=True, name=f"{tag}_dh_gate")
    dh = _mm(duv, wup_v, tb=True, res=dh, name=f"{tag}_dh_val")
    dx, dxb, dnorm = _rms_bwd(x, norm_w, dh, dout, f"{tag}_norm_bwd")
    return dx, dxb, (jnp.concatenate([dwg, dwv], axis=1), jnp.concatenate([dcg, dcv], axis=1), dwdown, dnorm)


def kernel(x, a_norm, ssm_w_in, ssm_conv_w, ssm_conv_b, ssm_dt_bias, ssm_a_log, ssm_d, ssm_norm, ssm_w_out, kv_norm, w_kv, b_norm, att_w_q, att_w_o, ffn_norm, ffn_w_up, ffn_conv_w, ffn_w_down, final_norm, loss_target, m_a_norm, m_ssm_w_in, m_ssm_conv_w, m_ssm_conv_b, m_ssm_dt_bias, m_ssm_a_log, m_ssm_d, m_ssm_norm, m_ssm_w_out, m_kv_norm, m_w_kv, m_b_norm, m_att_w_q, m_att_w_o, m_ffn_norm, m_ffn_w_up, m_ffn_conv_w, m_ffn_w_down, m_final_norm, v_a_norm, v_ssm_w_in, v_ssm_conv_w, v_ssm_conv_b, v_ssm_dt_bias, v_ssm_a_log, v_ssm_d, v_ssm_norm, v_ssm_w_out, v_kv_norm, v_w_kv, v_b_norm, v_att_w_q, v_att_w_o, v_ffn_norm, v_ffn_w_up, v_ffn_conv_w, v_ffn_w_down, v_final_norm):
    given = dict(locals())
    xs, tgt = x[0], loss_target[0]
    s, d = xs.shape
    di = ssm_w_out.shape[1] * NDEV
    nh = ssm_dt_bias.shape[1]
    ng = SSM_N_GROUPS
    convd = di + 2 * ng * SSM_D_STATE
    f = ffn_w_down.shape[1] * NDEV
    n_att = len(ATT_PATTERNS)
    qg = ATT_HEADS_PER_GROUP * ATT_HEAD_DIM
    kg = ATT_KV_HEADS_PER_GROUP * ATT_HEAD_DIM
    kvd = n_att * kg
    assert all(w // dil == ATT_BLOCK for w, dil in ATT_PATTERNS)

    small, _ = _pack([a_norm, ssm_conv_w, ssm_conv_b, ssm_norm, ffn_conv_w])
    big = [ssm_w_in[0], ssm_w_out[0], w_kv, att_w_q[0], att_w_o[0], ffn_w_up, ffn_w_down]
    gat = _all_gather([b.astype(BF16) for b in big] + [small], "gather_weights")
    w_in = _slabs_to_cols(gat[0])
    w_z, w_xbc = w_in[:, :di], w_in[:, di:di + convd]
    w_dt = jnp.pad(w_in[:, di + convd:], ((0, 0), (0, LANES - nh)))
    w_out = _slabs_to_rows(gat[1])
    w_kvf = _slabs_to_cols(gat[2])
    w_q = _slabs_to_cols(gat[3])
    w_o = _slabs_to_rows(gat[4])
    w_up = _slabs_to_cols(gat[5])
    w_up_g, w_up_v = w_up[:, :, :f], w_up[:, :, f:]
    w_down = _slabs_to_rows(gat[6])
    sm = gat[7].reshape(NDEV, -1)
    o0 = 0

    def take(shape):
        nonlocal o0
        n = math.prod(shape)
        out = sm[:, o0:o0 + n].reshape((NDEV,) + shape)
        o0 += n + (-n) % LANES
        return out
    a_norm_f = _slabs_to_cols(take(a_norm.shape))
    conv_w_f = _slabs_to_cols(take(ssm_conv_w.shape))[0]
    conv_b_f = _slabs_to_cols(take(ssm_conv_b.shape))
    ssm_norm_f = _slabs_to_cols(take(ssm_norm.shape))
    fcw = _slabs_to_cols(take(ffn_conv_w.shape))
    fcw_g, fcw_v = fcw[:, :, :f], fcw[:, :, f:]
    dtb, alog, dsk = _pad_lanes(ssm_dt_bias), _pad_lanes(ssm_a_log), _pad_lanes(ssm_d)
    kvn, fin = kv_norm.reshape(1, d), final_norm.reshape(1, d)

    h0 = _rms_fwd(xs, a_norm_f, "a_norm")
    z = _mm(h0, w_z, name="in_z")
    xbc_pre = _mm(h0, w_xbc, name="in_xbc")
    dtr = _mm(h0, w_dt, name="in_dt")
    xbc = _conv_silu_fwd(xbc_pre, conv_w_f, conv_b_f, "ssm_conv")
    dt = _softplus_fwd(dtr, dtb, "ssm_dt")
    y, prevs = _ssd_fwd2(xbc, dt, alog, dsk, di, nh, ng, "ssd")
    yn = _gnorm_fwd(y, z, ssm_norm_f, ng, "ssm_gnorm")
    x1 = _mm(yn, w_out, res=xs, name="ssm_out")
    x2, ffn0 = _ffn_forward(x1, ffn_norm[0:1], w_up_g[0], w_up_v[0], fcw_g[0], fcw_v[0], w_down[0], "ffn0")
    hk = _rms_fwd(x2, kvn, "kv_norm")
    kv = _mm(hk, w_kvf, name="kv_proj")
    h2 = _rms_fwd(x2, b_norm, "b_norm")
    q = _mm(h2, w_q, name="q_proj")
    att = []
    for g, (window, dil) in enumerate(ATT_PATTERNS):
        tabs = _rot_tables(s, dil)
        qp = _perm(q[:, g * qg:(g + 1) * qg], dil)
        kp = _perm(kv[:, g * kg:(g + 1) * kg], dil)
        vp = _perm(kv[:, kvd + g * kg:kvd + (g + 1) * kg], dil)
        n_blk = s // dil // ATT_BLOCK
        og, lg = _attn_fwd(qp, kp, vp, tabs, n_blk, f"attn{g}")
        att.append((qp, kp, vp, tabs, n_blk, dil, _unperm(og, dil), _unperm(lg, dil)))
    o, ob, lse = _merge_fwd([t[6] for t in att], [t[7] for t in att], "attn_merge")
    x3 = _mm(ob, w_o, res=x2, name="attn_out")
    x4, ffn1 = _ffn_forward(x3, ffn_norm[1:2], w_up_g[1], w_up_v[1], fcw_g[1], fcw_v[1], w_down[1], "ffn1")
    loss_part, dx4, dx4b, dfin = _final_loss(x4, fin, tgt, "loss_head")

    dx3, dx3b, (dwup1, dfc1, dwdown1, dfn1) = _ffn_backward(
        x3, ffn1, dx4, dx4b, ffn_norm[1:2], w_up_g[1], w_up_v[1], fcw_g[1], fcw_v[1], w_down[1], "ffn1")
    dw_o = _mm(ob, dx3b, ta=True, name="attn_dwo")
    do = _mm(dx3b, w_o, tb=True, name="attn_do")
    delta = _delta(do, o, "attn_delta")
    dqs, dks, dvs = [], [], []
    for g, (qp, kp, vp, tabs, n_blk, dil, _, _) in enumerate(att):
        dqp, dkp, dvp = _attn_bwd(qp, kp, vp, _perm(do, dil), _perm(lse, dil), _perm(delta, dil), tabs, n_blk,
                                  f"attn{g}_bwd")
        dqs.append(_unperm(dqp, dil))
        dks.append(_unperm(dkp, dil))
        dvs.append(_unperm(dvp, dil))
    dq = jnp.concatenate(dqs, axis=1)
    dkv = jnp.concatenate(dks + dvs, axis=1)
    dw_q = _mm(h2, dq, ta=True, name="q_dw")
    dh2 = _mm(dq, w_q, tb=True, name="q_dh")
    dw_kv = _mm(hk, dkv, ta=True, name="kv_dw")
    dhk = _mm(dkv, w_kvf, tb=True, name="kv_dh")
    dx2, _, db_norm = _rms_bwd(x2, b_norm, dh2, dx3, "b_norm_bwd")
    dx2, dx2b, dkv_norm = _rms_bwd(x2, kvn, dhk, dx2, "kv_norm_bwd")
    dx1, dx1b, (dwup0, dfc0, dwdown0, dfn0) = _ffn_backward(
        x1, ffn0, dx2, dx2b, ffn_norm[0:1], w_up_g[0], w_up_v[0], fcw_g[0], fcw_v[0], w_down[0], "ffn0")
    dw_out = _mm(yn, dx1b, ta=True, name="ssm_dwout")
    dyn = _mm(dx1b, w_out, tb=True, name="ssm_dyn")
    dy, dz, dssm_norm = _gnorm_bwd(dyn, y, z, ssm_norm_f, ng, "ssm_gnorm_bwd")
    dxbc, ddt, dalog, ddsk = _ssd_bwd2(xbc, dt, alog, dsk, prevs, dy, di, nh, ng, "ssd_bwd")
    ddtr, ddtb = _softplus_bwd(ddt, dtr, dtb, nh, "ssm_dt_bwd")
    dxbc_pre, dconv_w, dconv_b = _conv_silu_bwd(xbc_pre, conv_w_f, conv_b_f, dxbc, "ssm_conv_bwd")
    dw_z = _mm(h0, dz, ta=True, name="in_dwz")
    dw_xbc = _mm(h0, dxbc_pre, ta=True, name="in_dwxbc")
    dw_dt = _mm(h0, ddtr, ta=True, name="in_dwdt")[:, :nh]
    dh0 = _mm(dz, w_z, tb=True, name="in_dh_z")
    dh0 = _mm(dxbc_pre, w_xbc, tb=True, res=dh0, name="in_dh_xbc")
    dh0 = _mm(ddtr, w_dt, tb=True, res=dh0, name="in_dh_dt")
    dx0, _, da_norm = _rms_bwd(xs, a_norm_f, dh0, dx1, "a_norm_bwd")

    full = {
        'ssm_w_in': jnp.concatenate([dw_z, dw_xbc, dw_dt], axis=1)[None], 'ssm_w_out': dw_out[None], 'w_kv': dw_kv,
        'att_w_q': dw_q[None], 'att_w_o': dw_o[None], 'ffn_w_up': jnp.stack([dwup0, dwup1]),
        'ffn_w_down': jnp.stack([dwdown0, dwdown1]),
    }
    by_cols = {'ssm_w_in', 'w_kv', 'att_w_q', 'ffn_w_up'}
    big_names = list(full)
    slabs = [(_cols_to_slabs if n in by_cols else _rows_to_slabs)(full[n]).astype(BF16) for n in big_names]
    small_full = {
        'a_norm': da_norm, 'ssm_conv_w': dconv_w[None], 'ssm_conv_b': dconv_b, 'ssm_dt_bias': ddtb[:, :nh],
        'ssm_a_log': dalog[:, :nh], 'ssm_d': ddsk[:, :nh], 'ssm_norm': dssm_norm, 'kv_norm': dkv_norm.reshape(d),
        'b_norm': db_norm, 'ffn_norm': jnp.concatenate([dfn0, dfn1], axis=0), 'ffn_conv_w': jnp.stack([dfc0, dfc1]),
        'final_norm': dfin.reshape(d),
    }
    small_names = list(small_full)
    packed, spans = _pack([small_full[n] for n in small_names])
    recv = _exchange(slabs, [packed], "exchange_grads")
    small_sum = _sum_slabs(recv[-1], "sum_small_grads").reshape(-1)

    me = _dev_index(*_coords())
    res = {}
    for n, r in zip(big_names, recv[:-1]):
        w = given[n]
        c = w.shape[-1]
        outs = _adamw(r.reshape(NDEV, -1, c), w.reshape(-1, c), given['m_' + n].reshape(-1, c),
                      given['v_' + n].reshape(-1, c), f"adamw_{n}")
        res[n] = [o_.reshape(w.shape) for o_ in outs]
    sharded_small = {'a_norm', 'ssm_conv_w', 'ssm_conv_b', 'ssm_norm', 'ffn_conv_w'}
    for n, (off, size) in zip(small_names, spans):
        w = given[n]
        gfull = small_sum[off:off + size].reshape(small_full[n].shape)
        if n in sharded_small:
            c = w.shape[-1]
            gfull = lax.dynamic_slice_in_dim(gfull, me * c, c, axis=gfull.ndim - 1)
        c = w.shape[-1]
        outs = _adamw(gfull.reshape(1, -1, c), w.reshape(-1, c), given['m_' + n].reshape(-1, c),
                      given['v_' + n].reshape(-1, c), f"adamw_{n}")
        res[n] = [o_.reshape(w.shape) for o_ in outs]

    loss = lax.psum(loss_part[0, 0], AXES)
    return (loss, dx0[None], *[res[n][0] for n in WEIGHTS], *[res[n][1] for n in WEIGHTS],
            *[res[n][2] for n in WEIGHTS], *[res[n][3] for n in WEIGHTS])
```

```python
import functools
import math

import jax
import jax.numpy as jnp
from jax import lax
from jax.experimental import pallas as pl
from jax.experimental.pallas import tpu as pltpu

F32, BF16 = jnp.float32, jnp.bfloat16
AXES = ("x", "y", "c")
NDEV = 8
MESH = pl.DeviceIdType.MESH
HIGHEST = lax.Precision.HIGHEST

LANES = 128
VMEM_LIMIT_BYTES = 48 * 1024 * 1024

RMS_EPS = 1e-6
GATED_NORM_EPS = 1e-5
SSM_HEAD_DIM = 64
SSM_N_GROUPS = 8
SSM_D_STATE = 128
SSM_CONV = 4
SSM_CHUNK = 128
ATT_PATTERNS = ((128, 1), (512, 4), (2048, 16))
ATT_HEAD_DIM = 128
ATT_HEADS_PER_GROUP = 8
ATT_KV_HEADS_PER_GROUP = 2
ATT_BLOCK = 128
ROPE_DIM = ATT_HEAD_DIM // 4
ROPE_THETA = 500000.0
FFN_CONV = 3
ADAM_LR = 0.001
ADAM_B1 = 0.9
ADAM_B2 = 0.999
ADAM_EPS = 1e-08
ADAM_WD = 0.01
ADAM_STEP = 10
NEG = -1e30

WEIGHTS = ['a_norm', 'ssm_w_in', 'ssm_conv_w', 'ssm_conv_b', 'ssm_dt_bias', 'ssm_a_log', 'ssm_d', 'ssm_norm',
           'ssm_w_out', 'kv_norm', 'w_kv', 'b_norm', 'att_w_q', 'att_w_o', 'ffn_norm', 'ffn_w_up', 'ffn_conv_w',
           'ffn_w_down', 'final_norm']


def _params(sem=None):
    kw = dict(vmem_limit_bytes=VMEM_LIMIT_BYTES)
    if sem is not None:
        kw["dimension_semantics"] = sem
    return pltpu.CompilerParams(**kw)


def _pick(n, pref):
    if n <= pref:
        return n
    t = (pref // LANES) * LANES
    while t >= LANES:
        if n % t == 0:
            return t
        t -= LANES
    return n


def _dot(a, b, dims=(((1,), (0,)), ((), ())), precision=None):
    return lax.dot_general(a, b, dims, precision=precision, preferred_element_type=F32)


_NT = (((1,), (1,)), ((), ()))
_TN = (((0,), (0,)), ((), ()))


def _mm(a, b, *, ta=False, tb=False, res=None, out_dtype=F32, name, tm=1024, tn=1408, tk=2048):
    m = a.shape[1] if ta else a.shape[0]
    k = a.shape[0] if ta else a.shape[1]
    n = b.shape[0] if tb else b.shape[1]
    assert k == (b.shape[1] if tb else b.shape[0])
    tm, tn, tk = _pick(m, tm), _pick(n, tn), _pick(k, tk)
    nk = k // tk
    a_spec = pl.BlockSpec((tk, tm), lambda i, j, l: (l, i)) if ta else pl.BlockSpec((tm, tk), lambda i, j, l: (i, l))
    b_spec = pl.BlockSpec((tn, tk), lambda i, j, l: (j, l)) if tb else pl.BlockSpec((tk, tn), lambda i, j, l: (l, j))
    o_spec = pl.BlockSpec((tm, tn), lambda i, j, l: (i, j))
    dims = (((0 if ta else 1,), (1 if tb else 0,)), ((), ()))
    has_res = res is not None

    def body(*refs):
        a_ref, b_ref = refs[:2]
        r_ref = refs[2] if has_res else None
        o_ref = refs[2 + has_res]
        p = _dot(a_ref[...].astype(BF16), b_ref[...].astype(BF16), dims)

        def finish(r):
            if has_res:
                r = r + r_ref[...]
            o_ref[...] = r.astype(o_ref.dtype)

        if nk == 1:
            finish(p)
            return
        acc = refs[3 + has_res]
        l = pl.program_id(2)

        @pl.when(l == 0)
        def _():
            acc[...] = p

        @pl.when(jnp.logical_and(l > 0, l < nk - 1))
        def _():
            acc[...] += p

        @pl.when(l == nk - 1)
        def _():
            finish(acc[...] + p)

    ins = [a, b] + ([res] if has_res else [])
    in_specs = [a_spec, b_spec] + ([o_spec] if has_res else [])
    return pl.pallas_call(
        body, name=name, grid=(m // tm, n // tn, nk), in_specs=in_specs, out_specs=o_spec,
        out_shape=jax.ShapeDtypeStruct((m, n), out_dtype),
        scratch_shapes=[pltpu.VMEM((tm, tn), F32)] if nk > 1 else [],
        compiler_params=_params(("parallel", "parallel", "arbitrary")))(*ins)


def _rowwise(fn, rows, bcasts, outs, accs=(), *, tile, name):
    s = rows[0].shape[0]
    tile = min(tile, s)
    n_in, n_out, n_acc = len(rows) + len(bcasts), len(outs), len(accs)

    def body(*refs):
        vals = fn(*[r[...] for r in refs[:n_in]])
        o_refs = refs[n_in:n_in + n_out]
        a_refs = refs[n_in + n_out:]
        for r, v in zip(o_refs, vals[:n_out]):
            r[...] = v.astype(r.dtype)

        @pl.when(pl.program_id(0) == 0)
        def _():
            for r in a_refs:
                r[...] = jnp.zeros(r.shape, r.dtype)

        for r, v in zip(a_refs, vals[n_out:]):
            r[...] += v

    in_specs = [pl.BlockSpec((tile, r.shape[1]), lambda i: (i, 0)) for r in rows]
    in_specs += [pl.BlockSpec(b.shape, lambda i: (0, 0)) for b in bcasts]
    out_specs = [pl.BlockSpec((tile, c), lambda i: (i, 0)) for c, _ in outs]
    out_specs += [pl.BlockSpec(sh, lambda i: (0, 0)) for sh, _ in accs]
    out_shape = [jax.ShapeDtypeStruct((s, c), dt) for c, dt in outs]
    out_shape += [jax.ShapeDtypeStruct(sh, dt) for sh, dt in accs]
    return pl.pallas_call(body, name=name, grid=(s // tile,), in_specs=in_specs, out_specs=out_specs,
                          out_shape=out_shape, compiler_params=_params(("arbitrary",)))(*rows, *bcasts)


def _rms_fwd(x, w, name):
    def fn(x, w):
        r = lax.rsqrt(jnp.mean(x * x, axis=-1, keepdims=True) + RMS_EPS)
        return (x * r * w,)
    return _rowwise(fn, [x], [w], [(x.shape[1], BF16)], tile=256, name=name)[0]


def _rms_bwd(x, w, dh, dres, name):
    def fn(x, dh, dres, w):
        r = lax.rsqrt(jnp.mean(x * x, axis=-1, keepdims=True) + RMS_EPS)
        xh = x * r
        dxh = dh * w
        dx = dres + r * (dxh - xh * jnp.mean(dxh * xh, axis=-1, keepdims=True))
        return dx, dx, jnp.sum(dh * xh, axis=0, keepdims=True)
    d = x.shape[1]
    return _rowwise(fn, [x, dh, dres], [w], [(d, F32), (d, BF16)], [((1, d), F32)], tile=256, name=name)


def _final_loss(x, w, tgt, name):
    d = x.shape[1]

    def fn(x, t, w):
        r = lax.rsqrt(jnp.mean(x * x, axis=-1, keepdims=True) + RMS_EPS)
        xh = x * r
        err = xh * w - t
        part = jnp.sum(jnp.mean(err * err, axis=-1, keepdims=True), axis=0, keepdims=True) * 0.5
        dy = err * (1.0 / d)
        dxh = dy * w
        dx = r * (dxh - xh * jnp.mean(dxh * xh, axis=-1, keepdims=True))
        return dx, dx, part, jnp.sum(dy * xh, axis=0, keepdims=True)
    dx, dxb, part, dw = _rowwise(fn, [x, tgt], [w], [(d, F32), (d, BF16)], [((1, 1), F32), ((1, d), F32)],
                                 tile=256, name=name)
    return part, dx, dxb, dw


def _softplus_fwd(dtr, bias, name):
    def fn(r, b):
        v = r + b
        return (jnp.maximum(v, 0.0) + jnp.log(1.0 + jnp.exp(-jnp.abs(v))),)
    return _rowwise(fn, [dtr], [bias], [(LANES, F32)], tile=512, name=name)[0]


def _softplus_bwd(ddt, dtr, bias, n_heads, name):
    def fn(g, r, b):
        lane = lax.broadcasted_iota(jnp.int32, g.shape, 1)
        d = jnp.where(lane < n_heads, g * jax.nn.sigmoid(r + b), 0.0)
        return d, jnp.sum(d, axis=0, keepdims=True)
    return _rowwise(fn, [ddt, dtr], [bias], [(LANES, BF16)], [((1, LANES), F32)], tile=512, name=name)


def _gnorm_fwd(y, z, w, n_groups, name):
    di = y.shape[1]
    gs = di // n_groups

    def fn(y, z, w):
        y2 = y * (z * jax.nn.sigmoid(z))
        out = []
        for g in range(n_groups):
            sl = y2[:, g * gs:(g + 1) * gs]
            r = lax.rsqrt(jnp.mean(sl * sl, axis=-1, keepdims=True) + GATED_NORM_EPS)
            out.append(sl * r)
        return (jnp.concatenate(out, axis=1) * w,)
    return _rowwise(fn, [y, z], [w], [(di, BF16)], tile=256, name=name)[0]


def _gnorm_bwd(dyn, y, z, w, n_groups, name):
    di = y.shape[1]
    gs = di // n_groups

    def fn(dyn, y, z, w):
        sig = jax.nn.sigmoid(z)
        sz = z * sig
        y2 = y * sz
        d2n = dyn * w
        dy2, yhat = [], []
        for g in range(n_groups):
            sl = y2[:, g * gs:(g + 1) * gs]
            dg = d2n[:, g * gs:(g + 1) * gs]
            r = lax.rsqrt(jnp.mean(sl * sl, axis=-1, keepdims=True) + GATED_NORM_EPS)
            yh = sl * r
            dy2.append(r * (dg - yh * jnp.mean(dg * yh, axis=-1, keepdims=True)))
            yhat.append(yh)
        dy2 = jnp.concatenate(dy2, axis=1)
        yhat = jnp.concatenate(yhat, axis=1)
        dz = dy2 * y * (sig * (1.0 + z * (1.0 - sig)))
        return dy2 * sz, dz, jnp.sum(dyn * yhat, axis=0, keepdims=True)
    return _rowwise(fn, [dyn, y, z], [w], [(di, F32), (di, BF16)], [((1, di), F32)], tile=128, name=name)


def _merge_fwd(os_, lses, name):
    n = len(os_)

    def fn(*v):
        o, l = v[:n], v[n:]
        m = functools.reduce(jnp.maximum, l)
        e = [jnp.exp(li - m) for li in l]
        tot = functools.reduce(jnp.add, e)
        acc = functools.reduce(jnp.add, [ei * oi for ei, oi in zip(e, o)]) / tot
        return acc, acc, m + jnp.log(tot)
    c = os_[0].shape[1]
    return _rowwise(fn, list(os_) + list(lses), [], [(c, F32), (c, BF16), (c, F32)], tile=256, name=name)


def _delta(do, o, name):
    c = o.shape[1]

    def fn(do, o):
        p = do * o
        out = [jnp.broadcast_to(jnp.sum(p[:, j:j + ATT_HEAD_DIM], axis=-1, keepdims=True), (p.shape[0], ATT_HEAD_DIM))
               for j in range(0, c, ATT_HEAD_DIM)]
        return (jnp.concatenate(out, axis=1),)
    return _rowwise(fn, [do, o], [], [(c, F32)], tile=256, name=name)[0]


def _sum_slabs(recv, name):
    def body(r_ref, o_ref):
        acc = r_ref[0]
        for k in range(1, NDEV):
            acc = acc + r_ref[k]
        o_ref[...] = acc
    return pl.pallas_call(body, name=name, out_shape=jax.ShapeDtypeStruct(recv.shape[1:], F32),
                          compiler_params=_params())(recv)


def _shift_down(x, k):
    if k == 0:
        return x
    row = lax.broadcasted_iota(jnp.int32, x.shape, 0)
    return jnp.where(row >= k, pltpu.roll(x, k, 0), 0.0)


def _shift_up(x, k):
    if k == 0:
        return x
    s = x.shape[0]
    row = lax.broadcasted_iota(jnp.int32, x.shape, 0)
    return jnp.where(row < s - k, pltpu.roll(x, s - k, 0), 0.0)


def _conv(x, w):
    kw = w.shape[0]
    return functools.reduce(jnp.add, [w[k:k + 1, :] * _shift_down(x, kw - 1 - k) for k in range(kw)])


def _conv_t(dy, w):
    kw = w.shape[0]
    return functools.reduce(jnp.add, [w[k:k + 1, :] * _shift_up(dy, kw - 1 - k) for k in range(kw)])


def _conv_dw(x, dy, dw_ref):
    kw = dw_ref.shape[0]
    for k in range(kw):
        dw_ref[k:k + 1, :] = jnp.sum(dy * _shift_down(x, kw - 1 - k), axis=0, keepdims=True)


def _dsilu(pre):
    sig = jax.nn.sigmoid(pre)
    return sig * (1.0 + pre * (1.0 - sig))


def _col_specs(s, c, kw, tc):
    return (pl.BlockSpec((s, tc), lambda j: (0, j)), pl.BlockSpec((kw, tc), lambda j: (0, j)),
            pl.BlockSpec((1, tc), lambda j: (0, j)))


def _conv_silu_fwd(x, w, b, name):
    s, c = x.shape
    tc = LANES
    xs, ws, bs = _col_specs(s, c, w.shape[0], tc)

    def body(x_ref, w_ref, b_ref, o_ref):
        pre = _conv(x_ref[...], w_ref[...]) + b_ref[...]
        o_ref[...] = pre * jax.nn.sigmoid(pre)
    return pl.pallas_call(body, name=name, grid=(c // tc,), in_specs=[xs, ws, bs], out_specs=xs,
                          out_shape=jax.ShapeDtypeStruct((s, c), F32), compiler_params=_params(("parallel",)))(x, w, b)


def _conv_silu_bwd(x, w, b, dy, name):
    s, c = x.shape
    tc = LANES
    xs, ws, bs = _col_specs(s, c, w.shape[0], tc)

    def body(x_ref, w_ref, b_ref, dy_ref, dx_ref, dw_ref, db_ref):
        xv, wv = x_ref[...], w_ref[...]
        pre = _conv(xv, wv) + b_ref[...]
        dpre = dy_ref[...] * _dsilu(pre)
        dx_ref[...] = _conv_t(dpre, wv).astype(dx_ref.dtype)
        _conv_dw(xv, dpre, dw_ref)
        db_ref[...] = jnp.sum(dpre, axis=0, keepdims=True)
    return pl.pallas_call(
        body, name=name, grid=(c // tc,), in_specs=[xs, ws, bs, xs], out_specs=[xs, ws, bs],
        out_shape=[jax.ShapeDtypeStruct((s, c), BF16), jax.ShapeDtypeStruct(w.shape, F32),
                   jax.ShapeDtypeStruct((1, c), F32)],
        compiler_params=_params(("parallel",)))(x, w, b, dy)


def _ffn_gate_fwd(ug, uv, wg, wv, name):
    s, c = ug.shape
    tc = LANES
    xs, ws, _ = _col_specs(s, c, wg.shape[0], tc)

    def body(g_ref, v_ref, wg_ref, wv_ref, o_ref):
        g = _conv(g_ref[...], wg_ref[...])
        v = _conv(v_ref[...], wv_ref[...])
        o_ref[...] = (g * jax.nn.sigmoid(g) * v).astype(o_ref.dtype)
    return pl.pallas_call(body, name=name, grid=(c // tc,), in_specs=[xs, xs, ws, ws], out_specs=xs,
                          out_shape=jax.ShapeDtypeStruct((s, c), BF16),
                          compiler_params=_params(("parallel",)))(ug, uv, wg, wv)


def _ffn_gate_bwd(ug, uv, wg, wv, df, name):
    s, c = ug.shape
    tc = LANES
    xs, ws, _ = _col_specs(s, c, wg.shape[0], tc)

    def body(g_ref, v_ref, wg_ref, wv_ref, df_ref, dg_ref, dv_ref, dwg_ref, dwv_ref):
        gp, vp, wgv, wvv = g_ref[...], v_ref[...], wg_ref[...], wv_ref[...]
        g = _conv(gp, wgv)
        v = _conv(vp, wvv)
        dfv = df_ref[...]
        dg = dfv * v * _dsilu(g)
        dv = dfv * (g * jax.nn.sigmoid(g))
        dg_ref[...] = _conv_t(dg, wgv).astype(dg_ref.dtype)
        dv_ref[...] = _conv_t(dv, wvv).astype(dv_ref.dtype)
        _conv_dw(gp, dg, dwg_ref)
        _conv_dw(vp, dv, dwv_ref)
    return pl.pallas_call(
        body, name=name, grid=(c // tc,), in_specs=[xs, xs, ws, ws, xs], out_specs=[xs, xs, ws, ws],
        out_shape=[jax.ShapeDtypeStruct((s, c), BF16), jax.ShapeDtypeStruct((s, c), BF16),
                   jax.ShapeDtypeStruct(wg.shape, F32), jax.ShapeDtypeStruct(wv.shape, F32)],
        compiler_params=_params(("parallel",)))(ug, uv, wg, wv, df)


def _ssd_common(dt, alog, n_heads):
    ln = dt.shape[0]
    lane = lax.broadcasted_iota(jnp.int32, (1, LANES), 1)
    a = jnp.where(lane < n_heads, -jnp.exp(alog), 0.0)
    row = lax.broadcasted_iota(jnp.int32, (ln, ln), 0)
    col = lax.broadcasted_iota(jnp.int32, (ln, ln), 1)
    tril = col <= row
    acs = _dot(tril.astype(F32), dt * a, precision=HIGHEST)
    return a, acs, acs.T, tril


def _ssd_fwd(xbc, dt, alog, dskip, di, n_heads, n_groups, name):
    s, convd = xbc.shape
    ln, p, ns = SSM_CHUNK, SSM_HEAD_DIM, SSM_D_STATE
    nc, hg = s // ln, n_heads // n_groups

    def body(x_ref, dt_ref, alog_ref, d_ref, y_ref, prev_ref, st):
        @pl.when(pl.program_id(0) == 0)
        def _():
            st[...] = jnp.zeros(st.shape, F32)

        dt = dt_ref[...]
        _, acs, acs_t, tril = _ssd_common(dt, alog_ref[...], n_heads)
        e_all = jnp.exp(acs)
        last = acs[ln - 1:ln, :]
        ds_all = jnp.exp(last - acs)
        t_all = jnp.exp(last)
        dsk = d_ref[...]
        for g in range(n_groups):
            bg = x_ref[:, di + g * ns:di + (g + 1) * ns].astype(BF16)
            cg = x_ref[:, di + (n_groups + g) * ns:di + (n_groups + g + 1) * ns].astype(BF16)
            gm = _dot(cg, bg, _NT)
            for j in range(hg):
                h = g * hg + j
                xh = x_ref[:, h * p:(h + 1) * p]
                xdt = xh * dt[:, h:h + 1]
                seg = acs[:, h:h + 1] - acs_t[h:h + 1, :]
                m = jnp.where(tril, gm * jnp.exp(jnp.where(tril, seg, 0.0)), 0.0)
                prev = st[h]
                prev_ref[0, h] = prev
                y = _dot(m.astype(BF16), xdt.astype(BF16))
                y = y + _dot(cg, prev.astype(BF16), _NT) * e_all[:, h:h + 1]
                y = y + xh * dsk[:, h:h + 1]
                snew = _dot((xdt * ds_all[:, h:h + 1]).astype(BF16), bg, _TN)
                st[h] = prev * t_all[:, h:h + 1] + snew
                y_ref[:, h * p:(h + 1) * p] = y

    vec = pl.BlockSpec((1, LANES), lambda c: (0, 0))
    return pl.pallas_call(
        body, name=name, grid=(nc,),
        in_specs=[pl.BlockSpec((ln, convd), lambda c: (c, 0)), pl.BlockSpec((ln, LANES), lambda c: (c, 0)), vec, vec],
        out_specs=[pl.BlockSpec((ln, di), lambda c: (c, 0)),
                   pl.BlockSpec((1, n_heads, p, ns), lambda c: (c, 0, 0, 0))],
        out_shape=[jax.ShapeDtypeStruct((s, di), F32), jax.ShapeDtypeStruct((nc, n_heads, p, ns), F32)],
        scratch_shapes=[pltpu.VMEM((n_heads, p, ns), F32)],
        compiler_params=_params(("arbitrary",)))(xbc, dt, alog, dskip)


def _ssd_bwd(xbc, dt, alog, dskip, prev_all, dy, di, n_heads, n_groups, name):
    s, convd = xbc.shape
    ln, p, ns = SSM_CHUNK, SSM_HEAD_DIM, SSM_D_STATE
    nc, hg = s // ln, n_heads // n_groups

    def body(x_ref, dt_ref, alog_ref, d_ref, prev_ref, dy_ref, dx_ref, ddt_ref, da_ref, dd_ref, dh):
        step = pl.program_id(0)

        @pl.when(step == 0)
        def _():
            dh[...] = jnp.zeros(dh.shape, F32)
            da_ref[...] = jnp.zeros(da_ref.shape, F32)
            dd_ref[...] = jnp.zeros(dd_ref.shape, F32)

        dt = dt_ref[...]
        a, acs, acs_t, tril = _ssd_common(dt, alog_ref[...], n_heads)
        e_all = jnp.exp(acs)
        last = acs[ln - 1:ln, :]
        ds_all = jnp.exp(last - acs)
        t_all = jnp.exp(last)
        dsk = d_ref[...]
        lane = lax.broadcasted_iota(jnp.int32, (ln, LANES), 1)
        lane1 = lax.broadcasted_iota(jnp.int32, (1, LANES), 1)
        sub = lax.broadcasted_iota(jnp.int32, (LANES, ln), 0)
        rowi = lax.broadcasted_iota(jnp.int32, (ln, LANES), 0)
        dacs_c = jnp.zeros((ln, LANES), F32)
        dacs_r = jnp.zeros((LANES, ln), F32)
        dlast = jnp.zeros((1, LANES), F32)
        ddt_x = jnp.zeros((ln, LANES), F32)
        dd = jnp.zeros((1, LANES), F32)

        def tot(v):
            return jnp.sum(jnp.sum(v, axis=1, keepdims=True), axis=0, keepdims=True)

        for g in range(n_groups):
            bg = x_ref[:, di + g * ns:di + (g + 1) * ns].astype(BF16)
            cg = x_ref[:, di + (n_groups + g) * ns:di + (n_groups + g + 1) * ns].astype(BF16)
            gm = _dot(cg, bg, _NT)
            dgm = jnp.zeros((ln, ln), F32)
            dcg = jnp.zeros((ln, ns), F32)
            dbg = jnp.zeros((ln, ns), F32)
            for j in range(hg):
                h = g * hg + j
                xh = x_ref[:, h * p:(h + 1) * p]
                dth = dt[:, h:h + 1]
                xdt = xh * dth
                dyh = dy_ref[:, h * p:(h + 1) * p]
                eh, dsh, th = e_all[:, h:h + 1], ds_all[:, h:h + 1], t_all[:, h:h + 1]
                seg = acs[:, h:h + 1] - acs_t[h:h + 1, :]
                dec = jnp.where(tril, jnp.exp(jnp.where(tril, seg, 0.0)), 0.0)
                m = gm * dec
                prev = prev_ref[0, h]
                dhn = dh[h]
                prevb, dhb, dyb, xdtb = prev.astype(BF16), dhn.astype(BF16), dyh.astype(BF16), xdt.astype(BF16)
                yo = _dot(cg, prevb, _NT)
                dyob = (dyh * eh).astype(BF16)
                c_col = jnp.sum(dyh * yo, axis=1, keepdims=True) * eh
                dcg = dcg + _dot(dyob, prevb)
                dprev = th * dhn + _dot(dyob, cg, _TN)
                dtt = tot(dhn * prev)
                w = _dot(bg, dhb, _NT)
                dxdt = w * dsh
                dds = jnp.sum(w * xdt, axis=1, keepdims=True)
                dbg = dbg + _dot((xdt * dsh).astype(BF16), dhb)
                dm = _dot(dyb, xdtb, _NT)
                dxdt = dxdt + _dot(m.astype(BF16), dyb, _TN)
                dgm = dgm + dm * dec
                q = dm * m
                c_col = c_col + jnp.sum(q, axis=1, keepdims=True) - dds * dsh
                r_row = -jnp.sum(q, axis=0, keepdims=True)
                dlast_h = tot(dds * dsh) + dtt * th
                dacs_c = dacs_c + jnp.where(lane == h, c_col, 0.0)
                dacs_r = dacs_r + jnp.where(sub == h, r_row, 0.0)
                dlast = dlast + jnp.where(lane1 == h, dlast_h, 0.0)
                ddt_x = ddt_x + jnp.where(lane == h, jnp.sum(dxdt * xh, axis=1, keepdims=True), 0.0)
                dd = dd + jnp.where(lane1 == h, tot(dyh * xh), 0.0)
                dx_ref[:, h * p:(h + 1) * p] = dxdt * dth + dyh * dsk[:, h:h + 1]
                dh[h] = dprev
            dgb = dgm.astype(BF16)
            dx_ref[:, di + g * ns:di + (g + 1) * ns] = dbg + _dot(dgb, cg, _TN)
            dx_ref[:, di + (n_groups + g) * ns:di + (n_groups + g + 1) * ns] = dcg + _dot(dgb, bg)

        dacs = dacs_c + dacs_r.T + jnp.where(rowi == ln - 1, dlast, 0.0)
        row = lax.broadcasted_iota(jnp.int32, (ln, ln), 0)
        col = lax.broadcasted_iota(jnp.int32, (ln, ln), 1)
        dadt = _dot((col >= row).astype(F32), dacs, precision=HIGHEST)
        ddt_ref[...] = dadt * a + ddt_x
        da_ref[...] += jnp.sum(dadt * dt, axis=0, keepdims=True)
        dd_ref[...] += dd

        @pl.when(step == nc - 1)
        def _():
            da_ref[...] = da_ref[...] * a

    vec = pl.BlockSpec((1, LANES), lambda c: (0, 0))
    rev = lambda c: (nc - 1 - c, 0)
    return pl.pallas_call(
        body, name=name, grid=(nc,),
        in_specs=[pl.BlockSpec((ln, convd), rev), pl.BlockSpec((ln, LANES), rev), vec, vec,
                  pl.BlockSpec((1, n_heads, p, ns), lambda c: (nc - 1 - c, 0, 0, 0)), pl.BlockSpec((ln, di), rev)],
        out_specs=[pl.BlockSpec((ln, convd), rev), pl.BlockSpec((ln, LANES), rev), vec, vec],
        out_shape=[jax.ShapeDtypeStruct((s, convd), F32), jax.ShapeDtypeStruct((s, LANES), F32),
                   jax.ShapeDtypeStruct((1, LANES), F32), jax.ShapeDtypeStruct((1, LANES), F32)],
        scratch_shapes=[pltpu.VMEM((n_heads, p, ns), F32)],
        compiler_params=_params(("arbitrary",)))(xbc, dt, alog, dskip, prev_all, dy)


def _split(x, n):
    out = []
    for _ in range(n):
        piece = x.astype(BF16)
        out.append(piece)
        x = x - piece.astype(F32)
    return out


def _spread(x, onehot, n=2):
    return functools.reduce(jnp.add, [_dot(piece, onehot) for piece in _split(x, n)])


def _head_maps(di, p):
    e = (jnp.arange(di, dtype=jnp.int32)[None, :] // p == jnp.arange(LANES, dtype=jnp.int32)[:, None]).astype(BF16)
    return e, e.T


def _ssd_wide(dt, acs, acs_t, dskip, e_ref, et_ref):
    ln = dt.shape[0]
    last = acs[ln - 1:ln, :]
    stack = jnp.concatenate([dt, jnp.exp(acs), jnp.exp(last - acs), jnp.broadcast_to(dskip, (8, LANES))], axis=0)
    wide = _spread(stack, e_ref[...])
    tb = jnp.exp(jnp.broadcast_to(acs_t[:, ln - 1:ln], (LANES, LANES)))
    texp = functools.reduce(jnp.add, [_dot(et_ref[...], piece) for piece in _split(tb, 3)])
    return wide[:ln], wide[ln:2 * ln], wide[2 * ln:3 * ln], wide[3 * ln:3 * ln + 1], texp


def _ssd_fwd2(xbc, dt, alog, dskip, di, n_heads, n_groups, name):
    s, convd = xbc.shape
    ln, p, ns = SSM_CHUNK, SSM_HEAD_DIM, SSM_D_STATE
    nc, hg = s // ln, n_heads // n_groups
    gw = hg * p
    e64, e64t = _head_maps(di, p)

    def body(x_ref, dt_ref, alog_ref, d_ref, e_ref, et_ref, y_ref, prev_ref, st):
        @pl.when(pl.program_id(0) == 0)
        def _():
            st[...] = jnp.zeros(st.shape, F32)

        dt = dt_ref[...]
        _, acs, acs_t, tril = _ssd_common(dt, alog_ref[...], n_heads)
        dte, ee, dse, dske, texp = _ssd_wide(dt, acs, acs_t, d_ref[...], e_ref, et_ref)
        x = x_ref[:, :di]
        xdt = x * dte
        xdtb = xdt.astype(BF16)
        xdsb = (xdt * dse).astype(BF16)
        for g in range(n_groups):
            rows = slice(g * gw, (g + 1) * gw)
            bg = x_ref[:, di + g * ns:di + (g + 1) * ns].astype(BF16)
            cg = x_ref[:, di + (n_groups + g) * ns:di + (n_groups + g + 1) * ns].astype(BF16)
            gm = _dot(cg, bg, _NT)
            prev = st[rows, :]
            prev_ref[0, rows, :] = prev
            yo = _dot(cg, prev.astype(BF16), _NT)
            for j in range(hg):
                h = g * hg + j
                seg = acs[:, h:h + 1] - acs_t[h:h + 1, :]
                m = jnp.where(tril, gm * jnp.exp(jnp.where(tril, seg, 0.0)), 0.0)
                y_ref[:, h * p:(h + 1) * p] = _dot(m.astype(BF16), xdtb[:, h * p:(h + 1) * p])
            y_ref[:, rows] = y_ref[:, rows] + yo * ee[:, rows] + x[:, rows] * dske[:, rows]
            st[rows, :] = prev * texp[rows, :] + _dot(xdsb[:, rows], bg, _TN)

    vec = pl.BlockSpec((1, LANES), lambda c: (0, 0))
    return pl.pallas_call(
        body, name=name, grid=(nc,),
        in_specs=[pl.BlockSpec((ln, convd), lambda c: (c, 0)), pl.BlockSpec((ln, LANES), lambda c: (c, 0)), vec, vec,
                  pl.BlockSpec(e64.shape, lambda c: (0, 0)), pl.BlockSpec(e64t.shape, lambda c: (0, 0))],
        out_specs=[pl.BlockSpec((ln, di), lambda c: (c, 0)), pl.BlockSpec((1, di, ns), lambda c: (c, 0, 0))],
        out_shape=[jax.ShapeDtypeStruct((s, di), F32), jax.ShapeDtypeStruct((nc, di, ns), F32)],
        scratch_shapes=[pltpu.VMEM((di, ns), F32)],
        compiler_params=_params(("arbitrary",)))(xbc, dt, alog, dskip, e64, e64t)


def _ssd_bwd2(xbc, dt, alog, dskip, prev_all, dy, di, n_heads, n_groups, name):
    s, convd = xbc.shape
    ln, p, ns = SSM_CHUNK, SSM_HEAD_DIM, SSM_D_STATE
    nc, hg = s // ln, n_heads // n_groups
    gw = hg * p
    e64, e64t = _head_maps(di, p)

    def body(x_ref, dt_ref, alog_ref, d_ref, e_ref, et_ref, prev_ref, dy_ref,
             dx_ref, ddt_ref, da_ref, dd_ref, dh, yo_ref, w_ref):
        step = pl.program_id(0)

        @pl.when(step == 0)
        def _():
            dh[...] = jnp.zeros(dh.shape, F32)
            da_ref[...] = jnp.zeros(da_ref.shape, F32)
            dd_ref[...] = jnp.zeros(dd_ref.shape, F32)

        dt = dt_ref[...]
        a, acs, acs_t, tril = _ssd_common(dt, alog_ref[...], n_heads)
        dte, ee, dse, dske, texp = _ssd_wide(dt, acs, acs_t, d_ref[...], e_ref, et_ref)
        row = lax.broadcasted_iota(jnp.int32, (ln, ln), 0)
        col = lax.broadcasted_iota(jnp.int32, (ln, ln), 1)
        triu = col >= row
        x = x_ref[:, :di]
        dy = dy_ref[...]
        xdt = x * dte
        xdtb = xdt.astype(BF16)
        xdsb = (xdt * dse).astype(BF16)
        dyb = dy.astype(BF16)
        dyob = (dy * ee).astype(BF16)
        dhn = dh[...]
        dhb = dhn.astype(BF16)
        per_head = functools.reduce(jnp.add, [_dot(e_ref[...], piece) for piece in _split(dhn * prev_ref[0], 2)])
        ones8 = jnp.ones((8, LANES), BF16)
        dtt = functools.reduce(jnp.add, [_dot(ones8, piece, _NT) for piece in _split(per_head, 2)])[0:1]
        dacs_c = jnp.zeros((ln, LANES), F32)
        dacs_r = jnp.zeros((LANES, ln), F32)
        for g in range(n_groups):
            rows = slice(g * gw, (g + 1) * gw)
            bg = x_ref[:, di + g * ns:di + (g + 1) * ns].astype(BF16)
            cg = x_ref[:, di + (n_groups + g) * ns:di + (n_groups + g + 1) * ns].astype(BF16)
            gmt = _dot(bg, cg, _NT)
            prevb = prev_ref[0, rows, :].astype(BF16)
            dcg = _dot(dyob[:, rows], prevb)
            dh[rows, :] = texp[rows, :] * dhn[rows, :] + _dot(dyob[:, rows], cg, _TN)
            w = _dot(bg, dhb[rows, :], _NT)
            dbg = _dot(xdsb[:, rows], dhb[rows, :])
            yo_ref[:, rows] = _dot(cg, prevb, _NT)
            w_ref[:, rows] = w
            dgmt = jnp.zeros((ln, ln), F32)
            q_hi, q_lo = [], []
            for j in range(hg):
                h = g * hg + j
                segt = acs_t[h:h + 1, :] - acs[:, h:h + 1]
                dect = jnp.where(triu, jnp.exp(jnp.where(triu, segt, 0.0)), 0.0)
                dyh, xh = dyb[:, h * p:(h + 1) * p], xdtb[:, h * p:(h + 1) * p]
                mt = gmt * dect
                dmt = _dot(xh, dyh, _NT)
                dx_ref[:, h * p:(h + 1) * p] = _dot(mt.astype(BF16), dyh)
                dgmt = dgmt + dmt * dect
                hi, lo = _split(dmt * mt, 2)
                q_hi.append(hi)
                q_lo.append(lo)
            sel_c = (lax.broadcasted_iota(jnp.int32, (hg * ln, LANES), 1)
                     == g * hg + lax.broadcasted_iota(jnp.int32, (hg * ln, LANES), 0) // ln).astype(BF16)
            sel_r = (lax.broadcasted_iota(jnp.int32, (LANES, hg * ln), 0)
                     == g * hg + lax.broadcasted_iota(jnp.int32, (LANES, hg * ln), 1) // ln).astype(BF16)
            for pieces in (q_hi, q_lo):
                dacs_c = dacs_c - _dot(jnp.concatenate(pieces, axis=1), sel_c)
                dacs_r = dacs_r + _dot(sel_r, jnp.concatenate(pieces, axis=0))
            dgb = dgmt.astype(BF16)
            dx_ref[:, di + g * ns:di + (g + 1) * ns] = dbg + _dot(dgb, cg)
            dx_ref[:, di + (n_groups + g) * ns:di + (n_groups + g + 1) * ns] = dcg + _dot(dgb, bg, _TN)

        wds = w_ref[...] * dse
        dxdt = dx_ref[:, :di] + wds
        red = _spread(jnp.concatenate([dxdt * x, dy * yo_ref[...] * ee, xdt * wds, dy * x], axis=0), et_ref[...])
        ddt_x, r_off, r_state, ddr = red[:ln], red[ln:2 * ln], red[2 * ln:3 * ln], red[3 * ln:]
        dx_ref[:, :di] = dxdt * dte + dy * dske
        rowi = lax.broadcasted_iota(jnp.int32, (ln, LANES), 0)
        dlast = jnp.sum(r_state, axis=0, keepdims=True) + dtt * jnp.exp(acs[ln - 1:ln, :])
        dacs = r_off - r_state + dacs_c + dacs_r.T + jnp.where(rowi == ln - 1, dlast, 0.0)
        dadt = _dot(triu.astype(F32), dacs, precision=HIGHEST)
        ddt_ref[...] = dadt * a + ddt_x
        da_ref[...] += jnp.sum(dadt * dt, axis=0, keepdims=True)
        dd_ref[...] += jnp.sum(ddr, axis=0, keepdims=True)

        @pl.when(step == nc - 1)
        def _():
            da_ref[...] = da_ref[...] * a

    vec = pl.BlockSpec((1, LANES), lambda c: (0, 0))
    rev = lambda c: (nc - 1 - c, 0)
    return pl.pallas_call(
        body, name=name, grid=(nc,),
        in_specs=[pl.BlockSpec((ln, convd), rev), pl.BlockSpec((ln, LANES), rev), vec, vec,
                  pl.BlockSpec(e64.shape, lambda c: (0, 0)), pl.BlockSpec(e64t.shape, lambda c: (0, 0)),
                  pl.BlockSpec((1, di, ns), lambda c: (nc - 1 - c, 0, 0)), pl.BlockSpec((ln, di), rev)],
        out_specs=[pl.BlockSpec((ln, convd), rev), pl.BlockSpec((ln, LANES), rev), vec, vec],
        out_shape=[jax.ShapeDtypeStruct((s, convd), F32), jax.ShapeDtypeStruct((s, LANES), F32),
                   jax.ShapeDtypeStruct((1, LANES), F32), jax.ShapeDtypeStruct((1, LANES), F32)],
        scratch_shapes=[pltpu.VMEM((di, ns), F32), pltpu.VMEM((ln, di), F32), pltpu.VMEM((ln, di), F32)],
        compiler_params=_params(("arbitrary",)))(xbc, dt, alog, dskip, e64, e64t, prev_all, dy)


def _perm(a, d):
    if d == 1:
        return a
    s = a.shape[0]
    return a.reshape(s // d, d, -1).transpose(1, 0, 2).reshape(s, -1)


def _unperm(a, d):
    if d == 1:
        return a
    s = a.shape[0]
    return a.reshape(d, s // d, -1).transpose(1, 0, 2).reshape(s, -1)


def _rot_tables(s, d):
    half = ROPE_DIM // 2
    inv_freq = jnp.power(jnp.float32(ROPE_THETA), -jnp.arange(0, ROPE_DIM, 2, dtype=F32) / ROPE_DIM)
    v = jnp.arange(s, dtype=jnp.int32)
    pos = (v % (s // d)) * d + v // (s // d)
    ang = pos.astype(F32)[:, None] * inv_freq[None, :]
    cos, sin = jnp.cos(ang), jnp.sin(ang)
    zero = jnp.zeros((s, ATT_HEAD_DIM - ROPE_DIM), F32)
    cf = jnp.concatenate([cos, cos, jnp.ones_like(zero)], axis=1)
    s1 = jnp.concatenate([-sin, jnp.zeros_like(sin), zero], axis=1)
    s2 = jnp.concatenate([jnp.zeros_like(sin), sin, zero], axis=1)
    assert half * 2 == ROPE_DIM
    return cf, s1, s2


def _rot(x, tabs, sign):
    cf, s1, s2 = tabs
    half = ROPE_DIM // 2
    left = pltpu.roll(x, ATT_HEAD_DIM - half, 1)
    right = pltpu.roll(x, half, 1)
    return x * cf + sign * (left * s1 + right * s2)


def _att_masks(n, n_blk, rep):
    b = ATT_BLOCK
    row = lax.broadcasted_iota(jnp.int32, (rep * b, b), 0) & (b - 1)
    col = lax.broadcasted_iota(jnp.int32, (rep * b, b), 1)
    off = jnp.where(n % n_blk != 0, 0, 2 * b)
    return col <= row, col >= row + off


def _stack(x, rep):
    return jnp.concatenate([x[:, j * ATT_HEAD_DIM:(j + 1) * ATT_HEAD_DIM] for j in range(rep)], axis=0)


def _att_specs(nb, rep, cur, prv):
    b, hd = ATT_BLOCK, ATT_HEAD_DIM
    q_spec = pl.BlockSpec((b, rep * hd), lambda h, n: (cur(n), h))
    kc_spec = pl.BlockSpec((b, hd), lambda h, n: (cur(n), h))
    kp_spec = pl.BlockSpec((b, hd), lambda h, n: (prv(n), h))
    tc_spec = pl.BlockSpec((b, hd), lambda h, n: (cur(n), 0))
    tp_spec = pl.BlockSpec((b, hd), lambda h, n: (prv(n), 0))
    return q_spec, kc_spec, kp_spec, tc_spec, tp_spec


def _attn_fwd(q, k, v, tabs, n_blk, name):
    s = q.shape[0]
    b, hd = ATT_BLOCK, ATT_HEAD_DIM
    nb = s // b
    n_kv = ATT_KV_HEADS_PER_GROUP
    rep = ATT_HEADS_PER_GROUP // n_kv
    scale = hd ** -0.5

    def body(q_ref, kc_ref, kp_ref, vc_ref, vp_ref, cfc, s1c, s2c, cfp, s1p, s2p, o_ref, lse_ref):
        n = pl.program_id(1)
        tc = (cfc[...], s1c[...], s2c[...])
        tp = (cfp[...], s1p[...], s2p[...])
        qv = q_ref[...]
        q4 = jnp.concatenate([_rot(qv[:, j * hd:(j + 1) * hd], tc, 1.0) for j in range(rep)], axis=0).astype(BF16)
        kc = _rot(kc_ref[...], tc, 1.0).astype(BF16)
        kp = _rot(kp_ref[...], tp, 1.0).astype(BF16)
        mc, mp = _att_masks(n, n_blk, rep)
        sc = jnp.where(mc, _dot(q4, kc, _NT) * scale, NEG)
        sp = jnp.where(mp, _dot(q4, kp, _NT) * scale, NEG)
        m = jnp.maximum(jnp.max(sc, axis=1, keepdims=True), jnp.max(sp, axis=1, keepdims=True))
        pc, pp = jnp.exp(sc - m), jnp.exp(sp - m)
        l = jnp.sum(pc, axis=1, keepdims=True) + jnp.sum(pp, axis=1, keepdims=True)
        o = (_dot(pc.astype(BF16), vc_ref[...].astype(BF16)) + _dot(pp.astype(BF16), vp_ref[...].astype(BF16))) / l
        lse = jnp.broadcast_to(m + jnp.log(l), (rep * b, hd))
        for j in range(rep):
            o_ref[:, j * hd:(j + 1) * hd] = o[j * b:(j + 1) * b]
            lse_ref[:, j * hd:(j + 1) * hd] = lse[j * b:(j + 1) * b]

    cur = lambda n: n
    prv = lambda n: jnp.maximum(n - 1, 0)
    q_spec, kc_spec, kp_spec, tc_spec, tp_spec = _att_specs(nb, rep, cur, prv)
    return pl.pallas_call(
        body, name=name, grid=(n_kv, nb),
        in_specs=[q_spec, kc_spec, kp_spec, kc_spec, kp_spec, tc_spec, tc_spec, tc_spec, tp_spec, tp_spec, tp_spec],
        out_specs=[q_spec, q_spec],
        out_shape=[jax.ShapeDtypeStruct(q.shape, F32), jax.ShapeDtypeStruct(q.shape, F32)],
        compiler_params=_params(("parallel", "arbitrary")))(q, k, k, v, v, *tabs, *tabs)


def _attn_bwd(q, k, v, do, lse, delta, tabs, n_blk, name):
    s = q.shape[0]
    b, hd = ATT_BLOCK, ATT_HEAD_DIM
    nb = s // b
    n_kv = ATT_KV_HEADS_PER_GROUP
    rep = ATT_HEADS_PER_GROUP // n_kv
    scale = hd ** -0.5

    def body(q_ref, do_ref, lse_ref, dl_ref, kc_ref, kp_ref, vc_ref, vp_ref, cfc, s1c, s2c, cfp, s1p, s2p,
             dq_ref, dk_ref, dv_ref, ck, cv):
        n = pl.program_id(1)
        tp = (cfp[...], s1p[...], s2p[...])

        @pl.when(n == 0)
        def _():
            ck[...] = jnp.zeros(ck.shape, F32)
            cv[...] = jnp.zeros(cv.shape, F32)

        @pl.when(n < nb)
        def _():
            tc = (cfc[...], s1c[...], s2c[...])
            qv = q_ref[...]
            q4 = jnp.concatenate([_rot(qv[:, j * hd:(j + 1) * hd], tc, 1.0) for j in range(rep)],
                                 axis=0).astype(BF16)
            do4 = _stack(do_ref[...], rep).astype(BF16)
            lse4 = _stack(lse_ref[...], rep)
            dl4 = _stack(dl_ref[...], rep)
            kc = _rot(kc_ref[...], tc, 1.0).astype(BF16)
            kp = _rot(kp_ref[...], tp, 1.0).astype(BF16)
            vc, vp = vc_ref[...].astype(BF16), vp_ref[...].astype(BF16)
            mc, mp = _att_masks(n, n_blk, rep)
            pc = jnp.where(mc, jnp.exp(jnp.where(mc, _dot(q4, kc, _NT) * scale - lse4, 0.0)), 0.0)
            pp = jnp.where(mp, jnp.exp(jnp.where(mp, _dot(q4, kp, _NT) * scale - lse4, 0.0)), 0.0)
            dsc = (pc * (_dot(do4, vc, _NT) - dl4)).astype(BF16)
            dsp = (pp * (_dot(do4, vp, _NT) - dl4)).astype(BF16)
            dq4 = (_dot(dsc, kc) + _dot(dsp, kp)) * scale
            for j in range(rep):
                dq_ref[:, j * hd:(j + 1) * hd] = _rot(dq4[j * b:(j + 1) * b], tc, -1.0).astype(dq_ref.dtype)
            dk_prev = ck[...] + _dot(dsp, q4, _TN) * scale
            dv_prev = cv[...] + _dot(pp.astype(BF16), do4, _TN)
            dk_ref[...] = _rot(dk_prev, tp, -1.0).astype(dk_ref.dtype)
            dv_ref[...] = dv_prev.astype(dv_ref.dtype)
            ck[...] = _dot(dsc, q4, _TN) * scale
            cv[...] = _dot(pc.astype(BF16), do4, _TN)

        @pl.when(n == nb)
        def _():
            dk_ref[...] = _rot(ck[...], tp, -1.0).astype(dk_ref.dtype)
            dv_ref[...] = cv[...].astype(dv_ref.dtype)

    cur = lambda n: jnp.minimum(n, nb - 1)
    prv = lambda n: jnp.maximum(n - 1, 0)
    q_spec, kc_spec, kp_spec, tc_spec, tp_spec = _att_specs(nb, rep, cur, prv)
    return pl.pallas_call(
        body, name=name, grid=(n_kv, nb + 1),
        in_specs=[q_spec, q_spec, q_spec, q_spec, kc_spec, kp_spec, kc_spec, kp_spec,
                  tc_spec, tc_spec, tc_spec, tp_spec, tp_spec, tp_spec],
        out_specs=[q_spec, kp_spec, kp_spec],
        out_shape=[jax.ShapeDtypeStruct(q.shape, BF16), jax.ShapeDtypeStruct(k.shape, BF16),
                   jax.ShapeDtypeStruct(k.shape, BF16)],
        scratch_shapes=[pltpu.VMEM((b, hd), F32), pltpu.VMEM((b, hd), F32)],
        compiler_params=_params(("parallel", "arbitrary")))(q, do, lse, delta, k, k, v, v, *tabs, *tabs)


def _adamw(g_slabs, w, m, v, name):
    kk, r, c = g_slabs.shape
    tile = r if r <= 256 else _pick_rows(r, 256)

    def body(g_ref, w_ref, m_ref, v_ref, go_ref, d_ref, mo_ref, vo_ref):
        g = g_ref[0].astype(F32)
        for k in range(1, kk):
            g = g + g_ref[k].astype(F32)
        m2 = ADAM_B1 * m_ref[...] + (1.0 - ADAM_B1) * g
        v2 = ADAM_B2 * v_ref[...] + (1.0 - ADAM_B2) * jnp.square(g)
        m_hat = m2 / (1.0 - ADAM_B1 ** ADAM_STEP)
        v_hat = v2 / (1.0 - ADAM_B2 ** ADAM_STEP)
        go_ref[...] = g
        d_ref[...] = -ADAM_LR * (m_hat / (jnp.sqrt(v_hat) + ADAM_EPS) + ADAM_WD * w_ref[...])
        mo_ref[...] = m2
        vo_ref[...] = v2

    spec = pl.BlockSpec((tile, c), lambda i: (i, 0))
    return pl.pallas_call(
        body, name=name, grid=(r // tile,), in_specs=[pl.BlockSpec((kk, tile, c), lambda i: (0, i, 0)), spec, spec, spec],
        out_specs=[spec] * 4, out_shape=[jax.ShapeDtypeStruct((r, c), F32)] * 4,
        compiler_params=_params(("parallel",)))(g_slabs, w, m, v)


def _pick_rows(r, pref):
    t = (pref // 16) * 16
    while t >= 16:
        if r % t == 0:
            return t
        t -= 16
    return r


def _coords():
    return lax.axis_index("x"), lax.axis_index("y"), lax.axis_index("c")


def _dev_index(px, py, pc):
    return 4 * px + 2 * py + pc


def _all_gather(shards, name):
    na = len(shards)

    def body(*refs):
        ins, outs = refs[:na], refs[na:2 * na]
        send_sems, recv_sems, local_sems = refs[2 * na:]
        x, y, c = _coords()
        me, sibling = (x, y, c), (x, y, 1 - c)
        chips = [(1 - x, y), (x, 1 - y), (1 - x, 1 - y)]

        def copy(a, k, block, to, src=None):
            dst = outs[a].at[_dev_index(*block)]
            return pltpu.make_async_remote_copy(
                src_ref=dst if src is None else src, dst_ref=dst, send_sem=send_sems.at[a * 7 + k],
                recv_sem=recv_sems.at[a * 7 + k], device_id=to, device_id_type=MESH)

        mine = [pltpu.make_async_copy(ins[a], outs[a].at[_dev_index(*me)], local_sems.at[a]) for a in range(na)]
        for cp in mine:
            cp.start()
        first = []
        for a in range(na):
            first.append(copy(a, 0, me, sibling, src=ins[a]))
            first += [copy(a, 1 + j, me, (*chip, c), src=ins[a]) for j, chip in enumerate(chips)]
        for cp in first:
            cp.start()
        passed = []
        for j, chip in enumerate(chips):
            for a in range(na):
                copy(a, 1 + j, (*chip, c), me).wait_recv()
                cp = copy(a, 4 + j, (*chip, c), sibling)
                cp.start()
                passed.append(cp)
        for a in range(na):
            copy(a, 0, sibling, me).wait_recv()
            for j, chip in enumerate(chips):
                copy(a, 4 + j, (*chip, 1 - c), me).wait_recv()
        for cp in first + passed:
            cp.wait_send()
        for cp in mine:
            cp.wait()

    hbm = pl.BlockSpec(memory_space=pl.ANY)
    return pl.pallas_call(
        body, name=name, in_specs=[hbm] * na, out_specs=[hbm] * na,
        out_shape=[jax.ShapeDtypeStruct((NDEV,) + s.shape, s.dtype) for s in shards],
        scratch_shapes=[pltpu.SemaphoreType.DMA((7 * na,)), pltpu.SemaphoreType.DMA((7 * na,)),
                        pltpu.SemaphoreType.DMA((na,))])(*shards)


def _exchange(slabs, whole, name):
    ns, nw = len(slabs), len(whole)
    na = ns + nw

    def body(*refs):
        ins, outs = refs[:na], refs[na:2 * na]
        send_sems, recv_sems, local_sems = refs[2 * na:]
        x, y, c = _coords()
        me = _dev_index(x, y, c)

        def src_of(a, p):
            return ins[a].at[p] if a < ns else ins[a]

        def copy(a, k, peer):
            p = _dev_index(*peer)
            return pltpu.make_async_remote_copy(
                src_ref=src_of(a, p), dst_ref=outs[a].at[me], send_sem=send_sems.at[a * 7 + k - 1],
                recv_sem=recv_sems.at[a * 7 + k - 1], device_id=peer, device_id_type=MESH)

        def arrival(a, k, peer):
            p = _dev_index(*peer)
            return pltpu.make_async_remote_copy(
                src_ref=src_of(a, p), dst_ref=outs[a].at[p], send_sem=send_sems.at[a * 7 + k - 1],
                recv_sem=recv_sems.at[a * 7 + k - 1], device_id=peer, device_id_type=MESH)

        mine = [pltpu.make_async_copy(src_of(a, me), outs[a].at[me], local_sems.at[a]) for a in range(na)]
        for cp in mine:
            cp.start()
        peers = [(k, (x ^ (k >> 2), y ^ ((k >> 1) & 1), c ^ (k & 1))) for k in range(1, NDEV)]
        sent = [copy(a, k, peer) for k, peer in peers for a in range(na)]
        for cp in sent:
            cp.start()
        for k, peer in peers:
            for a in range(na):
                arrival(a, k, peer).wait_recv()
        for cp in sent:
            cp.wait_send()
        for cp in mine:
            cp.wait()

    hbm = pl.BlockSpec(memory_space=pl.ANY)
    out_shape = [jax.ShapeDtypeStruct(s.shape, s.dtype) for s in slabs]
    out_shape += [jax.ShapeDtypeStruct((NDEV,) + w.shape, w.dtype) for w in whole]
    return pl.pallas_call(
        body, name=name, in_specs=[hbm] * na, out_specs=[hbm] * na, out_shape=out_shape,
        scratch_shapes=[pltpu.SemaphoreType.DMA((7 * na,)), pltpu.SemaphoreType.DMA((7 * na,)),
                        pltpu.SemaphoreType.DMA((na,))])(*slabs, *whole)


_HBM = pl.BlockSpec(memory_space=pltpu.HBM)
_SEM = pl.BlockSpec(memory_space=pltpu.SEMAPHORE)
_EFFECT = pltpu.SideEffectType.DATAFLOW_SIDE_EFFECTING


def _peers(x, y, c):
    return [(k, (x ^ (k >> 2), y ^ ((k >> 1) & 1), c ^ (k & 1))) for k in range(1, NDEV)]


def _peer_copy(src, land, send_sems, recv_sems, a, k, dst_block, peer):
    return pltpu.make_async_remote_copy(
        src_ref=src, dst_ref=land.at[dst_block], send_sem=send_sems.at[a * 7 + k - 1],
        recv_sem=recv_sems.at[a * 7 + k - 1], device_id=peer, device_id_type=MESH)


def _send_start(arrays, slabs, name):
    na = len(arrays)
    lands = [jax.ShapeDtypeStruct(a.shape if slabs else (NDEV,) + a.shape, a.dtype) for a in arrays]

    def body(*refs):
        ins, zones = refs[:na], refs[na:2 * na]
        send_sems, recv_sems = refs[2 * na], refs[2 * na + 1]
        token = refs[-1]
        x, y, c = _coords()
        me = _dev_index(x, y, c)
        for k, peer in _peers(x, y, c):
            for a in range(na):
                src = ins[a].at[_dev_index(*peer)] if slabs else ins[a]
                _peer_copy(src, zones[a], send_sems, recv_sems, a, k, me, peer).start()
        token[...] = jnp.zeros_like(token)

    outs = pl.pallas_call(
        body, name=name,
        out_shape=(pltpu.SemaphoreType.DMA((7 * na,)), pltpu.SemaphoreType.DMA((7 * na,)),
                   *[pltpu.HBM(a.shape, a.dtype) for a in arrays], *[pltpu.HBM(l.shape, l.dtype) for l in lands],
                   jax.ShapeDtypeStruct((8, LANES), F32)),
        in_specs=[_HBM] * (2 * na), out_specs=(_SEM, _SEM, *([_HBM] * (2 * na)), pl.BlockSpec(memory_space=pltpu.VMEM)),
        input_output_aliases={i: 2 + i for i in range(2 * na)},
        compiler_params=pltpu.CompilerParams(has_side_effects=_EFFECT),
    )(*[pltpu.with_memory_space_constraint(a, pltpu.HBM) for a in arrays],
      *[pltpu.with_memory_space_constraint(lax.empty(l.shape, l.dtype), pltpu.HBM) for l in lands])
    return outs[0], outs[1], list(outs[2:2 + na]), list(outs[2 + na:2 + 2 * na]), outs[-1]


def _send_wait(started, after, slabs, name):
    send_sems, recv_sems, thru, zones, _ = started
    na = len(thru)

    def body(*refs):
        ins, lands = refs[:na], refs[na:2 * na]
        s_sems, r_sems = refs[2 * na], refs[2 * na + 1]
        x, y, c = _coords()
        for k, peer in _peers(x, y, c):
            p = _dev_index(*peer)
            for a in range(na):
                src = ins[a].at[p] if slabs else ins[a]
                cp = _peer_copy(src, lands[a], s_sems, r_sems, a, k, p, peer)
                cp.wait_send()
                cp.wait_recv()

    outs = pl.pallas_call(
        body, name=name, out_shape=tuple(pltpu.HBM(v.shape, v.dtype) for v in thru + zones),
        in_specs=[_HBM] * (2 * na) + [_SEM, _SEM, pl.BlockSpec(memory_space=pl.ANY)], out_specs=tuple([_HBM] * (2 * na)),
        input_output_aliases={i: i for i in range(2 * na)},
        compiler_params=pltpu.CompilerParams(has_side_effects=_EFFECT),
    )(*thru, *zones, send_sems, recv_sems, after)
    me = _dev_index(*_coords())
    filled = []
    for a in range(na):
        own = lax.dynamic_index_in_dim(outs[a], me, 0, keepdims=False) if slabs else outs[a]
        filled.append(lax.dynamic_update_index_in_dim(outs[na + a], own, me, 0))
    return filled


def _pack(vecs):
    parts, spans, off = [], [], 0
    for v in vecs:
        n = v.size
        pad = (-n) % LANES
        parts.append(jnp.pad(v.reshape(-1).astype(F32), (0, pad)))
        spans.append((off, n))
        off += n + pad
    return jnp.concatenate(parts).reshape(-1, LANES), spans


def _pad_lanes(v):
    v = v.reshape(1, -1)
    return jnp.pad(v, ((0, 0), (0, LANES - v.shape[1])))


def _cols_to_slabs(g):
    sh = g.shape
    g = g.reshape(sh[:-1] + (NDEV, sh[-1] // NDEV))
    return jnp.moveaxis(g, -2, 0)


def _rows_to_slabs(g):
    sh = g.shape
    g = g.reshape(sh[:-2] + (NDEV, sh[-2] // NDEV, sh[-1]))
    return jnp.moveaxis(g, -3, 0)


def _slabs_to_cols(a):
    a = jnp.moveaxis(a, 0, -2)
    return a.reshape(a.shape[:-2] + (a.shape[-2] * a.shape[-1],))


def _slabs_to_rows(a):
    a = jnp.moveaxis(a, 0, -3)
    return a.reshape(a.shape[:-3] + (a.shape[-3] * a.shape[-2], a.shape[-1]))


def _ffn_forward(x, norm_w, wup_g, wup_v, cw_g, cw_v, wdown, tag):
    h = _rms_fwd(x, norm_w, f"{tag}_norm")
    ug = _mm(h, wup_g, name=f"{tag}_up_gate")
    uv = _mm(h, wup_v, name=f"{tag}_up_val")
    f = _ffn_gate_fwd(ug, uv, cw_g, cw_v, f"{tag}_gate")
    return _mm(f, wdown, res=x, name=f"{tag}_down"), (h, ug, uv, f)


def _ffn_backward(x, saved, dout, dout_b, norm_w, wup_g, wup_v, cw_g, cw_v, wdown, tag):
    h, ug, uv, f = saved
    dwdown = _mm(f, dout_b, ta=True, name=f"{tag}_dwdown")
    df = _mm(dout_b, wdown, tb=True, name=f"{tag}_df")
    dug, duv, dcg, dcv = _ffn_gate_bwd(ug, uv, cw_g, cw_v, df, f"{tag}_gate_bwd")
    dwg = _mm(h, dug, ta=True, name=f"{tag}_dwup_gate")
    dwv = _mm(h, duv, ta=True, name=f"{tag}_dwup_val")
    dh = _mm(dug, wup_g, tb=True, name=f"{tag}_dh_gate")
    dh = _mm(duv, wup_v, tb=True, res=dh, name=f"{tag}_dh_val")
    dx, dxb, dnorm = _rms_bwd(x, norm_w, dh, dout, f"{tag}_norm_bwd")
    return dx, dxb, (jnp.concatenate([dwg, dwv], axis=1), jnp.concatenate([dcg, dcv], axis=1), dwdown, dnorm)


def kernel(x, a_norm, ssm_w_in, ssm_conv_w, ssm_conv_b, ssm_dt_bias, ssm_a_log, ssm_d, ssm_norm, ssm_w_out, kv_norm, w_kv, b_norm, att_w_q, att_w_o, ffn_norm, ffn_w_up, ffn_conv_w, ffn_w_down, final_norm, loss_target, m_a_norm, m_ssm_w_in, m_ssm_conv_w, m_ssm_conv_b, m_ssm_dt_bias, m_ssm_a_log, m_ssm_d, m_ssm_norm, m_ssm_w_out, m_kv_norm, m_w_kv, m_b_norm, m_att_w_q, m_att_w_o, m_ffn_norm, m_ffn_w_up, m_ffn_conv_w, m_ffn_w_down, m_final_norm, v_a_norm, v_ssm_w_in, v_ssm_conv_w, v_ssm_conv_b, v_ssm_dt_bias, v_ssm_a_log, v_ssm_d, v_ssm_norm, v_ssm_w_out, v_kv_norm, v_w_kv, v_b_norm, v_att_w_q, v_att_w_o, v_ffn_norm, v_ffn_w_up, v_ffn_conv_w, v_ffn_w_down, v_final_norm):
    given = dict(locals())
    xs, tgt = x[0], loss_target[0]
    s, d = xs.shape
    di = ssm_w_out.shape[1] * NDEV
    nh = ssm_dt_bias.shape[1]
    ng = SSM_N_GROUPS
    convd = di + 2 * ng * SSM_D_STATE
    f = ffn_w_down.shape[1] * NDEV
    n_att = len(ATT_PATTERNS)
    qg = ATT_HEADS_PER_GROUP * ATT_HEAD_DIM
    kg = ATT_KV_HEADS_PER_GROUP * ATT_HEAD_DIM
    kvd = n_att * kg
    assert all(w // dil == ATT_BLOCK for w, dil in ATT_PATTERNS)

    small, _ = _pack([a_norm, ssm_conv_w, ssm_conv_b, ssm_norm, ffn_conv_w])
    gat = _all_gather([ssm_w_in[0].astype(BF16), ssm_w_out[0].astype(BF16), small], "gather_weights")
    rest = _send_start([b.astype(BF16) for b in (ffn_w_up, ffn_w_down, w_kv, att_w_q[0], att_w_o[0])], False,
                       "gather_rest_start")
    w_in = _slabs_to_cols(gat[0])
    w_z, w_xbc = w_in[:, :di], w_in[:, di:di + convd]
    w_dt = jnp.pad(w_in[:, di + convd:], ((0, 0), (0, LANES - nh)))
    w_out = _slabs_to_rows(gat[1])
    sm = gat[2].reshape(NDEV, -1)
    o0 = 0

    def take(shape):
        nonlocal o0
        n = math.prod(shape)
        out = sm[:, o0:o0 + n].reshape((NDEV,) + shape)
        o0 += n + (-n) % LANES
        return out
    a_norm_f = _slabs_to_cols(take(a_norm.shape)) + rest[-1][0, 0]
    conv_w_f = _slabs_to_cols(take(ssm_conv_w.shape))[0]
    conv_b_f = _slabs_to_cols(take(ssm_conv_b.shape))
    ssm_norm_f = _slabs_to_cols(take(ssm_norm.shape))
    fcw = _slabs_to_cols(take(ffn_conv_w.shape))
    fcw_g, fcw_v = fcw[:, :, :f], fcw[:, :, f:]
    dtb, alog, dsk = _pad_lanes(ssm_dt_bias), _pad_lanes(ssm_a_log), _pad_lanes(ssm_d)
    kvn, fin = kv_norm.reshape(1, d), final_norm.reshape(1, d)

    h0 = _rms_fwd(xs, a_norm_f, "a_norm")
    z = _mm(h0, w_z, name="in_z")
    xbc_pre = _mm(h0, w_xbc, name="in_xbc")
    dtr = _mm(h0, w_dt, name="in_dt")
    xbc = _conv_silu_fwd(xbc_pre, conv_w_f, conv_b_f, "ssm_conv")
    dt = _softplus_fwd(dtr, dtb, "ssm_dt")
    y, prevs = _ssd_fwd2(xbc, dt, alog, dsk, di, nh, ng, "ssd")
    yn = _gnorm_fwd(y, z, ssm_norm_f, ng, "ssm_gnorm")
    x1 = _mm(yn, w_out, res=xs, name="ssm_out")
    got = _send_wait(rest, x1, False, "gather_rest_wait")
    w_up = _slabs_to_cols(got[0])
    w_up_g, w_up_v = w_up[:, :, :f], w_up[:, :, f:]
    w_down = _slabs_to_rows(got[1])
    w_kvf = _slabs_to_cols(got[2])
    w_q = _slabs_to_cols(got[3])
    w_o = _slabs_to_rows(got[4])
    x2, ffn0 = _ffn_forward(x1, ffn_norm[0:1], w_up_g[0], w_up_v[0], fcw_g[0], fcw_v[0], w_down[0], "ffn0")
    hk = _rms_fwd(x2, kvn, "kv_norm")
    kv = _mm(hk, w_kvf, name="kv_proj")
    h2 = _rms_fwd(x2, b_norm, "b_norm")
    q = _mm(h2, w_q, name="q_proj")
    att = []
    for g, (window, dil) in enumerate(ATT_PATTERNS):
        tabs = _rot_tables(s, dil)
        qp = _perm(q[:, g * qg:(g + 1) * qg], dil)
        kp = _perm(kv[:, g * kg:(g + 1) * kg], dil)
        vp = _perm(kv[:, kvd + g * kg:kvd + (g + 1) * kg], dil)
        n_blk = s // dil // ATT_BLOCK
        og, lg = _attn_fwd(qp, kp, vp, tabs, n_blk, f"attn{g}")
        att.append((qp, kp, vp, tabs, n_blk, dil, _unperm(og, dil), _unperm(lg, dil)))
    o, ob, lse = _merge_fwd([t[6] for t in att], [t[7] for t in att], "attn_merge")
    x3 = _mm(ob, w_o, res=x2, name="attn_out")
    x4, ffn1 = _ffn_forward(x3, ffn_norm[1:2], w_up_g[1], w_up_v[1], fcw_g[1], fcw_v[1], w_down[1], "ffn1")
    loss_part, dx4, dx4b, dfin = _final_loss(x4, fin, tgt, "loss_head")

    dx3, dx3b, (dwup1, dfc1, dwdown1, dfn1) = _ffn_backward(
        x3, ffn1, dx4, dx4b, ffn_norm[1:2], w_up_g[1], w_up_v[1], fcw_g[1], fcw_v[1], w_down[1], "ffn1")
    dw_o = _mm(ob, dx3b, ta=True, name="attn_dwo")
    do = _mm(dx3b, w_o, tb=True, name="attn_do")
    delta = _delta(do, o, "attn_delta")
    dqs, dks, dvs = [], [], []
    for g, (qp, kp, vp, tabs, n_blk, dil, _, _) in enumerate(att):
        dqp, dkp, dvp = _attn_bwd(qp, kp, vp, _perm(do, dil), _perm(lse, dil), _perm(delta, dil), tabs, n_blk,
                                  f"attn{g}_bwd")
        dqs.append(_unperm(dqp, dil))
        dks.append(_unperm(dkp, dil))
        dvs.append(_unperm(dvp, dil))
    dq = jnp.concatenate(dqs, axis=1)
    dkv = jnp.concatenate(dks + dvs, axis=1)
    dw_q = _mm(h2, dq, ta=True, name="q_dw")
    dh2 = _mm(dq, w_q, tb=True, name="q_dh")
    dw_kv = _mm(hk, dkv, ta=True, name="kv_dw")
    dhk = _mm(dkv, w_kvf, tb=True, name="kv_dh")
    dx2, _, db_norm = _rms_bwd(x2, b_norm, dh2, dx3, "b_norm_bwd")
    dx2, dx2b, dkv_norm = _rms_bwd(x2, kvn, dhk, dx2, "kv_norm_bwd")
    sent1 = _send_start([_cols_to_slabs(dwup1).astype(BF16), _rows_to_slabs(dwdown1).astype(BF16),
                         _cols_to_slabs(dw_kv).astype(BF16), _cols_to_slabs(dw_q).astype(BF16),
                         _rows_to_slabs(dw_o).astype(BF16)], True, "grads_late_start")
    dx1, dx1b, (dwup0, dfc0, dwdown0, dfn0) = _ffn_backward(
        x1, ffn0, dx2, dx2b, ffn_norm[0:1], w_up_g[0], w_up_v[0], fcw_g[0] + sent1[-1][0, 0], fcw_v[0], w_down[0],
        "ffn0")
    sent0 = _send_start([_cols_to_slabs(dwup0).astype(BF16), _rows_to_slabs(dwdown0).astype(BF16)], True,
                        "grads_ffn0_start")
    dw_out = _mm(yn, dx1b, ta=True, name="ssm_dwout")
    dyn = _mm(dx1b, w_out, tb=True, name="ssm_dyn")
    dy, dz, dssm_norm = _gnorm_bwd(dyn, y, z, ssm_norm_f + sent0[-1][0, 0], ng, "ssm_gnorm_bwd")
    dxbc, ddt, dalog, ddsk = _ssd_bwd2(xbc, dt, alog, dsk, prevs, dy, di, nh, ng, "ssd_bwd")
    ddtr, ddtb = _softplus_bwd(ddt, dtr, dtb, nh, "ssm_dt_bwd")
    dxbc_pre, dconv_w, dconv_b = _conv_silu_bwd(xbc_pre, conv_w_f, conv_b_f, dxbc, "ssm_conv_bwd")
    dw_z = _mm(h0, dz, ta=True, name="in_dwz")
    dw_xbc = _mm(h0, dxbc_pre, ta=True, name="in_dwxbc")
    dw_dt = _mm(h0, ddtr, ta=True, name="in_dwdt")[:, :nh]
    dh0 = _mm(dz, w_z, tb=True, name="in_dh_z")
    dh0 = _mm(dxbc_pre, w_xbc, tb=True, res=dh0, name="in_dh_xbc")
    dh0 = _mm(ddtr, w_dt, tb=True, res=dh0, name="in_dh_dt")
    dx0, _, da_norm = _rms_bwd(xs, a_norm_f, dh0, dx1, "a_norm_bwd")

    slabs = [_cols_to_slabs(jnp.concatenate([dw_z, dw_xbc, dw_dt], axis=1)).astype(BF16),
             _rows_to_slabs(dw_out).astype(BF16)]
    small_full = {
        'a_norm': da_norm, 'ssm_conv_w': dconv_w[None], 'ssm_conv_b': dconv_b, 'ssm_dt_bias': ddtb[:, :nh],
        'ssm_a_log': dalog[:, :nh], 'ssm_d': ddsk[:, :nh], 'ssm_norm': dssm_norm, 'kv_norm': dkv_norm.reshape(d),
        'b_norm': db_norm, 'ffn_norm': jnp.concatenate([dfn0, dfn1], axis=0), 'ffn_conv_w': jnp.stack([dfc0, dfc1]),
        'final_norm': dfin.reshape(d),
    }
    small_names = list(small_full)
    packed, spans = _pack([small_full[n] for n in small_names])
    recv = _exchange(slabs, [packed], "exchange_grads")
    small_sum = _sum_slabs(recv[-1], "sum_small_grads").reshape(-1)
    got1 = _send_wait(sent1, recv[-1], True, "grads_late_wait")
    got0 = _send_wait(sent0, recv[-1], True, "grads_ffn0_wait")
    recv_big = {
        'ssm_w_in': recv[0], 'ssm_w_out': recv[1], 'w_kv': got1[2], 'att_w_q': got1[3], 'att_w_o': got1[4],
        'ffn_w_up': jnp.concatenate([got0[0], got1[0]], axis=1),
        'ffn_w_down': jnp.concatenate([got0[1], got1[1]], axis=1),
    }

    me = _dev_index(*_coords())
    res = {}
    for n, r in recv_big.items():
        w = given[n]
        c = w.shape[-1]
        outs = _adamw(r.reshape(NDEV, -1, c), w.reshape(-1, c), given['m_' + n].reshape(-1, c),
                      given['v_' + n].reshape(-1, c), f"adamw_{n}")
        res[n] = [o_.reshape(w.shape) for o_ in outs]
    sharded_small = {'a_norm', 'ssm_conv_w', 'ssm_conv_b', 'ssm_norm', 'ffn_conv_w'}
    for n, (off, size) in zip(small_names, spans):
        w = given[n]
        gfull = small_sum[off:off + size].reshape(small_full[n].shape)
        if n in sharded_small:
            c = w.shape[-1]
            gfull = lax.dynamic_slice_in_dim(gfull, me * c, c, axis=gfull.ndim - 1)
        c = w.shape[-1]
        outs = _adamw(gfull.reshape(1, -1, c), w.reshape(-1, c), given['m_' + n].reshape(-1, c),
                      given['v_' + n].reshape(-1, c), f"adamw_{n}")
        res[n] = [o_.reshape(w.shape) for o_ in outs]

    loss = lax.psum(loss_part[0, 0], AXES)
    return (loss, dx0[None], *[res[n][0] for n in WEIGHTS], *[res[n][1] for n in WEIGHTS],
            *[res[n][2] for n in WEIGHTS], *[res[n][3] for n in WEIGHTS])
```

```python
import functools
import math

import jax
import jax.numpy as jnp
from jax import lax
from jax.experimental import pallas as pl
from jax.experimental.pallas import tpu as pltpu

F32, BF16 = jnp.float32, jnp.bfloat16
AXES = ("x", "y", "c")
NDEV = 8
MESH = pl.DeviceIdType.MESH
HIGHEST = lax.Precision.HIGHEST

LANES = 128
VMEM_LIMIT_BYTES = 48 * 1024 * 1024
VMEM_LIMIT_ATTN_BWD_BYTES = 58 * 1024 * 1024

RMS_EPS = 1e-6
GATED_NORM_EPS = 1e-5
SSM_HEAD_DIM = 64
SSM_N_GROUPS = 8
SSM_D_STATE = 128
SSM_CONV = 4
SSM_CHUNK = 128
ATT_PATTERNS = ((128, 1), (512, 4), (2048, 16))
ATT_HEAD_DIM = 128
ATT_HEADS_PER_GROUP = 8
ATT_KV_HEADS_PER_GROUP = 2
ATT_BLOCK = 128
ROPE_DIM = ATT_HEAD_DIM // 4
ROPE_THETA = 500000.0
FFN_CONV = 3
ADAM_LR = 0.001
ADAM_B1 = 0.9
ADAM_B2 = 0.999
ADAM_EPS = 1e-08
ADAM_WD = 0.01
ADAM_STEP = 10
NEG = -1e30

WEIGHTS = ['a_norm', 'ssm_w_in', 'ssm_conv_w', 'ssm_conv_b', 'ssm_dt_bias', 'ssm_a_log', 'ssm_d', 'ssm_norm',
           'ssm_w_out', 'kv_norm', 'w_kv', 'b_norm', 'att_w_q', 'att_w_o', 'ffn_norm', 'ffn_w_up', 'ffn_conv_w',
           'ffn_w_down', 'final_norm']


def _params(sem=None, vmem=VMEM_LIMIT_BYTES):
    kw = dict(vmem_limit_bytes=vmem)
    if sem is not None:
        kw["dimension_semantics"] = sem
    return pltpu.CompilerParams(**kw)


def _pick(n, pref):
    if n <= pref:
        return n
    t = (pref // LANES) * LANES
    while t >= LANES:
        if n % t == 0:
            return t
        t -= LANES
    return n


def _dot(a, b, dims=(((1,), (0,)), ((), ())), precision=None):
    return lax.dot_general(a, b, dims, precision=precision, preferred_element_type=F32)


_NT = (((1,), (1,)), ((), ()))
_TN = (((0,), (0,)), ((), ()))


def _mm(a, b, *, ta=False, tb=False, res=None, out_dtype=F32, name, tm=1024, tn=1408, tk=2048,
        a_heads=False, b_heads=False):
    assert not (a_heads and ta) and not (b_heads and tb)
    if a_heads:
        m, k = a.shape[1], a.shape[0] * LANES
    else:
        m = a.shape[1] if ta else a.shape[0]
        k = a.shape[0] if ta else a.shape[1]
    if b_heads:
        n, kb = b.shape[0] * LANES, b.shape[1]
    else:
        n = b.shape[0] if tb else b.shape[1]
        kb = b.shape[1] if tb else b.shape[0]
    assert k == kb
    tm, tn, tk = _pick(m, tm), _pick(n, tn), _pick(k, tk)
    nk = k // tk
    if a_heads:
        a_spec = pl.BlockSpec((tk // LANES, tm, LANES), lambda i, j, l: (l, i, 0))
    elif ta:
        a_spec = pl.BlockSpec((tk, tm), lambda i, j, l: (l, i))
    else:
        a_spec = pl.BlockSpec((tm, tk), lambda i, j, l: (i, l))
    if b_heads:
        b_spec = pl.BlockSpec((tn // LANES, tk, LANES), lambda i, j, l: (j, l, 0))
    elif tb:
        b_spec = pl.BlockSpec((tn, tk), lambda i, j, l: (j, l))
    else:
        b_spec = pl.BlockSpec((tk, tn), lambda i, j, l: (l, j))
    o_spec = pl.BlockSpec((tm, tn), lambda i, j, l: (i, j))
    dims = (((0 if ta else 1,), (1 if tb else 0,)), ((), ()))
    has_res = res is not None

    def load(ref, heads):
        if not heads:
            return ref[...].astype(BF16)
        return jnp.concatenate([ref[i].astype(BF16) for i in range(ref.shape[0])], axis=1)

    def body(*refs):
        a_ref, b_ref = refs[:2]
        r_ref = refs[2] if has_res else None
        o_ref = refs[2 + has_res]
        p = _dot(load(a_ref, a_heads), load(b_ref, b_heads), dims)

        def finish(r):
            if has_res:
                r = r + r_ref[...]
            o_ref[...] = r.astype(o_ref.dtype)

        if nk == 1:
            finish(p)
            return
        acc = refs[3 + has_res]
        l = pl.program_id(2)

        @pl.when(l == 0)
        def _():
            acc[...] = p

        @pl.when(jnp.logical_and(l > 0, l < nk - 1))
        def _():
            acc[...] += p

        @pl.when(l == nk - 1)
        def _():
            finish(acc[...] + p)

    ins = [a, b] + ([res] if has_res else [])
    in_specs = [a_spec, b_spec] + ([o_spec] if has_res else [])
    return pl.pallas_call(
        body, name=name, grid=(m // tm, n // tn, nk), in_specs=in_specs, out_specs=o_spec,
        out_shape=jax.ShapeDtypeStruct((m, n), out_dtype),
        scratch_shapes=[pltpu.VMEM((tm, tn), F32)] if nk > 1 else [],
        compiler_params=_params(("parallel", "parallel", "arbitrary")))(*ins)


def _rowwise(fn, rows, bcasts, outs, accs=(), *, tile, name):
    s = rows[0].shape[-2]
    tile = min(tile, s)
    n_in, n_out, n_acc = len(rows) + len(bcasts), len(outs), len(accs)

    def row_spec(c):
        if isinstance(c, tuple):
            return pl.BlockSpec((c[0], tile, c[1]), lambda i: (0, i, 0))
        return pl.BlockSpec((tile, c), lambda i: (i, 0))

    def row_shape(c):
        return (c[0], s, c[1]) if isinstance(c, tuple) else (s, c)

    def body(*refs):
        vals = fn(*[r[...] for r in refs[:n_in]])
        o_refs = refs[n_in:n_in + n_out]
        a_refs = refs[n_in + n_out:]
        for r, v in zip(o_refs, vals[:n_out]):
            if isinstance(v, list):
                for i, vi in enumerate(v):
                    r[i] = vi.astype(r.dtype)
            else:
                r[...] = v.astype(r.dtype)

        @pl.when(pl.program_id(0) == 0)
        def _():
            for r in a_refs:
                r[...] = jnp.zeros(r.shape, r.dtype)

        for r, v in zip(a_refs, vals[n_out:]):
            r[...] += v

    in_specs = [row_spec(r.shape[1] if r.ndim == 2 else (r.shape[0], r.shape[2])) for r in rows]
    in_specs += [pl.BlockSpec(b.shape, lambda i: (0, 0)) for b in bcasts]
    out_specs = [row_spec(c) for c, _ in outs]
    out_specs += [pl.BlockSpec(sh, lambda i: (0, 0)) for sh, _ in accs]
    out_shape = [jax.ShapeDtypeStruct(row_shape(c), dt) for c, dt in outs]
    out_shape += [jax.ShapeDtypeStruct(sh, dt) for sh, dt in accs]
    return pl.pallas_call(body, name=name, grid=(s // tile,), in_specs=in_specs, out_specs=out_specs,
                          out_shape=out_shape, compiler_params=_params(("arbitrary",)))(*rows, *bcasts)


def _rms_fwd(x, w, name):
    def fn(x, w):
        r = lax.rsqrt(jnp.mean(x * x, axis=-1, keepdims=True) + RMS_EPS)
        return (x * r * w,)
    return _rowwise(fn, [x], [w], [(x.shape[1], BF16)], tile=256, name=name)[0]


def _rms_bwd(x, w, dh, dres, name):
    def fn(x, dh, dres, w):
        r = lax.rsqrt(jnp.mean(x * x, axis=-1, keepdims=True) + RMS_EPS)
        xh = x * r
        dxh = dh * w
        dx = dres + r * (dxh - xh * jnp.mean(dxh * xh, axis=-1, keepdims=True))
        return dx, dx, jnp.sum(dh * xh, axis=0, keepdims=True)
    d = x.shape[1]
    return _rowwise(fn, [x, dh, dres], [w], [(d, F32), (d, BF16)], [((1, d), F32)], tile=256, name=name)


def _final_loss(x, w, tgt, name):
    d = x.shape[1]

    def fn(x, t, w):
        r = lax.rsqrt(jnp.mean(x * x, axis=-1, keepdims=True) + RMS_EPS)
        xh = x * r
        err = xh * w - t
        part = jnp.sum(jnp.mean(err * err, axis=-1, keepdims=True), axis=0, keepdims=True) * 0.5
        dy = err * (1.0 / d)
        dxh = dy * w
        dx = r * (dxh - xh * jnp.mean(dxh * xh, axis=-1, keepdims=True))
        return dx, dx, part, jnp.sum(dy * xh, axis=0, keepdims=True)
    dx, dxb, part, dw = _rowwise(fn, [x, tgt], [w], [(d, F32), (d, BF16)], [((1, 1), F32), ((1, d), F32)],
                                 tile=256, name=name)
    return part, dx, dxb, dw


def _softplus_fwd(dtr, bias, name):
    def fn(r, b):
        v = r + b
        return (jnp.maximum(v, 0.0) + jnp.log(1.0 + jnp.exp(-jnp.abs(v))),)
    return _rowwise(fn, [dtr], [bias], [(LANES, F32)], tile=512, name=name)[0]


def _softplus_bwd(ddt, dtr, bias, n_heads, name):
    def fn(g, r, b):
        lane = lax.broadcasted_iota(jnp.int32, g.shape, 1)
        d = jnp.where(lane < n_heads, g * jax.nn.sigmoid(r + b), 0.0)
        return d, jnp.sum(d, axis=0, keepdims=True)
    return _rowwise(fn, [ddt, dtr], [bias], [(LANES, BF16)], [((1, LANES), F32)], tile=512, name=name)


def _gnorm_fwd(y, z, w, n_groups, name):
    di = y.shape[1]
    gs = di // n_groups

    def fn(y, z, w):
        y2 = y * (z * jax.nn.sigmoid(z))
        out = []
        for g in range(n_groups):
            sl = y2[:, g * gs:(g + 1) * gs]
            r = lax.rsqrt(jnp.mean(sl * sl, axis=-1, keepdims=True) + GATED_NORM_EPS)
            out.append(sl * r)
        return (jnp.concatenate(out, axis=1) * w,)
    return _rowwise(fn, [y, z], [w], [(di, BF16)], tile=256, name=name)[0]


def _gnorm_bwd(dyn, y, z, w, n_groups, name):
    di = y.shape[1]
    gs = di // n_groups

    def fn(dyn, y, z, w):
        sig = jax.nn.sigmoid(z)
        sz = z * sig
        y2 = y * sz
        d2n = dyn * w
        dy2, yhat = [], []
        for g in range(n_groups):
            sl = y2[:, g * gs:(g + 1) * gs]
            dg = d2n[:, g * gs:(g + 1) * gs]
            r = lax.rsqrt(jnp.mean(sl * sl, axis=-1, keepdims=True) + GATED_NORM_EPS)
            yh = sl * r
            dy2.append(r * (dg - yh * jnp.mean(dg * yh, axis=-1, keepdims=True)))
            yhat.append(yh)
        dy2 = jnp.concatenate(dy2, axis=1)
        yhat = jnp.concatenate(yhat, axis=1)
        dz = dy2 * y * (sig * (1.0 + z * (1.0 - sig)))
        return dy2 * sz, dz, jnp.sum(dyn * yhat, axis=0, keepdims=True)
    return _rowwise(fn, [dyn, y, z], [w], [(di, F32), (di, BF16)], [((1, di), F32)], tile=128, name=name)


def _merge_fwd(os_, lses, name):
    n = len(os_)

    def fn(*v):
        o, l = v[:n], v[n:]
        m = functools.reduce(jnp.maximum, l)
        e = [jnp.exp(li - m) for li in l]
        tot = functools.reduce(jnp.add, e)
        acc = functools.reduce(jnp.add, [ei * oi for ei, oi in zip(e, o)]) / tot
        return acc, acc, m + jnp.log(tot)
    c = os_[0].shape[1]
    return _rowwise(fn, list(os_) + list(lses), [], [(c, F32), (c, BF16), (c, F32)], tile=256, name=name)


def _delta(do, o, name):
    c = o.shape[1]

    def fn(do, o):
        p = do * o
        out = [jnp.broadcast_to(jnp.sum(p[:, j:j + ATT_HEAD_DIM], axis=-1, keepdims=True), (p.shape[0], ATT_HEAD_DIM))
               for j in range(0, c, ATT_HEAD_DIM)]
        return (jnp.concatenate(out, axis=1),)
    return _rowwise(fn, [do, o], [], [(c, F32)], tile=256, name=name)[0]


def _lane_place(cols):
    rows = cols[0].shape[0]
    lane = lax.broadcasted_iota(jnp.int32, (rows, LANES), 1)
    out = jnp.zeros((rows, LANES), F32)
    for j, c in enumerate(cols):
        out = jnp.where(lane == j, c, out)
    return out


def _merge_heads(os_, lses, name):
    n = len(os_)
    n_kv, rep, hd = ATT_KV_HEADS_PER_GROUP, ATT_HEADS_PER_GROUP // ATT_KV_HEADS_PER_GROUP, ATT_HEAD_DIM

    def fn(*v):
        o, l = v[:n], v[n:]
        out, lse = [], []
        for h in range(n_kv):
            cols = []
            for j in range(rep):
                hh = h * rep + j
                lg = [li[h][:, j:j + 1] for li in l]
                m = functools.reduce(jnp.maximum, lg)
                e = [jnp.exp(x - m) for x in lg]
                tot = functools.reduce(jnp.add, e)
                acc = functools.reduce(jnp.add, [ei * oi[hh] for ei, oi in zip(e, o)])
                out.append(acc / tot)
                cols.append(m + jnp.log(tot))
            lse.append(_lane_place(cols))
        merged = jnp.concatenate(out, axis=1)
        return merged, merged, lse
    c = os_[0].shape[0] * hd
    return _rowwise(fn, list(os_) + list(lses), [], [(c, F32), (c, BF16), ((n_kv, LANES), F32)], tile=256, name=name)


def _delta_heads(do, o, name):
    n_kv, rep, hd = ATT_KV_HEADS_PER_GROUP, ATT_HEADS_PER_GROUP // ATT_KV_HEADS_PER_GROUP, ATT_HEAD_DIM

    def fn(do, o):
        p = do * o
        return ([_lane_place([jnp.sum(p[:, (h * rep + j) * hd:(h * rep + j + 1) * hd], axis=-1, keepdims=True)
                              for j in range(rep)]) for h in range(n_kv)],)
    return _rowwise(fn, [do, o], [], [((n_kv, LANES), F32)], tile=256, name=name)[0]


def _sum_slabs(recv, name):
    def body(r_ref, o_ref):
        acc = r_ref[0]
        for k in range(1, NDEV):
            acc = acc + r_ref[k]
        o_ref[...] = acc
    return pl.pallas_call(body, name=name, out_shape=jax.ShapeDtypeStruct(recv.shape[1:], F32),
                          compiler_params=_params())(recv)


def _shift_down(x, k):
    if k == 0:
        return x
    row = lax.broadcasted_iota(jnp.int32, x.shape, 0)
    return jnp.where(row >= k, pltpu.roll(x, k, 0), 0.0)


def _shift_up(x, k):
    if k == 0:
        return x
    s = x.shape[0]
    row = lax.broadcasted_iota(jnp.int32, x.shape, 0)
    return jnp.where(row < s - k, pltpu.roll(x, s - k, 0), 0.0)


def _conv(x, w):
    kw = w.shape[0]
    return functools.reduce(jnp.add, [w[k:k + 1, :] * _shift_down(x, kw - 1 - k) for k in range(kw)])


def _conv_t(dy, w):
    kw = w.shape[0]
    return functools.reduce(jnp.add, [w[k:k + 1, :] * _shift_up(dy, kw - 1 - k) for k in range(kw)])


def _conv_dw(x, dy, dw_ref):
    kw = dw_ref.shape[0]
    for k in range(kw):
        dw_ref[k:k + 1, :] = jnp.sum(dy * _shift_down(x, kw - 1 - k), axis=0, keepdims=True)


def _dsilu(pre):
    sig = jax.nn.sigmoid(pre)
    return sig * (1.0 + pre * (1.0 - sig))


def _col_specs(s, c, kw, tc):
    return (pl.BlockSpec((s, tc), lambda j: (0, j)), pl.BlockSpec((kw, tc), lambda j: (0, j)),
            pl.BlockSpec((1, tc), lambda j: (0, j)))


def _conv_silu_fwd(x, w, b, name):
    s, c = x.shape
    tc = LANES
    xs, ws, bs = _col_specs(s, c, w.shape[0], tc)

    def body(x_ref, w_ref, b_ref, o_ref):
        pre = _conv(x_ref[...], w_ref[...]) + b_ref[...]
        o_ref[...] = pre * jax.nn.sigmoid(pre)
    return pl.pallas_call(body, name=name, grid=(c // tc,), in_specs=[xs, ws, bs], out_specs=xs,
                          out_shape=jax.ShapeDtypeStruct((s, c), F32), compiler_params=_params(("parallel",)))(x, w, b)


def _conv_silu_bwd(x, w, b, dy, name):
    s, c = x.shape
    tc = LANES
    xs, ws, bs = _col_specs(s, c, w.shape[0], tc)

    def body(x_ref, w_ref, b_ref, dy_ref, dx_ref, dw_ref, db_ref):
        xv, wv = x_ref[...], w_ref[...]
        pre = _conv(xv, wv) + b_ref[...]
        dpre = dy_ref[...] * _dsilu(pre)
        dx_ref[...] = _conv_t(dpre, wv).astype(dx_ref.dtype)
        _conv_dw(xv, dpre, dw_ref)
        db_ref[...] = jnp.sum(dpre, axis=0, keepdims=True)
    return pl.pallas_call(
        body, name=name, grid=(c // tc,), in_specs=[xs, ws, bs, xs], out_specs=[xs, ws, bs],
        out_shape=[jax.ShapeDtypeStruct((s, c), BF16), jax.ShapeDtypeStruct(w.shape, F32),
                   jax.ShapeDtypeStruct((1, c), F32)],
        compiler_params=_params(("parallel",)))(x, w, b, dy)


def _ffn_gate_fwd(ug, uv, wg, wv, name):
    s, c = ug.shape
    tc = LANES
    xs, ws, _ = _col_specs(s, c, wg.shape[0], tc)

    def body(g_ref, v_ref, wg_ref, wv_ref, o_ref):
        g = _conv(g_ref[...], wg_ref[...])
        v = _conv(v_ref[...], wv_ref[...])
        o_ref[...] = (g * jax.nn.sigmoid(g) * v).astype(o_ref.dtype)
    return pl.pallas_call(body, name=name, grid=(c // tc,), in_specs=[xs, xs, ws, ws], out_specs=xs,
                          out_shape=jax.ShapeDtypeStruct((s, c), BF16),
                          compiler_params=_params(("parallel",)))(ug, uv, wg, wv)


def _ffn_gate_bwd(ug, uv, wg, wv, df, name):
    s, c = ug.shape
    tc = LANES
    xs, ws, _ = _col_specs(s, c, wg.shape[0], tc)

    def body(g_ref, v_ref, wg_ref, wv_ref, df_ref, dg_ref, dv_ref, dwg_ref, dwv_ref):
        gp, vp, wgv, wvv = g_ref[...], v_ref[...], wg_ref[...], wv_ref[...]
        g = _conv(gp, wgv)
        v = _conv(vp, wvv)
        dfv = df_ref[...]
        dg = dfv * v * _dsilu(g)
        dv = dfv * (g * jax.nn.sigmoid(g))
        dg_ref[...] = _conv_t(dg, wgv).astype(dg_ref.dtype)
        dv_ref[...] = _conv_t(dv, wvv).astype(dv_ref.dtype)
        _conv_dw(gp, dg, dwg_ref)
        _conv_dw(vp, dv, dwv_ref)
    return pl.pallas_call(
        body, name=name, grid=(c // tc,), in_specs=[xs, xs, ws, ws, xs], out_specs=[xs, xs, ws, ws],
        out_shape=[jax.ShapeDtypeStruct((s, c), BF16), jax.ShapeDtypeStruct((s, c), BF16),
                   jax.ShapeDtypeStruct(wg.shape, F32), jax.ShapeDtypeStruct(wv.shape, F32)],
        compiler_params=_params(("parallel",)))(ug, uv, wg, wv, df)


def _ssd_common(dt, alog, n_heads):
    ln = dt.shape[0]
    lane = lax.broadcasted_iota(jnp.int32, (1, LANES), 1)
    a = jnp.where(lane < n_heads, -jnp.exp(alog), 0.0)
    row = lax.broadcasted_iota(jnp.int32, (ln, ln), 0)
    col = lax.broadcasted_iota(jnp.int32, (ln, ln), 1)
    tril = col <= row
    acs = _dot(tril.astype(F32), dt * a, precision=HIGHEST)
    return a, acs, acs.T, tril


def _ssd_fwd(xbc, dt, alog, dskip, di, n_heads, n_groups, name):
    s, convd = xbc.shape
    ln, p, ns = SSM_CHUNK, SSM_HEAD_DIM, SSM_D_STATE
    nc, hg = s // ln, n_heads // n_groups

    def body(x_ref, dt_ref, alog_ref, d_ref, y_ref, prev_ref, st):
        @pl.when(pl.program_id(0) == 0)
        def _():
            st[...] = jnp.zeros(st.shape, F32)

        dt = dt_ref[...]
        _, acs, acs_t, tril = _ssd_common(dt, alog_ref[...], n_heads)
        e_all = jnp.exp(acs)
        last = acs[ln - 1:ln, :]
        ds_all = jnp.exp(last - acs)
        t_all = jnp.exp(last)
        dsk = d_ref[...]
        for g in range(n_groups):
            bg = x_ref[:, di + g * ns:di + (g + 1) * ns].astype(BF16)
            cg = x_ref[:, di + (n_groups + g) * ns:di + (n_groups + g + 1) * ns].astype(BF16)
            gm = _dot(cg, bg, _NT)
            for j in range(hg):
                h = g * hg + j
                xh = x_ref[:, h * p:(h + 1) * p]
                xdt = xh * dt[:, h:h + 1]
                seg = acs[:, h:h + 1] - acs_t[h:h + 1, :]
                m = jnp.where(tril, gm * jnp.exp(jnp.where(tril, seg, 0.0)), 0.0)
                prev = st[h]
                prev_ref[0, h] = prev
                y = _dot(m.astype(BF16), xdt.astype(BF16))
                y = y + _dot(cg, prev.astype(BF16), _NT) * e_all[:, h:h + 1]
                y = y + xh * dsk[:, h:h + 1]
                snew = _dot((xdt * ds_all[:, h:h + 1]).astype(BF16), bg, _TN)
                st[h] = prev * t_all[:, h:h + 1] + snew
                y_ref[:, h * p:(h + 1) * p] = y

    vec = pl.BlockSpec((1, LANES), lambda c: (0, 0))
    return pl.pallas_call(
        body, name=name, grid=(nc,),
        in_specs=[pl.BlockSpec((ln, convd), lambda c: (c, 0)), pl.BlockSpec((ln, LANES), lambda c: (c, 0)), vec, vec],
        out_specs=[pl.BlockSpec((ln, di), lambda c: (c, 0)),
                   pl.BlockSpec((1, n_heads, p, ns), lambda c: (c, 0, 0, 0))],
        out_shape=[jax.ShapeDtypeStruct((s, di), F32), jax.ShapeDtypeStruct((nc, n_heads, p, ns), F32)],
        scratch_shapes=[pltpu.VMEM((n_heads, p, ns), F32)],
        compiler_params=_params(("arbitrary",)))(xbc, dt, alog, dskip)


def _ssd_bwd(xbc, dt, alog, dskip, prev_all, dy, di, n_heads, n_groups, name):
    s, convd = xbc.shape
    ln, p, ns = SSM_CHUNK, SSM_HEAD_DIM, SSM_D_STATE
    nc, hg = s // ln, n_heads // n_groups

    def body(x_ref, dt_ref, alog_ref, d_ref, prev_ref, dy_ref, dx_ref, ddt_ref, da_ref, dd_ref, dh):
        step = pl.program_id(0)

        @pl.when(step == 0)
        def _():
            dh[...] = jnp.zeros(dh.shape, F32)
            da_ref[...] = jnp.zeros(da_ref.shape, F32)
            dd_ref[...] = jnp.zeros(dd_ref.shape, F32)

        dt = dt_ref[...]
        a, acs, acs_t, tril = _ssd_common(dt, alog_ref[...], n_heads)
        e_all = jnp.exp(acs)
        last = acs[ln - 1:ln, :]
        ds_all = jnp.exp(last - acs)
        t_all = jnp.exp(last)
        dsk = d_ref[...]
        lane = lax.broadcasted_iota(jnp.int32, (ln, LANES), 1)
        lane1 = lax.broadcasted_iota(jnp.int32, (1, LANES), 1)
        sub = lax.broadcasted_iota(jnp.int32, (LANES, ln), 0)
        rowi = lax.broadcasted_iota(jnp.int32, (ln, LANES), 0)
        dacs_c = jnp.zeros((ln, LANES), F32)
        dacs_r = jnp.zeros((LANES, ln), F32)
        dlast = jnp.zeros((1, LANES), F32)
        ddt_x = jnp.zeros((ln, LANES), F32)
        dd = jnp.zeros((1, LANES), F32)

        def tot(v):
            return jnp.sum(jnp.sum(v, axis=1, keepdims=True), axis=0, keepdims=True)

        for g in range(n_groups):
            bg = x_ref[:, di + g * ns:di + (g + 1) * ns].astype(BF16)
            cg = x_ref[:, di + (n_groups + g) * ns:di + (n_groups + g + 1) * ns].astype(BF16)
            gm = _dot(cg, bg, _NT)
            dgm = jnp.zeros((ln, ln), F32)
            dcg = jnp.zeros((ln, ns), F32)
            dbg = jnp.zeros((ln, ns), F32)
            for j in range(hg):
                h = g * hg + j
                xh = x_ref[:, h * p:(h + 1) * p]
                dth = dt[:, h:h + 1]
                xdt = xh * dth
                dyh = dy_ref[:, h * p:(h + 1) * p]
                eh, dsh, th = e_all[:, h:h + 1], ds_all[:, h:h + 1], t_all[:, h:h + 1]
                seg = acs[:, h:h + 1] - acs_t[h:h + 1, :]
                dec = jnp.where(tril, jnp.exp(jnp.where(tril, seg, 0.0)), 0.0)
                m = gm * dec
                prev = prev_ref[0, h]
                dhn = dh[h]
                prevb, dhb, dyb, xdtb = prev.astype(BF16), dhn.astype(BF16), dyh.astype(BF16), xdt.astype(BF16)
                yo = _dot(cg, prevb, _NT)
                dyob = (dyh * eh).astype(BF16)
                c_col = jnp.sum(dyh * yo, axis=1, keepdims=True) * eh
                dcg = dcg + _dot(dyob, prevb)
                dprev = th * dhn + _dot(dyob, cg, _TN)
                dtt = tot(dhn * prev)
                w = _dot(bg, dhb, _NT)
                dxdt = w * dsh
                dds = jnp.sum(w * xdt, axis=1, keepdims=True)
                dbg = dbg + _dot((xdt * dsh).astype(BF16), dhb)
                dm = _dot(dyb, xdtb, _NT)
                dxdt = dxdt + _dot(m.astype(BF16), dyb, _TN)
                dgm = dgm + dm * dec
                q = dm * m
                c_col = c_col + jnp.sum(q, axis=1, keepdims=True) - dds * dsh
                r_row = -jnp.sum(q, axis=0, keepdims=True)
                dlast_h = tot(dds * dsh) + dtt * th
                dacs_c = dacs_c + jnp.where(lane == h, c_col, 0.0)
                dacs_r = dacs_r + jnp.where(sub == h, r_row, 0.0)
                dlast = dlast + jnp.where(lane1 == h, dlast_h, 0.0)
                ddt_x = ddt_x + jnp.where(lane == h, jnp.sum(dxdt * xh, axis=1, keepdims=True), 0.0)
                dd = dd + jnp.where(lane1 == h, tot(dyh * xh), 0.0)
                dx_ref[:, h * p:(h + 1) * p] = dxdt * dth + dyh * dsk[:, h:h + 1]
                dh[h] = dprev
            dgb = dgm.astype(BF16)
            dx_ref[:, di + g * ns:di + (g + 1) * ns] = dbg + _dot(dgb, cg, _TN)
            dx_ref[:, di + (n_groups + g) * ns:di + (n_groups + g + 1) * ns] = dcg + _dot(dgb, bg)

        dacs = dacs_c + dacs_r.T + jnp.where(rowi == ln - 1, dlast, 0.0)
        row = lax.broadcasted_iota(jnp.int32, (ln, ln), 0)
        col = lax.broadcasted_iota(jnp.int32, (ln, ln), 1)
        dadt = _dot((col >= row).astype(F32), dacs, precision=HIGHEST)
        ddt_ref[...] = dadt * a + ddt_x
        da_ref[...] += jnp.sum(dadt * dt, axis=0, keepdims=True)
        dd_ref[...] += dd

        @pl.when(step == nc - 1)
        def _():
            da_ref[...] = da_ref[...] * a

    vec = pl.BlockSpec((1, LANES), lambda c: (0, 0))
    rev = lambda c: (nc - 1 - c, 0)
    return pl.pallas_call(
        body, name=name, grid=(nc,),
        in_specs=[pl.BlockSpec((ln, convd), rev), pl.BlockSpec((ln, LANES), rev), vec, vec,
                  pl.BlockSpec((1, n_heads, p, ns), lambda c: (nc - 1 - c, 0, 0, 0)), pl.BlockSpec((ln, di), rev)],
        out_specs=[pl.BlockSpec((ln, convd), rev), pl.BlockSpec((ln, LANES), rev), vec, vec],
        out_shape=[jax.ShapeDtypeStruct((s, convd), F32), jax.ShapeDtypeStruct((s, LANES), F32),
                   jax.ShapeDtypeStruct((1, LANES), F32), jax.ShapeDtypeStruct((1, LANES), F32)],
        scratch_shapes=[pltpu.VMEM((n_heads, p, ns), F32)],
        compiler_params=_params(("arbitrary",)))(xbc, dt, alog, dskip, prev_all, dy)


def _split(x, n):
    out = []
    for _ in range(n):
        piece = x.astype(BF16)
        out.append(piece)
        x = x - piece.astype(F32)
    return out


def _spread(x, onehot, n=2):
    return functools.reduce(jnp.add, [_dot(piece, onehot) for piece in _split(x, n)])


def _head_maps(di, p):
    e = (jnp.arange(di, dtype=jnp.int32)[None, :] // p == jnp.arange(LANES, dtype=jnp.int32)[:, None]).astype(BF16)
    return e, e.T


def _ssd_wide(dt, acs, acs_t, dskip, e_ref, et_ref):
    ln = dt.shape[0]
    last = acs[ln - 1:ln, :]
    stack = jnp.concatenate([dt, jnp.exp(acs), jnp.exp(last - acs), jnp.broadcast_to(dskip, (8, LANES))], axis=0)
    wide = _spread(stack, e_ref[...])
    tb = jnp.exp(jnp.broadcast_to(acs_t[:, ln - 1:ln], (LANES, LANES)))
    texp = functools.reduce(jnp.add, [_dot(et_ref[...], piece) for piece in _split(tb, 3)])
    return wide[:ln], wide[ln:2 * ln], wide[2 * ln:3 * ln], wide[3 * ln:3 * ln + 1], texp


def _ssd_fwd2(xbc, dt, alog, dskip, di, n_heads, n_groups, name):
    s, convd = xbc.shape
    ln, p, ns = SSM_CHUNK, SSM_HEAD_DIM, SSM_D_STATE
    nc, hg = s // ln, n_heads // n_groups
    gw = hg * p
    e64, e64t = _head_maps(di, p)

    def body(x_ref, dt_ref, alog_ref, d_ref, e_ref, et_ref, y_ref, prev_ref, st):
        @pl.when(pl.program_id(0) == 0)
        def _():
            st[...] = jnp.zeros(st.shape, F32)

        dt = dt_ref[...]
        _, acs, acs_t, tril = _ssd_common(dt, alog_ref[...], n_heads)
        dte, ee, dse, dske, texp = _ssd_wide(dt, acs, acs_t, d_ref[...], e_ref, et_ref)
        x = x_ref[:, :di]
        xdt = x * dte
        xdtb = xdt.astype(BF16)
        xdsb = (xdt * dse).astype(BF16)
        for g in range(n_groups):
            rows = slice(g * gw, (g + 1) * gw)
            bg = x_ref[:, di + g * ns:di + (g + 1) * ns].astype(BF16)
            cg = x_ref[:, di + (n_groups + g) * ns:di + (n_groups + g + 1) * ns].astype(BF16)
            gm = _dot(cg, bg, _NT)
            prev = st[rows, :]
            prev_ref[0, rows, :] = prev
            yo = _dot(cg, prev.astype(BF16), _NT)
            for j in range(hg):
                h = g * hg + j
                seg = acs[:, h:h + 1] - acs_t[h:h + 1, :]
                m = jnp.where(tril, gm * jnp.exp(jnp.where(tril, seg, 0.0)), 0.0)
                y_ref[:, h * p:(h + 1) * p] = _dot(m.astype(BF16), xdtb[:, h * p:(h + 1) * p])
            y_ref[:, rows] = y_ref[:, rows] + yo * ee[:, rows] + x[:, rows] * dske[:, rows]
            st[rows, :] = prev * texp[rows, :] + _dot(xdsb[:, rows], bg, _TN)

    vec = pl.BlockSpec((1, LANES), lambda c: (0, 0))
    return pl.pallas_call(
        body, name=name, grid=(nc,),
        in_specs=[pl.BlockSpec((ln, convd), lambda c: (c, 0)), pl.BlockSpec((ln, LANES), lambda c: (c, 0)), vec, vec,
                  pl.BlockSpec(e64.shape, lambda c: (0, 0)), pl.BlockSpec(e64t.shape, lambda c: (0, 0))],
        out_specs=[pl.BlockSpec((ln, di), lambda c: (c, 0)), pl.BlockSpec((1, di, ns), lambda c: (c, 0, 0))],
        out_shape=[jax.ShapeDtypeStruct((s, di), F32), jax.ShapeDtypeStruct((nc, di, ns), F32)],
        scratch_shapes=[pltpu.VMEM((di, ns), F32)],
        compiler_params=_params(("arbitrary",)))(xbc, dt, alog, dskip, e64, e64t)


def _ssd_bwd2(xbc, dt, alog, dskip, prev_all, dy, di, n_heads, n_groups, name):
    s, convd = xbc.shape
    ln, p, ns = SSM_CHUNK, SSM_HEAD_DIM, SSM_D_STATE
    nc, hg = s // ln, n_heads // n_groups
    gw = hg * p
    e64, e64t = _head_maps(di, p)

    def body(x_ref, dt_ref, alog_ref, d_ref, e_ref, et_ref, prev_ref, dy_ref,
             dx_ref, ddt_ref, da_ref, dd_ref, dh, yo_ref, w_ref):
        step = pl.program_id(0)

        @pl.when(step == 0)
        def _():
            dh[...] = jnp.zeros(dh.shape, F32)
            da_ref[...] = jnp.zeros(da_ref.shape, F32)
            dd_ref[...] = jnp.zeros(dd_ref.shape, F32)

        dt = dt_ref[...]
        a, acs, acs_t, tril = _ssd_common(dt, alog_ref[...], n_heads)
        dte, ee, dse, dske, texp = _ssd_wide(dt, acs, acs_t, d_ref[...], e_ref, et_ref)
        row = lax.broadcasted_iota(jnp.int32, (ln, ln), 0)
        col = lax.broadcasted_iota(jnp.int32, (ln, ln), 1)
        triu = col >= row
        x = x_ref[:, :di]
        dy = dy_ref[...]
        xdt = x * dte
        xdtb = xdt.astype(BF16)
        xdsb = (xdt * dse).astype(BF16)
        dyb = dy.astype(BF16)
        dyob = (dy * ee).astype(BF16)
        dhn = dh[...]
        dhb = dhn.astype(BF16)
        per_head = functools.reduce(jnp.add, [_dot(e_ref[...], piece) for piece in _split(dhn * prev_ref[0], 2)])
        ones8 = jnp.ones((8, LANES), BF16)
        dtt = functools.reduce(jnp.add, [_dot(ones8, piece, _NT) for piece in _split(per_head, 2)])[0:1]
        dacs_c = jnp.zeros((ln, LANES), F32)
        dacs_r = jnp.zeros((LANES, ln), F32)
        for g in range(n_groups):
            rows = slice(g * gw, (g + 1) * gw)
            bg = x_ref[:, di + g * ns:di + (g + 1) * ns].astype(BF16)
            cg = x_ref[:, di + (n_groups + g) * ns:di + (n_groups + g + 1) * ns].astype(BF16)
            gmt = _dot(bg, cg, _NT)
            prevb = prev_ref[0, rows, :].astype(BF16)
            dcg = _dot(dyob[:, rows], prevb)
            dh[rows, :] = texp[rows, :] * dhn[rows, :] + _dot(dyob[:, rows], cg, _TN)
            w = _dot(bg, dhb[rows, :], _NT)
            dbg = _dot(xdsb[:, rows], dhb[rows, :])
            yo_ref[:, rows] = _dot(cg, prevb, _NT)
            w_ref[:, rows] = w
            dgmt = jnp.zeros((ln, ln), F32)
            q_hi, q_lo = [], []
            for j in range(hg):
                h = g * hg + j
                segt = acs_t[h:h + 1, :] - acs[:, h:h + 1]
                dect = jnp.where(triu, jnp.exp(jnp.where(triu, segt, 0.0)), 0.0)
                dyh, xh = dyb[:, h * p:(h + 1) * p], xdtb[:, h * p:(h + 1) * p]
                mt = gmt * dect
                dmt = _dot(xh, dyh, _NT)
                dx_ref[:, h * p:(h + 1) * p] = _dot(mt.astype(BF16), dyh)
                dgmt = dgmt + dmt * dect
                hi, lo = _split(dmt * mt, 2)
                q_hi.append(hi)
                q_lo.append(lo)
            sel_c = (lax.broadcasted_iota(jnp.int32, (hg * ln, LANES), 1)
                     == g * hg + lax.broadcasted_iota(jnp.int32, (hg * ln, LANES), 0) // ln).astype(BF16)
            sel_r = (lax.broadcasted_iota(jnp.int32, (LANES, hg * ln), 0)
                     == g * hg + lax.broadcasted_iota(jnp.int32, (LANES, hg * ln), 1) // ln).astype(BF16)
            for pieces in (q_hi, q_lo):
                dacs_c = dacs_c - _dot(jnp.concatenate(pieces, axis=1), sel_c)
                dacs_r = dacs_r + _dot(sel_r, jnp.concatenate(pieces, axis=0))
            dgb = dgmt.astype(BF16)
            dx_ref[:, di + g * ns:di + (g + 1) * ns] = dbg + _dot(dgb, cg)
            dx_ref[:, di + (n_groups + g) * ns:di + (n_groups + g + 1) * ns] = dcg + _dot(dgb, bg, _TN)

        wds = w_ref[...] * dse
        dxdt = dx_ref[:, :di] + wds
        red = _spread(jnp.concatenate([dxdt * x, dy * yo_ref[...] * ee, xdt * wds, dy * x], axis=0), et_ref[...])
        ddt_x, r_off, r_state, ddr = red[:ln], red[ln:2 * ln], red[2 * ln:3 * ln], red[3 * ln:]
        dx_ref[:, :di] = dxdt * dte + dy * dske
        rowi = lax.broadcasted_iota(jnp.int32, (ln, LANES), 0)
        dlast = jnp.sum(r_state, axis=0, keepdims=True) + dtt * jnp.exp(acs[ln - 1:ln, :])
        dacs = r_off - r_state + dacs_c + dacs_r.T + jnp.where(rowi == ln - 1, dlast, 0.0)
        dadt = _dot(triu.astype(F32), dacs, precision=HIGHEST)
        ddt_ref[...] = dadt * a + ddt_x
        da_ref[...] += jnp.sum(dadt * dt, axis=0, keepdims=True)
        dd_ref[...] += jnp.sum(ddr, axis=0, keepdims=True)

        @pl.when(step == nc - 1)
        def _():
            da_ref[...] = da_ref[...] * a

    vec = pl.BlockSpec((1, LANES), lambda c: (0, 0))
    rev = lambda c: (nc - 1 - c, 0)
    return pl.pallas_call(
        body, name=name, grid=(nc,),
        in_specs=[pl.BlockSpec((ln, convd), rev), pl.BlockSpec((ln, LANES), rev), vec, vec,
                  pl.BlockSpec(e64.shape, lambda c: (0, 0)), pl.BlockSpec(e64t.shape, lambda c: (0, 0)),
                  pl.BlockSpec((1, di, ns), lambda c: (nc - 1 - c, 0, 0)), pl.BlockSpec((ln, di), rev)],
        out_specs=[pl.BlockSpec((ln, convd), rev), pl.BlockSpec((ln, LANES), rev), vec, vec],
        out_shape=[jax.ShapeDtypeStruct((s, convd), F32), jax.ShapeDtypeStruct((s, LANES), F32),
                   jax.ShapeDtypeStruct((1, LANES), F32), jax.ShapeDtypeStruct((1, LANES), F32)],
        scratch_shapes=[pltpu.VMEM((di, ns), F32), pltpu.VMEM((ln, di), F32), pltpu.VMEM((ln, di), F32)],
        compiler_params=_params(("arbitrary",)))(xbc, dt, alog, dskip, e64, e64t, prev_all, dy)


def _perm(a, d):
    if d == 1:
        return a
    s = a.shape[0]
    return a.reshape(s // d, d, -1).transpose(1, 0, 2).reshape(s, -1)


def _unperm(a, d):
    if d == 1:
        return a
    s = a.shape[0]
    return a.reshape(d, s // d, -1).transpose(1, 0, 2).reshape(s, -1)


def _rot_tables(s, d):
    half = ROPE_DIM // 2
    inv_freq = jnp.power(jnp.float32(ROPE_THETA), -jnp.arange(0, ROPE_DIM, 2, dtype=F32) / ROPE_DIM)
    v = jnp.arange(s, dtype=jnp.int32)
    pos = (v % (s // d)) * d + v // (s // d)
    ang = pos.astype(F32)[:, None] * inv_freq[None, :]
    cos, sin = jnp.cos(ang), jnp.sin(ang)
    zero = jnp.zeros((s, ATT_HEAD_DIM - ROPE_DIM), F32)
    cf = jnp.concatenate([cos, cos, jnp.ones_like(zero)], axis=1)
    s1 = jnp.concatenate([-sin, jnp.zeros_like(sin), zero], axis=1)
    s2 = jnp.concatenate([jnp.zeros_like(sin), sin, zero], axis=1)
    assert half * 2 == ROPE_DIM
    return cf, s1, s2


def _rot(x, tabs, sign):
    cf, s1, s2 = tabs
    half = ROPE_DIM // 2
    left = pltpu.roll(x, ATT_HEAD_DIM - half, 1)
    right = pltpu.roll(x, half, 1)
    return x * cf + sign * (left * s1 + right * s2)


def _att_masks(n, n_blk, rep):
    b = ATT_BLOCK
    row = lax.broadcasted_iota(jnp.int32, (rep * b, b), 0) & (b - 1)
    col = lax.broadcasted_iota(jnp.int32, (rep * b, b), 1)
    off = jnp.where(n % n_blk != 0, 0, 2 * b)
    return col <= row, col >= row + off


def _stack(x, rep):
    return jnp.concatenate([x[:, j * ATT_HEAD_DIM:(j + 1) * ATT_HEAD_DIM] for j in range(rep)], axis=0)


def _att_specs(nb, rep, cur, prv):
    b, hd = ATT_BLOCK, ATT_HEAD_DIM
    q_spec = pl.BlockSpec((b, rep * hd), lambda h, n: (cur(n), h))
    kc_spec = pl.BlockSpec((b, hd), lambda h, n: (cur(n), h))
    kp_spec = pl.BlockSpec((b, hd), lambda h, n: (prv(n), h))
    tc_spec = pl.BlockSpec((b, hd), lambda h, n: (cur(n), 0))
    tp_spec = pl.BlockSpec((b, hd), lambda h, n: (prv(n), 0))
    return q_spec, kc_spec, kp_spec, tc_spec, tp_spec


def _attn_fwd(q, k, v, tabs, n_blk, name):
    s = q.shape[0]
    b, hd = ATT_BLOCK, ATT_HEAD_DIM
    nb = s // b
    n_kv = ATT_KV_HEADS_PER_GROUP
    rep = ATT_HEADS_PER_GROUP // n_kv
    scale = hd ** -0.5

    def body(q_ref, kc_ref, kp_ref, vc_ref, vp_ref, cfc, s1c, s2c, cfp, s1p, s2p, o_ref, lse_ref):
        n = pl.program_id(1)
        tc = (cfc[...], s1c[...], s2c[...])
        tp = (cfp[...], s1p[...], s2p[...])
        qv = q_ref[...]
        q4 = jnp.concatenate([_rot(qv[:, j * hd:(j + 1) * hd], tc, 1.0) for j in range(rep)], axis=0).astype(BF16)
        kc = _rot(kc_ref[...], tc, 1.0).astype(BF16)
        kp = _rot(kp_ref[...], tp, 1.0).astype(BF16)
        mc, mp = _att_masks(n, n_blk, rep)
        sc = jnp.where(mc, _dot(q4, kc, _NT) * scale, NEG)
        sp = jnp.where(mp, _dot(q4, kp, _NT) * scale, NEG)
        m = jnp.maximum(jnp.max(sc, axis=1, keepdims=True), jnp.max(sp, axis=1, keepdims=True))
        pc, pp = jnp.exp(sc - m), jnp.exp(sp - m)
        l = jnp.sum(pc, axis=1, keepdims=True) + jnp.sum(pp, axis=1, keepdims=True)
        o = (_dot(pc.astype(BF16), vc_ref[...].astype(BF16)) + _dot(pp.astype(BF16), vp_ref[...].astype(BF16))) / l
        lse = jnp.broadcast_to(m + jnp.log(l), (rep * b, hd))
        for j in range(rep):
            o_ref[:, j * hd:(j + 1) * hd] = o[j * b:(j + 1) * b]
            lse_ref[:, j * hd:(j + 1) * hd] = lse[j * b:(j + 1) * b]

    cur = lambda n: n
    prv = lambda n: jnp.maximum(n - 1, 0)
    q_spec, kc_spec, kp_spec, tc_spec, tp_spec = _att_specs(nb, rep, cur, prv)
    return pl.pallas_call(
        body, name=name, grid=(n_kv, nb),
        in_specs=[q_spec, kc_spec, kp_spec, kc_spec, kp_spec, tc_spec, tc_spec, tc_spec, tp_spec, tp_spec, tp_spec],
        out_specs=[q_spec, q_spec],
        out_shape=[jax.ShapeDtypeStruct(q.shape, F32), jax.ShapeDtypeStruct(q.shape, F32)],
        compiler_params=_params(("parallel", "arbitrary")))(q, k, k, v, v, *tabs, *tabs)


def _attn_bwd(q, k, v, do, lse, delta, tabs, n_blk, name):
    s = q.shape[0]
    b, hd = ATT_BLOCK, ATT_HEAD_DIM
    nb = s // b
    n_kv = ATT_KV_HEADS_PER_GROUP
    rep = ATT_HEADS_PER_GROUP // n_kv
    scale = hd ** -0.5

    def body(q_ref, do_ref, lse_ref, dl_ref, kc_ref, kp_ref, vc_ref, vp_ref, cfc, s1c, s2c, cfp, s1p, s2p,
             dq_ref, dk_ref, dv_ref, ck, cv):
        n = pl.program_id(1)
        tp = (cfp[...], s1p[...], s2p[...])

        @pl.when(n == 0)
        def _():
            ck[...] = jnp.zeros(ck.shape, F32)
            cv[...] = jnp.zeros(cv.shape, F32)

        @pl.when(n < nb)
        def _():
            tc = (cfc[...], s1c[...], s2c[...])
            qv = q_ref[...]
            q4 = jnp.concatenate([_rot(qv[:, j * hd:(j + 1) * hd], tc, 1.0) for j in range(rep)],
                                 axis=0).astype(BF16)
            do4 = _stack(do_ref[...], rep).astype(BF16)
            lse4 = _stack(lse_ref[...], rep)
            dl4 = _stack(dl_ref[...], rep)
            kc = _rot(kc_ref[...], tc, 1.0).astype(BF16)
            kp = _rot(kp_ref[...], tp, 1.0).astype(BF16)
            vc, vp = vc_ref[...].astype(BF16), vp_ref[...].astype(BF16)
            mc, mp = _att_masks(n, n_blk, rep)
            pc = jnp.where(mc, jnp.exp(jnp.where(mc, _dot(q4, kc, _NT) * scale - lse4, 0.0)), 0.0)
            pp = jnp.where(mp, jnp.exp(jnp.where(mp, _dot(q4, kp, _NT) * scale - lse4, 0.0)), 0.0)
            dsc = (pc * (_dot(do4, vc, _NT) - dl4)).astype(BF16)
            dsp = (pp * (_dot(do4, vp, _NT) - dl4)).astype(BF16)
            dq4 = (_dot(dsc, kc) + _dot(dsp, kp)) * scale
            for j in range(rep):
                dq_ref[:, j * hd:(j + 1) * hd] = _rot(dq4[j * b:(j + 1) * b], tc, -1.0).astype(dq_ref.dtype)
            dk_prev = ck[...] + _dot(dsp, q4, _TN) * scale
            dv_prev = cv[...] + _dot(pp.astype(BF16), do4, _TN)
            dk_ref[...] = _rot(dk_prev, tp, -1.0).astype(dk_ref.dtype)
            dv_ref[...] = dv_prev.astype(dv_ref.dtype)
            ck[...] = _dot(dsc, q4, _TN) * scale
            cv[...] = _dot(pc.astype(BF16), do4, _TN)

        @pl.when(n == nb)
        def _():
            dk_ref[...] = _rot(ck[...], tp, -1.0).astype(dk_ref.dtype)
            dv_ref[...] = cv[...].astype(dv_ref.dtype)

    cur = lambda n: jnp.minimum(n, nb - 1)
    prv = lambda n: jnp.maximum(n - 1, 0)
    q_spec, kc_spec, kp_spec, tc_spec, tp_spec = _att_specs(nb, rep, cur, prv)
    return pl.pallas_call(
        body, name=name, grid=(n_kv, nb + 1),
        in_specs=[q_spec, q_spec, q_spec, q_spec, kc_spec, kp_spec, kc_spec, kp_spec,
                  tc_spec, tc_spec, tc_spec, tp_spec, tp_spec, tp_spec],
        out_specs=[q_spec, kp_spec, kp_spec],
        out_shape=[jax.ShapeDtypeStruct(q.shape, BF16), jax.ShapeDtypeStruct(k.shape, BF16),
                   jax.ShapeDtypeStruct(k.shape, BF16)],
        scratch_shapes=[pltpu.VMEM((b, hd), F32), pltpu.VMEM((b, hd), F32)],
        compiler_params=_params(("parallel", "arbitrary")))(q, do, lse, delta, k, k, v, v, *tabs, *tabs)


def _rot_heads(x, tabs, width, sign, out_dtype, name):
    s = x.shape[0]
    hd = ATT_HEAD_DIM
    tile = min(512, s)

    def body(x_ref, cf, s1, s2, o_ref):
        t = (cf[...], s1[...], s2[...])
        for j in range(width // hd):
            o_ref[:, j * hd:(j + 1) * hd] = _rot(x_ref[:, j * hd:(j + 1) * hd], t, sign).astype(o_ref.dtype)

    tab = pl.BlockSpec((tile, hd), lambda i: (i, 0))
    return pl.pallas_call(
        body, name=name, grid=(s // tile,), in_specs=[pl.BlockSpec((tile, width), lambda i: (i, 0)), tab, tab, tab],
        out_specs=pl.BlockSpec((tile, width), lambda i: (i, 0)), out_shape=jax.ShapeDtypeStruct((s, width), out_dtype),
        compiler_params=_params(("parallel",)))(x, *tabs)


def _rows_of(r, dil):
    return pl.ds(r, ATT_BLOCK, stride=dil) if dil > 1 else slice(None)


def _nat_specs(g, dil, n_kv_all, cur, prv):
    b, hd = ATT_BLOCK * dil, ATT_HEAD_DIM
    n_kv = ATT_KV_HEADS_PER_GROUP
    rep = ATT_HEADS_PER_GROUP // n_kv
    q_all = [pl.BlockSpec((b, hd), lambda h, n, j=j: (cur(n), (g * n_kv + h) * rep + j)) for j in range(rep)]
    q_own = [pl.BlockSpec((b, hd), lambda h, n, j=j: (cur(n), h * rep + j)) for j in range(rep)]
    hm_all = pl.BlockSpec((rep, b, hd), lambda h, n: (g * n_kv + h, cur(n), 0))
    hm_own = pl.BlockSpec((rep, b, hd), lambda h, n: (h, cur(n), 0))
    kc =pl.BlockSpec((b, hd), lambda h, n: (cur(n), g * n_kv + h))
    kp = pl.BlockSpec((b, hd), lambda h, n: (prv(n), g * n_kv + h))
    vc = pl.BlockSpec((b, hd), lambda h, n: (cur(n), n_kv_all + g * n_kv + h))
    vp = pl.BlockSpec((b, hd), lambda h, n: (prv(n), n_kv_all + g * n_kv + h))
    tab = pl.BlockSpec((b, hd), lambda h, n: (cur(n), 0))
    stat = pl.BlockSpec((None, b, LANES), lambda h, n: (h, cur(n), 0))
    return q_all, q_own, hm_all, hm_own, kc, kp, vc, vp, tab, stat


def _head_cols(stat, rep):
    return jnp.concatenate([jnp.broadcast_to(stat[:, j:j + 1], stat.shape) for j in range(rep)], axis=0)


def _attn_fwd_nat(q_all, k_rot, kv, tabs, g, dil, name):
    s = q_all.shape[0]
    b, hd = ATT_BLOCK, ATT_HEAD_DIM
    nbn = s // (b * dil)
    n_kv = ATT_KV_HEADS_PER_GROUP
    rep = ATT_HEADS_PER_GROUP // n_kv
    n_kv_all = k_rot.shape[1] // hd
    scale = hd ** -0.5

    def body(*refs):
        q_refs = refs[:rep]
        kc_ref, kp_ref, vc_ref, vp_ref, cf, s1, s2, o_ref, lse_ref = refs[rep:]
        mc, mp = _att_masks(jnp.where(pl.program_id(1) > 0, 1, 0), 2, rep)
        for r in range(dil):
            sl = _rows_of(r, dil)
            tc = (cf[sl, :], s1[sl, :], s2[sl, :])
            q4 = jnp.concatenate([_rot(q_ref[sl, :], tc, 1.0) for q_ref in q_refs], axis=0).astype(BF16)
            kc, kp = kc_ref[sl, :].astype(BF16), kp_ref[sl, :].astype(BF16)
            sc = jnp.where(mc, _dot(q4, kc, _NT) * scale, NEG)
            sp = jnp.where(mp, _dot(q4, kp, _NT) * scale, NEG)
            m = jnp.maximum(jnp.max(sc, axis=1, keepdims=True), jnp.max(sp, axis=1, keepdims=True))
            pc, pp = jnp.exp(sc - m), jnp.exp(sp - m)
            l = jnp.sum(pc, axis=1, keepdims=True) + jnp.sum(pp, axis=1, keepdims=True)
            o = (_dot(pc.astype(BF16), vc_ref[sl, :].astype(BF16))
                 + _dot(pp.astype(BF16), vp_ref[sl, :].astype(BF16))) / l
            lse = m + jnp.log(l)
            for j in range(rep):
                o_ref[j, sl, :] = o[j * b:(j + 1) * b]
            lse_ref[sl, :] = _lane_place([lse[j * b:(j + 1) * b] for j in range(rep)])

    cur = lambda n: n
    prv = lambda n: jnp.maximum(n - 1, 0)
    q_specs, _, _, hm_own, kc, kp, vc, vp, tab, stat = _nat_specs(g, dil, n_kv_all, cur, prv)
    return pl.pallas_call(
        body, name=name, grid=(n_kv, nbn), in_specs=[*q_specs, kc, kp, vc, vp, tab, tab, tab], out_specs=[hm_own, stat],
        out_shape=[jax.ShapeDtypeStruct((ATT_HEADS_PER_GROUP, s, hd), F32), jax.ShapeDtypeStruct((n_kv, s, LANES), F32)],
        compiler_params=_params(("parallel", "arbitrary")))(*([q_all] * rep), k_rot, k_rot, kv, kv, *tabs)


def _attn_bwd_nat(q_all, k_rot, kv, do, lse, delta, tabs, grads, g, dil, name):
    s = q_all.shape[0]
    b, hd = ATT_BLOCK, ATT_HEAD_DIM
    nbn = s // (b * dil)
    n_kv = ATT_KV_HEADS_PER_GROUP
    rep = ATT_HEADS_PER_GROUP // n_kv
    n_kv_all = k_rot.shape[1] // hd
    scale = hd ** -0.5

    def body(*refs):
        q_refs, do_refs = refs[:rep], refs[rep:2 * rep]
        (lse_ref, dl_ref, kc_ref, kp_ref, vc_ref, vp_ref, cf, s1, s2, _, _, _,
         dq_ref, dk_ref, dv_ref, ck, cv) = refs[2 * rep:]
        n = pl.program_id(1)

        @pl.when(n == 0)
        def _():
            ck[...] = jnp.zeros(ck.shape, F32)
            cv[...] = jnp.zeros(cv.shape, F32)

        @pl.when(n < nbn)
        def _():
            mc, mp = _att_masks(jnp.where(n > 0, 1, 0), 2, rep)
            for r in range(dil):
                sl = _rows_of(r, dil)
                own = slice(r * b, (r + 1) * b)
                tc = (cf[sl, :], s1[sl, :], s2[sl, :])
                q4 = jnp.concatenate([_rot(q_ref[sl, :], tc, 1.0) for q_ref in q_refs], axis=0).astype(BF16)
                do4 = jnp.concatenate([do_ref[sl, :] for do_ref in do_refs], axis=0).astype(BF16)
                lse4 = _head_cols(lse_ref[sl, :], rep)
                dl4 = _head_cols(dl_ref[sl, :], rep)
                kc, kp = kc_ref[sl, :].astype(BF16), kp_ref[sl, :].astype(BF16)
                vc, vp = vc_ref[sl, :].astype(BF16), vp_ref[sl, :].astype(BF16)
                pc = jnp.where(mc, jnp.exp(jnp.where(mc, _dot(q4, kc, _NT) * scale - lse4, 0.0)), 0.0)
                pp = jnp.where(mp, jnp.exp(jnp.where(mp, _dot(q4, kp, _NT) * scale - lse4, 0.0)), 0.0)
                dsc = (pc * (_dot(do4, vc, _NT) - dl4)).astype(BF16)
                dsp = (pp * (_dot(do4, vp, _NT) - dl4)).astype(BF16)
                dq4 = (_dot(dsc, kc) + _dot(dsp, kp)) * scale
                for j in range(rep):
                    dq_ref[j, sl, :] = _rot(dq4[j * b:(j + 1) * b], tc, -1.0)
                dk_ref[sl, :] = ck[own, :] + _dot(dsp, q4, _TN) * scale
                dv_ref[sl, :] = cv[own, :] + _dot(pp.astype(BF16), do4, _TN)
                ck[own, :] = _dot(dsc, q4, _TN) * scale
                cv[own, :] = _dot(pc.astype(BF16), do4, _TN)

        @pl.when(n == nbn)
        def _():
            for r in range(dil):
                sl = _rows_of(r, dil)
                dk_ref[sl, :] = ck[r * b:(r + 1) * b, :]
                dv_ref[sl, :] = cv[r * b:(r + 1) * b, :]

    cur = lambda n: jnp.minimum(n, nbn - 1)
    prv = lambda n: jnp.maximum(n - 1, 0)
    q_specs, do_specs, hm_all, _, kc, kp, vc, vp, tab, stat = _nat_specs(g, dil, n_kv_all, cur, prv)
    anyspace = pl.BlockSpec(memory_space=pl.ANY)
    n_in = 2 * rep + 9
    return pl.pallas_call(
        body, name=name, grid=(n_kv, nbn + 1),
        in_specs=[*q_specs, *do_specs, stat, stat, kc, kp, vc, vp, tab, tab, tab, anyspace, anyspace, anyspace],
        out_specs=[hm_all, kp, kp], out_shape=[jax.ShapeDtypeStruct(a.shape, a.dtype) for a in grads],
        input_output_aliases={n_in: 0, n_in + 1: 1, n_in + 2: 2},
        scratch_shapes=[pltpu.VMEM((dil * b, hd), F32), pltpu.VMEM((dil * b, hd), F32)],
        compiler_params=_params(("parallel", "arbitrary"), VMEM_LIMIT_ATTN_BWD_BYTES))(
            *([q_all] * rep), *([do] * rep), lse, delta, k_rot, k_rot, kv, kv, *tabs, *grads)


def _adamw(g_slabs, w, m, v, name):
    kk, r, c = g_slabs.shape
    tile = r if r <= 256 else _pick_rows(r, 256)

    def body(g_ref, w_ref, m_ref, v_ref, go_ref, d_ref, mo_ref, vo_ref):
        g = g_ref[0].astype(F32)
        for k in range(1, kk):
            g = g + g_ref[k].astype(F32)
        m2 = ADAM_B1 * m_ref[...] + (1.0 - ADAM_B1) * g
        v2 = ADAM_B2 * v_ref[...] + (1.0 - ADAM_B2) * jnp.square(g)
        m_hat = m2 / (1.0 - ADAM_B1 ** ADAM_STEP)
        v_hat = v2 / (1.0 - ADAM_B2 ** ADAM_STEP)
        go_ref[...] = g
        d_ref[...] = -ADAM_LR * (m_hat / (jnp.sqrt(v_hat) + ADAM_EPS) + ADAM_WD * w_ref[...])
        mo_ref[...] = m2
        vo_ref[...] = v2

    spec = pl.BlockSpec((tile, c), lambda i: (i, 0))
    return pl.pallas_call(
        body, name=name, grid=(r // tile,), in_specs=[pl.BlockSpec((kk, tile, c), lambda i: (0, i, 0)), spec, spec, spec],
        out_specs=[spec] * 4, out_shape=[jax.ShapeDtypeStruct((r, c), F32)] * 4,
        compiler_params=_params(("parallel",)))(g_slabs, w, m, v)


def _pick_rows(r, pref):
    t = (pref // 16) * 16
    while t >= 16:
        if r % t == 0:
            return t
        t -= 16
    return r


def _coords():
    return lax.axis_index("x"), lax.axis_index("y"), lax.axis_index("c")


def _dev_index(px, py, pc):
    return 4 * px + 2 * py + pc


def _all_gather(shards, name):
    na = len(shards)

    def body(*refs):
        ins, outs = refs[:na], refs[na:2 * na]
        send_sems, recv_sems, local_sems = refs[2 * na:]
        x, y, c = _coords()
        me, sibling = (x, y, c), (x, y, 1 - c)
        chips = [(1 - x, y), (x, 1 - y), (1 - x, 1 - y)]

        def copy(a, k, block, to, src=None):
            dst = outs[a].at[_dev_index(*block)]
            return pltpu.make_async_remote_copy(
                src_ref=dst if src is None else src, dst_ref=dst, send_sem=send_sems.at[a * 7 + k],
                recv_sem=recv_sems.at[a * 7 + k], device_id=to, device_id_type=MESH)

        mine = [pltpu.make_async_copy(ins[a], outs[a].at[_dev_index(*me)], local_sems.at[a]) for a in range(na)]
        for cp in mine:
            cp.start()
        first = []
        for a in range(na):
            first.append(copy(a, 0, me, sibling, src=ins[a]))
            first += [copy(a, 1 + j, me, (*chip, c), src=ins[a]) for j, chip in enumerate(chips)]
        for cp in first:
            cp.start()
        passed = []
        for j, chip in enumerate(chips):
            for a in range(na):
                copy(a, 1 + j, (*chip, c), me).wait_recv()
                cp = copy(a, 4 + j, (*chip, c), sibling)
                cp.start()
                passed.append(cp)
        for a in range(na):
            copy(a, 0, sibling, me).wait_recv()
            for j, chip in enumerate(chips):
                copy(a, 4 + j, (*chip, 1 - c), me).wait_recv()
        for cp in first + passed:
            cp.wait_send()
        for cp in mine:
            cp.wait()

    hbm = pl.BlockSpec(memory_space=pl.ANY)
    return pl.pallas_call(
        body, name=name, in_specs=[hbm] * na, out_specs=[hbm] * na,
        out_shape=[jax.ShapeDtypeStruct((NDEV,) + s.shape, s.dtype) for s in shards],
        scratch_shapes=[pltpu.SemaphoreType.DMA((7 * na,)), pltpu.SemaphoreType.DMA((7 * na,)),
                        pltpu.SemaphoreType.DMA((na,))])(*shards)


def _exchange(slabs, whole, name):
    ns, nw = len(slabs), len(whole)
    na = ns + nw

    def body(*refs):
        ins, outs = refs[:na], refs[na:2 * na]
        send_sems, recv_sems, local_sems = refs[2 * na:]
        x, y, c = _coords()
        me = _dev_index(x, y, c)

        def src_of(a, p):
            return ins[a].at[p] if a < ns else ins[a]

        def copy(a, k, peer):
            p = _dev_index(*peer)
            return pltpu.make_async_remote_copy(
                src_ref=src_of(a, p), dst_ref=outs[a].at[me], send_sem=send_sems.at[a * 7 + k - 1],
                recv_sem=recv_sems.at[a * 7 + k - 1], device_id=peer, device_id_type=MESH)

        def arrival(a, k, peer):
            p = _dev_index(*peer)
            return pltpu.make_async_remote_copy(
                src_ref=src_of(a, p), dst_ref=outs[a].at[p], send_sem=send_sems.at[a * 7 + k - 1],
                recv_sem=recv_sems.at[a * 7 + k - 1], device_id=peer, device_id_type=MESH)

        mine = [pltpu.make_async_copy(src_of(a, me), outs[a].at[me], local_sems.at[a]) for a in range(na)]
        for cp in mine:
            cp.start()
        peers = [(k, (x ^ (k >> 2), y ^ ((k >> 1) & 1), c ^ (k & 1))) for k in range(1, NDEV)]
        sent = [copy(a, k, peer) for k, peer in peers for a in range(na)]
        for cp in sent:
            cp.start()
        for k, peer in peers:
            for a in range(na):
                arrival(a, k, peer).wait_recv()
        for cp in sent:
            cp.wait_send()
        for cp in mine:
            cp.wait()

    hbm = pl.BlockSpec(memory_space=pl.ANY)
    out_shape = [jax.ShapeDtypeStruct(s.shape, s.dtype) for s in slabs]
    out_shape += [jax.ShapeDtypeStruct((NDEV,) + w.shape, w.dtype) for w in whole]
    return pl.pallas_call(
        body, name=name, in_specs=[hbm] * na, out_specs=[hbm] * na, out_shape=out_shape,
        scratch_shapes=[pltpu.SemaphoreType.DMA((7 * na,)), pltpu.SemaphoreType.DMA((7 * na,)),
                        pltpu.SemaphoreType.DMA((na,))])(*slabs, *whole)


_HBM = pl.BlockSpec(memory_space=pltpu.HBM)
_SEM = pl.BlockSpec(memory_space=pltpu.SEMAPHORE)
_EFFECT = pltpu.SideEffectType.DATAFLOW_SIDE_EFFECTING


def _peers(x, y, c):
    return [(k, (x ^ (k >> 2), y ^ ((k >> 1) & 1), c ^ (k & 1))) for k in range(1, NDEV)]


def _peer_copy(src, land, send_sems, recv_sems, a, k, dst_block, peer):
    return pltpu.make_async_remote_copy(
        src_ref=src, dst_ref=land.at[dst_block], send_sem=send_sems.at[a * 7 + k - 1],
        recv_sem=recv_sems.at[a * 7 + k - 1], device_id=peer, device_id_type=MESH)


def _send_start(arrays, slabs, name):
    na = len(arrays)
    lands = [jax.ShapeDtypeStruct(a.shape if slabs else (NDEV,) + a.shape, a.dtype) for a in arrays]

    def body(*refs):
        ins, zones = refs[:na], refs[na:2 * na]
        send_sems, recv_sems = refs[2 * na], refs[2 * na + 1]
        token = refs[-1]
        x, y, c = _coords()
        me = _dev_index(x, y, c)
        for k, peer in _peers(x, y, c):
            for a in range(na):
                src = ins[a].at[_dev_index(*peer)] if slabs else ins[a]
                _peer_copy(src, zones[a], send_sems, recv_sems, a, k, me, peer).start()
        token[...] = jnp.zeros_like(token)

    outs = pl.pallas_call(
        body, name=name,
        out_shape=(pltpu.SemaphoreType.DMA((7 * na,)), pltpu.SemaphoreType.DMA((7 * na,)),
                   *[pltpu.HBM(a.shape, a.dtype) for a in arrays], *[pltpu.HBM(l.shape, l.dtype) for l in lands],
                   jax.ShapeDtypeStruct((8, LANES), F32)),
        in_specs=[_HBM] * (2 * na), out_specs=(_SEM, _SEM, *([_HBM] * (2 * na)), pl.BlockSpec(memory_space=pltpu.VMEM)),
        input_output_aliases={i: 2 + i for i in range(2 * na)},
        compiler_params=pltpu.CompilerParams(has_side_effects=_EFFECT),
    )(*[pltpu.with_memory_space_constraint(a, pltpu.HBM) for a in arrays],
      *[pltpu.with_memory_space_constraint(lax.empty(l.shape, l.dtype), pltpu.HBM) for l in lands])
    return outs[0], outs[1], list(outs[2:2 + na]), list(outs[2 + na:2 + 2 * na]), outs[-1]


def _send_wait(started, after, slabs, name):
    send_sems, recv_sems, thru, zones, _ = started
    na = len(thru)

    def body(*refs):
        ins, lands = refs[:na], refs[na:2 * na]
        s_sems, r_sems = refs[2 * na], refs[2 * na + 1]
        x, y, c = _coords()
        for k, peer in _peers(x, y, c):
            p = _dev_index(*peer)
            for a in range(na):
                src = ins[a].at[p] if slabs else ins[a]
                cp = _peer_copy(src, lands[a], s_sems, r_sems, a, k, p, peer)
                cp.wait_send()
                cp.wait_recv()

    outs = pl.pallas_call(
        body, name=name, out_shape=tuple(pltpu.HBM(v.shape, v.dtype) for v in thru + zones),
        in_specs=[_HBM] * (2 * na) + [_SEM, _SEM, pl.BlockSpec(memory_space=pl.ANY)], out_specs=tuple([_HBM] * (2 * na)),
        input_output_aliases={i: i for i in range(2 * na)},
        compiler_params=pltpu.CompilerParams(has_side_effects=_EFFECT),
    )(*thru, *zones, send_sems, recv_sems, after)
    me = _dev_index(*_coords())
    filled = []
    for a in range(na):
        own = lax.dynamic_index_in_dim(outs[a], me, 0, keepdims=False) if slabs else outs[a]
        filled.append(lax.dynamic_update_index_in_dim(outs[na + a], own, me, 0))
    return filled


def _pack(vecs):
    parts, spans, off = [], [], 0
    for v in vecs:
        n = v.size
        pad = (-n) % LANES
        parts.append(jnp.pad(v.reshape(-1).astype(F32), (0, pad)))
        spans.append((off, n))
        off += n + pad
    return jnp.concatenate(parts).reshape(-1, LANES), spans


def _pad_lanes(v):
    v = v.reshape(1, -1)
    return jnp.pad(v, ((0, 0), (0, LANES - v.shape[1])))


def _cols_to_slabs(g):
    sh = g.shape
    g = g.reshape(sh[:-1] + (NDEV, sh[-1] // NDEV))
    return jnp.moveaxis(g, -2, 0)


def _rows_to_slabs(g):
    sh = g.shape
    g = g.reshape(sh[:-2] + (NDEV, sh[-2] // NDEV, sh[-1]))
    return jnp.moveaxis(g, -3, 0)


def _slabs_to_cols(a):
    a = jnp.moveaxis(a, 0, -2)
    return a.reshape(a.shape[:-2] + (a.shape[-2] * a.shape[-1],))


def _slabs_to_rows(a):
    a = jnp.moveaxis(a, 0, -3)
    return a.reshape(a.shape[:-3] + (a.shape[-3] * a.shape[-2], a.shape[-1]))


def _ffn_forward(x, norm_w, wup_g, wup_v, cw_g, cw_v, wdown, tag):
    h = _rms_fwd(x, norm_w, f"{tag}_norm")
    ug = _mm(h, wup_g, name=f"{tag}_up_gate")
    uv = _mm(h, wup_v, name=f"{tag}_up_val")
    f = _ffn_gate_fwd(ug, uv, cw_g, cw_v, f"{tag}_gate")
    return _mm(f, wdown, res=x, name=f"{tag}_down"), (h, ug, uv, f)


def _ffn_backward(x, saved, dout, dout_b, norm_w, wup_g, wup_v, cw_g, cw_v, wdown, tag):
    h, ug, uv, f = saved
    dwdown = _mm(f, dout_b, ta=True, name=f"{tag}_dwdown")
    df = _mm(dout_b, wdown, tb=True, name=f"{tag}_df")
    dug, duv, dcg, dcv = _ffn_gate_bwd(ug, uv, cw_g, cw_v, df, f"{tag}_gate_bwd")
    dwg = _mm(h, dug, ta=True, name=f"{tag}_dwup_gate")
    dwv = _mm(h, duv, ta=True, name=f"{tag}_dwup_val")
    dh = _mm(dug, wup_g, tb=True, name=f"{tag}_dh_gate")
    dh = _mm(duv, wup_v, tb=True, res=dh, name=f"{tag}_dh_val")
    dx, dxb, dnorm = _rms_bwd(x, norm_w, dh, dout, f"{tag}_norm_bwd")
    return dx, dxb, (jnp.concatenate([dwg, dwv], axis=1), jnp.concatenate([dcg, dcv], axis=1), dwdown, dnorm)


def kernel(x, a_norm, ssm_w_in, ssm_conv_w, ssm_conv_b, ssm_dt_bias, ssm_a_log, ssm_d, ssm_norm, ssm_w_out, kv_norm, w_kv, b_norm, att_w_q, att_w_o, ffn_norm, ffn_w_up, ffn_conv_w, ffn_w_down, final_norm, loss_target, m_a_norm, m_ssm_w_in, m_ssm_conv_w, m_ssm_conv_b, m_ssm_dt_bias, m_ssm_a_log, m_ssm_d, m_ssm_norm, m_ssm_w_out, m_kv_norm, m_w_kv, m_b_norm, m_att_w_q, m_att_w_o, m_ffn_norm, m_ffn_w_up, m_ffn_conv_w, m_ffn_w_down, m_final_norm, v_a_norm, v_ssm_w_in, v_ssm_conv_w, v_ssm_conv_b, v_ssm_dt_bias, v_ssm_a_log, v_ssm_d, v_ssm_norm, v_ssm_w_out, v_kv_norm, v_w_kv, v_b_norm, v_att_w_q, v_att_w_o, v_ffn_norm, v_ffn_w_up, v_ffn_conv_w, v_ffn_w_down, v_final_norm):
    given = dict(locals())
    xs, tgt = x[0], loss_target[0]
    s, d = xs.shape
    di = ssm_w_out.shape[1] * NDEV
    nh = ssm_dt_bias.shape[1]
    ng = SSM_N_GROUPS
    convd = di + 2 * ng * SSM_D_STATE
    f = ffn_w_down.shape[1] * NDEV
    n_att = len(ATT_PATTERNS)
    qg = ATT_HEADS_PER_GROUP * ATT_HEAD_DIM
    kg = ATT_KV_HEADS_PER_GROUP * ATT_HEAD_DIM
    kvd = n_att * kg
    assert all(w // dil == ATT_BLOCK for w, dil in ATT_PATTERNS)

    small, _ = _pack([a_norm, ssm_conv_w, ssm_conv_b, ssm_norm, ffn_conv_w])
    gat = _all_gather([ssm_w_in[0].astype(BF16), ssm_w_out[0].astype(BF16), small], "gather_weights")
    rest = _send_start([b.astype(BF16) for b in (ffn_w_up, ffn_w_down, w_kv, att_w_q[0], att_w_o[0])], False,
                       "gather_rest_start")
    w_in = _slabs_to_cols(gat[0])
    w_z, w_xbc = w_in[:, :di], w_in[:, di:di + convd]
    w_dt = jnp.pad(w_in[:, di + convd:], ((0, 0), (0, LANES - nh)))
    w_out = _slabs_to_rows(gat[1])
    sm = gat[2].reshape(NDEV, -1)
    o0 = 0

    def take(shape):
        nonlocal o0
        n = math.prod(shape)
        out = sm[:, o0:o0 + n].reshape((NDEV,) + shape)
        o0 += n + (-n) % LANES
        return out
    a_norm_f = _slabs_to_cols(take(a_norm.shape)) + rest[-1][0, 0]
    conv_w_f = _slabs_to_cols(take(ssm_conv_w.shape))[0]
    conv_b_f = _slabs_to_cols(take(ssm_conv_b.shape))
    ssm_norm_f = _slabs_to_cols(take(ssm_norm.shape))
    fcw = _slabs_to_cols(take(ffn_conv_w.shape))
    fcw_g, fcw_v = fcw[:, :, :f], fcw[:, :, f:]
    dtb, alog, dsk = _pad_lanes(ssm_dt_bias), _pad_lanes(ssm_a_log), _pad_lanes(ssm_d)
    kvn, fin = kv_norm.reshape(1, d), final_norm.reshape(1, d)

    h0 = _rms_fwd(xs, a_norm_f, "a_norm")
    z = _mm(h0, w_z, name="in_z")
    xbc_pre = _mm(h0, w_xbc, name="in_xbc")
    dtr = _mm(h0, w_dt, name="in_dt")
    xbc = _conv_silu_fwd(xbc_pre, conv_w_f, conv_b_f, "ssm_conv")
    dt = _softplus_fwd(dtr, dtb, "ssm_dt")
    y, prevs = _ssd_fwd2(xbc, dt, alog, dsk, di, nh, ng, "ssd")
    yn = _gnorm_fwd(y, z, ssm_norm_f, ng, "ssm_gnorm")
    x1 = _mm(yn, w_out, res=xs, name="ssm_out")
    got = _send_wait(rest, x1, False, "gather_rest_wait")
    w_up = _slabs_to_cols(got[0])
    w_up_g, w_up_v = w_up[:, :, :f], w_up[:, :, f:]
    w_down = _slabs_to_rows(got[1])
    w_kvf = _slabs_to_cols(got[2])
    w_q = _slabs_to_cols(got[3])
    w_o = _slabs_to_rows(got[4])
    x2, ffn0 = _ffn_forward(x1, ffn_norm[0:1], w_up_g[0], w_up_v[0], fcw_g[0], fcw_v[0], w_down[0], "ffn0")
    hk = _rms_fwd(x2, kvn, "kv_norm")
    kv = _mm(hk, w_kvf, name="kv_proj")
    h2 = _rms_fwd(x2, b_norm, "b_norm")
    q = _mm(h2, w_q, name="q_proj")
    tabs = _rot_tables(s, 1)
    k_rot = _rot_heads(kv, tabs, kvd, 1.0, F32, "k_rot")
    att = [_attn_fwd_nat(q, k_rot, kv, tabs, g, dil, f"attn{g}") for g, (_, dil) in enumerate(ATT_PATTERNS)]
    o, ob, lse = _merge_heads([t[0] for t in att], [t[1] for t in att], "attn_merge")
    x3 = _mm(ob, w_o, res=x2, name="attn_out")
    x4, ffn1 = _ffn_forward(x3, ffn_norm[1:2], w_up_g[1], w_up_v[1], fcw_g[1], fcw_v[1], w_down[1], "ffn1")
    loss_part, dx4, dx4b, dfin = _final_loss(x4, fin, tgt, "loss_head")

    dx3, dx3b, (dwup1, dfc1, dwdown1, dfn1) = _ffn_backward(
        x3, ffn1, dx4, dx4b, ffn_norm[1:2], w_up_g[1], w_up_v[1], fcw_g[1], fcw_v[1], w_down[1], "ffn1")
    dw_o = _mm(ob, dx3b, ta=True, name="attn_dwo")
    do = _mm(dx3b, w_o, tb=True, name="attn_do")
    delta = _delta_heads(do, o, "attn_delta")
    grads = (lax.empty((n_att * qg // LANES, s, LANES), F32), lax.empty((s, kvd), F32), lax.empty((s, kvd), F32))
    for g, (_, dil) in enumerate(ATT_PATTERNS):
        grads = _attn_bwd_nat(q, k_rot, kv, do, lse, delta, tabs, grads, g, dil, f"attn{g}_bwd")
    dq, dk_rot, dv = grads
    dk = _rot_heads(dk_rot, tabs, kvd, -1.0, BF16, "k_rot_bwd")
    dw_q = _mm(h2, dq, ta=True, b_heads=True, name="q_dw")
    dh2 = _mm(dq, w_q, tb=True, a_heads=True, name="q_dh")
    dw_kv = jnp.concatenate([_mm(hk, dk, ta=True, name="k_dw"), _mm(hk, dv, ta=True, name="v_dw")], axis=1)
    dhk = _mm(dk, w_kvf[:, :kvd], tb=True, name="k_dh")
    dhk = _mm(dv, w_kvf[:, kvd:], tb=True, res=dhk, name="v_dh")
    dx2, _, db_norm = _rms_bwd(x2, b_norm, dh2, dx3, "b_norm_bwd")
    dx2, dx2b, dkv_norm = _rms_bwd(x2, kvn, dhk, dx2, "kv_norm_bwd")
    sent1 = _send_start([_cols_to_slabs(dwup1).astype(BF16), _rows_to_slabs(dwdown1).astype(BF16),
                         _cols_to_slabs(dw_kv).astype(BF16), _cols_to_slabs(dw_q).astype(BF16),
                         _rows_to_slabs(dw_o).astype(BF16)], True, "grads_late_start")
    dx1, dx1b, (dwup0, dfc0, dwdown0, dfn0) = _ffn_backward(
        x1, ffn0, dx2, dx2b, ffn_norm[0:1], w_up_g[0], w_up_v[0], fcw_g[0] + sent1[-1][0, 0], fcw_v[0], w_down[0],
        "ffn0")
    sent0 = _send_start([_cols_to_slabs(dwup0).astype(BF16), _rows_to_slabs(dwdown0).astype(BF16)], True,
                        "grads_ffn0_start")
    dw_out = _mm(yn, dx1b, ta=True, name="ssm_dwout")
    dyn = _mm(dx1b, w_out, tb=True, name="ssm_dyn")
    dy, dz, dssm_norm = _gnorm_bwd(dyn, y, z, ssm_norm_f + sent0[-1][0, 0], ng, "ssm_gnorm_bwd")
    dxbc, ddt, dalog, ddsk = _ssd_bwd2(xbc, dt, alog, dsk, prevs, dy, di, nh, ng, "ssd_bwd")
    ddtr, ddtb = _softplus_bwd(ddt, dtr, dtb, nh, "ssm_dt_bwd")
    dxbc_pre, dconv_w, dconv_b = _conv_silu_bwd(xbc_pre, conv_w_f, conv_b_f, dxbc, "ssm_conv_bwd")
    dw_z = _mm(h0, dz, ta=True, name="in_dwz")
    dw_xbc = _mm(h0, dxbc_pre, ta=True, name="in_dwxbc")
    dw_dt = _mm(h0, ddtr, ta=True, name="in_dwdt")[:, :nh]
    dh0 = _mm(dz, w_z, tb=True, name="in_dh_z")
    dh0 = _mm(dxbc_pre, w_xbc, tb=True, res=dh0, name="in_dh_xbc")
    dh0 = _mm(ddtr, w_dt, tb=True, res=dh0, name="in_dh_dt")
    dx0, _, da_norm = _rms_bwd(xs, a_norm_f, dh0, dx1, "a_norm_bwd")

    slabs = [_cols_to_slabs(jnp.concatenate([dw_z, dw_xbc, dw_dt], axis=1)).astype(BF16),
             _rows_to_slabs(dw_out).astype(BF16)]
    small_full = {
        'a_norm': da_norm, 'ssm_conv_w': dconv_w[None], 'ssm_conv_b': dconv_b, 'ssm_dt_bias': ddtb[:, :nh],
        'ssm_a_log': dalog[:, :nh], 'ssm_d': ddsk[:, :nh], 'ssm_norm': dssm_norm, 'kv_norm': dkv_norm.reshape(d),
        'b_norm': db_norm, 'ffn_norm': jnp.concatenate([dfn0, dfn1], axis=0), 'ffn_conv_w': jnp.stack([dfc0, dfc1]),
        'final_norm': dfin.reshape(d),
    }
    small_names = list(small_full)
    packed, spans = _pack([small_full[n] for n in small_names])
    recv = _exchange(slabs, [packed], "exchange_grads")
    small_sum = _sum_slabs(recv[-1], "sum_small_grads").reshape(-1)
    got1 = _send_wait(sent1, recv[-1], True, "grads_late_wait")
    got0 = _send_wait(sent0, recv[-1], True, "grads_ffn0_wait")
    recv_big = {
        'ssm_w_in': recv[0], 'ssm_w_out': recv[1], 'w_kv': got1[2], 'att_w_q': got1[3], 'att_w_o': got1[4],
        'ffn_w_up': jnp.concatenate([got0[0], got1[0]], axis=1),
        'ffn_w_down': jnp.concatenate([got0[1], got1[1]], axis=1),
    }

    me = _dev_index(*_coords())
    res = {}
    for n, r in recv_big.items():
        w = given[n]
        c = w.shape[-1]
        outs = _adamw(r.reshape(NDEV, -1, c), w.reshape(-1, c), given['m_' + n].reshape(-1, c),
                      given['v_' + n].reshape(-1, c), f"adamw_{n}")
        res[n] = [o_.reshape(w.shape) for o_ in outs]
    sharded_small = {'a_norm', 'ssm_conv_w', 'ssm_conv_b', 'ssm_norm', 'ffn_conv_w'}
    for n, (off, size) in zip(small_names, spans):
        w = given[n]
        gfull = small_sum[off:off + size].reshape(small_full[n].shape)
        if n in sharded_small:
            c = w.shape[-1]
            gfull = lax.dynamic_slice_in_dim(gfull, me * c, c, axis=gfull.ndim - 1)
        c = w.shape[-1]
        outs = _adamw(gfull.reshape(1, -1, c), w.reshape(-1, c), given['m_' + n].reshape(-1, c),
                      given['v_' + n].reshape(-1, c), f"adamw_{n}")
        res[n] = [o_.reshape(w.shape) for o_ in outs]

    loss = lax.psum(loss_part[0, 0], AXES)
    return (loss, dx0[None], *[res[n][0] for n in WEIGHTS], *[res[n][1] for n in WEIGHTS],
            *[res[n][2] for n in WEIGHTS], *[res[n][3] for n in WEIGHTS])
```

```python
import functools
import math

import jax
import jax.numpy as jnp
from jax import lax
from jax.experimental import pallas as pl
from jax.experimental.pallas import tpu as pltpu

F32, BF16 = jnp.float32, jnp.bfloat16
AXES = ("x", "y", "c")
NDEV = 8
MESH = pl.DeviceIdType.MESH
HIGHEST = lax.Precision.HIGHEST

LANES = 128
VMEM_LIMIT_BYTES = 48 * 1024 * 1024
VMEM_LIMIT_ATTN_BWD_BYTES = 58 * 1024 * 1024

RMS_EPS = 1e-6
GATED_NORM_EPS = 1e-5
SSM_HEAD_DIM = 64
SSM_N_GROUPS = 8
SSM_D_STATE = 128
SSM_CONV = 4
SSM_CHUNK = 128
ATT_PATTERNS = ((128, 1), (512, 4), (2048, 16))
ATT_HEAD_DIM = 128
ATT_HEADS_PER_GROUP = 8
ATT_KV_HEADS_PER_GROUP = 2
ATT_BLOCK = 128
ROPE_DIM = ATT_HEAD_DIM // 4
ROPE_THETA = 500000.0
FFN_CONV = 3
ADAM_LR = 0.001
ADAM_B1 = 0.9
ADAM_B2 = 0.999
ADAM_EPS = 1e-08
ADAM_WD = 0.01
ADAM_STEP = 10
NEG = -1e30

WEIGHTS = ['a_norm', 'ssm_w_in', 'ssm_conv_w', 'ssm_conv_b', 'ssm_dt_bias', 'ssm_a_log', 'ssm_d', 'ssm_norm',
           'ssm_w_out', 'kv_norm', 'w_kv', 'b_norm', 'att_w_q', 'att_w_o', 'ffn_norm', 'ffn_w_up', 'ffn_conv_w',
           'ffn_w_down', 'final_norm']


def _params(sem=None, vmem=VMEM_LIMIT_BYTES):
    kw = dict(vmem_limit_bytes=vmem)
    if sem is not None:
        kw["dimension_semantics"] = sem
    return pltpu.CompilerParams(**kw)


def _pick(n, pref):
    if n <= pref:
        return n
    t = (pref // LANES) * LANES
    while t >= LANES:
        if n % t == 0:
            return t
        t -= LANES
    return n


def _dot(a, b, dims=(((1,), (0,)), ((), ())), precision=None):
    return lax.dot_general(a, b, dims, precision=precision, preferred_element_type=F32)


_NT = (((1,), (1,)), ((), ()))
_TN = (((0,), (0,)), ((), ()))


def _mm(a, b, *, ta=False, tb=False, res=None, out_dtype=None, name, tm=1408, tn=1408, tk=2048,
        a_heads=False, b_heads=False):
    assert not (a_heads and ta) and not (b_heads and tb)
    if out_dtype is None:
        out_dtype = BF16 if ta else F32
    if a_heads:
        m, k = a.shape[1], a.shape[0] * LANES
    else:
        m = a.shape[1] if ta else a.shape[0]
        k = a.shape[0] if ta else a.shape[1]
    if b_heads:
        n, kb = b.shape[0] * LANES, b.shape[1]
    else:
        n = b.shape[0] if tb else b.shape[1]
        kb = b.shape[1] if tb else b.shape[0]
    assert k == kb
    tm, tn, tk = _pick(m, tm), _pick(n, tn), _pick(k, tk)
    nk = k // tk
    if a_heads:
        a_spec = pl.BlockSpec((tk // LANES, tm, LANES), lambda i, j, l: (l, i, 0))
    elif ta:
        a_spec = pl.BlockSpec((tk, tm), lambda i, j, l: (l, i))
    else:
        a_spec = pl.BlockSpec((tm, tk), lambda i, j, l: (i, l))
    if b_heads:
        b_spec = pl.BlockSpec((tn // LANES, tk, LANES), lambda i, j, l: (j, l, 0))
    elif tb:
        b_spec = pl.BlockSpec((tn, tk), lambda i, j, l: (j, l))
    else:
        b_spec = pl.BlockSpec((tk, tn), lambda i, j, l: (l, j))
    o_spec = pl.BlockSpec((tm, tn), lambda i, j, l: (i, j))
    dims = (((0 if ta else 1,), (1 if tb else 0,)), ((), ()))
    has_res = res is not None

    def load(ref, heads):
        if not heads:
            return ref[...].astype(BF16)
        return jnp.concatenate([ref[i].astype(BF16) for i in range(ref.shape[0])], axis=1)

    def body(*refs):
        a_ref, b_ref = refs[:2]
        r_ref = refs[2] if has_res else None
        o_ref = refs[2 + has_res]
        p = _dot(load(a_ref, a_heads), load(b_ref, b_heads), dims)

        def finish(r):
            if has_res:
                r = r + r_ref[...]
            o_ref[...] = r.astype(o_ref.dtype)

        if nk == 1:
            finish(p)
            return
        acc = refs[3 + has_res]
        l = pl.program_id(2)

        @pl.when(l == 0)
        def _():
            acc[...] = p

        @pl.when(jnp.logical_and(l > 0, l < nk - 1))
        def _():
            acc[...] += p

        @pl.when(l == nk - 1)
        def _():
            finish(acc[...] + p)

    ins = [a, b] + ([res] if has_res else [])
    in_specs = [a_spec, b_spec] + ([o_spec] if has_res else [])
    return pl.pallas_call(
        body, name=name, grid=(m // tm, n // tn, nk), in_specs=in_specs, out_specs=o_spec,
        out_shape=jax.ShapeDtypeStruct((m, n), out_dtype),
        scratch_shapes=[pltpu.VMEM((tm, tn), F32)] if nk > 1 else [],
        compiler_params=_params(("parallel", "parallel", "arbitrary")))(*ins)


def _rowwise(fn, rows, bcasts, outs, accs=(), *, tile, name):
    s = rows[0].shape[-2]
    tile = min(tile, s)
    n_in, n_out, n_acc = len(rows) + len(bcasts), len(outs), len(accs)

    def row_spec(c):
        if isinstance(c, tuple):
            return pl.BlockSpec((c[0], tile, c[1]), lambda i: (0, i, 0))
        return pl.BlockSpec((tile, c), lambda i: (i, 0))

    def row_shape(c):
        return (c[0], s, c[1]) if isinstance(c, tuple) else (s, c)

    def body(*refs):
        vals = fn(*[r[...] for r in refs[:n_in]])
        o_refs = refs[n_in:n_in + n_out]
        a_refs = refs[n_in + n_out:]
        for r, v in zip(o_refs, vals[:n_out]):
            if isinstance(v, list):
                for i, vi in enumerate(v):
                    r[i] = vi.astype(r.dtype)
            else:
                r[...] = v.astype(r.dtype)

        @pl.when(pl.program_id(0) == 0)
        def _():
            for r in a_refs:
                r[...] = jnp.zeros(r.shape, r.dtype)

        for r, v in zip(a_refs, vals[n_out:]):
            r[...] += v

    in_specs = [row_spec(r.shape[1] if r.ndim == 2 else (r.shape[0], r.shape[2])) for r in rows]
    in_specs += [pl.BlockSpec(b.shape, lambda i: (0, 0)) for b in bcasts]
    out_specs = [row_spec(c) for c, _ in outs]
    out_specs += [pl.BlockSpec(sh, lambda i: (0, 0)) for sh, _ in accs]
    out_shape = [jax.ShapeDtypeStruct(row_shape(c), dt) for c, dt in outs]
    out_shape += [jax.ShapeDtypeStruct(sh, dt) for sh, dt in accs]
    return pl.pallas_call(body, name=name, grid=(s // tile,), in_specs=in_specs, out_specs=out_specs,
                          out_shape=out_shape, compiler_params=_params(("arbitrary",)))(*rows, *bcasts)


def _rms_fwd(x, w, name):
    def fn(x, w):
        r = lax.rsqrt(jnp.mean(x * x, axis=-1, keepdims=True) + RMS_EPS)
        return (x * r * w,)
    return _rowwise(fn, [x], [w], [(x.shape[1], BF16)], tile=256, name=name)[0]


def _rms_bwd(x, w, dh, dres, name):
    def fn(x, dh, dres, w):
        r = lax.rsqrt(jnp.mean(x * x, axis=-1, keepdims=True) + RMS_EPS)
        xh = x * r
        dxh = dh * w
        dx = dres + r * (dxh - xh * jnp.mean(dxh * xh, axis=-1, keepdims=True))
        return dx, dx, jnp.sum(dh * xh, axis=0, keepdims=True)
    d = x.shape[1]
    return _rowwise(fn, [x, dh, dres], [w], [(d, F32), (d, BF16)], [((1, d), F32)], tile=256, name=name)


def _final_loss(x, w, tgt, name):
    d = x.shape[1]

    def fn(x, t, w):
        r = lax.rsqrt(jnp.mean(x * x, axis=-1, keepdims=True) + RMS_EPS)
        xh = x * r
        err = xh * w - t
        part = jnp.sum(jnp.mean(err * err, axis=-1, keepdims=True), axis=0, keepdims=True) * 0.5
        dy = err * (1.0 / d)
        dxh = dy * w
        dx = r * (dxh - xh * jnp.mean(dxh * xh, axis=-1, keepdims=True))
        return dx, dx, part, jnp.sum(dy * xh, axis=0, keepdims=True)
    dx, dxb, part, dw = _rowwise(fn, [x, tgt], [w], [(d, F32), (d, BF16)], [((1, 1), F32), ((1, d), F32)],
                                 tile=256, name=name)
    return part, dx, dxb, dw


def _softplus_fwd(dtr, bias, name):
    def fn(r, b):
        v = r + b
        return (jnp.maximum(v, 0.0) + jnp.log(1.0 + jnp.exp(-jnp.abs(v))),)
    return _rowwise(fn, [dtr], [bias], [(LANES, F32)], tile=512, name=name)[0]


def _softplus_bwd(ddt, dtr, bias, n_heads, name):
    def fn(g, r, b):
        lane = lax.broadcasted_iota(jnp.int32, g.shape, 1)
        d = jnp.where(lane < n_heads, g * jax.nn.sigmoid(r + b), 0.0)
        return d, jnp.sum(d, axis=0, keepdims=True)
    return _rowwise(fn, [ddt, dtr], [bias], [(LANES, BF16)], [((1, LANES), F32)], tile=512, name=name)


def _gnorm_fwd(y, z, w, n_groups, name):
    di = y.shape[1]
    gs = di // n_groups

    def fn(y, z, w):
        y2 = y * (z * jax.nn.sigmoid(z))
        out = []
        for g in range(n_groups):
            sl = y2[:, g * gs:(g + 1) * gs]
            r = lax.rsqrt(jnp.mean(sl * sl, axis=-1, keepdims=True) + GATED_NORM_EPS)
            out.append(sl * r)
        return (jnp.concatenate(out, axis=1) * w,)
    return _rowwise(fn, [y, z], [w], [(di, BF16)], tile=256, name=name)[0]


def _gnorm_bwd(dyn, y, z, w, n_groups, name):
    di = y.shape[1]
    gs = di // n_groups

    def fn(dyn, y, z, w):
        sig = jax.nn.sigmoid(z)
        sz = z * sig
        y2 = y * sz
        d2n = dyn * w
        dy2, yhat = [], []
        for g in range(n_groups):
            sl = y2[:, g * gs:(g + 1) * gs]
            dg = d2n[:, g * gs:(g + 1) * gs]
            r = lax.rsqrt(jnp.mean(sl * sl, axis=-1, keepdims=True) + GATED_NORM_EPS)
            yh = sl * r
            dy2.append(r * (dg - yh * jnp.mean(dg * yh, axis=-1, keepdims=True)))
            yhat.append(yh)
        dy2 = jnp.concatenate(dy2, axis=1)
        yhat = jnp.concatenate(yhat, axis=1)
        dz = dy2 * y * (sig * (1.0 + z * (1.0 - sig)))
        return dy2 * sz, dz, jnp.sum(dyn * yhat, axis=0, keepdims=True)
    return _rowwise(fn, [dyn, y, z], [w], [(di, F32), (di, BF16)], [((1, di), F32)], tile=128, name=name)


def _merge_fwd(os_, lses, name):
    n = len(os_)

    def fn(*v):
        o, l = v[:n], v[n:]
        m = functools.reduce(jnp.maximum, l)
        e = [jnp.exp(li - m) for li in l]
        tot = functools.reduce(jnp.add, e)
        acc = functools.reduce(jnp.add, [ei * oi for ei, oi in zip(e, o)]) / tot
        return acc, acc, m + jnp.log(tot)
    c = os_[0].shape[1]
    return _rowwise(fn, list(os_) + list(lses), [], [(c, F32), (c, BF16), (c, F32)], tile=256, name=name)


def _delta(do, o, name):
    c = o.shape[1]

    def fn(do, o):
        p = do * o
        out = [jnp.broadcast_to(jnp.sum(p[:, j:j + ATT_HEAD_DIM], axis=-1, keepdims=True), (p.shape[0], ATT_HEAD_DIM))
               for j in range(0, c, ATT_HEAD_DIM)]
        return (jnp.concatenate(out, axis=1),)
    return _rowwise(fn, [do, o], [], [(c, F32)], tile=256, name=name)[0]


def _lane_place(cols):
    rows = cols[0].shape[0]
    lane = lax.broadcasted_iota(jnp.int32, (rows, LANES), 1)
    out = jnp.zeros((rows, LANES), F32)
    for j, c in enumerate(cols):
        out = jnp.where(lane == j, c, out)
    return out


def _merge_heads(os_, lses, name):
    n = len(os_)
    n_kv, rep, hd = ATT_KV_HEADS_PER_GROUP, ATT_HEADS_PER_GROUP // ATT_KV_HEADS_PER_GROUP, ATT_HEAD_DIM

    def fn(*v):
        o, l = v[:n], v[n:]
        out, lse = [], []
        for h in range(n_kv):
            cols = []
            for j in range(rep):
                hh = h * rep + j
                lg = [li[h][:, j:j + 1] for li in l]
                m = functools.reduce(jnp.maximum, lg)
                e = [jnp.exp(x - m) for x in lg]
                tot = functools.reduce(jnp.add, e)
                acc = functools.reduce(jnp.add, [ei * oi[hh] for ei, oi in zip(e, o)])
                out.append(acc / tot)
                cols.append(m + jnp.log(tot))
            lse.append(_lane_place(cols))
        merged = jnp.concatenate(out, axis=1)
        return merged, merged, lse
    c = os_[0].shape[0] * hd
    return _rowwise(fn, list(os_) + list(lses), [], [(c, F32), (c, BF16), ((n_kv, LANES), F32)], tile=256, name=name)


def _delta_heads(do, o, name):
    n_kv, rep, hd = ATT_KV_HEADS_PER_GROUP, ATT_HEADS_PER_GROUP // ATT_KV_HEADS_PER_GROUP, ATT_HEAD_DIM

    def fn(do, o):
        p = do * o
        return ([_lane_place([jnp.sum(p[:, (h * rep + j) * hd:(h * rep + j + 1) * hd], axis=-1, keepdims=True)
                              for j in range(rep)]) for h in range(n_kv)],)
    return _rowwise(fn, [do, o], [], [((n_kv, LANES), F32)], tile=256, name=name)[0]


def _sum_slabs(recv, name):
    def body(r_ref, o_ref):
        acc = r_ref[0]
        for k in range(1, NDEV):
            acc = acc + r_ref[k]
        o_ref[...] = acc
    return pl.pallas_call(body, name=name, out_shape=jax.ShapeDtypeStruct(recv.shape[1:], F32),
                          compiler_params=_params())(recv)


def _shift_down(x, k):
    if k == 0:
        return x
    row = lax.broadcasted_iota(jnp.int32, x.shape, 0)
    return jnp.where(row >= k, pltpu.roll(x, k, 0), 0.0)


def _shift_up(x, k):
    if k == 0:
        return x
    s = x.shape[0]
    row = lax.broadcasted_iota(jnp.int32, x.shape, 0)
    return jnp.where(row < s - k, pltpu.roll(x, s - k, 0), 0.0)


def _conv(x, w):
    kw = w.shape[0]
    return functools.reduce(jnp.add, [w[k:k + 1, :] * _shift_down(x, kw - 1 - k) for k in range(kw)])


def _conv_t(dy, w):
    kw = w.shape[0]
    return functools.reduce(jnp.add, [w[k:k + 1, :] * _shift_up(dy, kw - 1 - k) for k in range(kw)])


def _conv_dw(x, dy, dw_ref):
    kw = dw_ref.shape[0]
    for k in range(kw):
        dw_ref[k:k + 1, :] = jnp.sum(dy * _shift_down(x, kw - 1 - k), axis=0, keepdims=True)


def _dsilu(pre):
    sig = jax.nn.sigmoid(pre)
    return sig * (1.0 + pre * (1.0 - sig))


def _col_specs(s, c, kw, tc):
    return (pl.BlockSpec((s, tc), lambda j: (0, j)), pl.BlockSpec((kw, tc), lambda j: (0, j)),
            pl.BlockSpec((1, tc), lambda j: (0, j)))


def _conv_silu_fwd(x, w, b, name):
    s, c = x.shape
    tc = LANES
    xs, ws, bs = _col_specs(s, c, w.shape[0], tc)

    def body(x_ref, w_ref, b_ref, o_ref):
        pre = _conv(x_ref[...], w_ref[...]) + b_ref[...]
        o_ref[...] = pre * jax.nn.sigmoid(pre)
    return pl.pallas_call(body, name=name, grid=(c // tc,), in_specs=[xs, ws, bs], out_specs=xs,
                          out_shape=jax.ShapeDtypeStruct((s, c), F32), compiler_params=_params(("parallel",)))(x, w, b)


def _conv_silu_bwd(x, w, b, dy, name):
    s, c = x.shape
    tc = LANES
    xs, ws, bs = _col_specs(s, c, w.shape[0], tc)

    def body(x_ref, w_ref, b_ref, dy_ref, dx_ref, dw_ref, db_ref):
        xv, wv = x_ref[...], w_ref[...]
        pre = _conv(xv, wv) + b_ref[...]
        dpre = dy_ref[...] * _dsilu(pre)
        dx_ref[...] = _conv_t(dpre, wv).astype(dx_ref.dtype)
        _conv_dw(xv, dpre, dw_ref)
        db_ref[...] = jnp.sum(dpre, axis=0, keepdims=True)
    return pl.pallas_call(
        body, name=name, grid=(c // tc,), in_specs=[xs, ws, bs, xs], out_specs=[xs, ws, bs],
        out_shape=[jax.ShapeDtypeStruct((s, c), BF16), jax.ShapeDtypeStruct(w.shape, F32),
                   jax.ShapeDtypeStruct((1, c), F32)],
        compiler_params=_params(("parallel",)))(x, w, b, dy)


def _ffn_gate_fwd(ug, uv, wg, wv, name):
    s, c = ug.shape
    tc = LANES
    xs, ws, _ = _col_specs(s, c, wg.shape[0], tc)

    def body(g_ref, v_ref, wg_ref, wv_ref, o_ref):
        g = _conv(g_ref[...], wg_ref[...])
        v = _conv(v_ref[...], wv_ref[...])
        o_ref[...] = (g * jax.nn.sigmoid(g) * v).astype(o_ref.dtype)
    return pl.pallas_call(body, name=name, grid=(c // tc,), in_specs=[xs, xs, ws, ws], out_specs=xs,
                          out_shape=jax.ShapeDtypeStruct((s, c), BF16),
                          compiler_params=_params(("parallel",)))(ug, uv, wg, wv)


def _ffn_gate_bwd(ug, uv, wg, wv, df, name):
    s, c = ug.shape
    tc = LANES
    xs, ws, _ = _col_specs(s, c, wg.shape[0], tc)

    def body(g_ref, v_ref, wg_ref, wv_ref, df_ref, dg_ref, dv_ref, dwg_ref, dwv_ref):
        gp, vp, wgv, wvv = g_ref[...], v_ref[...], wg_ref[...], wv_ref[...]
        g = _conv(gp, wgv)
        v = _conv(vp, wvv)
        dfv = df_ref[...]
        dg = dfv * v * _dsilu(g)
        dv = dfv * (g * jax.nn.sigmoid(g))
        dg_ref[...] = _conv_t(dg, wgv).astype(dg_ref.dtype)
        dv_ref[...] = _conv_t(dv, wvv).astype(dv_ref.dtype)
        _conv_dw(gp, dg, dwg_ref)
        _conv_dw(vp, dv, dwv_ref)
    return pl.pallas_call(
        body, name=name, grid=(c // tc,), in_specs=[xs, xs, ws, ws, xs], out_specs=[xs, xs, ws, ws],
        out_shape=[jax.ShapeDtypeStruct((s, c), BF16), jax.ShapeDtypeStruct((s, c), BF16),
                   jax.ShapeDtypeStruct(wg.shape, F32), jax.ShapeDtypeStruct(wv.shape, F32)],
        compiler_params=_params(("parallel",)))(ug, uv, wg, wv, df)


def _ssd_common(dt, alog, n_heads):
    ln = dt.shape[0]
    lane = lax.broadcasted_iota(jnp.int32, (1, LANES), 1)
    a = jnp.where(lane < n_heads, -jnp.exp(alog), 0.0)
    row = lax.broadcasted_iota(jnp.int32, (ln, ln), 0)
    col = lax.broadcasted_iota(jnp.int32, (ln, ln), 1)
    tril = col <= row
    acs = _dot(tril.astype(F32), dt * a, precision=HIGHEST)
    return a, acs, acs.T, tril


def _ssd_fwd(xbc, dt, alog, dskip, di, n_heads, n_groups, name):
    s, convd = xbc.shape
    ln, p, ns = SSM_CHUNK, SSM_HEAD_DIM, SSM_D_STATE
    nc, hg = s // ln, n_heads // n_groups

    def body(x_ref, dt_ref, alog_ref, d_ref, y_ref, prev_ref, st):
        @pl.when(pl.program_id(0) == 0)
        def _():
            st[...] = jnp.zeros(st.shape, F32)

        dt = dt_ref[...]
        _, acs, acs_t, tril = _ssd_common(dt, alog_ref[...], n_heads)
        e_all = jnp.exp(acs)
        last = acs[ln - 1:ln, :]
        ds_all = jnp.exp(last - acs)
        t_all = jnp.exp(last)
        dsk = d_ref[...]
        for g in range(n_groups):
            bg = x_ref[:, di + g * ns:di + (g + 1) * ns].astype(BF16)
            cg = x_ref[:, di + (n_groups + g) * ns:di + (n_groups + g + 1) * ns].astype(BF16)
            gm = _dot(cg, bg, _NT)
            for j in range(hg):
                h = g * hg + j
                xh = x_ref[:, h * p:(h + 1) * p]
                xdt = xh * dt[:, h:h + 1]
                seg = acs[:, h:h + 1] - acs_t[h:h + 1, :]
                m = jnp.where(tril, gm * jnp.exp(jnp.where(tril, seg, 0.0)), 0.0)
                prev = st[h]
                prev_ref[0, h] = prev
                y = _dot(m.astype(BF16), xdt.astype(BF16))
                y = y + _dot(cg, prev.astype(BF16), _NT) * e_all[:, h:h + 1]
                y = y + xh * dsk[:, h:h + 1]
                snew = _dot((xdt * ds_all[:, h:h + 1]).astype(BF16), bg, _TN)
                st[h] = prev * t_all[:, h:h + 1] + snew
                y_ref[:, h * p:(h + 1) * p] = y

    vec = pl.BlockSpec((1, LANES), lambda c: (0, 0))
    return pl.pallas_call(
        body, name=name, grid=(nc,),
        in_specs=[pl.BlockSpec((ln, convd), lambda c: (c, 0)), pl.BlockSpec((ln, LANES), lambda c: (c, 0)), vec, vec],
        out_specs=[pl.BlockSpec((ln, di), lambda c: (c, 0)),
                   pl.BlockSpec((1, n_heads, p, ns), lambda c: (c, 0, 0, 0))],
        out_shape=[jax.ShapeDtypeStruct((s, di), F32), jax.ShapeDtypeStruct((nc, n_heads, p, ns), F32)],
        scratch_shapes=[pltpu.VMEM((n_heads, p, ns), F32)],
        compiler_params=_params(("arbitrary",)))(xbc, dt, alog, dskip)


def _ssd_bwd(xbc, dt, alog, dskip, prev_all, dy, di, n_heads, n_groups, name):
    s, convd = xbc.shape
    ln, p, ns = SSM_CHUNK, SSM_HEAD_DIM, SSM_D_STATE
    nc, hg = s // ln, n_heads // n_groups

    def body(x_ref, dt_ref, alog_ref, d_ref, prev_ref, dy_ref, dx_ref, ddt_ref, da_ref, dd_ref, dh):
        step = pl.program_id(0)

        @pl.when(step == 0)
        def _():
            dh[...] = jnp.zeros(dh.shape, F32)
            da_ref[...] = jnp.zeros(da_ref.shape, F32)
            dd_ref[...] = jnp.zeros(dd_ref.shape, F32)

        dt = dt_ref[...]
        a, acs, acs_t, tril = _ssd_common(dt, alog_ref[...], n_heads)
        e_all = jnp.exp(acs)
        last = acs[ln - 1:ln, :]
        ds_all = jnp.exp(last - acs)
        t_all = jnp.exp(last)
        dsk = d_ref[...]
        lane = lax.broadcasted_iota(jnp.int32, (ln, LANES), 1)
        lane1 = lax.broadcasted_iota(jnp.int32, (1, LANES), 1)
        sub = lax.broadcasted_iota(jnp.int32, (LANES, ln), 0)
        rowi = lax.broadcasted_iota(jnp.int32, (ln, LANES), 0)
        dacs_c = jnp.zeros((ln, LANES), F32)
        dacs_r = jnp.zeros((LANES, ln), F32)
        dlast = jnp.zeros((1, LANES), F32)
        ddt_x = jnp.zeros((ln, LANES), F32)
        dd = jnp.zeros((1, LANES), F32)

        def tot(v):
            return jnp.sum(jnp.sum(v, axis=1, keepdims=True), axis=0, keepdims=True)

        for g in range(n_groups):
            bg = x_ref[:, di + g * ns:di + (g + 1) * ns].astype(BF16)
            cg = x_ref[:, di + (n_groups + g) * ns:di + (n_groups + g + 1) * ns].astype(BF16)
            gm = _dot(cg, bg, _NT)
            dgm = jnp.zeros((ln, ln), F32)
            dcg = jnp.zeros((ln, ns), F32)
            dbg = jnp.zeros((ln, ns), F32)
            for j in range(hg):
                h = g * hg + j
                xh = x_ref[:, h * p:(h + 1) * p]
                dth = dt[:, h:h + 1]
                xdt = xh * dth
                dyh = dy_ref[:, h * p:(h + 1) * p]
                eh, dsh, th = e_all[:, h:h + 1], ds_all[:, h:h + 1], t_all[:, h:h + 1]
                seg = acs[:, h:h + 1] - acs_t[h:h + 1, :]
                dec = jnp.where(tril, jnp.exp(jnp.where(tril, seg, 0.0)), 0.0)
                m = gm * dec
                prev = prev_ref[0, h]
                dhn = dh[h]
                prevb, dhb, dyb, xdtb = prev.astype(BF16), dhn.astype(BF16), dyh.astype(BF16), xdt.astype(BF16)
                yo = _dot(cg, prevb, _NT)
                dyob = (dyh * eh).astype(BF16)
                c_col = jnp.sum(dyh * yo, axis=1, keepdims=True) * eh
                dcg = dcg + _dot(dyob, prevb)
                dprev = th * dhn + _dot(dyob, cg, _TN)
                dtt = tot(dhn * prev)
                w = _dot(bg, dhb, _NT)
                dxdt = w * dsh
                dds = jnp.sum(w * xdt, axis=1, keepdims=True)
                dbg = dbg + _dot((xdt * dsh).astype(BF16), dhb)
                dm = _dot(dyb, xdtb, _NT)
                dxdt = dxdt + _dot(m.astype(BF16), dyb, _TN)
                dgm = dgm + dm * dec
                q = dm * m
                c_col = c_col + jnp.sum(q, axis=1, keepdims=True) - dds * dsh
                r_row = -jnp.sum(q, axis=0, keepdims=True)
                dlast_h = tot(dds * dsh) + dtt * th
                dacs_c = dacs_c + jnp.where(lane == h, c_col, 0.0)
                dacs_r = dacs_r + jnp.where(sub == h, r_row, 0.0)
                dlast = dlast + jnp.where(lane1 == h, dlast_h, 0.0)
                ddt_x = ddt_x + jnp.where(lane == h, jnp.sum(dxdt * xh, axis=1, keepdims=True), 0.0)
                dd = dd + jnp.where(lane1 == h, tot(dyh * xh), 0.0)
                dx_ref[:, h * p:(h + 1) * p] = dxdt * dth + dyh * dsk[:, h:h + 1]
                dh[h] = dprev
            dgb = dgm.astype(BF16)
            dx_ref[:, di + g * ns:di + (g + 1) * ns] = dbg + _dot(dgb, cg, _TN)
            dx_ref[:, di + (n_groups + g) * ns:di + (n_groups + g + 1) * ns] = dcg + _dot(dgb, bg)

        dacs = dacs_c + dacs_r.T + jnp.where(rowi == ln - 1, dlast, 0.0)
        row = lax.broadcasted_iota(jnp.int32, (ln, ln), 0)
        col = lax.broadcasted_iota(jnp.int32, (ln, ln), 1)
        dadt = _dot((col >= row).astype(F32), dacs, precision=HIGHEST)
        ddt_ref[...] = dadt * a + ddt_x
        da_ref[...] += jnp.sum(dadt * dt, axis=0, keepdims=True)
        dd_ref[...] += dd

        @pl.when(step == nc - 1)
        def _():
            da_ref[...] = da_ref[...] * a

    vec = pl.BlockSpec((1, LANES), lambda c: (0, 0))
    rev = lambda c: (nc - 1 - c, 0)
    return pl.pallas_call(
        body, name=name, grid=(nc,),
        in_specs=[pl.BlockSpec((ln, convd), rev), pl.BlockSpec((ln, LANES), rev), vec, vec,
                  pl.BlockSpec((1, n_heads, p, ns), lambda c: (nc - 1 - c, 0, 0, 0)), pl.BlockSpec((ln, di), rev)],
        out_specs=[pl.BlockSpec((ln, convd), rev), pl.BlockSpec((ln, LANES), rev), vec, vec],
        out_shape=[jax.ShapeDtypeStruct((s, convd), F32), jax.ShapeDtypeStruct((s, LANES), F32),
                   jax.ShapeDtypeStruct((1, LANES), F32), jax.ShapeDtypeStruct((1, LANES), F32)],
        scratch_shapes=[pltpu.VMEM((n_heads, p, ns), F32)],
        compiler_params=_params(("arbitrary",)))(xbc, dt, alog, dskip, prev_all, dy)


def _split(x, n):
    out = []
    for _ in range(n):
        piece = x.astype(BF16)
        out.append(piece)
        x = x - piece.astype(F32)
    return out


def _spread(x, onehot, n=2):
    return functools.reduce(jnp.add, [_dot(piece, onehot) for piece in _split(x, n)])


def _head_maps(di, p):
    e = (jnp.arange(di, dtype=jnp.int32)[None, :] // p == jnp.arange(LANES, dtype=jnp.int32)[:, None]).astype(BF16)
    return e, e.T


def _ssd_wide(dt, acs, acs_t, dskip, e_ref, et_ref):
    ln = dt.shape[0]
    last = acs[ln - 1:ln, :]
    stack = jnp.concatenate([dt, jnp.exp(acs), jnp.exp(last - acs), jnp.broadcast_to(dskip, (8, LANES))], axis=0)
    wide = _spread(stack, e_ref[...])
    tb = jnp.exp(jnp.broadcast_to(acs_t[:, ln - 1:ln], (LANES, LANES)))
    texp = functools.reduce(jnp.add, [_dot(et_ref[...], piece) for piece in _split(tb, 3)])
    return wide[:ln], wide[ln:2 * ln], wide[2 * ln:3 * ln], wide[3 * ln:3 * ln + 1], texp


def _ssd_fwd2(xbc, dt, alog, dskip, di, n_heads, n_groups, name):
    s, convd = xbc.shape
    ln, p, ns = SSM_CHUNK, SSM_HEAD_DIM, SSM_D_STATE
    nc, hg = s // ln, n_heads // n_groups
    gw = hg * p
    e64, e64t = _head_maps(di, p)

    def body(x_ref, dt_ref, alog_ref, d_ref, e_ref, et_ref, y_ref, prev_ref, st):
        @pl.when(pl.program_id(0) == 0)
        def _():
            st[...] = jnp.zeros(st.shape, F32)

        dt = dt_ref[...]
        _, acs, acs_t, tril = _ssd_common(dt, alog_ref[...], n_heads)
        dte, ee, dse, dske, texp = _ssd_wide(dt, acs, acs_t, d_ref[...], e_ref, et_ref)
        x = x_ref[:, :di]
        xdt = x * dte
        xdtb = xdt.astype(BF16)
        xdsb = (xdt * dse).astype(BF16)
        for g in range(n_groups):
            rows = slice(g * gw, (g + 1) * gw)
            bg = x_ref[:, di + g * ns:di + (g + 1) * ns].astype(BF16)
            cg = x_ref[:, di + (n_groups + g) * ns:di + (n_groups + g + 1) * ns].astype(BF16)
            gm = _dot(cg, bg, _NT)
            prev = st[rows, :]
            prev_ref[0, rows, :] = prev
            yo = _dot(cg, prev.astype(BF16), _NT)
            for j in range(hg):
                h = g * hg + j
                seg = acs[:, h:h + 1] - acs_t[h:h + 1, :]
                m = jnp.where(tril, gm * jnp.exp(jnp.where(tril, seg, 0.0)), 0.0)
                y_ref[:, h * p:(h + 1) * p] = _dot(m.astype(BF16), xdtb[:, h * p:(h + 1) * p])
            y_ref[:, rows] = y_ref[:, rows] + yo * ee[:, rows] + x[:, rows] * dske[:, rows]
            st[rows, :] = prev * texp[rows, :] + _dot(xdsb[:, rows], bg, _TN)

    vec = pl.BlockSpec((1, LANES), lambda c: (0, 0))
    return pl.pallas_call(
        body, name=name, grid=(nc,),
        in_specs=[pl.BlockSpec((ln, convd), lambda c: (c, 0)), pl.BlockSpec((ln, LANES), lambda c: (c, 0)), vec, vec,
                  pl.BlockSpec(e64.shape, lambda c: (0, 0)), pl.BlockSpec(e64t.shape, lambda c: (0, 0))],
        out_specs=[pl.BlockSpec((ln, di), lambda c: (c, 0)), pl.BlockSpec((1, di, ns), lambda c: (c, 0, 0))],
        out_shape=[jax.ShapeDtypeStruct((s, di), F32), jax.ShapeDtypeStruct((nc, di, ns), F32)],
        scratch_shapes=[pltpu.VMEM((di, ns), F32)],
        compiler_params=_params(("arbitrary",)))(xbc, dt, alog, dskip, e64, e64t)


def _ssd_bwd2(xbc, dt, alog, dskip, prev_all, dy, di, n_heads, n_groups, name):
    s, convd = xbc.shape
    ln, p, ns = SSM_CHUNK, SSM_HEAD_DIM, SSM_D_STATE
    nc, hg = s // ln, n_heads // n_groups
    gw = hg * p
    e64, e64t = _head_maps(di, p)

    def body(x_ref, dt_ref, alog_ref, d_ref, e_ref, et_ref, prev_ref, dy_ref,
             dx_ref, ddt_ref, da_ref, dd_ref, dh, yo_ref, w_ref):
        step = pl.program_id(0)

        @pl.when(step == 0)
        def _():
            dh[...] = jnp.zeros(dh.shape, F32)
            da_ref[...] = jnp.zeros(da_ref.shape, F32)
            dd_ref[...] = jnp.zeros(dd_ref.shape, F32)

        dt = dt_ref[...]
        a, acs, acs_t, tril = _ssd_common(dt, alog_ref[...], n_heads)
        dte, ee, dse, dske, texp = _ssd_wide(dt, acs, acs_t, d_ref[...], e_ref, et_ref)
        row = lax.broadcasted_iota(jnp.int32, (ln, ln), 0)
        col = lax.broadcasted_iota(jnp.int32, (ln, ln), 1)
        triu = col >= row
        x = x_ref[:, :di]
        dy = dy_ref[...]
        xdt = x * dte
        xdtb = xdt.astype(BF16)
        xdsb = (xdt * dse).astype(BF16)
        dyb = dy.astype(BF16)
        dyob = (dy * ee).astype(BF16)
        dhn = dh[...]
        dhb = dhn.astype(BF16)
        per_head = functools.reduce(jnp.add, [_dot(e_ref[...], piece) for piece in _split(dhn * prev_ref[0], 2)])
        ones8 = jnp.ones((8, LANES), BF16)
        dtt = functools.reduce(jnp.add, [_dot(ones8, piece, _NT) for piece in _split(per_head, 2)])[0:1]
        dacs_c = jnp.zeros((ln, LANES), F32)
        dacs_r = jnp.zeros((LANES, ln), F32)
        for g in range(n_groups):
            rows = slice(g * gw, (g + 1) * gw)
            bg = x_ref[:, di + g * ns:di + (g + 1) * ns].astype(BF16)
            cg = x_ref[:, di + (n_groups + g) * ns:di + (n_groups + g + 1) * ns].astype(BF16)
            gmt = _dot(bg, cg, _NT)
            prevb = prev_ref[0, rows, :].astype(BF16)
            dcg = _dot(dyob[:, rows], prevb)
            dh[rows, :] = texp[rows, :] * dhn[rows, :] + _dot(dyob[:, rows], cg, _TN)
            w = _dot(bg, dhb[rows, :], _NT)
            dbg = _dot(xdsb[:, rows], dhb[rows, :])
            yo_ref[:, rows] = _dot(cg, prevb, _NT)
            w_ref[:, rows] = w
            dgmt = jnp.zeros((ln, ln), F32)
            q_hi, q_lo = [], []
            for j in range(hg):
                h = g * hg + j
                segt = acs_t[h:h + 1, :] - acs[:, h:h + 1]
                dect = jnp.where(triu, jnp.exp(jnp.where(triu, segt, 0.0)), 0.0)
                dyh, xh = dyb[:, h * p:(h + 1) * p], xdtb[:, h * p:(h + 1) * p]
                mt = gmt * dect
                dmt = _dot(xh, dyh, _NT)
                dx_ref[:, h * p:(h + 1) * p] = _dot(mt.astype(BF16), dyh)
                dgmt = dgmt + dmt * dect
                hi, lo = _split(dmt * mt, 2)
                q_hi.append(hi)
                q_lo.append(lo)
            sel_c = (lax.broadcasted_iota(jnp.int32, (hg * ln, LANES), 1)
                     == g * hg + lax.broadcasted_iota(jnp.int32, (hg * ln, LANES), 0) // ln).astype(BF16)
            sel_r = (lax.broadcasted_iota(jnp.int32, (LANES, hg * ln), 0)
                     == g * hg + lax.broadcasted_iota(jnp.int32, (LANES, hg * ln), 1) // ln).astype(BF16)
            for pieces in (q_hi, q_lo):
                dacs_c = dacs_c - _dot(jnp.concatenate(pieces, axis=1), sel_c)
                dacs_r = dacs_r + _dot(sel_r, jnp.concatenate(pieces, axis=0))
            dgb = dgmt.astype(BF16)
            dx_ref[:, di + g * ns:di + (g + 1) * ns] = dbg + _dot(dgb, cg)
            dx_ref[:, di + (n_groups + g) * ns:di + (n_groups + g + 1) * ns] = dcg + _dot(dgb, bg, _TN)

        wds = w_ref[...] * dse
        dxdt = dx_ref[:, :di] + wds
        red = _spread(jnp.concatenate([dxdt * x, dy * yo_ref[...] * ee, xdt * wds, dy * x], axis=0), et_ref[...])
        ddt_x, r_off, r_state, ddr = red[:ln], red[ln:2 * ln], red[2 * ln:3 * ln], red[3 * ln:]
        dx_ref[:, :di] = dxdt * dte + dy * dske
        rowi = lax.broadcasted_iota(jnp.int32, (ln, LANES), 0)
        dlast = jnp.sum(r_state, axis=0, keepdims=True) + dtt * jnp.exp(acs[ln - 1:ln, :])
        dacs = r_off - r_state + dacs_c + dacs_r.T + jnp.where(rowi == ln - 1, dlast, 0.0)
        dadt = _dot(triu.astype(F32), dacs, precision=HIGHEST)
        ddt_ref[...] = dadt * a + ddt_x
        da_ref[...] += jnp.sum(dadt * dt, axis=0, keepdims=True)
        dd_ref[...] += jnp.sum(ddr, axis=0, keepdims=True)

        @pl.when(step == nc - 1)
        def _():
            da_ref[...] = da_ref[...] * a

    vec = pl.BlockSpec((1, LANES), lambda c: (0, 0))
    rev = lambda c: (nc - 1 - c, 0)
    return pl.pallas_call(
        body, name=name, grid=(nc,),
        in_specs=[pl.BlockSpec((ln, convd), rev), pl.BlockSpec((ln, LANES), rev), vec, vec,
                  pl.BlockSpec(e64.shape, lambda c: (0, 0)), pl.BlockSpec(e64t.shape, lambda c: (0, 0)),
                  pl.BlockSpec((1, di, ns), lambda c: (nc - 1 - c, 0, 0)), pl.BlockSpec((ln, di), rev)],
        out_specs=[pl.BlockSpec((ln, convd), rev), pl.BlockSpec((ln, LANES), rev), vec, vec],
        out_shape=[jax.ShapeDtypeStruct((s, convd), F32), jax.ShapeDtypeStruct((s, LANES), F32),
                   jax.ShapeDtypeStruct((1, LANES), F32), jax.ShapeDtypeStruct((1, LANES), F32)],
        scratch_shapes=[pltpu.VMEM((di, ns), F32), pltpu.VMEM((ln, di), F32), pltpu.VMEM((ln, di), F32)],
        compiler_params=_params(("arbitrary",)))(xbc, dt, alog, dskip, e64, e64t, prev_all, dy)


def _perm(a, d):
    if d == 1:
        return a
    s = a.shape[0]
    return a.reshape(s // d, d, -1).transpose(1, 0, 2).reshape(s, -1)


def _unperm(a, d):
    if d == 1:
        return a
    s = a.shape[0]
    return a.reshape(d, s // d, -1).transpose(1, 0, 2).reshape(s, -1)


def _rot_tables(s, d):
    half = ROPE_DIM // 2
    inv_freq = jnp.power(jnp.float32(ROPE_THETA), -jnp.arange(0, ROPE_DIM, 2, dtype=F32) / ROPE_DIM)
    v = jnp.arange(s, dtype=jnp.int32)
    pos = (v % (s // d)) * d + v // (s // d)
    ang = pos.astype(F32)[:, None] * inv_freq[None, :]
    cos, sin = jnp.cos(ang), jnp.sin(ang)
    zero = jnp.zeros((s, ATT_HEAD_DIM - ROPE_DIM), F32)
    cf = jnp.concatenate([cos, cos, jnp.ones_like(zero)], axis=1)
    s1 = jnp.concatenate([-sin, jnp.zeros_like(sin), zero], axis=1)
    s2 = jnp.concatenate([jnp.zeros_like(sin), sin, zero], axis=1)
    assert half * 2 == ROPE_DIM
    return cf, s1, s2


def _rot(x, tabs, sign):
    cf, s1, s2 = tabs
    half = ROPE_DIM // 2
    left = pltpu.roll(x, ATT_HEAD_DIM - half, 1)
    right = pltpu.roll(x, half, 1)
    return x * cf + sign * (left * s1 + right * s2)


def _att_masks(n, n_blk, rep):
    b = ATT_BLOCK
    row = lax.broadcasted_iota(jnp.int32, (rep * b, b), 0) & (b - 1)
    col = lax.broadcasted_iota(jnp.int32, (rep * b, b), 1)
    off = jnp.where(n % n_blk != 0, 0, 2 * b)
    return col <= row, col >= row + off


def _stack(x, rep):
    return jnp.concatenate([x[:, j * ATT_HEAD_DIM:(j + 1) * ATT_HEAD_DIM] for j in range(rep)], axis=0)


def _att_specs(nb, rep, cur, prv):
    b, hd = ATT_BLOCK, ATT_HEAD_DIM
    q_spec = pl.BlockSpec((b, rep * hd), lambda h, n: (cur(n), h))
    kc_spec = pl.BlockSpec((b, hd), lambda h, n: (cur(n), h))
    kp_spec = pl.BlockSpec((b, hd), lambda h, n: (prv(n), h))
    tc_spec = pl.BlockSpec((b, hd), lambda h, n: (cur(n), 0))
    tp_spec = pl.BlockSpec((b, hd), lambda h, n: (prv(n), 0))
    return q_spec, kc_spec, kp_spec, tc_spec, tp_spec


def _attn_fwd(q, k, v, tabs, n_blk, name):
    s = q.shape[0]
    b, hd = ATT_BLOCK, ATT_HEAD_DIM
    nb = s // b
    n_kv = ATT_KV_HEADS_PER_GROUP
    rep = ATT_HEADS_PER_GROUP // n_kv
    scale = hd ** -0.5

    def body(q_ref, kc_ref, kp_ref, vc_ref, vp_ref, cfc, s1c, s2c, cfp, s1p, s2p, o_ref, lse_ref):
        n = pl.program_id(1)
        tc = (cfc[...], s1c[...], s2c[...])
        tp = (cfp[...], s1p[...], s2p[...])
        qv = q_ref[...]
        q4 = jnp.concatenate([_rot(qv[:, j * hd:(j + 1) * hd], tc, 1.0) for j in range(rep)], axis=0).astype(BF16)
        kc = _rot(kc_ref[...], tc, 1.0).astype(BF16)
        kp = _rot(kp_ref[...], tp, 1.0).astype(BF16)
        mc, mp = _att_masks(n, n_blk, rep)
        sc = jnp.where(mc, _dot(q4, kc, _NT) * scale, NEG)
        sp = jnp.where(mp, _dot(q4, kp, _NT) * scale, NEG)
        m = jnp.maximum(jnp.max(sc, axis=1, keepdims=True), jnp.max(sp, axis=1, keepdims=True))
        pc, pp = jnp.exp(sc - m), jnp.exp(sp - m)
        l = jnp.sum(pc, axis=1, keepdims=True) + jnp.sum(pp, axis=1, keepdims=True)
        o = (_dot(pc.astype(BF16), vc_ref[...].astype(BF16)) + _dot(pp.astype(BF16), vp_ref[...].astype(BF16))) / l
        lse = jnp.broadcast_to(m + jnp.log(l), (rep * b, hd))
        for j in range(rep):
            o_ref[:, j * hd:(j + 1) * hd] = o[j * b:(j + 1) * b]
            lse_ref[:, j * hd:(j + 1) * hd] = lse[j * b:(j + 1) * b]

    cur = lambda n: n
    prv = lambda n: jnp.maximum(n - 1, 0)
    q_spec, kc_spec, kp_spec, tc_spec, tp_spec = _att_specs(nb, rep, cur, prv)
    return pl.pallas_call(
        body, name=name, grid=(n_kv, nb),
        in_specs=[q_spec, kc_spec, kp_spec, kc_spec, kp_spec, tc_spec, tc_spec, tc_spec, tp_spec, tp_spec, tp_spec],
        out_specs=[q_spec, q_spec],
        out_shape=[jax.ShapeDtypeStruct(q.shape, F32), jax.ShapeDtypeStruct(q.shape, F32)],
        compiler_params=_params(("parallel", "arbitrary")))(q, k, k, v, v, *tabs, *tabs)


def _attn_bwd(q, k, v, do, lse, delta, tabs, n_blk, name):
    s = q.shape[0]
    b, hd = ATT_BLOCK, ATT_HEAD_DIM
    nb = s // b
    n_kv = ATT_KV_HEADS_PER_GROUP
    rep = ATT_HEADS_PER_GROUP // n_kv
    scale = hd ** -0.5

    def body(q_ref, do_ref, lse_ref, dl_ref, kc_ref, kp_ref, vc_ref, vp_ref, cfc, s1c, s2c, cfp, s1p, s2p,
             dq_ref, dk_ref, dv_ref, ck, cv):
        n = pl.program_id(1)
        tp = (cfp[...], s1p[...], s2p[...])

        @pl.when(n == 0)
        def _():
            ck[...] = jnp.zeros(ck.shape, F32)
            cv[...] = jnp.zeros(cv.shape, F32)

        @pl.when(n < nb)
        def _():
            tc = (cfc[...], s1c[...], s2c[...])
            qv = q_ref[...]
            q4 = jnp.concatenate([_rot(qv[:, j * hd:(j + 1) * hd], tc, 1.0) for j in range(rep)],
                                 axis=0).astype(BF16)
            do4 = _stack(do_ref[...], rep).astype(BF16)
            lse4 = _stack(lse_ref[...], rep)
            dl4 = _stack(dl_ref[...], rep)
            kc = _rot(kc_ref[...], tc, 1.0).astype(BF16)
            kp = _rot(kp_ref[...], tp, 1.0).astype(BF16)
            vc, vp = vc_ref[...].astype(BF16), vp_ref[...].astype(BF16)
            mc, mp = _att_masks(n, n_blk, rep)
            pc = jnp.where(mc, jnp.exp(jnp.where(mc, _dot(q4, kc, _NT) * scale - lse4, 0.0)), 0.0)
            pp = jnp.where(mp, jnp.exp(jnp.where(mp, _dot(q4, kp, _NT) * scale - lse4, 0.0)), 0.0)
            dsc = (pc * (_dot(do4, vc, _NT) - dl4)).astype(BF16)
            dsp = (pp * (_dot(do4, vp, _NT) - dl4)).astype(BF16)
            dq4 = (_dot(dsc, kc) + _dot(dsp, kp)) * scale
            for j in range(rep):
                dq_ref[:, j * hd:(j + 1) * hd] = _rot(dq4[j * b:(j + 1) * b], tc, -1.0).astype(dq_ref.dtype)
            dk_prev = ck[...] + _dot(dsp, q4, _TN) * scale
            dv_prev = cv[...] + _dot(pp.astype(BF16), do4, _TN)
            dk_ref[...] = _rot(dk_prev, tp, -1.0).astype(dk_ref.dtype)
            dv_ref[...] = dv_prev.astype(dv_ref.dtype)
            ck[...] = _dot(dsc, q4, _TN) * scale
            cv[...] = _dot(pc.astype(BF16), do4, _TN)

        @pl.when(n == nb)
        def _():
            dk_ref[...] = _rot(ck[...], tp, -1.0).astype(dk_ref.dtype)
            dv_ref[...] = cv[...].astype(dv_ref.dtype)

    cur = lambda n: jnp.minimum(n, nb - 1)
    prv = lambda n: jnp.maximum(n - 1, 0)
    q_spec, kc_spec, kp_spec, tc_spec, tp_spec = _att_specs(nb, rep, cur, prv)
    return pl.pallas_call(
        body, name=name, grid=(n_kv, nb + 1),
        in_specs=[q_spec, q_spec, q_spec, q_spec, kc_spec, kp_spec, kc_spec, kp_spec,
                  tc_spec, tc_spec, tc_spec, tp_spec, tp_spec, tp_spec],
        out_specs=[q_spec, kp_spec, kp_spec],
        out_shape=[jax.ShapeDtypeStruct(q.shape, BF16), jax.ShapeDtypeStruct(k.shape, BF16),
                   jax.ShapeDtypeStruct(k.shape, BF16)],
        scratch_shapes=[pltpu.VMEM((b, hd), F32), pltpu.VMEM((b, hd), F32)],
        compiler_params=_params(("parallel", "arbitrary")))(q, do, lse, delta, k, k, v, v, *tabs, *tabs)


def _rot_heads(x, tabs, width, sign, out_dtype, name):
    s = x.shape[0]
    hd = ATT_HEAD_DIM
    tile = min(512, s)

    def body(x_ref, cf, s1, s2, o_ref):
        t = (cf[...], s1[...], s2[...])
        for j in range(width // hd):
            o_ref[:, j * hd:(j + 1) * hd] = _rot(x_ref[:, j * hd:(j + 1) * hd], t, sign).astype(o_ref.dtype)

    tab = pl.BlockSpec((tile, hd), lambda i: (i, 0))
    return pl.pallas_call(
        body, name=name, grid=(s // tile,), in_specs=[pl.BlockSpec((tile, width), lambda i: (i, 0)), tab, tab, tab],
        out_specs=pl.BlockSpec((tile, width), lambda i: (i, 0)), out_shape=jax.ShapeDtypeStruct((s, width), out_dtype),
        compiler_params=_params(("parallel",)))(x, *tabs)


def _rows_of(r, dil):
    return pl.ds(r, ATT_BLOCK, stride=dil) if dil > 1 else slice(None)


def _nat_specs(g, dil, n_kv_all, cur, prv):
    b, hd = ATT_BLOCK * dil, ATT_HEAD_DIM
    n_kv = ATT_KV_HEADS_PER_GROUP
    rep = ATT_HEADS_PER_GROUP // n_kv
    q_all = [pl.BlockSpec((b, hd), lambda h, n, j=j: (cur(n), (g * n_kv + h) * rep + j)) for j in range(rep)]
    q_own = [pl.BlockSpec((b, hd), lambda h, n, j=j: (cur(n), h * rep + j)) for j in range(rep)]
    hm_all = pl.BlockSpec((rep, b, hd), lambda h, n: (g * n_kv + h, cur(n), 0))
    hm_own = pl.BlockSpec((rep, b, hd), lambda h, n: (h, cur(n), 0))
    kc =pl.BlockSpec((b, hd), lambda h, n: (cur(n), g * n_kv + h))
    kp = pl.BlockSpec((b, hd), lambda h, n: (prv(n), g * n_kv + h))
    vc = pl.BlockSpec((b, hd), lambda h, n: (cur(n), n_kv_all + g * n_kv + h))
    vp = pl.BlockSpec((b, hd), lambda h, n: (prv(n), n_kv_all + g * n_kv + h))
    tab = pl.BlockSpec((b, hd), lambda h, n: (cur(n), 0))
    stat = pl.BlockSpec((None, b, LANES), lambda h, n: (h, cur(n), 0))
    return q_all, q_own, hm_all, hm_own, kc, kp, vc, vp, tab, stat


def _head_cols(stat, rep):
    return jnp.concatenate([jnp.broadcast_to(stat[:, j:j + 1], stat.shape) for j in range(rep)], axis=0)


def _attn_fwd_nat(q_all, k_rot, kv, tabs, g, dil, name):
    s = q_all.shape[0]
    b, hd = ATT_BLOCK, ATT_HEAD_DIM
    nbn = s // (b * dil)
    n_kv = ATT_KV_HEADS_PER_GROUP
    rep = ATT_HEADS_PER_GROUP // n_kv
    n_kv_all = k_rot.shape[1] // hd
    scale = hd ** -0.5

    def body(*refs):
        q_refs = refs[:rep]
        kc_ref, kp_ref, vc_ref, vp_ref, cf, s1, s2, o_ref, lse_ref = refs[rep:]
        mc, mp = _att_masks(jnp.where(pl.program_id(1) > 0, 1, 0), 2, rep)
        for r in range(dil):
            sl = _rows_of(r, dil)
            tc = (cf[sl, :], s1[sl, :], s2[sl, :])
            q4 = jnp.concatenate([_rot(q_ref[sl, :], tc, 1.0) for q_ref in q_refs], axis=0).astype(BF16)
            kc, kp = kc_ref[sl, :].astype(BF16), kp_ref[sl, :].astype(BF16)
            sc = jnp.where(mc, _dot(q4, kc, _NT) * scale, NEG)
            sp = jnp.where(mp, _dot(q4, kp, _NT) * scale, NEG)
            m = jnp.maximum(jnp.max(sc, axis=1, keepdims=True), jnp.max(sp, axis=1, keepdims=True))
            pc, pp = jnp.exp(sc - m), jnp.exp(sp - m)
            l = jnp.sum(pc, axis=1, keepdims=True) + jnp.sum(pp, axis=1, keepdims=True)
            o = (_dot(pc.astype(BF16), vc_ref[sl, :].astype(BF16))
                 + _dot(pp.astype(BF16), vp_ref[sl, :].astype(BF16))) / l
            lse = m + jnp.log(l)
            for j in range(rep):
                o_ref[j, sl, :] = o[j * b:(j + 1) * b]
            lse_ref[sl, :] = _lane_place([lse[j * b:(j + 1) * b] for j in range(rep)])

    cur = lambda n: n
    prv = lambda n: jnp.maximum(n - 1, 0)
    q_specs, _, _, hm_own, kc, kp, vc, vp, tab, stat = _nat_specs(g, dil, n_kv_all, cur, prv)
    return pl.pallas_call(
        body, name=name, grid=(n_kv, nbn), in_specs=[*q_specs, kc, kp, vc, vp, tab, tab, tab], out_specs=[hm_own, stat],
        out_shape=[jax.ShapeDtypeStruct((ATT_HEADS_PER_GROUP, s, hd), F32), jax.ShapeDtypeStruct((n_kv, s, LANES), F32)],
        compiler_params=_params(("parallel", "arbitrary")))(*([q_all] * rep), k_rot, k_rot, kv, kv, *tabs)


def _attn_bwd_nat(q_all, k_rot, kv, do, lse, delta, tabs, grads, g, dil, name):
    s = q_all.shape[0]
    b, hd = ATT_BLOCK, ATT_HEAD_DIM
    nbn = s // (b * dil)
    n_kv = ATT_KV_HEADS_PER_GROUP
    rep = ATT_HEADS_PER_GROUP // n_kv
    n_kv_all = k_rot.shape[1] // hd
    scale = hd ** -0.5

    def body(*refs):
        q_refs, do_refs = refs[:rep], refs[rep:2 * rep]
        (lse_ref, dl_ref, kc_ref, kp_ref, vc_ref, vp_ref, cf, s1, s2, _, _, _,
         dq_ref, dk_ref, dv_ref, ck, cv) = refs[2 * rep:]
        n = pl.program_id(1)

        @pl.when(n == 0)
        def _():
            ck[...] = jnp.zeros(ck.shape, F32)
            cv[...] = jnp.zeros(cv.shape, F32)

        @pl.when(n < nbn)
        def _():
            mc, mp = _att_masks(jnp.where(n > 0, 1, 0), 2, rep)
            for r in range(dil):
                sl = _rows_of(r, dil)
                own = slice(r * b, (r + 1) * b)
                tc = (cf[sl, :], s1[sl, :], s2[sl, :])
                q4 = jnp.concatenate([_rot(q_ref[sl, :], tc, 1.0) for q_ref in q_refs], axis=0).astype(BF16)
                do4 = jnp.concatenate([do_ref[sl, :] for do_ref in do_refs], axis=0).astype(BF16)
                lse4 = _head_cols(lse_ref[sl, :], rep)
                dl4 = _head_cols(dl_ref[sl, :], rep)
                kc, kp = kc_ref[sl, :].astype(BF16), kp_ref[sl, :].astype(BF16)
                vc, vp = vc_ref[sl, :].astype(BF16), vp_ref[sl, :].astype(BF16)
                pc = jnp.where(mc, jnp.exp(_dot(q4, kc, _NT) * scale - lse4), 0.0)
                pp = jnp.where(mp, jnp.exp(_dot(q4, kp, _NT) * scale - lse4), 0.0)
                dsc = (pc * (_dot(do4, vc, _NT) - dl4)).astype(BF16)
                dsp = (pp * (_dot(do4, vp, _NT) - dl4)).astype(BF16)
                dq4 = (_dot(dsc, kc) + _dot(dsp, kp)) * scale
                for j in range(rep):
                    dq_ref[j, sl, :] = _rot(dq4[j * b:(j + 1) * b], tc, -1.0)
                dk_ref[sl, :] = ck[own, :] + _dot(dsp, q4, _TN) * scale
                dv_ref[sl, :] = cv[own, :] + _dot(pp.astype(BF16), do4, _TN)
                ck[own, :] = _dot(dsc, q4, _TN) * scale
                cv[own, :] = _dot(pc.astype(BF16), do4, _TN)

        @pl.when(n == nbn)
        def _():
            for r in range(dil):
                sl = _rows_of(r, dil)
                dk_ref[sl, :] = ck[r * b:(r + 1) * b, :]
                dv_ref[sl, :] = cv[r * b:(r + 1) * b, :]

    cur = lambda n: jnp.minimum(n, nbn - 1)
    prv = lambda n: jnp.maximum(n - 1, 0)
    q_specs, do_specs, hm_all, _, kc, kp, vc, vp, tab, stat = _nat_specs(g, dil, n_kv_all, cur, prv)
    anyspace = pl.BlockSpec(memory_space=pl.ANY)
    n_in = 2 * rep + 9
    return pl.pallas_call(
        body, name=name, grid=(n_kv, nbn + 1),
        in_specs=[*q_specs, *do_specs, stat, stat, kc, kp, vc, vp, tab, tab, tab, anyspace, anyspace, anyspace],
        out_specs=[hm_all, kp, kp], out_shape=[jax.ShapeDtypeStruct(a.shape, a.dtype) for a in grads],
        input_output_aliases={n_in: 0, n_in + 1: 1, n_in + 2: 2},
        scratch_shapes=[pltpu.VMEM((dil * b, hd), F32), pltpu.VMEM((dil * b, hd), F32)],
        compiler_params=_params(("parallel", "arbitrary"), VMEM_LIMIT_ATTN_BWD_BYTES))(
            *([q_all] * rep), *([do] * rep), lse, delta, k_rot, k_rot, kv, kv, *tabs, *grads)


def _adamw(g_slabs, w, m, v, name):
    kk, r, c = g_slabs.shape
    tile = r if r <= 256 else _pick_rows(r, 256)

    def body(g_ref, w_ref, m_ref, v_ref, go_ref, d_ref, mo_ref, vo_ref):
        g = g_ref[0].astype(F32)
        for k in range(1, kk):
            g = g + g_ref[k].astype(F32)
        m2 = ADAM_B1 * m_ref[...] + (1.0 - ADAM_B1) * g
        v2 = ADAM_B2 * v_ref[...] + (1.0 - ADAM_B2) * jnp.square(g)
        m_hat = m2 / (1.0 - ADAM_B1 ** ADAM_STEP)
        v_hat = v2 / (1.0 - ADAM_B2 ** ADAM_STEP)
        go_ref[...] = g
        d_ref[...] = -ADAM_LR * (m_hat / (jnp.sqrt(v_hat) + ADAM_EPS) + ADAM_WD * w_ref[...])
        mo_ref[...] = m2
        vo_ref[...] = v2

    spec = pl.BlockSpec((tile, c), lambda i: (i, 0))
    return pl.pallas_call(
        body, name=name, grid=(r // tile,), in_specs=[pl.BlockSpec((kk, tile, c), lambda i: (0, i, 0)), spec, spec, spec],
        out_specs=[spec] * 4, out_shape=[jax.ShapeDtypeStruct((r, c), F32)] * 4,
        compiler_params=_params(("parallel",)))(g_slabs, w, m, v)


def _pick_rows(r, pref):
    t = (pref // 16) * 16
    while t >= 16:
        if r % t == 0:
            return t
        t -= 16
    return r


def _coords():
    return lax.axis_index("x"), lax.axis_index("y"), lax.axis_index("c")


def _dev_index(px, py, pc):
    return 4 * px + 2 * py + pc


def _all_gather(shards, name):
    na = len(shards)

    def body(*refs):
        ins, outs = refs[:na], refs[na:2 * na]
        send_sems, recv_sems, local_sems = refs[2 * na:]
        x, y, c = _coords()
        me, sibling = (x, y, c), (x, y, 1 - c)
        chips = [(1 - x, y), (x, 1 - y), (1 - x, 1 - y)]

        def copy(a, k, block, to, src=None):
            dst = outs[a].at[_dev_index(*block)]
            return pltpu.make_async_remote_copy(
                src_ref=dst if src is None else src, dst_ref=dst, send_sem=send_sems.at[a * 7 + k],
                recv_sem=recv_sems.at[a * 7 + k], device_id=to, device_id_type=MESH)

        mine = [pltpu.make_async_copy(ins[a], outs[a].at[_dev_index(*me)], local_sems.at[a]) for a in range(na)]
        for cp in mine:
            cp.start()
        first = []
        for a in range(na):
            first.append(copy(a, 0, me, sibling, src=ins[a]))
            first += [copy(a, 1 + j, me, (*chip, c), src=ins[a]) for j, chip in enumerate(chips)]
        for cp in first:
            cp.start()
        passed = []
        for j, chip in enumerate(chips):
            for a in range(na):
                copy(a, 1 + j, (*chip, c), me).wait_recv()
                cp = copy(a, 4 + j, (*chip, c), sibling)
                cp.start()
                passed.append(cp)
        for a in range(na):
            copy(a, 0, sibling, me).wait_recv()
            for j, chip in enumerate(chips):
                copy(a, 4 + j, (*chip, 1 - c), me).wait_recv()
        for cp in first + passed:
            cp.wait_send()
        for cp in mine:
            cp.wait()

    hbm = pl.BlockSpec(memory_space=pl.ANY)
    return pl.pallas_call(
        body, name=name, in_specs=[hbm] * na, out_specs=[hbm] * na,
        out_shape=[jax.ShapeDtypeStruct((NDEV,) + s.shape, s.dtype) for s in shards],
        scratch_shapes=[pltpu.SemaphoreType.DMA((7 * na,)), pltpu.SemaphoreType.DMA((7 * na,)),
                        pltpu.SemaphoreType.DMA((na,))])(*shards)


def _exchange(slabs, whole, name):
    ns, nw = len(slabs), len(whole)
    na = ns + nw

    def body(*refs):
        ins, outs = refs[:na], refs[na:2 * na]
        send_sems, recv_sems, local_sems = refs[2 * na:]
        x, y, c = _coords()
        me = _dev_index(x, y, c)

        def src_of(a, p):
            return ins[a].at[p] if a < ns else ins[a]

        def copy(a, k, peer):
            p = _dev_index(*peer)
            return pltpu.make_async_remote_copy(
                src_ref=src_of(a, p), dst_ref=outs[a].at[me], send_sem=send_sems.at[a * 7 + k - 1],
                recv_sem=recv_sems.at[a * 7 + k - 1], device_id=peer, device_id_type=MESH)

        def arrival(a, k, peer):
            p = _dev_index(*peer)
            return pltpu.make_async_remote_copy(
                src_ref=src_of(a, p), dst_ref=outs[a].at[p], send_sem=send_sems.at[a * 7 + k - 1],
                recv_sem=recv_sems.at[a * 7 + k - 1], device_id=peer, device_id_type=MESH)

        mine = [pltpu.make_async_copy(src_of(a, me), outs[a].at[me], local_sems.at[a]) for a in range(na)]
        for cp in mine:
            cp.start()
        peers = [(k, (x ^ (k >> 2), y ^ ((k >> 1) & 1), c ^ (k & 1))) for k in range(1, NDEV)]
        sent = [copy(a, k, peer) for k, peer in peers for a in range(na)]
        for cp in sent:
            cp.start()
        for k, peer in peers:
            for a in range(na):
                arrival(a, k, peer).wait_recv()
        for cp in sent:
            cp.wait_send()
        for cp in mine:
            cp.wait()

    hbm = pl.BlockSpec(memory_space=pl.ANY)
    out_shape = [jax.ShapeDtypeStruct(s.shape, s.dtype) for s in slabs]
    out_shape += [jax.ShapeDtypeStruct((NDEV,) + w.shape, w.dtype) for w in whole]
    return pl.pallas_call(
        body, name=name, in_specs=[hbm] * na, out_specs=[hbm] * na, out_shape=out_shape,
        scratch_shapes=[pltpu.SemaphoreType.DMA((7 * na,)), pltpu.SemaphoreType.DMA((7 * na,)),
                        pltpu.SemaphoreType.DMA((na,))])(*slabs, *whole)


_HBM = pl.BlockSpec(memory_space=pltpu.HBM)
_SEM = pl.BlockSpec(memory_space=pltpu.SEMAPHORE)
_EFFECT = pltpu.SideEffectType.DATAFLOW_SIDE_EFFECTING


def _peers(x, y, c):
    return [(k, (x ^ (k >> 2), y ^ ((k >> 1) & 1), c ^ (k & 1))) for k in range(1, NDEV)]


def _peer_copy(src, land, send_sems, recv_sems, a, k, dst_block, peer):
    return pltpu.make_async_remote_copy(
        src_ref=src, dst_ref=land.at[dst_block], send_sem=send_sems.at[a * 7 + k - 1],
        recv_sem=recv_sems.at[a * 7 + k - 1], device_id=peer, device_id_type=MESH)


def _send_start(arrays, slabs, name):
    na = len(arrays)
    lands = [jax.ShapeDtypeStruct(a.shape if slabs else (NDEV,) + a.shape, a.dtype) for a in arrays]

    def body(*refs):
        ins, zones = refs[:na], refs[na:2 * na]
        send_sems, recv_sems = refs[2 * na], refs[2 * na + 1]
        token = refs[-1]
        x, y, c = _coords()
        me = _dev_index(x, y, c)
        for k, peer in _peers(x, y, c):
            for a in range(na):
                src = ins[a].at[_dev_index(*peer)] if slabs else ins[a]
                _peer_copy(src, zones[a], send_sems, recv_sems, a, k, me, peer).start()
        token[...] = jnp.zeros_like(token)

    outs = pl.pallas_call(
        body, name=name,
        out_shape=(pltpu.SemaphoreType.DMA((7 * na,)), pltpu.SemaphoreType.DMA((7 * na,)),
                   *[pltpu.HBM(a.shape, a.dtype) for a in arrays], *[pltpu.HBM(l.shape, l.dtype) for l in lands],
                   jax.ShapeDtypeStruct((8, LANES), F32)),
        in_specs=[_HBM] * (2 * na), out_specs=(_SEM, _SEM, *([_HBM] * (2 * na)), pl.BlockSpec(memory_space=pltpu.VMEM)),
        input_output_aliases={i: 2 + i for i in range(2 * na)},
        compiler_params=pltpu.CompilerParams(has_side_effects=_EFFECT),
    )(*[pltpu.with_memory_space_constraint(a, pltpu.HBM) for a in arrays],
      *[pltpu.with_memory_space_constraint(lax.empty(l.shape, l.dtype), pltpu.HBM) for l in lands])
    return outs[0], outs[1], list(outs[2:2 + na]), list(outs[2 + na:2 + 2 * na]), outs[-1]


def _send_wait(started, after, slabs, name):
    send_sems, recv_sems, thru, zones, _ = started
    na = len(thru)

    def body(*refs):
        ins, lands = refs[:na], refs[na:2 * na]
        s_sems, r_sems = refs[2 * na], refs[2 * na + 1]
        x, y, c = _coords()
        for k, peer in _peers(x, y, c):
            p = _dev_index(*peer)
            for a in range(na):
                src = ins[a].at[p] if slabs else ins[a]
                cp = _peer_copy(src, lands[a], s_sems, r_sems, a, k, p, peer)
                cp.wait_send()
                cp.wait_recv()

    outs = pl.pallas_call(
        body, name=name, out_shape=tuple(pltpu.HBM(v.shape, v.dtype) for v in thru + zones),
        in_specs=[_HBM] * (2 * na) + [_SEM, _SEM, pl.BlockSpec(memory_space=pl.ANY)], out_specs=tuple([_HBM] * (2 * na)),
        input_output_aliases={i: i for i in range(2 * na)},
        compiler_params=pltpu.CompilerParams(has_side_effects=_EFFECT),
    )(*thru, *zones, send_sems, recv_sems, after)
    me = _dev_index(*_coords())
    filled = []
    for a in range(na):
        own = lax.dynamic_index_in_dim(outs[a], me, 0, keepdims=False) if slabs else outs[a]
        filled.append(lax.dynamic_update_index_in_dim(outs[na + a], own, me, 0))
    return filled


def _pack(vecs):
    parts, spans, off = [], [], 0
    for v in vecs:
        n = v.size
        pad = (-n) % LANES
        parts.append(jnp.pad(v.reshape(-1).astype(F32), (0, pad)))
        spans.append((off, n))
        off += n + pad
    return jnp.concatenate(parts).reshape(-1, LANES), spans


def _pad_lanes(v):
    v = v.reshape(1, -1)
    return jnp.pad(v, ((0, 0), (0, LANES - v.shape[1])))


def _cols_to_slabs(g):
    sh = g.shape
    g = g.reshape(sh[:-1] + (NDEV, sh[-1] // NDEV))
    return jnp.moveaxis(g, -2, 0)


def _rows_to_slabs(g):
    sh = g.shape
    g = g.reshape(sh[:-2] + (NDEV, sh[-2] // NDEV, sh[-1]))
    return jnp.moveaxis(g, -3, 0)


def _slabs_to_cols(a):
    a = jnp.moveaxis(a, 0, -2)
    return a.reshape(a.shape[:-2] + (a.shape[-2] * a.shape[-1],))


def _slabs_to_rows(a):
    a = jnp.moveaxis(a, 0, -3)
    return a.reshape(a.shape[:-3] + (a.shape[-3] * a.shape[-2], a.shape[-1]))


def _ffn_forward(x, norm_w, wup_g, wup_v, cw_g, cw_v, wdown, tag):
    h = _rms_fwd(x, norm_w, f"{tag}_norm")
    ug = _mm(h, wup_g, name=f"{tag}_up_gate")
    uv = _mm(h, wup_v, name=f"{tag}_up_val")
    f = _ffn_gate_fwd(ug, uv, cw_g, cw_v, f"{tag}_gate")
    return _mm(f, wdown, res=x, name=f"{tag}_down"), (h, ug, uv, f)


def _ffn_backward(x, saved, dout, dout_b, norm_w, wup_g, wup_v, cw_g, cw_v, wdown, tag):
    h, ug, uv, f = saved
    dwdown = _mm(f, dout_b, ta=True, name=f"{tag}_dwdown")
    df = _mm(dout_b, wdown, tb=True, name=f"{tag}_df")
    dug, duv, dcg, dcv = _ffn_gate_bwd(ug, uv, cw_g, cw_v, df, f"{tag}_gate_bwd")
    dwg = _mm(h, dug, ta=True, name=f"{tag}_dwup_gate")
    dwv = _mm(h, duv, ta=True, name=f"{tag}_dwup_val")
    dh = _mm(dug, wup_g, tb=True, name=f"{tag}_dh_gate")
    dh = _mm(duv, wup_v, tb=True, res=dh, name=f"{tag}_dh_val")
    dx, dxb, dnorm = _rms_bwd(x, norm_w, dh, dout, f"{tag}_norm_bwd")
    return dx, dxb, (jnp.concatenate([dwg, dwv], axis=1), jnp.concatenate([dcg, dcv], axis=1), dwdown, dnorm)


def kernel(x, a_norm, ssm_w_in, ssm_conv_w, ssm_conv_b, ssm_dt_bias, ssm_a_log, ssm_d, ssm_norm, ssm_w_out, kv_norm, w_kv, b_norm, att_w_q, att_w_o, ffn_norm, ffn_w_up, ffn_conv_w, ffn_w_down, final_norm, loss_target, m_a_norm, m_ssm_w_in, m_ssm_conv_w, m_ssm_conv_b, m_ssm_dt_bias, m_ssm_a_log, m_ssm_d, m_ssm_norm, m_ssm_w_out, m_kv_norm, m_w_kv, m_b_norm, m_att_w_q, m_att_w_o, m_ffn_norm, m_ffn_w_up, m_ffn_conv_w, m_ffn_w_down, m_final_norm, v_a_norm, v_ssm_w_in, v_ssm_conv_w, v_ssm_conv_b, v_ssm_dt_bias, v_ssm_a_log, v_ssm_d, v_ssm_norm, v_ssm_w_out, v_kv_norm, v_w_kv, v_b_norm, v_att_w_q, v_att_w_o, v_ffn_norm, v_ffn_w_up, v_ffn_conv_w, v_ffn_w_down, v_final_norm):
    given = dict(locals())
    xs, tgt = x[0], loss_target[0]
    s, d = xs.shape
    di = ssm_w_out.shape[1] * NDEV
    nh = ssm_dt_bias.shape[1]
    ng = SSM_N_GROUPS
    convd = di + 2 * ng * SSM_D_STATE
    f = ffn_w_down.shape[1] * NDEV
    n_att = len(ATT_PATTERNS)
    qg = ATT_HEADS_PER_GROUP * ATT_HEAD_DIM
    kg = ATT_KV_HEADS_PER_GROUP * ATT_HEAD_DIM
    kvd = n_att * kg
    assert all(w // dil == ATT_BLOCK for w, dil in ATT_PATTERNS)

    small, _ = _pack([a_norm, ssm_conv_w, ssm_conv_b, ssm_norm, ffn_conv_w])
    gat = _all_gather([ssm_w_in[0].astype(BF16), ssm_w_out[0].astype(BF16), small], "gather_weights")
    first = _send_start([ffn_w_up[0].astype(BF16), ffn_w_down[0].astype(BF16)], False, "gather_ffn0_start")
    rest = _send_start([b.astype(BF16) for b in (w_kv, att_w_q[0], att_w_o[0], ffn_w_up[1], ffn_w_down[1])], False,
                       "gather_rest_start")
    w_in = _slabs_to_cols(gat[0])
    w_z, w_xbc = w_in[:, :di], w_in[:, di:di + convd]
    w_dt = jnp.pad(w_in[:, di + convd:], ((0, 0), (0, LANES - nh)))
    w_out = _slabs_to_rows(gat[1])
    sm = gat[2].reshape(NDEV, -1)
    o0 = 0

    def take(shape):
        nonlocal o0
        n = math.prod(shape)
        out = sm[:, o0:o0 + n].reshape((NDEV,) + shape)
        o0 += n + (-n) % LANES
        return out
    a_norm_f = _slabs_to_cols(take(a_norm.shape)) + (first[-1][0, 0] + rest[-1][0, 0])
    conv_w_f = _slabs_to_cols(take(ssm_conv_w.shape))[0]
    conv_b_f = _slabs_to_cols(take(ssm_conv_b.shape))
    ssm_norm_f = _slabs_to_cols(take(ssm_norm.shape))
    fcw = _slabs_to_cols(take(ffn_conv_w.shape))
    fcw_g, fcw_v = fcw[:, :, :f], fcw[:, :, f:]
    dtb, alog, dsk = _pad_lanes(ssm_dt_bias), _pad_lanes(ssm_a_log), _pad_lanes(ssm_d)
    kvn, fin = kv_norm.reshape(1, d), final_norm.reshape(1, d)

    h0 = _rms_fwd(xs, a_norm_f, "a_norm")
    z = _mm(h0, w_z, name="in_z")
    xbc_pre = _mm(h0, w_xbc, name="in_xbc")
    dtr = _mm(h0, w_dt, name="in_dt")
    xbc = _conv_silu_fwd(xbc_pre, conv_w_f, conv_b_f, "ssm_conv")
    dt = _softplus_fwd(dtr, dtb, "ssm_dt")
    y, prevs = _ssd_fwd2(xbc, dt, alog, dsk, di, nh, ng, "ssd")
    yn = _gnorm_fwd(y, z, ssm_norm_f, ng, "ssm_gnorm")
    x1 = _mm(yn, w_out, res=xs, name="ssm_out")
    got = _send_wait(first, x1, False, "gather_ffn0_wait")
    w_up0, w_down0 = _slabs_to_cols(got[0]), _slabs_to_rows(got[1])
    x2, ffn0 = _ffn_forward(x1, ffn_norm[0:1], w_up0[:, :f], w_up0[:, f:], fcw_g[0], fcw_v[0], w_down0, "ffn0")
    got = _send_wait(rest, x2, False, "gather_rest_wait")
    w_kvf = _slabs_to_cols(got[0])
    w_q = _slabs_to_cols(got[1])
    w_o = _slabs_to_rows(got[2])
    w_up1, w_down1 = _slabs_to_cols(got[3]), _slabs_to_rows(got[4])
    w_up_g, w_up_v = (w_up0[:, :f], w_up1[:, :f]), (w_up0[:, f:], w_up1[:, f:])
    w_down = (w_down0, w_down1)
    hk = _rms_fwd(x2, kvn, "kv_norm")
    kv = _mm(hk, w_kvf, name="kv_proj")
    h2 = _rms_fwd(x2, b_norm, "b_norm")
    q = _mm(h2, w_q, name="q_proj")
    tabs = _rot_tables(s, 1)
    k_rot = _rot_heads(kv, tabs, kvd, 1.0, F32, "k_rot")
    att = [_attn_fwd_nat(q, k_rot, kv, tabs, g, dil, f"attn{g}") for g, (_, dil) in enumerate(ATT_PATTERNS)]
    o, ob, lse = _merge_heads([t[0] for t in att], [t[1] for t in att], "attn_merge")
    x3 = _mm(ob, w_o, res=x2, name="attn_out")
    x4, ffn1 = _ffn_forward(x3, ffn_norm[1:2], w_up_g[1], w_up_v[1], fcw_g[1], fcw_v[1], w_down[1], "ffn1")
    loss_part, dx4, dx4b, dfin = _final_loss(x4, fin, tgt, "loss_head")

    dx3, dx3b, (dwup1, dfc1, dwdown1, dfn1) = _ffn_backward(
        x3, ffn1, dx4, dx4b, ffn_norm[1:2], w_up_g[1], w_up_v[1], fcw_g[1], fcw_v[1], w_down[1], "ffn1")
    dw_o = _mm(ob, dx3b, ta=True, name="attn_dwo")
    do = _mm(dx3b, w_o, tb=True, name="attn_do")
    delta = _delta_heads(do, o, "attn_delta")
    grads = (lax.empty((n_att * qg // LANES, s, LANES), F32), lax.empty((s, kvd), F32), lax.empty((s, kvd), F32))
    for g, (_, dil) in enumerate(ATT_PATTERNS):
        grads = _attn_bwd_nat(q, k_rot, kv, do, lse, delta, tabs, grads, g, dil, f"attn{g}_bwd")
    dq, dk_rot, dv = grads
    dk = _rot_heads(dk_rot, tabs, kvd, -1.0, BF16, "k_rot_bwd")
    dw_q = _mm(h2, dq, ta=True, b_heads=True, name="q_dw")
    dh2 = _mm(dq, w_q, tb=True, a_heads=True, name="q_dh")
    dw_kv = jnp.concatenate([_mm(hk, dk, ta=True, name="k_dw"), _mm(hk, dv, ta=True, name="v_dw")], axis=1)
    dhk = _mm(dk, w_kvf[:, :kvd], tb=True, name="k_dh")
    dhk = _mm(dv, w_kvf[:, kvd:], tb=True, res=dhk, name="v_dh")
    dx2, _, db_norm = _rms_bwd(x2, b_norm, dh2, dx3, "b_norm_bwd")
    dx2, dx2b, dkv_norm = _rms_bwd(x2, kvn, dhk, dx2, "kv_norm_bwd")
    sent1 = _send_start([_cols_to_slabs(dwup1).astype(BF16), _rows_to_slabs(dwdown1).astype(BF16),
                         _cols_to_slabs(dw_kv).astype(BF16), _cols_to_slabs(dw_q).astype(BF16),
                         _rows_to_slabs(dw_o).astype(BF16)], True, "grads_late_start")
    dx1, dx1b, (dwup0, dfc0, dwdown0, dfn0) = _ffn_backward(
        x1, ffn0, dx2, dx2b, ffn_norm[0:1], w_up_g[0], w_up_v[0], fcw_g[0] + sent1[-1][0, 0], fcw_v[0], w_down[0],
        "ffn0")
    sent0 = _send_start([_cols_to_slabs(dwup0).astype(BF16), _rows_to_slabs(dwdown0).astype(BF16)], True,
                        "grads_ffn0_start")
    dw_out = _mm(yn, dx1b, ta=True, name="ssm_dwout")
    dyn = _mm(dx1b, w_out, tb=True, name="ssm_dyn")
    dy, dz, dssm_norm = _gnorm_bwd(dyn, y, z, ssm_norm_f + sent0[-1][0, 0], ng, "ssm_gnorm_bwd")
    dxbc, ddt, dalog, ddsk = _ssd_bwd2(xbc, dt, alog, dsk, prevs, dy, di, nh, ng, "ssd_bwd")
    ddtr, ddtb = _softplus_bwd(ddt, dtr, dtb, nh, "ssm_dt_bwd")
    dxbc_pre, dconv_w, dconv_b = _conv_silu_bwd(xbc_pre, conv_w_f, conv_b_f, dxbc, "ssm_conv_bwd")
    dw_z = _mm(h0, dz, ta=True, name="in_dwz")
    dw_xbc = _mm(h0, dxbc_pre, ta=True, name="in_dwxbc")
    dw_dt = _mm(h0, ddtr, ta=True, name="in_dwdt")[:, :nh]
    dh0 = _mm(dz, w_z, tb=True, name="in_dh_z")
    dh0 = _mm(dxbc_pre, w_xbc, tb=True, res=dh0, name="in_dh_xbc")
    dh0 = _mm(ddtr, w_dt, tb=True, res=dh0, name="in_dh_dt")
    dx0, _, da_norm = _rms_bwd(xs, a_norm_f, dh0, dx1, "a_norm_bwd")

    slabs = [_cols_to_slabs(jnp.concatenate([dw_z, dw_xbc, dw_dt], axis=1)).astype(BF16),
             _rows_to_slabs(dw_out).astype(BF16)]
    small_full = {
        'a_norm': da_norm, 'ssm_conv_w': dconv_w[None], 'ssm_conv_b': dconv_b, 'ssm_dt_bias': ddtb[:, :nh],
        'ssm_a_log': dalog[:, :nh], 'ssm_d': ddsk[:, :nh], 'ssm_norm': dssm_norm, 'kv_norm': dkv_norm.reshape(d),
        'b_norm': db_norm, 'ffn_norm': jnp.concatenate([dfn0, dfn1], axis=0), 'ffn_conv_w': jnp.stack([dfc0, dfc1]),
        'final_norm': dfin.reshape(d),
    }
    small_names = list(small_full)
    packed, spans = _pack([small_full[n] for n in small_names])
    recv = _exchange(slabs, [packed], "exchange_grads")
    small_sum = _sum_slabs(recv[-1], "sum_small_grads").reshape(-1)
    got1 = _send_wait(sent1, recv[-1], True, "grads_late_wait")
    got0 = _send_wait(sent0, recv[-1], True, "grads_ffn0_wait")
    recv_big = {
        'ssm_w_in': recv[0], 'ssm_w_out': recv[1], 'w_kv': got1[2], 'att_w_q': got1[3], 'att_w_o': got1[4],
        'ffn_w_up': jnp.concatenate([got0[0], got1[0]], axis=1),
        'ffn_w_down': jnp.concatenate([got0[1], got1[1]], axis=1),
    }

    me = _dev_index(*_coords())
    res = {}
    for n, r in recv_big.items():
        w = given[n]
        c = w.shape[-1]
        outs = _adamw(r.reshape(NDEV, -1, c), w.reshape(-1, c), given['m_' + n].reshape(-1, c),
                      given['v_' + n].reshape(-1, c), f"adamw_{n}")
        res[n] = [o_.reshape(w.shape) for o_ in outs]
    sharded_small = {'a_norm', 'ssm_conv_w', 'ssm_conv_b', 'ssm_norm', 'ffn_conv_w'}
    for n, (off, size) in zip(small_names, spans):
        w = given[n]
        gfull = small_sum[off:off + size].reshape(small_full[n].shape)
        if n in sharded_small:
            c = w.shape[-1]
            gfull = lax.dynamic_slice_in_dim(gfull, me * c, c, axis=gfull.ndim - 1)
        c = w.shape[-1]
        outs = _adamw(gfull.reshape(1, -1, c), w.reshape(-1, c), given['m_' + n].reshape(-1, c),
                      given['v_' + n].reshape(-1, c), f"adamw_{n}")
        res[n] = [o_.reshape(w.shape) for o_ in outs]

    loss = lax.psum(loss_part[0, 0], AXES)
    return (loss, dx0[None], *[res[n][0] for n in WEIGHTS], *[res[n][1] for n in WEIGHTS],
            *[res[n][2] for n in WEIGHTS], *[res[n][3] for n in WEIGHTS])
```

```python
import functools
import math

import jax
import jax.numpy as jnp
from jax import lax
from jax.experimental import pallas as pl
from jax.experimental.pallas import tpu as pltpu

F32, BF16 = jnp.float32, jnp.bfloat16
AXES = ("x", "y", "c")
NDEV = 8
MESH = pl.DeviceIdType.MESH
HIGHEST = lax.Precision.HIGHEST

LANES = 128
VMEM_LIMIT_BYTES = 48 * 1024 * 1024
VMEM_LIMIT_ATTN_BWD_BYTES = 58 * 1024 * 1024

RMS_EPS = 1e-6
GATED_NORM_EPS = 1e-5
SSM_HEAD_DIM = 64
SSM_N_GROUPS = 8
SSM_D_STATE = 128
SSM_CONV = 4
SSM_CHUNK = 128
ATT_PATTERNS = ((128, 1), (512, 4), (2048, 16))
ATT_HEAD_DIM = 128
ATT_HEADS_PER_GROUP = 8
ATT_KV_HEADS_PER_GROUP = 2
ATT_BLOCK = 128
ROPE_DIM = ATT_HEAD_DIM // 4
ROPE_THETA = 500000.0
FFN_CONV = 3
ADAM_LR = 0.001
ADAM_B1 = 0.9
ADAM_B2 = 0.999
ADAM_EPS = 1e-08
ADAM_WD = 0.01
ADAM_STEP = 10
NEG = -1e30

WEIGHTS = ['a_norm', 'ssm_w_in', 'ssm_conv_w', 'ssm_conv_b', 'ssm_dt_bias', 'ssm_a_log', 'ssm_d', 'ssm_norm',
           'ssm_w_out', 'kv_norm', 'w_kv', 'b_norm', 'att_w_q', 'att_w_o', 'ffn_norm', 'ffn_w_up', 'ffn_conv_w',
           'ffn_w_down', 'final_norm']


def _params(sem=None, vmem=VMEM_LIMIT_BYTES):
    kw = dict(vmem_limit_bytes=vmem)
    if sem is not None:
        kw["dimension_semantics"] = sem
    return pltpu.CompilerParams(**kw)


def _pick(n, pref):
    if n <= pref:
        return n
    t = (pref // LANES) * LANES
    while t >= LANES:
        if n % t == 0:
            return t
        t -= LANES
    return n


def _dot(a, b, dims=(((1,), (0,)), ((), ())), precision=None):
    return lax.dot_general(a, b, dims, precision=precision, preferred_element_type=F32)


_NT = (((1,), (1,)), ((), ()))
_TN = (((0,), (0,)), ((), ()))


def _mm(a, b, *, ta=False, tb=False, res=None, out_dtype=None, name, tm=1408, tn=1408, tk=2048,
        a_heads=False, b_heads=False):
    assert not (a_heads and ta) and not (b_heads and tb)
    if out_dtype is None:
        out_dtype = BF16 if ta else F32
    if a_heads:
        m, k = a.shape[1], a.shape[0] * LANES
    else:
        m = a.shape[1] if ta else a.shape[0]
        k = a.shape[0] if ta else a.shape[1]
    if b_heads:
        n, kb = b.shape[0] * LANES, b.shape[1]
    else:
        n = b.shape[0] if tb else b.shape[1]
        kb = b.shape[1] if tb else b.shape[0]
    assert k == kb
    tm, tn, tk = _pick(m, tm), _pick(n, tn), _pick(k, tk)
    nk = k // tk
    if a_heads:
        a_spec = pl.BlockSpec((tk // LANES, tm, LANES), lambda i, j, l: (l, i, 0))
    elif ta:
        a_spec = pl.BlockSpec((tk, tm), lambda i, j, l: (l, i))
    else:
        a_spec = pl.BlockSpec((tm, tk), lambda i, j, l: (i, l))
    if b_heads:
        b_spec = pl.BlockSpec((tn // LANES, tk, LANES), lambda i, j, l: (j, l, 0))
    elif tb:
        b_spec = pl.BlockSpec((tn, tk), lambda i, j, l: (j, l))
    else:
        b_spec = pl.BlockSpec((tk, tn), lambda i, j, l: (l, j))
    o_spec = pl.BlockSpec((tm, tn), lambda i, j, l: (i, j))
    dims = (((0 if ta else 1,), (1 if tb else 0,)), ((), ()))
    has_res = res is not None

    def load(ref, heads):
        if not heads:
            return ref[...].astype(BF16)
        return jnp.concatenate([ref[i].astype(BF16) for i in range(ref.shape[0])], axis=1)

    def body(*refs):
        a_ref, b_ref = refs[:2]
        r_ref = refs[2] if has_res else None
        o_ref = refs[2 + has_res]
        p = _dot(load(a_ref, a_heads), load(b_ref, b_heads), dims)

        def finish(r):
            if has_res:
                r = r + r_ref[...]
            o_ref[...] = r.astype(o_ref.dtype)

        if nk == 1:
            finish(p)
            return
        acc = refs[3 + has_res]
        l = pl.program_id(2)

        @pl.when(l == 0)
        def _():
            acc[...] = p

        @pl.when(jnp.logical_and(l > 0, l < nk - 1))
        def _():
            acc[...] += p

        @pl.when(l == nk - 1)
        def _():
            finish(acc[...] + p)

    ins = [a, b] + ([res] if has_res else [])
    in_specs = [a_spec, b_spec] + ([o_spec] if has_res else [])
    return pl.pallas_call(
        body, name=name, grid=(m // tm, n // tn, nk), in_specs=in_specs, out_specs=o_spec,
        out_shape=jax.ShapeDtypeStruct((m, n), out_dtype),
        scratch_shapes=[pltpu.VMEM((tm, tn), F32)] if nk > 1 else [],
        compiler_params=_params(("parallel", "parallel", "arbitrary")))(*ins)


def _rowwise(fn, rows, bcasts, outs, accs=(), *, tile, name):
    s = rows[0].shape[-2]
    tile = min(tile, s)
    n_in, n_out, n_acc = len(rows) + len(bcasts), len(outs), len(accs)

    def row_spec(c):
        if isinstance(c, tuple):
            return pl.BlockSpec((c[0], tile, c[1]), lambda i: (0, i, 0))
        return pl.BlockSpec((tile, c), lambda i: (i, 0))

    def row_shape(c):
        return (c[0], s, c[1]) if isinstance(c, tuple) else (s, c)

    def body(*refs):
        vals = fn(*[r[...] for r in refs[:n_in]])
        o_refs = refs[n_in:n_in + n_out]
        a_refs = refs[n_in + n_out:]
        for r, v in zip(o_refs, vals[:n_out]):
            if isinstance(v, list):
                for i, vi in enumerate(v):
                    r[i] = vi.astype(r.dtype)
            else:
                r[...] = v.astype(r.dtype)

        @pl.when(pl.program_id(0) == 0)
        def _():
            for r in a_refs:
                r[...] = jnp.zeros(r.shape, r.dtype)

        for r, v in zip(a_refs, vals[n_out:]):
            r[...] += v

    in_specs = [row_spec(r.shape[1] if r.ndim == 2 else (r.shape[0], r.shape[2])) for r in rows]
    in_specs += [pl.BlockSpec(b.shape, lambda i: (0, 0)) for b in bcasts]
    out_specs = [row_spec(c) for c, _ in outs]
    out_specs += [pl.BlockSpec(sh, lambda i: (0, 0)) for sh, _ in accs]
    out_shape = [jax.ShapeDtypeStruct(row_shape(c), dt) for c, dt in outs]
    out_shape += [jax.ShapeDtypeStruct(sh, dt) for sh, dt in accs]
    return pl.pallas_call(body, name=name, grid=(s // tile,), in_specs=in_specs, out_specs=out_specs,
                          out_shape=out_shape, compiler_params=_params(("arbitrary",)))(*rows, *bcasts)


def _rms_fwd(x, w, name):
    def fn(x, w):
        r = lax.rsqrt(jnp.mean(x * x, axis=-1, keepdims=True) + RMS_EPS)
        return (x * r * w,)
    return _rowwise(fn, [x], [w], [(x.shape[1], BF16)], tile=256, name=name)[0]


def _rms_bwd(x, w, dh, dres, name):
    def fn(x, dh, dres, w):
        r = lax.rsqrt(jnp.mean(x * x, axis=-1, keepdims=True) + RMS_EPS)
        xh = x * r
        dxh = dh * w
        dx = dres + r * (dxh - xh * jnp.mean(dxh * xh, axis=-1, keepdims=True))
        return dx, dx, jnp.sum(dh * xh, axis=0, keepdims=True)
    d = x.shape[1]
    return _rowwise(fn, [x, dh, dres], [w], [(d, F32), (d, BF16)], [((1, d), F32)], tile=256, name=name)


def _final_loss(x, w, tgt, name):
    d = x.shape[1]

    def fn(x, t, w):
        r = lax.rsqrt(jnp.mean(x * x, axis=-1, keepdims=True) + RMS_EPS)
        xh = x * r
        err = xh * w - t
        part = jnp.sum(jnp.mean(err * err, axis=-1, keepdims=True), axis=0, keepdims=True) * 0.5
        dy = err * (1.0 / d)
        dxh = dy * w
        dx = r * (dxh - xh * jnp.mean(dxh * xh, axis=-1, keepdims=True))
        return dx, dx, part, jnp.sum(dy * xh, axis=0, keepdims=True)
    dx, dxb, part, dw = _rowwise(fn, [x, tgt], [w], [(d, F32), (d, BF16)], [((1, 1), F32), ((1, d), F32)],
                                 tile=256, name=name)
    return part, dx, dxb, dw


def _softplus_fwd(dtr, bias, name):
    def fn(r, b):
        v = r + b
        return (jnp.maximum(v, 0.0) + jnp.log(1.0 + jnp.exp(-jnp.abs(v))),)
    return _rowwise(fn, [dtr], [bias], [(LANES, F32)], tile=512, name=name)[0]


def _softplus_bwd(ddt, dtr, bias, n_heads, name):
    def fn(g, r, b):
        lane = lax.broadcasted_iota(jnp.int32, g.shape, 1)
        d = jnp.where(lane < n_heads, g * jax.nn.sigmoid(r + b), 0.0)
        return d, jnp.sum(d, axis=0, keepdims=True)
    return _rowwise(fn, [ddt, dtr], [bias], [(LANES, BF16)], [((1, LANES), F32)], tile=512, name=name)


def _gnorm_fwd(y, z, w, n_groups, name):
    di = y.shape[1]
    gs = di // n_groups

    def fn(y, z, w):
        y2 = y * (z * jax.nn.sigmoid(z))
        out = []
        for g in range(n_groups):
            sl = y2[:, g * gs:(g + 1) * gs]
            r = lax.rsqrt(jnp.mean(sl * sl, axis=-1, keepdims=True) + GATED_NORM_EPS)
            out.append(sl * r)
        return (jnp.concatenate(out, axis=1) * w,)
    return _rowwise(fn, [y, z], [w], [(di, BF16)], tile=256, name=name)[0]


def _gnorm_bwd(dyn, y, z, w, n_groups, name):
    di = y.shape[1]
    gs = di // n_groups

    def fn(dyn, y, z, w):
        sig = jax.nn.sigmoid(z)
        sz = z * sig
        y2 = y * sz
        d2n = dyn * w
        dy2, yhat = [], []
        for g in range(n_groups):
            sl = y2[:, g * gs:(g + 1) * gs]
            dg = d2n[:, g * gs:(g + 1) * gs]
            r = lax.rsqrt(jnp.mean(sl * sl, axis=-1, keepdims=True) + GATED_NORM_EPS)
            yh = sl * r
            dy2.append(r * (dg - yh * jnp.mean(dg * yh, axis=-1, keepdims=True)))
            yhat.append(yh)
        dy2 = jnp.concatenate(dy2, axis=1)
        yhat = jnp.concatenate(yhat, axis=1)
        dz = dy2 * y * (sig * (1.0 + z * (1.0 - sig)))
        return dy2 * sz, dz, jnp.sum(dyn * yhat, axis=0, keepdims=True)
    return _rowwise(fn, [dyn, y, z], [w], [(di, F32), (di, BF16)], [((1, di), F32)], tile=128, name=name)


def _merge_fwd(os_, lses, name):
    n = len(os_)

    def fn(*v):
        o, l = v[:n], v[n:]
        m = functools.reduce(jnp.maximum, l)
        e = [jnp.exp(li - m) for li in l]
        tot = functools.reduce(jnp.add, e)
        acc = functools.reduce(jnp.add, [ei * oi for ei, oi in zip(e, o)]) / tot
        return acc, acc, m + jnp.log(tot)
    c = os_[0].shape[1]
    return _rowwise(fn, list(os_) + list(lses), [], [(c, F32), (c, BF16), (c, F32)], tile=256, name=name)


def _delta(do, o, name):
    c = o.shape[1]

    def fn(do, o):
        p = do * o
        out = [jnp.broadcast_to(jnp.sum(p[:, j:j + ATT_HEAD_DIM], axis=-1, keepdims=True), (p.shape[0], ATT_HEAD_DIM))
               for j in range(0, c, ATT_HEAD_DIM)]
        return (jnp.concatenate(out, axis=1),)
    return _rowwise(fn, [do, o], [], [(c, F32)], tile=256, name=name)[0]


def _lane_place(cols):
    rows = cols[0].shape[0]
    lane = lax.broadcasted_iota(jnp.int32, (rows, LANES), 1)
    out = jnp.zeros((rows, LANES), F32)
    for j, c in enumerate(cols):
        out = jnp.where(lane == j, c, out)
    return out


def _merge_heads(os_, lses, name):
    n = len(os_)
    n_kv, rep, hd = ATT_KV_HEADS_PER_GROUP, ATT_HEADS_PER_GROUP // ATT_KV_HEADS_PER_GROUP, ATT_HEAD_DIM

    def fn(*v):
        o, l = v[:n], v[n:]
        out, lse = [], []
        for h in range(n_kv):
            cols = []
            for j in range(rep):
                hh = h * rep + j
                lg = [li[h][:, j:j + 1] for li in l]
                m = functools.reduce(jnp.maximum, lg)
                e = [jnp.exp(x - m) for x in lg]
                tot = functools.reduce(jnp.add, e)
                acc = functools.reduce(jnp.add, [ei * oi[hh] for ei, oi in zip(e, o)])
                out.append(acc / tot)
                cols.append(m + jnp.log(tot))
            lse.append(_lane_place(cols))
        merged = jnp.concatenate(out, axis=1)
        return merged, merged, lse
    c = os_[0].shape[0] * hd
    return _rowwise(fn, list(os_) + list(lses), [], [(c, F32), (c, BF16), ((n_kv, LANES), F32)], tile=256, name=name)


def _delta_heads(do, o, name):
    n_kv, rep, hd = ATT_KV_HEADS_PER_GROUP, ATT_HEADS_PER_GROUP // ATT_KV_HEADS_PER_GROUP, ATT_HEAD_DIM

    def fn(do, o):
        p = do * o
        return ([_lane_place([jnp.sum(p[:, (h * rep + j) * hd:(h * rep + j + 1) * hd], axis=-1, keepdims=True)
                              for j in range(rep)]) for h in range(n_kv)],)
    return _rowwise(fn, [do, o], [], [((n_kv, LANES), F32)], tile=256, name=name)[0]


def _sum_slabs(recv, name):
    def body(r_ref, o_ref):
        acc = r_ref[0]
        for k in range(1, NDEV):
            acc = acc + r_ref[k]
        o_ref[...] = acc
    return pl.pallas_call(body, name=name, out_shape=jax.ShapeDtypeStruct(recv.shape[1:], F32),
                          compiler_params=_params())(recv)


def _shift_down(x, k):
    if k == 0:
        return x
    row = lax.broadcasted_iota(jnp.int32, x.shape, 0)
    return jnp.where(row >= k, pltpu.roll(x, k, 0), 0.0)


def _shift_up(x, k):
    if k == 0:
        return x
    s = x.shape[0]
    row = lax.broadcasted_iota(jnp.int32, x.shape, 0)
    return jnp.where(row < s - k, pltpu.roll(x, s - k, 0), 0.0)


def _conv(x, w):
    kw = w.shape[0]
    return functools.reduce(jnp.add, [w[k:k + 1, :] * _shift_down(x, kw - 1 - k) for k in range(kw)])


def _conv_t(dy, w):
    kw = w.shape[0]
    return functools.reduce(jnp.add, [w[k:k + 1, :] * _shift_up(dy, kw - 1 - k) for k in range(kw)])


def _conv_dw(x, dy, dw_ref):
    kw = dw_ref.shape[0]
    for k in range(kw):
        dw_ref[k:k + 1, :] = jnp.sum(dy * _shift_down(x, kw - 1 - k), axis=0, keepdims=True)


def _dsilu(pre):
    sig = jax.nn.sigmoid(pre)
    return sig * (1.0 + pre * (1.0 - sig))


def _col_specs(s, c, kw, tc):
    return (pl.BlockSpec((s, tc), lambda j: (0, j)), pl.BlockSpec((kw, tc), lambda j: (0, j)),
            pl.BlockSpec((1, tc), lambda j: (0, j)))


def _conv_silu_fwd(x, w, b, name):
    s, c = x.shape
    tc = LANES
    xs, ws, bs = _col_specs(s, c, w.shape[0], tc)

    def body(x_ref, w_ref, b_ref, o_ref):
        pre = _conv(x_ref[...], w_ref[...]) + b_ref[...]
        o_ref[...] = pre * jax.nn.sigmoid(pre)
    return pl.pallas_call(body, name=name, grid=(c // tc,), in_specs=[xs, ws, bs], out_specs=xs,
                          out_shape=jax.ShapeDtypeStruct((s, c), F32), compiler_params=_params(("parallel",)))(x, w, b)


def _conv_silu_bwd(x, w, b, dy, name):
    s, c = x.shape
    tc = LANES
    xs, ws, bs = _col_specs(s, c, w.shape[0], tc)

    def body(x_ref, w_ref, b_ref, dy_ref, dx_ref, dw_ref, db_ref):
        xv, wv = x_ref[...], w_ref[...]
        pre = _conv(xv, wv) + b_ref[...]
        dpre = dy_ref[...] * _dsilu(pre)
        dx_ref[...] = _conv_t(dpre, wv).astype(dx_ref.dtype)
        _conv_dw(xv, dpre, dw_ref)
        db_ref[...] = jnp.sum(dpre, axis=0, keepdims=True)
    return pl.pallas_call(
        body, name=name, grid=(c // tc,), in_specs=[xs, ws, bs, xs], out_specs=[xs, ws, bs],
        out_shape=[jax.ShapeDtypeStruct((s, c), BF16), jax.ShapeDtypeStruct(w.shape, F32),
                   jax.ShapeDtypeStruct((1, c), F32)],
        compiler_params=_params(("parallel",)))(x, w, b, dy)


def _ffn_gate_fwd(ug, uv, wg, wv, name):
    s, c = ug.shape
    tc = LANES
    xs, ws, _ = _col_specs(s, c, wg.shape[0], tc)

    def body(g_ref, v_ref, wg_ref, wv_ref, o_ref):
        g = _conv(g_ref[...], wg_ref[...])
        v = _conv(v_ref[...], wv_ref[...])
        o_ref[...] = (g * jax.nn.sigmoid(g) * v).astype(o_ref.dtype)
    return pl.pallas_call(body, name=name, grid=(c // tc,), in_specs=[xs, xs, ws, ws], out_specs=xs,
                          out_shape=jax.ShapeDtypeStruct((s, c), BF16),
                          compiler_params=_params(("parallel",)))(ug, uv, wg, wv)


def _ffn_gate_bwd(ug, uv, wg, wv, df, name):
    s, c = ug.shape
    tc = LANES
    xs, ws, _ = _col_specs(s, c, wg.shape[0], tc)

    def body(g_ref, v_ref, wg_ref, wv_ref, df_ref, dg_ref, dv_ref, dwg_ref, dwv_ref):
        gp, vp, wgv, wvv = g_ref[...], v_ref[...], wg_ref[...], wv_ref[...]
        g = _conv(gp, wgv)
        v = _conv(vp, wvv)
        dfv = df_ref[...]
        dg = dfv * v * _dsilu(g)
        dv = dfv * (g * jax.nn.sigmoid(g))
        dg_ref[...] = _conv_t(dg, wgv).astype(dg_ref.dtype)
        dv_ref[...] = _conv_t(dv, wvv).astype(dv_ref.dtype)
        _conv_dw(gp, dg, dwg_ref)
        _conv_dw(vp, dv, dwv_ref)
    return pl.pallas_call(
        body, name=name, grid=(c // tc,), in_specs=[xs, xs, ws, ws, xs], out_specs=[xs, xs, ws, ws],
        out_shape=[jax.ShapeDtypeStruct((s, c), BF16), jax.ShapeDtypeStruct((s, c), BF16),
                   jax.ShapeDtypeStruct(wg.shape, F32), jax.ShapeDtypeStruct(wv.shape, F32)],
        compiler_params=_params(("parallel",)))(ug, uv, wg, wv, df)


def _ssd_common(dt, alog, n_heads):
    ln = dt.shape[0]
    lane = lax.broadcasted_iota(jnp.int32, (1, LANES), 1)
    a = jnp.where(lane < n_heads, -jnp.exp(alog), 0.0)
    row = lax.broadcasted_iota(jnp.int32, (ln, ln), 0)
    col = lax.broadcasted_iota(jnp.int32, (ln, ln), 1)
    tril = col <= row
    acs = _dot(tril.astype(F32), dt * a, precision=HIGHEST)
    return a, acs, acs.T, tril


def _ssd_fwd(xbc, dt, alog, dskip, di, n_heads, n_groups, name):
    s, convd = xbc.shape
    ln, p, ns = SSM_CHUNK, SSM_HEAD_DIM, SSM_D_STATE
    nc, hg = s // ln, n_heads // n_groups

    def body(x_ref, dt_ref, alog_ref, d_ref, y_ref, prev_ref, st):
        @pl.when(pl.program_id(0) == 0)
        def _():
            st[...] = jnp.zeros(st.shape, F32)

        dt = dt_ref[...]
        _, acs, acs_t, tril = _ssd_common(dt, alog_ref[...], n_heads)
        e_all = jnp.exp(acs)
        last = acs[ln - 1:ln, :]
        ds_all = jnp.exp(last - acs)
        t_all = jnp.exp(last)
        dsk = d_ref[...]
        for g in range(n_groups):
            bg = x_ref[:, di + g * ns:di + (g + 1) * ns].astype(BF16)
            cg = x_ref[:, di + (n_groups + g) * ns:di + (n_groups + g + 1) * ns].astype(BF16)
            gm = _dot(cg, bg, _NT)
            for j in range(hg):
                h = g * hg + j
                xh = x_ref[:, h * p:(h + 1) * p]
                xdt = xh * dt[:, h:h + 1]
                seg = acs[:, h:h + 1] - acs_t[h:h + 1, :]
                m = jnp.where(tril, gm * jnp.exp(jnp.where(tril, seg, 0.0)), 0.0)
                prev = st[h]
                prev_ref[0, h] = prev
                y = _dot(m.astype(BF16), xdt.astype(BF16))
                y = y + _dot(cg, prev.astype(BF16), _NT) * e_all[:, h:h + 1]
                y = y + xh * dsk[:, h:h + 1]
                snew = _dot((xdt * ds_all[:, h:h + 1]).astype(BF16), bg, _TN)
                st[h] = prev * t_all[:, h:h + 1] + snew
                y_ref[:, h * p:(h + 1) * p] = y

    vec = pl.BlockSpec((1, LANES), lambda c: (0, 0))
    return pl.pallas_call(
        body, name=name, grid=(nc,),
        in_specs=[pl.BlockSpec((ln, convd), lambda c: (c, 0)), pl.BlockSpec((ln, LANES), lambda c: (c, 0)), vec, vec],
        out_specs=[pl.BlockSpec((ln, di), lambda c: (c, 0)),
                   pl.BlockSpec((1, n_heads, p, ns), lambda c: (c, 0, 0, 0))],
        out_shape=[jax.ShapeDtypeStruct((s, di), F32), jax.ShapeDtypeStruct((nc, n_heads, p, ns), F32)],
        scratch_shapes=[pltpu.VMEM((n_heads, p, ns), F32)],
        compiler_params=_params(("arbitrary",)))(xbc, dt, alog, dskip)


def _ssd_bwd(xbc, dt, alog, dskip, prev_all, dy, di, n_heads, n_groups, name):
    s, convd = xbc.shape
    ln, p, ns = SSM_CHUNK, SSM_HEAD_DIM, SSM_D_STATE
    nc, hg = s // ln, n_heads // n_groups

    def body(x_ref, dt_ref, alog_ref, d_ref, prev_ref, dy_ref, dx_ref, ddt_ref, da_ref, dd_ref, dh):
        step = pl.program_id(0)

        @pl.when(step == 0)
        def _():
            dh[...] = jnp.zeros(dh.shape, F32)
            da_ref[...] = jnp.zeros(da_ref.shape, F32)
            dd_ref[...] = jnp.zeros(dd_ref.shape, F32)

        dt = dt_ref[...]
        a, acs, acs_t, tril = _ssd_common(dt, alog_ref[...], n_heads)
        e_all = jnp.exp(acs)
        last = acs[ln - 1:ln, :]
        ds_all = jnp.exp(last - acs)
        t_all = jnp.exp(last)
        dsk = d_ref[...]
        lane = lax.broadcasted_iota(jnp.int32, (ln, LANES), 1)
        lane1 = lax.broadcasted_iota(jnp.int32, (1, LANES), 1)
        sub = lax.broadcasted_iota(jnp.int32, (LANES, ln), 0)
        rowi = lax.broadcasted_iota(jnp.int32, (ln, LANES), 0)
        dacs_c = jnp.zeros((ln, LANES), F32)
        dacs_r = jnp.zeros((LANES, ln), F32)
        dlast = jnp.zeros((1, LANES), F32)
        ddt_x = jnp.zeros((ln, LANES), F32)
        dd = jnp.zeros((1, LANES), F32)

        def tot(v):
            return jnp.sum(jnp.sum(v, axis=1, keepdims=True), axis=0, keepdims=True)

        for g in range(n_groups):
            bg = x_ref[:, di + g * ns:di + (g + 1) * ns].astype(BF16)
            cg = x_ref[:, di + (n_groups + g) * ns:di + (n_groups + g + 1) * ns].astype(BF16)
            gm = _dot(cg, bg, _NT)
            dgm = jnp.zeros((ln, ln), F32)
            dcg = jnp.zeros((ln, ns), F32)
            dbg = jnp.zeros((ln, ns), F32)
            for j in range(hg):
                h = g * hg + j
                xh = x_ref[:, h * p:(h + 1) * p]
                dth = dt[:, h:h + 1]
                xdt = xh * dth
                dyh = dy_ref[:, h * p:(h + 1) * p]
                eh, dsh, th = e_all[:, h:h + 1], ds_all[:, h:h + 1], t_all[:, h:h + 1]
                seg = acs[:, h:h + 1] - acs_t[h:h + 1, :]
                dec = jnp.where(tril, jnp.exp(jnp.where(tril, seg, 0.0)), 0.0)
                m = gm * dec
                prev = prev_ref[0, h]
                dhn = dh[h]
                prevb, dhb, dyb, xdtb = prev.astype(BF16), dhn.astype(BF16), dyh.astype(BF16), xdt.astype(BF16)
                yo = _dot(cg, prevb, _NT)
                dyob = (dyh * eh).astype(BF16)
                c_col = jnp.sum(dyh * yo, axis=1, keepdims=True) * eh
                dcg = dcg + _dot(dyob, prevb)
                dprev = th * dhn + _dot(dyob, cg, _TN)
                dtt = tot(dhn * prev)
                w = _dot(bg, dhb, _NT)
                dxdt = w * dsh
                dds = jnp.sum(w * xdt, axis=1, keepdims=True)
                dbg = dbg + _dot((xdt * dsh).astype(BF16), dhb)
                dm = _dot(dyb, xdtb, _NT)
                dxdt = dxdt + _dot(m.astype(BF16), dyb, _TN)
                dgm = dgm + dm * dec
                q = dm * m
                c_col = c_col + jnp.sum(q, axis=1, keepdims=True) - dds * dsh
                r_row = -jnp.sum(q, axis=0, keepdims=True)
                dlast_h = tot(dds * dsh) + dtt * th
                dacs_c = dacs_c + jnp.where(lane == h, c_col, 0.0)
                dacs_r = dacs_r + jnp.where(sub == h, r_row, 0.0)
                dlast = dlast + jnp.where(lane1 == h, dlast_h, 0.0)
                ddt_x = ddt_x + jnp.where(lane == h, jnp.sum(dxdt * xh, axis=1, keepdims=True), 0.0)
                dd = dd + jnp.where(lane1 == h, tot(dyh * xh), 0.0)
                dx_ref[:, h * p:(h + 1) * p] = dxdt * dth + dyh * dsk[:, h:h + 1]
                dh[h] = dprev
            dgb = dgm.astype(BF16)
            dx_ref[:, di + g * ns:di + (g + 1) * ns] = dbg + _dot(dgb, cg, _TN)
            dx_ref[:, di + (n_groups + g) * ns:di + (n_groups + g + 1) * ns] = dcg + _dot(dgb, bg)

        dacs = dacs_c + dacs_r.T + jnp.where(rowi == ln - 1, dlast, 0.0)
        row = lax.broadcasted_iota(jnp.int32, (ln, ln), 0)
        col = lax.broadcasted_iota(jnp.int32, (ln, ln), 1)
        dadt = _dot((col >= row).astype(F32), dacs, precision=HIGHEST)
        ddt_ref[...] = dadt * a + ddt_x
        da_ref[...] += jnp.sum(dadt * dt, axis=0, keepdims=True)
        dd_ref[...] += dd

        @pl.when(step == nc - 1)
        def _():
            da_ref[...] = da_ref[...] * a

    vec = pl.BlockSpec((1, LANES), lambda c: (0, 0))
    rev = lambda c: (nc - 1 - c, 0)
    return pl.pallas_call(
        body, name=name, grid=(nc,),
        in_specs=[pl.BlockSpec((ln, convd), rev), pl.BlockSpec((ln, LANES), rev), vec, vec,
                  pl.BlockSpec((1, n_heads, p, ns), lambda c: (nc - 1 - c, 0, 0, 0)), pl.BlockSpec((ln, di), rev)],
        out_specs=[pl.BlockSpec((ln, convd), rev), pl.BlockSpec((ln, LANES), rev), vec, vec],
        out_shape=[jax.ShapeDtypeStruct((s, convd), F32), jax.ShapeDtypeStruct((s, LANES), F32),
                   jax.ShapeDtypeStruct((1, LANES), F32), jax.ShapeDtypeStruct((1, LANES), F32)],
        scratch_shapes=[pltpu.VMEM((n_heads, p, ns), F32)],
        compiler_params=_params(("arbitrary",)))(xbc, dt, alog, dskip, prev_all, dy)


def _split(x, n):
    out = []
    for _ in range(n):
        piece = x.astype(BF16)
        out.append(piece)
        x = x - piece.astype(F32)
    return out


def _spread(x, onehot, n=2):
    return functools.reduce(jnp.add, [_dot(piece, onehot) for piece in _split(x, n)])


def _head_maps(di, p):
    e = (jnp.arange(di, dtype=jnp.int32)[None, :] // p == jnp.arange(LANES, dtype=jnp.int32)[:, None]).astype(BF16)
    return e, e.T


def _ssd_wide(dt, acs, acs_t, dskip, e_ref, et_ref):
    ln = dt.shape[0]
    last = acs[ln - 1:ln, :]
    stack = jnp.concatenate([dt, jnp.exp(acs), jnp.exp(last - acs), jnp.broadcast_to(dskip, (8, LANES))], axis=0)
    wide = _spread(stack, e_ref[...])
    tb = jnp.exp(jnp.broadcast_to(acs_t[:, ln - 1:ln], (LANES, LANES)))
    texp = functools.reduce(jnp.add, [_dot(et_ref[...], piece) for piece in _split(tb, 3)])
    return wide[:ln], wide[ln:2 * ln], wide[2 * ln:3 * ln], wide[3 * ln:3 * ln + 1], texp


def _ssd_fwd2(xbc, dt, alog, dskip, di, n_heads, n_groups, name):
    s, convd = xbc.shape
    ln, p, ns = SSM_CHUNK, SSM_HEAD_DIM, SSM_D_STATE
    nc, hg = s // ln, n_heads // n_groups
    gw = hg * p
    e64, e64t = _head_maps(di, p)

    def body(x_ref, dt_ref, alog_ref, d_ref, e_ref, et_ref, y_ref, prev_ref, st):
        @pl.when(pl.program_id(0) == 0)
        def _():
            st[...] = jnp.zeros(st.shape, F32)

        dt = dt_ref[...]
        _, acs, acs_t, tril = _ssd_common(dt, alog_ref[...], n_heads)
        dte, ee, dse, dske, texp = _ssd_wide(dt, acs, acs_t, d_ref[...], e_ref, et_ref)
        x = x_ref[:, :di]
        xdt = x * dte
        xdtb = xdt.astype(BF16)
        xdsb = (xdt * dse).astype(BF16)
        for g in range(n_groups):
            rows = slice(g * gw, (g + 1) * gw)
            bg = x_ref[:, di + g * ns:di + (g + 1) * ns].astype(BF16)
            cg = x_ref[:, di + (n_groups + g) * ns:di + (n_groups + g + 1) * ns].astype(BF16)
            gm = _dot(cg, bg, _NT)
            prev = st[rows, :]
            prev_ref[0, rows, :] = prev
            yo = _dot(cg, prev.astype(BF16), _NT)
            for j in range(hg):
                h = g * hg + j
                seg = acs[:, h:h + 1] - acs_t[h:h + 1, :]
                m = jnp.where(tril, gm * jnp.exp(jnp.where(tril, seg, 0.0)), 0.0)
                y_ref[:, h * p:(h + 1) * p] = _dot(m.astype(BF16), xdtb[:, h * p:(h + 1) * p])
            y_ref[:, rows] = y_ref[:, rows] + yo * ee[:, rows] + x[:, rows] * dske[:, rows]
            st[rows, :] = prev * texp[rows, :] + _dot(xdsb[:, rows], bg, _TN)

    vec = pl.BlockSpec((1, LANES), lambda c: (0, 0))
    return pl.pallas_call(
        body, name=name, grid=(nc,),
        in_specs=[pl.BlockSpec((ln, convd), lambda c: (c, 0)), pl.BlockSpec((ln, LANES), lambda c: (c, 0)), vec, vec,
                  pl.BlockSpec(e64.shape, lambda c: (0, 0)), pl.BlockSpec(e64t.shape, lambda c: (0, 0))],
        out_specs=[pl.BlockSpec((ln, di), lambda c: (c, 0)), pl.BlockSpec((1, di, ns), lambda c: (c, 0, 0))],
        out_shape=[jax.ShapeDtypeStruct((s, di), F32), jax.ShapeDtypeStruct((nc, di, ns), F32)],
        scratch_shapes=[pltpu.VMEM((di, ns), F32)],
        compiler_params=_params(("arbitrary",)))(xbc, dt, alog, dskip, e64, e64t)


def _ssd_bwd2(xbc, dt, alog, dskip, prev_all, dy, di, n_heads, n_groups, name):
    s, convd = xbc.shape
    ln, p, ns = SSM_CHUNK, SSM_HEAD_DIM, SSM_D_STATE
    nc, hg = s // ln, n_heads // n_groups
    gw = hg * p
    e64, e64t = _head_maps(di, p)

    def body(x_ref, dt_ref, alog_ref, d_ref, e_ref, et_ref, prev_ref, dy_ref,
             dx_ref, ddt_ref, da_ref, dd_ref, dh, yo_ref, w_ref):
        step = pl.program_id(0)

        @pl.when(step == 0)
        def _():
            dh[...] = jnp.zeros(dh.shape, F32)
            da_ref[...] = jnp.zeros(da_ref.shape, F32)
            dd_ref[...] = jnp.zeros(dd_ref.shape, F32)

        dt = dt_ref[...]
        a, acs, acs_t, tril = _ssd_common(dt, alog_ref[...], n_heads)
        dte, ee, dse, dske, texp = _ssd_wide(dt, acs, acs_t, d_ref[...], e_ref, et_ref)
        row = lax.broadcasted_iota(jnp.int32, (ln, ln), 0)
        col = lax.broadcasted_iota(jnp.int32, (ln, ln), 1)
        triu = col >= row
        x = x_ref[:, :di]
        dy = dy_ref[...]
        xdt = x * dte
        xdtb = xdt.astype(BF16)
        xdsb = (xdt * dse).astype(BF16)
        dyb = dy.astype(BF16)
        dyob = (dy * ee).astype(BF16)
        dhn = dh[...]
        dhb = dhn.astype(BF16)
        per_head = functools.reduce(jnp.add, [_dot(e_ref[...], piece) for piece in _split(dhn * prev_ref[0], 2)])
        ones8 = jnp.ones((8, LANES), BF16)
        dtt = functools.reduce(jnp.add, [_dot(ones8, piece, _NT) for piece in _split(per_head, 2)])[0:1]
        dacs_c = jnp.zeros((ln, LANES), F32)
        dacs_r = jnp.zeros((LANES, ln), F32)
        for g in range(n_groups):
            rows = slice(g * gw, (g + 1) * gw)
            bg = x_ref[:, di + g * ns:di + (g + 1) * ns].astype(BF16)
            cg = x_ref[:, di + (n_groups + g) * ns:di + (n_groups + g + 1) * ns].astype(BF16)
            gmt = _dot(bg, cg, _NT)
            prevb = prev_ref[0, rows, :].astype(BF16)
            dcg = _dot(dyob[:, rows], prevb)
            dh[rows, :] = texp[rows, :] * dhn[rows, :] + _dot(dyob[:, rows], cg, _TN)
            w = _dot(bg, dhb[rows, :], _NT)
            dbg = _dot(xdsb[:, rows], dhb[rows, :])
            yo_ref[:, rows] = _dot(cg, prevb, _NT)
            w_ref[:, rows] = w
            dgmt = jnp.zeros((ln, ln), F32)
            q_hi, q_lo = [], []
            for j in range(hg):
                h = g * hg + j
                segt = acs_t[h:h + 1, :] - acs[:, h:h + 1]
                dect = jnp.where(triu, jnp.exp(jnp.where(triu, segt, 0.0)), 0.0)
                dyh, xh = dyb[:, h * p:(h + 1) * p], xdtb[:, h * p:(h + 1) * p]
                mt = gmt * dect
                dmt = _dot(xh, dyh, _NT)
                dx_ref[:, h * p:(h + 1) * p] = _dot(mt.astype(BF16), dyh)
                dgmt = dgmt + dmt * dect
                hi, lo = _split(dmt * mt, 2)
                q_hi.append(hi)
                q_lo.append(lo)
            sel_c = (lax.broadcasted_iota(jnp.int32, (hg * ln, LANES), 1)
                     == g * hg + lax.broadcasted_iota(jnp.int32, (hg * ln, LANES), 0) // ln).astype(BF16)
            sel_r = (lax.broadcasted_iota(jnp.int32, (LANES, hg * ln), 0)
                     == g * hg + lax.broadcasted_iota(jnp.int32, (LANES, hg * ln), 1) // ln).astype(BF16)
            for pieces in (q_hi, q_lo):
                dacs_c = dacs_c - _dot(jnp.concatenate(pieces, axis=1), sel_c)
                dacs_r = dacs_r + _dot(sel_r, jnp.concatenate(pieces, axis=0))
            dgb = dgmt.astype(BF16)
            dx_ref[:, di + g * ns:di + (g + 1) * ns] = dbg + _dot(dgb, cg)
            dx_ref[:, di + (n_groups + g) * ns:di + (n_groups + g + 1) * ns] = dcg + _dot(dgb, bg, _TN)

        wds = w_ref[...] * dse
        dxdt = dx_ref[:, :di] + wds
        red = _spread(jnp.concatenate([dxdt * x, dy * yo_ref[...] * ee, xdt * wds, dy * x], axis=0), et_ref[...])
        ddt_x, r_off, r_state, ddr = red[:ln], red[ln:2 * ln], red[2 * ln:3 * ln], red[3 * ln:]
        dx_ref[:, :di] = dxdt * dte + dy * dske
        rowi = lax.broadcasted_iota(jnp.int32, (ln, LANES), 0)
        dlast = jnp.sum(r_state, axis=0, keepdims=True) + dtt * jnp.exp(acs[ln - 1:ln, :])
        dacs = r_off - r_state + dacs_c + dacs_r.T + jnp.where(rowi == ln - 1, dlast, 0.0)
        dadt = _dot(triu.astype(F32), dacs, precision=HIGHEST)
        ddt_ref[...] = dadt * a + ddt_x
        da_ref[...] += jnp.sum(dadt * dt, axis=0, keepdims=True)
        dd_ref[...] += jnp.sum(ddr, axis=0, keepdims=True)

        @pl.when(step == nc - 1)
        def _():
            da_ref[...] = da_ref[...] * a

    vec = pl.BlockSpec((1, LANES), lambda c: (0, 0))
    rev = lambda c: (nc - 1 - c, 0)
    return pl.pallas_call(
        body, name=name, grid=(nc,),
        in_specs=[pl.BlockSpec((ln, convd), rev), pl.BlockSpec((ln, LANES), rev), vec, vec,
                  pl.BlockSpec(e64.shape, lambda c: (0, 0)), pl.BlockSpec(e64t.shape, lambda c: (0, 0)),
                  pl.BlockSpec((1, di, ns), lambda c: (nc - 1 - c, 0, 0)), pl.BlockSpec((ln, di), rev)],
        out_specs=[pl.BlockSpec((ln, convd), rev), pl.BlockSpec((ln, LANES), rev), vec, vec],
        out_shape=[jax.ShapeDtypeStruct((s, convd), F32), jax.ShapeDtypeStruct((s, LANES), F32),
                   jax.ShapeDtypeStruct((1, LANES), F32), jax.ShapeDtypeStruct((1, LANES), F32)],
        scratch_shapes=[pltpu.VMEM((di, ns), F32), pltpu.VMEM((ln, di), F32), pltpu.VMEM((ln, di), F32)],
        compiler_params=_params(("arbitrary",)))(xbc, dt, alog, dskip, e64, e64t, prev_all, dy)


def _perm(a, d):
    if d == 1:
        return a
    s = a.shape[0]
    return a.reshape(s // d, d, -1).transpose(1, 0, 2).reshape(s, -1)


def _unperm(a, d):
    if d == 1:
        return a
    s = a.shape[0]
    return a.reshape(d, s // d, -1).transpose(1, 0, 2).reshape(s, -1)


def _rot_tables(s, d):
    half = ROPE_DIM // 2
    inv_freq = jnp.power(jnp.float32(ROPE_THETA), -jnp.arange(0, ROPE_DIM, 2, dtype=F32) / ROPE_DIM)
    v = jnp.arange(s, dtype=jnp.int32)
    pos = (v % (s // d)) * d + v // (s // d)
    ang = pos.astype(F32)[:, None] * inv_freq[None, :]
    cos, sin = jnp.cos(ang), jnp.sin(ang)
    zero = jnp.zeros((s, ATT_HEAD_DIM - ROPE_DIM), F32)
    cf = jnp.concatenate([cos, cos, jnp.ones_like(zero)], axis=1)
    s1 = jnp.concatenate([-sin, jnp.zeros_like(sin), zero], axis=1)
    s2 = jnp.concatenate([jnp.zeros_like(sin), sin, zero], axis=1)
    assert half * 2 == ROPE_DIM
    return cf, s1, s2


def _rot(x, tabs, sign):
    cf, s1, s2 = tabs
    half = ROPE_DIM // 2
    left = pltpu.roll(x, ATT_HEAD_DIM - half, 1)
    right = pltpu.roll(x, half, 1)
    return x * cf + sign * (left * s1 + right * s2)


def _att_masks(n, n_blk, rep):
    b = ATT_BLOCK
    row = lax.broadcasted_iota(jnp.int32, (rep * b, b), 0) & (b - 1)
    col = lax.broadcasted_iota(jnp.int32, (rep * b, b), 1)
    off = jnp.where(n % n_blk != 0, 0, 2 * b)
    return col <= row, col >= row + off


def _stack(x, rep):
    return jnp.concatenate([x[:, j * ATT_HEAD_DIM:(j + 1) * ATT_HEAD_DIM] for j in range(rep)], axis=0)


def _att_specs(nb, rep, cur, prv):
    b, hd = ATT_BLOCK, ATT_HEAD_DIM
    q_spec = pl.BlockSpec((b, rep * hd), lambda h, n: (cur(n), h))
    kc_spec = pl.BlockSpec((b, hd), lambda h, n: (cur(n), h))
    kp_spec = pl.BlockSpec((b, hd), lambda h, n: (prv(n), h))
    tc_spec = pl.BlockSpec((b, hd), lambda h, n: (cur(n), 0))
    tp_spec = pl.BlockSpec((b, hd), lambda h, n: (prv(n), 0))
    return q_spec, kc_spec, kp_spec, tc_spec, tp_spec


def _attn_fwd(q, k, v, tabs, n_blk, name):
    s = q.shape[0]
    b, hd = ATT_BLOCK, ATT_HEAD_DIM
    nb = s // b
    n_kv = ATT_KV_HEADS_PER_GROUP
    rep = ATT_HEADS_PER_GROUP // n_kv
    scale = hd ** -0.5

    def body(q_ref, kc_ref, kp_ref, vc_ref, vp_ref, cfc, s1c, s2c, cfp, s1p, s2p, o_ref, lse_ref):
        n = pl.program_id(1)
        tc = (cfc[...], s1c[...], s2c[...])
        tp = (cfp[...], s1p[...], s2p[...])
        qv = q_ref[...]
        q4 = jnp.concatenate([_rot(qv[:, j * hd:(j + 1) * hd], tc, 1.0) for j in range(rep)], axis=0).astype(BF16)
        kc = _rot(kc_ref[...], tc, 1.0).astype(BF16)
        kp = _rot(kp_ref[...], tp, 1.0).astype(BF16)
        mc, mp = _att_masks(n, n_blk, rep)
        sc = jnp.where(mc, _dot(q4, kc, _NT) * scale, NEG)
        sp = jnp.where(mp, _dot(q4, kp, _NT) * scale, NEG)
        m = jnp.maximum(jnp.max(sc, axis=1, keepdims=True), jnp.max(sp, axis=1, keepdims=True))
        pc, pp = jnp.exp(sc - m), jnp.exp(sp - m)
        l = jnp.sum(pc, axis=1, keepdims=True) + jnp.sum(pp, axis=1, keepdims=True)
        o = (_dot(pc.astype(BF16), vc_ref[...].astype(BF16)) + _dot(pp.astype(BF16), vp_ref[...].astype(BF16))) / l
        lse = jnp.broadcast_to(m + jnp.log(l), (rep * b, hd))
        for j in range(rep):
            o_ref[:, j * hd:(j + 1) * hd] = o[j * b:(j + 1) * b]
            lse_ref[:, j * hd:(j + 1) * hd] = lse[j * b:(j + 1) * b]

    cur = lambda n: n
    prv = lambda n: jnp.maximum(n - 1, 0)
    q_spec, kc_spec, kp_spec, tc_spec, tp_spec = _att_specs(nb, rep, cur, prv)
    return pl.pallas_call(
        body, name=name, grid=(n_kv, nb),
        in_specs=[q_spec, kc_spec, kp_spec, kc_spec, kp_spec, tc_spec, tc_spec, tc_spec, tp_spec, tp_spec, tp_spec],
        out_specs=[q_spec, q_spec],
        out_shape=[jax.ShapeDtypeStruct(q.shape, F32), jax.ShapeDtypeStruct(q.shape, F32)],
        compiler_params=_params(("parallel", "arbitrary")))(q, k, k, v, v, *tabs, *tabs)


def _attn_bwd(q, k, v, do, lse, delta, tabs, n_blk, name):
    s = q.shape[0]
    b, hd = ATT_BLOCK, ATT_HEAD_DIM
    nb = s // b
    n_kv = ATT_KV_HEADS_PER_GROUP
    rep = ATT_HEADS_PER_GROUP // n_kv
    scale = hd ** -0.5

    def body(q_ref, do_ref, lse_ref, dl_ref, kc_ref, kp_ref, vc_ref, vp_ref, cfc, s1c, s2c, cfp, s1p, s2p,
             dq_ref, dk_ref, dv_ref, ck, cv):
        n = pl.program_id(1)
        tp = (cfp[...], s1p[...], s2p[...])

        @pl.when(n == 0)
        def _():
            ck[...] = jnp.zeros(ck.shape, F32)
            cv[...] = jnp.zeros(cv.shape, F32)

        @pl.when(n < nb)
        def _():
            tc = (cfc[...], s1c[...], s2c[...])
            qv = q_ref[...]
            q4 = jnp.concatenate([_rot(qv[:, j * hd:(j + 1) * hd], tc, 1.0) for j in range(rep)],
                                 axis=0).astype(BF16)
            do4 = _stack(do_ref[...], rep).astype(BF16)
            lse4 = _stack(lse_ref[...], rep)
            dl4 = _stack(dl_ref[...], rep)
            kc = _rot(kc_ref[...], tc, 1.0).astype(BF16)
            kp = _rot(kp_ref[...], tp, 1.0).astype(BF16)
            vc, vp = vc_ref[...].astype(BF16), vp_ref[...].astype(BF16)
            mc, mp = _att_masks(n, n_blk, rep)
            pc = jnp.where(mc, jnp.exp(jnp.where(mc, _dot(q4, kc, _NT) * scale - lse4, 0.0)), 0.0)
            pp = jnp.where(mp, jnp.exp(jnp.where(mp, _dot(q4, kp, _NT) * scale - lse4, 0.0)), 0.0)
            dsc = (pc * (_dot(do4, vc, _NT) - dl4)).astype(BF16)
            dsp = (pp * (_dot(do4, vp, _NT) - dl4)).astype(BF16)
            dq4 = (_dot(dsc, kc) + _dot(dsp, kp)) * scale
            for j in range(rep):
                dq_ref[:, j * hd:(j + 1) * hd] = _rot(dq4[j * b:(j + 1) * b], tc, -1.0).astype(dq_ref.dtype)
            dk_prev = ck[...] + _dot(dsp, q4, _TN) * scale
            dv_prev = cv[...] + _dot(pp.astype(BF16), do4, _TN)
            dk_ref[...] = _rot(dk_prev, tp, -1.0).astype(dk_ref.dtype)
            dv_ref[...] = dv_prev.astype(dv_ref.dtype)
            ck[...] = _dot(dsc, q4, _TN) * scale
            cv[...] = _dot(pc.astype(BF16), do4, _TN)

        @pl.when(n == nb)
        def _():
            dk_ref[...] = _rot(ck[...], tp, -1.0).astype(dk_ref.dtype)
            dv_ref[...] = cv[...].astype(dv_ref.dtype)

    cur = lambda n: jnp.minimum(n, nb - 1)
    prv = lambda n: jnp.maximum(n - 1, 0)
    q_spec, kc_spec, kp_spec, tc_spec, tp_spec = _att_specs(nb, rep, cur, prv)
    return pl.pallas_call(
        body, name=name, grid=(n_kv, nb + 1),
        in_specs=[q_spec, q_spec, q_spec, q_spec, kc_spec, kp_spec, kc_spec, kp_spec,
                  tc_spec, tc_spec, tc_spec, tp_spec, tp_spec, tp_spec],
        out_specs=[q_spec, kp_spec, kp_spec],
        out_shape=[jax.ShapeDtypeStruct(q.shape, BF16), jax.ShapeDtypeStruct(k.shape, BF16),
                   jax.ShapeDtypeStruct(k.shape, BF16)],
        scratch_shapes=[pltpu.VMEM((b, hd), F32), pltpu.VMEM((b, hd), F32)],
        compiler_params=_params(("parallel", "arbitrary")))(q, do, lse, delta, k, k, v, v, *tabs, *tabs)


def _rot_heads(x, tabs, width, sign, out_dtype, name):
    s = x.shape[0]
    hd = ATT_HEAD_DIM
    tile = min(512, s)

    def body(x_ref, cf, s1, s2, o_ref):
        t = (cf[...], s1[...], s2[...])
        for j in range(width // hd):
            o_ref[:, j * hd:(j + 1) * hd] = _rot(x_ref[:, j * hd:(j + 1) * hd], t, sign).astype(o_ref.dtype)

    tab = pl.BlockSpec((tile, hd), lambda i: (i, 0))
    return pl.pallas_call(
        body, name=name, grid=(s // tile,), in_specs=[pl.BlockSpec((tile, width), lambda i: (i, 0)), tab, tab, tab],
        out_specs=pl.BlockSpec((tile, width), lambda i: (i, 0)), out_shape=jax.ShapeDtypeStruct((s, width), out_dtype),
        compiler_params=_params(("parallel",)))(x, *tabs)


def _rows_of(r, dil):
    return pl.ds(r, ATT_BLOCK, stride=dil) if dil > 1 else slice(None)


def _nat_specs(g, dil, n_kv_all, cur, prv):
    b, hd = ATT_BLOCK * dil, ATT_HEAD_DIM
    n_kv = ATT_KV_HEADS_PER_GROUP
    rep = ATT_HEADS_PER_GROUP // n_kv
    q_all = [pl.BlockSpec((b, hd), lambda h, n, j=j: (cur(n), (g * n_kv + h) * rep + j)) for j in range(rep)]
    q_own = [pl.BlockSpec((b, hd), lambda h, n, j=j: (cur(n), h * rep + j)) for j in range(rep)]
    hm_all = pl.BlockSpec((rep, b, hd), lambda h, n: (g * n_kv + h, cur(n), 0))
    hm_own = pl.BlockSpec((rep, b, hd), lambda h, n: (h, cur(n), 0))
    kc =pl.BlockSpec((b, hd), lambda h, n: (cur(n), g * n_kv + h))
    kp = pl.BlockSpec((b, hd), lambda h, n: (prv(n), g * n_kv + h))
    vc = pl.BlockSpec((b, hd), lambda h, n: (cur(n), n_kv_all + g * n_kv + h))
    vp = pl.BlockSpec((b, hd), lambda h, n: (prv(n), n_kv_all + g * n_kv + h))
    tab = pl.BlockSpec((b, hd), lambda h, n: (cur(n), 0))
    stat = pl.BlockSpec((None, b, LANES), lambda h, n: (h, cur(n), 0))
    return q_all, q_own, hm_all, hm_own, kc, kp, vc, vp, tab, stat


def _head_cols(stat, rep):
    return jnp.concatenate([jnp.broadcast_to(stat[:, j:j + 1], stat.shape) for j in range(rep)], axis=0)


def _attn_fwd_nat(q_all, k_rot, kv, tabs, g, dil, name):
    s = q_all.shape[0]
    b, hd = ATT_BLOCK, ATT_HEAD_DIM
    nbn = s // (b * dil)
    n_kv = ATT_KV_HEADS_PER_GROUP
    rep = ATT_HEADS_PER_GROUP // n_kv
    n_kv_all = k_rot.shape[1] // hd
    scale = hd ** -0.5

    def body(*refs):
        q_refs = refs[:rep]
        kc_ref, kp_ref, vc_ref, vp_ref, cf, s1, s2, o_ref, lse_ref = refs[rep:]
        mc, mp = _att_masks(jnp.where(pl.program_id(1) > 0, 1, 0), 2, rep)
        for r in range(dil):
            sl = _rows_of(r, dil)
            tc = (cf[sl, :], s1[sl, :], s2[sl, :])
            q4 = jnp.concatenate([_rot(q_ref[sl, :], tc, 1.0) for q_ref in q_refs], axis=0).astype(BF16)
            kc, kp = kc_ref[sl, :].astype(BF16), kp_ref[sl, :].astype(BF16)
            sc = jnp.where(mc, _dot(q4, kc, _NT) * scale, NEG)
            sp = jnp.where(mp, _dot(q4, kp, _NT) * scale, NEG)
            m = jnp.maximum(jnp.max(sc, axis=1, keepdims=True), jnp.max(sp, axis=1, keepdims=True))
            pc, pp = jnp.exp(sc - m), jnp.exp(sp - m)
            l = jnp.sum(pc, axis=1, keepdims=True) + jnp.sum(pp, axis=1, keepdims=True)
            o = (_dot(pc.astype(BF16), vc_ref[sl, :].astype(BF16))
                 + _dot(pp.astype(BF16), vp_ref[sl, :].astype(BF16))) / l
            lse = m + jnp.log(l)
            for j in range(rep):
                o_ref[j, sl, :] = o[j * b:(j + 1) * b]
            lse_ref[sl, :] = _lane_place([lse[j * b:(j + 1) * b] for j in range(rep)])

    cur = lambda n: n
    prv = lambda n: jnp.maximum(n - 1, 0)
    q_specs, _, _, hm_own, kc, kp, vc, vp, tab, stat = _nat_specs(g, dil, n_kv_all, cur, prv)
    return pl.pallas_call(
        body, name=name, grid=(n_kv, nbn), in_specs=[*q_specs, kc, kp, vc, vp, tab, tab, tab], out_specs=[hm_own, stat],
        out_shape=[jax.ShapeDtypeStruct((ATT_HEADS_PER_GROUP, s, hd), F32), jax.ShapeDtypeStruct((n_kv, s, LANES), F32)],
        compiler_params=_params(("parallel", "arbitrary")))(*([q_all] * rep), k_rot, k_rot, kv, kv, *tabs)


def _attn_bwd_nat(q_all, k_rot, kv, do, lse, delta, tabs, grads, g, dil, name):
    s = q_all.shape[0]
    b, hd = ATT_BLOCK, ATT_HEAD_DIM
    nbn = s // (b * dil)
    n_kv = ATT_KV_HEADS_PER_GROUP
    rep = ATT_HEADS_PER_GROUP // n_kv
    n_kv_all = k_rot.shape[1] // hd
    scale = hd ** -0.5

    def body(*refs):
        q_refs, do_refs = refs[:rep], refs[rep:2 * rep]
        (lse_ref, dl_ref, kc_ref, kp_ref, vc_ref, vp_ref, cf, s1, s2, _, _, _,
         dq_ref, dk_ref, dv_ref, ck, cv) = refs[2 * rep:]
        n = pl.program_id(1)

        @pl.when(n == 0)
        def _():
            ck[...] = jnp.zeros(ck.shape, F32)
            cv[...] = jnp.zeros(cv.shape, F32)

        @pl.when(n < nbn)
        def _():
            mc, mp = _att_masks(jnp.where(n > 0, 1, 0), 2, rep)
            for r in range(dil):
                sl = _rows_of(r, dil)
                own = slice(r * b, (r + 1) * b)
                tc = (cf[sl, :], s1[sl, :], s2[sl, :])
                q4 = jnp.concatenate([_rot(q_ref[sl, :], tc, 1.0) for q_ref in q_refs], axis=0).astype(BF16)
                do4 = jnp.concatenate([do_ref[sl, :] for do_ref in do_refs], axis=0).astype(BF16)
                lse4 = _head_cols(lse_ref[sl, :], rep)
                dl4 = _head_cols(dl_ref[sl, :], rep)
                kc, kp = kc_ref[sl, :].astype(BF16), kp_ref[sl, :].astype(BF16)
                vc, vp = vc_ref[sl, :].astype(BF16), vp_ref[sl, :].astype(BF16)
                pc = jnp.where(mc, jnp.exp(_dot(q4, kc, _NT) * scale - lse4), 0.0)
                pp = jnp.where(mp, jnp.exp(_dot(q4, kp, _NT) * scale - lse4), 0.0)
                dsc = (pc * (_dot(do4, vc, _NT) - dl4)).astype(BF16)
                dsp = (pp * (_dot(do4, vp, _NT) - dl4)).astype(BF16)
                dq4 = (_dot(dsc, kc) + _dot(dsp, kp)) * scale
                for j in range(rep):
                    dq_ref[j, sl, :] = _rot(dq4[j * b:(j + 1) * b], tc, -1.0)
                dk_ref[sl, :] = ck[own, :] + _dot(dsp, q4, _TN) * scale
                dv_ref[sl, :] = cv[own, :] + _dot(pp.astype(BF16), do4, _TN)
                ck[own, :] = _dot(dsc, q4, _TN) * scale
                cv[own, :] = _dot(pc.astype(BF16), do4, _TN)

        @pl.when(n == nbn)
        def _():
            for r in range(dil):
                sl = _rows_of(r, dil)
                dk_ref[sl, :] = ck[r * b:(r + 1) * b, :]
                dv_ref[sl, :] = cv[r * b:(r + 1) * b, :]

    cur = lambda n: jnp.minimum(n, nbn - 1)
    prv = lambda n: jnp.maximum(n - 1, 0)
    q_specs, do_specs, hm_all, _, kc, kp, vc, vp, tab, stat = _nat_specs(g, dil, n_kv_all, cur, prv)
    anyspace = pl.BlockSpec(memory_space=pl.ANY)
    n_in = 2 * rep + 9
    return pl.pallas_call(
        body, name=name, grid=(n_kv, nbn + 1),
        in_specs=[*q_specs, *do_specs, stat, stat, kc, kp, vc, vp, tab, tab, tab, anyspace, anyspace, anyspace],
        out_specs=[hm_all, kp, kp], out_shape=[jax.ShapeDtypeStruct(a.shape, a.dtype) for a in grads],
        input_output_aliases={n_in: 0, n_in + 1: 1, n_in + 2: 2},
        scratch_shapes=[pltpu.VMEM((dil * b, hd), F32), pltpu.VMEM((dil * b, hd), F32)],
        compiler_params=_params(("parallel", "arbitrary"), VMEM_LIMIT_ATTN_BWD_BYTES))(
            *([q_all] * rep), *([do] * rep), lse, delta, k_rot, k_rot, kv, kv, *tabs, *grads)


def _adamw(g_slabs, w, m, v, name):
    kk, r, c = g_slabs.shape
    tile = r if r <= 256 else _pick_rows(r, 256)

    def body(g_ref, w_ref, m_ref, v_ref, go_ref, d_ref, mo_ref, vo_ref):
        g = g_ref[0].astype(F32)
        for k in range(1, kk):
            g = g + g_ref[k].astype(F32)
        m2 = ADAM_B1 * m_ref[...] + (1.0 - ADAM_B1) * g
        v2 = ADAM_B2 * v_ref[...] + (1.0 - ADAM_B2) * jnp.square(g)
        m_hat = m2 / (1.0 - ADAM_B1 ** ADAM_STEP)
        v_hat = v2 / (1.0 - ADAM_B2 ** ADAM_STEP)
        go_ref[...] = g
        d_ref[...] = -ADAM_LR * (m_hat / (jnp.sqrt(v_hat) + ADAM_EPS) + ADAM_WD * w_ref[...])
        mo_ref[...] = m2
        vo_ref[...] = v2

    spec = pl.BlockSpec((tile, c), lambda i: (i, 0))
    return pl.pallas_call(
        body, name=name, grid=(r // tile,), in_specs=[pl.BlockSpec((kk, tile, c), lambda i: (0, i, 0)), spec, spec, spec],
        out_specs=[spec] * 4, out_shape=[jax.ShapeDtypeStruct((r, c), F32)] * 4,
        compiler_params=_params(("parallel",)))(g_slabs, w, m, v)


def _pick_rows(r, pref):
    t = (pref // 16) * 16
    while t >= 16:
        if r % t == 0:
            return t
        t -= 16
    return r


def _coords():
    return lax.axis_index("x"), lax.axis_index("y"), lax.axis_index("c")


def _dev_index(px, py, pc):
    return 4 * px + 2 * py + pc


def _all_gather(shards, name):
    na = len(shards)

    def body(*refs):
        ins, outs = refs[:na], refs[na:2 * na]
        send_sems, recv_sems, local_sems = refs[2 * na:]
        x, y, c = _coords()
        me, sibling = (x, y, c), (x, y, 1 - c)
        chips = [(1 - x, y), (x, 1 - y), (1 - x, 1 - y)]

        def copy(a, k, block, to, src=None):
            dst = outs[a].at[_dev_index(*block)]
            return pltpu.make_async_remote_copy(
                src_ref=dst if src is None else src, dst_ref=dst, send_sem=send_sems.at[a * 7 + k],
                recv_sem=recv_sems.at[a * 7 + k], device_id=to, device_id_type=MESH)

        mine = [pltpu.make_async_copy(ins[a], outs[a].at[_dev_index(*me)], local_sems.at[a]) for a in range(na)]
        for cp in mine:
            cp.start()
        first = []
        for a in range(na):
            first.append(copy(a, 0, me, sibling, src=ins[a]))
            first += [copy(a, 1 + j, me, (*chip, c), src=ins[a]) for j, chip in enumerate(chips)]
        for cp in first:
            cp.start()
        passed = []
        for j, chip in enumerate(chips):
            for a in range(na):
                copy(a, 1 + j, (*chip, c), me).wait_recv()
                cp = copy(a, 4 + j, (*chip, c), sibling)
                cp.start()
                passed.append(cp)
        for a in range(na):
            copy(a, 0, sibling, me).wait_recv()
            for j, chip in enumerate(chips):
                copy(a, 4 + j, (*chip, 1 - c), me).wait_recv()
        for cp in first + passed:
            cp.wait_send()
        for cp in mine:
            cp.wait()

    hbm = pl.BlockSpec(memory_space=pl.ANY)
    return pl.pallas_call(
        body, name=name, in_specs=[hbm] * na, out_specs=[hbm] * na,
        out_shape=[jax.ShapeDtypeStruct((NDEV,) + s.shape, s.dtype) for s in shards],
        scratch_shapes=[pltpu.SemaphoreType.DMA((7 * na,)), pltpu.SemaphoreType.DMA((7 * na,)),
                        pltpu.SemaphoreType.DMA((na,))])(*shards)


def _exchange(slabs, whole, name):
    ns, nw = len(slabs), len(whole)
    na = ns + nw

    def body(*refs):
        ins, outs = refs[:na], refs[na:2 * na]
        send_sems, recv_sems, local_sems = refs[2 * na:]
        x, y, c = _coords()
        me = _dev_index(x, y, c)

        def src_of(a, p):
            return ins[a].at[p] if a < ns else ins[a]

        def copy(a, k, peer):
            p = _dev_index(*peer)
            return pltpu.make_async_remote_copy(
                src_ref=src_of(a, p), dst_ref=outs[a].at[me], send_sem=send_sems.at[a * 7 + k - 1],
                recv_sem=recv_sems.at[a * 7 + k - 1], device_id=peer, device_id_type=MESH)

        def arrival(a, k, peer):
            p = _dev_index(*peer)
            return pltpu.make_async_remote_copy(
                src_ref=src_of(a, p), dst_ref=outs[a].at[p], send_sem=send_sems.at[a * 7 + k - 1],
                recv_sem=recv_sems.at[a * 7 + k - 1], device_id=peer, device_id_type=MESH)

        mine = [pltpu.make_async_copy(src_of(a, me), outs[a].at[me], local_sems.at[a]) for a in range(na)]
        for cp in mine:
            cp.start()
        peers = [(k, (x ^ (k >> 2), y ^ ((k >> 1) & 1), c ^ (k & 1))) for k in range(1, NDEV)]
        sent = [copy(a, k, peer) for k, peer in peers for a in range(na)]
        for cp in sent:
            cp.start()
        for k, peer in peers:
            for a in range(na):
                arrival(a, k, peer).wait_recv()
        for cp in sent:
            cp.wait_send()
        for cp in mine:
            cp.wait()

    hbm = pl.BlockSpec(memory_space=pl.ANY)
    out_shape = [jax.ShapeDtypeStruct(s.shape, s.dtype) for s in slabs]
    out_shape += [jax.ShapeDtypeStruct((NDEV,) + w.shape, w.dtype) for w in whole]
    return pl.pallas_call(
        body, name=name, in_specs=[hbm] * na, out_specs=[hbm] * na, out_shape=out_shape,
        scratch_shapes=[pltpu.SemaphoreType.DMA((7 * na,)), pltpu.SemaphoreType.DMA((7 * na,)),
                        pltpu.SemaphoreType.DMA((na,))])(*slabs, *whole)


_HBM = pl.BlockSpec(memory_space=pltpu.HBM)
_SEM = pl.BlockSpec(memory_space=pltpu.SEMAPHORE)
_EFFECT = pltpu.SideEffectType.DATAFLOW_SIDE_EFFECTING


def _peers(x, y, c):
    return [(k, (x ^ (k >> 2), y ^ ((k >> 1) & 1), c ^ (k & 1))) for k in range(1, NDEV)]


def _peer_copy(src, land, send_sems, recv_sems, a, k, dst_block, peer):
    return pltpu.make_async_remote_copy(
        src_ref=src, dst_ref=land.at[dst_block], send_sem=send_sems.at[a * 7 + k - 1],
        recv_sem=recv_sems.at[a * 7 + k - 1], device_id=peer, device_id_type=MESH)


def _send_start(arrays, slabs, name):
    na = len(arrays)
    lands = [jax.ShapeDtypeStruct(a.shape if slabs else (NDEV,) + a.shape, a.dtype) for a in arrays]

    def body(*refs):
        ins, zones = refs[:na], refs[na:2 * na]
        send_sems, recv_sems = refs[2 * na], refs[2 * na + 1]
        token = refs[-1]
        x, y, c = _coords()
        me = _dev_index(x, y, c)
        for k, peer in _peers(x, y, c):
            for a in range(na):
                src = ins[a].at[_dev_index(*peer)] if slabs else ins[a]
                _peer_copy(src, zones[a], send_sems, recv_sems, a, k, me, peer).start()
        token[...] = jnp.zeros_like(token)

    outs = pl.pallas_call(
        body, name=name,
        out_shape=(pltpu.SemaphoreType.DMA((7 * na,)), pltpu.SemaphoreType.DMA((7 * na,)),
                   *[pltpu.HBM(a.shape, a.dtype) for a in arrays], *[pltpu.HBM(l.shape, l.dtype) for l in lands],
                   jax.ShapeDtypeStruct((8, LANES), F32)),
        in_specs=[_HBM] * (2 * na), out_specs=(_SEM, _SEM, *([_HBM] * (2 * na)), pl.BlockSpec(memory_space=pltpu.VMEM)),
        input_output_aliases={i: 2 + i for i in range(2 * na)},
        compiler_params=pltpu.CompilerParams(has_side_effects=_EFFECT),
    )(*[pltpu.with_memory_space_constraint(a, pltpu.HBM) for a in arrays],
      *[pltpu.with_memory_space_constraint(lax.empty(l.shape, l.dtype), pltpu.HBM) for l in lands])
    return outs[0], outs[1], list(outs[2:2 + na]), list(outs[2 + na:2 + 2 * na]), outs[-1]


def _send_wait(started, after, slabs, name):
    send_sems, recv_sems, thru, zones, _ = started
    na = len(thru)

    def body(*refs):
        ins, lands = refs[:na], refs[na:2 * na]
        s_sems, r_sems = refs[2 * na], refs[2 * na + 1]
        x, y, c = _coords()
        for k, peer in _peers(x, y, c):
            p = _dev_index(*peer)
            for a in range(na):
                src = ins[a].at[p] if slabs else ins[a]
                cp = _peer_copy(src, lands[a], s_sems, r_sems, a, k, p, peer)
                cp.wait_send()
                cp.wait_recv()

    outs = pl.pallas_call(
        body, name=name, out_shape=tuple(pltpu.HBM(v.shape, v.dtype) for v in thru + zones),
        in_specs=[_HBM] * (2 * na) + [_SEM, _SEM, pl.BlockSpec(memory_space=pl.ANY)], out_specs=tuple([_HBM] * (2 * na)),
        input_output_aliases={i: i for i in range(2 * na)},
        compiler_params=pltpu.CompilerParams(has_side_effects=_EFFECT),
    )(*thru, *zones, send_sems, recv_sems, after)
    me = _dev_index(*_coords())
    filled = []
    for a in range(na):
        own = lax.dynamic_index_in_dim(outs[a], me, 0, keepdims=False) if slabs else outs[a]
        filled.append(lax.dynamic_update_index_in_dim(outs[na + a], own, me, 0))
    return filled


def _pack(vecs):
    parts, spans, off = [], [], 0
    for v in vecs:
        n = v.size
        pad = (-n) % LANES
        parts.append(jnp.pad(v.reshape(-1).astype(F32), (0, pad)))
        spans.append((off, n))
        off += n + pad
    return jnp.concatenate(parts).reshape(-1, LANES), spans


def _pad_lanes(v):
    v = v.reshape(1, -1)
    return jnp.pad(v, ((0, 0), (0, LANES - v.shape[1])))


def _cols_to_slabs(g):
    sh = g.shape
    g = g.reshape(sh[:-1] + (NDEV, sh[-1] // NDEV))
    return jnp.moveaxis(g, -2, 0)


def _rows_to_slabs(g):
    sh = g.shape
    g = g.reshape(sh[:-2] + (NDEV, sh[-2] // NDEV, sh[-1]))
    return jnp.moveaxis(g, -3, 0)


def _slabs_to_cols(a):
    a = jnp.moveaxis(a, 0, -2)
    return a.reshape(a.shape[:-2] + (a.shape[-2] * a.shape[-1],))


def _slabs_to_rows(a):
    a = jnp.moveaxis(a, 0, -3)
    return a.reshape(a.shape[:-3] + (a.shape[-3] * a.shape[-2], a.shape[-1]))


def _ffn_forward(x, norm_w, wup_g, wup_v, cw_g, cw_v, wdown, tag):
    h = _rms_fwd(x, norm_w, f"{tag}_norm")
    ug = _mm(h, wup_g, name=f"{tag}_up_gate")
    uv = _mm(h, wup_v, name=f"{tag}_up_val")
    f = _ffn_gate_fwd(ug, uv, cw_g, cw_v, f"{tag}_gate")
    return _mm(f, wdown, res=x, name=f"{tag}_down"), (h, ug, uv, f)


def _ffn_backward(x, saved, dout, dout_b, norm_w, wup_g, wup_v, cw_g, cw_v, wdown, tag):
    h, ug, uv, f = saved
    dwdown = _mm(f, dout_b, ta=True, name=f"{tag}_dwdown")
    df = _mm(dout_b, wdown, tb=True, name=f"{tag}_df")
    dug, duv, dcg, dcv = _ffn_gate_bwd(ug, uv, cw_g, cw_v, df, f"{tag}_gate_bwd")
    dwg = _mm(h, dug, ta=True, name=f"{tag}_dwup_gate")
    dwv = _mm(h, duv, ta=True, name=f"{tag}_dwup_val")
    dh = _mm(dug, wup_g, tb=True, name=f"{tag}_dh_gate")
    dh = _mm(duv, wup_v, tb=True, res=dh, name=f"{tag}_dh_val")
    dx, dxb, dnorm = _rms_bwd(x, norm_w, dh, dout, f"{tag}_norm_bwd")
    return dx, dxb, (jnp.concatenate([dwg, dwv], axis=1), jnp.concatenate([dcg, dcv], axis=1), dwdown, dnorm)


def kernel(x, a_norm, ssm_w_in, ssm_conv_w, ssm_conv_b, ssm_dt_bias, ssm_a_log, ssm_d, ssm_norm, ssm_w_out, kv_norm, w_kv, b_norm, att_w_q, att_w_o, ffn_norm, ffn_w_up, ffn_conv_w, ffn_w_down, final_norm, loss_target, m_a_norm, m_ssm_w_in, m_ssm_conv_w, m_ssm_conv_b, m_ssm_dt_bias, m_ssm_a_log, m_ssm_d, m_ssm_norm, m_ssm_w_out, m_kv_norm, m_w_kv, m_b_norm, m_att_w_q, m_att_w_o, m_ffn_norm, m_ffn_w_up, m_ffn_conv_w, m_ffn_w_down, m_final_norm, v_a_norm, v_ssm_w_in, v_ssm_conv_w, v_ssm_conv_b, v_ssm_dt_bias, v_ssm_a_log, v_ssm_d, v_ssm_norm, v_ssm_w_out, v_kv_norm, v_w_kv, v_b_norm, v_att_w_q, v_att_w_o, v_ffn_norm, v_ffn_w_up, v_ffn_conv_w, v_ffn_w_down, v_final_norm):
    given = dict(locals())
    xs, tgt = x[0], loss_target[0]
    s, d = xs.shape
    di = ssm_w_out.shape[1] * NDEV
    nh = ssm_dt_bias.shape[1]
    ng = SSM_N_GROUPS
    convd = di + 2 * ng * SSM_D_STATE
    f = ffn_w_down.shape[1] * NDEV
    n_att = len(ATT_PATTERNS)
    qg = ATT_HEADS_PER_GROUP * ATT_HEAD_DIM
    kg = ATT_KV_HEADS_PER_GROUP * ATT_HEAD_DIM
    kvd = n_att * kg
    assert all(w // dil == ATT_BLOCK for w, dil in ATT_PATTERNS)

    small, _ = _pack([a_norm, ssm_conv_w, ssm_conv_b, ssm_norm, ffn_conv_w])
    gat = _all_gather([ssm_w_in[0].astype(BF16), small], "gather_weights")
    first = _send_start([ssm_w_out[0].astype(BF16), ffn_w_up[0].astype(BF16), ffn_w_down[0].astype(BF16)], False,
                        "gather_ffn0_start")
    rest = _send_start([b.astype(BF16) for b in (w_kv, att_w_q[0], att_w_o[0], ffn_w_up[1], ffn_w_down[1])], False,
                       "gather_rest_start")
    w_in = _slabs_to_cols(gat[0])
    w_z, w_xbc = w_in[:, :di], w_in[:, di:di + convd]
    w_dt = jnp.pad(w_in[:, di + convd:], ((0, 0), (0, LANES - nh)))
    sm = gat[1].reshape(NDEV, -1)
    o0 = 0

    def take(shape):
        nonlocal o0
        n = math.prod(shape)
        out = sm[:, o0:o0 + n].reshape((NDEV,) + shape)
        o0 += n + (-n) % LANES
        return out
    a_norm_f = _slabs_to_cols(take(a_norm.shape)) + (first[-1][0, 0] + rest[-1][0, 0])
    conv_w_f = _slabs_to_cols(take(ssm_conv_w.shape))[0]
    conv_b_f = _slabs_to_cols(take(ssm_conv_b.shape))
    ssm_norm_f = _slabs_to_cols(take(ssm_norm.shape))
    fcw = _slabs_to_cols(take(ffn_conv_w.shape))
    fcw_g, fcw_v = fcw[:, :, :f], fcw[:, :, f:]
    dtb, alog, dsk = _pad_lanes(ssm_dt_bias), _pad_lanes(ssm_a_log), _pad_lanes(ssm_d)
    kvn, fin = kv_norm.reshape(1, d), final_norm.reshape(1, d)

    h0 = _rms_fwd(xs, a_norm_f, "a_norm")
    z = _mm(h0, w_z, name="in_z")
    xbc_pre = _mm(h0, w_xbc, name="in_xbc")
    dtr = _mm(h0, w_dt, name="in_dt")
    xbc = _conv_silu_fwd(xbc_pre, conv_w_f, conv_b_f, "ssm_conv")
    dt = _softplus_fwd(dtr, dtb, "ssm_dt")
    y, prevs = _ssd_fwd2(xbc, dt, alog, dsk, di, nh, ng, "ssd")
    yn = _gnorm_fwd(y, z, ssm_norm_f, ng, "ssm_gnorm")
    got = _send_wait(first, yn, False, "gather_ffn0_wait")
    w_out = _slabs_to_rows(got[0])
    w_up0, w_down0 = _slabs_to_cols(got[1]), _slabs_to_rows(got[2])
    x1 = _mm(yn, w_out, res=xs, name="ssm_out")
    x2, ffn0 = _ffn_forward(x1, ffn_norm[0:1], w_up0[:, :f], w_up0[:, f:], fcw_g[0], fcw_v[0], w_down0, "ffn0")
    got = _send_wait(rest, x2, False, "gather_rest_wait")
    w_kvf = _slabs_to_cols(got[0])
    w_q = _slabs_to_cols(got[1])
    w_o = _slabs_to_rows(got[2])
    w_up1, w_down1 = _slabs_to_cols(got[3]), _slabs_to_rows(got[4])
    w_up_g, w_up_v = (w_up0[:, :f], w_up1[:, :f]), (w_up0[:, f:], w_up1[:, f:])
    w_down = (w_down0, w_down1)
    hk = _rms_fwd(x2, kvn, "kv_norm")
    kv = _mm(hk, w_kvf, name="kv_proj")
    h2 = _rms_fwd(x2, b_norm, "b_norm")
    q = _mm(h2, w_q, name="q_proj")
    tabs = _rot_tables(s, 1)
    k_rot = _rot_heads(kv, tabs, kvd, 1.0, F32, "k_rot")
    att = [_attn_fwd_nat(q, k_rot, kv, tabs, g, dil, f"attn{g}") for g, (_, dil) in enumerate(ATT_PATTERNS)]
    o, ob, lse = _merge_heads([t[0] for t in att], [t[1] for t in att], "attn_merge")
    x3 = _mm(ob, w_o, res=x2, name="attn_out")
    x4, ffn1 = _ffn_forward(x3, ffn_norm[1:2], w_up_g[1], w_up_v[1], fcw_g[1], fcw_v[1], w_down[1], "ffn1")
    loss_part, dx4, dx4b, dfin = _final_loss(x4, fin, tgt, "loss_head")

    dx3, dx3b, (dwup1, dfc1, dwdown1, dfn1) = _ffn_backward(
        x3, ffn1, dx4, dx4b, ffn_norm[1:2], w_up_g[1], w_up_v[1], fcw_g[1], fcw_v[1], w_down[1], "ffn1")
    dw_o = _mm(ob, dx3b, ta=True, name="attn_dwo")
    do = _mm(dx3b, w_o, tb=True, name="attn_do")
    delta = _delta_heads(do, o, "attn_delta")
    grads = (lax.empty((n_att * qg // LANES, s, LANES), F32), lax.empty((s, kvd), F32), lax.empty((s, kvd), F32))
    for g, (_, dil) in enumerate(ATT_PATTERNS):
        grads = _attn_bwd_nat(q, k_rot, kv, do, lse, delta, tabs, grads, g, dil, f"attn{g}_bwd")
    dq, dk_rot, dv = grads
    dk = _rot_heads(dk_rot, tabs, kvd, -1.0, BF16, "k_rot_bwd")
    dw_q = _mm(h2, dq, ta=True, b_heads=True, name="q_dw")
    dh2 = _mm(dq, w_q, tb=True, a_heads=True, name="q_dh")
    dw_kv = jnp.concatenate([_mm(hk, dk, ta=True, name="k_dw"), _mm(hk, dv, ta=True, name="v_dw")], axis=1)
    dhk = _mm(dk, w_kvf[:, :kvd], tb=True, name="k_dh")
    dhk = _mm(dv, w_kvf[:, kvd:], tb=True, res=dhk, name="v_dh")
    dx2, _, db_norm = _rms_bwd(x2, b_norm, dh2, dx3, "b_norm_bwd")
    dx2, dx2b, dkv_norm = _rms_bwd(x2, kvn, dhk, dx2, "kv_norm_bwd")
    sent1 = _send_start([_cols_to_slabs(dwup1).astype(BF16), _rows_to_slabs(dwdown1).astype(BF16),
                         _cols_to_slabs(dw_kv).astype(BF16), _cols_to_slabs(dw_q).astype(BF16),
                         _rows_to_slabs(dw_o).astype(BF16)], True, "grads_late_start")
    dx1, dx1b, (dwup0, dfc0, dwdown0, dfn0) = _ffn_backward(
        x1, ffn0, dx2, dx2b, ffn_norm[0:1], w_up_g[0], w_up_v[0], fcw_g[0] + sent1[-1][0, 0], fcw_v[0], w_down[0],
        "ffn0")
    sent0 = _send_start([_cols_to_slabs(dwup0).astype(BF16), _rows_to_slabs(dwdown0).astype(BF16)], True,
                        "grads_ffn0_start")
    dw_out = _mm(yn, dx1b, ta=True, name="ssm_dwout")
    dyn = _mm(dx1b, w_out, tb=True, name="ssm_dyn")
    dy, dz, dssm_norm = _gnorm_bwd(dyn, y, z, ssm_norm_f + sent0[-1][0, 0], ng, "ssm_gnorm_bwd")
    dxbc, ddt, dalog, ddsk = _ssd_bwd2(xbc, dt, alog, dsk, prevs, dy, di, nh, ng, "ssd_bwd")
    ddtr, ddtb = _softplus_bwd(ddt, dtr, dtb, nh, "ssm_dt_bwd")
    dxbc_pre, dconv_w, dconv_b = _conv_silu_bwd(xbc_pre, conv_w_f, conv_b_f, dxbc, "ssm_conv_bwd")
    dw_z = _mm(h0, dz, ta=True, name="in_dwz")
    dw_xbc = _mm(h0, dxbc_pre, ta=True, name="in_dwxbc")
    dw_dt = _mm(h0, ddtr, ta=True, name="in_dwdt")[:, :nh]
    sent_m = _send_start([_cols_to_slabs(jnp.concatenate([dw_z, dw_xbc, dw_dt], axis=1)).astype(BF16),
                          _rows_to_slabs(dw_out).astype(BF16)], True, "grads_mamba_start")
    dh0 = _mm(dz, w_z, tb=True, name="in_dh_z")
    dh0 = _mm(dxbc_pre, w_xbc, tb=True, res=dh0, name="in_dh_xbc")
    dh0 = _mm(ddtr, w_dt, tb=True, res=dh0, name="in_dh_dt")
    dx0, _, da_norm = _rms_bwd(xs, a_norm_f + sent_m[-1][0, 0], dh0, dx1, "a_norm_bwd")

    small_full = {
        'a_norm': da_norm, 'ssm_conv_w': dconv_w[None], 'ssm_conv_b': dconv_b, 'ssm_dt_bias': ddtb[:, :nh],
        'ssm_a_log': dalog[:, :nh], 'ssm_d': ddsk[:, :nh], 'ssm_norm': dssm_norm, 'kv_norm': dkv_norm.reshape(d),
        'b_norm': db_norm, 'ffn_norm': jnp.concatenate([dfn0, dfn1], axis=0), 'ffn_conv_w': jnp.stack([dfc0, dfc1]),
        'final_norm': dfin.reshape(d),
    }
    small_names = list(small_full)
    packed, spans = _pack([small_full[n] for n in small_names])
    recv = _exchange([], [packed], "exchange_grads")
    small_sum = _sum_slabs(recv[-1], "sum_small_grads").reshape(-1)
    got1 = _send_wait(sent1, recv[-1], True, "grads_late_wait")
    got0 = _send_wait(sent0, recv[-1], True, "grads_ffn0_wait")
    recv_big = {
        'w_kv': got1[2], 'att_w_q': got1[3], 'att_w_o': got1[4],
        'ffn_w_up': jnp.concatenate([got0[0], got1[0]], axis=1),
        'ffn_w_down': jnp.concatenate([got0[1], got1[1]], axis=1),
    }

    me = _dev_index(*_coords())
    res = {}

    def update_big(n, r):
        w = given[n]
        c = w.shape[-1]
        outs = _adamw(r.reshape(NDEV, -1, c), w.reshape(-1, c), given['m_' + n].reshape(-1, c),
                      given['v_' + n].reshape(-1, c), f"adamw_{n}")
        res[n] = [o_.reshape(w.shape) for o_ in outs]
    for n, r in recv_big.items():
        update_big(n, r)
    gotm = _send_wait(sent_m, res['ffn_w_up'][1], True, "grads_mamba_wait")
    update_big('ssm_w_in', gotm[0])
    update_big('ssm_w_out', gotm[1])
    sharded_small = {'a_norm', 'ssm_conv_w', 'ssm_conv_b', 'ssm_norm', 'ffn_conv_w'}
    for n, (off, size) in zip(small_names, spans):
        w = given[n]
        gfull = small_sum[off:off + size].reshape(small_full[n].shape)
        if n in sharded_small:
            c = w.shape[-1]
            gfull = lax.dynamic_slice_in_dim(gfull, me * c, c, axis=gfull.ndim - 1)
        c = w.shape[-1]
        outs = _adamw(gfull.reshape(1, -1, c), w.reshape(-1, c), given['m_' + n].reshape(-1, c),
                      given['v_' + n].reshape(-1, c), f"adamw_{n}")
        res[n] = [o_.reshape(w.shape) for o_ in outs]

    loss = lax.psum(loss_part[0, 0], AXES)
    return (loss, dx0[None], *[res[n][0] for n in WEIGHTS], *[res[n][1] for n in WEIGHTS],
            *[res[n][2] for n in WEIGHTS], *[res[n][3] for n in WEIGHTS])
```

```python
import functools
import math

import jax
import jax.numpy as jnp
from jax import lax
from jax.experimental import pallas as pl
from jax.experimental.pallas import tpu as pltpu

F32, BF16 = jnp.float32, jnp.bfloat16
AXES = ("x", "y", "c")
NDEV = 8
MESH = pl.DeviceIdType.MESH
HIGHEST = lax.Precision.HIGHEST

LANES = 128
VMEM_LIMIT_BYTES = 48 * 1024 * 1024
VMEM_LIMIT_ATTN_BWD_BYTES = 58 * 1024 * 1024

RMS_EPS = 1e-6
GATED_NORM_EPS = 1e-5
SSM_HEAD_DIM = 64
SSM_N_GROUPS = 8
SSM_D_STATE = 128
SSM_CONV = 4
SSM_CHUNK = 128
ATT_PATTERNS = ((128, 1), (512, 4), (2048, 16))
ATT_HEAD_DIM = 128
ATT_HEADS_PER_GROUP = 8
ATT_KV_HEADS_PER_GROUP = 2
ATT_BLOCK = 128
ROPE_DIM = ATT_HEAD_DIM // 4
ROPE_THETA = 500000.0
FFN_CONV = 3
ADAM_LR = 0.001
ADAM_B1 = 0.9
ADAM_B2 = 0.999
ADAM_EPS = 1e-08
ADAM_WD = 0.01
ADAM_STEP = 10
NEG = -1e30

WEIGHTS = ['a_norm', 'ssm_w_in', 'ssm_conv_w', 'ssm_conv_b', 'ssm_dt_bias', 'ssm_a_log', 'ssm_d', 'ssm_norm',
           'ssm_w_out', 'kv_norm', 'w_kv', 'b_norm', 'att_w_q', 'att_w_o', 'ffn_norm', 'ffn_w_up', 'ffn_conv_w',
           'ffn_w_down', 'final_norm']


def _params(sem=None, vmem=VMEM_LIMIT_BYTES):
    kw = dict(vmem_limit_bytes=vmem)
    if sem is not None:
        kw["dimension_semantics"] = sem
    return pltpu.CompilerParams(**kw)


def _pick(n, pref):
    if n <= pref:
        return n
    t = (pref // LANES) * LANES
    while t >= LANES:
        if n % t == 0:
            return t
        t -= LANES
    return n


def _dot(a, b, dims=(((1,), (0,)), ((), ())), precision=None):
    return lax.dot_general(a, b, dims, precision=precision, preferred_element_type=F32)


_NT = (((1,), (1,)), ((), ()))
_TN = (((0,), (0,)), ((), ()))


def _mm(a, b, *, ta=False, tb=False, res=None, out_dtype=None, name, tm=1408, tn=1408, tk=2048,
        a_heads=False, b_heads=False):
    assert not (a_heads and ta) and not (b_heads and tb)
    if out_dtype is None:
        out_dtype = BF16 if ta else F32
    if a_heads:
        m, k = a.shape[1], a.shape[0] * LANES
    else:
        m = a.shape[1] if ta else a.shape[0]
        k = a.shape[0] if ta else a.shape[1]
    if b_heads:
        n, kb = b.shape[0] * LANES, b.shape[1]
    else:
        n = b.shape[0] if tb else b.shape[1]
        kb = b.shape[1] if tb else b.shape[0]
    assert k == kb
    tm, tn, tk = _pick(m, tm), _pick(n, tn), _pick(k, tk)
    nk = k // tk
    if a_heads:
        a_spec = pl.BlockSpec((tk // LANES, tm, LANES), lambda i, j, l: (l, i, 0))
    elif ta:
        a_spec = pl.BlockSpec((tk, tm), lambda i, j, l: (l, i))
    else:
        a_spec = pl.BlockSpec((tm, tk), lambda i, j, l: (i, l))
    if b_heads:
        b_spec = pl.BlockSpec((tn // LANES, tk, LANES), lambda i, j, l: (j, l, 0))
    elif tb:
        b_spec = pl.BlockSpec((tn, tk), lambda i, j, l: (j, l))
    else:
        b_spec = pl.BlockSpec((tk, tn), lambda i, j, l: (l, j))
    o_spec = pl.BlockSpec((tm, tn), lambda i, j, l: (i, j))
    dims = (((0 if ta else 1,), (1 if tb else 0,)), ((), ()))
    has_res = res is not None

    def load(ref, heads):
        if not heads:
            return ref[...].astype(BF16)
        return jnp.concatenate([ref[i].astype(BF16) for i in range(ref.shape[0])], axis=1)

    def body(*refs):
        a_ref, b_ref = refs[:2]
        r_ref = refs[2] if has_res else None
        o_ref = refs[2 + has_res]
        p = _dot(load(a_ref, a_heads), load(b_ref, b_heads), dims)

        def finish(r):
            if has_res:
                r = r + r_ref[...]
            o_ref[...] = r.astype(o_ref.dtype)

        if nk == 1:
            finish(p)
            return
        acc = refs[3 + has_res]
        l = pl.program_id(2)

        @pl.when(l == 0)
        def _():
            acc[...] = p

        @pl.when(jnp.logical_and(l > 0, l < nk - 1))
        def _():
            acc[...] += p

        @pl.when(l == nk - 1)
        def _():
            finish(acc[...] + p)

    ins = [a, b] + ([res] if has_res else [])
    in_specs = [a_spec, b_spec] + ([o_spec] if has_res else [])
    return pl.pallas_call(
        body, name=name, grid=(m // tm, n // tn, nk), in_specs=in_specs, out_specs=o_spec,
        out_shape=jax.ShapeDtypeStruct((m, n), out_dtype),
        scratch_shapes=[pltpu.VMEM((tm, tn), F32)] if nk > 1 else [],
        compiler_params=_params(("parallel", "parallel", "arbitrary")))(*ins)


def _rowwise(fn, rows, bcasts, outs, accs=(), *, tile, name):
    s = rows[0].shape[-2]
    tile = min(tile, s)
    n_in, n_out, n_acc = len(rows) + len(bcasts), len(outs), len(accs)

    def row_spec(c):
        if isinstance(c, tuple):
            return pl.BlockSpec((c[0], tile, c[1]), lambda i: (0, i, 0))
        return pl.BlockSpec((tile, c), lambda i: (i, 0))

    def row_shape(c):
        return (c[0], s, c[1]) if isinstance(c, tuple) else (s, c)

    def body(*refs):
        vals = fn(*[r[...] for r in refs[:n_in]])
        o_refs = refs[n_in:n_in + n_out]
        a_refs = refs[n_in + n_out:]
        for r, v in zip(o_refs, vals[:n_out]):
            if isinstance(v, list):
                for i, vi in enumerate(v):
                    r[i] = vi.astype(r.dtype)
            else:
                r[...] = v.astype(r.dtype)

        @pl.when(pl.program_id(0) == 0)
        def _():
            for r in a_refs:
                r[...] = jnp.zeros(r.shape, r.dtype)

        for r, v in zip(a_refs, vals[n_out:]):
            r[...] += v

    in_specs = [row_spec(r.shape[1] if r.ndim == 2 else (r.shape[0], r.shape[2])) for r in rows]
    in_specs += [pl.BlockSpec(b.shape, lambda i: (0, 0)) for b in bcasts]
    out_specs = [row_spec(c) for c, _ in outs]
    out_specs += [pl.BlockSpec(sh, lambda i: (0, 0)) for sh, _ in accs]
    out_shape = [jax.ShapeDtypeStruct(row_shape(c), dt) for c, dt in outs]
    out_shape += [jax.ShapeDtypeStruct(sh, dt) for sh, dt in accs]
    return pl.pallas_call(body, name=name, grid=(s // tile,), in_specs=in_specs, out_specs=out_specs,
                          out_shape=out_shape, compiler_params=_params(("arbitrary",)))(*rows, *bcasts)


def _rms_fwd(x, w, name):
    def fn(x, w):
        r = lax.rsqrt(jnp.mean(x * x, axis=-1, keepdims=True) + RMS_EPS)
        return (x * r * w,)
    return _rowwise(fn, [x], [w], [(x.shape[1], BF16)], tile=256, name=name)[0]


def _rms_bwd(x, w, dh, dres, name):
    def fn(x, dh, dres, w):
        r = lax.rsqrt(jnp.mean(x * x, axis=-1, keepdims=True) + RMS_EPS)
        xh = x * r
        dxh = dh * w
        dx = dres + r * (dxh - xh * jnp.mean(dxh * xh, axis=-1, keepdims=True))
        return dx, dx, jnp.sum(dh * xh, axis=0, keepdims=True)
    d = x.shape[1]
    return _rowwise(fn, [x, dh, dres], [w], [(d, F32), (d, BF16)], [((1, d), F32)], tile=256, name=name)


def _final_loss(x, w, tgt, name):
    d = x.shape[1]

    def fn(x, t, w):
        r = lax.rsqrt(jnp.mean(x * x, axis=-1, keepdims=True) + RMS_EPS)
        xh = x * r
        err = xh * w - t
        part = jnp.sum(jnp.mean(err * err, axis=-1, keepdims=True), axis=0, keepdims=True) * 0.5
        dy = err * (1.0 / d)
        dxh = dy * w
        dx = r * (dxh - xh * jnp.mean(dxh * xh, axis=-1, keepdims=True))
        return dx, dx, part, jnp.sum(dy * xh, axis=0, keepdims=True)
    dx, dxb, part, dw = _rowwise(fn, [x, tgt], [w], [(d, F32), (d, BF16)], [((1, 1), F32), ((1, d), F32)],
                                 tile=256, name=name)
    return part, dx, dxb, dw


def _softplus_fwd(dtr, bias, name):
    def fn(r, b):
        v = r + b
        return (jnp.maximum(v, 0.0) + jnp.log(1.0 + jnp.exp(-jnp.abs(v))),)
    return _rowwise(fn, [dtr], [bias], [(LANES, F32)], tile=512, name=name)[0]


def _softplus_bwd(ddt, dtr, bias, n_heads, name):
    def fn(g, r, b):
        lane = lax.broadcasted_iota(jnp.int32, g.shape, 1)
        d = jnp.where(lane < n_heads, g * jax.nn.sigmoid(r + b), 0.0)
        return d, jnp.sum(d, axis=0, keepdims=True)
    return _rowwise(fn, [ddt, dtr], [bias], [(LANES, BF16)], [((1, LANES), F32)], tile=512, name=name)


def _gnorm_fwd(y, z, w, n_groups, name):
    di = y.shape[1]
    gs = di // n_groups

    def fn(y, z, w):
        y2 = y * (z * jax.nn.sigmoid(z))
        out = []
        for g in range(n_groups):
            sl = y2[:, g * gs:(g + 1) * gs]
            r = lax.rsqrt(jnp.mean(sl * sl, axis=-1, keepdims=True) + GATED_NORM_EPS)
            out.append(sl * r)
        return (jnp.concatenate(out, axis=1) * w,)
    return _rowwise(fn, [y, z], [w], [(di, BF16)], tile=256, name=name)[0]


def _gnorm_bwd(dyn, y, z, w, n_groups, name):
    di = y.shape[1]
    gs = di // n_groups

    def fn(dyn, y, z, w):
        sig = jax.nn.sigmoid(z)
        sz = z * sig
        y2 = y * sz
        d2n = dyn * w
        dy2, yhat = [], []
        for g in range(n_groups):
            sl = y2[:, g * gs:(g + 1) * gs]
            dg = d2n[:, g * gs:(g + 1) * gs]
            r = lax.rsqrt(jnp.mean(sl * sl, axis=-1, keepdims=True) + GATED_NORM_EPS)
            yh = sl * r
            dy2.append(r * (dg - yh * jnp.mean(dg * yh, axis=-1, keepdims=True)))
            yhat.append(yh)
        dy2 = jnp.concatenate(dy2, axis=1)
        yhat = jnp.concatenate(yhat, axis=1)
        dz = dy2 * y * (sig * (1.0 + z * (1.0 - sig)))
        return dy2 * sz, dz, jnp.sum(dyn * yhat, axis=0, keepdims=True)
    return _rowwise(fn, [dyn, y, z], [w], [(di, F32), (di, BF16)], [((1, di), F32)], tile=128, name=name)


def _merge_fwd(os_, lses, name):
    n = len(os_)

    def fn(*v):
        o, l = v[:n], v[n:]
        m = functools.reduce(jnp.maximum, l)
        e = [jnp.exp(li - m) for li in l]
        tot = functools.reduce(jnp.add, e)
        acc = functools.reduce(jnp.add, [ei * oi for ei, oi in zip(e, o)]) / tot
        return acc, acc, m + jnp.log(tot)
    c = os_[0].shape[1]
    return _rowwise(fn, list(os_) + list(lses), [], [(c, F32), (c, BF16), (c, F32)], tile=256, name=name)


def _delta(do, o, name):
    c = o.shape[1]

    def fn(do, o):
        p = do * o
        out = [jnp.broadcast_to(jnp.sum(p[:, j:j + ATT_HEAD_DIM], axis=-1, keepdims=True), (p.shape[0], ATT_HEAD_DIM))
               for j in range(0, c, ATT_HEAD_DIM)]
        return (jnp.concatenate(out, axis=1),)
    return _rowwise(fn, [do, o], [], [(c, F32)], tile=256, name=name)[0]


def _lane_place(cols):
    rows = cols[0].shape[0]
    lane = lax.broadcasted_iota(jnp.int32, (rows, LANES), 1)
    out = jnp.zeros((rows, LANES), F32)
    for j, c in enumerate(cols):
        out = jnp.where(lane == j, c, out)
    return out


def _merge_heads(os_, lses, name):
    n = len(os_)
    n_kv, rep, hd = ATT_KV_HEADS_PER_GROUP, ATT_HEADS_PER_GROUP // ATT_KV_HEADS_PER_GROUP, ATT_HEAD_DIM

    def fn(*v):
        o, l = v[:n], v[n:]
        out, lse = [], []
        for h in range(n_kv):
            cols = []
            for j in range(rep):
                hh = h * rep + j
                lg = [li[h][:, j:j + 1] for li in l]
                m = functools.reduce(jnp.maximum, lg)
                e = [jnp.exp(x - m) for x in lg]
                tot = functools.reduce(jnp.add, e)
                acc = functools.reduce(jnp.add, [ei * oi[hh] for ei, oi in zip(e, o)])
                out.append(acc / tot)
                cols.append(m + jnp.log(tot))
            lse.append(_lane_place(cols))
        merged = jnp.concatenate(out, axis=1)
        return merged, merged, lse
    c = os_[0].shape[0] * hd
    return _rowwise(fn, list(os_) + list(lses), [], [(c, F32), (c, BF16), ((n_kv, LANES), F32)], tile=256, name=name)


def _delta_heads(do, o, name):
    n_kv, rep, hd = ATT_KV_HEADS_PER_GROUP, ATT_HEADS_PER_GROUP // ATT_KV_HEADS_PER_GROUP, ATT_HEAD_DIM

    def fn(do, o):
        p = do * o
        return ([_lane_place([jnp.sum(p[:, (h * rep + j) * hd:(h * rep + j + 1) * hd], axis=-1, keepdims=True)
                              for j in range(rep)]) for h in range(n_kv)],)
    return _rowwise(fn, [do, o], [], [((n_kv, LANES), F32)], tile=256, name=name)[0]


def _sum_slabs(recv, name):
    def body(r_ref, o_ref):
        acc = r_ref[0]
        for k in range(1, NDEV):
            acc = acc + r_ref[k]
        o_ref[...] = acc
    return pl.pallas_call(body, name=name, out_shape=jax.ShapeDtypeStruct(recv.shape[1:], F32),
                          compiler_params=_params())(recv)


def _shift_down(x, k):
    if k == 0:
        return x
    row = lax.broadcasted_iota(jnp.int32, x.shape, 0)
    return jnp.where(row >= k, pltpu.roll(x, k, 0), 0.0)


def _shift_up(x, k):
    if k == 0:
        return x
    s = x.shape[0]
    row = lax.broadcasted_iota(jnp.int32, x.shape, 0)
    return jnp.where(row < s - k, pltpu.roll(x, s - k, 0), 0.0)


def _conv(x, w):
    kw = w.shape[0]
    return functools.reduce(jnp.add, [w[k:k + 1, :] * _shift_down(x, kw - 1 - k) for k in range(kw)])


def _conv_t(dy, w):
    kw = w.shape[0]
    return functools.reduce(jnp.add, [w[k:k + 1, :] * _shift_up(dy, kw - 1 - k) for k in range(kw)])


def _conv_dw(x, dy, dw_ref):
    kw = dw_ref.shape[0]
    for k in range(kw):
        dw_ref[k:k + 1, :] = jnp.sum(dy * _shift_down(x, kw - 1 - k), axis=0, keepdims=True)


def _dsilu(pre):
    sig = jax.nn.sigmoid(pre)
    return sig * (1.0 + pre * (1.0 - sig))


def _col_specs(s, c, kw, tc):
    return (pl.BlockSpec((s, tc), lambda j: (0, j)), pl.BlockSpec((kw, tc), lambda j: (0, j)),
            pl.BlockSpec((1, tc), lambda j: (0, j)))


def _conv_silu_fwd(x, w, b, name):
    s, c = x.shape
    tc = LANES
    xs, ws, bs = _col_specs(s, c, w.shape[0], tc)

    def body(x_ref, w_ref, b_ref, o_ref):
        pre = _conv(x_ref[...], w_ref[...]) + b_ref[...]
        o_ref[...] = pre * jax.nn.sigmoid(pre)
    return pl.pallas_call(body, name=name, grid=(c // tc,), in_specs=[xs, ws, bs], out_specs=xs,
                          out_shape=jax.ShapeDtypeStruct((s, c), F32), compiler_params=_params(("parallel",)))(x, w, b)


def _conv_silu_bwd(x, w, b, dy, name):
    s, c = x.shape
    tc = LANES
    xs, ws, bs = _col_specs(s, c, w.shape[0], tc)

    def body(x_ref, w_ref, b_ref, dy_ref, dx_ref, dw_ref, db_ref):
        xv, wv = x_ref[...], w_ref[...]
        pre = _conv(xv, wv) + b_ref[...]
        dpre = dy_ref[...] * _dsilu(pre)
        dx_ref[...] = _conv_t(dpre, wv).astype(dx_ref.dtype)
        _conv_dw(xv, dpre, dw_ref)
        db_ref[...] = jnp.sum(dpre, axis=0, keepdims=True)
    return pl.pallas_call(
        body, name=name, grid=(c // tc,), in_specs=[xs, ws, bs, xs], out_specs=[xs, ws, bs],
        out_shape=[jax.ShapeDtypeStruct((s, c), BF16), jax.ShapeDtypeStruct(w.shape, F32),
                   jax.ShapeDtypeStruct((1, c), F32)],
        compiler_params=_params(("parallel",)))(x, w, b, dy)


def _ffn_gate_fwd(ug, uv, wg, wv, name):
    s, c = ug.shape
    tc = LANES
    xs, ws, _ = _col_specs(s, c, wg.shape[0], tc)

    def body(g_ref, v_ref, wg_ref, wv_ref, o_ref):
        g = _conv(g_ref[...], wg_ref[...])
        v = _conv(v_ref[...], wv_ref[...])
        o_ref[...] = (g * jax.nn.sigmoid(g) * v).astype(o_ref.dtype)
    return pl.pallas_call(body, name=name, grid=(c // tc,), in_specs=[xs, xs, ws, ws], out_specs=xs,
                          out_shape=jax.ShapeDtypeStruct((s, c), BF16),
                          compiler_params=_params(("parallel",)))(ug, uv, wg, wv)


def _ffn_gate_bwd(ug, uv, wg, wv, df, name):
    s, c = ug.shape
    tc = LANES
    xs, ws, _ = _col_specs(s, c, wg.shape[0], tc)

    def body(g_ref, v_ref, wg_ref, wv_ref, df_ref, dg_ref, dv_ref, dwg_ref, dwv_ref):
        gp, vp, wgv, wvv = g_ref[...], v_ref[...], wg_ref[...], wv_ref[...]
        g = _conv(gp, wgv)
        v = _conv(vp, wvv)
        dfv = df_ref[...]
        dg = dfv * v * _dsilu(g)
        dv = dfv * (g * jax.nn.sigmoid(g))
        dg_ref[...] = _conv_t(dg, wgv).astype(dg_ref.dtype)
        dv_ref[...] = _conv_t(dv, wvv).astype(dv_ref.dtype)
        _conv_dw(gp, dg, dwg_ref)
        _conv_dw(vp, dv, dwv_ref)
    return pl.pallas_call(
        body, name=name, grid=(c // tc,), in_specs=[xs, xs, ws, ws, xs], out_specs=[xs, xs, ws, ws],
        out_shape=[jax.ShapeDtypeStruct((s, c), BF16), jax.ShapeDtypeStruct((s, c), BF16),
                   jax.ShapeDtypeStruct(wg.shape, F32), jax.ShapeDtypeStruct(wv.shape, F32)],
        compiler_params=_params(("parallel",)))(ug, uv, wg, wv, df)


def _ssd_common(dt, alog, n_heads):
    ln = dt.shape[0]
    lane = lax.broadcasted_iota(jnp.int32, (1, LANES), 1)
    a = jnp.where(lane < n_heads, -jnp.exp(alog), 0.0)
    row = lax.broadcasted_iota(jnp.int32, (ln, ln), 0)
    col = lax.broadcasted_iota(jnp.int32, (ln, ln), 1)
    tril = col <= row
    acs = _dot(tril.astype(F32), dt * a, precision=HIGHEST)
    return a, acs, acs.T, tril


def _ssd_fwd(xbc, dt, alog, dskip, di, n_heads, n_groups, name):
    s, convd = xbc.shape
    ln, p, ns = SSM_CHUNK, SSM_HEAD_DIM, SSM_D_STATE
    nc, hg = s // ln, n_heads // n_groups

    def body(x_ref, dt_ref, alog_ref, d_ref, y_ref, prev_ref, st):
        @pl.when(pl.program_id(0) == 0)
        def _():
            st[...] = jnp.zeros(st.shape, F32)

        dt = dt_ref[...]
        _, acs, acs_t, tril = _ssd_common(dt, alog_ref[...], n_heads)
        e_all = jnp.exp(acs)
        last = acs[ln - 1:ln, :]
        ds_all = jnp.exp(last - acs)
        t_all = jnp.exp(last)
        dsk = d_ref[...]
        for g in range(n_groups):
            bg = x_ref[:, di + g * ns:di + (g + 1) * ns].astype(BF16)
            cg = x_ref[:, di + (n_groups + g) * ns:di + (n_groups + g + 1) * ns].astype(BF16)
            gm = _dot(cg, bg, _NT)
            for j in range(hg):
                h = g * hg + j
                xh = x_ref[:, h * p:(h + 1) * p]
                xdt = xh * dt[:, h:h + 1]
                seg = acs[:, h:h + 1] - acs_t[h:h + 1, :]
                m = jnp.where(tril, gm * jnp.exp(jnp.where(tril, seg, 0.0)), 0.0)
                prev = st[h]
                prev_ref[0, h] = prev
                y = _dot(m.astype(BF16), xdt.astype(BF16))
                y = y + _dot(cg, prev.astype(BF16), _NT) * e_all[:, h:h + 1]
                y = y + xh * dsk[:, h:h + 1]
                snew = _dot((xdt * ds_all[:, h:h + 1]).astype(BF16), bg, _TN)
                st[h] = prev * t_all[:, h:h + 1] + snew
                y_ref[:, h * p:(h + 1) * p] = y

    vec = pl.BlockSpec((1, LANES), lambda c: (0, 0))
    return pl.pallas_call(
        body, name=name, grid=(nc,),
        in_specs=[pl.BlockSpec((ln, convd), lambda c: (c, 0)), pl.BlockSpec((ln, LANES), lambda c: (c, 0)), vec, vec],
        out_specs=[pl.BlockSpec((ln, di), lambda c: (c, 0)),
                   pl.BlockSpec((1, n_heads, p, ns), lambda c: (c, 0, 0, 0))],
        out_shape=[jax.ShapeDtypeStruct((s, di), F32), jax.ShapeDtypeStruct((nc, n_heads, p, ns), F32)],
        scratch_shapes=[pltpu.VMEM((n_heads, p, ns), F32)],
        compiler_params=_params(("arbitrary",)))(xbc, dt, alog, dskip)


def _ssd_bwd(xbc, dt, alog, dskip, prev_all, dy, di, n_heads, n_groups, name):
    s, convd = xbc.shape
    ln, p, ns = SSM_CHUNK, SSM_HEAD_DIM, SSM_D_STATE
    nc, hg = s // ln, n_heads // n_groups

    def body(x_ref, dt_ref, alog_ref, d_ref, prev_ref, dy_ref, dx_ref, ddt_ref, da_ref, dd_ref, dh):
        step = pl.program_id(0)

        @pl.when(step == 0)
        def _():
            dh[...] = jnp.zeros(dh.shape, F32)
            da_ref[...] = jnp.zeros(da_ref.shape, F32)
            dd_ref[...] = jnp.zeros(dd_ref.shape, F32)

        dt = dt_ref[...]
        a, acs, acs_t, tril = _ssd_common(dt, alog_ref[...], n_heads)
        e_all = jnp.exp(acs)
        last = acs[ln - 1:ln, :]
        ds_all = jnp.exp(last - acs)
        t_all = jnp.exp(last)
        dsk = d_ref[...]
        lane = lax.broadcasted_iota(jnp.int32, (ln, LANES), 1)
        lane1 = lax.broadcasted_iota(jnp.int32, (1, LANES), 1)
        sub = lax.broadcasted_iota(jnp.int32, (LANES, ln), 0)
        rowi = lax.broadcasted_iota(jnp.int32, (ln, LANES), 0)
        dacs_c = jnp.zeros((ln, LANES), F32)
        dacs_r = jnp.zeros((LANES, ln), F32)
        dlast = jnp.zeros((1, LANES), F32)
        ddt_x = jnp.zeros((ln, LANES), F32)
        dd = jnp.zeros((1, LANES), F32)

        def tot(v):
            return jnp.sum(jnp.sum(v, axis=1, keepdims=True), axis=0, keepdims=True)

        for g in range(n_groups):
            bg = x_ref[:, di + g * ns:di + (g + 1) * ns].astype(BF16)
            cg = x_ref[:, di + (n_groups + g) * ns:di + (n_groups + g + 1) * ns].astype(BF16)
            gm = _dot(cg, bg, _NT)
            dgm = jnp.zeros((ln, ln), F32)
            dcg = jnp.zeros((ln, ns), F32)
            dbg = jnp.zeros((ln, ns), F32)
            for j in range(hg):
                h = g * hg + j
                xh = x_ref[:, h * p:(h + 1) * p]
                dth = dt[:, h:h + 1]
                xdt = xh * dth
                dyh = dy_ref[:, h * p:(h + 1) * p]
                eh, dsh, th = e_all[:, h:h + 1], ds_all[:, h:h + 1], t_all[:, h:h + 1]
                seg = acs[:, h:h + 1] - acs_t[h:h + 1, :]
                dec = jnp.where(tril, jnp.exp(jnp.where(tril, seg, 0.0)), 0.0)
                m = gm * dec
                prev = prev_ref[0, h]
                dhn = dh[h]
                prevb, dhb, dyb, xdtb = prev.astype(BF16), dhn.astype(BF16), dyh.astype(BF16), xdt.astype(BF16)
                yo = _dot(cg, prevb, _NT)
                dyob = (dyh * eh).astype(BF16)
                c_col = jnp.sum(dyh * yo, axis=1, keepdims=True) * eh
                dcg = dcg + _dot(dyob, prevb)
                dprev = th * dhn + _dot(dyob, cg, _TN)
                dtt = tot(dhn * prev)
                w = _dot(bg, dhb, _NT)
                dxdt = w * dsh
                dds = jnp.sum(w * xdt, axis=1, keepdims=True)
                dbg = dbg + _dot((xdt * dsh).astype(BF16), dhb)
                dm = _dot(dyb, xdtb, _NT)
                dxdt = dxdt + _dot(m.astype(BF16), dyb, _TN)
                dgm = dgm + dm * dec
                q = dm * m
                c_col = c_col + jnp.sum(q, axis=1, keepdims=True) - dds * dsh
                r_row = -jnp.sum(q, axis=0, keepdims=True)
                dlast_h = tot(dds * dsh) + dtt * th
                dacs_c = dacs_c + jnp.where(lane == h, c_col, 0.0)
                dacs_r = dacs_r + jnp.where(sub == h, r_row, 0.0)
                dlast = dlast + jnp.where(lane1 == h, dlast_h, 0.0)
                ddt_x = ddt_x + jnp.where(lane == h, jnp.sum(dxdt * xh, axis=1, keepdims=True), 0.0)
                dd = dd + jnp.where(lane1 == h, tot(dyh * xh), 0.0)
                dx_ref[:, h * p:(h + 1) * p] = dxdt * dth + dyh * dsk[:, h:h + 1]
                dh[h] = dprev
            dgb = dgm.astype(BF16)
            dx_ref[:, di + g * ns:di + (g + 1) * ns] = dbg + _dot(dgb, cg, _TN)
            dx_ref[:, di + (n_groups + g) * ns:di + (n_groups + g + 1) * ns] = dcg + _dot(dgb, bg)

        dacs = dacs_c + dacs_r.T + jnp.where(rowi == ln - 1, dlast, 0.0)
        row = lax.broadcasted_iota(jnp.int32, (ln, ln), 0)
        col = lax.broadcasted_iota(jnp.int32, (ln, ln), 1)
        dadt = _dot((col >= row).astype(F32), dacs, precision=HIGHEST)
        ddt_ref[...] = dadt * a + ddt_x
        da_ref[...] += jnp.sum(dadt * dt, axis=0, keepdims=True)
        dd_ref[...] += dd

        @pl.when(step == nc - 1)
        def _():
            da_ref[...] = da_ref[...] * a

    vec = pl.BlockSpec((1, LANES), lambda c: (0, 0))
    rev = lambda c: (nc - 1 - c, 0)
    return pl.pallas_call(
        body, name=name, grid=(nc,),
        in_specs=[pl.BlockSpec((ln, convd), rev), pl.BlockSpec((ln, LANES), rev), vec, vec,
                  pl.BlockSpec((1, n_heads, p, ns), lambda c: (nc - 1 - c, 0, 0, 0)), pl.BlockSpec((ln, di), rev)],
        out_specs=[pl.BlockSpec((ln, convd), rev), pl.BlockSpec((ln, LANES), rev), vec, vec],
        out_shape=[jax.ShapeDtypeStruct((s, convd), F32), jax.ShapeDtypeStruct((s, LANES), F32),
                   jax.ShapeDtypeStruct((1, LANES), F32), jax.ShapeDtypeStruct((1, LANES), F32)],
        scratch_shapes=[pltpu.VMEM((n_heads, p, ns), F32)],
        compiler_params=_params(("arbitrary",)))(xbc, dt, alog, dskip, prev_all, dy)


def _split(x, n):
    out = []
    for _ in range(n):
        piece = x.astype(BF16)
        out.append(piece)
        x = x - piece.astype(F32)
    return out


def _spread(x, onehot, n=2):
    return functools.reduce(jnp.add, [_dot(piece, onehot) for piece in _split(x, n)])


def _head_maps(di, p):
    e = (jnp.arange(di, dtype=jnp.int32)[None, :] // p == jnp.arange(LANES, dtype=jnp.int32)[:, None]).astype(BF16)
    return e, e.T


def _ssd_wide(dt, acs, acs_t, dskip, e_ref, et_ref):
    ln = dt.shape[0]
    last = acs[ln - 1:ln, :]
    stack = jnp.concatenate([dt, jnp.exp(acs), jnp.exp(last - acs), jnp.broadcast_to(dskip, (8, LANES))], axis=0)
    wide = _spread(stack, e_ref[...])
    tb = jnp.exp(jnp.broadcast_to(acs_t[:, ln - 1:ln], (LANES, LANES)))
    texp = functools.reduce(jnp.add, [_dot(et_ref[...], piece) for piece in _split(tb, 3)])
    return wide[:ln], wide[ln:2 * ln], wide[2 * ln:3 * ln], wide[3 * ln:3 * ln + 1], texp


def _ssd_fwd2(xbc, dt, alog, dskip, di, n_heads, n_groups, name):
    s, convd = xbc.shape
    ln, p, ns = SSM_CHUNK, SSM_HEAD_DIM, SSM_D_STATE
    nc, hg = s // ln, n_heads // n_groups
    gw = hg * p
    e64, e64t = _head_maps(di, p)

    def body(x_ref, dt_ref, alog_ref, d_ref, e_ref, et_ref, y_ref, prev_ref, st):
        @pl.when(pl.program_id(0) == 0)
        def _():
            st[...] = jnp.zeros(st.shape, F32)

        dt = dt_ref[...]
        _, acs, acs_t, tril = _ssd_common(dt, alog_ref[...], n_heads)
        dte, ee, dse, dske, texp = _ssd_wide(dt, acs, acs_t, d_ref[...], e_ref, et_ref)
        x = x_ref[:, :di]
        xdt = x * dte
        xdtb = xdt.astype(BF16)
        xdsb = (xdt * dse).astype(BF16)
        for g in range(n_groups):
            rows = slice(g * gw, (g + 1) * gw)
            bg = x_ref[:, di + g * ns:di + (g + 1) * ns].astype(BF16)
            cg = x_ref[:, di + (n_groups + g) * ns:di + (n_groups + g + 1) * ns].astype(BF16)
            gm = _dot(cg, bg, _NT)
            prev = st[rows, :]
            prev_ref[0, rows, :] = prev
            yo = _dot(cg, prev.astype(BF16), _NT)
            for j in range(hg):
                h = g * hg + j
                seg = acs[:, h:h + 1] - acs_t[h:h + 1, :]
                m = jnp.where(tril, gm * jnp.exp(jnp.where(tril, seg, 0.0)), 0.0)
                y_ref[:, h * p:(h + 1) * p] = _dot(m.astype(BF16), xdtb[:, h * p:(h + 1) * p])
            y_ref[:, rows] = y_ref[:, rows] + yo * ee[:, rows] + x[:, rows] * dske[:, rows]
            st[rows, :] = prev * texp[rows, :] + _dot(xdsb[:, rows], bg, _TN)

    vec = pl.BlockSpec((1, LANES), lambda c: (0, 0))
    return pl.pallas_call(
        body, name=name, grid=(nc,),
        in_specs=[pl.BlockSpec((ln, convd), lambda c: (c, 0)), pl.BlockSpec((ln, LANES), lambda c: (c, 0)), vec, vec,
                  pl.BlockSpec(e64.shape, lambda c: (0, 0)), pl.BlockSpec(e64t.shape, lambda c: (0, 0))],
        out_specs=[pl.BlockSpec((ln, di), lambda c: (c, 0)), pl.BlockSpec((1, di, ns), lambda c: (c, 0, 0))],
        out_shape=[jax.ShapeDtypeStruct((s, di), F32), jax.ShapeDtypeStruct((nc, di, ns), F32)],
        scratch_shapes=[pltpu.VMEM((di, ns), F32)],
        compiler_params=_params(("arbitrary",)))(xbc, dt, alog, dskip, e64, e64t)


def _ssd_bwd2(xbc, dt, alog, dskip, prev_all, dy, di, n_heads, n_groups, name):
    s, convd = xbc.shape
    ln, p, ns = SSM_CHUNK, SSM_HEAD_DIM, SSM_D_STATE
    nc, hg = s // ln, n_heads // n_groups
    gw = hg * p
    e64, e64t = _head_maps(di, p)

    def body(x_ref, dt_ref, alog_ref, d_ref, e_ref, et_ref, prev_ref, dy_ref,
             dx_ref, ddt_ref, da_ref, dd_ref, dh, yo_ref, w_ref):
        step = pl.program_id(0)

        @pl.when(step == 0)
        def _():
            dh[...] = jnp.zeros(dh.shape, F32)
            da_ref[...] = jnp.zeros(da_ref.shape, F32)
            dd_ref[...] = jnp.zeros(dd_ref.shape, F32)

        dt = dt_ref[...]
        a, acs, acs_t, tril = _ssd_common(dt, alog_ref[...], n_heads)
        dte, ee, dse, dske, texp = _ssd_wide(dt, acs, acs_t, d_ref[...], e_ref, et_ref)
        row = lax.broadcasted_iota(jnp.int32, (ln, ln), 0)
        col = lax.broadcasted_iota(jnp.int32, (ln, ln), 1)
        triu = col >= row
        x = x_ref[:, :di]
        dy = dy_ref[...]
        xdt = x * dte
        xdtb = xdt.astype(BF16)
        xdsb = (xdt * dse).astype(BF16)
        dyb = dy.astype(BF16)
        dyob = (dy * ee).astype(BF16)
        dhn = dh[...]
        dhb = dhn.astype(BF16)
        per_head = functools.reduce(jnp.add, [_dot(e_ref[...], piece) for piece in _split(dhn * prev_ref[0], 2)])
        ones8 = jnp.ones((8, LANES), BF16)
        dtt = functools.reduce(jnp.add, [_dot(ones8, piece, _NT) for piece in _split(per_head, 2)])[0:1]
        dacs_c = jnp.zeros((ln, LANES), F32)
        dacs_r = jnp.zeros((LANES, ln), F32)
        for g in range(n_groups):
            rows = slice(g * gw, (g + 1) * gw)
            bg = x_ref[:, di + g * ns:di + (g + 1) * ns].astype(BF16)
            cg = x_ref[:, di + (n_groups + g) * ns:di + (n_groups + g + 1) * ns].astype(BF16)
            gmt = _dot(bg, cg, _NT)
            prevb = prev_ref[0, rows, :].astype(BF16)
            dcg = _dot(dyob[:, rows], prevb)
            dh[rows, :] = texp[rows, :] * dhn[rows, :] + _dot(dyob[:, rows], cg, _TN)
            w = _dot(bg, dhb[rows, :], _NT)
            dbg = _dot(xdsb[:, rows], dhb[rows, :])
            yo_ref[:, rows] = _dot(cg, prevb, _NT)
            w_ref[:, rows] = w
            dgmt = jnp.zeros((ln, ln), F32)
            q_hi, q_lo = [], []
            for j in range(hg):
                h = g * hg + j
                segt = acs_t[h:h + 1, :] - acs[:, h:h + 1]
                dect = jnp.where(triu, jnp.exp(jnp.where(triu, segt, 0.0)), 0.0)
                dyh, xh = dyb[:, h * p:(h + 1) * p], xdtb[:, h * p:(h + 1) * p]
                mt = gmt * dect
                dmt = _dot(xh, dyh, _NT)
                dx_ref[:, h * p:(h + 1) * p] = _dot(mt.astype(BF16), dyh)
                dgmt = dgmt + dmt * dect
                hi, lo = _split(dmt * mt, 2)
                q_hi.append(hi)
                q_lo.append(lo)
            sel_c = (lax.broadcasted_iota(jnp.int32, (hg * ln, LANES), 1)
                     == g * hg + lax.broadcasted_iota(jnp.int32, (hg * ln, LANES), 0) // ln).astype(BF16)
            sel_r = (lax.broadcasted_iota(jnp.int32, (LANES, hg * ln), 0)
                     == g * hg + lax.broadcasted_iota(jnp.int32, (LANES, hg * ln), 1) // ln).astype(BF16)
            for pieces in (q_hi, q_lo):
                dacs_c = dacs_c - _dot(jnp.concatenate(pieces, axis=1), sel_c)
                dacs_r = dacs_r + _dot(sel_r, jnp.concatenate(pieces, axis=0))
            dgb = dgmt.astype(BF16)
            dx_ref[:, di + g * ns:di + (g + 1) * ns] = dbg + _dot(dgb, cg)
            dx_ref[:, di + (n_groups + g) * ns:di + (n_groups + g + 1) * ns] = dcg + _dot(dgb, bg, _TN)

        wds = w_ref[...] * dse
        dxdt = dx_ref[:, :di] + wds
        red = _spread(jnp.concatenate([dxdt * x, dy * yo_ref[...] * ee, xdt * wds, dy * x], axis=0), et_ref[...])
        ddt_x, r_off, r_state, ddr = red[:ln], red[ln:2 * ln], red[2 * ln:3 * ln], red[3 * ln:]
        dx_ref[:, :di] = dxdt * dte + dy * dske
        rowi = lax.broadcasted_iota(jnp.int32, (ln, LANES), 0)
        dlast = jnp.sum(r_state, axis=0, keepdims=True) + dtt * jnp.exp(acs[ln - 1:ln, :])
        dacs = r_off - r_state + dacs_c + dacs_r.T + jnp.where(rowi == ln - 1, dlast, 0.0)
        dadt = _dot(triu.astype(F32), dacs, precision=HIGHEST)
        ddt_ref[...] = dadt * a + ddt_x
        da_ref[...] += jnp.sum(dadt * dt, axis=0, keepdims=True)
        dd_ref[...] += jnp.sum(ddr, axis=0, keepdims=True)

        @pl.when(step == nc - 1)
        def _():
            da_ref[...] = da_ref[...] * a

    vec = pl.BlockSpec((1, LANES), lambda c: (0, 0))
    rev = lambda c: (nc - 1 - c, 0)
    return pl.pallas_call(
        body, name=name, grid=(nc,),
        in_specs=[pl.BlockSpec((ln, convd), rev), pl.BlockSpec((ln, LANES), rev), vec, vec,
                  pl.BlockSpec(e64.shape, lambda c: (0, 0)), pl.BlockSpec(e64t.shape, lambda c: (0, 0)),
                  pl.BlockSpec((1, di, ns), lambda c: (nc - 1 - c, 0, 0)), pl.BlockSpec((ln, di), rev)],
        out_specs=[pl.BlockSpec((ln, convd), rev), pl.BlockSpec((ln, LANES), rev), vec, vec],
        out_shape=[jax.ShapeDtypeStruct((s, convd), F32), jax.ShapeDtypeStruct((s, LANES), F32),
                   jax.ShapeDtypeStruct((1, LANES), F32), jax.ShapeDtypeStruct((1, LANES), F32)],
        scratch_shapes=[pltpu.VMEM((di, ns), F32), pltpu.VMEM((ln, di), F32), pltpu.VMEM((ln, di), F32)],
        compiler_params=_params(("arbitrary",)))(xbc, dt, alog, dskip, e64, e64t, prev_all, dy)


def _perm(a, d):
    if d == 1:
        return a
    s = a.shape[0]
    return a.reshape(s // d, d, -1).transpose(1, 0, 2).reshape(s, -1)


def _unperm(a, d):
    if d == 1:
        return a
    s = a.shape[0]
    return a.reshape(d, s // d, -1).transpose(1, 0, 2).reshape(s, -1)


def _rot_tables(s, d):
    half = ROPE_DIM // 2
    inv_freq = jnp.power(jnp.float32(ROPE_THETA), -jnp.arange(0, ROPE_DIM, 2, dtype=F32) / ROPE_DIM)
    v = jnp.arange(s, dtype=jnp.int32)
    pos = (v % (s // d)) * d + v // (s // d)
    ang = pos.astype(F32)[:, None] * inv_freq[None, :]
    cos, sin = jnp.cos(ang), jnp.sin(ang)
    zero = jnp.zeros((s, ATT_HEAD_DIM - ROPE_DIM), F32)
    cf = jnp.concatenate([cos, cos, jnp.ones_like(zero)], axis=1)
    s1 = jnp.concatenate([-sin, jnp.zeros_like(sin), zero], axis=1)
    s2 = jnp.concatenate([jnp.zeros_like(sin), sin, zero], axis=1)
    assert half * 2 == ROPE_DIM
    return cf, s1, s2


def _rot(x, tabs, sign):
    cf, s1, s2 = tabs
    half = ROPE_DIM // 2
    left = pltpu.roll(x, ATT_HEAD_DIM - half, 1)
    right = pltpu.roll(x, half, 1)
    return x * cf + sign * (left * s1 + right * s2)


def _att_masks(n, n_blk, rep):
    b = ATT_BLOCK
    row = lax.broadcasted_iota(jnp.int32, (rep * b, b), 0) & (b - 1)
    col = lax.broadcasted_iota(jnp.int32, (rep * b, b), 1)
    off = jnp.where(n % n_blk != 0, 0, 2 * b)
    return col <= row, col >= row + off


def _stack(x, rep):
    return jnp.concatenate([x[:, j * ATT_HEAD_DIM:(j + 1) * ATT_HEAD_DIM] for j in range(rep)], axis=0)


def _att_specs(nb, rep, cur, prv):
    b, hd = ATT_BLOCK, ATT_HEAD_DIM
    q_spec = pl.BlockSpec((b, rep * hd), lambda h, n: (cur(n), h))
    kc_spec = pl.BlockSpec((b, hd), lambda h, n: (cur(n), h))
    kp_spec = pl.BlockSpec((b, hd), lambda h, n: (prv(n), h))
    tc_spec = pl.BlockSpec((b, hd), lambda h, n: (cur(n), 0))
    tp_spec = pl.BlockSpec((b, hd), lambda h, n: (prv(n), 0))
    return q_spec, kc_spec, kp_spec, tc_spec, tp_spec


def _attn_fwd(q, k, v, tabs, n_blk, name):
    s = q.shape[0]
    b, hd = ATT_BLOCK, ATT_HEAD_DIM
    nb = s // b
    n_kv = ATT_KV_HEADS_PER_GROUP
    rep = ATT_HEADS_PER_GROUP // n_kv
    scale = hd ** -0.5

    def body(q_ref, kc_ref, kp_ref, vc_ref, vp_ref, cfc, s1c, s2c, cfp, s1p, s2p, o_ref, lse_ref):
        n = pl.program_id(1)
        tc = (cfc[...], s1c[...], s2c[...])
        tp = (cfp[...], s1p[...], s2p[...])
        qv = q_ref[...]
        q4 = jnp.concatenate([_rot(qv[:, j * hd:(j + 1) * hd], tc, 1.0) for j in range(rep)], axis=0).astype(BF16)
        kc = _rot(kc_ref[...], tc, 1.0).astype(BF16)
        kp = _rot(kp_ref[...], tp, 1.0).astype(BF16)
        mc, mp = _att_masks(n, n_blk, rep)
        sc = jnp.where(mc, _dot(q4, kc, _NT) * scale, NEG)
        sp = jnp.where(mp, _dot(q4, kp, _NT) * scale, NEG)
        m = jnp.maximum(jnp.max(sc, axis=1, keepdims=True), jnp.max(sp, axis=1, keepdims=True))
        pc, pp = jnp.exp(sc - m), jnp.exp(sp - m)
        l = jnp.sum(pc, axis=1, keepdims=True) + jnp.sum(pp, axis=1, keepdims=True)
        o = (_dot(pc.astype(BF16), vc_ref[...].astype(BF16)) + _dot(pp.astype(BF16), vp_ref[...].astype(BF16))) / l
        lse = jnp.broadcast_to(m + jnp.log(l), (rep * b, hd))
        for j in range(rep):
            o_ref[:, j * hd:(j + 1) * hd] = o[j * b:(j + 1) * b]
            lse_ref[:, j * hd:(j + 1) * hd] = lse[j * b:(j + 1) * b]

    cur = lambda n: n
    prv = lambda n: jnp.maximum(n - 1, 0)
    q_spec, kc_spec, kp_spec, tc_spec, tp_spec = _att_specs(nb, rep, cur, prv)
    return pl.pallas_call(
        body, name=name, grid=(n_kv, nb),
        in_specs=[q_spec, kc_spec, kp_spec, kc_spec, kp_spec, tc_spec, tc_spec, tc_spec, tp_spec, tp_spec, tp_spec],
        out_specs=[q_spec, q_spec],
        out_shape=[jax.ShapeDtypeStruct(q.shape, F32), jax.ShapeDtypeStruct(q.shape, F32)],
        compiler_params=_params(("parallel", "arbitrary")))(q, k, k, v, v, *tabs, *tabs)


def _attn_bwd(q, k, v, do, lse, delta, tabs, n_blk, name):
    s = q.shape[0]
    b, hd = ATT_BLOCK, ATT_HEAD_DIM
    nb = s // b
    n_kv = ATT_KV_HEADS_PER_GROUP
    rep = ATT_HEADS_PER_GROUP // n_kv
    scale = hd ** -0.5

    def body(q_ref, do_ref, lse_ref, dl_ref, kc_ref, kp_ref, vc_ref, vp_ref, cfc, s1c, s2c, cfp, s1p, s2p,
             dq_ref, dk_ref, dv_ref, ck, cv):
        n = pl.program_id(1)
        tp = (cfp[...], s1p[...], s2p[...])

        @pl.when(n == 0)
        def _():
            ck[...] = jnp.zeros(ck.shape, F32)
            cv[...] = jnp.zeros(cv.shape, F32)

        @pl.when(n < nb)
        def _():
            tc = (cfc[...], s1c[...], s2c[...])
            qv = q_ref[...]
            q4 = jnp.concatenate([_rot(qv[:, j * hd:(j + 1) * hd], tc, 1.0) for j in range(rep)],
                                 axis=0).astype(BF16)
            do4 = _stack(do_ref[...], rep).astype(BF16)
            lse4 = _stack(lse_ref[...], rep)
            dl4 = _stack(dl_ref[...], rep)
            kc = _rot(kc_ref[...], tc, 1.0).astype(BF16)
            kp = _rot(kp_ref[...], tp, 1.0).astype(BF16)
            vc, vp = vc_ref[...].astype(BF16), vp_ref[...].astype(BF16)
            mc, mp = _att_masks(n, n_blk, rep)
            pc = jnp.where(mc, jnp.exp(jnp.where(mc, _dot(q4, kc, _NT) * scale - lse4, 0.0)), 0.0)
            pp = jnp.where(mp, jnp.exp(jnp.where(mp, _dot(q4, kp, _NT) * scale - lse4, 0.0)), 0.0)
            dsc = (pc * (_dot(do4, vc, _NT) - dl4)).astype(BF16)
            dsp = (pp * (_dot(do4, vp, _NT) - dl4)).astype(BF16)
            dq4 = (_dot(dsc, kc) + _dot(dsp, kp)) * scale
            for j in range(rep):
                dq_ref[:, j * hd:(j + 1) * hd] = _rot(dq4[j * b:(j + 1) * b], tc, -1.0).astype(dq_ref.dtype)
            dk_prev = ck[...] + _dot(dsp, q4, _TN) * scale
            dv_prev = cv[...] + _dot(pp.astype(BF16), do4, _TN)
            dk_ref[...] = _rot(dk_prev, tp, -1.0).astype(dk_ref.dtype)
            dv_ref[...] = dv_prev.astype(dv_ref.dtype)
            ck[...] = _dot(dsc, q4, _TN) * scale
            cv[...] = _dot(pc.astype(BF16), do4, _TN)

        @pl.when(n == nb)
        def _():
            dk_ref[...] = _rot(ck[...], tp, -1.0).astype(dk_ref.dtype)
            dv_ref[...] = cv[...].astype(dv_ref.dtype)

    cur = lambda n: jnp.minimum(n, nb - 1)
    prv = lambda n: jnp.maximum(n - 1, 0)
    q_spec, kc_spec, kp_spec, tc_spec, tp_spec = _att_specs(nb, rep, cur, prv)
    return pl.pallas_call(
        body, name=name, grid=(n_kv, nb + 1),
        in_specs=[q_spec, q_spec, q_spec, q_spec, kc_spec, kp_spec, kc_spec, kp_spec,
                  tc_spec, tc_spec, tc_spec, tp_spec, tp_spec, tp_spec],
        out_specs=[q_spec, kp_spec, kp_spec],
        out_shape=[jax.ShapeDtypeStruct(q.shape, BF16), jax.ShapeDtypeStruct(k.shape, BF16),
                   jax.ShapeDtypeStruct(k.shape, BF16)],
        scratch_shapes=[pltpu.VMEM((b, hd), F32), pltpu.VMEM((b, hd), F32)],
        compiler_params=_params(("parallel", "arbitrary")))(q, do, lse, delta, k, k, v, v, *tabs, *tabs)


def _rot_heads(x, tabs, width, sign, out_dtype, name):
    s = x.shape[0]
    hd = ATT_HEAD_DIM
    tile = min(512, s)

    def body(x_ref, cf, s1, s2, o_ref):
        t = (cf[...], s1[...], s2[...])
        for j in range(width // hd):
            o_ref[:, j * hd:(j + 1) * hd] = _rot(x_ref[:, j * hd:(j + 1) * hd], t, sign).astype(o_ref.dtype)

    tab = pl.BlockSpec((tile, hd), lambda i: (i, 0))
    return pl.pallas_call(
        body, name=name, grid=(s // tile,), in_specs=[pl.BlockSpec((tile, width), lambda i: (i, 0)), tab, tab, tab],
        out_specs=pl.BlockSpec((tile, width), lambda i: (i, 0)), out_shape=jax.ShapeDtypeStruct((s, width), out_dtype),
        compiler_params=_params(("parallel",)))(x, *tabs)


def _rows_of(r, dil):
    return pl.ds(r, ATT_BLOCK, stride=dil) if dil > 1 else slice(None)


def _nat_specs(g, dil, n_kv_all, cur, prv):
    b, hd = ATT_BLOCK * dil, ATT_HEAD_DIM
    n_kv = ATT_KV_HEADS_PER_GROUP
    rep = ATT_HEADS_PER_GROUP // n_kv
    q_all = [pl.BlockSpec((b, hd), lambda h, n, j=j: (cur(n), (g * n_kv + h) * rep + j)) for j in range(rep)]
    q_own = [pl.BlockSpec((b, hd), lambda h, n, j=j: (cur(n), h * rep + j)) for j in range(rep)]
    hm_all = pl.BlockSpec((rep, b, hd), lambda h, n: (g * n_kv + h, cur(n), 0))
    hm_own = pl.BlockSpec((rep, b, hd), lambda h, n: (h, cur(n), 0))
    kc =pl.BlockSpec((b, hd), lambda h, n: (cur(n), g * n_kv + h))
    kp = pl.BlockSpec((b, hd), lambda h, n: (prv(n), g * n_kv + h))
    vc = pl.BlockSpec((b, hd), lambda h, n: (cur(n), n_kv_all + g * n_kv + h))
    vp = pl.BlockSpec((b, hd), lambda h, n: (prv(n), n_kv_all + g * n_kv + h))
    tab = pl.BlockSpec((b, hd), lambda h, n: (cur(n), 0))
    stat = pl.BlockSpec((None, b, LANES), lambda h, n: (h, cur(n), 0))
    return q_all, q_own, hm_all, hm_own, kc, kp, vc, vp, tab, stat


def _head_cols(stat, rep):
    return jnp.concatenate([jnp.broadcast_to(stat[:, j:j + 1], stat.shape) for j in range(rep)], axis=0)


def _attn_fwd_nat(q_all, k_rot, kv, tabs, g, dil, name):
    s = q_all.shape[0]
    b, hd = ATT_BLOCK, ATT_HEAD_DIM
    nbn = s // (b * dil)
    n_kv = ATT_KV_HEADS_PER_GROUP
    rep = ATT_HEADS_PER_GROUP // n_kv
    n_kv_all = k_rot.shape[1] // hd
    scale = hd ** -0.5

    def body(*refs):
        q_refs = refs[:rep]
        kc_ref, kp_ref, vc_ref, vp_ref, cf, s1, s2, o_ref, lse_ref = refs[rep:]
        mc, mp = _att_masks(jnp.where(pl.program_id(1) > 0, 1, 0), 2, rep)
        for r in range(dil):
            sl = _rows_of(r, dil)
            tc = (cf[sl, :], s1[sl, :], s2[sl, :])
            q4 = jnp.concatenate([_rot(q_ref[sl, :], tc, 1.0) for q_ref in q_refs], axis=0).astype(BF16)
            kc, kp = kc_ref[sl, :].astype(BF16), kp_ref[sl, :].astype(BF16)
            sc = jnp.where(mc, _dot(q4, kc, _NT) * scale, NEG)
            sp = jnp.where(mp, _dot(q4, kp, _NT) * scale, NEG)
            m = jnp.maximum(jnp.max(sc, axis=1, keepdims=True), jnp.max(sp, axis=1, keepdims=True))
            pc, pp = jnp.exp(sc - m), jnp.exp(sp - m)
            l = jnp.sum(pc, axis=1, keepdims=True) + jnp.sum(pp, axis=1, keepdims=True)
            o = (_dot(pc.astype(BF16), vc_ref[sl, :].astype(BF16))
                 + _dot(pp.astype(BF16), vp_ref[sl, :].astype(BF16))) / l
            lse = m + jnp.log(l)
            for j in range(rep):
                o_ref[j, sl, :] = o[j * b:(j + 1) * b]
            lse_ref[sl, :] = _lane_place([lse[j * b:(j + 1) * b] for j in range(rep)])

    cur = lambda n: n
    prv = lambda n: jnp.maximum(n - 1, 0)
    q_specs, _, _, hm_own, kc, kp, vc, vp, tab, stat = _nat_specs(g, dil, n_kv_all, cur, prv)
    return pl.pallas_call(
        body, name=name, grid=(n_kv, nbn), in_specs=[*q_specs, kc, kp, vc, vp, tab, tab, tab], out_specs=[hm_own, stat],
        out_shape=[jax.ShapeDtypeStruct((ATT_HEADS_PER_GROUP, s, hd), F32), jax.ShapeDtypeStruct((n_kv, s, LANES), F32)],
        compiler_params=_params(("parallel", "arbitrary")))(*([q_all] * rep), k_rot, k_rot, kv, kv, *tabs)


def _attn_bwd_nat(q_all, k_rot, kv, do, lse, delta, tabs, grads, g, dil, name):
    s = q_all.shape[0]
    b, hd = ATT_BLOCK, ATT_HEAD_DIM
    nbn = s // (b * dil)
    n_kv = ATT_KV_HEADS_PER_GROUP
    rep = ATT_HEADS_PER_GROUP // n_kv
    n_kv_all = k_rot.shape[1] // hd
    scale = hd ** -0.5

    def body(*refs):
        q_refs, do_refs = refs[:rep], refs[rep:2 * rep]
        (lse_ref, dl_ref, kc_ref, kp_ref, vc_ref, vp_ref, cf, s1, s2, _, _, _,
         dq_ref, dk_ref, dv_ref, ck, cv) = refs[2 * rep:]
        n = pl.program_id(1)

        @pl.when(n == 0)
        def _():
            ck[...] = jnp.zeros(ck.shape, F32)
            cv[...] = jnp.zeros(cv.shape, F32)

        @pl.when(n < nbn)
        def _():
            mc, mp = _att_masks(jnp.where(n > 0, 1, 0), 2, rep)
            for r in range(dil):
                sl = _rows_of(r, dil)
                own = slice(r * b, (r + 1) * b)
                tc = (cf[sl, :], s1[sl, :], s2[sl, :])
                q4 = jnp.concatenate([_rot(q_ref[sl, :], tc, 1.0) for q_ref in q_refs], axis=0).astype(BF16)
                do4 = jnp.concatenate([do_ref[sl, :] for do_ref in do_refs], axis=0).astype(BF16)
                lse4 = _head_cols(lse_ref[sl, :], rep)
                dl4 = _head_cols(dl_ref[sl, :], rep)
                kc, kp = kc_ref[sl, :].astype(BF16), kp_ref[sl, :].astype(BF16)
                vc, vp = vc_ref[sl, :].astype(BF16), vp_ref[sl, :].astype(BF16)
                pc = jnp.where(mc, jnp.exp(_dot(q4, kc, _NT) * scale - lse4), 0.0)
                pp = jnp.where(mp, jnp.exp(_dot(q4, kp, _NT) * scale - lse4), 0.0)
                dsc = (pc * (_dot(do4, vc, _NT) - dl4)).astype(BF16)
                dsp = (pp * (_dot(do4, vp, _NT) - dl4)).astype(BF16)
                dq4 = (_dot(dsc, kc) + _dot(dsp, kp)) * scale
                for j in range(rep):
                    dq_ref[j, sl, :] = _rot(dq4[j * b:(j + 1) * b], tc, -1.0)
                dk_ref[sl, :] = ck[own, :] + _dot(dsp, q4, _TN) * scale
                dv_ref[sl, :] = cv[own, :] + _dot(pp.astype(BF16), do4, _TN)
                ck[own, :] = _dot(dsc, q4, _TN) * scale
                cv[own, :] = _dot(pc.astype(BF16), do4, _TN)

        @pl.when(n == nbn)
        def _():
            for r in range(dil):
                sl = _rows_of(r, dil)
                dk_ref[sl, :] = ck[r * b:(r + 1) * b, :]
                dv_ref[sl, :] = cv[r * b:(r + 1) * b, :]

    cur = lambda n: jnp.minimum(n, nbn - 1)
    prv = lambda n: jnp.maximum(n - 1, 0)
    q_specs, do_specs, hm_all, _, kc, kp, vc, vp, tab, stat = _nat_specs(g, dil, n_kv_all, cur, prv)
    anyspace = pl.BlockSpec(memory_space=pl.ANY)
    n_in = 2 * rep + 9
    return pl.pallas_call(
        body, name=name, grid=(n_kv, nbn + 1),
        in_specs=[*q_specs, *do_specs, stat, stat, kc, kp, vc, vp, tab, tab, tab, anyspace, anyspace, anyspace],
        out_specs=[hm_all, kp, kp], out_shape=[jax.ShapeDtypeStruct(a.shape, a.dtype) for a in grads],
        input_output_aliases={n_in: 0, n_in + 1: 1, n_in + 2: 2},
        scratch_shapes=[pltpu.VMEM((dil * b, hd), F32), pltpu.VMEM((dil * b, hd), F32)],
        compiler_params=_params(("parallel", "arbitrary"), VMEM_LIMIT_ATTN_BWD_BYTES))(
            *([q_all] * rep), *([do] * rep), lse, delta, k_rot, k_rot, kv, kv, *tabs, *grads)


def _adamw(g_slabs, w, m, v, name):
    kk, r, c = g_slabs.shape
    tile = r if r <= 256 else _pick_rows(r, 256)

    def body(g_ref, w_ref, m_ref, v_ref, go_ref, d_ref, mo_ref, vo_ref):
        g = g_ref[0].astype(F32)
        for k in range(1, kk):
            g = g + g_ref[k].astype(F32)
        m2 = ADAM_B1 * m_ref[...] + (1.0 - ADAM_B1) * g
        v2 = ADAM_B2 * v_ref[...] + (1.0 - ADAM_B2) * jnp.square(g)
        m_hat = m2 / (1.0 - ADAM_B1 ** ADAM_STEP)
        v_hat = v2 / (1.0 - ADAM_B2 ** ADAM_STEP)
        go_ref[...] = g
        d_ref[...] = -ADAM_LR * (m_hat / (jnp.sqrt(v_hat) + ADAM_EPS) + ADAM_WD * w_ref[...])
        mo_ref[...] = m2
        vo_ref[...] = v2

    spec = pl.BlockSpec((tile, c), lambda i: (i, 0))
    return pl.pallas_call(
        body, name=name, grid=(r // tile,), in_specs=[pl.BlockSpec((kk, tile, c), lambda i: (0, i, 0)), spec, spec, spec],
        out_specs=[spec] * 4, out_shape=[jax.ShapeDtypeStruct((r, c), F32)] * 4,
        compiler_params=_params(("parallel",)))(g_slabs, w, m, v)


def _pick_rows(r, pref):
    t = (pref // 16) * 16
    while t >= 16:
        if r % t == 0:
            return t
        t -= 16
    return r


def _coords():
    return lax.axis_index("x"), lax.axis_index("y"), lax.axis_index("c")


def _dev_index(px, py, pc):
    return 4 * px + 2 * py + pc


def _all_gather(shards, name):
    na = len(shards)

    def body(*refs):
        ins, outs = refs[:na], refs[na:2 * na]
        send_sems, recv_sems, local_sems = refs[2 * na:]
        x, y, c = _coords()
        me, sibling = (x, y, c), (x, y, 1 - c)
        chips = [(1 - x, y), (x, 1 - y), (1 - x, 1 - y)]

        def copy(a, k, block, to, src=None):
            dst = outs[a].at[_dev_index(*block)]
            return pltpu.make_async_remote_copy(
                src_ref=dst if src is None else src, dst_ref=dst, send_sem=send_sems.at[a * 7 + k],
                recv_sem=recv_sems.at[a * 7 + k], device_id=to, device_id_type=MESH)

        mine = [pltpu.make_async_copy(ins[a], outs[a].at[_dev_index(*me)], local_sems.at[a]) for a in range(na)]
        for cp in mine:
            cp.start()
        first = []
        for a in range(na):
            first.append(copy(a, 0, me, sibling, src=ins[a]))
            first += [copy(a, 1 + j, me, (*chip, c), src=ins[a]) for j, chip in enumerate(chips)]
        for cp in first:
            cp.start()
        passed = []
        for j, chip in enumerate(chips):
            for a in range(na):
                copy(a, 1 + j, (*chip, c), me).wait_recv()
                cp = copy(a, 4 + j, (*chip, c), sibling)
                cp.start()
                passed.append(cp)
        for a in range(na):
            copy(a, 0, sibling, me).wait_recv()
            for j, chip in enumerate(chips):
                copy(a, 4 + j, (*chip, 1 - c), me).wait_recv()
        for cp in first + passed:
            cp.wait_send()
        for cp in mine:
            cp.wait()

    hbm = pl.BlockSpec(memory_space=pl.ANY)
    return pl.pallas_call(
        body, name=name, in_specs=[hbm] * na, out_specs=[hbm] * na,
        out_shape=[jax.ShapeDtypeStruct((NDEV,) + s.shape, s.dtype) for s in shards],
        scratch_shapes=[pltpu.SemaphoreType.DMA((7 * na,)), pltpu.SemaphoreType.DMA((7 * na,)),
                        pltpu.SemaphoreType.DMA((na,))])(*shards)


def _exchange(slabs, whole, name):
    ns, nw = len(slabs), len(whole)
    na = ns + nw

    def body(*refs):
        ins, outs = refs[:na], refs[na:2 * na]
        send_sems, recv_sems, local_sems = refs[2 * na:]
        x, y, c = _coords()
        me = _dev_index(x, y, c)

        def src_of(a, p):
            return ins[a].at[p] if a < ns else ins[a]

        def copy(a, k, peer):
            p = _dev_index(*peer)
            return pltpu.make_async_remote_copy(
                src_ref=src_of(a, p), dst_ref=outs[a].at[me], send_sem=send_sems.at[a * 7 + k - 1],
                recv_sem=recv_sems.at[a * 7 + k - 1], device_id=peer, device_id_type=MESH)

        def arrival(a, k, peer):
            p = _dev_index(*peer)
            return pltpu.make_async_remote_copy(
                src_ref=src_of(a, p), dst_ref=outs[a].at[p], send_sem=send_sems.at[a * 7 + k - 1],
                recv_sem=recv_sems.at[a * 7 + k - 1], device_id=peer, device_id_type=MESH)

        mine = [pltpu.make_async_copy(src_of(a, me), outs[a].at[me], local_sems.at[a]) for a in range(na)]
        for cp in mine:
            cp.start()
        peers = [(k, (x ^ (k >> 2), y ^ ((k >> 1) & 1), c ^ (k & 1))) for k in range(1, NDEV)]
        sent = [copy(a, k, peer) for k, peer in peers for a in range(na)]
        for cp in sent:
            cp.start()
        for k, peer in peers:
            for a in range(na):
                arrival(a, k, peer).wait_recv()
        for cp in sent:
            cp.wait_send()
        for cp in mine:
            cp.wait()

    hbm = pl.BlockSpec(memory_space=pl.ANY)
    out_shape = [jax.ShapeDtypeStruct(s.shape, s.dtype) for s in slabs]
    out_shape += [jax.ShapeDtypeStruct((NDEV,) + w.shape, w.dtype) for w in whole]
    return pl.pallas_call(
        body, name=name, in_specs=[hbm] * na, out_specs=[hbm] * na, out_shape=out_shape,
        scratch_shapes=[pltpu.SemaphoreType.DMA((7 * na,)), pltpu.SemaphoreType.DMA((7 * na,)),
                        pltpu.SemaphoreType.DMA((na,))])(*slabs, *whole)


_HBM = pl.BlockSpec(memory_space=pltpu.HBM)
_SEM = pl.BlockSpec(memory_space=pltpu.SEMAPHORE)
_EFFECT = pltpu.SideEffectType.DATAFLOW_SIDE_EFFECTING


def _peers(x, y, c):
    return [(k, (x ^ (k >> 2), y ^ ((k >> 1) & 1), c ^ (k & 1))) for k in range(1, NDEV)]


def _peer_copy(src, land, send_sems, recv_sems, a, k, dst_block, peer):
    return pltpu.make_async_remote_copy(
        src_ref=src, dst_ref=land.at[dst_block], send_sem=send_sems.at[a * 7 + k - 1],
        recv_sem=recv_sems.at[a * 7 + k - 1], device_id=peer, device_id_type=MESH)


def _send_start(arrays, slabs, name):
    na = len(arrays)
    lands = [jax.ShapeDtypeStruct(a.shape if slabs else (NDEV,) + a.shape, a.dtype) for a in arrays]

    def body(*refs):
        ins, zones = refs[:na], refs[na:2 * na]
        send_sems, recv_sems = refs[2 * na], refs[2 * na + 1]
        token = refs[-1]
        x, y, c = _coords()
        me = _dev_index(x, y, c)
        for k, peer in _peers(x, y, c):
            for a in range(na):
                src = ins[a].at[_dev_index(*peer)] if slabs else ins[a]
                _peer_copy(src, zones[a], send_sems, recv_sems, a, k, me, peer).start()
        token[...] = jnp.zeros_like(token)

    outs = pl.pallas_call(
        body, name=name,
        out_shape=(pltpu.SemaphoreType.DMA((7 * na,)), pltpu.SemaphoreType.DMA((7 * na,)),
                   *[pltpu.HBM(a.shape, a.dtype) for a in arrays], *[pltpu.HBM(l.shape, l.dtype) for l in lands],
                   jax.ShapeDtypeStruct((8, LANES), F32)),
        in_specs=[_HBM] * (2 * na), out_specs=(_SEM, _SEM, *([_HBM] * (2 * na)), pl.BlockSpec(memory_space=pltpu.VMEM)),
        input_output_aliases={i: 2 + i for i in range(2 * na)},
        compiler_params=pltpu.CompilerParams(has_side_effects=_EFFECT),
    )(*[pltpu.with_memory_space_constraint(a, pltpu.HBM) for a in arrays],
      *[pltpu.with_memory_space_constraint(lax.empty(l.shape, l.dtype), pltpu.HBM) for l in lands])
    return outs[0], outs[1], list(outs[2:2 + na]), list(outs[2 + na:2 + 2 * na]), outs[-1]


def _send_wait(started, after, slabs, name):
    send_sems, recv_sems, thru, zones, _ = started
    na = len(thru)

    def body(*refs):
        ins, lands = refs[:na], refs[na:2 * na]
        s_sems, r_sems = refs[2 * na], refs[2 * na + 1]
        x, y, c = _coords()
        for k, peer in _peers(x, y, c):
            p = _dev_index(*peer)
            for a in range(na):
                src = ins[a].at[p] if slabs else ins[a]
                cp = _peer_copy(src, lands[a], s_sems, r_sems, a, k, p, peer)
                cp.wait_send()
                cp.wait_recv()

    outs = pl.pallas_call(
        body, name=name, out_shape=tuple(pltpu.HBM(v.shape, v.dtype) for v in thru + zones),
        in_specs=[_HBM] * (2 * na) + [_SEM, _SEM, pl.BlockSpec(memory_space=pl.ANY)], out_specs=tuple([_HBM] * (2 * na)),
        input_output_aliases={i: i for i in range(2 * na)},
        compiler_params=pltpu.CompilerParams(has_side_effects=_EFFECT),
    )(*thru, *zones, send_sems, recv_sems, after)
    me = _dev_index(*_coords())
    filled = []
    for a in range(na):
        own = lax.dynamic_index_in_dim(outs[a], me, 0, keepdims=False) if slabs else outs[a]
        filled.append(lax.dynamic_update_index_in_dim(outs[na + a], own, me, 0))
    return filled


def _pack(vecs):
    parts, spans, off = [], [], 0
    for v in vecs:
        n = v.size
        pad = (-n) % LANES
        parts.append(jnp.pad(v.reshape(-1).astype(F32), (0, pad)))
        spans.append((off, n))
        off += n + pad
    return jnp.concatenate(parts).reshape(-1, LANES), spans


def _pad_lanes(v):
    v = v.reshape(1, -1)
    return jnp.pad(v, ((0, 0), (0, LANES - v.shape[1])))


def _cols_to_slabs(g):
    sh = g.shape
    g = g.reshape(sh[:-1] + (NDEV, sh[-1] // NDEV))
    return jnp.moveaxis(g, -2, 0)


def _rows_to_slabs(g):
    sh = g.shape
    g = g.reshape(sh[:-2] + (NDEV, sh[-2] // NDEV, sh[-1]))
    return jnp.moveaxis(g, -3, 0)


def _slabs_to_cols(a):
    a = jnp.moveaxis(a, 0, -2)
    return a.reshape(a.shape[:-2] + (a.shape[-2] * a.shape[-1],))


def _slabs_to_rows(a):
    a = jnp.moveaxis(a, 0, -3)
    return a.reshape(a.shape[:-3] + (a.shape[-3] * a.shape[-2], a.shape[-1]))


def _ffn_forward(x, norm_w, wup_g, wup_v, cw_g, cw_v, wdown, tag):
    h = _rms_fwd(x, norm_w, f"{tag}_norm")
    ug = _mm(h, wup_g, name=f"{tag}_up_gate")
    uv = _mm(h, wup_v, name=f"{tag}_up_val")
    f = _ffn_gate_fwd(ug, uv, cw_g, cw_v, f"{tag}_gate")
    return _mm(f, wdown, res=x, name=f"{tag}_down"), (h, ug, uv, f)


def _ffn_backward(x, saved, dout, dout_b, norm_w, wup_g, wup_v, cw_g, cw_v, wdown, tag):
    h, ug, uv, f = saved
    dwdown = _mm(f, dout_b, ta=True, name=f"{tag}_dwdown")
    df = _mm(dout_b, wdown, tb=True, name=f"{tag}_df")
    dug, duv, dcg, dcv = _ffn_gate_bwd(ug, uv, cw_g, cw_v, df, f"{tag}_gate_bwd")
    dwg = _mm(h, dug, ta=True, name=f"{tag}_dwup_gate")
    dwv = _mm(h, duv, ta=True, name=f"{tag}_dwup_val")
    dh = _mm(dug, wup_g, tb=True, name=f"{tag}_dh_gate")
    dh = _mm(duv, wup_v, tb=True, res=dh, name=f"{tag}_dh_val")
    dx, dxb, dnorm = _rms_bwd(x, norm_w, dh, dout, f"{tag}_norm_bwd")
    return dx, dxb, (jnp.concatenate([dwg, dwv], axis=1), jnp.concatenate([dcg, dcv], axis=1), dwdown, dnorm)


def kernel(x, a_norm, ssm_w_in, ssm_conv_w, ssm_conv_b, ssm_dt_bias, ssm_a_log, ssm_d, ssm_norm, ssm_w_out, kv_norm, w_kv, b_norm, att_w_q, att_w_o, ffn_norm, ffn_w_up, ffn_conv_w, ffn_w_down, final_norm, loss_target, m_a_norm, m_ssm_w_in, m_ssm_conv_w, m_ssm_conv_b, m_ssm_dt_bias, m_ssm_a_log, m_ssm_d, m_ssm_norm, m_ssm_w_out, m_kv_norm, m_w_kv, m_b_norm, m_att_w_q, m_att_w_o, m_ffn_norm, m_ffn_w_up, m_ffn_conv_w, m_ffn_w_down, m_final_norm, v_a_norm, v_ssm_w_in, v_ssm_conv_w, v_ssm_conv_b, v_ssm_dt_bias, v_ssm_a_log, v_ssm_d, v_ssm_norm, v_ssm_w_out, v_kv_norm, v_w_kv, v_b_norm, v_att_w_q, v_att_w_o, v_ffn_norm, v_ffn_w_up, v_ffn_conv_w, v_ffn_w_down, v_final_norm):
    given = dict(locals())
    xs, tgt = x[0], loss_target[0]
    s, d = xs.shape
    di = ssm_w_out.shape[1] * NDEV
    nh = ssm_dt_bias.shape[1]
    ng = SSM_N_GROUPS
    convd = di + 2 * ng * SSM_D_STATE
    f = ffn_w_down.shape[1] * NDEV
    n_att = len(ATT_PATTERNS)
    qg = ATT_HEADS_PER_GROUP * ATT_HEAD_DIM
    kg = ATT_KV_HEADS_PER_GROUP * ATT_HEAD_DIM
    kvd = n_att * kg
    assert all(w // dil == ATT_BLOCK for w, dil in ATT_PATTERNS)

    small, _ = _pack([a_norm, ssm_conv_w, ssm_conv_b, ssm_norm, ffn_conv_w])
    gat = _all_gather([ssm_w_in[0].astype(BF16), small], "gather_weights")
    first = _send_start([ssm_w_out[0].astype(BF16), ffn_w_up[0].astype(BF16), ffn_w_down[0].astype(BF16)], False,
                        "gather_ffn0_start")
    rest = _send_start([b.astype(BF16) for b in (w_kv, att_w_q[0], att_w_o[0], ffn_w_up[1], ffn_w_down[1])], False,
                       "gather_rest_start")
    w_in = _slabs_to_cols(gat[0])
    w_z, w_xbc = w_in[:, :di], w_in[:, di:di + convd]
    w_dt = jnp.pad(w_in[:, di + convd:], ((0, 0), (0, LANES - nh)))
    sm = gat[1].reshape(NDEV, -1)
    o0 = 0

    def take(shape):
        nonlocal o0
        n = math.prod(shape)
        out = sm[:, o0:o0 + n].reshape((NDEV,) + shape)
        o0 += n + (-n) % LANES
        return out
    a_norm_f = _slabs_to_cols(take(a_norm.shape)) + (first[-1][0, 0] + rest[-1][0, 0])
    conv_w_f = _slabs_to_cols(take(ssm_conv_w.shape))[0]
    conv_b_f = _slabs_to_cols(take(ssm_conv_b.shape))
    ssm_norm_f = _slabs_to_cols(take(ssm_norm.shape))
    fcw = _slabs_to_cols(take(ffn_conv_w.shape))
    fcw_g, fcw_v = fcw[:, :, :f], fcw[:, :, f:]
    dtb, alog, dsk = _pad_lanes(ssm_dt_bias), _pad_lanes(ssm_a_log), _pad_lanes(ssm_d)
    kvn, fin = kv_norm.reshape(1, d), final_norm.reshape(1, d)

    h0 = _rms_fwd(xs, a_norm_f, "a_norm")
    z = _mm(h0, w_z, name="in_z")
    xbc_pre = _mm(h0, w_xbc, name="in_xbc")
    dtr = _mm(h0, w_dt, name="in_dt")
    xbc = _conv_silu_fwd(xbc_pre, conv_w_f, conv_b_f, "ssm_conv")
    dt = _softplus_fwd(dtr, dtb, "ssm_dt")
    y, prevs = _ssd_fwd2(xbc, dt, alog, dsk, di, nh, ng, "ssd")
    yn = _gnorm_fwd(y, z, ssm_norm_f, ng, "ssm_gnorm")
    got = _send_wait(first, yn, False, "gather_ffn0_wait")
    w_out = _slabs_to_rows(got[0])
    w_up0, w_down0 = _slabs_to_cols(got[1]), _slabs_to_rows(got[2])
    x1 = _mm(yn, w_out, res=xs, name="ssm_out")
    x2, ffn0 = _ffn_forward(x1, ffn_norm[0:1], w_up0[:, :f], w_up0[:, f:], fcw_g[0], fcw_v[0], w_down0, "ffn0")
    got = _send_wait(rest, x2, False, "gather_rest_wait")
    w_kvf = _slabs_to_cols(got[0])
    w_q = _slabs_to_cols(got[1])
    w_o = _slabs_to_rows(got[2])
    w_up1, w_down1 = _slabs_to_cols(got[3]), _slabs_to_rows(got[4])
    w_up_g, w_up_v = (w_up0[:, :f], w_up1[:, :f]), (w_up0[:, f:], w_up1[:, f:])
    w_down = (w_down0, w_down1)
    hk = _rms_fwd(x2, kvn, "kv_norm")
    kv = _mm(hk, w_kvf, name="kv_proj")
    h2 = _rms_fwd(x2, b_norm, "b_norm")
    q = _mm(h2, w_q, name="q_proj")
    tabs = _rot_tables(s, 1)
    k_rot = _rot_heads(kv, tabs, kvd, 1.0, F32, "k_rot")
    att = [_attn_fwd_nat(q, k_rot, kv, tabs, g, dil, f"attn{g}") for g, (_, dil) in enumerate(ATT_PATTERNS)]
    o, ob, lse = _merge_heads([t[0] for t in att], [t[1] for t in att], "attn_merge")
    x3 = _mm(ob, w_o, res=x2, name="attn_out")
    x4, ffn1 = _ffn_forward(x3, ffn_norm[1:2], w_up_g[1], w_up_v[1], fcw_g[1], fcw_v[1], w_down[1], "ffn1")
    loss_part, dx4, dx4b, dfin = _final_loss(x4, fin, tgt, "loss_head")

    dx3, dx3b, (dwup1, dfc1, dwdown1, dfn1) = _ffn_backward(
        x3, ffn1, dx4, dx4b, ffn_norm[1:2], w_up_g[1], w_up_v[1], fcw_g[1], fcw_v[1], w_down[1], "ffn1")
    dw_o = _mm(ob, dx3b, ta=True, name="attn_dwo")
    do = _mm(dx3b, w_o, tb=True, name="attn_do")
    delta = _delta_heads(do, o, "attn_delta")
    grads = (lax.empty((n_att * qg // LANES, s, LANES), F32), lax.empty((s, kvd), F32), lax.empty((s, kvd), F32))
    for g, (_, dil) in enumerate(ATT_PATTERNS):
        grads = _attn_bwd_nat(q, k_rot, kv, do, lse, delta, tabs, grads, g, dil, f"attn{g}_bwd")
    dq, dk_rot, dv = grads
    dk = _rot_heads(dk_rot, tabs, kvd, -1.0, BF16, "k_rot_bwd")
    dw_q = _mm(h2, dq, ta=True, b_heads=True, name="q_dw")
    dh2 = _mm(dq, w_q, tb=True, a_heads=True, name="q_dh")
    dw_kv = jnp.concatenate([_mm(hk, dk, ta=True, name="k_dw"), _mm(hk, dv, ta=True, name="v_dw")], axis=1)
    dhk = _mm(dk, w_kvf[:, :kvd], tb=True, name="k_dh")
    dhk = _mm(dv, w_kvf[:, kvd:], tb=True, res=dhk, name="v_dh")
    dx2, _, db_norm = _rms_bwd(x2, b_norm, dh2, dx3, "b_norm_bwd")
    dx2, dx2b, dkv_norm = _rms_bwd(x2, kvn, dhk, dx2, "kv_norm_bwd")
    sent1 = _send_start([_cols_to_slabs(dwup1).astype(BF16), _rows_to_slabs(dwdown1).astype(BF16),
                         _cols_to_slabs(dw_kv).astype(BF16), _cols_to_slabs(dw_q).astype(BF16),
                         _rows_to_slabs(dw_o).astype(BF16)], True, "grads_late_start")
    dx1, dx1b, (dwup0, dfc0, dwdown0, dfn0) = _ffn_backward(
        x1, ffn0, dx2, dx2b, ffn_norm[0:1], w_up_g[0], w_up_v[0], fcw_g[0] + sent1[-1][0, 0], fcw_v[0], w_down[0],
        "ffn0")
    dw_out = _mm(yn, dx1b, ta=True, name="ssm_dwout")
    sent0 = _send_start([_cols_to_slabs(dwup0).astype(BF16), _rows_to_slabs(dwdown0).astype(BF16),
                         _rows_to_slabs(dw_out).astype(BF16)], True, "grads_ffn0_start")
    dyn = _mm(dx1b, w_out, tb=True, name="ssm_dyn")
    dy, dz, dssm_norm = _gnorm_bwd(dyn, y, z, ssm_norm_f + sent0[-1][0, 0], ng, "ssm_gnorm_bwd")
    dxbc, ddt, dalog, ddsk = _ssd_bwd2(xbc, dt, alog, dsk, prevs, dy, di, nh, ng, "ssd_bwd")
    ddtr, ddtb = _softplus_bwd(ddt, dtr, dtb, nh, "ssm_dt_bwd")
    dxbc_pre, dconv_w, dconv_b = _conv_silu_bwd(xbc_pre, conv_w_f, conv_b_f, dxbc, "ssm_conv_bwd")
    dw_z = _mm(h0, dz, ta=True, name="in_dwz")
    dw_xbc = _mm(h0, dxbc_pre, ta=True, name="in_dwxbc")
    dw_dt = _mm(h0, ddtr, ta=True, name="in_dwdt")[:, :nh]
    sent_m = _send_start([_cols_to_slabs(jnp.concatenate([dw_z, dw_xbc, dw_dt], axis=1)).astype(BF16)], True,
                         "grads_mamba_start")
    dh0 = _mm(dz, w_z, tb=True, name="in_dh_z")
    dh0 = _mm(dxbc_pre, w_xbc, tb=True, res=dh0, name="in_dh_xbc")
    dh0 = _mm(ddtr, w_dt, tb=True, res=dh0, name="in_dh_dt")
    dx0, _, da_norm = _rms_bwd(xs, a_norm_f + sent_m[-1][0, 0], dh0, dx1, "a_norm_bwd")

    small_full = {
        'a_norm': da_norm, 'ssm_conv_w': dconv_w[None], 'ssm_conv_b': dconv_b, 'ssm_dt_bias': ddtb[:, :nh],
        'ssm_a_log': dalog[:, :nh], 'ssm_d': ddsk[:, :nh], 'ssm_norm': dssm_norm, 'kv_norm': dkv_norm.reshape(d),
        'b_norm': db_norm, 'ffn_norm': jnp.concatenate([dfn0, dfn1], axis=0), 'ffn_conv_w': jnp.stack([dfc0, dfc1]),
        'final_norm': dfin.reshape(d),
    }
    small_names = list(small_full)
    packed, spans = _pack([small_full[n] for n in small_names])
    recv = _exchange([], [packed], "exchange_grads")
    small_sum = _sum_slabs(recv[-1], "sum_small_grads").reshape(-1)
    got1 = _send_wait(sent1, recv[-1], True, "grads_late_wait")
    got0 = _send_wait(sent0, recv[-1], True, "grads_ffn0_wait")
    recv_big = {
        'w_kv': got1[2], 'att_w_q': got1[3], 'att_w_o': got1[4],
        'ffn_w_up': jnp.concatenate([got0[0], got1[0]], axis=1),
        'ffn_w_down': jnp.concatenate([got0[1], got1[1]], axis=1),
    }

    me = _dev_index(*_coords())
    res = {}

    def update_big(n, r):
        w = given[n]
        c = w.shape[-1]
        outs = _adamw(r.reshape(NDEV, -1, c), w.reshape(-1, c), given['m_' + n].reshape(-1, c),
                      given['v_' + n].reshape(-1, c), f"adamw_{n}")
        res[n] = [o_.reshape(w.shape) for o_ in outs]
    for n, r in recv_big.items():
        update_big(n, r)
    update_big('ssm_w_out', got0[2])
    gotm = _send_wait(sent_m, res['ssm_w_out'][1], True, "grads_mamba_wait")
    update_big('ssm_w_in', gotm[0])
    sharded_small = {'a_norm', 'ssm_conv_w', 'ssm_conv_b', 'ssm_norm', 'ffn_conv_w'}
    for n, (off, size) in zip(small_names, spans):
        w = given[n]
        gfull = small_sum[off:off + size].reshape(small_full[n].shape)
        if n in sharded_small:
            c = w.shape[-1]
            gfull = lax.dynamic_slice_in_dim(gfull, me * c, c, axis=gfull.ndim - 1)
        c = w.shape[-1]
        outs = _adamw(gfull.reshape(1, -1, c), w.reshape(-1, c), given['m_' + n].reshape(-1, c),
                      given['v_' + n].reshape(-1, c), f"adamw_{n}")
        res[n] = [o_.reshape(w.shape) for o_ in outs]

    loss = lax.psum(loss_part[0, 0], AXES)
    return (loss, dx0[None], *[res[n][0] for n in WEIGHTS], *[res[n][1] for n in WEIGHTS],
            *[res[n][2] for n in WEIGHTS], *[res[n][3] for n in WEIGHTS])
```

```python
import functools
import math

import jax
import jax.numpy as jnp
from jax import lax
from jax.experimental import pallas as pl
from jax.experimental.pallas import tpu as pltpu

F32, BF16 = jnp.float32, jnp.bfloat16
AXES = ("x", "y", "c")
NDEV = 8
MESH = pl.DeviceIdType.MESH
HIGHEST = lax.Precision.HIGHEST

LANES = 128
SUBLANES = 8
VMEM_LIMIT_BYTES = 48 * 1024 * 1024
VMEM_LIMIT_ATTN_BWD_BYTES = 58 * 1024 * 1024

RMS_EPS = 1e-6
GATED_NORM_EPS = 1e-5
SSM_HEAD_DIM = 64
SSM_N_GROUPS = 8
SSM_D_STATE = 128
SSM_CONV = 4
SSM_CHUNK = 128
ATT_PATTERNS = ((128, 1), (512, 4), (2048, 16))
ATT_HEAD_DIM = 128
ATT_HEADS_PER_GROUP = 8
ATT_KV_HEADS_PER_GROUP = 2
ATT_BLOCK = 128
ROPE_DIM = ATT_HEAD_DIM // 4
ROPE_THETA = 500000.0
FFN_CONV = 3
ADAM_LR = 0.001
ADAM_B1 = 0.9
ADAM_B2 = 0.999
ADAM_EPS = 1e-08
ADAM_WD = 0.01
ADAM_STEP = 10
NEG = -1e30

WEIGHTS = ['a_norm', 'ssm_w_in', 'ssm_conv_w', 'ssm_conv_b', 'ssm_dt_bias', 'ssm_a_log', 'ssm_d', 'ssm_norm',
           'ssm_w_out', 'kv_norm', 'w_kv', 'b_norm', 'att_w_q', 'att_w_o', 'ffn_norm', 'ffn_w_up', 'ffn_conv_w',
           'ffn_w_down', 'final_norm']


def _params(sem=None, vmem=VMEM_LIMIT_BYTES):
    kw = dict(vmem_limit_bytes=vmem)
    if sem is not None:
        kw["dimension_semantics"] = sem
    return pltpu.CompilerParams(**kw)


def _pick(n, pref):
    if n <= pref:
        return n
    t = (pref // LANES) * LANES
    while t >= LANES:
        if n % t == 0:
            return t
        t -= LANES
    return n


def _dot(a, b, dims=(((1,), (0,)), ((), ())), precision=None):
    return lax.dot_general(a, b, dims, precision=precision, preferred_element_type=F32)


_NT = (((1,), (1,)), ((), ()))
_TN = (((0,), (0,)), ((), ()))


def _mm(a, b, *, ta=False, tb=False, res=None, out_dtype=None, name, tm=1408, tn=1408, tk=2048,
        a_heads=False, b_heads=False):
    assert not (a_heads and ta) and not (b_heads and tb)
    if out_dtype is None:
        out_dtype = BF16 if ta else F32
    if a_heads:
        m, k = a.shape[1], a.shape[0] * LANES
    else:
        m = a.shape[1] if ta else a.shape[0]
        k = a.shape[0] if ta else a.shape[1]
    if b_heads:
        n, kb = b.shape[0] * LANES, b.shape[1]
    else:
        n = b.shape[0] if tb else b.shape[1]
        kb = b.shape[1] if tb else b.shape[0]
    assert k == kb
    tm, tn, tk = _pick(m, tm), _pick(n, tn), _pick(k, tk)
    nk = k // tk
    if a_heads:
        a_spec = pl.BlockSpec((tk // LANES, tm, LANES), lambda i, j, l: (l, i, 0))
    elif ta:
        a_spec = pl.BlockSpec((tk, tm), lambda i, j, l: (l, i))
    else:
        a_spec = pl.BlockSpec((tm, tk), lambda i, j, l: (i, l))
    if b_heads:
        b_spec = pl.BlockSpec((tn // LANES, tk, LANES), lambda i, j, l: (j, l, 0))
    elif tb:
        b_spec = pl.BlockSpec((tn, tk), lambda i, j, l: (j, l))
    else:
        b_spec = pl.BlockSpec((tk, tn), lambda i, j, l: (l, j))
    o_spec = pl.BlockSpec((tm, tn), lambda i, j, l: (i, j))
    dims = (((0 if ta else 1,), (1 if tb else 0,)), ((), ()))
    has_res = res is not None

    def load(ref, heads):
        if not heads:
            return ref[...].astype(BF16)
        return jnp.concatenate([ref[i].astype(BF16) for i in range(ref.shape[0])], axis=1)

    def body(*refs):
        a_ref, b_ref = refs[:2]
        r_ref = refs[2] if has_res else None
        o_ref = refs[2 + has_res]
        p = _dot(load(a_ref, a_heads), load(b_ref, b_heads), dims)

        def finish(r):
            if has_res:
                r = r + r_ref[...]
            o_ref[...] = r.astype(o_ref.dtype)

        if nk == 1:
            finish(p)
            return
        acc = refs[3 + has_res]
        l = pl.program_id(2)

        @pl.when(l == 0)
        def _():
            acc[...] = p

        @pl.when(jnp.logical_and(l > 0, l < nk - 1))
        def _():
            acc[...] += p

        @pl.when(l == nk - 1)
        def _():
            finish(acc[...] + p)

    ins = [a, b] + ([res] if has_res else [])
    in_specs = [a_spec, b_spec] + ([o_spec] if has_res else [])
    return pl.pallas_call(
        body, name=name, grid=(m // tm, n // tn, nk), in_specs=in_specs, out_specs=o_spec,
        out_shape=jax.ShapeDtypeStruct((m, n), out_dtype),
        scratch_shapes=[pltpu.VMEM((tm, tn), F32)] if nk > 1 else [],
        compiler_params=_params(("parallel", "parallel", "arbitrary")))(*ins)


def _rowwise(fn, rows, bcasts, outs, accs=(), *, tile, name):
    s = rows[0].shape[-2]
    tile = min(tile, s)
    n_in, n_out, n_acc = len(rows) + len(bcasts), len(outs), len(accs)

    def row_spec(c):
        if isinstance(c, tuple):
            return pl.BlockSpec((c[0], tile, c[1]), lambda i: (0, i, 0))
        return pl.BlockSpec((tile, c), lambda i: (i, 0))

    def row_shape(c):
        return (c[0], s, c[1]) if isinstance(c, tuple) else (s, c)

    def body(*refs):
        vals = fn(*[r[...] for r in refs[:n_in]])
        o_refs = refs[n_in:n_in + n_out]
        a_refs = refs[n_in + n_out:]
        for r, v in zip(o_refs, vals[:n_out]):
            if isinstance(v, list):
                for i, vi in enumerate(v):
                    r[i] = vi.astype(r.dtype)
            else:
                r[...] = v.astype(r.dtype)

        @pl.when(pl.program_id(0) == 0)
        def _():
            for r in a_refs:
                r[...] = jnp.zeros(r.shape, r.dtype)

        for r, v in zip(a_refs, vals[n_out:]):
            r[...] += v

    in_specs = [row_spec(r.shape[1] if r.ndim == 2 else (r.shape[0], r.shape[2])) for r in rows]
    in_specs += [pl.BlockSpec(b.shape, lambda i: (0, 0)) for b in bcasts]
    out_specs = [row_spec(c) for c, _ in outs]
    out_specs += [pl.BlockSpec(sh, lambda i: (0, 0)) for sh, _ in accs]
    out_shape = [jax.ShapeDtypeStruct(row_shape(c), dt) for c, dt in outs]
    out_shape += [jax.ShapeDtypeStruct(sh, dt) for sh, dt in accs]
    return pl.pallas_call(body, name=name, grid=(s // tile,), in_specs=in_specs, out_specs=out_specs,
                          out_shape=out_shape, compiler_params=_params(("arbitrary",)))(*rows, *bcasts)


def _rms_fwd(x, w, name):
    def fn(x, w):
        r = lax.rsqrt(jnp.mean(x * x, axis=-1, keepdims=True) + RMS_EPS)
        return (x * r * w,)
    return _rowwise(fn, [x], [w], [(x.shape[1], BF16)], tile=256, name=name)[0]


def _rms_bwd(x, w, dh, dres, name):
    def fn(x, dh, dres, w):
        r = lax.rsqrt(jnp.mean(x * x, axis=-1, keepdims=True) + RMS_EPS)
        xh = x * r
        dxh = dh * w
        dx = dres + r * (dxh - xh * jnp.mean(dxh * xh, axis=-1, keepdims=True))
        return dx, dx, jnp.sum(dh * xh, axis=0, keepdims=True)
    d = x.shape[1]
    return _rowwise(fn, [x, dh, dres], [w], [(d, F32), (d, BF16)], [((1, d), F32)], tile=256, name=name)


def _final_loss(x, w, tgt, name):
    d = x.shape[1]

    def fn(x, t, w):
        r = lax.rsqrt(jnp.mean(x * x, axis=-1, keepdims=True) + RMS_EPS)
        xh = x * r
        err = xh * w - t
        part = jnp.sum(jnp.mean(err * err, axis=-1, keepdims=True), axis=0, keepdims=True) * 0.5
        dy = err * (1.0 / d)
        dxh = dy * w
        dx = r * (dxh - xh * jnp.mean(dxh * xh, axis=-1, keepdims=True))
        return dx, dx, part, jnp.sum(dy * xh, axis=0, keepdims=True)
    dx, dxb, part, dw = _rowwise(fn, [x, tgt], [w], [(d, F32), (d, BF16)], [((1, 1), F32), ((1, d), F32)],
                                 tile=256, name=name)
    return part, dx, dxb, dw


def _softplus_fwd(dtr, bias, name):
    def fn(r, b):
        v = r + b
        return (jnp.maximum(v, 0.0) + jnp.log(1.0 + jnp.exp(-jnp.abs(v))),)
    return _rowwise(fn, [dtr], [bias], [(LANES, F32)], tile=512, name=name)[0]


def _softplus_bwd(ddt, dtr, bias, n_heads, name):
    def fn(g, r, b):
        lane = lax.broadcasted_iota(jnp.int32, g.shape, 1)
        d = jnp.where(lane < n_heads, g * jax.nn.sigmoid(r + b), 0.0)
        return d, jnp.sum(d, axis=0, keepdims=True)
    return _rowwise(fn, [ddt, dtr], [bias], [(LANES, BF16)], [((1, LANES), F32)], tile=512, name=name)


def _gnorm_fwd(y, z, w, n_groups, name):
    di = y.shape[1]
    gs = di // n_groups

    def fn(y, z, w):
        y2 = y * (z * jax.nn.sigmoid(z))
        out = []
        for g in range(n_groups):
            sl = y2[:, g * gs:(g + 1) * gs]
            r = lax.rsqrt(jnp.mean(sl * sl, axis=-1, keepdims=True) + GATED_NORM_EPS)
            out.append(sl * r)
        return (jnp.concatenate(out, axis=1) * w,)
    return _rowwise(fn, [y, z], [w], [(di, BF16)], tile=256, name=name)[0]


def _gnorm_bwd(dyn, y, z, w, n_groups, name):
    di = y.shape[1]
    gs = di // n_groups

    def fn(dyn, y, z, w):
        sig = jax.nn.sigmoid(z)
        sz = z * sig
        y2 = y * sz
        d2n = dyn * w
        dy2, yhat = [], []
        for g in range(n_groups):
            sl = y2[:, g * gs:(g + 1) * gs]
            dg = d2n[:, g * gs:(g + 1) * gs]
            r = lax.rsqrt(jnp.mean(sl * sl, axis=-1, keepdims=True) + GATED_NORM_EPS)
            yh = sl * r
            dy2.append(r * (dg - yh * jnp.mean(dg * yh, axis=-1, keepdims=True)))
            yhat.append(yh)
        dy2 = jnp.concatenate(dy2, axis=1)
        yhat = jnp.concatenate(yhat, axis=1)
        dz = dy2 * y * (sig * (1.0 + z * (1.0 - sig)))
        return dy2 * sz, dz, jnp.sum(dyn * yhat, axis=0, keepdims=True)
    return _rowwise(fn, [dyn, y, z], [w], [(di, F32), (di, BF16)], [((1, di), F32)], tile=128, name=name)


def _merge_fwd(os_, lses, name):
    n = len(os_)

    def fn(*v):
        o, l = v[:n], v[n:]
        m = functools.reduce(jnp.maximum, l)
        e = [jnp.exp(li - m) for li in l]
        tot = functools.reduce(jnp.add, e)
        acc = functools.reduce(jnp.add, [ei * oi for ei, oi in zip(e, o)]) / tot
        return acc, acc, m + jnp.log(tot)
    c = os_[0].shape[1]
    return _rowwise(fn, list(os_) + list(lses), [], [(c, F32), (c, BF16), (c, F32)], tile=256, name=name)


def _delta(do, o, name):
    c = o.shape[1]

    def fn(do, o):
        p = do * o
        out = [jnp.broadcast_to(jnp.sum(p[:, j:j + ATT_HEAD_DIM], axis=-1, keepdims=True), (p.shape[0], ATT_HEAD_DIM))
               for j in range(0, c, ATT_HEAD_DIM)]
        return (jnp.concatenate(out, axis=1),)
    return _rowwise(fn, [do, o], [], [(c, F32)], tile=256, name=name)[0]


def _lane_place(cols):
    rows = cols[0].shape[0]
    lane = lax.broadcasted_iota(jnp.int32, (rows, LANES), 1)
    out = jnp.zeros((rows, LANES), F32)
    for j, c in enumerate(cols):
        out = jnp.where(lane == j, c, out)
    return out


def _merge_heads(os_, lses, name):
    n = len(os_)
    n_kv, rep, hd = ATT_KV_HEADS_PER_GROUP, ATT_HEADS_PER_GROUP // ATT_KV_HEADS_PER_GROUP, ATT_HEAD_DIM

    def fn(*v):
        o, l = v[:n], v[n:]
        out, lse = [], []
        for h in range(n_kv):
            cols = []
            for j in range(rep):
                hh = h * rep + j
                lg = [li[h][:, j:j + 1] for li in l]
                m = functools.reduce(jnp.maximum, lg)
                e = [jnp.exp(x - m) for x in lg]
                tot = functools.reduce(jnp.add, e)
                acc = functools.reduce(jnp.add, [ei * oi[hh] for ei, oi in zip(e, o)])
                out.append(acc / tot)
                cols.append(m + jnp.log(tot))
            lse.append(_lane_place(cols))
        merged = jnp.concatenate(out, axis=1)
        return merged, merged, lse
    c = os_[0].shape[0] * hd
    return _rowwise(fn, list(os_) + list(lses), [], [(c, F32), (c, BF16), ((n_kv, LANES), F32)], tile=256, name=name)


def _delta_heads(do, o, name):
    n_kv, rep, hd = ATT_KV_HEADS_PER_GROUP, ATT_HEADS_PER_GROUP // ATT_KV_HEADS_PER_GROUP, ATT_HEAD_DIM

    def fn(do, o):
        p = do * o
        return ([_lane_place([jnp.sum(p[:, (h * rep + j) * hd:(h * rep + j + 1) * hd], axis=-1, keepdims=True)
                              for j in range(rep)]) for h in range(n_kv)],)
    return _rowwise(fn, [do, o], [], [((n_kv, LANES), F32)], tile=256, name=name)[0]


def _sum_slabs(recv, name):
    def body(r_ref, o_ref):
        acc = r_ref[0]
        for k in range(1, NDEV):
            acc = acc + r_ref[k]
        o_ref[...] = acc
    return pl.pallas_call(body, name=name, out_shape=jax.ShapeDtypeStruct(recv.shape[1:], F32),
                          compiler_params=_params())(recv)


def _shift_down(x, k):
    if k == 0:
        return x
    r = pltpu.roll(x, k, 0)
    row = lax.broadcasted_iota(jnp.int32, (SUBLANES, x.shape[1]), 0)
    return jnp.concatenate([jnp.where(row >= k, r[:SUBLANES], 0.0), r[SUBLANES:]], axis=0)


def _shift_up(x, k):
    if k == 0:
        return x
    s = x.shape[0]
    r = pltpu.roll(x, s - k, 0)
    row = lax.broadcasted_iota(jnp.int32, (SUBLANES, x.shape[1]), 0)
    return jnp.concatenate([r[:s - SUBLANES], jnp.where(row < SUBLANES - k, r[s - SUBLANES:], 0.0)], axis=0)


def _conv(x, w):
    kw = w.shape[0]
    return functools.reduce(jnp.add, [w[k:k + 1, :] * _shift_down(x, kw - 1 - k) for k in range(kw)])


def _conv_t(dy, w):
    kw = w.shape[0]
    return functools.reduce(jnp.add, [w[k:k + 1, :] * _shift_up(dy, kw - 1 - k) for k in range(kw)])


def _conv_dw(x, dy, dw_ref):
    kw = dw_ref.shape[0]
    for k in range(kw):
        dw_ref[k:k + 1, :] = jnp.sum(dy * _shift_down(x, kw - 1 - k), axis=0, keepdims=True)


def _dsilu(pre):
    sig = jax.nn.sigmoid(pre)
    return sig * (1.0 + pre * (1.0 - sig))


def _col_specs(s, c, kw, tc):
    return (pl.BlockSpec((s, tc), lambda j: (0, j)), pl.BlockSpec((kw, tc), lambda j: (0, j)),
            pl.BlockSpec((1, tc), lambda j: (0, j)))


def _conv_silu_fwd(x, w, b, name):
    s, c = x.shape
    tc = LANES
    xs, ws, bs = _col_specs(s, c, w.shape[0], tc)

    def body(x_ref, w_ref, b_ref, o_ref):
        pre = _conv(x_ref[...], w_ref[...]) + b_ref[...]
        o_ref[...] = pre * jax.nn.sigmoid(pre)
    return pl.pallas_call(body, name=name, grid=(c // tc,), in_specs=[xs, ws, bs], out_specs=xs,
                          out_shape=jax.ShapeDtypeStruct((s, c), F32), compiler_params=_params(("parallel",)))(x, w, b)


def _conv_silu_bwd(x, w, b, dy, name):
    s, c = x.shape
    tc = LANES
    xs, ws, bs = _col_specs(s, c, w.shape[0], tc)

    def body(x_ref, w_ref, b_ref, dy_ref, dx_ref, dw_ref, db_ref):
        xv, wv = x_ref[...], w_ref[...]
        pre = _conv(xv, wv) + b_ref[...]
        dpre = dy_ref[...] * _dsilu(pre)
        dx_ref[...] = _conv_t(dpre, wv).astype(dx_ref.dtype)
        _conv_dw(xv, dpre, dw_ref)
        db_ref[...] = jnp.sum(dpre, axis=0, keepdims=True)
    return pl.pallas_call(
        body, name=name, grid=(c // tc,), in_specs=[xs, ws, bs, xs], out_specs=[xs, ws, bs],
        out_shape=[jax.ShapeDtypeStruct((s, c), BF16), jax.ShapeDtypeStruct(w.shape, F32),
                   jax.ShapeDtypeStruct((1, c), F32)],
        compiler_params=_params(("parallel",)))(x, w, b, dy)


def _ffn_gate_fwd(ug, uv, wg, wv, name):
    s, c = ug.shape
    tc = LANES
    xs, ws, _ = _col_specs(s, c, wg.shape[0], tc)

    def body(g_ref, v_ref, wg_ref, wv_ref, o_ref):
        g = _conv(g_ref[...], wg_ref[...])
        v = _conv(v_ref[...], wv_ref[...])
        o_ref[...] = (g * jax.nn.sigmoid(g) * v).astype(o_ref.dtype)
    return pl.pallas_call(body, name=name, grid=(c // tc,), in_specs=[xs, xs, ws, ws], out_specs=xs,
                          out_shape=jax.ShapeDtypeStruct((s, c), BF16),
                          compiler_params=_params(("parallel",)))(ug, uv, wg, wv)


def _ffn_gate_bwd(ug, uv, wg, wv, df, name):
    s, c = ug.shape
    tc = LANES
    xs, ws, _ = _col_specs(s, c, wg.shape[0], tc)

    def body(g_ref, v_ref, wg_ref, wv_ref, df_ref, dg_ref, dv_ref, dwg_ref, dwv_ref):
        gp, vp, wgv, wvv = g_ref[...], v_ref[...], wg_ref[...], wv_ref[...]
        g = _conv(gp, wgv)
        v = _conv(vp, wvv)
        dfv = df_ref[...]
        dg = dfv * v * _dsilu(g)
        dv = dfv * (g * jax.nn.sigmoid(g))
        dg_ref[...] = _conv_t(dg, wgv).astype(dg_ref.dtype)
        dv_ref[...] = _conv_t(dv, wvv).astype(dv_ref.dtype)
        _conv_dw(gp, dg, dwg_ref)
        _conv_dw(vp, dv, dwv_ref)
    return pl.pallas_call(
        body, name=name, grid=(c // tc,), in_specs=[xs, xs, ws, ws, xs], out_specs=[xs, xs, ws, ws],
        out_shape=[jax.ShapeDtypeStruct((s, c), BF16), jax.ShapeDtypeStruct((s, c), BF16),
                   jax.ShapeDtypeStruct(wg.shape, F32), jax.ShapeDtypeStruct(wv.shape, F32)],
        compiler_params=_params(("parallel",)))(ug, uv, wg, wv, df)


def _ssd_common(dt, alog, n_heads):
    ln = dt.shape[0]
    lane = lax.broadcasted_iota(jnp.int32, (1, LANES), 1)
    a = jnp.where(lane < n_heads, -jnp.exp(alog), 0.0)
    row = lax.broadcasted_iota(jnp.int32, (ln, ln), 0)
    col = lax.broadcasted_iota(jnp.int32, (ln, ln), 1)
    tril = col <= row
    acs = _dot(tril.astype(F32), dt * a, precision=HIGHEST)
    return a, acs, acs.T, tril


def _ssd_fwd(xbc, dt, alog, dskip, di, n_heads, n_groups, name):
    s, convd = xbc.shape
    ln, p, ns = SSM_CHUNK, SSM_HEAD_DIM, SSM_D_STATE
    nc, hg = s // ln, n_heads // n_groups

    def body(x_ref, dt_ref, alog_ref, d_ref, y_ref, prev_ref, st):
        @pl.when(pl.program_id(0) == 0)
        def _():
            st[...] = jnp.zeros(st.shape, F32)

        dt = dt_ref[...]
        _, acs, acs_t, tril = _ssd_common(dt, alog_ref[...], n_heads)
        e_all = jnp.exp(acs)
        last = acs[ln - 1:ln, :]
        ds_all = jnp.exp(last - acs)
        t_all = jnp.exp(last)
        dsk = d_ref[...]
        for g in range(n_groups):
            bg = x_ref[:, di + g * ns:di + (g + 1) * ns].astype(BF16)
            cg = x_ref[:, di + (n_groups + g) * ns:di + (n_groups + g + 1) * ns].astype(BF16)
            gm = _dot(cg, bg, _NT)
            for j in range(hg):
                h = g * hg + j
                xh = x_ref[:, h * p:(h + 1) * p]
                xdt = xh * dt[:, h:h + 1]
                seg = acs[:, h:h + 1] - acs_t[h:h + 1, :]
                m = jnp.where(tril, gm * jnp.exp(jnp.where(tril, seg, 0.0)), 0.0)
                prev = st[h]
                prev_ref[0, h] = prev
                y = _dot(m.astype(BF16), xdt.astype(BF16))
                y = y + _dot(cg, prev.astype(BF16), _NT) * e_all[:, h:h + 1]
                y = y + xh * dsk[:, h:h + 1]
                snew = _dot((xdt * ds_all[:, h:h + 1]).astype(BF16), bg, _TN)
                st[h] = prev * t_all[:, h:h + 1] + snew
                y_ref[:, h * p:(h + 1) * p] = y

    vec = pl.BlockSpec((1, LANES), lambda c: (0, 0))
    return pl.pallas_call(
        body, name=name, grid=(nc,),
        in_specs=[pl.BlockSpec((ln, convd), lambda c: (c, 0)), pl.BlockSpec((ln, LANES), lambda c: (c, 0)), vec, vec],
        out_specs=[pl.BlockSpec((ln, di), lambda c: (c, 0)),
                   pl.BlockSpec((1, n_heads, p, ns), lambda c: (c, 0, 0, 0))],
        out_shape=[jax.ShapeDtypeStruct((s, di), F32), jax.ShapeDtypeStruct((nc, n_heads, p, ns), F32)],
        scratch_shapes=[pltpu.VMEM((n_heads, p, ns), F32)],
        compiler_params=_params(("arbitrary",)))(xbc, dt, alog, dskip)


def _ssd_bwd(xbc, dt, alog, dskip, prev_all, dy, di, n_heads, n_groups, name):
    s, convd = xbc.shape
    ln, p, ns = SSM_CHUNK, SSM_HEAD_DIM, SSM_D_STATE
    nc, hg = s // ln, n_heads // n_groups

    def body(x_ref, dt_ref, alog_ref, d_ref, prev_ref, dy_ref, dx_ref, ddt_ref, da_ref, dd_ref, dh):
        step = pl.program_id(0)

        @pl.when(step == 0)
        def _():
            dh[...] = jnp.zeros(dh.shape, F32)
            da_ref[...] = jnp.zeros(da_ref.shape, F32)
            dd_ref[...] = jnp.zeros(dd_ref.shape, F32)

        dt = dt_ref[...]
        a, acs, acs_t, tril = _ssd_common(dt, alog_ref[...], n_heads)
        e_all = jnp.exp(acs)
        last = acs[ln - 1:ln, :]
        ds_all = jnp.exp(last - acs)
        t_all = jnp.exp(last)
        dsk = d_ref[...]
        lane = lax.broadcasted_iota(jnp.int32, (ln, LANES), 1)
        lane1 = lax.broadcasted_iota(jnp.int32, (1, LANES), 1)
        sub = lax.broadcasted_iota(jnp.int32, (LANES, ln), 0)
        rowi = lax.broadcasted_iota(jnp.int32, (ln, LANES), 0)
        dacs_c = jnp.zeros((ln, LANES), F32)
        dacs_r = jnp.zeros((LANES, ln), F32)
        dlast = jnp.zeros((1, LANES), F32)
        ddt_x = jnp.zeros((ln, LANES), F32)
        dd = jnp.zeros((1, LANES), F32)

        def tot(v):
            return jnp.sum(jnp.sum(v, axis=1, keepdims=True), axis=0, keepdims=True)

        for g in range(n_groups):
            bg = x_ref[:, di + g * ns:di + (g + 1) * ns].astype(BF16)
            cg = x_ref[:, di + (n_groups + g) * ns:di + (n_groups + g + 1) * ns].astype(BF16)
            gm = _dot(cg, bg, _NT)
            dgm = jnp.zeros((ln, ln), F32)
            dcg = jnp.zeros((ln, ns), F32)
            dbg = jnp.zeros((ln, ns), F32)
            for j in range(hg):
                h = g * hg + j
                xh = x_ref[:, h * p:(h + 1) * p]
                dth = dt[:, h:h + 1]
                xdt = xh * dth
                dyh = dy_ref[:, h * p:(h + 1) * p]
                eh, dsh, th = e_all[:, h:h + 1], ds_all[:, h:h + 1], t_all[:, h:h + 1]
                seg = acs[:, h:h + 1] - acs_t[h:h + 1, :]
                dec = jnp.where(tril, jnp.exp(jnp.where(tril, seg, 0.0)), 0.0)
                m = gm * dec
                prev = prev_ref[0, h]
                dhn = dh[h]
                prevb, dhb, dyb, xdtb = prev.astype(BF16), dhn.astype(BF16), dyh.astype(BF16), xdt.astype(BF16)
                yo = _dot(cg, prevb, _NT)
                dyob = (dyh * eh).astype(BF16)
                c_col = jnp.sum(dyh * yo, axis=1, keepdims=True) * eh
                dcg = dcg + _dot(dyob, prevb)
                dprev = th * dhn + _dot(dyob, cg, _TN)
                dtt = tot(dhn * prev)
                w = _dot(bg, dhb, _NT)
                dxdt = w * dsh
                dds = jnp.sum(w * xdt, axis=1, keepdims=True)
                dbg = dbg + _dot((xdt * dsh).astype(BF16), dhb)
                dm = _dot(dyb, xdtb, _NT)
                dxdt = dxdt + _dot(m.astype(BF16), dyb, _TN)
                dgm = dgm + dm * dec
                q = dm * m
                c_col = c_col + jnp.sum(q, axis=1, keepdims=True) - dds * dsh
                r_row = -jnp.sum(q, axis=0, keepdims=True)
                dlast_h = tot(dds * dsh) + dtt * th
                dacs_c = dacs_c + jnp.where(lane == h, c_col, 0.0)
                dacs_r = dacs_r + jnp.where(sub == h, r_row, 0.0)
                dlast = dlast + jnp.where(lane1 == h, dlast_h, 0.0)
                ddt_x = ddt_x + jnp.where(lane == h, jnp.sum(dxdt * xh, axis=1, keepdims=True), 0.0)
                dd = dd + jnp.where(lane1 == h, tot(dyh * xh), 0.0)
                dx_ref[:, h * p:(h + 1) * p] = dxdt * dth + dyh * dsk[:, h:h + 1]
                dh[h] = dprev
            dgb = dgm.astype(BF16)
            dx_ref[:, di + g * ns:di + (g + 1) * ns] = dbg + _dot(dgb, cg, _TN)
            dx_ref[:, di + (n_groups + g) * ns:di + (n_groups + g + 1) * ns] = dcg + _dot(dgb, bg)

        dacs = dacs_c + dacs_r.T + jnp.where(rowi == ln - 1, dlast, 0.0)
        row = lax.broadcasted_iota(jnp.int32, (ln, ln), 0)
        col = lax.broadcasted_iota(jnp.int32, (ln, ln), 1)
        dadt = _dot((col >= row).astype(F32), dacs, precision=HIGHEST)
        ddt_ref[...] = dadt * a + ddt_x
        da_ref[...] += jnp.sum(dadt * dt, axis=0, keepdims=True)
        dd_ref[...] += dd

        @pl.when(step == nc - 1)
        def _():
            da_ref[...] = da_ref[...] * a

    vec = pl.BlockSpec((1, LANES), lambda c: (0, 0))
    rev = lambda c: (nc - 1 - c, 0)
    return pl.pallas_call(
        body, name=name, grid=(nc,),
        in_specs=[pl.BlockSpec((ln, convd), rev), pl.BlockSpec((ln, LANES), rev), vec, vec,
                  pl.BlockSpec((1, n_heads, p, ns), lambda c: (nc - 1 - c, 0, 0, 0)), pl.BlockSpec((ln, di), rev)],
        out_specs=[pl.BlockSpec((ln, convd), rev), pl.BlockSpec((ln, LANES), rev), vec, vec],
        out_shape=[jax.ShapeDtypeStruct((s, convd), F32), jax.ShapeDtypeStruct((s, LANES), F32),
                   jax.ShapeDtypeStruct((1, LANES), F32), jax.ShapeDtypeStruct((1, LANES), F32)],
        scratch_shapes=[pltpu.VMEM((n_heads, p, ns), F32)],
        compiler_params=_params(("arbitrary",)))(xbc, dt, alog, dskip, prev_all, dy)


def _split(x, n):
    out = []
    for _ in range(n):
        piece = x.astype(BF16)
        out.append(piece)
        x = x - piece.astype(F32)
    return out


def _spread(x, onehot, n=2):
    return functools.reduce(jnp.add, [_dot(piece, onehot) for piece in _split(x, n)])


def _head_maps(di, p):
    e = (jnp.arange(di, dtype=jnp.int32)[None, :] // p == jnp.arange(LANES, dtype=jnp.int32)[:, None]).astype(BF16)
    return e, e.T


def _ssd_wide(dt, acs, acs_t, dskip, e_ref, et_ref):
    ln = dt.shape[0]
    last = acs[ln - 1:ln, :]
    stack = jnp.concatenate([dt, jnp.exp(acs), jnp.exp(last - acs), jnp.broadcast_to(dskip, (8, LANES))], axis=0)
    wide = _spread(stack, e_ref[...])
    tb = jnp.exp(jnp.broadcast_to(acs_t[:, ln - 1:ln], (LANES, LANES)))
    texp = functools.reduce(jnp.add, [_dot(et_ref[...], piece) for piece in _split(tb, 3)])
    return wide[:ln], wide[ln:2 * ln], wide[2 * ln:3 * ln], wide[3 * ln:3 * ln + 1], texp


def _ssd_fwd2(xbc, dt, alog, dskip, di, n_heads, n_groups, name):
    s, convd = xbc.shape
    ln, p, ns = SSM_CHUNK, SSM_HEAD_DIM, SSM_D_STATE
    nc, hg = s // ln, n_heads // n_groups
    gw = hg * p
    e64, e64t = _head_maps(di, p)

    def body(x_ref, dt_ref, alog_ref, d_ref, e_ref, et_ref, y_ref, prev_ref, st):
        @pl.when(pl.program_id(0) == 0)
        def _():
            st[...] = jnp.zeros(st.shape, F32)

        dt = dt_ref[...]
        _, acs, acs_t, tril = _ssd_common(dt, alog_ref[...], n_heads)
        dte, ee, dse, dske, texp = _ssd_wide(dt, acs, acs_t, d_ref[...], e_ref, et_ref)
        x = x_ref[:, :di]
        xdt = x * dte
        xdtb = xdt.astype(BF16)
        xdsb = (xdt * dse).astype(BF16)
        for g in range(n_groups):
            rows = slice(g * gw, (g + 1) * gw)
            bg = x_ref[:, di + g * ns:di + (g + 1) * ns].astype(BF16)
            cg = x_ref[:, di + (n_groups + g) * ns:di + (n_groups + g + 1) * ns].astype(BF16)
            gm = _dot(cg, bg, _NT)
            prev = st[rows, :]
            prev_ref[0, rows, :] = prev
            yo = _dot(cg, prev.astype(BF16), _NT)
            for j in range(hg):
                h = g * hg + j
                seg = acs[:, h:h + 1] - acs_t[h:h + 1, :]
                m = jnp.where(tril, gm * jnp.exp(jnp.where(tril, seg, 0.0)), 0.0)
                y_ref[:, h * p:(h + 1) * p] = _dot(m.astype(BF16), xdtb[:, h * p:(h + 1) * p])
            y_ref[:, rows] = y_ref[:, rows] + yo * ee[:, rows] + x[:, rows] * dske[:, rows]
            st[rows, :] = prev * texp[rows, :] + _dot(xdsb[:, rows], bg, _TN)

    vec = pl.BlockSpec((1, LANES), lambda c: (0, 0))
    return pl.pallas_call(
        body, name=name, grid=(nc,),
        in_specs=[pl.BlockSpec((ln, convd), lambda c: (c, 0)), pl.BlockSpec((ln, LANES), lambda c: (c, 0)), vec, vec,
                  pl.BlockSpec(e64.shape, lambda c: (0, 0)), pl.BlockSpec(e64t.shape, lambda c: (0, 0))],
        out_specs=[pl.BlockSpec((ln, di), lambda c: (c, 0)), pl.BlockSpec((1, di, ns), lambda c: (c, 0, 0))],
        out_shape=[jax.ShapeDtypeStruct((s, di), F32), jax.ShapeDtypeStruct((nc, di, ns), F32)],
        scratch_shapes=[pltpu.VMEM((di, ns), F32)],
        compiler_params=_params(("arbitrary",)))(xbc, dt, alog, dskip, e64, e64t)


def _ssd_bwd2(xbc, dt, alog, dskip, prev_all, dy, di, n_heads, n_groups, name):
    s, convd = xbc.shape
    ln, p, ns = SSM_CHUNK, SSM_HEAD_DIM, SSM_D_STATE
    nc, hg = s // ln, n_heads // n_groups
    gw = hg * p
    e64, e64t = _head_maps(di, p)

    def body(x_ref, dt_ref, alog_ref, d_ref, e_ref, et_ref, prev_ref, dy_ref,
             dx_ref, ddt_ref, da_ref, dd_ref, dh, yo_ref, w_ref):
        step = pl.program_id(0)

        @pl.when(step == 0)
        def _():
            dh[...] = jnp.zeros(dh.shape, F32)
            da_ref[...] = jnp.zeros(da_ref.shape, F32)
            dd_ref[...] = jnp.zeros(dd_ref.shape, F32)

        dt = dt_ref[...]
        a, acs, acs_t, tril = _ssd_common(dt, alog_ref[...], n_heads)
        dte, ee, dse, dske, texp = _ssd_wide(dt, acs, acs_t, d_ref[...], e_ref, et_ref)
        row = lax.broadcasted_iota(jnp.int32, (ln, ln), 0)
        col = lax.broadcasted_iota(jnp.int32, (ln, ln), 1)
        triu = col >= row
        x = x_ref[:, :di]
        dy = dy_ref[...]
        xdt = x * dte
        xdtb = xdt.astype(BF16)
        xdsb = (xdt * dse).astype(BF16)
        dyb = dy.astype(BF16)
        dyob = (dy * ee).astype(BF16)
        dhn = dh[...]
        dhb = dhn.astype(BF16)
        per_head = functools.reduce(jnp.add, [_dot(e_ref[...], piece) for piece in _split(dhn * prev_ref[0], 2)])
        ones8 = jnp.ones((8, LANES), BF16)
        dtt = functools.reduce(jnp.add, [_dot(ones8, piece, _NT) for piece in _split(per_head, 2)])[0:1]
        dacs_c = jnp.zeros((ln, LANES), F32)
        dacs_r = jnp.zeros((LANES, ln), F32)
        for g in range(n_groups):
            rows = slice(g * gw, (g + 1) * gw)
            bg = x_ref[:, di + g * ns:di + (g + 1) * ns].astype(BF16)
            cg = x_ref[:, di + (n_groups + g) * ns:di + (n_groups + g + 1) * ns].astype(BF16)
            gmt = _dot(bg, cg, _NT)
            prevb = prev_ref[0, rows, :].astype(BF16)
            dcg = _dot(dyob[:, rows], prevb)
            dh[rows, :] = texp[rows, :] * dhn[rows, :] + _dot(dyob[:, rows], cg, _TN)
            w = _dot(bg, dhb[rows, :], _NT)
            dbg = _dot(xdsb[:, rows], dhb[rows, :])
            yo_ref[:, rows] = _dot(cg, prevb, _NT)
            w_ref[:, rows] = w
            dgmt = jnp.zeros((ln, ln), F32)
            q_hi, q_lo = [], []
            for j in range(hg):
                h = g * hg + j
                segt = acs_t[h:h + 1, :] - acs[:, h:h + 1]
                dect = jnp.where(triu, jnp.exp(jnp.where(triu, segt, 0.0)), 0.0)
                dyh, xh = dyb[:, h * p:(h + 1) * p], xdtb[:, h * p:(h + 1) * p]
                mt = gmt * dect
                dmt = _dot(xh, dyh, _NT)
                dx_ref[:, h * p:(h + 1) * p] = _dot(mt.astype(BF16), dyh)
                dgmt = dgmt + dmt * dect
                hi, lo = _split(dmt * mt, 2)
                q_hi.append(hi)
                q_lo.append(lo)
            sel_c = (lax.broadcasted_iota(jnp.int32, (hg * ln, LANES), 1)
                     == g * hg + lax.broadcasted_iota(jnp.int32, (hg * ln, LANES), 0) // ln).astype(BF16)
            sel_r = (lax.broadcasted_iota(jnp.int32, (LANES, hg * ln), 0)
                     == g * hg + lax.broadcasted_iota(jnp.int32, (LANES, hg * ln), 1) // ln).astype(BF16)
            for pieces in (q_hi, q_lo):
                dacs_c = dacs_c - _dot(jnp.concatenate(pieces, axis=1), sel_c)
                dacs_r = dacs_r + _dot(sel_r, jnp.concatenate(pieces, axis=0))
            dgb = dgmt.astype(BF16)
            dx_ref[:, di + g * ns:di + (g + 1) * ns] = dbg + _dot(dgb, cg)
            dx_ref[:, di + (n_groups + g) * ns:di + (n_groups + g + 1) * ns] = dcg + _dot(dgb, bg, _TN)

        wds = w_ref[...] * dse
        dxdt = dx_ref[:, :di] + wds
        red = _spread(jnp.concatenate([dxdt * x, dy * yo_ref[...] * ee, xdt * wds, dy * x], axis=0), et_ref[...])
        ddt_x, r_off, r_state, ddr = red[:ln], red[ln:2 * ln], red[2 * ln:3 * ln], red[3 * ln:]
        dx_ref[:, :di] = dxdt * dte + dy * dske
        rowi = lax.broadcasted_iota(jnp.int32, (ln, LANES), 0)
        dlast = jnp.sum(r_state, axis=0, keepdims=True) + dtt * jnp.exp(acs[ln - 1:ln, :])
        dacs = r_off - r_state + dacs_c + dacs_r.T + jnp.where(rowi == ln - 1, dlast, 0.0)
        dadt = _dot(triu.astype(F32), dacs, precision=HIGHEST)
        ddt_ref[...] = dadt * a + ddt_x
        da_ref[...] += jnp.sum(dadt * dt, axis=0, keepdims=True)
        dd_ref[...] += jnp.sum(ddr, axis=0, keepdims=True)

        @pl.when(step == nc - 1)
        def _():
            da_ref[...] = da_ref[...] * a

    vec = pl.BlockSpec((1, LANES), lambda c: (0, 0))
    rev = lambda c: (nc - 1 - c, 0)
    return pl.pallas_call(
        body, name=name, grid=(nc,),
        in_specs=[pl.BlockSpec((ln, convd), rev), pl.BlockSpec((ln, LANES), rev), vec, vec,
                  pl.BlockSpec(e64.shape, lambda c: (0, 0)), pl.BlockSpec(e64t.shape, lambda c: (0, 0)),
                  pl.BlockSpec((1, di, ns), lambda c: (nc - 1 - c, 0, 0)), pl.BlockSpec((ln, di), rev)],
        out_specs=[pl.BlockSpec((ln, convd), rev), pl.BlockSpec((ln, LANES), rev), vec, vec],
        out_shape=[jax.ShapeDtypeStruct((s, convd), F32), jax.ShapeDtypeStruct((s, LANES), F32),
                   jax.ShapeDtypeStruct((1, LANES), F32), jax.ShapeDtypeStruct((1, LANES), F32)],
        scratch_shapes=[pltpu.VMEM((di, ns), F32), pltpu.VMEM((ln, di), F32), pltpu.VMEM((ln, di), F32)],
        compiler_params=_params(("arbitrary",)))(xbc, dt, alog, dskip, e64, e64t, prev_all, dy)


def _perm(a, d):
    if d == 1:
        return a
    s = a.shape[0]
    return a.reshape(s // d, d, -1).transpose(1, 0, 2).reshape(s, -1)


def _unperm(a, d):
    if d == 1:
        return a
    s = a.shape[0]
    return a.reshape(d, s // d, -1).transpose(1, 0, 2).reshape(s, -1)


def _rot_tables(s, d):
    half = ROPE_DIM // 2
    inv_freq = jnp.power(jnp.float32(ROPE_THETA), -jnp.arange(0, ROPE_DIM, 2, dtype=F32) / ROPE_DIM)
    v = jnp.arange(s, dtype=jnp.int32)
    pos = (v % (s // d)) * d + v // (s // d)
    ang = pos.astype(F32)[:, None] * inv_freq[None, :]
    cos, sin = jnp.cos(ang), jnp.sin(ang)
    zero = jnp.zeros((s, ATT_HEAD_DIM - ROPE_DIM), F32)
    cf = jnp.concatenate([cos, cos, jnp.ones_like(zero)], axis=1)
    s1 = jnp.concatenate([-sin, jnp.zeros_like(sin), zero], axis=1)
    s2 = jnp.concatenate([jnp.zeros_like(sin), sin, zero], axis=1)
    assert half * 2 == ROPE_DIM
    return cf, s1, s2


def _rot(x, tabs, sign):
    cf, s1, s2 = tabs
    half = ROPE_DIM // 2
    left = pltpu.roll(x, ATT_HEAD_DIM - half, 1)
    right = pltpu.roll(x, half, 1)
    return x * cf + sign * (left * s1 + right * s2)


def _att_masks(n, n_blk, rep):
    b = ATT_BLOCK
    row = lax.broadcasted_iota(jnp.int32, (rep * b, b), 0) & (b - 1)
    col = lax.broadcasted_iota(jnp.int32, (rep * b, b), 1)
    off = jnp.where(n % n_blk != 0, 0, 2 * b)
    return col <= row, col >= row + off


def _stack(x, rep):
    return jnp.concatenate([x[:, j * ATT_HEAD_DIM:(j + 1) * ATT_HEAD_DIM] for j in range(rep)], axis=0)


def _att_specs(nb, rep, cur, prv):
    b, hd = ATT_BLOCK, ATT_HEAD_DIM
    q_spec = pl.BlockSpec((b, rep * hd), lambda h, n: (cur(n), h))
    kc_spec = pl.BlockSpec((b, hd), lambda h, n: (cur(n), h))
    kp_spec = pl.BlockSpec((b, hd), lambda h, n: (prv(n), h))
    tc_spec = pl.BlockSpec((b, hd), lambda h, n: (cur(n), 0))
    tp_spec = pl.BlockSpec((b, hd), lambda h, n: (prv(n), 0))
    return q_spec, kc_spec, kp_spec, tc_spec, tp_spec


def _attn_fwd(q, k, v, tabs, n_blk, name):
    s = q.shape[0]
    b, hd = ATT_BLOCK, ATT_HEAD_DIM
    nb = s // b
    n_kv = ATT_KV_HEADS_PER_GROUP
    rep = ATT_HEADS_PER_GROUP // n_kv
    scale = hd ** -0.5

    def body(q_ref, kc_ref, kp_ref, vc_ref, vp_ref, cfc, s1c, s2c, cfp, s1p, s2p, o_ref, lse_ref):
        n = pl.program_id(1)
        tc = (cfc[...], s1c[...], s2c[...])
        tp = (cfp[...], s1p[...], s2p[...])
        qv = q_ref[...]
        q4 = jnp.concatenate([_rot(qv[:, j * hd:(j + 1) * hd], tc, 1.0) for j in range(rep)], axis=0).astype(BF16)
        kc = _rot(kc_ref[...], tc, 1.0).astype(BF16)
        kp = _rot(kp_ref[...], tp, 1.0).astype(BF16)
        mc, mp = _att_masks(n, n_blk, rep)
        sc = jnp.where(mc, _dot(q4, kc, _NT) * scale, NEG)
        sp = jnp.where(mp, _dot(q4, kp, _NT) * scale, NEG)
        m = jnp.maximum(jnp.max(sc, axis=1, keepdims=True), jnp.max(sp, axis=1, keepdims=True))
        pc, pp = jnp.exp(sc - m), jnp.exp(sp - m)
        l = jnp.sum(pc, axis=1, keepdims=True) + jnp.sum(pp, axis=1, keepdims=True)
        o = (_dot(pc.astype(BF16), vc_ref[...].astype(BF16)) + _dot(pp.astype(BF16), vp_ref[...].astype(BF16))) / l
        lse = jnp.broadcast_to(m + jnp.log(l), (rep * b, hd))
        for j in range(rep):
            o_ref[:, j * hd:(j + 1) * hd] = o[j * b:(j + 1) * b]
            lse_ref[:, j * hd:(j + 1) * hd] = lse[j * b:(j + 1) * b]

    cur = lambda n: n
    prv = lambda n: jnp.maximum(n - 1, 0)
    q_spec, kc_spec, kp_spec, tc_spec, tp_spec = _att_specs(nb, rep, cur, prv)
    return pl.pallas_call(
        body, name=name, grid=(n_kv, nb),
        in_specs=[q_spec, kc_spec, kp_spec, kc_spec, kp_spec, tc_spec, tc_spec, tc_spec, tp_spec, tp_spec, tp_spec],
        out_specs=[q_spec, q_spec],
        out_shape=[jax.ShapeDtypeStruct(q.shape, F32), jax.ShapeDtypeStruct(q.shape, F32)],
        compiler_params=_params(("parallel", "arbitrary")))(q, k, k, v, v, *tabs, *tabs)


def _attn_bwd(q, k, v, do, lse, delta, tabs, n_blk, name):
    s = q.shape[0]
    b, hd = ATT_BLOCK, ATT_HEAD_DIM
    nb = s // b
    n_kv = ATT_KV_HEADS_PER_GROUP
    rep = ATT_HEADS_PER_GROUP // n_kv
    scale = hd ** -0.5

    def body(q_ref, do_ref, lse_ref, dl_ref, kc_ref, kp_ref, vc_ref, vp_ref, cfc, s1c, s2c, cfp, s1p, s2p,
             dq_ref, dk_ref, dv_ref, ck, cv):
        n = pl.program_id(1)
        tp = (cfp[...], s1p[...], s2p[...])

        @pl.when(n == 0)
        def _():
            ck[...] = jnp.zeros(ck.shape, F32)
            cv[...] = jnp.zeros(cv.shape, F32)

        @pl.when(n < nb)
        def _():
            tc = (cfc[...], s1c[...], s2c[...])
            qv = q_ref[...]
            q4 = jnp.concatenate([_rot(qv[:, j * hd:(j + 1) * hd], tc, 1.0) for j in range(rep)],
                                 axis=0).astype(BF16)
            do4 = _stack(do_ref[...], rep).astype(BF16)
            lse4 = _stack(lse_ref[...], rep)
            dl4 = _stack(dl_ref[...], rep)
            kc = _rot(kc_ref[...], tc, 1.0).astype(BF16)
            kp = _rot(kp_ref[...], tp, 1.0).astype(BF16)
            vc, vp = vc_ref[...].astype(BF16), vp_ref[...].astype(BF16)
            mc, mp = _att_masks(n, n_blk, rep)
            pc = jnp.where(mc, jnp.exp(jnp.where(mc, _dot(q4, kc, _NT) * scale - lse4, 0.0)), 0.0)
            pp = jnp.where(mp, jnp.exp(jnp.where(mp, _dot(q4, kp, _NT) * scale - lse4, 0.0)), 0.0)
            dsc = (pc * (_dot(do4, vc, _NT) - dl4)).astype(BF16)
            dsp = (pp * (_dot(do4, vp, _NT) - dl4)).astype(BF16)
            dq4 = (_dot(dsc, kc) + _dot(dsp, kp)) * scale
            for j in range(rep):
                dq_ref[:, j * hd:(j + 1) * hd] = _rot(dq4[j * b:(j + 1) * b], tc, -1.0).astype(dq_ref.dtype)
            dk_prev = ck[...] + _dot(dsp, q4, _TN) * scale
            dv_prev = cv[...] + _dot(pp.astype(BF16), do4, _TN)
            dk_ref[...] = _rot(dk_prev, tp, -1.0).astype(dk_ref.dtype)
            dv_ref[...] = dv_prev.astype(dv_ref.dtype)
            ck[...] = _dot(dsc, q4, _TN) * scale
            cv[...] = _dot(pc.astype(BF16), do4, _TN)

        @pl.when(n == nb)
        def _():
            dk_ref[...] = _rot(ck[...], tp, -1.0).astype(dk_ref.dtype)
            dv_ref[...] = cv[...].astype(dv_ref.dtype)

    cur = lambda n: jnp.minimum(n, nb - 1)
    prv = lambda n: jnp.maximum(n - 1, 0)
    q_spec, kc_spec, kp_spec, tc_spec, tp_spec = _att_specs(nb, rep, cur, prv)
    return pl.pallas_call(
        body, name=name, grid=(n_kv, nb + 1),
        in_specs=[q_spec, q_spec, q_spec, q_spec, kc_spec, kp_spec, kc_spec, kp_spec,
                  tc_spec, tc_spec, tc_spec, tp_spec, tp_spec, tp_spec],
        out_specs=[q_spec, kp_spec, kp_spec],
        out_shape=[jax.ShapeDtypeStruct(q.shape, BF16), jax.ShapeDtypeStruct(k.shape, BF16),
                   jax.ShapeDtypeStruct(k.shape, BF16)],
        scratch_shapes=[pltpu.VMEM((b, hd), F32), pltpu.VMEM((b, hd), F32)],
        compiler_params=_params(("parallel", "arbitrary")))(q, do, lse, delta, k, k, v, v, *tabs, *tabs)


def _rot_heads(x, tabs, width, sign, out_dtype, name):
    s = x.shape[0]
    hd = ATT_HEAD_DIM
    tile = min(512, s)

    def body(x_ref, cf, s1, s2, o_ref):
        t = (cf[...], s1[...], s2[...])
        for j in range(width // hd):
            o_ref[:, j * hd:(j + 1) * hd] = _rot(x_ref[:, j * hd:(j + 1) * hd], t, sign).astype(o_ref.dtype)

    tab = pl.BlockSpec((tile, hd), lambda i: (i, 0))
    return pl.pallas_call(
        body, name=name, grid=(s // tile,), in_specs=[pl.BlockSpec((tile, width), lambda i: (i, 0)), tab, tab, tab],
        out_specs=pl.BlockSpec((tile, width), lambda i: (i, 0)), out_shape=jax.ShapeDtypeStruct((s, width), out_dtype),
        compiler_params=_params(("parallel",)))(x, *tabs)


def _rows_of(r, dil):
    return pl.ds(r, ATT_BLOCK, stride=dil) if dil > 1 else slice(None)


def _nat_specs(g, dil, n_kv_all, cur, prv):
    b, hd = ATT_BLOCK * dil, ATT_HEAD_DIM
    n_kv = ATT_KV_HEADS_PER_GROUP
    rep = ATT_HEADS_PER_GROUP // n_kv
    q_all = [pl.BlockSpec((b, hd), lambda h, n, j=j: (cur(n), (g * n_kv + h) * rep + j)) for j in range(rep)]
    q_own = [pl.BlockSpec((b, hd), lambda h, n, j=j: (cur(n), h * rep + j)) for j in range(rep)]
    hm_all = pl.BlockSpec((rep, b, hd), lambda h, n: (g * n_kv + h, cur(n), 0))
    hm_own = pl.BlockSpec((rep, b, hd), lambda h, n: (h, cur(n), 0))
    kc =pl.BlockSpec((b, hd), lambda h, n: (cur(n), g * n_kv + h))
    kp = pl.BlockSpec((b, hd), lambda h, n: (prv(n), g * n_kv + h))
    vc = pl.BlockSpec((b, hd), lambda h, n: (cur(n), n_kv_all + g * n_kv + h))
    vp = pl.BlockSpec((b, hd), lambda h, n: (prv(n), n_kv_all + g * n_kv + h))
    tab = pl.BlockSpec((b, hd), lambda h, n: (cur(n), 0))
    stat = pl.BlockSpec((None, b, LANES), lambda h, n: (h, cur(n), 0))
    return q_all, q_own, hm_all, hm_own, kc, kp, vc, vp, tab, stat


def _head_cols(stat, rep):
    return jnp.concatenate([jnp.broadcast_to(stat[:, j:j + 1], stat.shape) for j in range(rep)], axis=0)


def _attn_fwd_nat(q_all, k_rot, kv, tabs, g, dil, name):
    s = q_all.shape[0]
    b, hd = ATT_BLOCK, ATT_HEAD_DIM
    nbn = s // (b * dil)
    n_kv = ATT_KV_HEADS_PER_GROUP
    rep = ATT_HEADS_PER_GROUP // n_kv
    n_kv_all = k_rot.shape[1] // hd
    scale = hd ** -0.5

    def body(*refs):
        q_refs = refs[:rep]
        kc_ref, kp_ref, vc_ref, vp_ref, cf, s1, s2, o_ref, lse_ref = refs[rep:]
        mc, mp = _att_masks(jnp.where(pl.program_id(1) > 0, 1, 0), 2, rep)
        for r in range(dil):
            sl = _rows_of(r, dil)
            tc = (cf[sl, :], s1[sl, :], s2[sl, :])
            q4 = jnp.concatenate([_rot(q_ref[sl, :], tc, 1.0) for q_ref in q_refs], axis=0).astype(BF16)
            kc, kp = kc_ref[sl, :].astype(BF16), kp_ref[sl, :].astype(BF16)
            sc = jnp.where(mc, _dot(q4, kc, _NT) * scale, NEG)
            sp = jnp.where(mp, _dot(q4, kp, _NT) * scale, NEG)
            m = jnp.maximum(jnp.max(sc, axis=1, keepdims=True), jnp.max(sp, axis=1, keepdims=True))
            pc, pp = jnp.exp(sc - m), jnp.exp(sp - m)
            l = jnp.sum(pc, axis=1, keepdims=True) + jnp.sum(pp, axis=1, keepdims=True)
            o = (_dot(pc.astype(BF16), vc_ref[sl, :].astype(BF16))
                 + _dot(pp.astype(BF16), vp_ref[sl, :].astype(BF16))) / l
            lse = m + jnp.log(l)
            for j in range(rep):
                o_ref[j, sl, :] = o[j * b:(j + 1) * b]
            lse_ref[sl, :] = _lane_place([lse[j * b:(j + 1) * b] for j in range(rep)])

    cur = lambda n: n
    prv = lambda n: jnp.maximum(n - 1, 0)
    q_specs, _, _, hm_own, kc, kp, vc, vp, tab, stat = _nat_specs(g, dil, n_kv_all, cur, prv)
    return pl.pallas_call(
        body, name=name, grid=(n_kv, nbn), in_specs=[*q_specs, kc, kp, vc, vp, tab, tab, tab], out_specs=[hm_own, stat],
        out_shape=[jax.ShapeDtypeStruct((ATT_HEADS_PER_GROUP, s, hd), F32), jax.ShapeDtypeStruct((n_kv, s, LANES), F32)],
        compiler_params=_params(("parallel", "arbitrary")))(*([q_all] * rep), k_rot, k_rot, kv, kv, *tabs)


def _attn_bwd_nat(q_all, k_rot, kv, do, lse, delta, tabs, grads, g, dil, name):
    s = q_all.shape[0]
    b, hd = ATT_BLOCK, ATT_HEAD_DIM
    nbn = s // (b * dil)
    n_kv = ATT_KV_HEADS_PER_GROUP
    rep = ATT_HEADS_PER_GROUP // n_kv
    n_kv_all = k_rot.shape[1] // hd
    scale = hd ** -0.5

    def body(*refs):
        q_refs, do_refs = refs[:rep], refs[rep:2 * rep]
        (lse_ref, dl_ref, kc_ref, kp_ref, vc_ref, vp_ref, cf, s1, s2, _, _, _,
         dq_ref, dk_ref, dv_ref, ck, cv) = refs[2 * rep:]
        n = pl.program_id(1)

        @pl.when(n == 0)
        def _():
            ck[...] = jnp.zeros(ck.shape, F32)
            cv[...] = jnp.zeros(cv.shape, F32)

        @pl.when(n < nbn)
        def _():
            mc, mp = _att_masks(jnp.where(n > 0, 1, 0), 2, rep)
            for r in range(dil):
                sl = _rows_of(r, dil)
                own = slice(r * b, (r + 1) * b)
                tc = (cf[sl, :], s1[sl, :], s2[sl, :])
                q4 = jnp.concatenate([_rot(q_ref[sl, :], tc, 1.0) for q_ref in q_refs], axis=0).astype(BF16)
                do4 = jnp.concatenate([do_ref[sl, :] for do_ref in do_refs], axis=0).astype(BF16)
                lse4 = _head_cols(lse_ref[sl, :], rep)
                dl4 = _head_cols(dl_ref[sl, :], rep)
                kc, kp = kc_ref[sl, :].astype(BF16), kp_ref[sl, :].astype(BF16)
                vc, vp = vc_ref[sl, :].astype(BF16), vp_ref[sl, :].astype(BF16)
                pc = jnp.where(mc, jnp.exp(_dot(q4, kc, _NT) * scale - lse4), 0.0)
                pp = jnp.where(mp, jnp.exp(_dot(q4, kp, _NT) * scale - lse4), 0.0)
                dsc = (pc * (_dot(do4, vc, _NT) - dl4)).astype(BF16)
                dsp = (pp * (_dot(do4, vp, _NT) - dl4)).astype(BF16)
                dq4 = (_dot(dsc, kc) + _dot(dsp, kp)) * scale
                for j in range(rep):
                    dq_ref[j, sl, :] = _rot(dq4[j * b:(j + 1) * b], tc, -1.0)
                dk_ref[sl, :] = ck[own, :] + _dot(dsp, q4, _TN) * scale
                dv_ref[sl, :] = cv[own, :] + _dot(pp.astype(BF16), do4, _TN)
                ck[own, :] = _dot(dsc, q4, _TN) * scale
                cv[own, :] = _dot(pc.astype(BF16), do4, _TN)

        @pl.when(n == nbn)
        def _():
            for r in range(dil):
                sl = _rows_of(r, dil)
                dk_ref[sl, :] = ck[r * b:(r + 1) * b, :]
                dv_ref[sl, :] = cv[r * b:(r + 1) * b, :]

    cur = lambda n: jnp.minimum(n, nbn - 1)
    prv = lambda n: jnp.maximum(n - 1, 0)
    q_specs, do_specs, hm_all, _, kc, kp, vc, vp, tab, stat = _nat_specs(g, dil, n_kv_all, cur, prv)
    anyspace = pl.BlockSpec(memory_space=pl.ANY)
    n_in = 2 * rep + 9
    return pl.pallas_call(
        body, name=name, grid=(n_kv, nbn + 1),
        in_specs=[*q_specs, *do_specs, stat, stat, kc, kp, vc, vp, tab, tab, tab, anyspace, anyspace, anyspace],
        out_specs=[hm_all, kp, kp], out_shape=[jax.ShapeDtypeStruct(a.shape, a.dtype) for a in grads],
        input_output_aliases={n_in: 0, n_in + 1: 1, n_in + 2: 2},
        scratch_shapes=[pltpu.VMEM((dil * b, hd), F32), pltpu.VMEM((dil * b, hd), F32)],
        compiler_params=_params(("parallel", "arbitrary"), VMEM_LIMIT_ATTN_BWD_BYTES))(
            *([q_all] * rep), *([do] * rep), lse, delta, k_rot, k_rot, kv, kv, *tabs, *grads)


def _adamw(g_slabs, w, m, v, name):
    kk, r, c = g_slabs.shape
    tile = r if r <= 256 else _pick_rows(r, 256)

    def body(g_ref, w_ref, m_ref, v_ref, go_ref, d_ref, mo_ref, vo_ref):
        g = g_ref[0].astype(F32)
        for k in range(1, kk):
            g = g + g_ref[k].astype(F32)
        m2 = ADAM_B1 * m_ref[...] + (1.0 - ADAM_B1) * g
        v2 = ADAM_B2 * v_ref[...] + (1.0 - ADAM_B2) * jnp.square(g)
        m_hat = m2 / (1.0 - ADAM_B1 ** ADAM_STEP)
        v_hat = v2 / (1.0 - ADAM_B2 ** ADAM_STEP)
        go_ref[...] = g
        d_ref[...] = -ADAM_LR * (m_hat / (jnp.sqrt(v_hat) + ADAM_EPS) + ADAM_WD * w_ref[...])
        mo_ref[...] = m2
        vo_ref[...] = v2

    spec = pl.BlockSpec((tile, c), lambda i: (i, 0))
    return pl.pallas_call(
        body, name=name, grid=(r // tile,), in_specs=[pl.BlockSpec((kk, tile, c), lambda i: (0, i, 0)), spec, spec, spec],
        out_specs=[spec] * 4, out_shape=[jax.ShapeDtypeStruct((r, c), F32)] * 4,
        compiler_params=_params(("parallel",)))(g_slabs, w, m, v)


def _pick_rows(r, pref):
    t = (pref // 16) * 16
    while t >= 16:
        if r % t == 0:
            return t
        t -= 16
    return r


def _coords():
    return lax.axis_index("x"), lax.axis_index("y"), lax.axis_index("c")


def _dev_index(px, py, pc):
    return 4 * px + 2 * py + pc


def _all_gather(shards, name):
    na = len(shards)

    def body(*refs):
        ins, outs = refs[:na], refs[na:2 * na]
        send_sems, recv_sems, local_sems = refs[2 * na:]
        x, y, c = _coords()
        me, sibling = (x, y, c), (x, y, 1 - c)
        chips = [(1 - x, y), (x, 1 - y), (1 - x, 1 - y)]

        def copy(a, k, block, to, src=None):
            dst = outs[a].at[_dev_index(*block)]
            return pltpu.make_async_remote_copy(
                src_ref=dst if src is None else src, dst_ref=dst, send_sem=send_sems.at[a * 7 + k],
                recv_sem=recv_sems.at[a * 7 + k], device_id=to, device_id_type=MESH)

        mine = [pltpu.make_async_copy(ins[a], outs[a].at[_dev_index(*me)], local_sems.at[a]) for a in range(na)]
        for cp in mine:
            cp.start()
        first = []
        for a in range(na):
            first.append(copy(a, 0, me, sibling, src=ins[a]))
            first += [copy(a, 1 + j, me, (*chip, c), src=ins[a]) for j, chip in enumerate(chips)]
        for cp in first:
            cp.start()
        passed = []
        for j, chip in enumerate(chips):
            for a in range(na):
                copy(a, 1 + j, (*chip, c), me).wait_recv()
                cp = copy(a, 4 + j, (*chip, c), sibling)
                cp.start()
                passed.append(cp)
        for a in range(na):
            copy(a, 0, sibling, me).wait_recv()
            for j, chip in enumerate(chips):
                copy(a, 4 + j, (*chip, 1 - c), me).wait_recv()
        for cp in first + passed:
            cp.wait_send()
        for cp in mine:
            cp.wait()

    hbm = pl.BlockSpec(memory_space=pl.ANY)
    return pl.pallas_call(
        body, name=name, in_specs=[hbm] * na, out_specs=[hbm] * na,
        out_shape=[jax.ShapeDtypeStruct((NDEV,) + s.shape, s.dtype) for s in shards],
        scratch_shapes=[pltpu.SemaphoreType.DMA((7 * na,)), pltpu.SemaphoreType.DMA((7 * na,)),
                        pltpu.SemaphoreType.DMA((na,))])(*shards)


def _exchange(slabs, whole, name, after=()):
    ns, nw = len(slabs), len(whole)
    na = ns + nw
    nb = len(after)

    def body(*refs):
        ins, outs = refs[:na], refs[na + nb:2 * na + nb]
        send_sems, recv_sems, local_sems = refs[2 * na + nb:]
        x, y, c = _coords()
        me = _dev_index(x, y, c)

        def src_of(a, p):
            return ins[a].at[p] if a < ns else ins[a]

        def copy(a, k, peer):
            p = _dev_index(*peer)
            return pltpu.make_async_remote_copy(
                src_ref=src_of(a, p), dst_ref=outs[a].at[me], send_sem=send_sems.at[a * 7 + k - 1],
                recv_sem=recv_sems.at[a * 7 + k - 1], device_id=peer, device_id_type=MESH)

        def arrival(a, k, peer):
            p = _dev_index(*peer)
            return pltpu.make_async_remote_copy(
                src_ref=src_of(a, p), dst_ref=outs[a].at[p], send_sem=send_sems.at[a * 7 + k - 1],
                recv_sem=recv_sems.at[a * 7 + k - 1], device_id=peer, device_id_type=MESH)

        mine = [pltpu.make_async_copy(src_of(a, me), outs[a].at[me], local_sems.at[a]) for a in range(na)]
        for cp in mine:
            cp.start()
        peers = [(k, (x ^ (k >> 2), y ^ ((k >> 1) & 1), c ^ (k & 1))) for k in range(1, NDEV)]
        sent = [copy(a, k, peer) for k, peer in peers for a in range(na)]
        for cp in sent:
            cp.start()
        for k, peer in peers:
            for a in range(na):
                arrival(a, k, peer).wait_recv()
        for cp in sent:
            cp.wait_send()
        for cp in mine:
            cp.wait()

    hbm = pl.BlockSpec(memory_space=pl.ANY)
    out_shape = [jax.ShapeDtypeStruct(s.shape, s.dtype) for s in slabs]
    out_shape += [jax.ShapeDtypeStruct((NDEV,) + w.shape, w.dtype) for w in whole]
    return pl.pallas_call(
        body, name=name, in_specs=[hbm] * (na + nb), out_specs=[hbm] * na, out_shape=out_shape,
        scratch_shapes=[pltpu.SemaphoreType.DMA((7 * na,)), pltpu.SemaphoreType.DMA((7 * na,)),
                        pltpu.SemaphoreType.DMA((na,))])(*slabs, *whole, *after)


_HBM = pl.BlockSpec(memory_space=pltpu.HBM)
_SEM = pl.BlockSpec(memory_space=pltpu.SEMAPHORE)
_EFFECT = pltpu.SideEffectType.DATAFLOW_SIDE_EFFECTING


def _peers(x, y, c):
    return [(k, (x ^ (k >> 2), y ^ ((k >> 1) & 1), c ^ (k & 1))) for k in range(1, NDEV)]


def _peer_copy(src, land, send_sems, recv_sems, a, k, dst_block, peer):
    return pltpu.make_async_remote_copy(
        src_ref=src, dst_ref=land.at[dst_block], send_sem=send_sems.at[a * 7 + k - 1],
        recv_sem=recv_sems.at[a * 7 + k - 1], device_id=peer, device_id_type=MESH)


def _send_start(arrays, slabs, name):
    na = len(arrays)
    lands = [jax.ShapeDtypeStruct(a.shape if slabs else (NDEV,) + a.shape, a.dtype) for a in arrays]

    def body(*refs):
        ins, zones = refs[:na], refs[na:2 * na]
        send_sems, recv_sems = refs[2 * na], refs[2 * na + 1]
        token = refs[-1]
        x, y, c = _coords()
        me = _dev_index(x, y, c)
        for k, peer in _peers(x, y, c):
            for a in range(na):
                src = ins[a].at[_dev_index(*peer)] if slabs else ins[a]
                _peer_copy(src, zones[a], send_sems, recv_sems, a, k, me, peer).start()
        token[...] = jnp.zeros_like(token)

    outs = pl.pallas_call(
        body, name=name,
        out_shape=(pltpu.SemaphoreType.DMA((7 * na,)), pltpu.SemaphoreType.DMA((7 * na,)),
                   *[pltpu.HBM(a.shape, a.dtype) for a in arrays], *[pltpu.HBM(l.shape, l.dtype) for l in lands],
                   jax.ShapeDtypeStruct((8, LANES), F32)),
        in_specs=[_HBM] * (2 * na), out_specs=(_SEM, _SEM, *([_HBM] * (2 * na)), pl.BlockSpec(memory_space=pltpu.VMEM)),
        input_output_aliases={i: 2 + i for i in range(2 * na)},
        compiler_params=pltpu.CompilerParams(has_side_effects=_EFFECT),
    )(*[pltpu.with_memory_space_constraint(a, pltpu.HBM) for a in arrays],
      *[pltpu.with_memory_space_constraint(lax.empty(l.shape, l.dtype), pltpu.HBM) for l in lands])
    return outs[0], outs[1], list(outs[2:2 + na]), list(outs[2 + na:2 + 2 * na]), outs[-1]


def _send_wait(started, after, slabs, name):
    send_sems, recv_sems, thru, zones, _ = started
    na = len(thru)

    def body(*refs):
        ins, lands = refs[:na], refs[na:2 * na]
        s_sems, r_sems = refs[2 * na], refs[2 * na + 1]
        x, y, c = _coords()
        for k, peer in _peers(x, y, c):
            p = _dev_index(*peer)
            for a in range(na):
                src = ins[a].at[p] if slabs else ins[a]
                cp = _peer_copy(src, lands[a], s_sems, r_sems, a, k, p, peer)
                cp.wait_send()
                cp.wait_recv()

    outs = pl.pallas_call(
        body, name=name, out_shape=tuple(pltpu.HBM(v.shape, v.dtype) for v in thru + zones),
        in_specs=[_HBM] * (2 * na) + [_SEM, _SEM, pl.BlockSpec(memory_space=pl.ANY)], out_specs=tuple([_HBM] * (2 * na)),
        input_output_aliases={i: i for i in range(2 * na)},
        compiler_params=pltpu.CompilerParams(has_side_effects=_EFFECT),
    )(*thru, *zones, send_sems, recv_sems, after)
    me = _dev_index(*_coords())
    filled = []
    for a in range(na):
        own = lax.dynamic_index_in_dim(outs[a], me, 0, keepdims=False) if slabs else outs[a]
        filled.append(lax.dynamic_update_index_in_dim(outs[na + a], own, me, 0))
    return filled


def _pack(vecs):
    parts, spans, off = [], [], 0
    for v in vecs:
        n = v.size
        pad = (-n) % LANES
        parts.append(jnp.pad(v.reshape(-1).astype(F32), (0, pad)))
        spans.append((off, n))
        off += n + pad
    return jnp.concatenate(parts).reshape(-1, LANES), spans


def _pad_lanes(v):
    v = v.reshape(1, -1)
    return jnp.pad(v, ((0, 0), (0, LANES - v.shape[1])))


def _cols_to_slabs(g):
    sh = g.shape
    g = g.reshape(sh[:-1] + (NDEV, sh[-1] // NDEV))
    return jnp.moveaxis(g, -2, 0)


def _rows_to_slabs(g):
    sh = g.shape
    g = g.reshape(sh[:-2] + (NDEV, sh[-2] // NDEV, sh[-1]))
    return jnp.moveaxis(g, -3, 0)


def _slabs_to_cols(a):
    a = jnp.moveaxis(a, 0, -2)
    return a.reshape(a.shape[:-2] + (a.shape[-2] * a.shape[-1],))


def _slabs_to_rows(a):
    a = jnp.moveaxis(a, 0, -3)
    return a.reshape(a.shape[:-3] + (a.shape[-3] * a.shape[-2], a.shape[-1]))


def _ffn_forward(x, norm_w, wup_g, wup_v, cw_g, cw_v, wdown, tag):
    h = _rms_fwd(x, norm_w, f"{tag}_norm")
    ug = _mm(h, wup_g, name=f"{tag}_up_gate")
    uv = _mm(h, wup_v, name=f"{tag}_up_val")
    f = _ffn_gate_fwd(ug, uv, cw_g, cw_v, f"{tag}_gate")
    return _mm(f, wdown, res=x, name=f"{tag}_down"), (h, ug, uv, f)


def _ffn_backward(x, saved, dout, dout_b, norm_w, wup_g, wup_v, cw_g, cw_v, wdown, tag):
    h, ug, uv, f = saved
    dwdown = _mm(f, dout_b, ta=True, name=f"{tag}_dwdown")
    df = _mm(dout_b, wdown, tb=True, name=f"{tag}_df")
    dug, duv, dcg, dcv = _ffn_gate_bwd(ug, uv, cw_g, cw_v, df, f"{tag}_gate_bwd")
    dwg = _mm(h, dug, ta=True, name=f"{tag}_dwup_gate")
    dwv = _mm(h, duv, ta=True, name=f"{tag}_dwup_val")
    dh = _mm(dug, wup_g, tb=True, name=f"{tag}_dh_gate")
    dh = _mm(duv, wup_v, tb=True, res=dh, name=f"{tag}_dh_val")
    dx, dxb, dnorm = _rms_bwd(x, norm_w, dh, dout, f"{tag}_norm_bwd")
    return dx, dxb, (jnp.concatenate([dwg, dwv], axis=1), jnp.concatenate([dcg, dcv], axis=1), dwdown, dnorm)


def kernel(x, a_norm, ssm_w_in, ssm_conv_w, ssm_conv_b, ssm_dt_bias, ssm_a_log, ssm_d, ssm_norm, ssm_w_out, kv_norm, w_kv, b_norm, att_w_q, att_w_o, ffn_norm, ffn_w_up, ffn_conv_w, ffn_w_down, final_norm, loss_target, m_a_norm, m_ssm_w_in, m_ssm_conv_w, m_ssm_conv_b, m_ssm_dt_bias, m_ssm_a_log, m_ssm_d, m_ssm_norm, m_ssm_w_out, m_kv_norm, m_w_kv, m_b_norm, m_att_w_q, m_att_w_o, m_ffn_norm, m_ffn_w_up, m_ffn_conv_w, m_ffn_w_down, m_final_norm, v_a_norm, v_ssm_w_in, v_ssm_conv_w, v_ssm_conv_b, v_ssm_dt_bias, v_ssm_a_log, v_ssm_d, v_ssm_norm, v_ssm_w_out, v_kv_norm, v_w_kv, v_b_norm, v_att_w_q, v_att_w_o, v_ffn_norm, v_ffn_w_up, v_ffn_conv_w, v_ffn_w_down, v_final_norm):
    given = dict(locals())
    xs, tgt = x[0], loss_target[0]
    s, d = xs.shape
    di = ssm_w_out.shape[1] * NDEV
    nh = ssm_dt_bias.shape[1]
    ng = SSM_N_GROUPS
    convd = di + 2 * ng * SSM_D_STATE
    f = ffn_w_down.shape[1] * NDEV
    n_att = len(ATT_PATTERNS)
    qg = ATT_HEADS_PER_GROUP * ATT_HEAD_DIM
    kg = ATT_KV_HEADS_PER_GROUP * ATT_HEAD_DIM
    kvd = n_att * kg
    assert all(w // dil == ATT_BLOCK for w, dil in ATT_PATTERNS)

    small, _ = _pack([a_norm, ssm_conv_w, ssm_conv_b, ssm_norm, ffn_conv_w])
    gat = _all_gather([ssm_w_in[0].astype(BF16), small], "gather_weights")
    first = _send_start([ssm_w_out[0].astype(BF16), ffn_w_up[0].astype(BF16), ffn_w_down[0].astype(BF16)], False,
                        "gather_ffn0_start")
    rest = _send_start([b.astype(BF16) for b in (w_kv, att_w_q[0], att_w_o[0], ffn_w_up[1], ffn_w_down[1])], False,
                       "gather_rest_start")
    w_in = _slabs_to_cols(gat[0])
    w_z, w_xbc = w_in[:, :di], w_in[:, di:di + convd]
    w_dt = jnp.pad(w_in[:, di + convd:], ((0, 0), (0, LANES - nh)))
    sm = gat[1].reshape(NDEV, -1)
    o0 = 0

    def take(shape):
        nonlocal o0
        n = math.prod(shape)
        out = sm[:, o0:o0 + n].reshape((NDEV,) + shape)
        o0 += n + (-n) % LANES
        return out
    a_norm_f = _slabs_to_cols(take(a_norm.shape)) + (first[-1][0, 0] + rest[-1][0, 0])
    conv_w_f = _slabs_to_cols(take(ssm_conv_w.shape))[0]
    conv_b_f = _slabs_to_cols(take(ssm_conv_b.shape))
    ssm_norm_f = _slabs_to_cols(take(ssm_norm.shape))
    fcw = _slabs_to_cols(take(ffn_conv_w.shape))
    fcw_g, fcw_v = fcw[:, :, :f], fcw[:, :, f:]
    dtb, alog, dsk = _pad_lanes(ssm_dt_bias), _pad_lanes(ssm_a_log), _pad_lanes(ssm_d)
    kvn, fin = kv_norm.reshape(1, d), final_norm.reshape(1, d)

    h0 = _rms_fwd(xs, a_norm_f, "a_norm")
    z = _mm(h0, w_z, name="in_z")
    xbc_pre = _mm(h0, w_xbc, name="in_xbc")
    dtr = _mm(h0, w_dt, name="in_dt")
    xbc = _conv_silu_fwd(xbc_pre, conv_w_f, conv_b_f, "ssm_conv")
    dt = _softplus_fwd(dtr, dtb, "ssm_dt")
    y, prevs = _ssd_fwd2(xbc, dt, alog, dsk, di, nh, ng, "ssd")
    yn = _gnorm_fwd(y, z, ssm_norm_f, ng, "ssm_gnorm")
    got = _send_wait(first, yn, False, "gather_ffn0_wait")
    w_out = _slabs_to_rows(got[0])
    w_up0, w_down0 = _slabs_to_cols(got[1]), _slabs_to_rows(got[2])
    x1 = _mm(yn, w_out, res=xs, name="ssm_out")
    x2, ffn0 = _ffn_forward(x1, ffn_norm[0:1], w_up0[:, :f], w_up0[:, f:], fcw_g[0], fcw_v[0], w_down0, "ffn0")
    got = _send_wait(rest, x2, False, "gather_rest_wait")
    w_kvf = _slabs_to_cols(got[0])
    w_q = _slabs_to_cols(got[1])
    w_o = _slabs_to_rows(got[2])
    w_up1, w_down1 = _slabs_to_cols(got[3]), _slabs_to_rows(got[4])
    w_up_g, w_up_v = (w_up0[:, :f], w_up1[:, :f]), (w_up0[:, f:], w_up1[:, f:])
    w_down = (w_down0, w_down1)
    hk = _rms_fwd(x2, kvn, "kv_norm")
    kv = _mm(hk, w_kvf, name="kv_proj")
    h2 = _rms_fwd(x2, b_norm, "b_norm")
    q = _mm(h2, w_q, name="q_proj")
    tabs = _rot_tables(s, 1)
    k_rot = _rot_heads(kv, tabs, kvd, 1.0, F32, "k_rot")
    att = [_attn_fwd_nat(q, k_rot, kv, tabs, g, dil, f"attn{g}") for g, (_, dil) in enumerate(ATT_PATTERNS)]
    o, ob, lse = _merge_heads([t[0] for t in att], [t[1] for t in att], "attn_merge")
    x3 = _mm(ob, w_o, res=x2, name="attn_out")
    x4, ffn1 = _ffn_forward(x3, ffn_norm[1:2], w_up_g[1], w_up_v[1], fcw_g[1], fcw_v[1], w_down[1], "ffn1")
    loss_part, dx4, dx4b, dfin = _final_loss(x4, fin, tgt, "loss_head")

    dx3, dx3b, (dwup1, dfc1, dwdown1, dfn1) = _ffn_backward(
        x3, ffn1, dx4, dx4b, ffn_norm[1:2], w_up_g[1], w_up_v[1], fcw_g[1], fcw_v[1], w_down[1], "ffn1")
    dw_o = _mm(ob, dx3b, ta=True, name="attn_dwo")
    do = _mm(dx3b, w_o, tb=True, name="attn_do")
    delta = _delta_heads(do, o, "attn_delta")
    grads = (lax.empty((n_att * qg // LANES, s, LANES), F32), lax.empty((s, kvd), F32), lax.empty((s, kvd), F32))
    for g, (_, dil) in enumerate(ATT_PATTERNS):
        grads = _attn_bwd_nat(q, k_rot, kv, do, lse, delta, tabs, grads, g, dil, f"attn{g}_bwd")
    dq, dk_rot, dv = grads
    dk = _rot_heads(dk_rot, tabs, kvd, -1.0, BF16, "k_rot_bwd")
    dw_q = _mm(h2, dq, ta=True, b_heads=True, name="q_dw")
    dh2 = _mm(dq, w_q, tb=True, a_heads=True, name="q_dh")
    dw_kv = jnp.concatenate([_mm(hk, dk, ta=True, name="k_dw"), _mm(hk, dv, ta=True, name="v_dw")], axis=1)
    dhk = _mm(dk, w_kvf[:, :kvd], tb=True, name="k_dh")
    dhk = _mm(dv, w_kvf[:, kvd:], tb=True, res=dhk, name="v_dh")
    dx2, _, db_norm = _rms_bwd(x2, b_norm, dh2, dx3, "b_norm_bwd")
    dx2, dx2b, dkv_norm = _rms_bwd(x2, kvn, dhk, dx2, "kv_norm_bwd")
    sent1 = _send_start([_cols_to_slabs(dwup1).astype(BF16), _rows_to_slabs(dwdown1).astype(BF16),
                         _cols_to_slabs(dw_kv).astype(BF16), _cols_to_slabs(dw_q).astype(BF16),
                         _rows_to_slabs(dw_o).astype(BF16)], True, "grads_late_start")
    dx1, dx1b, (dwup0, dfc0, dwdown0, dfn0) = _ffn_backward(
        x1, ffn0, dx2, dx2b, ffn_norm[0:1], w_up_g[0], w_up_v[0], fcw_g[0] + sent1[-1][0, 0], fcw_v[0], w_down[0],
        "ffn0")
    dw_out = _mm(yn, dx1b, ta=True, name="ssm_dwout")
    sent0 = _send_start([_cols_to_slabs(dwup0).astype(BF16), _rows_to_slabs(dwdown0).astype(BF16),
                         _rows_to_slabs(dw_out).astype(BF16)], True, "grads_ffn0_start")
    dyn = _mm(dx1b, w_out, tb=True, name="ssm_dyn")
    dy, dz, dssm_norm = _gnorm_bwd(dyn, y, z, ssm_norm_f + sent0[-1][0, 0], ng, "ssm_gnorm_bwd")
    dxbc, ddt, dalog, ddsk = _ssd_bwd2(xbc, dt, alog, dsk, prevs, dy, di, nh, ng, "ssd_bwd")
    ddtr, ddtb = _softplus_bwd(ddt, dtr, dtb, nh, "ssm_dt_bwd")
    dxbc_pre, dconv_w, dconv_b = _conv_silu_bwd(xbc_pre, conv_w_f, conv_b_f, dxbc, "ssm_conv_bwd")
    dw_z = _mm(h0, dz, ta=True, name="in_dwz")
    dw_xbc = _mm(h0, dxbc_pre, ta=True, name="in_dwxbc")
    dw_dt = _mm(h0, ddtr, ta=True, name="in_dwdt")[:, :nh]
    sent_m = _send_start([_cols_to_slabs(jnp.concatenate([dw_z, dw_xbc, dw_dt], axis=1)).astype(BF16)], True,
                         "grads_mamba_start")
    dh0 = _mm(dz, w_z, tb=True, name="in_dh_z")
    dh0 = _mm(dxbc_pre, w_xbc, tb=True, res=dh0, name="in_dh_xbc")
    dh0 = _mm(ddtr, w_dt, tb=True, res=dh0, name="in_dh_dt")
    dx0, _, da_norm = _rms_bwd(xs, a_norm_f + sent_m[-1][0, 0], dh0, dx1, "a_norm_bwd")

    small_full = {
        'a_norm': da_norm, 'ssm_conv_w': dconv_w[None], 'ssm_conv_b': dconv_b, 'ssm_dt_bias': ddtb[:, :nh],
        'ssm_a_log': dalog[:, :nh], 'ssm_d': ddsk[:, :nh], 'ssm_norm': dssm_norm, 'kv_norm': dkv_norm.reshape(d),
        'b_norm': db_norm, 'ffn_norm': jnp.concatenate([dfn0, dfn1], axis=0), 'ffn_conv_w': jnp.stack([dfc0, dfc1]),
        'final_norm': dfin.reshape(d),
    }
    small_names = list(small_full)
    packed, spans = _pack([small_full[n] for n in small_names])
    got1 = _send_wait(sent1, dx0, True, "grads_late_wait")
    got0 = _send_wait(sent0, dx0, True, "grads_ffn0_wait")
    recv_big = {
        'w_kv': got1[2], 'att_w_q': got1[3], 'att_w_o': got1[4],
        'ffn_w_up': jnp.concatenate([got0[0], got1[0]], axis=1),
        'ffn_w_down': jnp.concatenate([got0[1], got1[1]], axis=1),
    }

    me = _dev_index(*_coords())
    res = {}

    def update_big(n, r):
        w = given[n]
        c = w.shape[-1]
        outs = _adamw(r.reshape(NDEV, -1, c), w.reshape(-1, c), given['m_' + n].reshape(-1, c),
                      given['v_' + n].reshape(-1, c), f"adamw_{n}")
        res[n] = [o_.reshape(w.shape) for o_ in outs]
    for n, r in recv_big.items():
        update_big(n, r)
    update_big('ssm_w_out', got0[2])
    recv = _exchange([], [packed], "exchange_grads", after=[res[n][1] for n in res])
    small_sum = _sum_slabs(recv[-1], "sum_small_grads").reshape(-1)
    gotm = _send_wait(sent_m, recv[-1], True, "grads_mamba_wait")
    update_big('ssm_w_in', gotm[0])
    sharded_small = {'a_norm', 'ssm_conv_w', 'ssm_conv_b', 'ssm_norm', 'ffn_conv_w'}
    for n, (off, size) in zip(small_names, spans):
        w = given[n]
        gfull = small_sum[off:off + size].reshape(small_full[n].shape)
        if n in sharded_small:
            c = w.shape[-1]
            gfull = lax.dynamic_slice_in_dim(gfull, me * c, c, axis=gfull.ndim - 1)
        c = w.shape[-1]
        outs = _adamw(gfull.reshape(1, -1, c), w.reshape(-1, c), given['m_' + n].reshape(-1, c),
                      given['v_' + n].reshape(-1, c), f"adamw_{n}")
        res[n] = [o_.reshape(w.shape) for o_ in outs]

    loss = lax.psum(loss_part[0, 0], AXES)
    return (loss, dx0[None], *[res[n][0] for n in WEIGHTS], *[res[n][1] for n in WEIGHTS],
            *[res[n][2] for n in WEIGHTS], *[res[n][3] for n in WEIGHTS])
```

```python
import functools
import math

import jax
import jax.numpy as jnp
from jax import lax
from jax.experimental import pallas as pl
from jax.experimental.pallas import tpu as pltpu

F32, BF16 = jnp.float32, jnp.bfloat16
AXES = ("x", "y", "c")
NDEV = 8
MESH = pl.DeviceIdType.MESH
HIGHEST = lax.Precision.HIGHEST

LANES = 128
SUBLANES = 8
VMEM_LIMIT_BYTES = 48 * 1024 * 1024
VMEM_LIMIT_ATTN_BWD_BYTES = 58 * 1024 * 1024

RMS_EPS = 1e-6
GATED_NORM_EPS = 1e-5
SSM_HEAD_DIM = 64
SSM_N_GROUPS = 8
SSM_D_STATE = 128
SSM_CONV = 4
SSM_CHUNK = 128
ATT_PATTERNS = ((128, 1), (512, 4), (2048, 16))
ATT_HEAD_DIM = 128
ATT_HEADS_PER_GROUP = 8
ATT_KV_HEADS_PER_GROUP = 2
ATT_BLOCK = 128
ROPE_DIM = ATT_HEAD_DIM // 4
ROPE_THETA = 500000.0
FFN_CONV = 3
ADAM_LR = 0.001
ADAM_B1 = 0.9
ADAM_B2 = 0.999
ADAM_EPS = 1e-08
ADAM_WD = 0.01
ADAM_STEP = 10
NEG = -1e30

WEIGHTS = ['a_norm', 'ssm_w_in', 'ssm_conv_w', 'ssm_conv_b', 'ssm_dt_bias', 'ssm_a_log', 'ssm_d', 'ssm_norm',
           'ssm_w_out', 'kv_norm', 'w_kv', 'b_norm', 'att_w_q', 'att_w_o', 'ffn_norm', 'ffn_w_up', 'ffn_conv_w',
           'ffn_w_down', 'final_norm']


def _params(sem=None, vmem=VMEM_LIMIT_BYTES):
    kw = dict(vmem_limit_bytes=vmem)
    if sem is not None:
        kw["dimension_semantics"] = sem
    return pltpu.CompilerParams(**kw)


def _pick(n, pref):
    if n <= pref:
        return n
    t = (pref // LANES) * LANES
    while t >= LANES:
        if n % t == 0:
            return t
        t -= LANES
    return n


def _dot(a, b, dims=(((1,), (0,)), ((), ())), precision=None):
    return lax.dot_general(a, b, dims, precision=precision, preferred_element_type=F32)


_NT = (((1,), (1,)), ((), ()))
_TN = (((0,), (0,)), ((), ()))


def _mm(a, b, *, ta=False, tb=False, res=None, out_dtype=None, name, tm=1408, tn=1408, tk=2048,
        a_heads=False, b_heads=False):
    assert not (a_heads and ta) and not (b_heads and tb)
    a_parts = a.ndim == 3 and not a_heads
    b_parts = b.ndim == 3 and not b_heads
    assert not (a_parts and ta) and not (b_parts and tb)
    if out_dtype is None:
        out_dtype = BF16 if ta else F32
    if a_heads:
        m, k = a.shape[1], a.shape[0] * LANES
    elif a_parts:
        m, k = a.shape[1], a.shape[0] * a.shape[2]
    else:
        m = a.shape[1] if ta else a.shape[0]
        k = a.shape[0] if ta else a.shape[1]
    if b_heads:
        n, kb = b.shape[0] * LANES, b.shape[1]
    elif b_parts:
        n, kb = b.shape[0] * b.shape[2], b.shape[1]
    else:
        n = b.shape[0] if tb else b.shape[1]
        kb = b.shape[1] if tb else b.shape[0]
    assert k == kb
    tm = _pick(m, tm)
    tn = _pick(b.shape[2], tn) if b_parts else _pick(n, tn)
    tk = _pick(a.shape[2], tk) if a_parts else _pick(k, tk)
    nk = k // tk
    if a_heads:
        a_spec = pl.BlockSpec((tk // LANES, tm, LANES), lambda i, j, l: (l, i, 0))
    elif a_parts:
        per = a.shape[2] // tk
        a_spec = pl.BlockSpec((None, tm, tk), lambda i, j, l: (l // per, i, l % per))
    elif ta:
        a_spec = pl.BlockSpec((tk, tm), lambda i, j, l: (l, i))
    else:
        a_spec = pl.BlockSpec((tm, tk), lambda i, j, l: (i, l))
    if b_heads:
        b_spec = pl.BlockSpec((tn // LANES, tk, LANES), lambda i, j, l: (j, l, 0))
    elif b_parts:
        per_n = b.shape[2] // tn
        b_spec = pl.BlockSpec((None, tk, tn), lambda i, j, l: (j // per_n, l, j % per_n))
    elif tb:
        b_spec = pl.BlockSpec((tn, tk), lambda i, j, l: (j, l))
    else:
        b_spec = pl.BlockSpec((tk, tn), lambda i, j, l: (l, j))
    o_spec = pl.BlockSpec((tm, tn), lambda i, j, l: (i, j))
    dims = (((0 if ta else 1,), (1 if tb else 0,)), ((), ()))
    has_res = res is not None

    def load(ref, heads):
        if not heads:
            return ref[...].astype(BF16)
        return jnp.concatenate([ref[i].astype(BF16) for i in range(ref.shape[0])], axis=1)

    def body(*refs):
        a_ref, b_ref = refs[:2]
        r_ref = refs[2] if has_res else None
        o_ref = refs[2 + has_res]
        p = _dot(load(a_ref, a_heads), load(b_ref, b_heads), dims)

        def finish(r):
            if has_res:
                r = r + r_ref[...]
            o_ref[...] = r.astype(o_ref.dtype)

        if nk == 1:
            finish(p)
            return
        acc = refs[3 + has_res]
        l = pl.program_id(2)

        @pl.when(l == 0)
        def _():
            acc[...] = p

        @pl.when(jnp.logical_and(l > 0, l < nk - 1))
        def _():
            acc[...] += p

        @pl.when(l == nk - 1)
        def _():
            finish(acc[...] + p)

    ins = [a, b] + ([res] if has_res else [])
    in_specs = [a_spec, b_spec] + ([o_spec] if has_res else [])
    return pl.pallas_call(
        body, name=name, grid=(m // tm, n // tn, nk), in_specs=in_specs, out_specs=o_spec,
        out_shape=jax.ShapeDtypeStruct((m, n), out_dtype),
        scratch_shapes=[pltpu.VMEM((tm, tn), F32)] if nk > 1 else [],
        compiler_params=_params(("parallel", "parallel", "arbitrary")))(*ins)


def _rowwise(fn, rows, bcasts, outs, accs=(), *, tile, name):
    s = (rows[0][0] if isinstance(rows[0], tuple) else rows[0]).shape[-2]
    tile = min(tile, s)
    n_val = len(rows) + len(bcasts)
    intos = [(k, o) for k, o in enumerate(outs) if len(o) == 4]
    n_in, n_out, n_acc = n_val + len(intos), len(outs), len(accs)

    def row_spec(c):
        if isinstance(c, tuple):
            return pl.BlockSpec((c[0], tile, c[1]), lambda i: (0, i, 0))
        return pl.BlockSpec((tile, c), lambda i: (i, 0))

    def row_shape(c):
        return (c[0], s, c[1]) if isinstance(c, tuple) else (s, c)

    def window(width, cb):
        return pl.BlockSpec((tile, width), lambda i: (i, cb))

    def body(*refs):
        vals = fn(*[r[...] for r in refs[:n_val]])
        o_refs = refs[n_in:n_in + n_out]
        a_refs = refs[n_in + n_out:]
        for r, v in zip(o_refs, vals[:n_out]):
            if isinstance(v, list):
                for i, vi in enumerate(v):
                    r[i] = vi.astype(r.dtype)
            else:
                r[...] = v.astype(r.dtype)

        @pl.when(pl.program_id(0) == 0)
        def _():
            for r in a_refs:
                r[...] = jnp.zeros(r.shape, r.dtype)

        for r, v in zip(a_refs, vals[n_out:]):
            r[...] += v

    in_specs = [window(r[1], r[2]) if isinstance(r, tuple)
                else row_spec(r.shape[1] if r.ndim == 2 else (r.shape[0], r.shape[2])) for r in rows]
    in_specs += [pl.BlockSpec(b.shape, lambda i: (0, 0)) for b in bcasts]
    in_specs += [pl.BlockSpec(memory_space=pl.ANY) for _ in intos]
    out_specs = [window(o[0], o[3]) if len(o) == 4 else row_spec(o[0]) for o in outs]
    out_specs += [pl.BlockSpec(sh, lambda i: (0, 0)) for sh, _ in accs]
    out_shape = [jax.ShapeDtypeStruct(o[2].shape, o[2].dtype) if len(o) == 4
                 else jax.ShapeDtypeStruct(row_shape(o[0]), o[1]) for o in outs]
    out_shape += [jax.ShapeDtypeStruct(sh, dt) for sh, dt in accs]
    args = [r[0] if isinstance(r, tuple) else r for r in rows] + list(bcasts) + [o[2] for _, o in intos]
    return pl.pallas_call(body, name=name, grid=(s // tile,), in_specs=in_specs, out_specs=out_specs,
                          out_shape=out_shape, input_output_aliases={n_val + i: k for i, (k, _) in enumerate(intos)},
                          compiler_params=_params(("arbitrary",)))(*args)


def _rms_fwd(x, w, name):
    def fn(x, w):
        r = lax.rsqrt(jnp.mean(x * x, axis=-1, keepdims=True) + RMS_EPS)
        return (x * r * w,)
    return _rowwise(fn, [x], [w], [(x.shape[1], BF16)], tile=256, name=name)[0]


def _rms_bwd(x, w, dh, dres, name):
    def fn(x, dh, dres, w):
        r = lax.rsqrt(jnp.mean(x * x, axis=-1, keepdims=True) + RMS_EPS)
        xh = x * r
        dxh = dh * w
        dx = dres + r * (dxh - xh * jnp.mean(dxh * xh, axis=-1, keepdims=True))
        return dx, dx, jnp.sum(dh * xh, axis=0, keepdims=True)
    d = x.shape[1]
    return _rowwise(fn, [x, dh, dres], [w], [(d, F32), (d, BF16)], [((1, d), F32)], tile=256, name=name)


def _final_loss(x, w, tgt, name):
    d = x.shape[1]

    def fn(x, t, w):
        r = lax.rsqrt(jnp.mean(x * x, axis=-1, keepdims=True) + RMS_EPS)
        xh = x * r
        err = xh * w - t
        part = jnp.sum(jnp.mean(err * err, axis=-1, keepdims=True), axis=0, keepdims=True) * 0.5
        dy = err * (1.0 / d)
        dxh = dy * w
        dx = r * (dxh - xh * jnp.mean(dxh * xh, axis=-1, keepdims=True))
        return dx, dx, part, jnp.sum(dy * xh, axis=0, keepdims=True)
    dx, dxb, part, dw = _rowwise(fn, [x, tgt], [w], [(d, F32), (d, BF16)], [((1, 1), F32), ((1, d), F32)],
                                 tile=256, name=name)
    return part, dx, dxb, dw


def _softplus_fwd(dtr, bias, name):
    def fn(r, b):
        v = r + b
        return (jnp.maximum(v, 0.0) + jnp.log(1.0 + jnp.exp(-jnp.abs(v))),)
    return _rowwise(fn, [dtr], [bias], [(LANES, F32)], tile=512, name=name)[0]


def _softplus_bwd(ddt, dtr, bias, n_heads, into, name):
    def fn(g, r, b):
        lane = lax.broadcasted_iota(jnp.int32, g.shape, 1)
        d = jnp.where(lane < n_heads, g * jax.nn.sigmoid(r + b), 0.0)
        return d, jnp.sum(d, axis=0, keepdims=True)
    return _rowwise(fn, [ddt, dtr], [bias], [(LANES, BF16, *into)], [((1, LANES), F32)], tile=512, name=name)


def _gnorm_fwd(y, z, w, n_groups, name):
    di = y.shape[1]
    gs = di // n_groups

    def fn(y, z, w):
        y2 = y * (z * jax.nn.sigmoid(z))
        out = []
        for g in range(n_groups):
            sl = y2[:, g * gs:(g + 1) * gs]
            r = lax.rsqrt(jnp.mean(sl * sl, axis=-1, keepdims=True) + GATED_NORM_EPS)
            out.append(sl * r)
        return (jnp.concatenate(out, axis=1) * w,)
    return _rowwise(fn, [y, z], [w], [(di, BF16)], tile=256, name=name)[0]


def _gnorm_bwd(dyn, y, z, w, n_groups, into, name):
    di = y.shape[1]
    gs = di // n_groups

    def fn(dyn, y, z, w):
        sig = jax.nn.sigmoid(z)
        sz = z * sig
        y2 = y * sz
        d2n = dyn * w
        dy2, yhat = [], []
        for g in range(n_groups):
            sl = y2[:, g * gs:(g + 1) * gs]
            dg = d2n[:, g * gs:(g + 1) * gs]
            r = lax.rsqrt(jnp.mean(sl * sl, axis=-1, keepdims=True) + GATED_NORM_EPS)
            yh = sl * r
            dy2.append(r * (dg - yh * jnp.mean(dg * yh, axis=-1, keepdims=True)))
            yhat.append(yh)
        dy2 = jnp.concatenate(dy2, axis=1)
        yhat = jnp.concatenate(yhat, axis=1)
        dz = dy2 * y * (sig * (1.0 + z * (1.0 - sig)))
        return dy2 * sz, dz, jnp.sum(dyn * yhat, axis=0, keepdims=True)
    return _rowwise(fn, [dyn, y, z], [w], [(di, F32), (di, BF16, *into)], [((1, di), F32)], tile=128, name=name)


def _merge_fwd(os_, lses, name):
    n = len(os_)

    def fn(*v):
        o, l = v[:n], v[n:]
        m = functools.reduce(jnp.maximum, l)
        e = [jnp.exp(li - m) for li in l]
        tot = functools.reduce(jnp.add, e)
        acc = functools.reduce(jnp.add, [ei * oi for ei, oi in zip(e, o)]) / tot
        return acc, acc, m + jnp.log(tot)
    c = os_[0].shape[1]
    return _rowwise(fn, list(os_) + list(lses), [], [(c, F32), (c, BF16), (c, F32)], tile=256, name=name)


def _delta(do, o, name):
    c = o.shape[1]

    def fn(do, o):
        p = do * o
        out = [jnp.broadcast_to(jnp.sum(p[:, j:j + ATT_HEAD_DIM], axis=-1, keepdims=True), (p.shape[0], ATT_HEAD_DIM))
               for j in range(0, c, ATT_HEAD_DIM)]
        return (jnp.concatenate(out, axis=1),)
    return _rowwise(fn, [do, o], [], [(c, F32)], tile=256, name=name)[0]


def _lane_place(cols):
    rows = cols[0].shape[0]
    lane = lax.broadcasted_iota(jnp.int32, (rows, LANES), 1)
    out = jnp.zeros((rows, LANES), F32)
    for j, c in enumerate(cols):
        out = jnp.where(lane == j, c, out)
    return out


def _merge_heads(os_, lses, name):
    n = len(os_)
    n_kv, rep, hd = ATT_KV_HEADS_PER_GROUP, ATT_HEADS_PER_GROUP // ATT_KV_HEADS_PER_GROUP, ATT_HEAD_DIM

    def fn(*v):
        o, l = v[:n], v[n:]
        out, lse = [], []
        for h in range(n_kv):
            cols = []
            for j in range(rep):
                hh = h * rep + j
                lg = [li[h][:, j:j + 1] for li in l]
                m = functools.reduce(jnp.maximum, lg)
                e = [jnp.exp(x - m) for x in lg]
                tot = functools.reduce(jnp.add, e)
                acc = functools.reduce(jnp.add, [ei * oi[hh] for ei, oi in zip(e, o)])
                out.append(acc / tot)
                cols.append(m + jnp.log(tot))
            lse.append(_lane_place(cols))
        merged = jnp.concatenate(out, axis=1)
        return merged, merged, lse
    c = os_[0].shape[0] * hd
    return _rowwise(fn, list(os_) + list(lses), [], [(c, F32), (c, BF16), ((n_kv, LANES), F32)], tile=256, name=name)


def _delta_heads(do, o, name):
    n_kv, rep, hd = ATT_KV_HEADS_PER_GROUP, ATT_HEADS_PER_GROUP // ATT_KV_HEADS_PER_GROUP, ATT_HEAD_DIM

    def fn(do, o):
        p = do * o
        return ([_lane_place([jnp.sum(p[:, (h * rep + j) * hd:(h * rep + j + 1) * hd], axis=-1, keepdims=True)
                              for j in range(rep)]) for h in range(n_kv)],)
    return _rowwise(fn, [do, o], [], [((n_kv, LANES), F32)], tile=256, name=name)[0]


def _sum_slabs(recv, name):
    def body(r_ref, o_ref):
        acc = r_ref[0]
        for k in range(1, NDEV):
            acc = acc + r_ref[k]
        o_ref[...] = acc
    return pl.pallas_call(body, name=name, out_shape=jax.ShapeDtypeStruct(recv.shape[1:], F32),
                          compiler_params=_params())(recv)


def _shift_down(x, k):
    if k == 0:
        return x
    r = pltpu.roll(x, k, 0)
    row = lax.broadcasted_iota(jnp.int32, (SUBLANES, x.shape[1]), 0)
    return jnp.concatenate([jnp.where(row >= k, r[:SUBLANES], 0.0), r[SUBLANES:]], axis=0)


def _shift_up(x, k):
    if k == 0:
        return x
    s = x.shape[0]
    r = pltpu.roll(x, s - k, 0)
    row = lax.broadcasted_iota(jnp.int32, (SUBLANES, x.shape[1]), 0)
    return jnp.concatenate([r[:s - SUBLANES], jnp.where(row < SUBLANES - k, r[s - SUBLANES:], 0.0)], axis=0)


def _conv(x, w):
    kw = w.shape[0]
    return functools.reduce(jnp.add, [w[k:k + 1, :] * _shift_down(x, kw - 1 - k) for k in range(kw)])


def _conv_t(dy, w):
    kw = w.shape[0]
    return functools.reduce(jnp.add, [w[k:k + 1, :] * _shift_up(dy, kw - 1 - k) for k in range(kw)])


def _conv_dw(x, dy, dw_ref):
    kw = dw_ref.shape[0]
    for k in range(kw):
        dw_ref[k:k + 1, :] = jnp.sum(dy * _shift_down(x, kw - 1 - k), axis=0, keepdims=True)


def _dsilu(pre):
    sig = jax.nn.sigmoid(pre)
    return sig * (1.0 + pre * (1.0 - sig))


def _col_specs(s, c, kw, tc):
    return (pl.BlockSpec((s, tc), lambda j: (0, j)), pl.BlockSpec((kw, tc), lambda j: (0, j)),
            pl.BlockSpec((1, tc), lambda j: (0, j)))


def _conv_silu_fwd(x, col0, w, b, name):
    s, c = x.shape[0], w.shape[1]
    tc = LANES
    xs, ws, bs = _col_specs(s, c, w.shape[0], tc)
    xwin = pl.BlockSpec((s, tc), lambda j: (0, j + col0 // tc))

    def body(x_ref, w_ref, b_ref, o_ref):
        pre = _conv(x_ref[...], w_ref[...]) + b_ref[...]
        o_ref[...] = pre * jax.nn.sigmoid(pre)
    return pl.pallas_call(body, name=name, grid=(c // tc,), in_specs=[xwin, ws, bs], out_specs=xs,
                          out_shape=jax.ShapeDtypeStruct((s, c), F32), compiler_params=_params(("parallel",)))(x, w, b)


def _conv_silu_bwd(x, col0, w, b, dy, into, name):
    s, c = x.shape[0], w.shape[1]
    tc = LANES
    xs, ws, bs = _col_specs(s, c, w.shape[0], tc)
    xwin = pl.BlockSpec((s, tc), lambda j: (0, j + col0 // tc))

    def body(x_ref, w_ref, b_ref, dy_ref, _, dx_ref, dw_ref, db_ref):
        xv, wv = x_ref[...], w_ref[...]
        pre = _conv(xv, wv) + b_ref[...]
        dpre = dy_ref[...] * _dsilu(pre)
        dx_ref[...] = _conv_t(dpre, wv).astype(dx_ref.dtype)
        _conv_dw(xv, dpre, dw_ref)
        db_ref[...] = jnp.sum(dpre, axis=0, keepdims=True)
    return pl.pallas_call(
        body, name=name, grid=(c // tc,), in_specs=[xwin, ws, bs, xs, pl.BlockSpec(memory_space=pl.ANY)],
        out_specs=[xwin, ws, bs], input_output_aliases={4: 0},
        out_shape=[jax.ShapeDtypeStruct(into.shape, into.dtype), jax.ShapeDtypeStruct(w.shape, F32),
                   jax.ShapeDtypeStruct((1, c), F32)],
        compiler_params=_params(("parallel",)))(x, w, b, dy, into)


def _gate_specs(s, f, kw):
    nt = f // LANES
    return (pl.BlockSpec((s, LANES), lambda j: (0, j)), pl.BlockSpec((s, LANES), lambda j: (0, j + nt)),
            pl.BlockSpec((kw, LANES), lambda j: (0, j)), pl.BlockSpec((kw, LANES), lambda j: (0, j + nt)))


def _ffn_gate_fwd2(u, w, name):
    s, f = u.shape[0], u.shape[1] // 2
    gs, vs, wgs, wvs = _gate_specs(s, f, w.shape[0])

    def body(g_ref, v_ref, wg_ref, wv_ref, o_ref):
        g = _conv(g_ref[...], wg_ref[...])
        v = _conv(v_ref[...], wv_ref[...])
        o_ref[...] = (g * jax.nn.sigmoid(g) * v).astype(o_ref.dtype)
    return pl.pallas_call(body, name=name, grid=(f // LANES,), in_specs=[gs, vs, wgs, wvs], out_specs=gs,
                          out_shape=jax.ShapeDtypeStruct((s, f), BF16),
                          compiler_params=_params(("parallel",)))(u, u, w, w)


def _ffn_gate_bwd2(u, w, df, name):
    s, f = u.shape[0], u.shape[1] // 2
    kw = w.shape[0]
    gs, vs, wgs, wvs = _gate_specs(s, f, kw)

    def body(g_ref, v_ref, wg_ref, wv_ref, df_ref, du_ref, dw_ref):
        gp, vp, wgv, wvv = g_ref[...], v_ref[...], wg_ref[...], wv_ref[...]
        g = _conv(gp, wgv)
        v = _conv(vp, wvv)
        dfv = df_ref[...]
        dg = dfv * v * _dsilu(g)
        dv = dfv * (g * jax.nn.sigmoid(g))
        du_ref[0] = _conv_t(dg, wgv).astype(du_ref.dtype)
        du_ref[1] = _conv_t(dv, wvv).astype(du_ref.dtype)
        _conv_dw(gp, dg, dw_ref.at[0])
        _conv_dw(vp, dv, dw_ref.at[1])
    return pl.pallas_call(
        body, name=name, grid=(f // LANES,), in_specs=[gs, vs, wgs, wvs, gs],
        out_specs=[pl.BlockSpec((2, s, LANES), lambda j: (0, 0, j)), pl.BlockSpec((2, kw, LANES), lambda j: (0, 0, j))],
        out_shape=[jax.ShapeDtypeStruct((2, s, f), BF16), jax.ShapeDtypeStruct((2, kw, f), F32)],
        compiler_params=_params(("parallel",)))(u, u, w, w, df)


def _ffn_gate_fwd(ug, uv, wg, wv, name):
    s, c = ug.shape
    tc = LANES
    xs, ws, _ = _col_specs(s, c, wg.shape[0], tc)

    def body(g_ref, v_ref, wg_ref, wv_ref, o_ref):
        g = _conv(g_ref[...], wg_ref[...])
        v = _conv(v_ref[...], wv_ref[...])
        o_ref[...] = (g * jax.nn.sigmoid(g) * v).astype(o_ref.dtype)
    return pl.pallas_call(body, name=name, grid=(c // tc,), in_specs=[xs, xs, ws, ws], out_specs=xs,
                          out_shape=jax.ShapeDtypeStruct((s, c), BF16),
                          compiler_params=_params(("parallel",)))(ug, uv, wg, wv)


def _ffn_gate_bwd(ug, uv, wg, wv, df, name):
    s, c = ug.shape
    tc = LANES
    xs, ws, _ = _col_specs(s, c, wg.shape[0], tc)

    def body(g_ref, v_ref, wg_ref, wv_ref, df_ref, dg_ref, dv_ref, dwg_ref, dwv_ref):
        gp, vp, wgv, wvv = g_ref[...], v_ref[...], wg_ref[...], wv_ref[...]
        g = _conv(gp, wgv)
        v = _conv(vp, wvv)
        dfv = df_ref[...]
        dg = dfv * v * _dsilu(g)
        dv = dfv * (g * jax.nn.sigmoid(g))
        dg_ref[...] = _conv_t(dg, wgv).astype(dg_ref.dtype)
        dv_ref[...] = _conv_t(dv, wvv).astype(dv_ref.dtype)
        _conv_dw(gp, dg, dwg_ref)
        _conv_dw(vp, dv, dwv_ref)
    return pl.pallas_call(
        body, name=name, grid=(c // tc,), in_specs=[xs, xs, ws, ws, xs], out_specs=[xs, xs, ws, ws],
        out_shape=[jax.ShapeDtypeStruct((s, c), BF16), jax.ShapeDtypeStruct((s, c), BF16),
                   jax.ShapeDtypeStruct(wg.shape, F32), jax.ShapeDtypeStruct(wv.shape, F32)],
        compiler_params=_params(("parallel",)))(ug, uv, wg, wv, df)


def _ssd_common(dt, alog, n_heads):
    ln = dt.shape[0]
    lane = lax.broadcasted_iota(jnp.int32, (1, LANES), 1)
    a = jnp.where(lane < n_heads, -jnp.exp(alog), 0.0)
    row = lax.broadcasted_iota(jnp.int32, (ln, ln), 0)
    col = lax.broadcasted_iota(jnp.int32, (ln, ln), 1)
    tril = col <= row
    acs = _dot(tril.astype(F32), dt * a, precision=HIGHEST)
    return a, acs, acs.T, tril


def _ssd_fwd(xbc, dt, alog, dskip, di, n_heads, n_groups, name):
    s, convd = xbc.shape
    ln, p, ns = SSM_CHUNK, SSM_HEAD_DIM, SSM_D_STATE
    nc, hg = s // ln, n_heads // n_groups

    def body(x_ref, dt_ref, alog_ref, d_ref, y_ref, prev_ref, st):
        @pl.when(pl.program_id(0) == 0)
        def _():
            st[...] = jnp.zeros(st.shape, F32)

        dt = dt_ref[...]
        _, acs, acs_t, tril = _ssd_common(dt, alog_ref[...], n_heads)
        e_all = jnp.exp(acs)
        last = acs[ln - 1:ln, :]
        ds_all = jnp.exp(last - acs)
        t_all = jnp.exp(last)
        dsk = d_ref[...]
        for g in range(n_groups):
            bg = x_ref[:, di + g * ns:di + (g + 1) * ns].astype(BF16)
            cg = x_ref[:, di + (n_groups + g) * ns:di + (n_groups + g + 1) * ns].astype(BF16)
            gm = _dot(cg, bg, _NT)
            for j in range(hg):
                h = g * hg + j
                xh = x_ref[:, h * p:(h + 1) * p]
                xdt = xh * dt[:, h:h + 1]
                seg = acs[:, h:h + 1] - acs_t[h:h + 1, :]
                m = jnp.where(tril, gm * jnp.exp(jnp.where(tril, seg, 0.0)), 0.0)
                prev = st[h]
                prev_ref[0, h] = prev
                y = _dot(m.astype(BF16), xdt.astype(BF16))
                y = y + _dot(cg, prev.astype(BF16), _NT) * e_all[:, h:h + 1]
                y = y + xh * dsk[:, h:h + 1]
                snew = _dot((xdt * ds_all[:, h:h + 1]).astype(BF16), bg, _TN)
                st[h] = prev * t_all[:, h:h + 1] + snew
                y_ref[:, h * p:(h + 1) * p] = y

    vec = pl.BlockSpec((1, LANES), lambda c: (0, 0))
    return pl.pallas_call(
        body, name=name, grid=(nc,),
        in_specs=[pl.BlockSpec((ln, convd), lambda c: (c, 0)), pl.BlockSpec((ln, LANES), lambda c: (c, 0)), vec, vec],
        out_specs=[pl.BlockSpec((ln, di), lambda c: (c, 0)),
                   pl.BlockSpec((1, n_heads, p, ns), lambda c: (c, 0, 0, 0))],
        out_shape=[jax.ShapeDtypeStruct((s, di), F32), jax.ShapeDtypeStruct((nc, n_heads, p, ns), F32)],
        scratch_shapes=[pltpu.VMEM((n_heads, p, ns), F32)],
        compiler_params=_params(("arbitrary",)))(xbc, dt, alog, dskip)


def _ssd_bwd(xbc, dt, alog, dskip, prev_all, dy, di, n_heads, n_groups, name):
    s, convd = xbc.shape
    ln, p, ns = SSM_CHUNK, SSM_HEAD_DIM, SSM_D_STATE
    nc, hg = s // ln, n_heads // n_groups

    def body(x_ref, dt_ref, alog_ref, d_ref, prev_ref, dy_ref, dx_ref, ddt_ref, da_ref, dd_ref, dh):
        step = pl.program_id(0)

        @pl.when(step == 0)
        def _():
            dh[...] = jnp.zeros(dh.shape, F32)
            da_ref[...] = jnp.zeros(da_ref.shape, F32)
            dd_ref[...] = jnp.zeros(dd_ref.shape, F32)

        dt = dt_ref[...]
        a, acs, acs_t, tril = _ssd_common(dt, alog_ref[...], n_heads)
        e_all = jnp.exp(acs)
        last = acs[ln - 1:ln, :]
        ds_all = jnp.exp(last - acs)
        t_all = jnp.exp(last)
        dsk = d_ref[...]
        lane = lax.broadcasted_iota(jnp.int32, (ln, LANES), 1)
        lane1 = lax.broadcasted_iota(jnp.int32, (1, LANES), 1)
        sub = lax.broadcasted_iota(jnp.int32, (LANES, ln), 0)
        rowi = lax.broadcasted_iota(jnp.int32, (ln, LANES), 0)
        dacs_c = jnp.zeros((ln, LANES), F32)
        dacs_r = jnp.zeros((LANES, ln), F32)
        dlast = jnp.zeros((1, LANES), F32)
        ddt_x = jnp.zeros((ln, LANES), F32)
        dd = jnp.zeros((1, LANES), F32)

        def tot(v):
            return jnp.sum(jnp.sum(v, axis=1, keepdims=True), axis=0, keepdims=True)

        for g in range(n_groups):
            bg = x_ref[:, di + g * ns:di + (g + 1) * ns].astype(BF16)
            cg = x_ref[:, di + (n_groups + g) * ns:di + (n_groups + g + 1) * ns].astype(BF16)
            gm = _dot(cg, bg, _NT)
            dgm = jnp.zeros((ln, ln), F32)
            dcg = jnp.zeros((ln, ns), F32)
            dbg = jnp.zeros((ln, ns), F32)
            for j in range(hg):
                h = g * hg + j
                xh = x_ref[:, h * p:(h + 1) * p]
                dth = dt[:, h:h + 1]
                xdt = xh * dth
                dyh = dy_ref[:, h * p:(h + 1) * p]
                eh, dsh, th = e_all[:, h:h + 1], ds_all[:, h:h + 1], t_all[:, h:h + 1]
                seg = acs[:, h:h + 1] - acs_t[h:h + 1, :]
                dec = jnp.where(tril, jnp.exp(jnp.where(tril, seg, 0.0)), 0.0)
                m = gm * dec
                prev = prev_ref[0, h]
                dhn = dh[h]
                prevb, dhb, dyb, xdtb = prev.astype(BF16), dhn.astype(BF16), dyh.astype(BF16), xdt.astype(BF16)
                yo = _dot(cg, prevb, _NT)
                dyob = (dyh * eh).astype(BF16)
                c_col = jnp.sum(dyh * yo, axis=1, keepdims=True) * eh
                dcg = dcg + _dot(dyob, prevb)
                dprev = th * dhn + _dot(dyob, cg, _TN)
                dtt = tot(dhn * prev)
                w = _dot(bg, dhb, _NT)
                dxdt = w * dsh
                dds = jnp.sum(w * xdt, axis=1, keepdims=True)
                dbg = dbg + _dot((xdt * dsh).astype(BF16), dhb)
                dm = _dot(dyb, xdtb, _NT)
                dxdt = dxdt + _dot(m.astype(BF16), dyb, _TN)
                dgm = dgm + dm * dec
                q = dm * m
                c_col = c_col + jnp.sum(q, axis=1, keepdims=True) - dds * dsh
                r_row = -jnp.sum(q, axis=0, keepdims=True)
                dlast_h = tot(dds * dsh) + dtt * th
                dacs_c = dacs_c + jnp.where(lane == h, c_col, 0.0)
                dacs_r = dacs_r + jnp.where(sub == h, r_row, 0.0)
                dlast = dlast + jnp.where(lane1 == h, dlast_h, 0.0)
                ddt_x = ddt_x + jnp.where(lane == h, jnp.sum(dxdt * xh, axis=1, keepdims=True), 0.0)
                dd = dd + jnp.where(lane1 == h, tot(dyh * xh), 0.0)
                dx_ref[:, h * p:(h + 1) * p] = dxdt * dth + dyh * dsk[:, h:h + 1]
                dh[h] = dprev
            dgb = dgm.astype(BF16)
            dx_ref[:, di + g * ns:di + (g + 1) * ns] = dbg + _dot(dgb, cg, _TN)
            dx_ref[:, di + (n_groups + g) * ns:di + (n_groups + g + 1) * ns] = dcg + _dot(dgb, bg)

        dacs = dacs_c + dacs_r.T + jnp.where(rowi == ln - 1, dlast, 0.0)
        row = lax.broadcasted_iota(jnp.int32, (ln, ln), 0)
        col = lax.broadcasted_iota(jnp.int32, (ln, ln), 1)
        dadt = _dot((col >= row).astype(F32), dacs, precision=HIGHEST)
        ddt_ref[...] = dadt * a + ddt_x
        da_ref[...] += jnp.sum(dadt * dt, axis=0, keepdims=True)
        dd_ref[...] += dd

        @pl.when(step == nc - 1)
        def _():
            da_ref[...] = da_ref[...] * a

    vec = pl.BlockSpec((1, LANES), lambda c: (0, 0))
    rev = lambda c: (nc - 1 - c, 0)
    return pl.pallas_call(
        body, name=name, grid=(nc,),
        in_specs=[pl.BlockSpec((ln, convd), rev), pl.BlockSpec((ln, LANES), rev), vec, vec,
                  pl.BlockSpec((1, n_heads, p, ns), lambda c: (nc - 1 - c, 0, 0, 0)), pl.BlockSpec((ln, di), rev)],
        out_specs=[pl.BlockSpec((ln, convd), rev), pl.BlockSpec((ln, LANES), rev), vec, vec],
        out_shape=[jax.ShapeDtypeStruct((s, convd), F32), jax.ShapeDtypeStruct((s, LANES), F32),
                   jax.ShapeDtypeStruct((1, LANES), F32), jax.ShapeDtypeStruct((1, LANES), F32)],
        scratch_shapes=[pltpu.VMEM((n_heads, p, ns), F32)],
        compiler_params=_params(("arbitrary",)))(xbc, dt, alog, dskip, prev_all, dy)


def _split(x, n):
    out = []
    for _ in range(n):
        piece = x.astype(BF16)
        out.append(piece)
        x = x - piece.astype(F32)
    return out


def _spread(x, onehot, n=2):
    return functools.reduce(jnp.add, [_dot(piece, onehot) for piece in _split(x, n)])


def _head_maps(di, p):
    e = (jnp.arange(di, dtype=jnp.int32)[None, :] // p == jnp.arange(LANES, dtype=jnp.int32)[:, None]).astype(BF16)
    return e, e.T


def _ssd_wide(dt, acs, acs_t, dskip, e_ref, et_ref):
    ln = dt.shape[0]
    last = acs[ln - 1:ln, :]
    stack = jnp.concatenate([dt, jnp.exp(acs), jnp.exp(last - acs), jnp.broadcast_to(dskip, (8, LANES))], axis=0)
    wide = _spread(stack, e_ref[...])
    tb = jnp.exp(jnp.broadcast_to(acs_t[:, ln - 1:ln], (LANES, LANES)))
    texp = functools.reduce(jnp.add, [_dot(et_ref[...], piece) for piece in _split(tb, 3)])
    return wide[:ln], wide[ln:2 * ln], wide[2 * ln:3 * ln], wide[3 * ln:3 * ln + 1], texp


def _ssd_fwd2(xbc, dt, alog, dskip, di, n_heads, n_groups, name):
    s, convd = xbc.shape
    ln, p, ns = SSM_CHUNK, SSM_HEAD_DIM, SSM_D_STATE
    nc, hg = s // ln, n_heads // n_groups
    gw = hg * p
    e64, e64t = _head_maps(di, p)

    def body(x_ref, dt_ref, alog_ref, d_ref, e_ref, et_ref, y_ref, prev_ref, st):
        @pl.when(pl.program_id(0) == 0)
        def _():
            st[...] = jnp.zeros(st.shape, F32)

        dt = dt_ref[...]
        _, acs, acs_t, tril = _ssd_common(dt, alog_ref[...], n_heads)
        dte, ee, dse, dske, texp = _ssd_wide(dt, acs, acs_t, d_ref[...], e_ref, et_ref)
        x = x_ref[:, :di]
        xdt = x * dte
        xdtb = xdt.astype(BF16)
        xdsb = (xdt * dse).astype(BF16)
        for g in range(n_groups):
            rows = slice(g * gw, (g + 1) * gw)
            bg = x_ref[:, di + g * ns:di + (g + 1) * ns].astype(BF16)
            cg = x_ref[:, di + (n_groups + g) * ns:di + (n_groups + g + 1) * ns].astype(BF16)
            gm = _dot(cg, bg, _NT)
            prev = st[rows, :]
            prev_ref[0, rows, :] = prev
            yo = _dot(cg, prev.astype(BF16), _NT)
            for j in range(hg):
                h = g * hg + j
                seg = acs[:, h:h + 1] - acs_t[h:h + 1, :]
                m = jnp.where(tril, gm * jnp.exp(jnp.where(tril, seg, 0.0)), 0.0)
                y_ref[:, h * p:(h + 1) * p] = _dot(m.astype(BF16), xdtb[:, h * p:(h + 1) * p])
            y_ref[:, rows] = y_ref[:, rows] + yo * ee[:, rows] + x[:, rows] * dske[:, rows]
            st[rows, :] = prev * texp[rows, :] + _dot(xdsb[:, rows], bg, _TN)

    vec = pl.BlockSpec((1, LANES), lambda c: (0, 0))
    return pl.pallas_call(
        body, name=name, grid=(nc,),
        in_specs=[pl.BlockSpec((ln, convd), lambda c: (c, 0)), pl.BlockSpec((ln, LANES), lambda c: (c, 0)), vec, vec,
                  pl.BlockSpec(e64.shape, lambda c: (0, 0)), pl.BlockSpec(e64t.shape, lambda c: (0, 0))],
        out_specs=[pl.BlockSpec((ln, di), lambda c: (c, 0)), pl.BlockSpec((1, di, ns), lambda c: (c, 0, 0))],
        out_shape=[jax.ShapeDtypeStruct((s, di), F32), jax.ShapeDtypeStruct((nc, di, ns), F32)],
        scratch_shapes=[pltpu.VMEM((di, ns), F32)],
        compiler_params=_params(("arbitrary",)))(xbc, dt, alog, dskip, e64, e64t)


def _ssd_bwd2(xbc, dt, alog, dskip, prev_all, dy, di, n_heads, n_groups, name):
    s, convd = xbc.shape
    ln, p, ns = SSM_CHUNK, SSM_HEAD_DIM, SSM_D_STATE
    nc, hg = s // ln, n_heads // n_groups
    gw = hg * p
    e64, e64t = _head_maps(di, p)

    def body(x_ref, dt_ref, alog_ref, d_ref, e_ref, et_ref, prev_ref, dy_ref,
             dx_ref, ddt_ref, da_ref, dd_ref, dh, yo_ref, w_ref):
        step = pl.program_id(0)

        @pl.when(step == 0)
        def _():
            dh[...] = jnp.zeros(dh.shape, F32)
            da_ref[...] = jnp.zeros(da_ref.shape, F32)
            dd_ref[...] = jnp.zeros(dd_ref.shape, F32)

        dt = dt_ref[...]
        a, acs, acs_t, tril = _ssd_common(dt, alog_ref[...], n_heads)
        dte, ee, dse, dske, texp = _ssd_wide(dt, acs, acs_t, d_ref[...], e_ref, et_ref)
        row = lax.broadcasted_iota(jnp.int32, (ln, ln), 0)
        col = lax.broadcasted_iota(jnp.int32, (ln, ln), 1)
        triu = col >= row
        x = x_ref[:, :di]
        dy = dy_ref[...]
        xdt = x * dte
        xdtb = xdt.astype(BF16)
        xdsb = (xdt * dse).astype(BF16)
        dyb = dy.astype(BF16)
        dyob = (dy * ee).astype(BF16)
        dhn = dh[...]
        dhb = dhn.astype(BF16)
        per_head = functools.reduce(jnp.add, [_dot(e_ref[...], piece) for piece in _split(dhn * prev_ref[0], 2)])
        ones8 = jnp.ones((8, LANES), BF16)
        dtt = functools.reduce(jnp.add, [_dot(ones8, piece, _NT) for piece in _split(per_head, 2)])[0:1]
        dacs_c = jnp.zeros((ln, LANES), F32)
        dacs_r = jnp.zeros((LANES, ln), F32)
        for g in range(n_groups):
            rows = slice(g * gw, (g + 1) * gw)
            bg = x_ref[:, di + g * ns:di + (g + 1) * ns].astype(BF16)
            cg = x_ref[:, di + (n_groups + g) * ns:di + (n_groups + g + 1) * ns].astype(BF16)
            gmt = _dot(bg, cg, _NT)
            prevb = prev_ref[0, rows, :].astype(BF16)
            dcg = _dot(dyob[:, rows], prevb)
            dh[rows, :] = texp[rows, :] * dhn[rows, :] + _dot(dyob[:, rows], cg, _TN)
            w = _dot(bg, dhb[rows, :], _NT)
            dbg = _dot(xdsb[:, rows], dhb[rows, :])
            yo_ref[:, rows] = _dot(cg, prevb, _NT)
            w_ref[:, rows] = w
            dgmt = jnp.zeros((ln, ln), F32)
            q_hi, q_lo = [], []
            for j in range(hg):
                h = g * hg + j
                segt = acs_t[h:h + 1, :] - acs[:, h:h + 1]
                dect = jnp.where(triu, jnp.exp(jnp.where(triu, segt, 0.0)), 0.0)
                dyh, xh = dyb[:, h * p:(h + 1) * p], xdtb[:, h * p:(h + 1) * p]
                mt = gmt * dect
                dmt = _dot(xh, dyh, _NT)
                dx_ref[:, h * p:(h + 1) * p] = _dot(mt.astype(BF16), dyh)
                dgmt = dgmt + dmt * dect
                hi, lo = _split(dmt * mt, 2)
                q_hi.append(hi)
                q_lo.append(lo)
            sel_c = (lax.broadcasted_iota(jnp.int32, (hg * ln, LANES), 1)
                     == g * hg + lax.broadcasted_iota(jnp.int32, (hg * ln, LANES), 0) // ln).astype(BF16)
            sel_r = (lax.broadcasted_iota(jnp.int32, (LANES, hg * ln), 0)
                     == g * hg + lax.broadcasted_iota(jnp.int32, (LANES, hg * ln), 1) // ln).astype(BF16)
            for pieces in (q_hi, q_lo):
                dacs_c = dacs_c - _dot(jnp.concatenate(pieces, axis=1), sel_c)
                dacs_r = dacs_r + _dot(sel_r, jnp.concatenate(pieces, axis=0))
            dgb = dgmt.astype(BF16)
            dx_ref[:, di + g * ns:di + (g + 1) * ns] = dbg + _dot(dgb, cg)
            dx_ref[:, di + (n_groups + g) * ns:di + (n_groups + g + 1) * ns] = dcg + _dot(dgb, bg, _TN)

        wds = w_ref[...] * dse
        dxdt = dx_ref[:, :di] + wds
        red = _spread(jnp.concatenate([dxdt * x, dy * yo_ref[...] * ee, xdt * wds, dy * x], axis=0), et_ref[...])
        ddt_x, r_off, r_state, ddr = red[:ln], red[ln:2 * ln], red[2 * ln:3 * ln], red[3 * ln:]
        dx_ref[:, :di] = dxdt * dte + dy * dske
        rowi = lax.broadcasted_iota(jnp.int32, (ln, LANES), 0)
        dlast = jnp.sum(r_state, axis=0, keepdims=True) + dtt * jnp.exp(acs[ln - 1:ln, :])
        dacs = r_off - r_state + dacs_c + dacs_r.T + jnp.where(rowi == ln - 1, dlast, 0.0)
        dadt = _dot(triu.astype(F32), dacs, precision=HIGHEST)
        ddt_ref[...] = dadt * a + ddt_x
        da_ref[...] += jnp.sum(dadt * dt, axis=0, keepdims=True)
        dd_ref[...] += jnp.sum(ddr, axis=0, keepdims=True)

        @pl.when(step == nc - 1)
        def _():
            da_ref[...] = da_ref[...] * a

    vec = pl.BlockSpec((1, LANES), lambda c: (0, 0))
    rev = lambda c: (nc - 1 - c, 0)
    return pl.pallas_call(
        body, name=name, grid=(nc,),
        in_specs=[pl.BlockSpec((ln, convd), rev), pl.BlockSpec((ln, LANES), rev), vec, vec,
                  pl.BlockSpec(e64.shape, lambda c: (0, 0)), pl.BlockSpec(e64t.shape, lambda c: (0, 0)),
                  pl.BlockSpec((1, di, ns), lambda c: (nc - 1 - c, 0, 0)), pl.BlockSpec((ln, di), rev)],
        out_specs=[pl.BlockSpec((ln, convd), rev), pl.BlockSpec((ln, LANES), rev), vec, vec],
        out_shape=[jax.ShapeDtypeStruct((s, convd), F32), jax.ShapeDtypeStruct((s, LANES), F32),
                   jax.ShapeDtypeStruct((1, LANES), F32), jax.ShapeDtypeStruct((1, LANES), F32)],
        scratch_shapes=[pltpu.VMEM((di, ns), F32), pltpu.VMEM((ln, di), F32), pltpu.VMEM((ln, di), F32)],
        compiler_params=_params(("arbitrary",)))(xbc, dt, alog, dskip, e64, e64t, prev_all, dy)


def _perm(a, d):
    if d == 1:
        return a
    s = a.shape[0]
    return a.reshape(s // d, d, -1).transpose(1, 0, 2).reshape(s, -1)


def _unperm(a, d):
    if d == 1:
        return a
    s = a.shape[0]
    return a.reshape(d, s // d, -1).transpose(1, 0, 2).reshape(s, -1)


def _rot_tables(s, d):
    half = ROPE_DIM // 2
    inv_freq = jnp.power(jnp.float32(ROPE_THETA), -jnp.arange(0, ROPE_DIM, 2, dtype=F32) / ROPE_DIM)
    v = jnp.arange(s, dtype=jnp.int32)
    pos = (v % (s // d)) * d + v // (s // d)
    ang = pos.astype(F32)[:, None] * inv_freq[None, :]
    cos, sin = jnp.cos(ang), jnp.sin(ang)
    zero = jnp.zeros((s, ATT_HEAD_DIM - ROPE_DIM), F32)
    cf = jnp.concatenate([cos, cos, jnp.ones_like(zero)], axis=1)
    s1 = jnp.concatenate([-sin, jnp.zeros_like(sin), zero], axis=1)
    s2 = jnp.concatenate([jnp.zeros_like(sin), sin, zero], axis=1)
    assert half * 2 == ROPE_DIM
    return cf, s1, s2


def _rot(x, tabs, sign):
    cf, s1, s2 = tabs
    half = ROPE_DIM // 2
    left = pltpu.roll(x, ATT_HEAD_DIM - half, 1)
    right = pltpu.roll(x, half, 1)
    return x * cf + sign * (left * s1 + right * s2)


def _att_masks(n, n_blk, rep):
    b = ATT_BLOCK
    row = lax.broadcasted_iota(jnp.int32, (rep * b, b), 0) & (b - 1)
    col = lax.broadcasted_iota(jnp.int32, (rep * b, b), 1)
    off = jnp.where(n % n_blk != 0, 0, 2 * b)
    return col <= row, col >= row + off


def _stack(x, rep):
    return jnp.concatenate([x[:, j * ATT_HEAD_DIM:(j + 1) * ATT_HEAD_DIM] for j in range(rep)], axis=0)


def _att_specs(nb, rep, cur, prv):
    b, hd = ATT_BLOCK, ATT_HEAD_DIM
    q_spec = pl.BlockSpec((b, rep * hd), lambda h, n: (cur(n), h))
    kc_spec = pl.BlockSpec((b, hd), lambda h, n: (cur(n), h))
    kp_spec = pl.BlockSpec((b, hd), lambda h, n: (prv(n), h))
    tc_spec = pl.BlockSpec((b, hd), lambda h, n: (cur(n), 0))
    tp_spec = pl.BlockSpec((b, hd), lambda h, n: (prv(n), 0))
    return q_spec, kc_spec, kp_spec, tc_spec, tp_spec


def _attn_fwd(q, k, v, tabs, n_blk, name):
    s = q.shape[0]
    b, hd = ATT_BLOCK, ATT_HEAD_DIM
    nb = s // b
    n_kv = ATT_KV_HEADS_PER_GROUP
    rep = ATT_HEADS_PER_GROUP // n_kv
    scale = hd ** -0.5

    def body(q_ref, kc_ref, kp_ref, vc_ref, vp_ref, cfc, s1c, s2c, cfp, s1p, s2p, o_ref, lse_ref):
        n = pl.program_id(1)
        tc = (cfc[...], s1c[...], s2c[...])
        tp = (cfp[...], s1p[...], s2p[...])
        qv = q_ref[...]
        q4 = jnp.concatenate([_rot(qv[:, j * hd:(j + 1) * hd], tc, 1.0) for j in range(rep)], axis=0).astype(BF16)
        kc = _rot(kc_ref[...], tc, 1.0).astype(BF16)
        kp = _rot(kp_ref[...], tp, 1.0).astype(BF16)
        mc, mp = _att_masks(n, n_blk, rep)
        sc = jnp.where(mc, _dot(q4, kc, _NT) * scale, NEG)
        sp = jnp.where(mp, _dot(q4, kp, _NT) * scale, NEG)
        m = jnp.maximum(jnp.max(sc, axis=1, keepdims=True), jnp.max(sp, axis=1, keepdims=True))
        pc, pp = jnp.exp(sc - m), jnp.exp(sp - m)
        l = jnp.sum(pc, axis=1, keepdims=True) + jnp.sum(pp, axis=1, keepdims=True)
        o = (_dot(pc.astype(BF16), vc_ref[...].astype(BF16)) + _dot(pp.astype(BF16), vp_ref[...].astype(BF16))) / l
        lse = jnp.broadcast_to(m + jnp.log(l), (rep * b, hd))
        for j in range(rep):
            o_ref[:, j * hd:(j + 1) * hd] = o[j * b:(j + 1) * b]
            lse_ref[:, j * hd:(j + 1) * hd] = lse[j * b:(j + 1) * b]

    cur = lambda n: n
    prv = lambda n: jnp.maximum(n - 1, 0)
    q_spec, kc_spec, kp_spec, tc_spec, tp_spec = _att_specs(nb, rep, cur, prv)
    return pl.pallas_call(
        body, name=name, grid=(n_kv, nb),
        in_specs=[q_spec, kc_spec, kp_spec, kc_spec, kp_spec, tc_spec, tc_spec, tc_spec, tp_spec, tp_spec, tp_spec],
        out_specs=[q_spec, q_spec],
        out_shape=[jax.ShapeDtypeStruct(q.shape, F32), jax.ShapeDtypeStruct(q.shape, F32)],
        compiler_params=_params(("parallel", "arbitrary")))(q, k, k, v, v, *tabs, *tabs)


def _attn_bwd(q, k, v, do, lse, delta, tabs, n_blk, name):
    s = q.shape[0]
    b, hd = ATT_BLOCK, ATT_HEAD_DIM
    nb = s // b
    n_kv = ATT_KV_HEADS_PER_GROUP
    rep = ATT_HEADS_PER_GROUP // n_kv
    scale = hd ** -0.5

    def body(q_ref, do_ref, lse_ref, dl_ref, kc_ref, kp_ref, vc_ref, vp_ref, cfc, s1c, s2c, cfp, s1p, s2p,
             dq_ref, dk_ref, dv_ref, ck, cv):
        n = pl.program_id(1)
        tp = (cfp[...], s1p[...], s2p[...])

        @pl.when(n == 0)
        def _():
            ck[...] = jnp.zeros(ck.shape, F32)
            cv[...] = jnp.zeros(cv.shape, F32)

        @pl.when(n < nb)
        def _():
            tc = (cfc[...], s1c[...], s2c[...])
            qv = q_ref[...]
            q4 = jnp.concatenate([_rot(qv[:, j * hd:(j + 1) * hd], tc, 1.0) for j in range(rep)],
                                 axis=0).astype(BF16)
            do4 = _stack(do_ref[...], rep).astype(BF16)
            lse4 = _stack(lse_ref[...], rep)
            dl4 = _stack(dl_ref[...], rep)
            kc = _rot(kc_ref[...], tc, 1.0).astype(BF16)
            kp = _rot(kp_ref[...], tp, 1.0).astype(BF16)
            vc, vp = vc_ref[...].astype(BF16), vp_ref[...].astype(BF16)
            mc, mp = _att_masks(n, n_blk, rep)
            pc = jnp.where(mc, jnp.exp(jnp.where(mc, _dot(q4, kc, _NT) * scale - lse4, 0.0)), 0.0)
            pp = jnp.where(mp, jnp.exp(jnp.where(mp, _dot(q4, kp, _NT) * scale - lse4, 0.0)), 0.0)
            dsc = (pc * (_dot(do4, vc, _NT) - dl4)).astype(BF16)
            dsp = (pp * (_dot(do4, vp, _NT) - dl4)).astype(BF16)
            dq4 = (_dot(dsc, kc) + _dot(dsp, kp)) * scale
            for j in range(rep):
                dq_ref[:, j * hd:(j + 1) * hd] = _rot(dq4[j * b:(j + 1) * b], tc, -1.0).astype(dq_ref.dtype)
            dk_prev = ck[...] + _dot(dsp, q4, _TN) * scale
            dv_prev = cv[...] + _dot(pp.astype(BF16), do4, _TN)
            dk_ref[...] = _rot(dk_prev, tp, -1.0).astype(dk_ref.dtype)
            dv_ref[...] = dv_prev.astype(dv_ref.dtype)
            ck[...] = _dot(dsc, q4, _TN) * scale
            cv[...] = _dot(pc.astype(BF16), do4, _TN)

        @pl.when(n == nb)
        def _():
            dk_ref[...] = _rot(ck[...], tp, -1.0).astype(dk_ref.dtype)
            dv_ref[...] = cv[...].astype(dv_ref.dtype)

    cur = lambda n: jnp.minimum(n, nb - 1)
    prv = lambda n: jnp.maximum(n - 1, 0)
    q_spec, kc_spec, kp_spec, tc_spec, tp_spec = _att_specs(nb, rep, cur, prv)
    return pl.pallas_call(
        body, name=name, grid=(n_kv, nb + 1),
        in_specs=[q_spec, q_spec, q_spec, q_spec, kc_spec, kp_spec, kc_spec, kp_spec,
                  tc_spec, tc_spec, tc_spec, tp_spec, tp_spec, tp_spec],
        out_specs=[q_spec, kp_spec, kp_spec],
        out_shape=[jax.ShapeDtypeStruct(q.shape, BF16), jax.ShapeDtypeStruct(k.shape, BF16),
                   jax.ShapeDtypeStruct(k.shape, BF16)],
        scratch_shapes=[pltpu.VMEM((b, hd), F32), pltpu.VMEM((b, hd), F32)],
        compiler_params=_params(("parallel", "arbitrary")))(q, do, lse, delta, k, k, v, v, *tabs, *tabs)


def _rot_heads(x, tabs, width, sign, out_dtype, name):
    s = x.shape[0]
    hd = ATT_HEAD_DIM
    tile = min(512, s)

    def body(x_ref, cf, s1, s2, o_ref):
        t = (cf[...], s1[...], s2[...])
        for j in range(width // hd):
            o_ref[:, j * hd:(j + 1) * hd] = _rot(x_ref[:, j * hd:(j + 1) * hd], t, sign).astype(o_ref.dtype)

    tab = pl.BlockSpec((tile, hd), lambda i: (i, 0))
    return pl.pallas_call(
        body, name=name, grid=(s // tile,), in_specs=[pl.BlockSpec((tile, width), lambda i: (i, 0)), tab, tab, tab],
        out_specs=pl.BlockSpec((tile, width), lambda i: (i, 0)), out_shape=jax.ShapeDtypeStruct((s, width), out_dtype),
        compiler_params=_params(("parallel",)))(x, *tabs)


def _rows_of(r, dil):
    return pl.ds(r, ATT_BLOCK, stride=dil) if dil > 1 else slice(None)


def _nat_specs(g, dil, n_kv_all, cur, prv):
    b, hd = ATT_BLOCK * dil, ATT_HEAD_DIM
    n_kv = ATT_KV_HEADS_PER_GROUP
    rep = ATT_HEADS_PER_GROUP // n_kv
    q_all = [pl.BlockSpec((b, hd), lambda h, n, j=j: (cur(n), (g * n_kv + h) * rep + j)) for j in range(rep)]
    q_own = [pl.BlockSpec((b, hd), lambda h, n, j=j: (cur(n), h * rep + j)) for j in range(rep)]
    hm_all = pl.BlockSpec((rep, b, hd), lambda h, n: (g * n_kv + h, cur(n), 0))
    hm_own = pl.BlockSpec((rep, b, hd), lambda h, n: (h, cur(n), 0))
    kc =pl.BlockSpec((b, hd), lambda h, n: (cur(n), g * n_kv + h))
    kp = pl.BlockSpec((b, hd), lambda h, n: (prv(n), g * n_kv + h))
    vc = pl.BlockSpec((b, hd), lambda h, n: (cur(n), n_kv_all + g * n_kv + h))
    vp = pl.BlockSpec((b, hd), lambda h, n: (prv(n), n_kv_all + g * n_kv + h))
    tab = pl.BlockSpec((b, hd), lambda h, n: (cur(n), 0))
    stat = pl.BlockSpec((None, b, LANES), lambda h, n: (h, cur(n), 0))
    return q_all, q_own, hm_all, hm_own, kc, kp, vc, vp, tab, stat


def _head_cols(stat, rep):
    return jnp.concatenate([jnp.broadcast_to(stat[:, j:j + 1], stat.shape) for j in range(rep)], axis=0)


def _attn_fwd_nat(q_all, k_rot, kv, tabs, g, dil, name):
    s = q_all.shape[0]
    b, hd = ATT_BLOCK, ATT_HEAD_DIM
    nbn = s // (b * dil)
    n_kv = ATT_KV_HEADS_PER_GROUP
    rep = ATT_HEADS_PER_GROUP // n_kv
    n_kv_all = k_rot.shape[1] // hd
    scale = hd ** -0.5

    def body(*refs):
        q_refs = refs[:rep]
        kc_ref, kp_ref, vc_ref, vp_ref, cf, s1, s2, o_ref, lse_ref = refs[rep:]
        mc, mp = _att_masks(jnp.where(pl.program_id(1) > 0, 1, 0), 2, rep)
        for r in range(dil):
            sl = _rows_of(r, dil)
            tc = (cf[sl, :], s1[sl, :], s2[sl, :])
            q4 = jnp.concatenate([_rot(q_ref[sl, :], tc, 1.0) for q_ref in q_refs], axis=0).astype(BF16)
            kc, kp = kc_ref[sl, :].astype(BF16), kp_ref[sl, :].astype(BF16)
            sc = jnp.where(mc, _dot(q4, kc, _NT) * scale, NEG)
            sp = jnp.where(mp, _dot(q4, kp, _NT) * scale, NEG)
            m = jnp.maximum(jnp.max(sc, axis=1, keepdims=True), jnp.max(sp, axis=1, keepdims=True))
            pc, pp = jnp.exp(sc - m), jnp.exp(sp - m)
            l = jnp.sum(pc, axis=1, keepdims=True) + jnp.sum(pp, axis=1, keepdims=True)
            o = (_dot(pc.astype(BF16), vc_ref[sl, :].astype(BF16))
                 + _dot(pp.astype(BF16), vp_ref[sl, :].astype(BF16))) / l
            lse = m + jnp.log(l)
            for j in range(rep):
                o_ref[j, sl, :] = o[j * b:(j + 1) * b]
            lse_ref[sl, :] = _lane_place([lse[j * b:(j + 1) * b] for j in range(rep)])

    cur = lambda n: n
    prv = lambda n: jnp.maximum(n - 1, 0)
    q_specs, _, _, hm_own, kc, kp, vc, vp, tab, stat = _nat_specs(g, dil, n_kv_all, cur, prv)
    return pl.pallas_call(
        body, name=name, grid=(n_kv, nbn), in_specs=[*q_specs, kc, kp, vc, vp, tab, tab, tab], out_specs=[hm_own, stat],
        out_shape=[jax.ShapeDtypeStruct((ATT_HEADS_PER_GROUP, s, hd), F32), jax.ShapeDtypeStruct((n_kv, s, LANES), F32)],
        compiler_params=_params(("parallel", "arbitrary")))(*([q_all] * rep), k_rot, k_rot, kv, kv, *tabs)


def _attn_bwd_nat(q_all, k_rot, kv, do, lse, delta, tabs, grads, g, dil, name):
    s = q_all.shape[0]
    b, hd = ATT_BLOCK, ATT_HEAD_DIM
    nbn = s // (b * dil)
    n_kv = ATT_KV_HEADS_PER_GROUP
    rep = ATT_HEADS_PER_GROUP // n_kv
    n_kv_all = k_rot.shape[1] // hd
    scale = hd ** -0.5

    def body(*refs):
        q_refs, do_refs = refs[:rep], refs[rep:2 * rep]
        (lse_ref, dl_ref, kc_ref, kp_ref, vc_ref, vp_ref, cf, s1, s2, _, _, _,
         dq_ref, dk_ref, dv_ref, ck, cv) = refs[2 * rep:]
        n = pl.program_id(1)

        @pl.when(n == 0)
        def _():
            ck[...] = jnp.zeros(ck.shape, F32)
            cv[...] = jnp.zeros(cv.shape, F32)

        @pl.when(n < nbn)
        def _():
            mc, mp = _att_masks(jnp.where(n > 0, 1, 0), 2, rep)
            for r in range(dil):
                sl = _rows_of(r, dil)
                own = slice(r * b, (r + 1) * b)
                tc = (cf[sl, :], s1[sl, :], s2[sl, :])
                q4 = jnp.concatenate([_rot(q_ref[sl, :], tc, 1.0) for q_ref in q_refs], axis=0).astype(BF16)
                do4 = jnp.concatenate([do_ref[sl, :] for do_ref in do_refs], axis=0).astype(BF16)
                lse4 = _head_cols(lse_ref[sl, :], rep)
                dl4 = _head_cols(dl_ref[sl, :], rep)
                kc, kp = kc_ref[sl, :].astype(BF16), kp_ref[sl, :].astype(BF16)
                vc, vp = vc_ref[sl, :].astype(BF16), vp_ref[sl, :].astype(BF16)
                pc = jnp.where(mc, jnp.exp(_dot(q4, kc, _NT) * scale - lse4), 0.0)
                pp = jnp.where(mp, jnp.exp(_dot(q4, kp, _NT) * scale - lse4), 0.0)
                dsc = (pc * (_dot(do4, vc, _NT) - dl4)).astype(BF16)
                dsp = (pp * (_dot(do4, vp, _NT) - dl4)).astype(BF16)
                dq4 = (_dot(dsc, kc) + _dot(dsp, kp)) * scale
                for j in range(rep):
                    dq_ref[j, sl, :] = _rot(dq4[j * b:(j + 1) * b], tc, -1.0)
                dk_ref[sl, :] = ck[own, :] + _dot(dsp, q4, _TN) * scale
                dv_ref[sl, :] = cv[own, :] + _dot(pp.astype(BF16), do4, _TN)
                ck[own, :] = _dot(dsc, q4, _TN) * scale
                cv[own, :] = _dot(pc.astype(BF16), do4, _TN)

        @pl.when(n == nbn)
        def _():
            for r in range(dil):
                sl = _rows_of(r, dil)
                dk_ref[sl, :] = ck[r * b:(r + 1) * b, :]
                dv_ref[sl, :] = cv[r * b:(r + 1) * b, :]

    cur = lambda n: jnp.minimum(n, nbn - 1)
    prv = lambda n: jnp.maximum(n - 1, 0)
    q_specs, do_specs, hm_all, _, kc, kp, vc, vp, tab, stat = _nat_specs(g, dil, n_kv_all, cur, prv)
    anyspace = pl.BlockSpec(memory_space=pl.ANY)
    n_in = 2 * rep + 9
    return pl.pallas_call(
        body, name=name, grid=(n_kv, nbn + 1),
        in_specs=[*q_specs, *do_specs, stat, stat, kc, kp, vc, vp, tab, tab, tab, anyspace, anyspace, anyspace],
        out_specs=[hm_all, kp, kp], out_shape=[jax.ShapeDtypeStruct(a.shape, a.dtype) for a in grads],
        input_output_aliases={n_in: 0, n_in + 1: 1, n_in + 2: 2},
        scratch_shapes=[pltpu.VMEM((dil * b, hd), F32), pltpu.VMEM((dil * b, hd), F32)],
        compiler_params=_params(("parallel", "arbitrary"), VMEM_LIMIT_ATTN_BWD_BYTES))(
            *([q_all] * rep), *([do] * rep), lse, delta, k_rot, k_rot, kv, kv, *tabs, *grads)


def _adamw(g_slabs, w, m, v, name):
    kk, r, c = g_slabs.shape
    tile = r if r <= 256 else _pick_rows(r, 256)

    def body(g_ref, w_ref, m_ref, v_ref, go_ref, d_ref, mo_ref, vo_ref):
        g = g_ref[0].astype(F32)
        for k in range(1, kk):
            g = g + g_ref[k].astype(F32)
        m2 = ADAM_B1 * m_ref[...] + (1.0 - ADAM_B1) * g
        v2 = ADAM_B2 * v_ref[...] + (1.0 - ADAM_B2) * jnp.square(g)
        m_hat = m2 / (1.0 - ADAM_B1 ** ADAM_STEP)
        v_hat = v2 / (1.0 - ADAM_B2 ** ADAM_STEP)
        go_ref[...] = g
        d_ref[...] = -ADAM_LR * (m_hat / (jnp.sqrt(v_hat) + ADAM_EPS) + ADAM_WD * w_ref[...])
        mo_ref[...] = m2
        vo_ref[...] = v2

    spec = pl.BlockSpec((tile, c), lambda i: (i, 0))
    return pl.pallas_call(
        body, name=name, grid=(r // tile,), in_specs=[pl.BlockSpec((kk, tile, c), lambda i: (0, i, 0)), spec, spec, spec],
        out_specs=[spec] * 4, out_shape=[jax.ShapeDtypeStruct((r, c), F32)] * 4,
        compiler_params=_params(("parallel",)))(g_slabs, w, m, v)


def _pick_rows(r, pref):
    t = (pref // 16) * 16
    while t >= 16:
        if r % t == 0:
            return t
        t -= 16
    return r


def _coords():
    return lax.axis_index("x"), lax.axis_index("y"), lax.axis_index("c")


def _dev_index(px, py, pc):
    return 4 * px + 2 * py + pc


def _all_gather(shards, name):
    na = len(shards)

    def body(*refs):
        ins, outs = refs[:na], refs[na:2 * na]
        send_sems, recv_sems, local_sems = refs[2 * na:]
        x, y, c = _coords()
        me, sibling = (x, y, c), (x, y, 1 - c)
        chips = [(1 - x, y), (x, 1 - y), (1 - x, 1 - y)]

        def copy(a, k, block, to, src=None):
            dst = outs[a].at[_dev_index(*block)]
            return pltpu.make_async_remote_copy(
                src_ref=dst if src is None else src, dst_ref=dst, send_sem=send_sems.at[a * 7 + k],
                recv_sem=recv_sems.at[a * 7 + k], device_id=to, device_id_type=MESH)

        mine = [pltpu.make_async_copy(ins[a], outs[a].at[_dev_index(*me)], local_sems.at[a]) for a in range(na)]
        for cp in mine:
            cp.start()
        first = []
        for a in range(na):
            first.append(copy(a, 0, me, sibling, src=ins[a]))
            first += [copy(a, 1 + j, me, (*chip, c), src=ins[a]) for j, chip in enumerate(chips)]
        for cp in first:
            cp.start()
        passed = []
        for j, chip in enumerate(chips):
            for a in range(na):
                copy(a, 1 + j, (*chip, c), me).wait_recv()
                cp = copy(a, 4 + j, (*chip, c), sibling)
                cp.start()
                passed.append(cp)
        for a in range(na):
            copy(a, 0, sibling, me).wait_recv()
            for j, chip in enumerate(chips):
                copy(a, 4 + j, (*chip, 1 - c), me).wait_recv()
        for cp in first + passed:
            cp.wait_send()
        for cp in mine:
            cp.wait()

    hbm = pl.BlockSpec(memory_space=pl.ANY)
    return pl.pallas_call(
        body, name=name, in_specs=[hbm] * na, out_specs=[hbm] * na,
        out_shape=[jax.ShapeDtypeStruct((NDEV,) + s.shape, s.dtype) for s in shards],
        scratch_shapes=[pltpu.SemaphoreType.DMA((7 * na,)), pltpu.SemaphoreType.DMA((7 * na,)),
                        pltpu.SemaphoreType.DMA((na,))])(*shards)


def _exchange(slabs, whole, name, after=()):
    ns, nw = len(slabs), len(whole)
    na = ns + nw
    nb = len(after)

    def body(*refs):
        ins, outs = refs[:na], refs[na + nb:2 * na + nb]
        send_sems, recv_sems, local_sems = refs[2 * na + nb:]
        x, y, c = _coords()
        me = _dev_index(x, y, c)

        def src_of(a, p):
            return ins[a].at[p] if a < ns else ins[a]

        def copy(a, k, peer):
            p = _dev_index(*peer)
            return pltpu.make_async_remote_copy(
                src_ref=src_of(a, p), dst_ref=outs[a].at[me], send_sem=send_sems.at[a * 7 + k - 1],
                recv_sem=recv_sems.at[a * 7 + k - 1], device_id=peer, device_id_type=MESH)

        def arrival(a, k, peer):
            p = _dev_index(*peer)
            return pltpu.make_async_remote_copy(
                src_ref=src_of(a, p), dst_ref=outs[a].at[p], send_sem=send_sems.at[a * 7 + k - 1],
                recv_sem=recv_sems.at[a * 7 + k - 1], device_id=peer, device_id_type=MESH)

        mine = [pltpu.make_async_copy(src_of(a, me), outs[a].at[me], local_sems.at[a]) for a in range(na)]
        for cp in mine:
            cp.start()
        peers = [(k, (x ^ (k >> 2), y ^ ((k >> 1) & 1), c ^ (k & 1))) for k in range(1, NDEV)]
        sent = [copy(a, k, peer) for k, peer in peers for a in range(na)]
        for cp in sent:
            cp.start()
        for k, peer in peers:
            for a in range(na):
                arrival(a, k, peer).wait_recv()
        for cp in sent:
            cp.wait_send()
        for cp in mine:
            cp.wait()

    hbm = pl.BlockSpec(memory_space=pl.ANY)
    out_shape = [jax.ShapeDtypeStruct(s.shape, s.dtype) for s in slabs]
    out_shape += [jax.ShapeDtypeStruct((NDEV,) + w.shape, w.dtype) for w in whole]
    return pl.pallas_call(
        body, name=name, in_specs=[hbm] * (na + nb), out_specs=[hbm] * na, out_shape=out_shape,
        scratch_shapes=[pltpu.SemaphoreType.DMA((7 * na,)), pltpu.SemaphoreType.DMA((7 * na,)),
                        pltpu.SemaphoreType.DMA((na,))])(*slabs, *whole, *after)


_HBM = pl.BlockSpec(memory_space=pltpu.HBM)
_SEM = pl.BlockSpec(memory_space=pltpu.SEMAPHORE)
_EFFECT = pltpu.SideEffectType.DATAFLOW_SIDE_EFFECTING


def _peers(x, y, c):
    return [(k, (x ^ (k >> 2), y ^ ((k >> 1) & 1), c ^ (k & 1))) for k in range(1, NDEV)]


def _peer_copy(src, land, send_sems, recv_sems, a, k, dst_block, peer):
    return pltpu.make_async_remote_copy(
        src_ref=src, dst_ref=land.at[dst_block], send_sem=send_sems.at[a * 7 + k - 1],
        recv_sem=recv_sems.at[a * 7 + k - 1], device_id=peer, device_id_type=MESH)


def _send_start(arrays, slabs, name):
    na = len(arrays)
    lands = [jax.ShapeDtypeStruct(a.shape if slabs else (NDEV,) + a.shape, a.dtype) for a in arrays]

    def body(*refs):
        ins, zones = refs[:na], refs[na:2 * na]
        send_sems, recv_sems = refs[2 * na], refs[2 * na + 1]
        token = refs[-1]
        x, y, c = _coords()
        me = _dev_index(x, y, c)
        for k, peer in _peers(x, y, c):
            for a in range(na):
                src = ins[a].at[_dev_index(*peer)] if slabs else ins[a]
                _peer_copy(src, zones[a], send_sems, recv_sems, a, k, me, peer).start()
        token[...] = jnp.zeros_like(token)

    outs = pl.pallas_call(
        body, name=name,
        out_shape=(pltpu.SemaphoreType.DMA((7 * na,)), pltpu.SemaphoreType.DMA((7 * na,)),
                   *[pltpu.HBM(a.shape, a.dtype) for a in arrays], *[pltpu.HBM(l.shape, l.dtype) for l in lands],
                   jax.ShapeDtypeStruct((8, LANES), F32)),
        in_specs=[_HBM] * (2 * na), out_specs=(_SEM, _SEM, *([_HBM] * (2 * na)), pl.BlockSpec(memory_space=pltpu.VMEM)),
        input_output_aliases={i: 2 + i for i in range(2 * na)},
        compiler_params=pltpu.CompilerParams(has_side_effects=_EFFECT),
    )(*[pltpu.with_memory_space_constraint(a, pltpu.HBM) for a in arrays],
      *[pltpu.with_memory_space_constraint(lax.empty(l.shape, l.dtype), pltpu.HBM) for l in lands])
    return outs[0], outs[1], list(outs[2:2 + na]), list(outs[2 + na:2 + 2 * na]), outs[-1]


def _send_wait(started, after, slabs, name):
    send_sems, recv_sems, thru, zones, _ = started
    na = len(thru)

    def body(*refs):
        ins, lands = refs[:na], refs[na:2 * na]
        s_sems, r_sems = refs[2 * na], refs[2 * na + 1]
        x, y, c = _coords()
        for k, peer in _peers(x, y, c):
            p = _dev_index(*peer)
            for a in range(na):
                src = ins[a].at[p] if slabs else ins[a]
                cp = _peer_copy(src, lands[a], s_sems, r_sems, a, k, p, peer)
                cp.wait_send()
                cp.wait_recv()

    outs = pl.pallas_call(
        body, name=name, out_shape=tuple(pltpu.HBM(v.shape, v.dtype) for v in thru + zones),
        in_specs=[_HBM] * (2 * na) + [_SEM, _SEM, pl.BlockSpec(memory_space=pl.ANY)], out_specs=tuple([_HBM] * (2 * na)),
        input_output_aliases={i: i for i in range(2 * na)},
        compiler_params=pltpu.CompilerParams(has_side_effects=_EFFECT),
    )(*thru, *zones, send_sems, recv_sems, after)
    me = _dev_index(*_coords())
    filled = []
    for a in range(na):
        own = lax.dynamic_index_in_dim(outs[a], me, 0, keepdims=False) if slabs else outs[a]
        filled.append(lax.dynamic_update_index_in_dim(outs[na + a], own, me, 0))
    return filled


def _pack(vecs):
    parts, spans, off = [], [], 0
    for v in vecs:
        n = v.size
        pad = (-n) % LANES
        parts.append(jnp.pad(v.reshape(-1).astype(F32), (0, pad)))
        spans.append((off, n))
        off += n + pad
    return jnp.concatenate(parts).reshape(-1, LANES), spans


def _pad_lanes(v):
    v = v.reshape(1, -1)
    return jnp.pad(v, ((0, 0), (0, LANES - v.shape[1])))


def _cols_to_slabs(g):
    sh = g.shape
    g = g.reshape(sh[:-1] + (NDEV, sh[-1] // NDEV))
    return jnp.moveaxis(g, -2, 0)


def _rows_to_slabs(g):
    sh = g.shape
    g = g.reshape(sh[:-2] + (NDEV, sh[-2] // NDEV, sh[-1]))
    return jnp.moveaxis(g, -3, 0)


def _slabs_to_cols(a):
    a = jnp.moveaxis(a, 0, -2)
    return a.reshape(a.shape[:-2] + (a.shape[-2] * a.shape[-1],))


def _slabs_to_rows(a):
    a = jnp.moveaxis(a, 0, -3)
    return a.reshape(a.shape[:-3] + (a.shape[-3] * a.shape[-2], a.shape[-1]))


def _ffn_forward2(x, norm_w, w_up, fcw, wdown, tag):
    h = _rms_fwd(x, norm_w, f"{tag}_norm")
    u = _mm(h, w_up, name=f"{tag}_up")
    f = _ffn_gate_fwd2(u, fcw, f"{tag}_gate")
    return _mm(f, wdown, res=x, name=f"{tag}_down"), (h, u, f)


def _ffn_backward2(x, saved, dout, dout_b, norm_w, w_up, fcw, wdown, tag):
    h, u, f = saved
    dwdown = _mm(f, dout_b, ta=True, name=f"{tag}_dwdown")
    df = _mm(dout_b, wdown, tb=True, name=f"{tag}_df")
    du, dfc = _ffn_gate_bwd2(u, fcw, df, f"{tag}_gate_bwd")
    dwup = _mm(h, du, ta=True, name=f"{tag}_dwup")
    dh = _mm(du, w_up, tb=True, name=f"{tag}_dh")
    dx, dxb, dnorm = _rms_bwd(x, norm_w, dh, dout, f"{tag}_norm_bwd")
    return dx, dxb, (dwup, jnp.concatenate([dfc[0], dfc[1]], axis=1), dwdown, dnorm)


def _ffn_forward(x, norm_w, wup_g, wup_v, cw_g, cw_v, wdown, tag):
    h = _rms_fwd(x, norm_w, f"{tag}_norm")
    ug = _mm(h, wup_g, name=f"{tag}_up_gate")
    uv = _mm(h, wup_v, name=f"{tag}_up_val")
    f = _ffn_gate_fwd(ug, uv, cw_g, cw_v, f"{tag}_gate")
    return _mm(f, wdown, res=x, name=f"{tag}_down"), (h, ug, uv, f)


def _ffn_backward(x, saved, dout, dout_b, norm_w, wup_g, wup_v, cw_g, cw_v, wdown, tag):
    h, ug, uv, f = saved
    dwdown = _mm(f, dout_b, ta=True, name=f"{tag}_dwdown")
    df = _mm(dout_b, wdown, tb=True, name=f"{tag}_df")
    dug, duv, dcg, dcv = _ffn_gate_bwd(ug, uv, cw_g, cw_v, df, f"{tag}_gate_bwd")
    dwg = _mm(h, dug, ta=True, name=f"{tag}_dwup_gate")
    dwv = _mm(h, duv, ta=True, name=f"{tag}_dwup_val")
    dh = _mm(dug, wup_g, tb=True, name=f"{tag}_dh_gate")
    dh = _mm(duv, wup_v, tb=True, res=dh, name=f"{tag}_dh_val")
    dx, dxb, dnorm = _rms_bwd(x, norm_w, dh, dout, f"{tag}_norm_bwd")
    return dx, dxb, (jnp.concatenate([dwg, dwv], axis=1), jnp.concatenate([dcg, dcv], axis=1), dwdown, dnorm)


def kernel(x, a_norm, ssm_w_in, ssm_conv_w, ssm_conv_b, ssm_dt_bias, ssm_a_log, ssm_d, ssm_norm, ssm_w_out, kv_norm, w_kv, b_norm, att_w_q, att_w_o, ffn_norm, ffn_w_up, ffn_conv_w, ffn_w_down, final_norm, loss_target, m_a_norm, m_ssm_w_in, m_ssm_conv_w, m_ssm_conv_b, m_ssm_dt_bias, m_ssm_a_log, m_ssm_d, m_ssm_norm, m_ssm_w_out, m_kv_norm, m_w_kv, m_b_norm, m_att_w_q, m_att_w_o, m_ffn_norm, m_ffn_w_up, m_ffn_conv_w, m_ffn_w_down, m_final_norm, v_a_norm, v_ssm_w_in, v_ssm_conv_w, v_ssm_conv_b, v_ssm_dt_bias, v_ssm_a_log, v_ssm_d, v_ssm_norm, v_ssm_w_out, v_kv_norm, v_w_kv, v_b_norm, v_att_w_q, v_att_w_o, v_ffn_norm, v_ffn_w_up, v_ffn_conv_w, v_ffn_w_down, v_final_norm):
    given = dict(locals())
    xs, tgt = x[0], loss_target[0]
    s, d = xs.shape
    di = ssm_w_out.shape[1] * NDEV
    nh = ssm_dt_bias.shape[1]
    ng = SSM_N_GROUPS
    convd = di + 2 * ng * SSM_D_STATE
    f = ffn_w_down.shape[1] * NDEV
    n_att = len(ATT_PATTERNS)
    qg = ATT_HEADS_PER_GROUP * ATT_HEAD_DIM
    kg = ATT_KV_HEADS_PER_GROUP * ATT_HEAD_DIM
    kvd = n_att * kg
    assert all(w // dil == ATT_BLOCK for w, dil in ATT_PATTERNS)

    small, _ = _pack([a_norm, ssm_conv_w, ssm_conv_b, ssm_norm, ffn_conv_w])
    gat = _all_gather([ssm_w_in[0].astype(BF16), small], "gather_weights")
    first = _send_start([ssm_w_out[0].astype(BF16), ffn_w_up[0].astype(BF16), ffn_w_down[0].astype(BF16)], False,
                        "gather_ffn0_start")
    rest = _send_start([b.astype(BF16) for b in (w_kv, att_w_q[0], att_w_o[0], ffn_w_up[1], ffn_w_down[1])], False,
                       "gather_rest_start")
    w_in = _slabs_to_cols(gat[0])
    in_dim = di + convd + nh
    in_pad = di + convd + LANES
    w_in = jnp.pad(w_in, ((0, 0), (0, in_pad - in_dim)))
    sm = gat[1].reshape(NDEV, -1)
    o0 = 0

    def take(shape):
        nonlocal o0
        n = math.prod(shape)
        out = sm[:, o0:o0 + n].reshape((NDEV,) + shape)
        o0 += n + (-n) % LANES
        return out
    a_norm_f = _slabs_to_cols(take(a_norm.shape)) + (first[-1][0, 0] + rest[-1][0, 0])
    conv_w_f = _slabs_to_cols(take(ssm_conv_w.shape))[0]
    conv_b_f = _slabs_to_cols(take(ssm_conv_b.shape))
    ssm_norm_f = _slabs_to_cols(take(ssm_norm.shape))
    fcw = _slabs_to_cols(take(ffn_conv_w.shape))
    dtb, alog, dsk = _pad_lanes(ssm_dt_bias), _pad_lanes(ssm_a_log), _pad_lanes(ssm_d)
    kvn, fin = kv_norm.reshape(1, d), final_norm.reshape(1, d)

    h0 = _rms_fwd(xs, a_norm_f, "a_norm")
    zx = _mm(h0, w_in, name="in_proj")
    z, dtr = (zx, di, 0), (zx, LANES, (di + convd) // LANES)
    xbc = _conv_silu_fwd(zx, di, conv_w_f, conv_b_f, "ssm_conv")
    dt = _softplus_fwd(dtr, dtb, "ssm_dt")
    y, prevs = _ssd_fwd2(xbc, dt, alog, dsk, di, nh, ng, "ssd")
    yn = _gnorm_fwd(y, z, ssm_norm_f, ng, "ssm_gnorm")
    got = _send_wait(first, yn, False, "gather_ffn0_wait")
    w_out = _slabs_to_rows(got[0])
    w_up0, w_down0 = _slabs_to_cols(got[1]), _slabs_to_rows(got[2])
    x1 = _mm(yn, w_out, res=xs, name="ssm_out")
    x2, ffn0 = _ffn_forward2(x1, ffn_norm[0:1], w_up0, fcw[0], w_down0, "ffn0")
    got = _send_wait(rest, x2, False, "gather_rest_wait")
    w_kvf = _slabs_to_cols(got[0])
    w_q = _slabs_to_cols(got[1])
    w_o = _slabs_to_rows(got[2])
    w_up1, w_down1 = _slabs_to_cols(got[3]), _slabs_to_rows(got[4])
    hk = _rms_fwd(x2, kvn, "kv_norm")
    kv = _mm(hk, w_kvf, name="kv_proj")
    h2 = _rms_fwd(x2, b_norm, "b_norm")
    q = _mm(h2, w_q, name="q_proj")
    tabs = _rot_tables(s, 1)
    k_rot = _rot_heads(kv, tabs, kvd, 1.0, F32, "k_rot")
    att = [_attn_fwd_nat(q, k_rot, kv, tabs, g, dil, f"attn{g}") for g, (_, dil) in enumerate(ATT_PATTERNS)]
    o, ob, lse = _merge_heads([t[0] for t in att], [t[1] for t in att], "attn_merge")
    x3 = _mm(ob, w_o, res=x2, name="attn_out")
    x4, ffn1 = _ffn_forward2(x3, ffn_norm[1:2], w_up1, fcw[1], w_down1, "ffn1")
    loss_part, dx4, dx4b, dfin = _final_loss(x4, fin, tgt, "loss_head")

    dx3, dx3b, (dwup1, dfc1, dwdown1, dfn1) = _ffn_backward2(
        x3, ffn1, dx4, dx4b, ffn_norm[1:2], w_up1, fcw[1], w_down1, "ffn1")
    dw_o = _mm(ob, dx3b, ta=True, name="attn_dwo")
    do = _mm(dx3b, w_o, tb=True, name="attn_do")
    delta = _delta_heads(do, o, "attn_delta")
    grads = (lax.empty((n_att * qg // LANES, s, LANES), F32), lax.empty((s, kvd), F32), lax.empty((s, kvd), F32))
    for g, (_, dil) in enumerate(ATT_PATTERNS):
        grads = _attn_bwd_nat(q, k_rot, kv, do, lse, delta, tabs, grads, g, dil, f"attn{g}_bwd")
    dq, dk_rot, dv = grads
    dk = _rot_heads(dk_rot, tabs, kvd, -1.0, BF16, "k_rot_bwd")
    dw_q = _mm(h2, dq, ta=True, b_heads=True, name="q_dw")
    dh2 = _mm(dq, w_q, tb=True, a_heads=True, name="q_dh")
    dw_kv = jnp.concatenate([_mm(hk, dk, ta=True, name="k_dw"), _mm(hk, dv, ta=True, name="v_dw")], axis=1)
    dhk = _mm(dk, w_kvf[:, :kvd], tb=True, name="k_dh")
    dhk = _mm(dv, w_kvf[:, kvd:], tb=True, res=dhk, name="v_dh")
    dx2, _, db_norm = _rms_bwd(x2, b_norm, dh2, dx3, "b_norm_bwd")
    dx2, dx2b, dkv_norm = _rms_bwd(x2, kvn, dhk, dx2, "kv_norm_bwd")
    sent1 = _send_start([_cols_to_slabs(dwup1).astype(BF16), _rows_to_slabs(dwdown1).astype(BF16),
                         _cols_to_slabs(dw_kv).astype(BF16), _cols_to_slabs(dw_q).astype(BF16),
                         _rows_to_slabs(dw_o).astype(BF16)], True, "grads_late_start")
    dx1, dx1b, (dwup0, dfc0, dwdown0, dfn0) = _ffn_backward2(
        x1, ffn0, dx2, dx2b, ffn_norm[0:1], w_up0, fcw[0] + sent1[-1][0, 0], w_down0, "ffn0")
    dw_out = _mm(yn, dx1b, ta=True, name="ssm_dwout")
    sent0 = _send_start([_cols_to_slabs(dwup0).astype(BF16), _rows_to_slabs(dwdown0).astype(BF16),
                         _rows_to_slabs(dw_out).astype(BF16)], True, "grads_ffn0_start")
    dyn = _mm(dx1b, w_out, tb=True, name="ssm_dyn")
    dzx = lax.empty((s, in_pad), BF16)
    dy, dzx, dssm_norm = _gnorm_bwd(dyn, y, z, ssm_norm_f + sent0[-1][0, 0], ng, (dzx, 0), "ssm_gnorm_bwd")
    dxbc, ddt, dalog, ddsk = _ssd_bwd2(xbc, dt, alog, dsk, prevs, dy, di, nh, ng, "ssd_bwd")
    dzx, ddtb = _softplus_bwd(ddt, dtr, dtb, nh, (dzx, (di + convd) // LANES), "ssm_dt_bwd")
    dzx, dconv_w, dconv_b = _conv_silu_bwd(zx, di, conv_w_f, conv_b_f, dxbc, dzx, "ssm_conv_bwd")
    dw_in = _mm(h0, dzx, ta=True, name="in_dw")
    sent_m = _send_start([_cols_to_slabs(dw_in[:, :in_dim])], True, "grads_mamba_start")
    dh0 = _mm(dzx, w_in, tb=True, name="in_dh")
    dx0, _, da_norm = _rms_bwd(xs, a_norm_f + sent_m[-1][0, 0], dh0, dx1, "a_norm_bwd")

    small_full = {
        'a_norm': da_norm, 'ssm_conv_w': dconv_w[None], 'ssm_conv_b': dconv_b, 'ssm_dt_bias': ddtb[:, :nh],
        'ssm_a_log': dalog[:, :nh], 'ssm_d': ddsk[:, :nh], 'ssm_norm': dssm_norm, 'kv_norm': dkv_norm.reshape(d),
        'b_norm': db_norm, 'ffn_norm': jnp.concatenate([dfn0, dfn1], axis=0), 'ffn_conv_w': jnp.stack([dfc0, dfc1]),
        'final_norm': dfin.reshape(d),
    }
    small_names = list(small_full)
    packed, spans = _pack([small_full[n] for n in small_names])
    got1 = _send_wait(sent1, dx0, True, "grads_late_wait")
    got0 = _send_wait(sent0, dx0, True, "grads_ffn0_wait")
    recv_big = {
        'w_kv': got1[2], 'att_w_q': got1[3], 'att_w_o': got1[4],
        'ffn_w_up': jnp.concatenate([got0[0], got1[0]], axis=1),
        'ffn_w_down': jnp.concatenate([got0[1], got1[1]], axis=1),
    }

    me = _dev_index(*_coords())
    res = {}

    def update_big(n, r):
        w = given[n]
        c = w.shape[-1]
        outs = _adamw(r.reshape(NDEV, -1, c), w.reshape(-1, c), given['m_' + n].reshape(-1, c),
                      given['v_' + n].reshape(-1, c), f"adamw_{n}")
        res[n] = [o_.reshape(w.shape) for o_ in outs]
    for n, r in recv_big.items():
        update_big(n, r)
    update_big('ssm_w_out', got0[2])
    recv = _exchange([], [packed], "exchange_grads", after=[res[n][1] for n in res])
    small_sum = _sum_slabs(recv[-1], "sum_small_grads").reshape(-1)
    gotm = _send_wait(sent_m, recv[-1], True, "grads_mamba_wait")
    update_big('ssm_w_in', gotm[0])
    sharded_small = {'a_norm', 'ssm_conv_w', 'ssm_conv_b', 'ssm_norm', 'ffn_conv_w'}
    for n, (off, size) in zip(small_names, spans):
        w = given[n]
        gfull = small_sum[off:off + size].reshape(small_full[n].shape)
        if n in sharded_small:
            c = w.shape[-1]
            gfull = lax.dynamic_slice_in_dim(gfull, me * c, c, axis=gfull.ndim - 1)
        c = w.shape[-1]
        outs = _adamw(gfull.reshape(1, -1, c), w.reshape(-1, c), given['m_' + n].reshape(-1, c),
                      given['v_' + n].reshape(-1, c), f"adamw_{n}")
        res[n] = [o_.reshape(w.shape) for o_ in outs]

    loss = lax.psum(loss_part[0, 0], AXES)
    return (loss, dx0[None], *[res[n][0] for n in WEIGHTS], *[res[n][1] for n in WEIGHTS],
            *[res[n][2] for n in WEIGHTS], *[res[n][3] for n in WEIGHTS])
```

```python
import functools
import math

import jax
import jax.numpy as jnp
from jax import lax
from jax.experimental import pallas as pl
from jax.experimental.pallas import tpu as pltpu

F32, BF16 = jnp.float32, jnp.bfloat16
AXES = ("x", "y", "c")
NDEV = 8
MESH = pl.DeviceIdType.MESH
HIGHEST = lax.Precision.HIGHEST

LANES = 128
SUBLANES = 8
VMEM_LIMIT_BYTES = 48 * 1024 * 1024
VMEM_LIMIT_ATTN_BWD_BYTES = 58 * 1024 * 1024
RMS_BWD_ROWS = 512

RMS_EPS = 1e-6
GATED_NORM_EPS = 1e-5
SSM_HEAD_DIM = 64
SSM_N_GROUPS = 8
SSM_D_STATE = 128
SSM_CONV = 4
SSM_CHUNK = 128
ATT_PATTERNS = ((128, 1), (512, 4), (2048, 16))
ATT_HEAD_DIM = 128
ATT_HEADS_PER_GROUP = 8
ATT_KV_HEADS_PER_GROUP = 2
ATT_BLOCK = 128
ROPE_DIM = ATT_HEAD_DIM // 4
ROPE_THETA = 500000.0
FFN_CONV = 3
ADAM_LR = 0.001
ADAM_B1 = 0.9
ADAM_B2 = 0.999
ADAM_EPS = 1e-08
ADAM_WD = 0.01
ADAM_STEP = 10
NEG = -1e30

WEIGHTS = ['a_norm', 'ssm_w_in', 'ssm_conv_w', 'ssm_conv_b', 'ssm_dt_bias', 'ssm_a_log', 'ssm_d', 'ssm_norm',
           'ssm_w_out', 'kv_norm', 'w_kv', 'b_norm', 'att_w_q', 'att_w_o', 'ffn_norm', 'ffn_w_up', 'ffn_conv_w',
           'ffn_w_down', 'final_norm']


def _params(sem=None, vmem=VMEM_LIMIT_BYTES):
    kw = dict(vmem_limit_bytes=vmem)
    if sem is not None:
        kw["dimension_semantics"] = sem
    return pltpu.CompilerParams(**kw)


def _pick(n, pref):
    if n <= pref:
        return n
    t = (pref // LANES) * LANES
    while t >= LANES:
        if n % t == 0:
            return t
        t -= LANES
    return n


def _dot(a, b, dims=(((1,), (0,)), ((), ())), precision=None):
    return lax.dot_general(a, b, dims, precision=precision, preferred_element_type=F32)


_NT = (((1,), (1,)), ((), ()))
_TN = (((0,), (0,)), ((), ()))


def _mm(a, b, *, ta=False, tb=False, res=None, out_dtype=None, name, tm=1408, tn=1408, tk=2048,
        a_heads=False, b_heads=False, rms_bwd=None):
    assert not (a_heads and ta) and not (b_heads and tb)
    a_parts = a.ndim == 3 and not a_heads
    b_parts = b.ndim == 3 and not b_heads
    assert not (a_parts and ta) and not (b_parts and tb)
    if out_dtype is None:
        out_dtype = BF16 if ta else F32
    if a_heads:
        m, k = a.shape[1], a.shape[0] * LANES
    elif a_parts:
        m, k = a.shape[1], a.shape[0] * a.shape[2]
    else:
        m = a.shape[1] if ta else a.shape[0]
        k = a.shape[0] if ta else a.shape[1]
    if b_heads:
        n, kb = b.shape[0] * LANES, b.shape[1]
    elif b_parts:
        n, kb = b.shape[0] * b.shape[2], b.shape[1]
    else:
        n = b.shape[0] if tb else b.shape[1]
        kb = b.shape[1] if tb else b.shape[0]
    assert k == kb
    tm = _pick(m, tm)
    tn = _pick(b.shape[2], tn) if b_parts else _pick(n, tn)
    tk = _pick(a.shape[2], tk) if a_parts else _pick(k, tk)
    nk = k // tk
    if a_heads:
        a_spec = pl.BlockSpec((tk // LANES, tm, LANES), lambda i, j, l: (l, i, 0))
    elif a_parts:
        per = a.shape[2] // tk
        a_spec = pl.BlockSpec((None, tm, tk), lambda i, j, l: (l // per, i, l % per))
    elif ta:
        a_spec = pl.BlockSpec((tk, tm), lambda i, j, l: (l, i))
    else:
        a_spec = pl.BlockSpec((tm, tk), lambda i, j, l: (i, l))
    if b_heads:
        b_spec = pl.BlockSpec((tn // LANES, tk, LANES), lambda i, j, l: (j, l, 0))
    elif b_parts:
        per_n = b.shape[2] // tn
        b_spec = pl.BlockSpec((None, tk, tn), lambda i, j, l: (j // per_n, l, j % per_n))
    elif tb:
        b_spec = pl.BlockSpec((tn, tk), lambda i, j, l: (j, l))
    else:
        b_spec = pl.BlockSpec((tk, tn), lambda i, j, l: (l, j))
    o_spec = pl.BlockSpec((tm, tn), lambda i, j, l: (i, j))
    dims = (((0 if ta else 1,), (1 if tb else 0,)), ((), ()))
    has_res = res is not None
    has_rms = rms_bwd is not None
    assert not (has_res and has_rms) and (not has_rms or tn == n)
    n_extra = 3 if has_rms else int(has_res)
    n_out = 3 if has_rms else 1

    def load(ref, heads):
        if not heads:
            return ref[...].astype(BF16)
        return jnp.concatenate([ref[i].astype(BF16) for i in range(ref.shape[0])], axis=1)

    def body(*refs):
        a_ref, b_ref = refs[:2]
        extra = refs[2:2 + n_extra]
        outs = refs[2 + n_extra:2 + n_extra + n_out]
        p = _dot(load(a_ref, a_heads), load(b_ref, b_heads), dims)

        def finish(r):
            if has_res:
                r = r + extra[0][...]
            if not has_rms:
                outs[0][...] = r.astype(outs[0].dtype)
                return
            x_ref, w_ref, dres_ref = extra
            dx_ref, dxb_ref, dw_ref = outs
            xv = x_ref[...]
            rs = lax.rsqrt(jnp.mean(xv * xv, axis=-1, keepdims=True) + RMS_EPS)
            xh = xv * rs
            dxh = r * w_ref[...]
            dx = dres_ref[...] + rs * (dxh - xh * jnp.mean(dxh * xh, axis=-1, keepdims=True))
            dx_ref[...] = dx
            dxb_ref[...] = dx.astype(BF16)

            @pl.when(pl.program_id(0) == 0)
            def _():
                dw_ref[...] = jnp.zeros(dw_ref.shape, F32)

            dw_ref[...] += jnp.sum(r * xh, axis=0, keepdims=True)

        if nk == 1:
            finish(p)
            return
        acc = refs[2 + n_extra + n_out]
        l = pl.program_id(2)

        @pl.when(l == 0)
        def _():
            acc[...] = p

        @pl.when(jnp.logical_and(l > 0, l < nk - 1))
        def _():
            acc[...] += p

        @pl.when(l == nk - 1)
        def _():
            finish(acc[...] + p)

    scratch = [pltpu.VMEM((tm, tn), F32)] if nk > 1 else []
    if has_rms:
        x, w, dres = rms_bwd
        vec = pl.BlockSpec((1, n), lambda i, j, l: (0, 0))
        return pl.pallas_call(
            body, name=name, grid=(m // tm, 1, nk), in_specs=[a_spec, b_spec, o_spec, vec, o_spec],
            out_specs=[o_spec, o_spec, vec], scratch_shapes=scratch,
            out_shape=[jax.ShapeDtypeStruct((m, n), F32), jax.ShapeDtypeStruct((m, n), BF16),
                       jax.ShapeDtypeStruct((1, n), F32)],
            compiler_params=_params(("arbitrary", "arbitrary", "arbitrary")))(a, b, x, w, dres)
    ins = [a, b] + ([res] if has_res else [])
    in_specs = [a_spec, b_spec] + ([o_spec] if has_res else [])
    return pl.pallas_call(
        body, name=name, grid=(m // tm, n // tn, nk), in_specs=in_specs, out_specs=o_spec,
        out_shape=jax.ShapeDtypeStruct((m, n), out_dtype), scratch_shapes=scratch,
        compiler_params=_params(("parallel", "parallel", "arbitrary")))(*ins)


def _rowwise(fn, rows, bcasts, outs, accs=(), *, tile, name):
    s = (rows[0][0] if isinstance(rows[0], tuple) else rows[0]).shape[-2]
    tile = min(tile, s)
    n_val = len(rows) + len(bcasts)
    intos = [(k, o) for k, o in enumerate(outs) if len(o) == 4]
    n_in, n_out, n_acc = n_val + len(intos), len(outs), len(accs)

    def row_spec(c):
        if isinstance(c, tuple):
            return pl.BlockSpec((c[0], tile, c[1]), lambda i: (0, i, 0))
        return pl.BlockSpec((tile, c), lambda i: (i, 0))

    def row_shape(c):
        return (c[0], s, c[1]) if isinstance(c, tuple) else (s, c)

    def window(width, cb):
        return pl.BlockSpec((tile, width), lambda i: (i, cb))

    def body(*refs):
        vals = fn(*[r[...] for r in refs[:n_val]])
        o_refs = refs[n_in:n_in + n_out]
        a_refs = refs[n_in + n_out:]
        for r, v in zip(o_refs, vals[:n_out]):
            if isinstance(v, list):
                for i, vi in enumerate(v):
                    r[i] = vi.astype(r.dtype)
            else:
                r[...] = v.astype(r.dtype)

        @pl.when(pl.program_id(0) == 0)
        def _():
            for r in a_refs:
                r[...] = jnp.zeros(r.shape, r.dtype)

        for r, v in zip(a_refs, vals[n_out:]):
            r[...] += v

    in_specs = [window(r[1], r[2]) if isinstance(r, tuple)
                else row_spec(r.shape[1] if r.ndim == 2 else (r.shape[0], r.shape[2])) for r in rows]
    in_specs += [pl.BlockSpec(b.shape, lambda i: (0, 0)) for b in bcasts]
    in_specs += [pl.BlockSpec(memory_space=pl.ANY) for _ in intos]
    out_specs = [window(o[0], o[3]) if len(o) == 4 else row_spec(o[0]) for o in outs]
    out_specs += [pl.BlockSpec(sh, lambda i: (0, 0)) for sh, _ in accs]
    out_shape = [jax.ShapeDtypeStruct(o[2].shape, o[2].dtype) if len(o) == 4
                 else jax.ShapeDtypeStruct(row_shape(o[0]), o[1]) for o in outs]
    out_shape += [jax.ShapeDtypeStruct(sh, dt) for sh, dt in accs]
    args = [r[0] if isinstance(r, tuple) else r for r in rows] + list(bcasts) + [o[2] for _, o in intos]
    return pl.pallas_call(body, name=name, grid=(s // tile,), in_specs=in_specs, out_specs=out_specs,
                          out_shape=out_shape, input_output_aliases={n_val + i: k for i, (k, _) in enumerate(intos)},
                          compiler_params=_params(("arbitrary",)))(*args)


def _rms_fwd(x, w, name):
    def fn(x, w):
        r = lax.rsqrt(jnp.mean(x * x, axis=-1, keepdims=True) + RMS_EPS)
        return (x * r * w,)
    return _rowwise(fn, [x], [w], [(x.shape[1], BF16)], tile=256, name=name)[0]


def _rms_bwd(x, w, dh, dres, name):
    def fn(x, dh, dres, w):
        r = lax.rsqrt(jnp.mean(x * x, axis=-1, keepdims=True) + RMS_EPS)
        xh = x * r
        dxh = dh * w
        dx = dres + r * (dxh - xh * jnp.mean(dxh * xh, axis=-1, keepdims=True))
        return dx, dx, jnp.sum(dh * xh, axis=0, keepdims=True)
    d = x.shape[1]
    return _rowwise(fn, [x, dh, dres], [w], [(d, F32), (d, BF16)], [((1, d), F32)], tile=256, name=name)


def _final_loss(x, w, tgt, name):
    d = x.shape[1]

    def fn(x, t, w):
        r = lax.rsqrt(jnp.mean(x * x, axis=-1, keepdims=True) + RMS_EPS)
        xh = x * r
        err = xh * w - t
        part = jnp.sum(jnp.mean(err * err, axis=-1, keepdims=True), axis=0, keepdims=True) * 0.5
        dy = err * (1.0 / d)
        dxh = dy * w
        dx = r * (dxh - xh * jnp.mean(dxh * xh, axis=-1, keepdims=True))
        return dx, dx, part, jnp.sum(dy * xh, axis=0, keepdims=True)
    dx, dxb, part, dw = _rowwise(fn, [x, tgt], [w], [(d, F32), (d, BF16)], [((1, 1), F32), ((1, d), F32)],
                                 tile=256, name=name)
    return part, dx, dxb, dw


def _softplus_fwd(dtr, bias, name):
    def fn(r, b):
        v = r + b
        return (jnp.maximum(v, 0.0) + jnp.log(1.0 + jnp.exp(-jnp.abs(v))),)
    return _rowwise(fn, [dtr], [bias], [(LANES, F32)], tile=512, name=name)[0]


def _softplus_bwd(ddt, dtr, bias, n_heads, into, name):
    def fn(g, r, b):
        lane = lax.broadcasted_iota(jnp.int32, g.shape, 1)
        d = jnp.where(lane < n_heads, g * jax.nn.sigmoid(r + b), 0.0)
        return d, jnp.sum(d, axis=0, keepdims=True)
    return _rowwise(fn, [ddt, dtr], [bias], [(LANES, BF16, *into)], [((1, LANES), F32)], tile=512, name=name)


def _gnorm_fwd(y, z, w, n_groups, name):
    di = y.shape[1]
    gs = di // n_groups

    def fn(y, z, w):
        y2 = y * (z * jax.nn.sigmoid(z))
        out = []
        for g in range(n_groups):
            sl = y2[:, g * gs:(g + 1) * gs]
            r = lax.rsqrt(jnp.mean(sl * sl, axis=-1, keepdims=True) + GATED_NORM_EPS)
            out.append(sl * r)
        return (jnp.concatenate(out, axis=1) * w,)
    return _rowwise(fn, [y, z], [w], [(di, BF16)], tile=256, name=name)[0]


def _gnorm_bwd(dyn, y, z, w, n_groups, into, name):
    di = y.shape[1]
    gs = di // n_groups

    def fn(dyn, y, z, w):
        sig = jax.nn.sigmoid(z)
        sz = z * sig
        y2 = y * sz
        d2n = dyn * w
        dy2, yhat = [], []
        for g in range(n_groups):
            sl = y2[:, g * gs:(g + 1) * gs]
            dg = d2n[:, g * gs:(g + 1) * gs]
            r = lax.rsqrt(jnp.mean(sl * sl, axis=-1, keepdims=True) + GATED_NORM_EPS)
            yh = sl * r
            dy2.append(r * (dg - yh * jnp.mean(dg * yh, axis=-1, keepdims=True)))
            yhat.append(yh)
        dy2 = jnp.concatenate(dy2, axis=1)
        yhat = jnp.concatenate(yhat, axis=1)
        dz = dy2 * y * (sig * (1.0 + z * (1.0 - sig)))
        return dy2 * sz, dz, jnp.sum(dyn * yhat, axis=0, keepdims=True)
    return _rowwise(fn, [dyn, y, z], [w], [(di, F32), (di, BF16, *into)], [((1, di), F32)], tile=128, name=name)


def _merge_fwd(os_, lses, name):
    n = len(os_)

    def fn(*v):
        o, l = v[:n], v[n:]
        m = functools.reduce(jnp.maximum, l)
        e = [jnp.exp(li - m) for li in l]
        tot = functools.reduce(jnp.add, e)
        acc = functools.reduce(jnp.add, [ei * oi for ei, oi in zip(e, o)]) / tot
        return acc, acc, m + jnp.log(tot)
    c = os_[0].shape[1]
    return _rowwise(fn, list(os_) + list(lses), [], [(c, F32), (c, BF16), (c, F32)], tile=256, name=name)


def _delta(do, o, name):
    c = o.shape[1]

    def fn(do, o):
        p = do * o
        out = [jnp.broadcast_to(jnp.sum(p[:, j:j + ATT_HEAD_DIM], axis=-1, keepdims=True), (p.shape[0], ATT_HEAD_DIM))
               for j in range(0, c, ATT_HEAD_DIM)]
        return (jnp.concatenate(out, axis=1),)
    return _rowwise(fn, [do, o], [], [(c, F32)], tile=256, name=name)[0]


def _lane_place(cols):
    rows = cols[0].shape[0]
    lane = lax.broadcasted_iota(jnp.int32, (rows, LANES), 1)
    out = jnp.zeros((rows, LANES), F32)
    for j, c in enumerate(cols):
        out = jnp.where(lane == j, c, out)
    return out


def _merge_heads(os_, lses, name):
    n = len(os_)
    n_kv, rep, hd = ATT_KV_HEADS_PER_GROUP, ATT_HEADS_PER_GROUP // ATT_KV_HEADS_PER_GROUP, ATT_HEAD_DIM

    def fn(*v):
        o, l = v[:n], v[n:]
        out, lse = [], []
        for h in range(n_kv):
            cols = []
            for j in range(rep):
                hh = h * rep + j
                lg = [li[h][:, j:j + 1] for li in l]
                m = functools.reduce(jnp.maximum, lg)
                e = [jnp.exp(x - m) for x in lg]
                tot = functools.reduce(jnp.add, e)
                acc = functools.reduce(jnp.add, [ei * oi[hh] for ei, oi in zip(e, o)])
                out.append(acc / tot)
                cols.append(m + jnp.log(tot))
            lse.append(_lane_place(cols))
        merged = jnp.concatenate(out, axis=1)
        return merged, merged, lse
    c = os_[0].shape[0] * hd
    return _rowwise(fn, list(os_) + list(lses), [], [(c, F32), (c, BF16), ((n_kv, LANES), F32)], tile=256, name=name)


def _delta_heads(do, o, name):
    n_kv, rep, hd = ATT_KV_HEADS_PER_GROUP, ATT_HEADS_PER_GROUP // ATT_KV_HEADS_PER_GROUP, ATT_HEAD_DIM

    def fn(do, o):
        p = do * o
        return ([_lane_place([jnp.sum(p[:, (h * rep + j) * hd:(h * rep + j + 1) * hd], axis=-1, keepdims=True)
                              for j in range(rep)]) for h in range(n_kv)],)
    return _rowwise(fn, [do, o], [], [((n_kv, LANES), F32)], tile=256, name=name)[0]


def _sum_slabs(recv, name):
    def body(r_ref, o_ref):
        acc = r_ref[0]
        for k in range(1, NDEV):
            acc = acc + r_ref[k]
        o_ref[...] = acc
    return pl.pallas_call(body, name=name, out_shape=jax.ShapeDtypeStruct(recv.shape[1:], F32),
                          compiler_params=_params())(recv)


def _shift_down(x, k):
    if k == 0:
        return x
    r = pltpu.roll(x, k, 0)
    row = lax.broadcasted_iota(jnp.int32, (SUBLANES, x.shape[1]), 0)
    return jnp.concatenate([jnp.where(row >= k, r[:SUBLANES], 0.0), r[SUBLANES:]], axis=0)


def _shift_up(x, k):
    if k == 0:
        return x
    s = x.shape[0]
    r = pltpu.roll(x, s - k, 0)
    row = lax.broadcasted_iota(jnp.int32, (SUBLANES, x.shape[1]), 0)
    return jnp.concatenate([r[:s - SUBLANES], jnp.where(row < SUBLANES - k, r[s - SUBLANES:], 0.0)], axis=0)


def _conv(x, w):
    kw = w.shape[0]
    return functools.reduce(jnp.add, [w[k:k + 1, :] * _shift_down(x, kw - 1 - k) for k in range(kw)])


def _conv_t(dy, w):
    kw = w.shape[0]
    return functools.reduce(jnp.add, [w[k:k + 1, :] * _shift_up(dy, kw - 1 - k) for k in range(kw)])


def _conv_dw(x, dy, dw_ref):
    kw = dw_ref.shape[0]
    for k in range(kw):
        dw_ref[k:k + 1, :] = jnp.sum(dy * _shift_down(x, kw - 1 - k), axis=0, keepdims=True)


def _dsilu(pre):
    sig = jax.nn.sigmoid(pre)
    return sig * (1.0 + pre * (1.0 - sig))


def _col_specs(s, c, kw, tc):
    return (pl.BlockSpec((s, tc), lambda j: (0, j)), pl.BlockSpec((kw, tc), lambda j: (0, j)),
            pl.BlockSpec((1, tc), lambda j: (0, j)))


def _conv_silu_fwd(x, col0, w, b, name):
    s, c = x.shape[0], w.shape[1]
    tc = LANES
    xs, ws, bs = _col_specs(s, c, w.shape[0], tc)
    xwin = pl.BlockSpec((s, tc), lambda j: (0, j + col0 // tc))

    def body(x_ref, w_ref, b_ref, o_ref):
        pre = _conv(x_ref[...], w_ref[...]) + b_ref[...]
        o_ref[...] = pre * jax.nn.sigmoid(pre)
    return pl.pallas_call(body, name=name, grid=(c // tc,), in_specs=[xwin, ws, bs], out_specs=xs,
                          out_shape=jax.ShapeDtypeStruct((s, c), F32), compiler_params=_params(("parallel",)))(x, w, b)


def _conv_silu_bwd(x, col0, w, b, dy, into, name):
    s, c = x.shape[0], w.shape[1]
    tc = LANES
    xs, ws, bs = _col_specs(s, c, w.shape[0], tc)
    xwin = pl.BlockSpec((s, tc), lambda j: (0, j + col0 // tc))

    def body(x_ref, w_ref, b_ref, dy_ref, _, dx_ref, dw_ref, db_ref):
        xv, wv = x_ref[...], w_ref[...]
        pre = _conv(xv, wv) + b_ref[...]
        dpre = dy_ref[...] * _dsilu(pre)
        dx_ref[...] = _conv_t(dpre, wv).astype(dx_ref.dtype)
        _conv_dw(xv, dpre, dw_ref)
        db_ref[...] = jnp.sum(dpre, axis=0, keepdims=True)
    return pl.pallas_call(
        body, name=name, grid=(c // tc,), in_specs=[xwin, ws, bs, xs, pl.BlockSpec(memory_space=pl.ANY)],
        out_specs=[xwin, ws, bs], input_output_aliases={4: 0},
        out_shape=[jax.ShapeDtypeStruct(into.shape, into.dtype), jax.ShapeDtypeStruct(w.shape, F32),
                   jax.ShapeDtypeStruct((1, c), F32)],
        compiler_params=_params(("parallel",)))(x, w, b, dy, into)


def _gate_specs(s, f, kw):
    nt = f // LANES
    return (pl.BlockSpec((s, LANES), lambda j: (0, j)), pl.BlockSpec((s, LANES), lambda j: (0, j + nt)),
            pl.BlockSpec((kw, LANES), lambda j: (0, j)), pl.BlockSpec((kw, LANES), lambda j: (0, j + nt)))


def _ffn_gate_fwd2(u, w, name):
    s, f = u.shape[0], u.shape[1] // 2
    gs, vs, wgs, wvs = _gate_specs(s, f, w.shape[0])

    def body(g_ref, v_ref, wg_ref, wv_ref, o_ref):
        g = _conv(g_ref[...], wg_ref[...])
        v = _conv(v_ref[...], wv_ref[...])
        o_ref[...] = (g * jax.nn.sigmoid(g) * v).astype(o_ref.dtype)
    return pl.pallas_call(body, name=name, grid=(f // LANES,), in_specs=[gs, vs, wgs, wvs], out_specs=gs,
                          out_shape=jax.ShapeDtypeStruct((s, f), BF16),
                          compiler_params=_params(("parallel",)))(u, u, w, w)


def _ffn_gate_bwd2(u, w, df, name):
    s, f = u.shape[0], u.shape[1] // 2
    kw = w.shape[0]
    gs, vs, wgs, wvs = _gate_specs(s, f, kw)

    def body(g_ref, v_ref, wg_ref, wv_ref, df_ref, du_ref, dw_ref):
        gp, vp, wgv, wvv = g_ref[...], v_ref[...], wg_ref[...], wv_ref[...]
        g = _conv(gp, wgv)
        v = _conv(vp, wvv)
        dfv = df_ref[...]
        dg = dfv * v * _dsilu(g)
        dv = dfv * (g * jax.nn.sigmoid(g))
        du_ref[0] = _conv_t(dg, wgv).astype(du_ref.dtype)
        du_ref[1] = _conv_t(dv, wvv).astype(du_ref.dtype)
        _conv_dw(gp, dg, dw_ref.at[0])
        _conv_dw(vp, dv, dw_ref.at[1])
    return pl.pallas_call(
        body, name=name, grid=(f // LANES,), in_specs=[gs, vs, wgs, wvs, gs],
        out_specs=[pl.BlockSpec((2, s, LANES), lambda j: (0, 0, j)), pl.BlockSpec((2, kw, LANES), lambda j: (0, 0, j))],
        out_shape=[jax.ShapeDtypeStruct((2, s, f), BF16), jax.ShapeDtypeStruct((2, kw, f), F32)],
        compiler_params=_params(("parallel",)))(u, u, w, w, df)


def _ffn_gate_fwd(ug, uv, wg, wv, name):
    s, c = ug.shape
    tc = LANES
    xs, ws, _ = _col_specs(s, c, wg.shape[0], tc)

    def body(g_ref, v_ref, wg_ref, wv_ref, o_ref):
        g = _conv(g_ref[...], wg_ref[...])
        v = _conv(v_ref[...], wv_ref[...])
        o_ref[...] = (g * jax.nn.sigmoid(g) * v).astype(o_ref.dtype)
    return pl.pallas_call(body, name=name, grid=(c // tc,), in_specs=[xs, xs, ws, ws], out_specs=xs,
                          out_shape=jax.ShapeDtypeStruct((s, c), BF16),
                          compiler_params=_params(("parallel",)))(ug, uv, wg, wv)


def _ffn_gate_bwd(ug, uv, wg, wv, df, name):
    s, c = ug.shape
    tc = LANES
    xs, ws, _ = _col_specs(s, c, wg.shape[0], tc)

    def body(g_ref, v_ref, wg_ref, wv_ref, df_ref, dg_ref, dv_ref, dwg_ref, dwv_ref):
        gp, vp, wgv, wvv = g_ref[...], v_ref[...], wg_ref[...], wv_ref[...]
        g = _conv(gp, wgv)
        v = _conv(vp, wvv)
        dfv = df_ref[...]
        dg = dfv * v * _dsilu(g)
        dv = dfv * (g * jax.nn.sigmoid(g))
        dg_ref[...] = _conv_t(dg, wgv).astype(dg_ref.dtype)
        dv_ref[...] = _conv_t(dv, wvv).astype(dv_ref.dtype)
        _conv_dw(gp, dg, dwg_ref)
        _conv_dw(vp, dv, dwv_ref)
    return pl.pallas_call(
        body, name=name, grid=(c // tc,), in_specs=[xs, xs, ws, ws, xs], out_specs=[xs, xs, ws, ws],
        out_shape=[jax.ShapeDtypeStruct((s, c), BF16), jax.ShapeDtypeStruct((s, c), BF16),
                   jax.ShapeDtypeStruct(wg.shape, F32), jax.ShapeDtypeStruct(wv.shape, F32)],
        compiler_params=_params(("parallel",)))(ug, uv, wg, wv, df)


def _ssd_common(dt, alog, n_heads):
    ln = dt.shape[0]
    lane = lax.broadcasted_iota(jnp.int32, (1, LANES), 1)
    a = jnp.where(lane < n_heads, -jnp.exp(alog), 0.0)
    row = lax.broadcasted_iota(jnp.int32, (ln, ln), 0)
    col = lax.broadcasted_iota(jnp.int32, (ln, ln), 1)
    tril = col <= row
    acs = _dot(tril.astype(F32), dt * a, precision=HIGHEST)
    return a, acs, acs.T, tril


def _ssd_fwd(xbc, dt, alog, dskip, di, n_heads, n_groups, name):
    s, convd = xbc.shape
    ln, p, ns = SSM_CHUNK, SSM_HEAD_DIM, SSM_D_STATE
    nc, hg = s // ln, n_heads // n_groups

    def body(x_ref, dt_ref, alog_ref, d_ref, y_ref, prev_ref, st):
        @pl.when(pl.program_id(0) == 0)
        def _():
            st[...] = jnp.zeros(st.shape, F32)

        dt = dt_ref[...]
        _, acs, acs_t, tril = _ssd_common(dt, alog_ref[...], n_heads)
        e_all = jnp.exp(acs)
        last = acs[ln - 1:ln, :]
        ds_all = jnp.exp(last - acs)
        t_all = jnp.exp(last)
        dsk = d_ref[...]
        for g in range(n_groups):
            bg = x_ref[:, di + g * ns:di + (g + 1) * ns].astype(BF16)
            cg = x_ref[:, di + (n_groups + g) * ns:di + (n_groups + g + 1) * ns].astype(BF16)
            gm = _dot(cg, bg, _NT)
            for j in range(hg):
                h = g * hg + j
                xh = x_ref[:, h * p:(h + 1) * p]
                xdt = xh * dt[:, h:h + 1]
                seg = acs[:, h:h + 1] - acs_t[h:h + 1, :]
                m = jnp.where(tril, gm * jnp.exp(jnp.where(tril, seg, 0.0)), 0.0)
                prev = st[h]
                prev_ref[0, h] = prev
                y = _dot(m.astype(BF16), xdt.astype(BF16))
                y = y + _dot(cg, prev.astype(BF16), _NT) * e_all[:, h:h + 1]
                y = y + xh * dsk[:, h:h + 1]
                snew = _dot((xdt * ds_all[:, h:h + 1]).astype(BF16), bg, _TN)
                st[h] = prev * t_all[:, h:h + 1] + snew
                y_ref[:, h * p:(h + 1) * p] = y

    vec = pl.BlockSpec((1, LANES), lambda c: (0, 0))
    return pl.pallas_call(
        body, name=name, grid=(nc,),
        in_specs=[pl.BlockSpec((ln, convd), lambda c: (c, 0)), pl.BlockSpec((ln, LANES), lambda c: (c, 0)), vec, vec],
        out_specs=[pl.BlockSpec((ln, di), lambda c: (c, 0)),
                   pl.BlockSpec((1, n_heads, p, ns), lambda c: (c, 0, 0, 0))],
        out_shape=[jax.ShapeDtypeStruct((s, di), F32), jax.ShapeDtypeStruct((nc, n_heads, p, ns), F32)],
        scratch_shapes=[pltpu.VMEM((n_heads, p, ns), F32)],
        compiler_params=_params(("arbitrary",)))(xbc, dt, alog, dskip)


def _ssd_bwd(xbc, dt, alog, dskip, prev_all, dy, di, n_heads, n_groups, name):
    s, convd = xbc.shape
    ln, p, ns = SSM_CHUNK, SSM_HEAD_DIM, SSM_D_STATE
    nc, hg = s // ln, n_heads // n_groups

    def body(x_ref, dt_ref, alog_ref, d_ref, prev_ref, dy_ref, dx_ref, ddt_ref, da_ref, dd_ref, dh):
        step = pl.program_id(0)

        @pl.when(step == 0)
        def _():
            dh[...] = jnp.zeros(dh.shape, F32)
            da_ref[...] = jnp.zeros(da_ref.shape, F32)
            dd_ref[...] = jnp.zeros(dd_ref.shape, F32)

        dt = dt_ref[...]
        a, acs, acs_t, tril = _ssd_common(dt, alog_ref[...], n_heads)
        e_all = jnp.exp(acs)
        last = acs[ln - 1:ln, :]
        ds_all = jnp.exp(last - acs)
        t_all = jnp.exp(last)
        dsk = d_ref[...]
        lane = lax.broadcasted_iota(jnp.int32, (ln, LANES), 1)
        lane1 = lax.broadcasted_iota(jnp.int32, (1, LANES), 1)
        sub = lax.broadcasted_iota(jnp.int32, (LANES, ln), 0)
        rowi = lax.broadcasted_iota(jnp.int32, (ln, LANES), 0)
        dacs_c = jnp.zeros((ln, LANES), F32)
        dacs_r = jnp.zeros((LANES, ln), F32)
        dlast = jnp.zeros((1, LANES), F32)
        ddt_x = jnp.zeros((ln, LANES), F32)
        dd = jnp.zeros((1, LANES), F32)

        def tot(v):
            return jnp.sum(jnp.sum(v, axis=1, keepdims=True), axis=0, keepdims=True)

        for g in range(n_groups):
            bg = x_ref[:, di + g * ns:di + (g + 1) * ns].astype(BF16)
            cg = x_ref[:, di + (n_groups + g) * ns:di + (n_groups + g + 1) * ns].astype(BF16)
            gm = _dot(cg, bg, _NT)
            dgm = jnp.zeros((ln, ln), F32)
            dcg = jnp.zeros((ln, ns), F32)
            dbg = jnp.zeros((ln, ns), F32)
            for j in range(hg):
                h = g * hg + j
                xh = x_ref[:, h * p:(h + 1) * p]
                dth = dt[:, h:h + 1]
                xdt = xh * dth
                dyh = dy_ref[:, h * p:(h + 1) * p]
                eh, dsh, th = e_all[:, h:h + 1], ds_all[:, h:h + 1], t_all[:, h:h + 1]
                seg = acs[:, h:h + 1] - acs_t[h:h + 1, :]
                dec = jnp.where(tril, jnp.exp(jnp.where(tril, seg, 0.0)), 0.0)
                m = gm * dec
                prev = prev_ref[0, h]
                dhn = dh[h]
                prevb, dhb, dyb, xdtb = prev.astype(BF16), dhn.astype(BF16), dyh.astype(BF16), xdt.astype(BF16)
                yo = _dot(cg, prevb, _NT)
                dyob = (dyh * eh).astype(BF16)
                c_col = jnp.sum(dyh * yo, axis=1, keepdims=True) * eh
                dcg = dcg + _dot(dyob, prevb)
                dprev = th * dhn + _dot(dyob, cg, _TN)
                dtt = tot(dhn * prev)
                w = _dot(bg, dhb, _NT)
                dxdt = w * dsh
                dds = jnp.sum(w * xdt, axis=1, keepdims=True)
                dbg = dbg + _dot((xdt * dsh).astype(BF16), dhb)
                dm = _dot(dyb, xdtb, _NT)
                dxdt = dxdt + _dot(m.astype(BF16), dyb, _TN)
                dgm = dgm + dm * dec
                q = dm * m
                c_col = c_col + jnp.sum(q, axis=1, keepdims=True) - dds * dsh
                r_row = -jnp.sum(q, axis=0, keepdims=True)
                dlast_h = tot(dds * dsh) + dtt * th
                dacs_c = dacs_c + jnp.where(lane == h, c_col, 0.0)
                dacs_r = dacs_r + jnp.where(sub == h, r_row, 0.0)
                dlast = dlast + jnp.where(lane1 == h, dlast_h, 0.0)
                ddt_x = ddt_x + jnp.where(lane == h, jnp.sum(dxdt * xh, axis=1, keepdims=True), 0.0)
                dd = dd + jnp.where(lane1 == h, tot(dyh * xh), 0.0)
                dx_ref[:, h * p:(h + 1) * p] = dxdt * dth + dyh * dsk[:, h:h + 1]
                dh[h] = dprev
            dgb = dgm.astype(BF16)
            dx_ref[:, di + g * ns:di + (g + 1) * ns] = dbg + _dot(dgb, cg, _TN)
            dx_ref[:, di + (n_groups + g) * ns:di + (n_groups + g + 1) * ns] = dcg + _dot(dgb, bg)

        dacs = dacs_c + dacs_r.T + jnp.where(rowi == ln - 1, dlast, 0.0)
        row = lax.broadcasted_iota(jnp.int32, (ln, ln), 0)
        col = lax.broadcasted_iota(jnp.int32, (ln, ln), 1)
        dadt = _dot((col >= row).astype(F32), dacs, precision=HIGHEST)
        ddt_ref[...] = dadt * a + ddt_x
        da_ref[...] += jnp.sum(dadt * dt, axis=0, keepdims=True)
        dd_ref[...] += dd

        @pl.when(step == nc - 1)
        def _():
            da_ref[...] = da_ref[...] * a

    vec = pl.BlockSpec((1, LANES), lambda c: (0, 0))
    rev = lambda c: (nc - 1 - c, 0)
    return pl.pallas_call(
        body, name=name, grid=(nc,),
        in_specs=[pl.BlockSpec((ln, convd), rev), pl.BlockSpec((ln, LANES), rev), vec, vec,
                  pl.BlockSpec((1, n_heads, p, ns), lambda c: (nc - 1 - c, 0, 0, 0)), pl.BlockSpec((ln, di), rev)],
        out_specs=[pl.BlockSpec((ln, convd), rev), pl.BlockSpec((ln, LANES), rev), vec, vec],
        out_shape=[jax.ShapeDtypeStruct((s, convd), F32), jax.ShapeDtypeStruct((s, LANES), F32),
                   jax.ShapeDtypeStruct((1, LANES), F32), jax.ShapeDtypeStruct((1, LANES), F32)],
        scratch_shapes=[pltpu.VMEM((n_heads, p, ns), F32)],
        compiler_params=_params(("arbitrary",)))(xbc, dt, alog, dskip, prev_all, dy)


def _split(x, n):
    out = []
    for _ in range(n):
        piece = x.astype(BF16)
        out.append(piece)
        x = x - piece.astype(F32)
    return out


def _spread(x, onehot, n=2):
    return functools.reduce(jnp.add, [_dot(piece, onehot) for piece in _split(x, n)])


def _head_maps(di, p):
    e = (jnp.arange(di, dtype=jnp.int32)[None, :] // p == jnp.arange(LANES, dtype=jnp.int32)[:, None]).astype(BF16)
    return e, e.T


def _ssd_wide(dt, acs, acs_t, dskip, e_ref, et_ref):
    ln = dt.shape[0]
    last = acs[ln - 1:ln, :]
    stack = jnp.concatenate([dt, jnp.exp(acs), jnp.exp(last - acs), jnp.broadcast_to(dskip, (8, LANES))], axis=0)
    wide = _spread(stack, e_ref[...])
    tb = jnp.exp(jnp.broadcast_to(acs_t[:, ln - 1:ln], (LANES, LANES)))
    texp = functools.reduce(jnp.add, [_dot(et_ref[...], piece) for piece in _split(tb, 3)])
    return wide[:ln], wide[ln:2 * ln], wide[2 * ln:3 * ln], wide[3 * ln:3 * ln + 1], texp


def _ssd_fwd2(xbc, dt, alog, dskip, di, n_heads, n_groups, name):
    s, convd = xbc.shape
    ln, p, ns = SSM_CHUNK, SSM_HEAD_DIM, SSM_D_STATE
    nc, hg = s // ln, n_heads // n_groups
    gw = hg * p
    e64, e64t = _head_maps(di, p)

    def body(x_ref, dt_ref, alog_ref, d_ref, e_ref, et_ref, y_ref, prev_ref, st):
        @pl.when(pl.program_id(0) == 0)
        def _():
            st[...] = jnp.zeros(st.shape, F32)

        dt = dt_ref[...]
        _, acs, acs_t, tril = _ssd_common(dt, alog_ref[...], n_heads)
        dte, ee, dse, dske, texp = _ssd_wide(dt, acs, acs_t, d_ref[...], e_ref, et_ref)
        x = x_ref[:, :di]
        xdt = x * dte
        xdtb = xdt.astype(BF16)
        xdsb = (xdt * dse).astype(BF16)
        for g in range(n_groups):
            rows = slice(g * gw, (g + 1) * gw)
            bg = x_ref[:, di + g * ns:di + (g + 1) * ns].astype(BF16)
            cg = x_ref[:, di + (n_groups + g) * ns:di + (n_groups + g + 1) * ns].astype(BF16)
            gm = _dot(cg, bg, _NT)
            prev = st[rows, :]
            prev_ref[0, rows, :] = prev
            yo = _dot(cg, prev.astype(BF16), _NT)
            for j in range(hg):
                h = g * hg + j
                seg = acs[:, h:h + 1] - acs_t[h:h + 1, :]
                m = jnp.where(tril, gm * jnp.exp(jnp.where(tril, seg, 0.0)), 0.0)
                y_ref[:, h * p:(h + 1) * p] = _dot(m.astype(BF16), xdtb[:, h * p:(h + 1) * p])
            y_ref[:, rows] = y_ref[:, rows] + yo * ee[:, rows] + x[:, rows] * dske[:, rows]
            st[rows, :] = prev * texp[rows, :] + _dot(xdsb[:, rows], bg, _TN)

    vec = pl.BlockSpec((1, LANES), lambda c: (0, 0))
    return pl.pallas_call(
        body, name=name, grid=(nc,),
        in_specs=[pl.BlockSpec((ln, convd), lambda c: (c, 0)), pl.BlockSpec((ln, LANES), lambda c: (c, 0)), vec, vec,
                  pl.BlockSpec(e64.shape, lambda c: (0, 0)), pl.BlockSpec(e64t.shape, lambda c: (0, 0))],
        out_specs=[pl.BlockSpec((ln, di), lambda c: (c, 0)), pl.BlockSpec((1, di, ns), lambda c: (c, 0, 0))],
        out_shape=[jax.ShapeDtypeStruct((s, di), F32), jax.ShapeDtypeStruct((nc, di, ns), F32)],
        scratch_shapes=[pltpu.VMEM((di, ns), F32)],
        compiler_params=_params(("arbitrary",)))(xbc, dt, alog, dskip, e64, e64t)


def _ssd_bwd2(xbc, dt, alog, dskip, prev_all, dy, di, n_heads, n_groups, name):
    s, convd = xbc.shape
    ln, p, ns = SSM_CHUNK, SSM_HEAD_DIM, SSM_D_STATE
    nc, hg = s // ln, n_heads // n_groups
    gw = hg * p
    e64, e64t = _head_maps(di, p)

    def body(x_ref, dt_ref, alog_ref, d_ref, e_ref, et_ref, prev_ref, dy_ref,
             dx_ref, ddt_ref, da_ref, dd_ref, dh, yo_ref, w_ref):
        step = pl.program_id(0)

        @pl.when(step == 0)
        def _():
            dh[...] = jnp.zeros(dh.shape, F32)
            da_ref[...] = jnp.zeros(da_ref.shape, F32)
            dd_ref[...] = jnp.zeros(dd_ref.shape, F32)

        dt = dt_ref[...]
        a, acs, acs_t, tril = _ssd_common(dt, alog_ref[...], n_heads)
        dte, ee, dse, dske, texp = _ssd_wide(dt, acs, acs_t, d_ref[...], e_ref, et_ref)
        row = lax.broadcasted_iota(jnp.int32, (ln, ln), 0)
        col = lax.broadcasted_iota(jnp.int32, (ln, ln), 1)
        triu = col >= row
        x = x_ref[:, :di]
        dy = dy_ref[...]
        xdt = x * dte
        xdtb = xdt.astype(BF16)
        xdsb = (xdt * dse).astype(BF16)
        dyb = dy.astype(BF16)
        dyob = (dy * ee).astype(BF16)
        dhn = dh[...]
        dhb = dhn.astype(BF16)
        per_head = functools.reduce(jnp.add, [_dot(e_ref[...], piece) for piece in _split(dhn * prev_ref[0], 2)])
        ones8 = jnp.ones((8, LANES), BF16)
        dtt = functools.reduce(jnp.add, [_dot(ones8, piece, _NT) for piece in _split(per_head, 2)])[0:1]
        dacs_c = jnp.zeros((ln, LANES), F32)
        dacs_r = jnp.zeros((LANES, ln), F32)
        for g in range(n_groups):
            rows = slice(g * gw, (g + 1) * gw)
            bg = x_ref[:, di + g * ns:di + (g + 1) * ns].astype(BF16)
            cg = x_ref[:, di + (n_groups + g) * ns:di + (n_groups + g + 1) * ns].astype(BF16)
            gmt = _dot(bg, cg, _NT)
            prevb = prev_ref[0, rows, :].astype(BF16)
            dcg = _dot(dyob[:, rows], prevb)
            dh[rows, :] = texp[rows, :] * dhn[rows, :] + _dot(dyob[:, rows], cg, _TN)
            w = _dot(bg, dhb[rows, :], _NT)
            dbg = _dot(xdsb[:, rows], dhb[rows, :])
            yo_ref[:, rows] = _dot(cg, prevb, _NT)
            w_ref[:, rows] = w
            dgmt = jnp.zeros((ln, ln), F32)
            q_hi, q_lo = [], []
            for j in range(hg):
                h = g * hg + j
                segt = acs_t[h:h + 1, :] - acs[:, h:h + 1]
                dect = jnp.where(triu, jnp.exp(jnp.where(triu, segt, 0.0)), 0.0)
                dyh, xh = dyb[:, h * p:(h + 1) * p], xdtb[:, h * p:(h + 1) * p]
                mt = gmt * dect
                dmt = _dot(xh, dyh, _NT)
                dx_ref[:, h * p:(h + 1) * p] = _dot(mt.astype(BF16), dyh)
                dgmt = dgmt + dmt * dect
                hi, lo = _split(dmt * mt, 2)
                q_hi.append(hi)
                q_lo.append(lo)
            sel_c = (lax.broadcasted_iota(jnp.int32, (hg * ln, LANES), 1)
                     == g * hg + lax.broadcasted_iota(jnp.int32, (hg * ln, LANES), 0) // ln).astype(BF16)
            sel_r = (lax.broadcasted_iota(jnp.int32, (LANES, hg * ln), 0)
                     == g * hg + lax.broadcasted_iota(jnp.int32, (LANES, hg * ln), 1) // ln).astype(BF16)
            for pieces in (q_hi, q_lo):
                dacs_c = dacs_c - _dot(jnp.concatenate(pieces, axis=1), sel_c)
                dacs_r = dacs_r + _dot(sel_r, jnp.concatenate(pieces, axis=0))
            dgb = dgmt.astype(BF16)
            dx_ref[:, di + g * ns:di + (g + 1) * ns] = dbg + _dot(dgb, cg)
            dx_ref[:, di + (n_groups + g) * ns:di + (n_groups + g + 1) * ns] = dcg + _dot(dgb, bg, _TN)

        wds = w_ref[...] * dse
        dxdt = dx_ref[:, :di] + wds
        red = _spread(jnp.concatenate([dxdt * x, dy * yo_ref[...] * ee, xdt * wds, dy * x], axis=0), et_ref[...])
        ddt_x, r_off, r_state, ddr = red[:ln], red[ln:2 * ln], red[2 * ln:3 * ln], red[3 * ln:]
        dx_ref[:, :di] = dxdt * dte + dy * dske
        rowi = lax.broadcasted_iota(jnp.int32, (ln, LANES), 0)
        dlast = jnp.sum(r_state, axis=0, keepdims=True) + dtt * jnp.exp(acs[ln - 1:ln, :])
        dacs = r_off - r_state + dacs_c + dacs_r.T + jnp.where(rowi == ln - 1, dlast, 0.0)
        dadt = _dot(triu.astype(F32), dacs, precision=HIGHEST)
        ddt_ref[...] = dadt * a + ddt_x
        da_ref[...] += jnp.sum(dadt * dt, axis=0, keepdims=True)
        dd_ref[...] += jnp.sum(ddr, axis=0, keepdims=True)

        @pl.when(step == nc - 1)
        def _():
            da_ref[...] = da_ref[...] * a

    vec = pl.BlockSpec((1, LANES), lambda c: (0, 0))
    rev = lambda c: (nc - 1 - c, 0)
    return pl.pallas_call(
        body, name=name, grid=(nc,),
        in_specs=[pl.BlockSpec((ln, convd), rev), pl.BlockSpec((ln, LANES), rev), vec, vec,
                  pl.BlockSpec(e64.shape, lambda c: (0, 0)), pl.BlockSpec(e64t.shape, lambda c: (0, 0)),
                  pl.BlockSpec((1, di, ns), lambda c: (nc - 1 - c, 0, 0)), pl.BlockSpec((ln, di), rev)],
        out_specs=[pl.BlockSpec((ln, convd), rev), pl.BlockSpec((ln, LANES), rev), vec, vec],
        out_shape=[jax.ShapeDtypeStruct((s, convd), F32), jax.ShapeDtypeStruct((s, LANES), F32),
                   jax.ShapeDtypeStruct((1, LANES), F32), jax.ShapeDtypeStruct((1, LANES), F32)],
        scratch_shapes=[pltpu.VMEM((di, ns), F32), pltpu.VMEM((ln, di), F32), pltpu.VMEM((ln, di), F32)],
        compiler_params=_params(("arbitrary",)))(xbc, dt, alog, dskip, e64, e64t, prev_all, dy)


def _perm(a, d):
    if d == 1:
        return a
    s = a.shape[0]
    return a.reshape(s // d, d, -1).transpose(1, 0, 2).reshape(s, -1)


def _unperm(a, d):
    if d == 1:
        return a
    s = a.shape[0]
    return a.reshape(d, s // d, -1).transpose(1, 0, 2).reshape(s, -1)


def _rot_tables(s, d):
    half = ROPE_DIM // 2
    inv_freq = jnp.power(jnp.float32(ROPE_THETA), -jnp.arange(0, ROPE_DIM, 2, dtype=F32) / ROPE_DIM)
    v = jnp.arange(s, dtype=jnp.int32)
    pos = (v % (s // d)) * d + v // (s // d)
    ang = pos.astype(F32)[:, None] * inv_freq[None, :]
    cos, sin = jnp.cos(ang), jnp.sin(ang)
    zero = jnp.zeros((s, ATT_HEAD_DIM - ROPE_DIM), F32)
    cf = jnp.concatenate([cos, cos, jnp.ones_like(zero)], axis=1)
    s1 = jnp.concatenate([-sin, jnp.zeros_like(sin), zero], axis=1)
    s2 = jnp.concatenate([jnp.zeros_like(sin), sin, zero], axis=1)
    assert half * 2 == ROPE_DIM
    return cf, s1, s2


def _rot(x, tabs, sign):
    cf, s1, s2 = tabs
    half = ROPE_DIM // 2
    left = pltpu.roll(x, ATT_HEAD_DIM - half, 1)
    right = pltpu.roll(x, half, 1)
    return x * cf + sign * (left * s1 + right * s2)


def _att_masks(n, n_blk, rep):
    b = ATT_BLOCK
    row = lax.broadcasted_iota(jnp.int32, (rep * b, b), 0) & (b - 1)
    col = lax.broadcasted_iota(jnp.int32, (rep * b, b), 1)
    off = jnp.where(n % n_blk != 0, 0, 2 * b)
    return col <= row, col >= row + off


def _stack(x, rep):
    return jnp.concatenate([x[:, j * ATT_HEAD_DIM:(j + 1) * ATT_HEAD_DIM] for j in range(rep)], axis=0)


def _att_specs(nb, rep, cur, prv):
    b, hd = ATT_BLOCK, ATT_HEAD_DIM
    q_spec = pl.BlockSpec((b, rep * hd), lambda h, n: (cur(n), h))
    kc_spec = pl.BlockSpec((b, hd), lambda h, n: (cur(n), h))
    kp_spec = pl.BlockSpec((b, hd), lambda h, n: (prv(n), h))
    tc_spec = pl.BlockSpec((b, hd), lambda h, n: (cur(n), 0))
    tp_spec = pl.BlockSpec((b, hd), lambda h, n: (prv(n), 0))
    return q_spec, kc_spec, kp_spec, tc_spec, tp_spec


def _attn_fwd(q, k, v, tabs, n_blk, name):
    s = q.shape[0]
    b, hd = ATT_BLOCK, ATT_HEAD_DIM
    nb = s // b
    n_kv = ATT_KV_HEADS_PER_GROUP
    rep = ATT_HEADS_PER_GROUP // n_kv
    scale = hd ** -0.5

    def body(q_ref, kc_ref, kp_ref, vc_ref, vp_ref, cfc, s1c, s2c, cfp, s1p, s2p, o_ref, lse_ref):
        n = pl.program_id(1)
        tc = (cfc[...], s1c[...], s2c[...])
        tp = (cfp[...], s1p[...], s2p[...])
        qv = q_ref[...]
        q4 = jnp.concatenate([_rot(qv[:, j * hd:(j + 1) * hd], tc, 1.0) for j in range(rep)], axis=0).astype(BF16)
        kc = _rot(kc_ref[...], tc, 1.0).astype(BF16)
        kp = _rot(kp_ref[...], tp, 1.0).astype(BF16)
        mc, mp = _att_masks(n, n_blk, rep)
        sc = jnp.where(mc, _dot(q4, kc, _NT) * scale, NEG)
        sp = jnp.where(mp, _dot(q4, kp, _NT) * scale, NEG)
        m = jnp.maximum(jnp.max(sc, axis=1, keepdims=True), jnp.max(sp, axis=1, keepdims=True))
        pc, pp = jnp.exp(sc - m), jnp.exp(sp - m)
        l = jnp.sum(pc, axis=1, keepdims=True) + jnp.sum(pp, axis=1, keepdims=True)
        o = (_dot(pc.astype(BF16), vc_ref[...].astype(BF16)) + _dot(pp.astype(BF16), vp_ref[...].astype(BF16))) / l
        lse = jnp.broadcast_to(m + jnp.log(l), (rep * b, hd))
        for j in range(rep):
            o_ref[:, j * hd:(j + 1) * hd] = o[j * b:(j + 1) * b]
            lse_ref[:, j * hd:(j + 1) * hd] = lse[j * b:(j + 1) * b]

    cur = lambda n: n
    prv = lambda n: jnp.maximum(n - 1, 0)
    q_spec, kc_spec, kp_spec, tc_spec, tp_spec = _att_specs(nb, rep, cur, prv)
    return pl.pallas_call(
        body, name=name, grid=(n_kv, nb),
        in_specs=[q_spec, kc_spec, kp_spec, kc_spec, kp_spec, tc_spec, tc_spec, tc_spec, tp_spec, tp_spec, tp_spec],
        out_specs=[q_spec, q_spec],
        out_shape=[jax.ShapeDtypeStruct(q.shape, F32), jax.ShapeDtypeStruct(q.shape, F32)],
        compiler_params=_params(("parallel", "arbitrary")))(q, k, k, v, v, *tabs, *tabs)


def _attn_bwd(q, k, v, do, lse, delta, tabs, n_blk, name):
    s = q.shape[0]
    b, hd = ATT_BLOCK, ATT_HEAD_DIM
    nb = s // b
    n_kv = ATT_KV_HEADS_PER_GROUP
    rep = ATT_HEADS_PER_GROUP // n_kv
    scale = hd ** -0.5

    def body(q_ref, do_ref, lse_ref, dl_ref, kc_ref, kp_ref, vc_ref, vp_ref, cfc, s1c, s2c, cfp, s1p, s2p,
             dq_ref, dk_ref, dv_ref, ck, cv):
        n = pl.program_id(1)
        tp = (cfp[...], s1p[...], s2p[...])

        @pl.when(n == 0)
        def _():
            ck[...] = jnp.zeros(ck.shape, F32)
            cv[...] = jnp.zeros(cv.shape, F32)

        @pl.when(n < nb)
        def _():
            tc = (cfc[...], s1c[...], s2c[...])
            qv = q_ref[...]
            q4 = jnp.concatenate([_rot(qv[:, j * hd:(j + 1) * hd], tc, 1.0) for j in range(rep)],
                                 axis=0).astype(BF16)
            do4 = _stack(do_ref[...], rep).astype(BF16)
            lse4 = _stack(lse_ref[...], rep)
            dl4 = _stack(dl_ref[...], rep)
            kc = _rot(kc_ref[...], tc, 1.0).astype(BF16)
            kp = _rot(kp_ref[...], tp, 1.0).astype(BF16)
            vc, vp = vc_ref[...].astype(BF16), vp_ref[...].astype(BF16)
            mc, mp = _att_masks(n, n_blk, rep)
            pc = jnp.where(mc, jnp.exp(jnp.where(mc, _dot(q4, kc, _NT) * scale - lse4, 0.0)), 0.0)
            pp = jnp.where(mp, jnp.exp(jnp.where(mp, _dot(q4, kp, _NT) * scale - lse4, 0.0)), 0.0)
            dsc = (pc * (_dot(do4, vc, _NT) - dl4)).astype(BF16)
            dsp = (pp * (_dot(do4, vp, _NT) - dl4)).astype(BF16)
            dq4 = (_dot(dsc, kc) + _dot(dsp, kp)) * scale
            for j in range(rep):
                dq_ref[:, j * hd:(j + 1) * hd] = _rot(dq4[j * b:(j + 1) * b], tc, -1.0).astype(dq_ref.dtype)
            dk_prev = ck[...] + _dot(dsp, q4, _TN) * scale
            dv_prev = cv[...] + _dot(pp.astype(BF16), do4, _TN)
            dk_ref[...] = _rot(dk_prev, tp, -1.0).astype(dk_ref.dtype)
            dv_ref[...] = dv_prev.astype(dv_ref.dtype)
            ck[...] = _dot(dsc, q4, _TN) * scale
            cv[...] = _dot(pc.astype(BF16), do4, _TN)

        @pl.when(n == nb)
        def _():
            dk_ref[...] = _rot(ck[...], tp, -1.0).astype(dk_ref.dtype)
            dv_ref[...] = cv[...].astype(dv_ref.dtype)

    cur = lambda n: jnp.minimum(n, nb - 1)
    prv = lambda n: jnp.maximum(n - 1, 0)
    q_spec, kc_spec, kp_spec, tc_spec, tp_spec = _att_specs(nb, rep, cur, prv)
    return pl.pallas_call(
        body, name=name, grid=(n_kv, nb + 1),
        in_specs=[q_spec, q_spec, q_spec, q_spec, kc_spec, kp_spec, kc_spec, kp_spec,
                  tc_spec, tc_spec, tc_spec, tp_spec, tp_spec, tp_spec],
        out_specs=[q_spec, kp_spec, kp_spec],
        out_shape=[jax.ShapeDtypeStruct(q.shape, BF16), jax.ShapeDtypeStruct(k.shape, BF16),
                   jax.ShapeDtypeStruct(k.shape, BF16)],
        scratch_shapes=[pltpu.VMEM((b, hd), F32), pltpu.VMEM((b, hd), F32)],
        compiler_params=_params(("parallel", "arbitrary")))(q, do, lse, delta, k, k, v, v, *tabs, *tabs)


def _rot_heads(x, tabs, width, sign, out_dtype, name):
    s = x.shape[0]
    hd = ATT_HEAD_DIM
    tile = min(512, s)

    def body(x_ref, cf, s1, s2, o_ref):
        t = (cf[...], s1[...], s2[...])
        for j in range(width // hd):
            o_ref[:, j * hd:(j + 1) * hd] = _rot(x_ref[:, j * hd:(j + 1) * hd], t, sign).astype(o_ref.dtype)

    tab = pl.BlockSpec((tile, hd), lambda i: (i, 0))
    return pl.pallas_call(
        body, name=name, grid=(s // tile,), in_specs=[pl.BlockSpec((tile, width), lambda i: (i, 0)), tab, tab, tab],
        out_specs=pl.BlockSpec((tile, width), lambda i: (i, 0)), out_shape=jax.ShapeDtypeStruct((s, width), out_dtype),
        compiler_params=_params(("parallel",)))(x, *tabs)


def _kv_grad(dk_rot, dv, tabs, name):
    s, width = dk_rot.shape
    hd = ATT_HEAD_DIM
    tile = min(512, s)

    def body(k_ref, v_ref, cf, s1, s2, o_ref):
        t = (cf[...], s1[...], s2[...])
        for j in range(width // hd):
            o_ref[:, j * hd:(j + 1) * hd] = _rot(k_ref[:, j * hd:(j + 1) * hd], t, -1.0).astype(o_ref.dtype)
        o_ref[:, width:] = v_ref[...].astype(o_ref.dtype)

    tab = pl.BlockSpec((tile, hd), lambda i: (i, 0))
    half = pl.BlockSpec((tile, width), lambda i: (i, 0))
    return pl.pallas_call(
        body, name=name, grid=(s // tile,), in_specs=[half, half, tab, tab, tab],
        out_specs=pl.BlockSpec((tile, 2 * width), lambda i: (i, 0)),
        out_shape=jax.ShapeDtypeStruct((s, 2 * width), BF16), compiler_params=_params(("parallel",)))(dk_rot, dv, *tabs)


def _rows_of(r, dil):
    return pl.ds(r, ATT_BLOCK, stride=dil) if dil > 1 else slice(None)


def _nat_specs(g, dil, n_kv_all, cur, prv):
    b, hd = ATT_BLOCK * dil, ATT_HEAD_DIM
    n_kv = ATT_KV_HEADS_PER_GROUP
    rep = ATT_HEADS_PER_GROUP // n_kv
    q_all = [pl.BlockSpec((b, hd), lambda h, n, j=j: (cur(n), (g * n_kv + h) * rep + j)) for j in range(rep)]
    q_own = [pl.BlockSpec((b, hd), lambda h, n, j=j: (cur(n), h * rep + j)) for j in range(rep)]
    hm_all = pl.BlockSpec((rep, b, hd), lambda h, n: (g * n_kv + h, cur(n), 0))
    hm_own = pl.BlockSpec((rep, b, hd), lambda h, n: (h, cur(n), 0))
    kc =pl.BlockSpec((b, hd), lambda h, n: (cur(n), g * n_kv + h))
    kp = pl.BlockSpec((b, hd), lambda h, n: (prv(n), g * n_kv + h))
    vc = pl.BlockSpec((b, hd), lambda h, n: (cur(n), n_kv_all + g * n_kv + h))
    vp = pl.BlockSpec((b, hd), lambda h, n: (prv(n), n_kv_all + g * n_kv + h))
    tab = pl.BlockSpec((b, hd), lambda h, n: (cur(n), 0))
    stat = pl.BlockSpec((None, b, LANES), lambda h, n: (h, cur(n), 0))
    return q_all, q_own, hm_all, hm_own, kc, kp, vc, vp, tab, stat


def _head_cols(stat, rep):
    return jnp.concatenate([jnp.broadcast_to(stat[:, j:j + 1], stat.shape) for j in range(rep)], axis=0)


def _attn_fwd_nat(q_all, k_rot, kv, tabs, g, dil, name):
    s = q_all.shape[0]
    b, hd = ATT_BLOCK, ATT_HEAD_DIM
    nbn = s // (b * dil)
    n_kv = ATT_KV_HEADS_PER_GROUP
    rep = ATT_HEADS_PER_GROUP // n_kv
    n_kv_all = k_rot.shape[1] // hd
    scale = hd ** -0.5

    def body(*refs):
        q_refs = refs[:rep]
        kc_ref, kp_ref, vc_ref, vp_ref, cf, s1, s2, o_ref, lse_ref = refs[rep:]
        mc, mp = _att_masks(jnp.where(pl.program_id(1) > 0, 1, 0), 2, rep)
        for r in range(dil):
            sl = _rows_of(r, dil)
            tc = (cf[sl, :], s1[sl, :], s2[sl, :])
            q4 = jnp.concatenate([_rot(q_ref[sl, :], tc, 1.0) for q_ref in q_refs], axis=0).astype(BF16)
            kc, kp = kc_ref[sl, :].astype(BF16), kp_ref[sl, :].astype(BF16)
            sc = jnp.where(mc, _dot(q4, kc, _NT) * scale, NEG)
            sp = jnp.where(mp, _dot(q4, kp, _NT) * scale, NEG)
            m = jnp.maximum(jnp.max(sc, axis=1, keepdims=True), jnp.max(sp, axis=1, keepdims=True))
            pc, pp = jnp.exp(sc - m), jnp.exp(sp - m)
            l = jnp.sum(pc, axis=1, keepdims=True) + jnp.sum(pp, axis=1, keepdims=True)
            o = (_dot(pc.astype(BF16), vc_ref[sl, :].astype(BF16))
                 + _dot(pp.astype(BF16), vp_ref[sl, :].astype(BF16))) / l
            lse = m + jnp.log(l)
            for j in range(rep):
                o_ref[j, sl, :] = o[j * b:(j + 1) * b]
            lse_ref[sl, :] = _lane_place([lse[j * b:(j + 1) * b] for j in range(rep)])

    cur = lambda n: n
    prv = lambda n: jnp.maximum(n - 1, 0)
    q_specs, _, _, hm_own, kc, kp, vc, vp, tab, stat = _nat_specs(g, dil, n_kv_all, cur, prv)
    return pl.pallas_call(
        body, name=name, grid=(n_kv, nbn), in_specs=[*q_specs, kc, kp, vc, vp, tab, tab, tab], out_specs=[hm_own, stat],
        out_shape=[jax.ShapeDtypeStruct((ATT_HEADS_PER_GROUP, s, hd), F32), jax.ShapeDtypeStruct((n_kv, s, LANES), F32)],
        compiler_params=_params(("parallel", "arbitrary")))(*([q_all] * rep), k_rot, k_rot, kv, kv, *tabs)


def _attn_bwd_nat(q_all, k_rot, kv, do, lse, delta, tabs, grads, g, dil, name):
    s = q_all.shape[0]
    b, hd = ATT_BLOCK, ATT_HEAD_DIM
    nbn = s // (b * dil)
    n_kv = ATT_KV_HEADS_PER_GROUP
    rep = ATT_HEADS_PER_GROUP // n_kv
    n_kv_all = k_rot.shape[1] // hd
    scale = hd ** -0.5

    def body(*refs):
        q_refs, do_refs = refs[:rep], refs[rep:2 * rep]
        (lse_ref, dl_ref, kc_ref, kp_ref, vc_ref, vp_ref, cf, s1, s2, _, _, _,
         dq_ref, dk_ref, dv_ref, ck, cv) = refs[2 * rep:]
        n = pl.program_id(1)

        @pl.when(n == 0)
        def _():
            ck[...] = jnp.zeros(ck.shape, F32)
            cv[...] = jnp.zeros(cv.shape, F32)

        @pl.when(n < nbn)
        def _():
            mc, mp = _att_masks(jnp.where(n > 0, 1, 0), 2, rep)
            for r in range(dil):
                sl = _rows_of(r, dil)
                own = slice(r * b, (r + 1) * b)
                tc = (cf[sl, :], s1[sl, :], s2[sl, :])
                q4 = jnp.concatenate([_rot(q_ref[sl, :], tc, 1.0) for q_ref in q_refs], axis=0).astype(BF16)
                do4 = jnp.concatenate([do_ref[sl, :] for do_ref in do_refs], axis=0).astype(BF16)
                lse4 = _head_cols(lse_ref[sl, :], rep)
                dl4 = _head_cols(dl_ref[sl, :], rep)
                kc, kp = kc_ref[sl, :].astype(BF16), kp_ref[sl, :].astype(BF16)
                vc, vp = vc_ref[sl, :].astype(BF16), vp_ref[sl, :].astype(BF16)
                pc = jnp.where(mc, jnp.exp(_dot(q4, kc, _NT) * scale - lse4), 0.0)
                pp = jnp.where(mp, jnp.exp(_dot(q4, kp, _NT) * scale - lse4), 0.0)
                dsc = (pc * (_dot(do4, vc, _NT) - dl4)).astype(BF16)
                dsp = (pp * (_dot(do4, vp, _NT) - dl4)).astype(BF16)
                dq4 = (_dot(dsc, kc) + _dot(dsp, kp)) * scale
                for j in range(rep):
                    dq_ref[j, sl, :] = _rot(dq4[j * b:(j + 1) * b], tc, -1.0)
                dk_ref[sl, :] = ck[own, :] + _dot(dsp, q4, _TN) * scale
                dv_ref[sl, :] = cv[own, :] + _dot(pp.astype(BF16), do4, _TN)
                ck[own, :] = _dot(dsc, q4, _TN) * scale
                cv[own, :] = _dot(pc.astype(BF16), do4, _TN)

        @pl.when(n == nbn)
        def _():
            for r in range(dil):
                sl = _rows_of(r, dil)
                dk_ref[sl, :] = ck[r * b:(r + 1) * b, :]
                dv_ref[sl, :] = cv[r * b:(r + 1) * b, :]

    cur = lambda n: jnp.minimum(n, nbn - 1)
    prv = lambda n: jnp.maximum(n - 1, 0)
    q_specs, do_specs, hm_all, _, kc, kp, vc, vp, tab, stat = _nat_specs(g, dil, n_kv_all, cur, prv)
    anyspace = pl.BlockSpec(memory_space=pl.ANY)
    n_in = 2 * rep + 9
    return pl.pallas_call(
        body, name=name, grid=(n_kv, nbn + 1),
        in_specs=[*q_specs, *do_specs, stat, stat, kc, kp, vc, vp, tab, tab, tab, anyspace, anyspace, anyspace],
        out_specs=[hm_all, kp, kp], out_shape=[jax.ShapeDtypeStruct(a.shape, a.dtype) for a in grads],
        input_output_aliases={n_in: 0, n_in + 1: 1, n_in + 2: 2},
        scratch_shapes=[pltpu.VMEM((dil * b, hd), F32), pltpu.VMEM((dil * b, hd), F32)],
        compiler_params=_params(("parallel", "arbitrary"), VMEM_LIMIT_ATTN_BWD_BYTES))(
            *([q_all] * rep), *([do] * rep), lse, delta, k_rot, k_rot, kv, kv, *tabs, *grads)


def _adamw(g_slabs, w, m, v, name, row0=0, prior=None):
    kk, r, c = g_slabs.shape
    tile = r if r <= 256 else _pick_rows(r, 256)
    off = row0 // tile
    assert off * tile == row0

    def body(g_ref, w_ref, m_ref, v_ref, *rest):
        go_ref, d_ref, mo_ref, vo_ref = rest[-4:]
        g = g_ref[0].astype(F32)
        for k in range(1, kk):
            g = g + g_ref[k].astype(F32)
        m2 = ADAM_B1 * m_ref[...] + (1.0 - ADAM_B1) * g
        v2 = ADAM_B2 * v_ref[...] + (1.0 - ADAM_B2) * jnp.square(g)
        m_hat = m2 / (1.0 - ADAM_B1 ** ADAM_STEP)
        v_hat = v2 / (1.0 - ADAM_B2 ** ADAM_STEP)
        go_ref[...] = g
        d_ref[...] = -ADAM_LR * (m_hat / (jnp.sqrt(v_hat) + ADAM_EPS) + ADAM_WD * w_ref[...])
        mo_ref[...] = m2
        vo_ref[...] = v2

    spec = pl.BlockSpec((tile, c), lambda i: (i + off, 0))
    prior = list(prior) if prior is not None else []
    return pl.pallas_call(
        body, name=name, grid=(r // tile,),
        in_specs=[pl.BlockSpec((kk, tile, c), lambda i: (0, i, 0)), spec, spec, spec]
        + [pl.BlockSpec(memory_space=pl.ANY)] * len(prior),
        out_specs=[spec] * 4, out_shape=[jax.ShapeDtypeStruct(w.shape, F32)] * 4,
        input_output_aliases={4 + i: i for i in range(len(prior))},
        compiler_params=_params(("parallel",)))(g_slabs, w, m, v, *prior)


def _pick_rows(r, pref):
    t = (pref // 16) * 16
    while t >= 16:
        if r % t == 0:
            return t
        t -= 16
    return r


def _coords():
    return lax.axis_index("x"), lax.axis_index("y"), lax.axis_index("c")


def _dev_index(px, py, pc):
    return 4 * px + 2 * py + pc


def _all_gather(shards, name):
    na = len(shards)

    def body(*refs):
        ins, outs = refs[:na], refs[na:2 * na]
        send_sems, recv_sems, local_sems = refs[2 * na:]
        x, y, c = _coords()
        me, sibling = (x, y, c), (x, y, 1 - c)
        chips = [(1 - x, y), (x, 1 - y), (1 - x, 1 - y)]

        def copy(a, k, block, to, src=None):
            dst = outs[a].at[_dev_index(*block)]
            return pltpu.make_async_remote_copy(
                src_ref=dst if src is None else src, dst_ref=dst, send_sem=send_sems.at[a * 7 + k],
                recv_sem=recv_sems.at[a * 7 + k], device_id=to, device_id_type=MESH)

        mine = [pltpu.make_async_copy(ins[a], outs[a].at[_dev_index(*me)], local_sems.at[a]) for a in range(na)]
        for cp in mine:
            cp.start()
        first = []
        for a in range(na):
            first.append(copy(a, 0, me, sibling, src=ins[a]))
            first += [copy(a, 1 + j, me, (*chip, c), src=ins[a]) for j, chip in enumerate(chips)]
        for cp in first:
            cp.start()
        passed = []
        for j, chip in enumerate(chips):
            for a in range(na):
                copy(a, 1 + j, (*chip, c), me).wait_recv()
                cp = copy(a, 4 + j, (*chip, c), sibling)
                cp.start()
                passed.append(cp)
        for a in range(na):
            copy(a, 0, sibling, me).wait_recv()
            for j, chip in enumerate(chips):
                copy(a, 4 + j, (*chip, 1 - c), me).wait_recv()
        for cp in first + passed:
            cp.wait_send()
        for cp in mine:
            cp.wait()

    hbm = pl.BlockSpec(memory_space=pl.ANY)
    return pl.pallas_call(
        body, name=name, in_specs=[hbm] * na, out_specs=[hbm] * na,
        out_shape=[jax.ShapeDtypeStruct((NDEV,) + s.shape, s.dtype) for s in shards],
        scratch_shapes=[pltpu.SemaphoreType.DMA((7 * na,)), pltpu.SemaphoreType.DMA((7 * na,)),
                        pltpu.SemaphoreType.DMA((na,))])(*shards)


def _exchange(slabs, whole, name, after=()):
    ns, nw = len(slabs), len(whole)
    na = ns + nw
    nb = len(after)

    def body(*refs):
        ins, outs = refs[:na], refs[na + nb:2 * na + nb]
        send_sems, recv_sems, local_sems = refs[2 * na + nb:]
        x, y, c = _coords()
        me = _dev_index(x, y, c)

        def src_of(a, p):
            return ins[a].at[p] if a < ns else ins[a]

        def copy(a, k, peer):
            p = _dev_index(*peer)
            return pltpu.make_async_remote_copy(
                src_ref=src_of(a, p), dst_ref=outs[a].at[me], send_sem=send_sems.at[a * 7 + k - 1],
                recv_sem=recv_sems.at[a * 7 + k - 1], device_id=peer, device_id_type=MESH)

        def arrival(a, k, peer):
            p = _dev_index(*peer)
            return pltpu.make_async_remote_copy(
                src_ref=src_of(a, p), dst_ref=outs[a].at[p], send_sem=send_sems.at[a * 7 + k - 1],
                recv_sem=recv_sems.at[a * 7 + k - 1], device_id=peer, device_id_type=MESH)

        mine = [pltpu.make_async_copy(src_of(a, me), outs[a].at[me], local_sems.at[a]) for a in range(na)]
        for cp in mine:
            cp.start()
        peers = [(k, (x ^ (k >> 2), y ^ ((k >> 1) & 1), c ^ (k & 1))) for k in range(1, NDEV)]
        sent = [copy(a, k, peer) for k, peer in peers for a in range(na)]
        for cp in sent:
            cp.start()
        for k, peer in peers:
            for a in range(na):
                arrival(a, k, peer).wait_recv()
        for cp in sent:
            cp.wait_send()
        for cp in mine:
            cp.wait()

    hbm = pl.BlockSpec(memory_space=pl.ANY)
    out_shape = [jax.ShapeDtypeStruct(s.shape, s.dtype) for s in slabs]
    out_shape += [jax.ShapeDtypeStruct((NDEV,) + w.shape, w.dtype) for w in whole]
    return pl.pallas_call(
        body, name=name, in_specs=[hbm] * (na + nb), out_specs=[hbm] * na, out_shape=out_shape,
        scratch_shapes=[pltpu.SemaphoreType.DMA((7 * na,)), pltpu.SemaphoreType.DMA((7 * na,)),
                        pltpu.SemaphoreType.DMA((na,))])(*slabs, *whole, *after)


_HBM = pl.BlockSpec(memory_space=pltpu.HBM)
_SEM = pl.BlockSpec(memory_space=pltpu.SEMAPHORE)
_EFFECT = pltpu.SideEffectType.DATAFLOW_SIDE_EFFECTING


def _peers(x, y, c):
    return [(k, (x ^ (k >> 2), y ^ ((k >> 1) & 1), c ^ (k & 1))) for k in range(1, NDEV)]


def _peer_copy(src, land, send_sems, recv_sems, a, k, dst_block, peer):
    return pltpu.make_async_remote_copy(
        src_ref=src, dst_ref=land.at[dst_block], send_sem=send_sems.at[a * 7 + k - 1],
        recv_sem=recv_sems.at[a * 7 + k - 1], device_id=peer, device_id_type=MESH)


def _send_start(arrays, slabs, name):
    na = len(arrays)
    lands = [jax.ShapeDtypeStruct(a.shape if slabs else (NDEV,) + a.shape, a.dtype) for a in arrays]

    def body(*refs):
        ins, zones = refs[:na], refs[na:2 * na]
        send_sems, recv_sems = refs[2 * na], refs[2 * na + 1]
        token = refs[-1]
        x, y, c = _coords()
        me = _dev_index(x, y, c)
        for k, peer in _peers(x, y, c):
            for a in range(na):
                src = ins[a].at[_dev_index(*peer)] if slabs else ins[a]
                _peer_copy(src, zones[a], send_sems, recv_sems, a, k, me, peer).start()
        token[...] = jnp.zeros_like(token)

    outs = pl.pallas_call(
        body, name=name,
        out_shape=(pltpu.SemaphoreType.DMA((7 * na,)), pltpu.SemaphoreType.DMA((7 * na,)),
                   *[pltpu.HBM(a.shape, a.dtype) for a in arrays], *[pltpu.HBM(l.shape, l.dtype) for l in lands],
                   jax.ShapeDtypeStruct((8, LANES), F32)),
        in_specs=[_HBM] * (2 * na), out_specs=(_SEM, _SEM, *([_HBM] * (2 * na)), pl.BlockSpec(memory_space=pltpu.VMEM)),
        input_output_aliases={i: 2 + i for i in range(2 * na)},
        compiler_params=pltpu.CompilerParams(has_side_effects=_EFFECT),
    )(*[pltpu.with_memory_space_constraint(a, pltpu.HBM) for a in arrays],
      *[pltpu.with_memory_space_constraint(lax.empty(l.shape, l.dtype), pltpu.HBM) for l in lands])
    return outs[0], outs[1], list(outs[2:2 + na]), list(outs[2 + na:2 + 2 * na]), outs[-1]


def _send_wait(started, after, slabs, name):
    send_sems, recv_sems, thru, zones, _ = started
    na = len(thru)

    def body(*refs):
        ins, lands = refs[:na], refs[na:2 * na]
        s_sems, r_sems = refs[2 * na], refs[2 * na + 1]
        x, y, c = _coords()
        for k, peer in _peers(x, y, c):
            p = _dev_index(*peer)
            for a in range(na):
                src = ins[a].at[p] if slabs else ins[a]
                cp = _peer_copy(src, lands[a], s_sems, r_sems, a, k, p, peer)
                cp.wait_send()
                cp.wait_recv()

    outs = pl.pallas_call(
        body, name=name, out_shape=tuple(pltpu.HBM(v.shape, v.dtype) for v in thru + zones),
        in_specs=[_HBM] * (2 * na) + [_SEM, _SEM, pl.BlockSpec(memory_space=pl.ANY)], out_specs=tuple([_HBM] * (2 * na)),
        input_output_aliases={i: i for i in range(2 * na)},
        compiler_params=pltpu.CompilerParams(has_side_effects=_EFFECT),
    )(*thru, *zones, send_sems, recv_sems, after)
    me = _dev_index(*_coords())
    filled = []
    for a in range(na):
        own = lax.dynamic_index_in_dim(outs[a], me, 0, keepdims=False) if slabs else outs[a]
        filled.append(lax.dynamic_update_index_in_dim(outs[na + a], own, me, 0))
    return filled


def _pack(vecs):
    parts, spans, off = [], [], 0
    for v in vecs:
        n = v.size
        pad = (-n) % LANES
        parts.append(jnp.pad(v.reshape(-1).astype(F32), (0, pad)))
        spans.append((off, n))
        off += n + pad
    return jnp.concatenate(parts).reshape(-1, LANES), spans


def _pad_lanes(v):
    v = v.reshape(1, -1)
    return jnp.pad(v, ((0, 0), (0, LANES - v.shape[1])))


def _cols_to_slabs(g):
    sh = g.shape
    g = g.reshape(sh[:-1] + (NDEV, sh[-1] // NDEV))
    return jnp.moveaxis(g, -2, 0)


def _rows_to_slabs(g):
    sh = g.shape
    g = g.reshape(sh[:-2] + (NDEV, sh[-2] // NDEV, sh[-1]))
    return jnp.moveaxis(g, -3, 0)


def _slabs_to_cols(a):
    a = jnp.moveaxis(a, 0, -2)
    return a.reshape(a.shape[:-2] + (a.shape[-2] * a.shape[-1],))


def _slabs_to_rows(a):
    a = jnp.moveaxis(a, 0, -3)
    return a.reshape(a.shape[:-3] + (a.shape[-3] * a.shape[-2], a.shape[-1]))


def _ffn_forward2(x, norm_w, w_up, fcw, wdown, tag):
    h = _rms_fwd(x, norm_w, f"{tag}_norm")
    u = _mm(h, w_up, name=f"{tag}_up")
    f = _ffn_gate_fwd2(u, fcw, f"{tag}_gate")
    return _mm(f, wdown, res=x, name=f"{tag}_down"), (h, u, f)


def _ffn_backward2(x, saved, dout, dout_b, norm_w, w_up, fcw, wdown, tag):
    h, u, f = saved
    dwdown = _mm(f, dout_b, ta=True, name=f"{tag}_dwdown")
    df = _mm(dout_b, wdown, tb=True, name=f"{tag}_df")
    du, dfc = _ffn_gate_bwd2(u, fcw, df, f"{tag}_gate_bwd")
    dwup = _mm(h, du, ta=True, name=f"{tag}_dwup")
    dx, dxb, dnorm = _mm(du, w_up, tb=True, rms_bwd=(x, norm_w, dout), tm=RMS_BWD_ROWS, name=f"{tag}_dh")
    return dx, dxb, (dwup, jnp.concatenate([dfc[0], dfc[1]], axis=1), dwdown, dnorm)


def _ffn_forward(x, norm_w, wup_g, wup_v, cw_g, cw_v, wdown, tag):
    h = _rms_fwd(x, norm_w, f"{tag}_norm")
    ug = _mm(h, wup_g, name=f"{tag}_up_gate")
    uv = _mm(h, wup_v, name=f"{tag}_up_val")
    f = _ffn_gate_fwd(ug, uv, cw_g, cw_v, f"{tag}_gate")
    return _mm(f, wdown, res=x, name=f"{tag}_down"), (h, ug, uv, f)


def _ffn_backward(x, saved, dout, dout_b, norm_w, wup_g, wup_v, cw_g, cw_v, wdown, tag):
    h, ug, uv, f = saved
    dwdown = _mm(f, dout_b, ta=True, name=f"{tag}_dwdown")
    df = _mm(dout_b, wdown, tb=True, name=f"{tag}_df")
    dug, duv, dcg, dcv = _ffn_gate_bwd(ug, uv, cw_g, cw_v, df, f"{tag}_gate_bwd")
    dwg = _mm(h, dug, ta=True, name=f"{tag}_dwup_gate")
    dwv = _mm(h, duv, ta=True, name=f"{tag}_dwup_val")
    dh = _mm(dug, wup_g, tb=True, name=f"{tag}_dh_gate")
    dh = _mm(duv, wup_v, tb=True, res=dh, name=f"{tag}_dh_val")
    dx, dxb, dnorm = _rms_bwd(x, norm_w, dh, dout, f"{tag}_norm_bwd")
    return dx, dxb, (jnp.concatenate([dwg, dwv], axis=1), jnp.concatenate([dcg, dcv], axis=1), dwdown, dnorm)


def kernel(x, a_norm, ssm_w_in, ssm_conv_w, ssm_conv_b, ssm_dt_bias, ssm_a_log, ssm_d, ssm_norm, ssm_w_out, kv_norm, w_kv, b_norm, att_w_q, att_w_o, ffn_norm, ffn_w_up, ffn_conv_w, ffn_w_down, final_norm, loss_target, m_a_norm, m_ssm_w_in, m_ssm_conv_w, m_ssm_conv_b, m_ssm_dt_bias, m_ssm_a_log, m_ssm_d, m_ssm_norm, m_ssm_w_out, m_kv_norm, m_w_kv, m_b_norm, m_att_w_q, m_att_w_o, m_ffn_norm, m_ffn_w_up, m_ffn_conv_w, m_ffn_w_down, m_final_norm, v_a_norm, v_ssm_w_in, v_ssm_conv_w, v_ssm_conv_b, v_ssm_dt_bias, v_ssm_a_log, v_ssm_d, v_ssm_norm, v_ssm_w_out, v_kv_norm, v_w_kv, v_b_norm, v_att_w_q, v_att_w_o, v_ffn_norm, v_ffn_w_up, v_ffn_conv_w, v_ffn_w_down, v_final_norm):
    given = dict(locals())
    xs, tgt = x[0], loss_target[0]
    s, d = xs.shape
    di = ssm_w_out.shape[1] * NDEV
    nh = ssm_dt_bias.shape[1]
    ng = SSM_N_GROUPS
    convd = di + 2 * ng * SSM_D_STATE
    f = ffn_w_down.shape[1] * NDEV
    n_att = len(ATT_PATTERNS)
    qg = ATT_HEADS_PER_GROUP * ATT_HEAD_DIM
    kg = ATT_KV_HEADS_PER_GROUP * ATT_HEAD_DIM
    kvd = n_att * kg
    assert all(w // dil == ATT_BLOCK for w, dil in ATT_PATTERNS)

    small, _ = _pack([a_norm, ssm_conv_w, ssm_conv_b, ssm_norm, ffn_conv_w])
    gat = _all_gather([ssm_w_in[0].astype(BF16), small], "gather_weights")
    first = _send_start([ssm_w_out[0].astype(BF16), ffn_w_up[0].astype(BF16), ffn_w_down[0].astype(BF16)], False,
                        "gather_ffn0_start")
    rest = _send_start([b.astype(BF16) for b in (w_kv, att_w_q[0], att_w_o[0], ffn_w_up[1], ffn_w_down[1])], False,
                       "gather_rest_start")
    w_in = _slabs_to_cols(gat[0])
    in_dim = di + convd + nh
    in_pad = di + convd + LANES
    w_in = jnp.pad(w_in, ((0, 0), (0, in_pad - in_dim)))
    sm = gat[1].reshape(NDEV, -1)
    o0 = 0

    def take(shape):
        nonlocal o0
        n = math.prod(shape)
        out = sm[:, o0:o0 + n].reshape((NDEV,) + shape)
        o0 += n + (-n) % LANES
        return out
    a_norm_f = _slabs_to_cols(take(a_norm.shape)) + (first[-1][0, 0] + rest[-1][0, 0])
    conv_w_f = _slabs_to_cols(take(ssm_conv_w.shape))[0]
    conv_b_f = _slabs_to_cols(take(ssm_conv_b.shape))
    ssm_norm_f = _slabs_to_cols(take(ssm_norm.shape))
    fcw = _slabs_to_cols(take(ffn_conv_w.shape))
    dtb, alog, dsk = _pad_lanes(ssm_dt_bias), _pad_lanes(ssm_a_log), _pad_lanes(ssm_d)
    kvn, fin = kv_norm.reshape(1, d), final_norm.reshape(1, d)

    h0 = _rms_fwd(xs, a_norm_f, "a_norm")
    zx = _mm(h0, w_in, name="in_proj")
    z, dtr = (zx, di, 0), (zx, LANES, (di + convd) // LANES)
    xbc = _conv_silu_fwd(zx, di, conv_w_f, conv_b_f, "ssm_conv")
    dt = _softplus_fwd(dtr, dtb, "ssm_dt")
    y, prevs = _ssd_fwd2(xbc, dt, alog, dsk, di, nh, ng, "ssd")
    yn = _gnorm_fwd(y, z, ssm_norm_f, ng, "ssm_gnorm")
    got = _send_wait(first, yn, False, "gather_ffn0_wait")
    w_out = _slabs_to_rows(got[0])
    w_up0, w_down0 = _slabs_to_cols(got[1]), _slabs_to_rows(got[2])
    x1 = _mm(yn, w_out, res=xs, name="ssm_out")
    x2, ffn0 = _ffn_forward2(x1, ffn_norm[0:1], w_up0, fcw[0], w_down0, "ffn0")
    got = _send_wait(rest, x2, False, "gather_rest_wait")
    w_kvf = _slabs_to_cols(got[0])
    w_q = _slabs_to_cols(got[1])
    w_o = _slabs_to_rows(got[2])
    w_up1, w_down1 = _slabs_to_cols(got[3]), _slabs_to_rows(got[4])
    hk = _rms_fwd(x2, kvn, "kv_norm")
    kv = _mm(hk, w_kvf, name="kv_proj")
    h2 = _rms_fwd(x2, b_norm, "b_norm")
    q = _mm(h2, w_q, name="q_proj")
    tabs = _rot_tables(s, 1)
    k_rot = _rot_heads(kv, tabs, kvd, 1.0, F32, "k_rot")
    att = [_attn_fwd_nat(q, k_rot, kv, tabs, g, dil, f"attn{g}") for g, (_, dil) in enumerate(ATT_PATTERNS)]
    o, ob, lse = _merge_heads([t[0] for t in att], [t[1] for t in att], "attn_merge")
    x3 = _mm(ob, w_o, res=x2, name="attn_out")
    x4, ffn1 = _ffn_forward2(x3, ffn_norm[1:2], w_up1, fcw[1], w_down1, "ffn1")
    loss_part, dx4, dx4b, dfin = _final_loss(x4, fin, tgt, "loss_head")

    dx3, dx3b, (dwup1, dfc1, dwdown1, dfn1) = _ffn_backward2(
        x3, ffn1, dx4, dx4b, ffn_norm[1:2], w_up1, fcw[1], w_down1, "ffn1")
    dw_o = _mm(ob, dx3b, ta=True, name="attn_dwo")
    do = _mm(dx3b, w_o, tb=True, name="attn_do")
    delta = _delta_heads(do, o, "attn_delta")
    grads = (lax.empty((n_att * qg // LANES, s, LANES), F32), lax.empty((s, kvd), F32), lax.empty((s, kvd), F32))
    for g, (_, dil) in enumerate(ATT_PATTERNS):
        grads = _attn_bwd_nat(q, k_rot, kv, do, lse, delta, tabs, grads, g, dil, f"attn{g}_bwd")
    dq, dk_rot, dv = grads
    dkv = _kv_grad(dk_rot, dv, tabs, "kv_grad")
    dw_q = _mm(h2, dq, ta=True, b_heads=True, name="q_dw")
    dx2, _, db_norm = _mm(dq, w_q, tb=True, a_heads=True, rms_bwd=(x2, b_norm, dx3), tm=RMS_BWD_ROWS, name="q_dh")
    dw_kv = _mm(hk, dkv, ta=True, name="kv_dw")
    dx2, dx2b, dkv_norm = _mm(dkv, w_kvf, tb=True, rms_bwd=(x2, kvn, dx2), tm=RMS_BWD_ROWS, name="kv_dh")
    sent1 = _send_start([_cols_to_slabs(dwup1).astype(BF16), _rows_to_slabs(dwdown1).astype(BF16),
                         _cols_to_slabs(dw_kv).astype(BF16), _cols_to_slabs(dw_q).astype(BF16),
                         _rows_to_slabs(dw_o).astype(BF16)], True, "grads_late_start")
    dx1, dx1b, (dwup0, dfc0, dwdown0, dfn0) = _ffn_backward2(
        x1, ffn0, dx2, dx2b, ffn_norm[0:1], w_up0, fcw[0] + sent1[-1][0, 0], w_down0, "ffn0")
    dw_out = _mm(yn, dx1b, ta=True, name="ssm_dwout")
    sent0 = _send_start([_cols_to_slabs(dwup0).astype(BF16), _rows_to_slabs(dwdown0).astype(BF16),
                         _rows_to_slabs(dw_out).astype(BF16)], True, "grads_ffn0_start")
    dyn = _mm(dx1b, w_out, tb=True, name="ssm_dyn")
    dzx = lax.empty((s, in_pad), BF16)
    dy, dzx, dssm_norm = _gnorm_bwd(dyn, y, z, ssm_norm_f + sent0[-1][0, 0], ng, (dzx, 0), "ssm_gnorm_bwd")
    dxbc, ddt, dalog, ddsk = _ssd_bwd2(xbc, dt, alog, dsk, prevs, dy, di, nh, ng, "ssd_bwd")
    dzx, ddtb = _softplus_bwd(ddt, dtr, dtb, nh, (dzx, (di + convd) // LANES), "ssm_dt_bwd")
    dzx, dconv_w, dconv_b = _conv_silu_bwd(zx, di, conv_w_f, conv_b_f, dxbc, dzx, "ssm_conv_bwd")
    dw_in = _mm(h0, dzx, ta=True, name="in_dw")
    sent_m = _send_start([_cols_to_slabs(dw_in[:, :in_dim])], True, "grads_mamba_start")
    dx0, _, da_norm = _mm(dzx, w_in, tb=True, rms_bwd=(xs, a_norm_f + sent_m[-1][0, 0], dx1), tm=RMS_BWD_ROWS,
                          name="in_dh")

    small_full = {
        'a_norm': da_norm, 'ssm_conv_w': dconv_w[None], 'ssm_conv_b': dconv_b, 'ssm_dt_bias': ddtb[:, :nh],
        'ssm_a_log': dalog[:, :nh], 'ssm_d': ddsk[:, :nh], 'ssm_norm': dssm_norm, 'kv_norm': dkv_norm.reshape(d),
        'b_norm': db_norm, 'ffn_norm': jnp.concatenate([dfn0, dfn1], axis=0), 'ffn_conv_w': jnp.stack([dfc0, dfc1]),
        'final_norm': dfin.reshape(d),
    }
    small_names = list(small_full)
    packed, spans = _pack([small_full[n] for n in small_names])
    got1 = _send_wait(sent1, dx0, True, "grads_late_wait")
    got0 = _send_wait(sent0, dx0, True, "grads_ffn0_wait")
    recv_big = {'w_kv': [got1[2]], 'att_w_q': [got1[3]], 'att_w_o': [got1[4]], 'ffn_w_up': [got0[0], got1[0]],
                'ffn_w_down': [got0[1], got1[1]]}

    me = _dev_index(*_coords())
    res = {}

    def update_big(n, layers):
        w = given[n]
        c = w.shape[-1]
        outs, row0 = None, 0
        for k, r in enumerate(layers):
            g = r.reshape(NDEV, -1, c)
            outs = _adamw(g, w.reshape(-1, c), given['m_' + n].reshape(-1, c), given['v_' + n].reshape(-1, c),
                          f"adamw_{n}_{k}", row0=row0, prior=outs)
            row0 += g.shape[1]
        res[n] = [o_.reshape(w.shape) for o_ in outs]
    for n, r in recv_big.items():
        update_big(n, r)
    update_big('ssm_w_out', [got0[2]])
    recv = _exchange([], [packed], "exchange_grads", after=[res[n][1] for n in res])
    small_sum = _sum_slabs(recv[-1], "sum_small_grads").reshape(-1)
    gotm = _send_wait(sent_m, recv[-1], True, "grads_mamba_wait")
    update_big('ssm_w_in', [gotm[0]])
    sharded_small = {'a_norm', 'ssm_conv_w', 'ssm_conv_b', 'ssm_norm', 'ffn_conv_w'}
    for n, (off, size) in zip(small_names, spans):
        w = given[n]
        gfull = small_sum[off:off + size].reshape(small_full[n].shape)
        if n in sharded_small:
            c = w.shape[-1]
            gfull = lax.dynamic_slice_in_dim(gfull, me * c, c, axis=gfull.ndim - 1)
        c = w.shape[-1]
        outs = _adamw(gfull.reshape(1, -1, c), w.reshape(-1, c), given['m_' + n].reshape(-1, c),
                      given['v_' + n].reshape(-1, c), f"adamw_{n}")
        res[n] = [o_.reshape(w.shape) for o_ in outs]

    loss = lax.psum(loss_part[0, 0], AXES)
    return (loss, dx0[None], *[res[n][0] for n in WEIGHTS], *[res[n][1] for n in WEIGHTS],
            *[res[n][2] for n in WEIGHTS], *[res[n][3] for n in WEIGHTS])
```

```python
import functools
import math

import jax
import jax.numpy as jnp
from jax import lax
from jax.experimental import pallas as pl
from jax.experimental.pallas import tpu as pltpu

F32, BF16 = jnp.float32, jnp.bfloat16
AXES = ("x", "y", "c")
NDEV = 8
MESH = pl.DeviceIdType.MESH
HIGHEST = lax.Precision.HIGHEST

LANES = 128
SUBLANES = 8
VMEM_LIMIT_BYTES = 48 * 1024 * 1024
VMEM_LIMIT_ATTN_BWD_BYTES = 58 * 1024 * 1024
RMS_BWD_ROWS = 512

RMS_EPS = 1e-6
GATED_NORM_EPS = 1e-5
SSM_HEAD_DIM = 64
SSM_N_GROUPS = 8
SSM_D_STATE = 128
SSM_CONV = 4
SSM_CHUNK = 128
ATT_PATTERNS = ((128, 1), (512, 4), (2048, 16))
ATT_HEAD_DIM = 128
ATT_HEADS_PER_GROUP = 8
ATT_KV_HEADS_PER_GROUP = 2
ATT_BLOCK = 128
ROPE_DIM = ATT_HEAD_DIM // 4
ROPE_THETA = 500000.0
FFN_CONV = 3
ADAM_LR = 0.001
ADAM_B1 = 0.9
ADAM_B2 = 0.999
ADAM_EPS = 1e-08
ADAM_WD = 0.01
ADAM_STEP = 10
NEG = -1e30

WEIGHTS = ['a_norm', 'ssm_w_in', 'ssm_conv_w', 'ssm_conv_b', 'ssm_dt_bias', 'ssm_a_log', 'ssm_d', 'ssm_norm',
           'ssm_w_out', 'kv_norm', 'w_kv', 'b_norm', 'att_w_q', 'att_w_o', 'ffn_norm', 'ffn_w_up', 'ffn_conv_w',
           'ffn_w_down', 'final_norm']


def _params(sem=None, vmem=VMEM_LIMIT_BYTES):
    kw = dict(vmem_limit_bytes=vmem)
    if sem is not None:
        kw["dimension_semantics"] = sem
    return pltpu.CompilerParams(**kw)


def _pick(n, pref):
    if n <= pref:
        return n
    t = (pref // LANES) * LANES
    while t >= LANES:
        if n % t == 0:
            return t
        t -= LANES
    return n


def _dot(a, b, dims=(((1,), (0,)), ((), ())), precision=None):
    return lax.dot_general(a, b, dims, precision=precision, preferred_element_type=F32)


_NT = (((1,), (1,)), ((), ()))
_TN = (((0,), (0,)), ((), ()))


def _mm(a, b, *, ta=False, tb=False, res=None, out_dtype=None, name, tm=1408, tn=1408, tk=2048,
        a_heads=False, b_heads=False, rms_bwd=None, rms_fwd=None):
    assert not (a_heads and ta) and not (b_heads and tb)
    a_parts = a.ndim == 3 and not a_heads
    b_parts = b.ndim == 3 and not b_heads
    assert not (a_parts and ta) and not (b_parts and tb)
    if out_dtype is None:
        out_dtype = BF16 if ta else F32
    if a_heads:
        m, k = a.shape[1], a.shape[0] * LANES
    elif a_parts:
        m, k = a.shape[1], a.shape[0] * a.shape[2]
    else:
        m = a.shape[1] if ta else a.shape[0]
        k = a.shape[0] if ta else a.shape[1]
    if b_heads:
        n, kb = b.shape[0] * LANES, b.shape[1]
    elif b_parts:
        n, kb = b.shape[0] * b.shape[2], b.shape[1]
    else:
        n = b.shape[0] if tb else b.shape[1]
        kb = b.shape[1] if tb else b.shape[0]
    assert k == kb
    tm = _pick(m, tm)
    tn = _pick(b.shape[2], tn) if b_parts else _pick(n, tn)
    tk = _pick(a.shape[2], tk) if a_parts else _pick(k, tk)
    nk = k // tk
    if a_heads:
        a_spec = pl.BlockSpec((tk // LANES, tm, LANES), lambda i, j, l: (l, i, 0))
    elif a_parts:
        per = a.shape[2] // tk
        a_spec = pl.BlockSpec((None, tm, tk), lambda i, j, l: (l // per, i, l % per))
    elif ta:
        a_spec = pl.BlockSpec((tk, tm), lambda i, j, l: (l, i))
    else:
        a_spec = pl.BlockSpec((tm, tk), lambda i, j, l: (i, l))
    if b_heads:
        b_spec = pl.BlockSpec((tn // LANES, tk, LANES), lambda i, j, l: (j, l, 0))
    elif b_parts:
        per_n = b.shape[2] // tn
        b_spec = pl.BlockSpec((None, tk, tn), lambda i, j, l: (j // per_n, l, j % per_n))
    elif tb:
        b_spec = pl.BlockSpec((tn, tk), lambda i, j, l: (j, l))
    else:
        b_spec = pl.BlockSpec((tk, tn), lambda i, j, l: (l, j))
    if rms_fwd is not None:
        assert nk == 1 and a.ndim == 2 and b.ndim == 2 and not (ta or tb or a_heads or b_heads)
        assert res is None and rms_bwd is None

        def nbody(a_ref, b_ref, w_ref, o_ref, h_ref, hs):
            @pl.when(pl.program_id(1) == 0)
            def _():
                xv = a_ref[...]
                rs = lax.rsqrt(jnp.mean(xv * xv, axis=-1, keepdims=True) + RMS_EPS)
                hv = (xv * rs * w_ref[...]).astype(BF16)
                hs[...] = hv
                h_ref[...] = hv

            o_ref[...] = _dot(hs[...], b_ref[...].astype(BF16)).astype(o_ref.dtype)

        rows = pl.BlockSpec((tm, k), lambda i, j: (i, 0))
        return pl.pallas_call(
            nbody, name=name, grid=(m // tm, n // tn),
            in_specs=[rows, pl.BlockSpec((k, tn), lambda i, j: (0, j)), pl.BlockSpec((1, k), lambda i, j: (0, 0))],
            out_specs=[pl.BlockSpec((tm, tn), lambda i, j: (i, j)), rows],
            out_shape=[jax.ShapeDtypeStruct((m, n), out_dtype), jax.ShapeDtypeStruct((m, k), BF16)],
            scratch_shapes=[pltpu.VMEM((tm, k), BF16)],
            compiler_params=_params(("parallel", "arbitrary")))(a, b, rms_fwd)
    o_spec = pl.BlockSpec((tm, tn), lambda i, j, l: (i, j))
    dims = (((0 if ta else 1,), (1 if tb else 0,)), ((), ()))
    has_res = res is not None
    has_rms = rms_bwd is not None
    assert not (has_res and has_rms) and (not has_rms or tn == n)
    n_extra = 3 if has_rms else int(has_res)
    n_out = 3 if has_rms else 1

    def load(ref, heads):
        if not heads:
            return ref[...].astype(BF16)
        return jnp.concatenate([ref[i].astype(BF16) for i in range(ref.shape[0])], axis=1)

    def body(*refs):
        a_ref, b_ref = refs[:2]
        extra = refs[2:2 + n_extra]
        outs = refs[2 + n_extra:2 + n_extra + n_out]
        p = _dot(load(a_ref, a_heads), load(b_ref, b_heads), dims)

        def finish(r):
            if has_res:
                r = r + extra[0][...]
            if not has_rms:
                outs[0][...] = r.astype(outs[0].dtype)
                return
            x_ref, w_ref, dres_ref = extra
            dx_ref, dxb_ref, dw_ref = outs
            xv = x_ref[...]
            rs = lax.rsqrt(jnp.mean(xv * xv, axis=-1, keepdims=True) + RMS_EPS)
            xh = xv * rs
            dxh = r * w_ref[...]
            dx = dres_ref[...] + rs * (dxh - xh * jnp.mean(dxh * xh, axis=-1, keepdims=True))
            dx_ref[...] = dx
            dxb_ref[...] = dx.astype(BF16)

            @pl.when(pl.program_id(0) == 0)
            def _():
                dw_ref[...] = jnp.zeros(dw_ref.shape, F32)

            dw_ref[...] += jnp.sum(r * xh, axis=0, keepdims=True)

        if nk == 1:
            finish(p)
            return
        acc = refs[2 + n_extra + n_out]
        l = pl.program_id(2)

        @pl.when(l == 0)
        def _():
            acc[...] = p

        @pl.when(jnp.logical_and(l > 0, l < nk - 1))
        def _():
            acc[...] += p

        @pl.when(l == nk - 1)
        def _():
            finish(acc[...] + p)

    scratch = [pltpu.VMEM((tm, tn), F32)] if nk > 1 else []
    if has_rms:
        x, w, dres = rms_bwd
        vec = pl.BlockSpec((1, n), lambda i, j, l: (0, 0))
        return pl.pallas_call(
            body, name=name, grid=(m // tm, 1, nk), in_specs=[a_spec, b_spec, o_spec, vec, o_spec],
            out_specs=[o_spec, o_spec, vec], scratch_shapes=scratch,
            out_shape=[jax.ShapeDtypeStruct((m, n), F32), jax.ShapeDtypeStruct((m, n), BF16),
                       jax.ShapeDtypeStruct((1, n), F32)],
            compiler_params=_params(("arbitrary", "arbitrary", "arbitrary")))(a, b, x, w, dres)
    ins = [a, b] + ([res] if has_res else [])
    in_specs = [a_spec, b_spec] + ([o_spec] if has_res else [])
    return pl.pallas_call(
        body, name=name, grid=(m // tm, n // tn, nk), in_specs=in_specs, out_specs=o_spec,
        out_shape=jax.ShapeDtypeStruct((m, n), out_dtype), scratch_shapes=scratch,
        compiler_params=_params(("parallel", "parallel", "arbitrary")))(*ins)


def _rowwise(fn, rows, bcasts, outs, accs=(), *, tile, name):
    s = (rows[0][0] if isinstance(rows[0], tuple) else rows[0]).shape[-2]
    tile = min(tile, s)
    n_val = len(rows) + len(bcasts)
    intos = [(k, o) for k, o in enumerate(outs) if len(o) == 4]
    n_in, n_out, n_acc = n_val + len(intos), len(outs), len(accs)

    def row_spec(c):
        if isinstance(c, tuple):
            return pl.BlockSpec((c[0], tile, c[1]), lambda i: (0, i, 0))
        return pl.BlockSpec((tile, c), lambda i: (i, 0))

    def row_shape(c):
        return (c[0], s, c[1]) if isinstance(c, tuple) else (s, c)

    def window(width, cb):
        return pl.BlockSpec((tile, width), lambda i: (i, cb))

    def body(*refs):
        vals = fn(*[r[...] for r in refs[:n_val]])
        o_refs = refs[n_in:n_in + n_out]
        a_refs = refs[n_in + n_out:]
        for r, v in zip(o_refs, vals[:n_out]):
            if isinstance(v, list):
                for i, vi in enumerate(v):
                    r[i] = vi.astype(r.dtype)
            else:
                r[...] = v.astype(r.dtype)

        @pl.when(pl.program_id(0) == 0)
        def _():
            for r in a_refs:
                r[...] = jnp.zeros(r.shape, r.dtype)

        for r, v in zip(a_refs, vals[n_out:]):
            r[...] += v

    in_specs = [window(r[1], r[2]) if isinstance(r, tuple)
                else row_spec(r.shape[1] if r.ndim == 2 else (r.shape[0], r.shape[2])) for r in rows]
    in_specs += [pl.BlockSpec(b.shape, lambda i: (0, 0)) for b in bcasts]
    in_specs += [pl.BlockSpec(memory_space=pl.ANY) for _ in intos]
    out_specs = [window(o[0], o[3]) if len(o) == 4 else row_spec(o[0]) for o in outs]
    out_specs += [pl.BlockSpec(sh, lambda i: (0, 0)) for sh, _ in accs]
    out_shape = [jax.ShapeDtypeStruct(o[2].shape, o[2].dtype) if len(o) == 4
                 else jax.ShapeDtypeStruct(row_shape(o[0]), o[1]) for o in outs]
    out_shape += [jax.ShapeDtypeStruct(sh, dt) for sh, dt in accs]
    args = [r[0] if isinstance(r, tuple) else r for r in rows] + list(bcasts) + [o[2] for _, o in intos]
    return pl.pallas_call(body, name=name, grid=(s // tile,), in_specs=in_specs, out_specs=out_specs,
                          out_shape=out_shape, input_output_aliases={n_val + i: k for i, (k, _) in enumerate(intos)},
                          compiler_params=_params(("arbitrary",)))(*args)


def _rms_fwd(x, w, name):
    def fn(x, w):
        r = lax.rsqrt(jnp.mean(x * x, axis=-1, keepdims=True) + RMS_EPS)
        return (x * r * w,)
    return _rowwise(fn, [x], [w], [(x.shape[1], BF16)], tile=256, name=name)[0]


def _rms_bwd(x, w, dh, dres, name):
    def fn(x, dh, dres, w):
        r = lax.rsqrt(jnp.mean(x * x, axis=-1, keepdims=True) + RMS_EPS)
        xh = x * r
        dxh = dh * w
        dx = dres + r * (dxh - xh * jnp.mean(dxh * xh, axis=-1, keepdims=True))
        return dx, dx, jnp.sum(dh * xh, axis=0, keepdims=True)
    d = x.shape[1]
    return _rowwise(fn, [x, dh, dres], [w], [(d, F32), (d, BF16)], [((1, d), F32)], tile=256, name=name)


def _final_loss(x, w, tgt, name):
    d = x.shape[1]

    def fn(x, t, w):
        r = lax.rsqrt(jnp.mean(x * x, axis=-1, keepdims=True) + RMS_EPS)
        xh = x * r
        err = xh * w - t
        part = jnp.sum(jnp.mean(err * err, axis=-1, keepdims=True), axis=0, keepdims=True) * 0.5
        dy = err * (1.0 / d)
        dxh = dy * w
        dx = r * (dxh - xh * jnp.mean(dxh * xh, axis=-1, keepdims=True))
        return dx, dx, part, jnp.sum(dy * xh, axis=0, keepdims=True)
    dx, dxb, part, dw = _rowwise(fn, [x, tgt], [w], [(d, F32), (d, BF16)], [((1, 1), F32), ((1, d), F32)],
                                 tile=256, name=name)
    return part, dx, dxb, dw


def _softplus_fwd(dtr, bias, name):
    def fn(r, b):
        v = r + b
        return (jnp.maximum(v, 0.0) + jnp.log(1.0 + jnp.exp(-jnp.abs(v))),)
    return _rowwise(fn, [dtr], [bias], [(LANES, F32)], tile=512, name=name)[0]


def _softplus_bwd(ddt, dtr, bias, n_heads, into, name):
    def fn(g, r, b):
        lane = lax.broadcasted_iota(jnp.int32, g.shape, 1)
        d = jnp.where(lane < n_heads, g * jax.nn.sigmoid(r + b), 0.0)
        return d, jnp.sum(d, axis=0, keepdims=True)
    return _rowwise(fn, [ddt, dtr], [bias], [(LANES, BF16, *into)], [((1, LANES), F32)], tile=512, name=name)


def _gnorm_fwd(y, z, w, n_groups, name):
    di = y.shape[1]
    gs = di // n_groups

    def fn(y, z, w):
        y2 = y * (z * jax.nn.sigmoid(z))
        out = []
        for g in range(n_groups):
            sl = y2[:, g * gs:(g + 1) * gs]
            r = lax.rsqrt(jnp.mean(sl * sl, axis=-1, keepdims=True) + GATED_NORM_EPS)
            out.append(sl * r)
        return (jnp.concatenate(out, axis=1) * w,)
    return _rowwise(fn, [y, z], [w], [(di, BF16)], tile=256, name=name)[0]


def _gnorm_bwd(dyn, y, z, w, n_groups, into, name):
    di = y.shape[1]
    gs = di // n_groups

    def fn(dyn, y, z, w):
        sig = jax.nn.sigmoid(z)
        sz = z * sig
        y2 = y * sz
        d2n = dyn * w
        dy2, yhat = [], []
        for g in range(n_groups):
            sl = y2[:, g * gs:(g + 1) * gs]
            dg = d2n[:, g * gs:(g + 1) * gs]
            r = lax.rsqrt(jnp.mean(sl * sl, axis=-1, keepdims=True) + GATED_NORM_EPS)
            yh = sl * r
            dy2.append(r * (dg - yh * jnp.mean(dg * yh, axis=-1, keepdims=True)))
            yhat.append(yh)
        dy2 = jnp.concatenate(dy2, axis=1)
        yhat = jnp.concatenate(yhat, axis=1)
        dz = dy2 * y * (sig * (1.0 + z * (1.0 - sig)))
        return dy2 * sz, dz, jnp.sum(dyn * yhat, axis=0, keepdims=True)
    return _rowwise(fn, [dyn, y, z], [w], [(di, F32), (di, BF16, *into)], [((1, di), F32)], tile=128, name=name)


def _merge_fwd(os_, lses, name):
    n = len(os_)

    def fn(*v):
        o, l = v[:n], v[n:]
        m = functools.reduce(jnp.maximum, l)
        e = [jnp.exp(li - m) for li in l]
        tot = functools.reduce(jnp.add, e)
        acc = functools.reduce(jnp.add, [ei * oi for ei, oi in zip(e, o)]) / tot
        return acc, acc, m + jnp.log(tot)
    c = os_[0].shape[1]
    return _rowwise(fn, list(os_) + list(lses), [], [(c, F32), (c, BF16), (c, F32)], tile=256, name=name)


def _delta(do, o, name):
    c = o.shape[1]

    def fn(do, o):
        p = do * o
        out = [jnp.broadcast_to(jnp.sum(p[:, j:j + ATT_HEAD_DIM], axis=-1, keepdims=True), (p.shape[0], ATT_HEAD_DIM))
               for j in range(0, c, ATT_HEAD_DIM)]
        return (jnp.concatenate(out, axis=1),)
    return _rowwise(fn, [do, o], [], [(c, F32)], tile=256, name=name)[0]


def _lane_place(cols):
    rows = cols[0].shape[0]
    lane = lax.broadcasted_iota(jnp.int32, (rows, LANES), 1)
    out = jnp.zeros((rows, LANES), F32)
    for j, c in enumerate(cols):
        out = jnp.where(lane == j, c, out)
    return out


def _merge_heads(os_, lses, name):
    n = len(os_)
    n_kv, rep, hd = ATT_KV_HEADS_PER_GROUP, ATT_HEADS_PER_GROUP // ATT_KV_HEADS_PER_GROUP, ATT_HEAD_DIM

    def fn(*v):
        o, l = v[:n], v[n:]
        out, lse = [], []
        for h in range(n_kv):
            cols = []
            for j in range(rep):
                hh = h * rep + j
                lg = [li[h][:, j:j + 1] for li in l]
                m = functools.reduce(jnp.maximum, lg)
                e = [jnp.exp(x - m) for x in lg]
                tot = functools.reduce(jnp.add, e)
                acc = functools.reduce(jnp.add, [ei * oi[hh] for ei, oi in zip(e, o)])
                out.append(acc / tot)
                cols.append(m + jnp.log(tot))
            lse.append(_lane_place(cols))
        merged = jnp.concatenate(out, axis=1)
        return merged, merged, lse
    c = os_[0].shape[0] * hd
    return _rowwise(fn, list(os_) + list(lses), [], [(c, F32), (c, BF16), ((n_kv, LANES), F32)], tile=256, name=name)


def _delta_heads(do, o, name):
    n_kv, rep, hd = ATT_KV_HEADS_PER_GROUP, ATT_HEADS_PER_GROUP // ATT_KV_HEADS_PER_GROUP, ATT_HEAD_DIM

    def fn(do, o):
        p = do * o
        return ([_lane_place([jnp.sum(p[:, (h * rep + j) * hd:(h * rep + j + 1) * hd], axis=-1, keepdims=True)
                              for j in range(rep)]) for h in range(n_kv)],)
    return _rowwise(fn, [do, o], [], [((n_kv, LANES), F32)], tile=256, name=name)[0]


def _sum_slabs(recv, name):
    def body(r_ref, o_ref):
        acc = r_ref[0]
        for k in range(1, NDEV):
            acc = acc + r_ref[k]
        o_ref[...] = acc
    return pl.pallas_call(body, name=name, out_shape=jax.ShapeDtypeStruct(recv.shape[1:], F32),
                          compiler_params=_params())(recv)


def _shift_down(x, k):
    if k == 0:
        return x
    r = pltpu.roll(x, k, 0)
    row = lax.broadcasted_iota(jnp.int32, (SUBLANES, x.shape[1]), 0)
    return jnp.concatenate([jnp.where(row >= k, r[:SUBLANES], 0.0), r[SUBLANES:]], axis=0)


def _shift_up(x, k):
    if k == 0:
        return x
    s = x.shape[0]
    r = pltpu.roll(x, s - k, 0)
    row = lax.broadcasted_iota(jnp.int32, (SUBLANES, x.shape[1]), 0)
    return jnp.concatenate([r[:s - SUBLANES], jnp.where(row < SUBLANES - k, r[s - SUBLANES:], 0.0)], axis=0)


def _conv(x, w):
    kw = w.shape[0]
    return functools.reduce(jnp.add, [w[k:k + 1, :] * _shift_down(x, kw - 1 - k) for k in range(kw)])


def _conv_t(dy, w):
    kw = w.shape[0]
    return functools.reduce(jnp.add, [w[k:k + 1, :] * _shift_up(dy, kw - 1 - k) for k in range(kw)])


def _conv_dw(x, dy, dw_ref):
    kw = dw_ref.shape[0]
    for k in range(kw):
        dw_ref[k:k + 1, :] = jnp.sum(dy * _shift_down(x, kw - 1 - k), axis=0, keepdims=True)


def _dsilu(pre):
    sig = jax.nn.sigmoid(pre)
    return sig * (1.0 + pre * (1.0 - sig))


def _col_specs(s, c, kw, tc):
    return (pl.BlockSpec((s, tc), lambda j: (0, j)), pl.BlockSpec((kw, tc), lambda j: (0, j)),
            pl.BlockSpec((1, tc), lambda j: (0, j)))


def _conv_silu_fwd(x, col0, w, b, name):
    s, c = x.shape[0], w.shape[1]
    tc = LANES
    xs, ws, bs = _col_specs(s, c, w.shape[0], tc)
    xwin = pl.BlockSpec((s, tc), lambda j: (0, j + col0 // tc))

    def body(x_ref, w_ref, b_ref, o_ref):
        pre = _conv(x_ref[...], w_ref[...]) + b_ref[...]
        o_ref[...] = pre * jax.nn.sigmoid(pre)
    return pl.pallas_call(body, name=name, grid=(c // tc,), in_specs=[xwin, ws, bs], out_specs=xs,
                          out_shape=jax.ShapeDtypeStruct((s, c), F32), compiler_params=_params(("parallel",)))(x, w, b)


def _conv_silu_bwd(x, col0, w, b, dy, into, name):
    s, c = x.shape[0], w.shape[1]
    tc = LANES
    xs, ws, bs = _col_specs(s, c, w.shape[0], tc)
    xwin = pl.BlockSpec((s, tc), lambda j: (0, j + col0 // tc))

    def body(x_ref, w_ref, b_ref, dy_ref, _, dx_ref, dw_ref, db_ref):
        xv, wv = x_ref[...], w_ref[...]
        pre = _conv(xv, wv) + b_ref[...]
        dpre = dy_ref[...] * _dsilu(pre)
        dx_ref[...] = _conv_t(dpre, wv).astype(dx_ref.dtype)
        _conv_dw(xv, dpre, dw_ref)
        db_ref[...] = jnp.sum(dpre, axis=0, keepdims=True)
    return pl.pallas_call(
        body, name=name, grid=(c // tc,), in_specs=[xwin, ws, bs, xs, pl.BlockSpec(memory_space=pl.ANY)],
        out_specs=[xwin, ws, bs], input_output_aliases={4: 0},
        out_shape=[jax.ShapeDtypeStruct(into.shape, into.dtype), jax.ShapeDtypeStruct(w.shape, F32),
                   jax.ShapeDtypeStruct((1, c), F32)],
        compiler_params=_params(("parallel",)))(x, w, b, dy, into)


def _gate_specs(s, f, kw):
    nt = f // LANES
    return (pl.BlockSpec((s, LANES), lambda j: (0, j)), pl.BlockSpec((s, LANES), lambda j: (0, j + nt)),
            pl.BlockSpec((kw, LANES), lambda j: (0, j)), pl.BlockSpec((kw, LANES), lambda j: (0, j + nt)))


def _ffn_gate_fwd2(u, w, name):
    s, f = u.shape[0], u.shape[1] // 2
    gs, vs, wgs, wvs = _gate_specs(s, f, w.shape[0])

    def body(g_ref, v_ref, wg_ref, wv_ref, o_ref):
        g = _conv(g_ref[...], wg_ref[...])
        v = _conv(v_ref[...], wv_ref[...])
        o_ref[...] = (g * jax.nn.sigmoid(g) * v).astype(o_ref.dtype)
    return pl.pallas_call(body, name=name, grid=(f // LANES,), in_specs=[gs, vs, wgs, wvs], out_specs=gs,
                          out_shape=jax.ShapeDtypeStruct((s, f), BF16),
                          compiler_params=_params(("parallel",)))(u, u, w, w)


def _ffn_gate_bwd2(u, w, df, name):
    s, f = u.shape[0], u.shape[1] // 2
    kw = w.shape[0]
    gs, vs, wgs, wvs = _gate_specs(s, f, kw)

    def body(g_ref, v_ref, wg_ref, wv_ref, df_ref, du_ref, dw_ref):
        gp, vp, wgv, wvv = g_ref[...], v_ref[...], wg_ref[...], wv_ref[...]
        g = _conv(gp, wgv)
        v = _conv(vp, wvv)
        dfv = df_ref[...]
        dg = dfv * v * _dsilu(g)
        dv = dfv * (g * jax.nn.sigmoid(g))
        du_ref[0] = _conv_t(dg, wgv).astype(du_ref.dtype)
        du_ref[1] = _conv_t(dv, wvv).astype(du_ref.dtype)
        _conv_dw(gp, dg, dw_ref.at[0])
        _conv_dw(vp, dv, dw_ref.at[1])
    return pl.pallas_call(
        body, name=name, grid=(f // LANES,), in_specs=[gs, vs, wgs, wvs, gs],
        out_specs=[pl.BlockSpec((2, s, LANES), lambda j: (0, 0, j)), pl.BlockSpec((2, kw, LANES), lambda j: (0, 0, j))],
        out_shape=[jax.ShapeDtypeStruct((2, s, f), BF16), jax.ShapeDtypeStruct((2, kw, f), F32)],
        compiler_params=_params(("parallel",)))(u, u, w, w, df)


def _ffn_gate_fwd(ug, uv, wg, wv, name):
    s, c = ug.shape
    tc = LANES
    xs, ws, _ = _col_specs(s, c, wg.shape[0], tc)

    def body(g_ref, v_ref, wg_ref, wv_ref, o_ref):
        g = _conv(g_ref[...], wg_ref[...])
        v = _conv(v_ref[...], wv_ref[...])
        o_ref[...] = (g * jax.nn.sigmoid(g) * v).astype(o_ref.dtype)
    return pl.pallas_call(body, name=name, grid=(c // tc,), in_specs=[xs, xs, ws, ws], out_specs=xs,
                          out_shape=jax.ShapeDtypeStruct((s, c), BF16),
                          compiler_params=_params(("parallel",)))(ug, uv, wg, wv)


def _ffn_gate_bwd(ug, uv, wg, wv, df, name):
    s, c = ug.shape
    tc = LANES
    xs, ws, _ = _col_specs(s, c, wg.shape[0], tc)

    def body(g_ref, v_ref, wg_ref, wv_ref, df_ref, dg_ref, dv_ref, dwg_ref, dwv_ref):
        gp, vp, wgv, wvv = g_ref[...], v_ref[...], wg_ref[...], wv_ref[...]
        g = _conv(gp, wgv)
        v = _conv(vp, wvv)
        dfv = df_ref[...]
        dg = dfv * v * _dsilu(g)
        dv = dfv * (g * jax.nn.sigmoid(g))
        dg_ref[...] = _conv_t(dg, wgv).astype(dg_ref.dtype)
        dv_ref[...] = _conv_t(dv, wvv).astype(dv_ref.dtype)
        _conv_dw(gp, dg, dwg_ref)
        _conv_dw(vp, dv, dwv_ref)
    return pl.pallas_call(
        body, name=name, grid=(c // tc,), in_specs=[xs, xs, ws, ws, xs], out_specs=[xs, xs, ws, ws],
        out_shape=[jax.ShapeDtypeStruct((s, c), BF16), jax.ShapeDtypeStruct((s, c), BF16),
                   jax.ShapeDtypeStruct(wg.shape, F32), jax.ShapeDtypeStruct(wv.shape, F32)],
        compiler_params=_params(("parallel",)))(ug, uv, wg, wv, df)


def _ssd_common(dt, alog, n_heads):
    ln = dt.shape[0]
    lane = lax.broadcasted_iota(jnp.int32, (1, LANES), 1)
    a = jnp.where(lane < n_heads, -jnp.exp(alog), 0.0)
    row = lax.broadcasted_iota(jnp.int32, (ln, ln), 0)
    col = lax.broadcasted_iota(jnp.int32, (ln, ln), 1)
    tril = col <= row
    acs = _dot(tril.astype(F32), dt * a, precision=HIGHEST)
    return a, acs, acs.T, tril


def _ssd_fwd(xbc, dt, alog, dskip, di, n_heads, n_groups, name):
    s, convd = xbc.shape
    ln, p, ns = SSM_CHUNK, SSM_HEAD_DIM, SSM_D_STATE
    nc, hg = s // ln, n_heads // n_groups

    def body(x_ref, dt_ref, alog_ref, d_ref, y_ref, prev_ref, st):
        @pl.when(pl.program_id(0) == 0)
        def _():
            st[...] = jnp.zeros(st.shape, F32)

        dt = dt_ref[...]
        _, acs, acs_t, tril = _ssd_common(dt, alog_ref[...], n_heads)
        e_all = jnp.exp(acs)
        last = acs[ln - 1:ln, :]
        ds_all = jnp.exp(last - acs)
        t_all = jnp.exp(last)
        dsk = d_ref[...]
        for g in range(n_groups):
            bg = x_ref[:, di + g * ns:di + (g + 1) * ns].astype(BF16)
            cg = x_ref[:, di + (n_groups + g) * ns:di + (n_groups + g + 1) * ns].astype(BF16)
            gm = _dot(cg, bg, _NT)
            for j in range(hg):
                h = g * hg + j
                xh = x_ref[:, h * p:(h + 1) * p]
                xdt = xh * dt[:, h:h + 1]
                seg = acs[:, h:h + 1] - acs_t[h:h + 1, :]
                m = jnp.where(tril, gm * jnp.exp(jnp.where(tril, seg, 0.0)), 0.0)
                prev = st[h]
                prev_ref[0, h] = prev
                y = _dot(m.astype(BF16), xdt.astype(BF16))
                y = y + _dot(cg, prev.astype(BF16), _NT) * e_all[:, h:h + 1]
                y = y + xh * dsk[:, h:h + 1]
                snew = _dot((xdt * ds_all[:, h:h + 1]).astype(BF16), bg, _TN)
                st[h] = prev * t_all[:, h:h + 1] + snew
                y_ref[:, h * p:(h + 1) * p] = y

    vec = pl.BlockSpec((1, LANES), lambda c: (0, 0))
    return pl.pallas_call(
        body, name=name, grid=(nc,),
        in_specs=[pl.BlockSpec((ln, convd), lambda c: (c, 0)), pl.BlockSpec((ln, LANES), lambda c: (c, 0)), vec, vec],
        out_specs=[pl.BlockSpec((ln, di), lambda c: (c, 0)),
                   pl.BlockSpec((1, n_heads, p, ns), lambda c: (c, 0, 0, 0))],
        out_shape=[jax.ShapeDtypeStruct((s, di), F32), jax.ShapeDtypeStruct((nc, n_heads, p, ns), F32)],
        scratch_shapes=[pltpu.VMEM((n_heads, p, ns), F32)],
        compiler_params=_params(("arbitrary",)))(xbc, dt, alog, dskip)


def _ssd_bwd(xbc, dt, alog, dskip, prev_all, dy, di, n_heads, n_groups, name):
    s, convd = xbc.shape
    ln, p, ns = SSM_CHUNK, SSM_HEAD_DIM, SSM_D_STATE
    nc, hg = s // ln, n_heads // n_groups

    def body(x_ref, dt_ref, alog_ref, d_ref, prev_ref, dy_ref, dx_ref, ddt_ref, da_ref, dd_ref, dh):
        step = pl.program_id(0)

        @pl.when(step == 0)
        def _():
            dh[...] = jnp.zeros(dh.shape, F32)
            da_ref[...] = jnp.zeros(da_ref.shape, F32)
            dd_ref[...] = jnp.zeros(dd_ref.shape, F32)

        dt = dt_ref[...]
        a, acs, acs_t, tril = _ssd_common(dt, alog_ref[...], n_heads)
        e_all = jnp.exp(acs)
        last = acs[ln - 1:ln, :]
        ds_all = jnp.exp(last - acs)
        t_all = jnp.exp(last)
        dsk = d_ref[...]
        lane = lax.broadcasted_iota(jnp.int32, (ln, LANES), 1)
        lane1 = lax.broadcasted_iota(jnp.int32, (1, LANES), 1)
        sub = lax.broadcasted_iota(jnp.int32, (LANES, ln), 0)
        rowi = lax.broadcasted_iota(jnp.int32, (ln, LANES), 0)
        dacs_c = jnp.zeros((ln, LANES), F32)
        dacs_r = jnp.zeros((LANES, ln), F32)
        dlast = jnp.zeros((1, LANES), F32)
        ddt_x = jnp.zeros((ln, LANES), F32)
        dd = jnp.zeros((1, LANES), F32)

        def tot(v):
            return jnp.sum(jnp.sum(v, axis=1, keepdims=True), axis=0, keepdims=True)

        for g in range(n_groups):
            bg = x_ref[:, di + g * ns:di + (g + 1) * ns].astype(BF16)
            cg = x_ref[:, di + (n_groups + g) * ns:di + (n_groups + g + 1) * ns].astype(BF16)
            gm = _dot(cg, bg, _NT)
            dgm = jnp.zeros((ln, ln), F32)
            dcg = jnp.zeros((ln, ns), F32)
            dbg = jnp.zeros((ln, ns), F32)
            for j in range(hg):
                h = g * hg + j
                xh = x_ref[:, h * p:(h + 1) * p]
                dth = dt[:, h:h + 1]
                xdt = xh * dth
                dyh = dy_ref[:, h * p:(h + 1) * p]
                eh, dsh, th = e_all[:, h:h + 1], ds_all[:, h:h + 1], t_all[:, h:h + 1]
                seg = acs[:, h:h + 1] - acs_t[h:h + 1, :]
                dec = jnp.where(tril, jnp.exp(jnp.where(tril, seg, 0.0)), 0.0)
                m = gm * dec
                prev = prev_ref[0, h]
                dhn = dh[h]
                prevb, dhb, dyb, xdtb = prev.astype(BF16), dhn.astype(BF16), dyh.astype(BF16), xdt.astype(BF16)
                yo = _dot(cg, prevb, _NT)
                dyob = (dyh * eh).astype(BF16)
                c_col = jnp.sum(dyh * yo, axis=1, keepdims=True) * eh
                dcg = dcg + _dot(dyob, prevb)
                dprev = th * dhn + _dot(dyob, cg, _TN)
                dtt = tot(dhn * prev)
                w = _dot(bg, dhb, _NT)
                dxdt = w * dsh
                dds = jnp.sum(w * xdt, axis=1, keepdims=True)
                dbg = dbg + _dot((xdt * dsh).astype(BF16), dhb)
                dm = _dot(dyb, xdtb, _NT)
                dxdt = dxdt + _dot(m.astype(BF16), dyb, _TN)
                dgm = dgm + dm * dec
                q = dm * m
                c_col = c_col + jnp.sum(q, axis=1, keepdims=True) - dds * dsh
                r_row = -jnp.sum(q, axis=0, keepdims=True)
                dlast_h = tot(dds * dsh) + dtt * th
                dacs_c = dacs_c + jnp.where(lane == h, c_col, 0.0)
                dacs_r = dacs_r + jnp.where(sub == h, r_row, 0.0)
                dlast = dlast + jnp.where(lane1 == h, dlast_h, 0.0)
                ddt_x = ddt_x + jnp.where(lane == h, jnp.sum(dxdt * xh, axis=1, keepdims=True), 0.0)
                dd = dd + jnp.where(lane1 == h, tot(dyh * xh), 0.0)
                dx_ref[:, h * p:(h + 1) * p] = dxdt * dth + dyh * dsk[:, h:h + 1]
                dh[h] = dprev
            dgb = dgm.astype(BF16)
            dx_ref[:, di + g * ns:di + (g + 1) * ns] = dbg + _dot(dgb, cg, _TN)
            dx_ref[:, di + (n_groups + g) * ns:di + (n_groups + g + 1) * ns] = dcg + _dot(dgb, bg)

        dacs = dacs_c + dacs_r.T + jnp.where(rowi == ln - 1, dlast, 0.0)
        row = lax.broadcasted_iota(jnp.int32, (ln, ln), 0)
        col = lax.broadcasted_iota(jnp.int32, (ln, ln), 1)
        dadt = _dot((col >= row).astype(F32), dacs, precision=HIGHEST)
        ddt_ref[...] = dadt * a + ddt_x
        da_ref[...] += jnp.sum(dadt * dt, axis=0, keepdims=True)
        dd_ref[...] += dd

        @pl.when(step == nc - 1)
        def _():
            da_ref[...] = da_ref[...] * a

    vec = pl.BlockSpec((1, LANES), lambda c: (0, 0))
    rev = lambda c: (nc - 1 - c, 0)
    return pl.pallas_call(
        body, name=name, grid=(nc,),
        in_specs=[pl.BlockSpec((ln, convd), rev), pl.BlockSpec((ln, LANES), rev), vec, vec,
                  pl.BlockSpec((1, n_heads, p, ns), lambda c: (nc - 1 - c, 0, 0, 0)), pl.BlockSpec((ln, di), rev)],
        out_specs=[pl.BlockSpec((ln, convd), rev), pl.BlockSpec((ln, LANES), rev), vec, vec],
        out_shape=[jax.ShapeDtypeStruct((s, convd), F32), jax.ShapeDtypeStruct((s, LANES), F32),
                   jax.ShapeDtypeStruct((1, LANES), F32), jax.ShapeDtypeStruct((1, LANES), F32)],
        scratch_shapes=[pltpu.VMEM((n_heads, p, ns), F32)],
        compiler_params=_params(("arbitrary",)))(xbc, dt, alog, dskip, prev_all, dy)


def _split(x, n):
    out = []
    for _ in range(n):
        piece = x.astype(BF16)
        out.append(piece)
        x = x - piece.astype(F32)
    return out


def _spread(x, onehot, n=2):
    return functools.reduce(jnp.add, [_dot(piece, onehot) for piece in _split(x, n)])


def _head_maps(di, p):
    e = (jnp.arange(di, dtype=jnp.int32)[None, :] // p == jnp.arange(LANES, dtype=jnp.int32)[:, None]).astype(BF16)
    return e, e.T


def _ssd_wide(dt, acs, acs_t, dskip, e_ref, et_ref):
    ln = dt.shape[0]
    last = acs[ln - 1:ln, :]
    stack = jnp.concatenate([dt, jnp.exp(acs), jnp.exp(last - acs), jnp.broadcast_to(dskip, (8, LANES))], axis=0)
    wide = _spread(stack, e_ref[...])
    tb = jnp.exp(jnp.broadcast_to(acs_t[:, ln - 1:ln], (LANES, LANES)))
    texp = functools.reduce(jnp.add, [_dot(et_ref[...], piece) for piece in _split(tb, 3)])
    return wide[:ln], wide[ln:2 * ln], wide[2 * ln:3 * ln], wide[3 * ln:3 * ln + 1], texp


def _ssd_fwd2(xbc, dt, alog, dskip, di, n_heads, n_groups, name):
    s, convd = xbc.shape
    ln, p, ns = SSM_CHUNK, SSM_HEAD_DIM, SSM_D_STATE
    nc, hg = s // ln, n_heads // n_groups
    gw = hg * p
    e64, e64t = _head_maps(di, p)

    def body(x_ref, dt_ref, alog_ref, d_ref, e_ref, et_ref, y_ref, prev_ref, st):
        @pl.when(pl.program_id(0) == 0)
        def _():
            st[...] = jnp.zeros(st.shape, F32)

        dt = dt_ref[...]
        _, acs, acs_t, tril = _ssd_common(dt, alog_ref[...], n_heads)
        dte, ee, dse, dske, texp = _ssd_wide(dt, acs, acs_t, d_ref[...], e_ref, et_ref)
        x = x_ref[:, :di]
        xdt = x * dte
        xdtb = xdt.astype(BF16)
        xdsb = (xdt * dse).astype(BF16)
        for g in range(n_groups):
            rows = slice(g * gw, (g + 1) * gw)
            bg = x_ref[:, di + g * ns:di + (g + 1) * ns].astype(BF16)
            cg = x_ref[:, di + (n_groups + g) * ns:di + (n_groups + g + 1) * ns].astype(BF16)
            gm = _dot(cg, bg, _NT)
            prev = st[rows, :]
            prev_ref[0, rows, :] = prev
            yo = _dot(cg, prev.astype(BF16), _NT)
            for j in range(hg):
                h = g * hg + j
                seg = acs[:, h:h + 1] - acs_t[h:h + 1, :]
                m = jnp.where(tril, gm * jnp.exp(jnp.where(tril, seg, 0.0)), 0.0)
                y_ref[:, h * p:(h + 1) * p] = _dot(m.astype(BF16), xdtb[:, h * p:(h + 1) * p])
            y_ref[:, rows] = y_ref[:, rows] + yo * ee[:, rows] + x[:, rows] * dske[:, rows]
            st[rows, :] = prev * texp[rows, :] + _dot(xdsb[:, rows], bg, _TN)

    vec = pl.BlockSpec((1, LANES), lambda c: (0, 0))
    return pl.pallas_call(
        body, name=name, grid=(nc,),
        in_specs=[pl.BlockSpec((ln, convd), lambda c: (c, 0)), pl.BlockSpec((ln, LANES), lambda c: (c, 0)), vec, vec,
                  pl.BlockSpec(e64.shape, lambda c: (0, 0)), pl.BlockSpec(e64t.shape, lambda c: (0, 0))],
        out_specs=[pl.BlockSpec((ln, di), lambda c: (c, 0)), pl.BlockSpec((1, di, ns), lambda c: (c, 0, 0))],
        out_shape=[jax.ShapeDtypeStruct((s, di), F32), jax.ShapeDtypeStruct((nc, di, ns), F32)],
        scratch_shapes=[pltpu.VMEM((di, ns), F32)],
        compiler_params=_params(("arbitrary",)))(xbc, dt, alog, dskip, e64, e64t)


def _ssd_bwd2(xbc, dt, alog, dskip, prev_all, dy, di, n_heads, n_groups, name):
    s, convd = xbc.shape
    ln, p, ns = SSM_CHUNK, SSM_HEAD_DIM, SSM_D_STATE
    nc, hg = s // ln, n_heads // n_groups
    gw = hg * p
    e64, e64t = _head_maps(di, p)

    def body(x_ref, dt_ref, alog_ref, d_ref, e_ref, et_ref, prev_ref, dy_ref,
             dx_ref, ddt_ref, da_ref, dd_ref, dh, yo_ref, w_ref):
        step = pl.program_id(0)

        @pl.when(step == 0)
        def _():
            dh[...] = jnp.zeros(dh.shape, F32)
            da_ref[...] = jnp.zeros(da_ref.shape, F32)
            dd_ref[...] = jnp.zeros(dd_ref.shape, F32)

        dt = dt_ref[...]
        a, acs, acs_t, tril = _ssd_common(dt, alog_ref[...], n_heads)
        dte, ee, dse, dske, texp = _ssd_wide(dt, acs, acs_t, d_ref[...], e_ref, et_ref)
        row = lax.broadcasted_iota(jnp.int32, (ln, ln), 0)
        col = lax.broadcasted_iota(jnp.int32, (ln, ln), 1)
        triu = col >= row
        x = x_ref[:, :di]
        dy = dy_ref[...]
        xdt = x * dte
        xdtb = xdt.astype(BF16)
        xdsb = (xdt * dse).astype(BF16)
        dyb = dy.astype(BF16)
        dyob = (dy * ee).astype(BF16)
        dhn = dh[...]
        dhb = dhn.astype(BF16)
        per_head = functools.reduce(jnp.add, [_dot(e_ref[...], piece) for piece in _split(dhn * prev_ref[0], 2)])
        ones8 = jnp.ones((8, LANES), BF16)
        dtt = functools.reduce(jnp.add, [_dot(ones8, piece, _NT) for piece in _split(per_head, 2)])[0:1]
        dacs_c = jnp.zeros((ln, LANES), F32)
        dacs_r = jnp.zeros((LANES, ln), F32)
        for g in range(n_groups):
            rows = slice(g * gw, (g + 1) * gw)
            bg = x_ref[:, di + g * ns:di + (g + 1) * ns].astype(BF16)
            cg = x_ref[:, di + (n_groups + g) * ns:di + (n_groups + g + 1) * ns].astype(BF16)
            gmt = _dot(bg, cg, _NT)
            prevb = prev_ref[0, rows, :].astype(BF16)
            dcg = _dot(dyob[:, rows], prevb)
            dh[rows, :] = texp[rows, :] * dhn[rows, :] + _dot(dyob[:, rows], cg, _TN)
            w = _dot(bg, dhb[rows, :], _NT)
            dbg = _dot(xdsb[:, rows], dhb[rows, :])
            yo_ref[:, rows] = _dot(cg, prevb, _NT)
            w_ref[:, rows] = w
            dgmt = jnp.zeros((ln, ln), F32)
            q_hi, q_lo = [], []
            for j in range(hg):
                h = g * hg + j
                segt = acs_t[h:h + 1, :] - acs[:, h:h + 1]
                dect = jnp.where(triu, jnp.exp(jnp.where(triu, segt, 0.0)), 0.0)
                dyh, xh = dyb[:, h * p:(h + 1) * p], xdtb[:, h * p:(h + 1) * p]
                mt = gmt * dect
                dmt = _dot(xh, dyh, _NT)
                dx_ref[:, h * p:(h + 1) * p] = _dot(mt.astype(BF16), dyh)
                dgmt = dgmt + dmt * dect
                hi, lo = _split(dmt * mt, 2)
                q_hi.append(hi)
                q_lo.append(lo)
            sel_c = (lax.broadcasted_iota(jnp.int32, (hg * ln, LANES), 1)
                     == g * hg + lax.broadcasted_iota(jnp.int32, (hg * ln, LANES), 0) // ln).astype(BF16)
            sel_r = (lax.broadcasted_iota(jnp.int32, (LANES, hg * ln), 0)
                     == g * hg + lax.broadcasted_iota(jnp.int32, (LANES, hg * ln), 1) // ln).astype(BF16)
            for pieces in (q_hi, q_lo):
                dacs_c = dacs_c - _dot(jnp.concatenate(pieces, axis=1), sel_c)
                dacs_r = dacs_r + _dot(sel_r, jnp.concatenate(pieces, axis=0))
            dgb = dgmt.astype(BF16)
            dx_ref[:, di + g * ns:di + (g + 1) * ns] = dbg + _dot(dgb, cg)
            dx_ref[:, di + (n_groups + g) * ns:di + (n_groups + g + 1) * ns] = dcg + _dot(dgb, bg, _TN)

        wds = w_ref[...] * dse
        dxdt = dx_ref[:, :di] + wds
        red = _spread(jnp.concatenate([dxdt * x, dy * yo_ref[...] * ee, xdt * wds, dy * x], axis=0), et_ref[...], n=1)
        ddt_x, r_off, r_state, ddr = red[:ln], red[ln:2 * ln], red[2 * ln:3 * ln], red[3 * ln:]
        dx_ref[:, :di] = dxdt * dte + dy * dske
        rowi = lax.broadcasted_iota(jnp.int32, (ln, LANES), 0)
        dlast = jnp.sum(r_state, axis=0, keepdims=True) + dtt * jnp.exp(acs[ln - 1:ln, :])
        dacs = r_off - r_state + dacs_c + dacs_r.T + jnp.where(rowi == ln - 1, dlast, 0.0)
        dadt = _dot(triu.astype(F32), dacs, precision=HIGHEST)
        ddt_ref[...] = dadt * a + ddt_x
        da_ref[...] += jnp.sum(dadt * dt, axis=0, keepdims=True)
        dd_ref[...] += jnp.sum(ddr, axis=0, keepdims=True)

        @pl.when(step == nc - 1)
        def _():
            da_ref[...] = da_ref[...] * a

    vec = pl.BlockSpec((1, LANES), lambda c: (0, 0))
    rev = lambda c: (nc - 1 - c, 0)
    return pl.pallas_call(
        body, name=name, grid=(nc,),
        in_specs=[pl.BlockSpec((ln, convd), rev), pl.BlockSpec((ln, LANES), rev), vec, vec,
                  pl.BlockSpec(e64.shape, lambda c: (0, 0)), pl.BlockSpec(e64t.shape, lambda c: (0, 0)),
                  pl.BlockSpec((1, di, ns), lambda c: (nc - 1 - c, 0, 0)), pl.BlockSpec((ln, di), rev)],
        out_specs=[pl.BlockSpec((ln, convd), rev), pl.BlockSpec((ln, LANES), rev), vec, vec],
        out_shape=[jax.ShapeDtypeStruct((s, convd), F32), jax.ShapeDtypeStruct((s, LANES), F32),
                   jax.ShapeDtypeStruct((1, LANES), F32), jax.ShapeDtypeStruct((1, LANES), F32)],
        scratch_shapes=[pltpu.VMEM((di, ns), F32), pltpu.VMEM((ln, di), F32), pltpu.VMEM((ln, di), F32)],
        compiler_params=_params(("arbitrary",)))(xbc, dt, alog, dskip, e64, e64t, prev_all, dy)


def _perm(a, d):
    if d == 1:
        return a
    s = a.shape[0]
    return a.reshape(s // d, d, -1).transpose(1, 0, 2).reshape(s, -1)


def _unperm(a, d):
    if d == 1:
        return a
    s = a.shape[0]
    return a.reshape(d, s // d, -1).transpose(1, 0, 2).reshape(s, -1)


def _rot_tables(s, d):
    half = ROPE_DIM // 2
    inv_freq = jnp.power(jnp.float32(ROPE_THETA), -jnp.arange(0, ROPE_DIM, 2, dtype=F32) / ROPE_DIM)
    v = jnp.arange(s, dtype=jnp.int32)
    pos = (v % (s // d)) * d + v // (s // d)
    ang = pos.astype(F32)[:, None] * inv_freq[None, :]
    cos, sin = jnp.cos(ang), jnp.sin(ang)
    zero = jnp.zeros((s, ATT_HEAD_DIM - ROPE_DIM), F32)
    cf = jnp.concatenate([cos, cos, jnp.ones_like(zero)], axis=1)
    s1 = jnp.concatenate([-sin, jnp.zeros_like(sin), zero], axis=1)
    s2 = jnp.concatenate([jnp.zeros_like(sin), sin, zero], axis=1)
    assert half * 2 == ROPE_DIM
    return cf, s1, s2


def _rot(x, tabs, sign):
    cf, s1, s2 = tabs
    half = ROPE_DIM // 2
    left = pltpu.roll(x, ATT_HEAD_DIM - half, 1)
    right = pltpu.roll(x, half, 1)
    return x * cf + sign * (left * s1 + right * s2)


def _att_masks(n, n_blk, rep):
    b = ATT_BLOCK
    row = lax.broadcasted_iota(jnp.int32, (rep * b, b), 0) & (b - 1)
    col = lax.broadcasted_iota(jnp.int32, (rep * b, b), 1)
    off = jnp.where(n % n_blk != 0, 0, 2 * b)
    return col <= row, col >= row + off


def _stack(x, rep):
    return jnp.concatenate([x[:, j * ATT_HEAD_DIM:(j + 1) * ATT_HEAD_DIM] for j in range(rep)], axis=0)


def _att_specs(nb, rep, cur, prv):
    b, hd = ATT_BLOCK, ATT_HEAD_DIM
    q_spec = pl.BlockSpec((b, rep * hd), lambda h, n: (cur(n), h))
    kc_spec = pl.BlockSpec((b, hd), lambda h, n: (cur(n), h))
    kp_spec = pl.BlockSpec((b, hd), lambda h, n: (prv(n), h))
    tc_spec = pl.BlockSpec((b, hd), lambda h, n: (cur(n), 0))
    tp_spec = pl.BlockSpec((b, hd), lambda h, n: (prv(n), 0))
    return q_spec, kc_spec, kp_spec, tc_spec, tp_spec


def _attn_fwd(q, k, v, tabs, n_blk, name):
    s = q.shape[0]
    b, hd = ATT_BLOCK, ATT_HEAD_DIM
    nb = s // b
    n_kv = ATT_KV_HEADS_PER_GROUP
    rep = ATT_HEADS_PER_GROUP // n_kv
    scale = hd ** -0.5

    def body(q_ref, kc_ref, kp_ref, vc_ref, vp_ref, cfc, s1c, s2c, cfp, s1p, s2p, o_ref, lse_ref):
        n = pl.program_id(1)
        tc = (cfc[...], s1c[...], s2c[...])
        tp = (cfp[...], s1p[...], s2p[...])
        qv = q_ref[...]
        q4 = jnp.concatenate([_rot(qv[:, j * hd:(j + 1) * hd], tc, 1.0) for j in range(rep)], axis=0).astype(BF16)
        kc = _rot(kc_ref[...], tc, 1.0).astype(BF16)
        kp = _rot(kp_ref[...], tp, 1.0).astype(BF16)
        mc, mp = _att_masks(n, n_blk, rep)
        sc = jnp.where(mc, _dot(q4, kc, _NT) * scale, NEG)
        sp = jnp.where(mp, _dot(q4, kp, _NT) * scale, NEG)
        m = jnp.maximum(jnp.max(sc, axis=1, keepdims=True), jnp.max(sp, axis=1, keepdims=True))
        pc, pp = jnp.exp(sc - m), jnp.exp(sp - m)
        l = jnp.sum(pc, axis=1, keepdims=True) + jnp.sum(pp, axis=1, keepdims=True)
        o = (_dot(pc.astype(BF16), vc_ref[...].astype(BF16)) + _dot(pp.astype(BF16), vp_ref[...].astype(BF16))) / l
        lse = jnp.broadcast_to(m + jnp.log(l), (rep * b, hd))
        for j in range(rep):
            o_ref[:, j * hd:(j + 1) * hd] = o[j * b:(j + 1) * b]
            lse_ref[:, j * hd:(j + 1) * hd] = lse[j * b:(j + 1) * b]

    cur = lambda n: n
    prv = lambda n: jnp.maximum(n - 1, 0)
    q_spec, kc_spec, kp_spec, tc_spec, tp_spec = _att_specs(nb, rep, cur, prv)
    return pl.pallas_call(
        body, name=name, grid=(n_kv, nb),
        in_specs=[q_spec, kc_spec, kp_spec, kc_spec, kp_spec, tc_spec, tc_spec, tc_spec, tp_spec, tp_spec, tp_spec],
        out_specs=[q_spec, q_spec],
        out_shape=[jax.ShapeDtypeStruct(q.shape, F32), jax.ShapeDtypeStruct(q.shape, F32)],
        compiler_params=_params(("parallel", "arbitrary")))(q, k, k, v, v, *tabs, *tabs)


def _attn_bwd(q, k, v, do, lse, delta, tabs, n_blk, name):
    s = q.shape[0]
    b, hd = ATT_BLOCK, ATT_HEAD_DIM
    nb = s // b
    n_kv = ATT_KV_HEADS_PER_GROUP
    rep = ATT_HEADS_PER_GROUP // n_kv
    scale = hd ** -0.5

    def body(q_ref, do_ref, lse_ref, dl_ref, kc_ref, kp_ref, vc_ref, vp_ref, cfc, s1c, s2c, cfp, s1p, s2p,
             dq_ref, dk_ref, dv_ref, ck, cv):
        n = pl.program_id(1)
        tp = (cfp[...], s1p[...], s2p[...])

        @pl.when(n == 0)
        def _():
            ck[...] = jnp.zeros(ck.shape, F32)
            cv[...] = jnp.zeros(cv.shape, F32)

        @pl.when(n < nb)
        def _():
            tc = (cfc[...], s1c[...], s2c[...])
            qv = q_ref[...]
            q4 = jnp.concatenate([_rot(qv[:, j * hd:(j + 1) * hd], tc, 1.0) for j in range(rep)],
                                 axis=0).astype(BF16)
            do4 = _stack(do_ref[...], rep).astype(BF16)
            lse4 = _stack(lse_ref[...], rep)
            dl4 = _stack(dl_ref[...], rep)
            kc = _rot(kc_ref[...], tc, 1.0).astype(BF16)
            kp = _rot(kp_ref[...], tp, 1.0).astype(BF16)
            vc, vp = vc_ref[...].astype(BF16), vp_ref[...].astype(BF16)
            mc, mp = _att_masks(n, n_blk, rep)
            pc = jnp.where(mc, jnp.exp(jnp.where(mc, _dot(q4, kc, _NT) * scale - lse4, 0.0)), 0.0)
            pp = jnp.where(mp, jnp.exp(jnp.where(mp, _dot(q4, kp, _NT) * scale - lse4, 0.0)), 0.0)
            dsc = (pc * (_dot(do4, vc, _NT) - dl4)).astype(BF16)
            dsp = (pp * (_dot(do4, vp, _NT) - dl4)).astype(BF16)
            dq4 = (_dot(dsc, kc) + _dot(dsp, kp)) * scale
            for j in range(rep):
                dq_ref[:, j * hd:(j + 1) * hd] = _rot(dq4[j * b:(j + 1) * b], tc, -1.0).astype(dq_ref.dtype)
            dk_prev = ck[...] + _dot(dsp, q4, _TN) * scale
            dv_prev = cv[...] + _dot(pp.astype(BF16), do4, _TN)
            dk_ref[...] = _rot(dk_prev, tp, -1.0).astype(dk_ref.dtype)
            dv_ref[...] = dv_prev.astype(dv_ref.dtype)
            ck[...] = _dot(dsc, q4, _TN) * scale
            cv[...] = _dot(pc.astype(BF16), do4, _TN)

        @pl.when(n == nb)
        def _():
            dk_ref[...] = _rot(ck[...], tp, -1.0).astype(dk_ref.dtype)
            dv_ref[...] = cv[...].astype(dv_ref.dtype)

    cur = lambda n: jnp.minimum(n, nb - 1)
    prv = lambda n: jnp.maximum(n - 1, 0)
    q_spec, kc_spec, kp_spec, tc_spec, tp_spec = _att_specs(nb, rep, cur, prv)
    return pl.pallas_call(
        body, name=name, grid=(n_kv, nb + 1),
        in_specs=[q_spec, q_spec, q_spec, q_spec, kc_spec, kp_spec, kc_spec, kp_spec,
                  tc_spec, tc_spec, tc_spec, tp_spec, tp_spec, tp_spec],
        out_specs=[q_spec, kp_spec, kp_spec],
        out_shape=[jax.ShapeDtypeStruct(q.shape, BF16), jax.ShapeDtypeStruct(k.shape, BF16),
                   jax.ShapeDtypeStruct(k.shape, BF16)],
        scratch_shapes=[pltpu.VMEM((b, hd), F32), pltpu.VMEM((b, hd), F32)],
        compiler_params=_params(("parallel", "arbitrary")))(q, do, lse, delta, k, k, v, v, *tabs, *tabs)


def _rot_heads(x, tabs, width, sign, out_dtype, name):
    s = x.shape[0]
    hd = ATT_HEAD_DIM
    tile = min(512, s)

    def body(x_ref, cf, s1, s2, o_ref):
        t = (cf[...], s1[...], s2[...])
        for j in range(width // hd):
            o_ref[:, j * hd:(j + 1) * hd] = _rot(x_ref[:, j * hd:(j + 1) * hd], t, sign).astype(o_ref.dtype)

    tab = pl.BlockSpec((tile, hd), lambda i: (i, 0))
    return pl.pallas_call(
        body, name=name, grid=(s // tile,), in_specs=[pl.BlockSpec((tile, width), lambda i: (i, 0)), tab, tab, tab],
        out_specs=pl.BlockSpec((tile, width), lambda i: (i, 0)), out_shape=jax.ShapeDtypeStruct((s, width), out_dtype),
        compiler_params=_params(("parallel",)))(x, *tabs)


def _kv_grad(dk_rot, dv, tabs, name):
    s, width = dk_rot.shape
    hd = ATT_HEAD_DIM
    tile = min(512, s)

    def body(k_ref, v_ref, cf, s1, s2, o_ref):
        t = (cf[...], s1[...], s2[...])
        for j in range(width // hd):
            o_ref[:, j * hd:(j + 1) * hd] = _rot(k_ref[:, j * hd:(j + 1) * hd], t, -1.0).astype(o_ref.dtype)
        o_ref[:, width:] = v_ref[...].astype(o_ref.dtype)

    tab = pl.BlockSpec((tile, hd), lambda i: (i, 0))
    half = pl.BlockSpec((tile, width), lambda i: (i, 0))
    return pl.pallas_call(
        body, name=name, grid=(s // tile,), in_specs=[half, half, tab, tab, tab],
        out_specs=pl.BlockSpec((tile, 2 * width), lambda i: (i, 0)),
        out_shape=jax.ShapeDtypeStruct((s, 2 * width), BF16), compiler_params=_params(("parallel",)))(dk_rot, dv, *tabs)


def _rows_of(r, dil):
    return pl.ds(r, ATT_BLOCK, stride=dil) if dil > 1 else slice(None)


def _nat_specs(g, dil, n_kv_all, cur, prv):
    b, hd = ATT_BLOCK * dil, ATT_HEAD_DIM
    n_kv = ATT_KV_HEADS_PER_GROUP
    rep = ATT_HEADS_PER_GROUP // n_kv
    q_all = [pl.BlockSpec((b, hd), lambda h, n, j=j: (cur(n), (g * n_kv + h) * rep + j)) for j in range(rep)]
    q_own = [pl.BlockSpec((b, hd), lambda h, n, j=j: (cur(n), h * rep + j)) for j in range(rep)]
    hm_all = pl.BlockSpec((rep, b, hd), lambda h, n: (g * n_kv + h, cur(n), 0))
    hm_own = pl.BlockSpec((rep, b, hd), lambda h, n: (h, cur(n), 0))
    kc =pl.BlockSpec((b, hd), lambda h, n: (cur(n), g * n_kv + h))
    kp = pl.BlockSpec((b, hd), lambda h, n: (prv(n), g * n_kv + h))
    vc = pl.BlockSpec((b, hd), lambda h, n: (cur(n), n_kv_all + g * n_kv + h))
    vp = pl.BlockSpec((b, hd), lambda h, n: (prv(n), n_kv_all + g * n_kv + h))
    tab = pl.BlockSpec((b, hd), lambda h, n: (cur(n), 0))
    stat = pl.BlockSpec((None, b, LANES), lambda h, n: (h, cur(n), 0))
    return q_all, q_own, hm_all, hm_own, kc, kp, vc, vp, tab, stat


def _head_cols(stat, rep):
    return jnp.concatenate([jnp.broadcast_to(stat[:, j:j + 1], stat.shape) for j in range(rep)], axis=0)


def _attn_fwd_nat(q_all, k_rot, kv, tabs, g, dil, name):
    s = q_all.shape[0]
    b, hd = ATT_BLOCK, ATT_HEAD_DIM
    nbn = s // (b * dil)
    n_kv = ATT_KV_HEADS_PER_GROUP
    rep = ATT_HEADS_PER_GROUP // n_kv
    n_kv_all = k_rot.shape[1] // hd
    scale = hd ** -0.5

    def body(*refs):
        q_refs = refs[:rep]
        kc_ref, kp_ref, vc_ref, vp_ref, cf, s1, s2, o_ref, lse_ref = refs[rep:]
        mc, mp = _att_masks(jnp.where(pl.program_id(1) > 0, 1, 0), 2, rep)
        for r in range(dil):
            sl = _rows_of(r, dil)
            tc = (cf[sl, :], s1[sl, :], s2[sl, :])
            q4 = jnp.concatenate([_rot(q_ref[sl, :], tc, 1.0) for q_ref in q_refs], axis=0).astype(BF16)
            kc, kp = kc_ref[sl, :].astype(BF16), kp_ref[sl, :].astype(BF16)
            sc = jnp.where(mc, _dot(q4, kc, _NT) * scale, NEG)
            sp = jnp.where(mp, _dot(q4, kp, _NT) * scale, NEG)
            m = jnp.maximum(jnp.max(sc, axis=1, keepdims=True), jnp.max(sp, axis=1, keepdims=True))
            pc, pp = jnp.exp(sc - m), jnp.exp(sp - m)
            l = jnp.sum(pc, axis=1, keepdims=True) + jnp.sum(pp, axis=1, keepdims=True)
            o = (_dot(pc.astype(BF16), vc_ref[sl, :].astype(BF16))
                 + _dot(pp.astype(BF16), vp_ref[sl, :].astype(BF16))) / l
            lse = m + jnp.log(l)
            for j in range(rep):
                o_ref[j, sl, :] = o[j * b:(j + 1) * b]
            lse_ref[sl, :] = _lane_place([lse[j * b:(j + 1) * b] for j in range(rep)])

    cur = lambda n: n
    prv = lambda n: jnp.maximum(n - 1, 0)
    q_specs, _, _, hm_own, kc, kp, vc, vp, tab, stat = _nat_specs(g, dil, n_kv_all, cur, prv)
    return pl.pallas_call(
        body, name=name, grid=(n_kv, nbn), in_specs=[*q_specs, kc, kp, vc, vp, tab, tab, tab], out_specs=[hm_own, stat],
        out_shape=[jax.ShapeDtypeStruct((ATT_HEADS_PER_GROUP, s, hd), F32), jax.ShapeDtypeStruct((n_kv, s, LANES), F32)],
        compiler_params=_params(("parallel", "arbitrary")))(*([q_all] * rep), k_rot, k_rot, kv, kv, *tabs)


def _attn_bwd_nat(q_all, k_rot, kv, do, lse, delta, tabs, grads, g, dil, name):
    s = q_all.shape[0]
    b, hd = ATT_BLOCK, ATT_HEAD_DIM
    nbn = s // (b * dil)
    n_kv = ATT_KV_HEADS_PER_GROUP
    rep = ATT_HEADS_PER_GROUP // n_kv
    n_kv_all = k_rot.shape[1] // hd
    scale = hd ** -0.5

    def body(*refs):
        q_refs, do_refs = refs[:rep], refs[rep:2 * rep]
        (lse_ref, dl_ref, kc_ref, kp_ref, vc_ref, vp_ref, cf, s1, s2, _, _, _,
         dq_ref, dk_ref, dv_ref, ck, cv) = refs[2 * rep:]
        n = pl.program_id(1)

        @pl.when(n == 0)
        def _():
            ck[...] = jnp.zeros(ck.shape, F32)
            cv[...] = jnp.zeros(cv.shape, F32)

        @pl.when(n < nbn)
        def _():
            mc, mp = _att_masks(jnp.where(n > 0, 1, 0), 2, rep)
            for r in range(dil):
                sl = _rows_of(r, dil)
                own = slice(r * b, (r + 1) * b)
                tc = (cf[sl, :], s1[sl, :], s2[sl, :])
                q4 = jnp.concatenate([_rot(q_ref[sl, :], tc, 1.0) for q_ref in q_refs], axis=0).astype(BF16)
                do4 = jnp.concatenate([do_ref[sl, :] for do_ref in do_refs], axis=0).astype(BF16)
                lse4 = _head_cols(lse_ref[sl, :], rep)
                dl4 = _head_cols(dl_ref[sl, :], rep)
                kc, kp = kc_ref[sl, :].astype(BF16), kp_ref[sl, :].astype(BF16)
                vc, vp = vc_ref[sl, :].astype(BF16), vp_ref[sl, :].astype(BF16)
                pc = jnp.where(mc, jnp.exp(_dot(q4, kc, _NT) * scale - lse4), 0.0)
                pp = jnp.where(mp, jnp.exp(_dot(q4, kp, _NT) * scale - lse4), 0.0)
                dsc = (pc * (_dot(do4, vc, _NT) - dl4)).astype(BF16)
                dsp = (pp * (_dot(do4, vp, _NT) - dl4)).astype(BF16)
                dq4 = (_dot(dsc, kc) + _dot(dsp, kp)) * scale
                for j in range(rep):
                    dq_ref[j, sl, :] = _rot(dq4[j * b:(j + 1) * b], tc, -1.0)
                dk_ref[sl, :] = ck[own, :] + _dot(dsp, q4, _TN) * scale
                dv_ref[sl, :] = cv[own, :] + _dot(pp.astype(BF16), do4, _TN)
                ck[own, :] = _dot(dsc, q4, _TN) * scale
                cv[own, :] = _dot(pc.astype(BF16), do4, _TN)

        @pl.when(n == nbn)
        def _():
            for r in range(dil):
                sl = _rows_of(r, dil)
                dk_ref[sl, :] = ck[r * b:(r + 1) * b, :]
                dv_ref[sl, :] = cv[r * b:(r + 1) * b, :]

    cur = lambda n: jnp.minimum(n, nbn - 1)
    prv = lambda n: jnp.maximum(n - 1, 0)
    q_specs, do_specs, hm_all, _, kc, kp, vc, vp, tab, stat = _nat_specs(g, dil, n_kv_all, cur, prv)
    anyspace = pl.BlockSpec(memory_space=pl.ANY)
    n_in = 2 * rep + 9
    return pl.pallas_call(
        body, name=name, grid=(n_kv, nbn + 1),
        in_specs=[*q_specs, *do_specs, stat, stat, kc, kp, vc, vp, tab, tab, tab, anyspace, anyspace, anyspace],
        out_specs=[hm_all, kp, kp], out_shape=[jax.ShapeDtypeStruct(a.shape, a.dtype) for a in grads],
        input_output_aliases={n_in: 0, n_in + 1: 1, n_in + 2: 2},
        scratch_shapes=[pltpu.VMEM((dil * b, hd), F32), pltpu.VMEM((dil * b, hd), F32)],
        compiler_params=_params(("parallel", "arbitrary"), VMEM_LIMIT_ATTN_BWD_BYTES))(
            *([q_all] * rep), *([do] * rep), lse, delta, k_rot, k_rot, kv, kv, *tabs, *grads)


def _adamw(g_slabs, w, m, v, name, row0=0, prior=None):
    kk, r, c = g_slabs.shape
    tile = r if r <= 256 else _pick_rows(r, 256)
    off = row0 // tile
    assert off * tile == row0

    def body(g_ref, w_ref, m_ref, v_ref, *rest):
        go_ref, d_ref, mo_ref, vo_ref = rest[-4:]
        g = g_ref[0].astype(F32)
        for k in range(1, kk):
            g = g + g_ref[k].astype(F32)
        m2 = ADAM_B1 * m_ref[...] + (1.0 - ADAM_B1) * g
        v2 = ADAM_B2 * v_ref[...] + (1.0 - ADAM_B2) * jnp.square(g)
        m_hat = m2 / (1.0 - ADAM_B1 ** ADAM_STEP)
        v_hat = v2 / (1.0 - ADAM_B2 ** ADAM_STEP)
        go_ref[...] = g
        d_ref[...] = -ADAM_LR * (m_hat / (jnp.sqrt(v_hat) + ADAM_EPS) + ADAM_WD * w_ref[...])
        mo_ref[...] = m2
        vo_ref[...] = v2

    spec = pl.BlockSpec((tile, c), lambda i: (i + off, 0))
    prior = list(prior) if prior is not None else []
    return pl.pallas_call(
        body, name=name, grid=(r // tile,),
        in_specs=[pl.BlockSpec((kk, tile, c), lambda i: (0, i, 0)), spec, spec, spec]
        + [pl.BlockSpec(memory_space=pl.ANY)] * len(prior),
        out_specs=[spec] * 4, out_shape=[jax.ShapeDtypeStruct(w.shape, F32)] * 4,
        input_output_aliases={4 + i: i for i in range(len(prior))},
        compiler_params=_params(("parallel",)))(g_slabs, w, m, v, *prior)


def _pick_rows(r, pref):
    t = (pref // 16) * 16
    while t >= 16:
        if r % t == 0:
            return t
        t -= 16
    return r


def _coords():
    return lax.axis_index("x"), lax.axis_index("y"), lax.axis_index("c")


def _dev_index(px, py, pc):
    return 4 * px + 2 * py + pc


def _all_gather(shards, name):
    na = len(shards)

    def body(*refs):
        ins, outs = refs[:na], refs[na:2 * na]
        send_sems, recv_sems, local_sems = refs[2 * na:]
        x, y, c = _coords()
        me, sibling = (x, y, c), (x, y, 1 - c)
        chips = [(1 - x, y), (x, 1 - y), (1 - x, 1 - y)]

        def copy(a, k, block, to, src=None):
            dst = outs[a].at[_dev_index(*block)]
            return pltpu.make_async_remote_copy(
                src_ref=dst if src is None else src, dst_ref=dst, send_sem=send_sems.at[a * 7 + k],
                recv_sem=recv_sems.at[a * 7 + k], device_id=to, device_id_type=MESH)

        mine = [pltpu.make_async_copy(ins[a], outs[a].at[_dev_index(*me)], local_sems.at[a]) for a in range(na)]
        for cp in mine:
            cp.start()
        first = []
        for a in range(na):
            first.append(copy(a, 0, me, sibling, src=ins[a]))
            first += [copy(a, 1 + j, me, (*chip, c), src=ins[a]) for j, chip in enumerate(chips)]
        for cp in first:
            cp.start()
        passed = []
        for j, chip in enumerate(chips):
            for a in range(na):
                copy(a, 1 + j, (*chip, c), me).wait_recv()
                cp = copy(a, 4 + j, (*chip, c), sibling)
                cp.start()
                passed.append(cp)
        for a in range(na):
            copy(a, 0, sibling, me).wait_recv()
            for j, chip in enumerate(chips):
                copy(a, 4 + j, (*chip, 1 - c), me).wait_recv()
        for cp in first + passed:
            cp.wait_send()
        for cp in mine:
            cp.wait()

    hbm = pl.BlockSpec(memory_space=pl.ANY)
    return pl.pallas_call(
        body, name=name, in_specs=[hbm] * na, out_specs=[hbm] * na,
        out_shape=[jax.ShapeDtypeStruct((NDEV,) + s.shape, s.dtype) for s in shards],
        scratch_shapes=[pltpu.SemaphoreType.DMA((7 * na,)), pltpu.SemaphoreType.DMA((7 * na,)),
                        pltpu.SemaphoreType.DMA((na,))])(*shards)


def _exchange(slabs, whole, name, after=()):
    ns, nw = len(slabs), len(whole)
    na = ns + nw
    nb = len(after)

    def body(*refs):
        ins, outs = refs[:na], refs[na + nb:2 * na + nb]
        send_sems, recv_sems, local_sems = refs[2 * na + nb:]
        x, y, c = _coords()
        me = _dev_index(x, y, c)

        def src_of(a, p):
            return ins[a].at[p] if a < ns else ins[a]

        def copy(a, k, peer):
            p = _dev_index(*peer)
            return pltpu.make_async_remote_copy(
                src_ref=src_of(a, p), dst_ref=outs[a].at[me], send_sem=send_sems.at[a * 7 + k - 1],
                recv_sem=recv_sems.at[a * 7 + k - 1], device_id=peer, device_id_type=MESH)

        def arrival(a, k, peer):
            p = _dev_index(*peer)
            return pltpu.make_async_remote_copy(
                src_ref=src_of(a, p), dst_ref=outs[a].at[p], send_sem=send_sems.at[a * 7 + k - 1],
                recv_sem=recv_sems.at[a * 7 + k - 1], device_id=peer, device_id_type=MESH)

        mine = [pltpu.make_async_copy(src_of(a, me), outs[a].at[me], local_sems.at[a]) for a in range(na)]
        for cp in mine:
            cp.start()
        peers = [(k, (x ^ (k >> 2), y ^ ((k >> 1) & 1), c ^ (k & 1))) for k in range(1, NDEV)]
        sent = [copy(a, k, peer) for k, peer in peers for a in range(na)]
        for cp in sent:
            cp.start()
        for k, peer in peers:
            for a in range(na):
                arrival(a, k, peer).wait_recv()
        for cp in sent:
            cp.wait_send()
        for cp in mine:
            cp.wait()

    hbm = pl.BlockSpec(memory_space=pl.ANY)
    out_shape = [jax.ShapeDtypeStruct(s.shape, s.dtype) for s in slabs]
    out_shape += [jax.ShapeDtypeStruct((NDEV,) + w.shape, w.dtype) for w in whole]
    return pl.pallas_call(
        body, name=name, in_specs=[hbm] * (na + nb), out_specs=[hbm] * na, out_shape=out_shape,
        scratch_shapes=[pltpu.SemaphoreType.DMA((7 * na,)), pltpu.SemaphoreType.DMA((7 * na,)),
                        pltpu.SemaphoreType.DMA((na,))])(*slabs, *whole, *after)


_HBM = pl.BlockSpec(memory_space=pltpu.HBM)
_SEM = pl.BlockSpec(memory_space=pltpu.SEMAPHORE)
_EFFECT = pltpu.SideEffectType.DATAFLOW_SIDE_EFFECTING


def _peers(x, y, c):
    return [(k, (x ^ (k >> 2), y ^ ((k >> 1) & 1), c ^ (k & 1))) for k in range(1, NDEV)]


def _peer_copy(src, land, send_sems, recv_sems, a, k, dst_block, peer):
    return pltpu.make_async_remote_copy(
        src_ref=src, dst_ref=land.at[dst_block], send_sem=send_sems.at[a * 7 + k - 1],
        recv_sem=recv_sems.at[a * 7 + k - 1], device_id=peer, device_id_type=MESH)


def _send_start(arrays, slabs, name):
    na = len(arrays)
    lands = [jax.ShapeDtypeStruct(a.shape if slabs else (NDEV,) + a.shape, a.dtype) for a in arrays]

    def body(*refs):
        ins, zones = refs[:na], refs[na:2 * na]
        send_sems, recv_sems = refs[2 * na], refs[2 * na + 1]
        token = refs[-1]
        x, y, c = _coords()
        me = _dev_index(x, y, c)
        for k, peer in _peers(x, y, c):
            for a in range(na):
                src = ins[a].at[_dev_index(*peer)] if slabs else ins[a]
                _peer_copy(src, zones[a], send_sems, recv_sems, a, k, me, peer).start()
        token[...] = jnp.zeros_like(token)

    outs = pl.pallas_call(
        body, name=name,
        out_shape=(pltpu.SemaphoreType.DMA((7 * na,)), pltpu.SemaphoreType.DMA((7 * na,)),
                   *[pltpu.HBM(a.shape, a.dtype) for a in arrays], *[pltpu.HBM(l.shape, l.dtype) for l in lands],
                   jax.ShapeDtypeStruct((8, LANES), F32)),
        in_specs=[_HBM] * (2 * na), out_specs=(_SEM, _SEM, *([_HBM] * (2 * na)), pl.BlockSpec(memory_space=pltpu.VMEM)),
        input_output_aliases={i: 2 + i for i in range(2 * na)},
        compiler_params=pltpu.CompilerParams(has_side_effects=_EFFECT),
    )(*[pltpu.with_memory_space_constraint(a, pltpu.HBM) for a in arrays],
      *[pltpu.with_memory_space_constraint(lax.empty(l.shape, l.dtype), pltpu.HBM) for l in lands])
    return outs[0], outs[1], list(outs[2:2 + na]), list(outs[2 + na:2 + 2 * na]), outs[-1]


def _send_wait(started, after, slabs, name):
    send_sems, recv_sems, thru, zones, _ = started
    na = len(thru)

    def body(*refs):
        ins, lands = refs[:na], refs[na:2 * na]
        s_sems, r_sems = refs[2 * na], refs[2 * na + 1]
        x, y, c = _coords()
        for k, peer in _peers(x, y, c):
            p = _dev_index(*peer)
            for a in range(na):
                src = ins[a].at[p] if slabs else ins[a]
                cp = _peer_copy(src, lands[a], s_sems, r_sems, a, k, p, peer)
                cp.wait_send()
                cp.wait_recv()

    outs = pl.pallas_call(
        body, name=name, out_shape=tuple(pltpu.HBM(v.shape, v.dtype) for v in thru + zones),
        in_specs=[_HBM] * (2 * na) + [_SEM, _SEM, pl.BlockSpec(memory_space=pl.ANY)], out_specs=tuple([_HBM] * (2 * na)),
        input_output_aliases={i: i for i in range(2 * na)},
        compiler_params=pltpu.CompilerParams(has_side_effects=_EFFECT),
    )(*thru, *zones, send_sems, recv_sems, after)
    me = _dev_index(*_coords())
    filled = []
    for a in range(na):
        own = lax.dynamic_index_in_dim(outs[a], me, 0, keepdims=False) if slabs else outs[a]
        filled.append(lax.dynamic_update_index_in_dim(outs[na + a], own, me, 0))
    return filled


def _pack(vecs):
    parts, spans, off = [], [], 0
    for v in vecs:
        n = v.size
        pad = (-n) % LANES
        parts.append(jnp.pad(v.reshape(-1).astype(F32), (0, pad)))
        spans.append((off, n))
        off += n + pad
    return jnp.concatenate(parts).reshape(-1, LANES), spans


def _pad_lanes(v):
    v = v.reshape(1, -1)
    return jnp.pad(v, ((0, 0), (0, LANES - v.shape[1])))


def _cols_to_slabs(g):
    sh = g.shape
    g = g.reshape(sh[:-1] + (NDEV, sh[-1] // NDEV))
    return jnp.moveaxis(g, -2, 0)


def _rows_to_slabs(g):
    sh = g.shape
    g = g.reshape(sh[:-2] + (NDEV, sh[-2] // NDEV, sh[-1]))
    return jnp.moveaxis(g, -3, 0)


def _slabs_to_cols(a):
    a = jnp.moveaxis(a, 0, -2)
    return a.reshape(a.shape[:-2] + (a.shape[-2] * a.shape[-1],))


def _slabs_to_rows(a):
    a = jnp.moveaxis(a, 0, -3)
    return a.reshape(a.shape[:-3] + (a.shape[-3] * a.shape[-2], a.shape[-1]))


def _ffn_forward2(x, norm_w, w_up, fcw, wdown, tag):
    u, h = _mm(x, w_up, rms_fwd=norm_w, name=f"{tag}_up")
    f = _ffn_gate_fwd2(u, fcw, f"{tag}_gate")
    return _mm(f, wdown, res=x, name=f"{tag}_down"), (h, u, f)


def _ffn_backward2(x, saved, dout, dout_b, norm_w, w_up, fcw, wdown, tag):
    h, u, f = saved
    dwdown = _mm(f, dout_b, ta=True, name=f"{tag}_dwdown")
    df = _mm(dout_b, wdown, tb=True, name=f"{tag}_df")
    du, dfc = _ffn_gate_bwd2(u, fcw, df, f"{tag}_gate_bwd")
    dwup = _mm(h, du, ta=True, name=f"{tag}_dwup")
    dx, dxb, dnorm = _mm(du, w_up, tb=True, rms_bwd=(x, norm_w, dout), tm=RMS_BWD_ROWS, name=f"{tag}_dh")
    return dx, dxb, (dwup, jnp.concatenate([dfc[0], dfc[1]], axis=1), dwdown, dnorm)


def _ffn_forward(x, norm_w, wup_g, wup_v, cw_g, cw_v, wdown, tag):
    h = _rms_fwd(x, norm_w, f"{tag}_norm")
    ug = _mm(h, wup_g, name=f"{tag}_up_gate")
    uv = _mm(h, wup_v, name=f"{tag}_up_val")
    f = _ffn_gate_fwd(ug, uv, cw_g, cw_v, f"{tag}_gate")
    return _mm(f, wdown, res=x, name=f"{tag}_down"), (h, ug, uv, f)


def _ffn_backward(x, saved, dout, dout_b, norm_w, wup_g, wup_v, cw_g, cw_v, wdown, tag):
    h, ug, uv, f = saved
    dwdown = _mm(f, dout_b, ta=True, name=f"{tag}_dwdown")
    df = _mm(dout_b, wdown, tb=True, name=f"{tag}_df")
    dug, duv, dcg, dcv = _ffn_gate_bwd(ug, uv, cw_g, cw_v, df, f"{tag}_gate_bwd")
    dwg = _mm(h, dug, ta=True, name=f"{tag}_dwup_gate")
    dwv = _mm(h, duv, ta=True, name=f"{tag}_dwup_val")
    dh = _mm(dug, wup_g, tb=True, name=f"{tag}_dh_gate")
    dh = _mm(duv, wup_v, tb=True, res=dh, name=f"{tag}_dh_val")
    dx, dxb, dnorm = _rms_bwd(x, norm_w, dh, dout, f"{tag}_norm_bwd")
    return dx, dxb, (jnp.concatenate([dwg, dwv], axis=1), jnp.concatenate([dcg, dcv], axis=1), dwdown, dnorm)


def kernel(x, a_norm, ssm_w_in, ssm_conv_w, ssm_conv_b, ssm_dt_bias, ssm_a_log, ssm_d, ssm_norm, ssm_w_out, kv_norm, w_kv, b_norm, att_w_q, att_w_o, ffn_norm, ffn_w_up, ffn_conv_w, ffn_w_down, final_norm, loss_target, m_a_norm, m_ssm_w_in, m_ssm_conv_w, m_ssm_conv_b, m_ssm_dt_bias, m_ssm_a_log, m_ssm_d, m_ssm_norm, m_ssm_w_out, m_kv_norm, m_w_kv, m_b_norm, m_att_w_q, m_att_w_o, m_ffn_norm, m_ffn_w_up, m_ffn_conv_w, m_ffn_w_down, m_final_norm, v_a_norm, v_ssm_w_in, v_ssm_conv_w, v_ssm_conv_b, v_ssm_dt_bias, v_ssm_a_log, v_ssm_d, v_ssm_norm, v_ssm_w_out, v_kv_norm, v_w_kv, v_b_norm, v_att_w_q, v_att_w_o, v_ffn_norm, v_ffn_w_up, v_ffn_conv_w, v_ffn_w_down, v_final_norm):
    given = dict(locals())
    xs, tgt = x[0], loss_target[0]
    s, d = xs.shape
    di = ssm_w_out.shape[1] * NDEV
    nh = ssm_dt_bias.shape[1]
    ng = SSM_N_GROUPS
    convd = di + 2 * ng * SSM_D_STATE
    f = ffn_w_down.shape[1] * NDEV
    n_att = len(ATT_PATTERNS)
    qg = ATT_HEADS_PER_GROUP * ATT_HEAD_DIM
    kg = ATT_KV_HEADS_PER_GROUP * ATT_HEAD_DIM
    kvd = n_att * kg
    assert all(w // dil == ATT_BLOCK for w, dil in ATT_PATTERNS)

    small, _ = _pack([a_norm, ssm_conv_w, ssm_conv_b, ssm_norm, ffn_conv_w])
    gat = _all_gather([ssm_w_in[0].astype(BF16), small], "gather_weights")
    first = _send_start([ssm_w_out[0].astype(BF16), ffn_w_up[0].astype(BF16), ffn_w_down[0].astype(BF16)], False,
                        "gather_ffn0_start")
    rest = _send_start([b.astype(BF16) for b in (w_kv, att_w_q[0], att_w_o[0], ffn_w_up[1], ffn_w_down[1])], False,
                       "gather_rest_start")
    w_in = _slabs_to_cols(gat[0])
    in_dim = di + convd + nh
    in_pad = di + convd + LANES
    w_in = jnp.pad(w_in, ((0, 0), (0, in_pad - in_dim)))
    sm = gat[1].reshape(NDEV, -1)
    o0 = 0

    def take(shape):
        nonlocal o0
        n = math.prod(shape)
        out = sm[:, o0:o0 + n].reshape((NDEV,) + shape)
        o0 += n + (-n) % LANES
        return out
    a_norm_f = _slabs_to_cols(take(a_norm.shape)) + (first[-1][0, 0] + rest[-1][0, 0])
    conv_w_f = _slabs_to_cols(take(ssm_conv_w.shape))[0]
    conv_b_f = _slabs_to_cols(take(ssm_conv_b.shape))
    ssm_norm_f = _slabs_to_cols(take(ssm_norm.shape))
    fcw = _slabs_to_cols(take(ffn_conv_w.shape))
    dtb, alog, dsk = _pad_lanes(ssm_dt_bias), _pad_lanes(ssm_a_log), _pad_lanes(ssm_d)
    kvn, fin = kv_norm.reshape(1, d), final_norm.reshape(1, d)

    zx, h0 = _mm(xs, w_in, rms_fwd=a_norm_f, name="in_proj")
    z, dtr = (zx, di, 0), (zx, LANES, (di + convd) // LANES)
    xbc = _conv_silu_fwd(zx, di, conv_w_f, conv_b_f, "ssm_conv")
    dt = _softplus_fwd(dtr, dtb, "ssm_dt")
    y, prevs = _ssd_fwd2(xbc, dt, alog, dsk, di, nh, ng, "ssd")
    yn = _gnorm_fwd(y, z, ssm_norm_f, ng, "ssm_gnorm")
    got = _send_wait(first, yn, False, "gather_ffn0_wait")
    w_out = _slabs_to_rows(got[0])
    w_up0, w_down0 = _slabs_to_cols(got[1]), _slabs_to_rows(got[2])
    x1 = _mm(yn, w_out, res=xs, name="ssm_out")
    x2, ffn0 = _ffn_forward2(x1, ffn_norm[0:1], w_up0, fcw[0], w_down0, "ffn0")
    got = _send_wait(rest, x2, False, "gather_rest_wait")
    w_kvf = _slabs_to_cols(got[0])
    w_q = _slabs_to_cols(got[1])
    w_o = _slabs_to_rows(got[2])
    w_up1, w_down1 = _slabs_to_cols(got[3]), _slabs_to_rows(got[4])
    kv, hk = _mm(x2, w_kvf, rms_fwd=kvn, name="kv_proj")
    q, h2 = _mm(x2, w_q, rms_fwd=b_norm, name="q_proj")
    tabs = _rot_tables(s, 1)
    k_rot = _rot_heads(kv, tabs, kvd, 1.0, F32, "k_rot")
    att = [_attn_fwd_nat(q, k_rot, kv, tabs, g, dil, f"attn{g}") for g, (_, dil) in enumerate(ATT_PATTERNS)]
    o, ob, lse = _merge_heads([t[0] for t in att], [t[1] for t in att], "attn_merge")
    x3 = _mm(ob, w_o, res=x2, name="attn_out")
    x4, ffn1 = _ffn_forward2(x3, ffn_norm[1:2], w_up1, fcw[1], w_down1, "ffn1")
    loss_part, dx4, dx4b, dfin = _final_loss(x4, fin, tgt, "loss_head")

    dx3, dx3b, (dwup1, dfc1, dwdown1, dfn1) = _ffn_backward2(
        x3, ffn1, dx4, dx4b, ffn_norm[1:2], w_up1, fcw[1], w_down1, "ffn1")
    dw_o = _mm(ob, dx3b, ta=True, name="attn_dwo")
    do = _mm(dx3b, w_o, tb=True, name="attn_do")
    delta = _delta_heads(do, o, "attn_delta")
    grads = (lax.empty((n_att * qg // LANES, s, LANES), F32), lax.empty((s, kvd), F32), lax.empty((s, kvd), F32))
    for g, (_, dil) in enumerate(ATT_PATTERNS):
        grads = _attn_bwd_nat(q, k_rot, kv, do, lse, delta, tabs, grads, g, dil, f"attn{g}_bwd")
    dq, dk_rot, dv = grads
    dkv = _kv_grad(dk_rot, dv, tabs, "kv_grad")
    dw_q = _mm(h2, dq, ta=True, b_heads=True, name="q_dw")
    dx2, _, db_norm = _mm(dq, w_q, tb=True, a_heads=True, rms_bwd=(x2, b_norm, dx3), tm=RMS_BWD_ROWS, name="q_dh")
    dw_kv = _mm(hk, dkv, ta=True, name="kv_dw")
    dx2, dx2b, dkv_norm = _mm(dkv, w_kvf, tb=True, rms_bwd=(x2, kvn, dx2), tm=RMS_BWD_ROWS, name="kv_dh")
    sent1 = _send_start([_cols_to_slabs(dwup1).astype(BF16), _rows_to_slabs(dwdown1).astype(BF16),
                         _cols_to_slabs(dw_kv).astype(BF16), _cols_to_slabs(dw_q).astype(BF16),
                         _rows_to_slabs(dw_o).astype(BF16)], True, "grads_late_start")
    dx1, dx1b, (dwup0, dfc0, dwdown0, dfn0) = _ffn_backward2(
        x1, ffn0, dx2, dx2b, ffn_norm[0:1], w_up0, fcw[0] + sent1[-1][0, 0], w_down0, "ffn0")
    dw_out = _mm(yn, dx1b, ta=True, name="ssm_dwout")
    sent0 = _send_start([_cols_to_slabs(dwup0).astype(BF16), _rows_to_slabs(dwdown0).astype(BF16),
                         _rows_to_slabs(dw_out).astype(BF16)], True, "grads_ffn0_start")
    dyn = _mm(dx1b, w_out, tb=True, name="ssm_dyn")
    dzx = lax.empty((s, in_pad), BF16)
    dy, dzx, dssm_norm = _gnorm_bwd(dyn, y, z, ssm_norm_f + sent0[-1][0, 0], ng, (dzx, 0), "ssm_gnorm_bwd")
    dxbc, ddt, dalog, ddsk = _ssd_bwd2(xbc, dt, alog, dsk, prevs, dy, di, nh, ng, "ssd_bwd")
    dzx, ddtb = _softplus_bwd(ddt, dtr, dtb, nh, (dzx, (di + convd) // LANES), "ssm_dt_bwd")
    dzx, dconv_w, dconv_b = _conv_silu_bwd(zx, di, conv_w_f, conv_b_f, dxbc, dzx, "ssm_conv_bwd")
    dw_in = _mm(h0, dzx, ta=True, name="in_dw")
    sent_m = _send_start([_cols_to_slabs(dw_in[:, :in_dim])], True, "grads_mamba_start")
    dx0, _, da_norm = _mm(dzx, w_in, tb=True, rms_bwd=(xs, a_norm_f + sent_m[-1][0, 0], dx1), tm=RMS_BWD_ROWS,
                          name="in_dh")

    small_full = {
        'a_norm': da_norm, 'ssm_conv_w': dconv_w[None], 'ssm_conv_b': dconv_b, 'ssm_dt_bias': ddtb[:, :nh],
        'ssm_a_log': dalog[:, :nh], 'ssm_d': ddsk[:, :nh], 'ssm_norm': dssm_norm, 'kv_norm': dkv_norm.reshape(d),
        'b_norm': db_norm, 'ffn_norm': jnp.concatenate([dfn0, dfn1], axis=0), 'ffn_conv_w': jnp.stack([dfc0, dfc1]),
        'final_norm': dfin.reshape(d),
    }
    small_names = list(small_full)
    packed, spans = _pack([small_full[n] for n in small_names])
    got1 = _send_wait(sent1, dx0, True, "grads_late_wait")
    got0 = _send_wait(sent0, dx0, True, "grads_ffn0_wait")
    recv_big = {'w_kv': [got1[2]], 'att_w_q': [got1[3]], 'att_w_o': [got1[4]], 'ffn_w_up': [got0[0], got1[0]],
                'ffn_w_down': [got0[1], got1[1]]}

    me = _dev_index(*_coords())
    res = {}

    def update_big(n, layers):
        w = given[n]
        c = w.shape[-1]
        outs, row0 = None, 0
        for k, r in enumerate(layers):
            g = r.reshape(NDEV, -1, c)
            outs = _adamw(g, w.reshape(-1, c), given['m_' + n].reshape(-1, c), given['v_' + n].reshape(-1, c),
                          f"adamw_{n}_{k}", row0=row0, prior=outs)
            row0 += g.shape[1]
        res[n] = [o_.reshape(w.shape) for o_ in outs]
    for n, r in recv_big.items():
        update_big(n, r)
    update_big('ssm_w_out', [got0[2]])
    recv = _exchange([], [packed], "exchange_grads", after=[res[n][1] for n in res])
    small_sum = _sum_slabs(recv[-1], "sum_small_grads").reshape(-1)
    gotm = _send_wait(sent_m, recv[-1], True, "grads_mamba_wait")
    update_big('ssm_w_in', [gotm[0]])
    sharded_small = {'a_norm', 'ssm_conv_w', 'ssm_conv_b', 'ssm_norm', 'ffn_conv_w'}
    for n, (off, size) in zip(small_names, spans):
        w = given[n]
        gfull = small_sum[off:off + size].reshape(small_full[n].shape)
        if n in sharded_small:
            c = w.shape[-1]
            gfull = lax.dynamic_slice_in_dim(gfull, me * c, c, axis=gfull.ndim - 1)
        c = w.shape[-1]
        outs = _adamw(gfull.reshape(1, -1, c), w.reshape(-1, c), given['m_' + n].reshape(-1, c),
                      given['v_' + n].reshape(-1, c), f"adamw_{n}")
        res[n] = [o_.reshape(w.shape) for o_ in outs]

    loss = lax.psum(loss_part[0, 0], AXES)
    return (loss, dx0[None], *[res[n][0] for n in WEIGHTS], *[res[n][1] for n in WEIGHTS],
            *[res[n][2] for n in WEIGHTS], *[res[n][3] for n in WEIGHTS])
```

```python
import functools
import math

import jax
import jax.numpy as jnp
from jax import lax
from jax.experimental import pallas as pl
from jax.experimental.pallas import tpu as pltpu

F32, BF16 = jnp.float32, jnp.bfloat16
AXES = ("x", "y", "c")
NDEV = 8
MESH = pl.DeviceIdType.MESH
HIGHEST = lax.Precision.HIGHEST

LANES = 128
SUBLANES = 8
VMEM_LIMIT_BYTES = 48 * 1024 * 1024
VMEM_LIMIT_ATTN_BWD_BYTES = 58 * 1024 * 1024
RMS_BWD_ROWS = 512

RMS_EPS = 1e-6
GATED_NORM_EPS = 1e-5
SSM_HEAD_DIM = 64
SSM_N_GROUPS = 8
SSM_D_STATE = 128
SSM_CONV = 4
SSM_CHUNK = 128
ATT_PATTERNS = ((128, 1), (512, 4), (2048, 16))
ATT_HEAD_DIM = 128
ATT_HEADS_PER_GROUP = 8
ATT_KV_HEADS_PER_GROUP = 2
ATT_BLOCK = 128
ROPE_DIM = ATT_HEAD_DIM // 4
ROPE_THETA = 500000.0
FFN_CONV = 3
ADAM_LR = 0.001
ADAM_B1 = 0.9
ADAM_B2 = 0.999
ADAM_EPS = 1e-08
ADAM_WD = 0.01
ADAM_STEP = 10
NEG = -1e30

WEIGHTS = ['a_norm', 'ssm_w_in', 'ssm_conv_w', 'ssm_conv_b', 'ssm_dt_bias', 'ssm_a_log', 'ssm_d', 'ssm_norm',
           'ssm_w_out', 'kv_norm', 'w_kv', 'b_norm', 'att_w_q', 'att_w_o', 'ffn_norm', 'ffn_w_up', 'ffn_conv_w',
           'ffn_w_down', 'final_norm']


def _params(sem=None, vmem=VMEM_LIMIT_BYTES):
    kw = dict(vmem_limit_bytes=vmem)
    if sem is not None:
        kw["dimension_semantics"] = sem
    return pltpu.CompilerParams(**kw)


def _pick(n, pref):
    if n <= pref:
        return n
    t = (pref // LANES) * LANES
    while t >= LANES:
        if n % t == 0:
            return t
        t -= LANES
    return n


def _dot(a, b, dims=(((1,), (0,)), ((), ())), precision=None):
    return lax.dot_general(a, b, dims, precision=precision, preferred_element_type=F32)


_NT = (((1,), (1,)), ((), ()))
_TN = (((0,), (0,)), ((), ()))


def _mm(a, b, *, ta=False, tb=False, res=None, out_dtype=None, name, tm=1408, tn=1408, tk=2048,
        a_heads=False, b_heads=False, rms_bwd=None, rms_fwd=None):
    assert not (a_heads and ta) and not (b_heads and tb)
    a_parts = a.ndim == 3 and not a_heads
    b_parts = b.ndim == 3 and not b_heads
    assert not (a_parts and ta) and not (b_parts and tb)
    if out_dtype is None:
        out_dtype = BF16 if ta else F32
    if a_heads:
        m, k = a.shape[1], a.shape[0] * LANES
    elif a_parts:
        m, k = a.shape[1], a.shape[0] * a.shape[2]
    else:
        m = a.shape[1] if ta else a.shape[0]
        k = a.shape[0] if ta else a.shape[1]
    if b_heads:
        n, kb = b.shape[0] * LANES, b.shape[1]
    elif b_parts:
        n, kb = b.shape[0] * b.shape[2], b.shape[1]
    else:
        n = b.shape[0] if tb else b.shape[1]
        kb = b.shape[1] if tb else b.shape[0]
    assert k == kb
    tm = _pick(m, tm)
    tn = _pick(b.shape[2], tn) if b_parts else _pick(n, tn)
    tk = _pick(a.shape[2], tk) if a_parts else _pick(k, tk)
    nk = k // tk
    if a_heads:
        a_spec = pl.BlockSpec((tk // LANES, tm, LANES), lambda i, j, l: (l, i, 0))
    elif a_parts:
        per = a.shape[2] // tk
        a_spec = pl.BlockSpec((None, tm, tk), lambda i, j, l: (l // per, i, l % per))
    elif ta:
        a_spec = pl.BlockSpec((tk, tm), lambda i, j, l: (l, i))
    else:
        a_spec = pl.BlockSpec((tm, tk), lambda i, j, l: (i, l))
    if b_heads:
        b_spec = pl.BlockSpec((tn // LANES, tk, LANES), lambda i, j, l: (j, l, 0))
    elif b_parts:
        per_n = b.shape[2] // tn
        b_spec = pl.BlockSpec((None, tk, tn), lambda i, j, l: (j // per_n, l, j % per_n))
    elif tb:
        b_spec = pl.BlockSpec((tn, tk), lambda i, j, l: (j, l))
    else:
        b_spec = pl.BlockSpec((tk, tn), lambda i, j, l: (l, j))
    if rms_fwd is not None:
        assert nk == 1 and a.ndim == 2 and b.ndim == 2 and not (ta or tb or a_heads or b_heads)
        assert res is None and rms_bwd is None

        def nbody(a_ref, b_ref, w_ref, o_ref, h_ref, hs):
            @pl.when(pl.program_id(1) == 0)
            def _():
                xv = a_ref[...]
                rs = lax.rsqrt(jnp.mean(xv * xv, axis=-1, keepdims=True) + RMS_EPS)
                hv = (xv * rs * w_ref[...]).astype(BF16)
                hs[...] = hv
                h_ref[...] = hv

            o_ref[...] = _dot(hs[...], b_ref[...].astype(BF16)).astype(o_ref.dtype)

        rows = pl.BlockSpec((tm, k), lambda i, j: (i, 0))
        return pl.pallas_call(
            nbody, name=name, grid=(m // tm, n // tn),
            in_specs=[rows, pl.BlockSpec((k, tn), lambda i, j: (0, j)), pl.BlockSpec((1, k), lambda i, j: (0, 0))],
            out_specs=[pl.BlockSpec((tm, tn), lambda i, j: (i, j)), rows],
            out_shape=[jax.ShapeDtypeStruct((m, n), out_dtype), jax.ShapeDtypeStruct((m, k), BF16)],
            scratch_shapes=[pltpu.VMEM((tm, k), BF16)],
            compiler_params=_params(("parallel", "arbitrary")))(a, b, rms_fwd)
    o_spec = pl.BlockSpec((tm, tn), lambda i, j, l: (i, j))
    dims = (((0 if ta else 1,), (1 if tb else 0,)), ((), ()))
    has_res = res is not None
    has_rms = rms_bwd is not None
    assert not (has_res and has_rms) and (not has_rms or tn == n)
    n_extra = 3 if has_rms else int(has_res)
    n_out = 3 if has_rms else 1

    def load(ref, heads):
        if not heads:
            return ref[...].astype(BF16)
        return jnp.concatenate([ref[i].astype(BF16) for i in range(ref.shape[0])], axis=1)

    def body(*refs):
        a_ref, b_ref = refs[:2]
        extra = refs[2:2 + n_extra]
        outs = refs[2 + n_extra:2 + n_extra + n_out]
        p = _dot(load(a_ref, a_heads), load(b_ref, b_heads), dims)

        def finish(r):
            if has_res:
                r = r + extra[0][...]
            if not has_rms:
                outs[0][...] = r.astype(outs[0].dtype)
                return
            x_ref, w_ref, dres_ref = extra
            dx_ref, dxb_ref, dw_ref = outs
            xv = x_ref[...]
            rs = lax.rsqrt(jnp.mean(xv * xv, axis=-1, keepdims=True) + RMS_EPS)
            xh = xv * rs
            dxh = r * w_ref[...]
            dx = dres_ref[...] + rs * (dxh - xh * jnp.mean(dxh * xh, axis=-1, keepdims=True))
            dx_ref[...] = dx
            dxb_ref[...] = dx.astype(BF16)

            @pl.when(pl.program_id(0) == 0)
            def _():
                dw_ref[...] = jnp.zeros(dw_ref.shape, F32)

            dw_ref[...] += jnp.sum(r * xh, axis=0, keepdims=True)

        if nk == 1:
            finish(p)
            return
        acc = refs[2 + n_extra + n_out]
        l = pl.program_id(2)

        @pl.when(l == 0)
        def _():
            acc[...] = p

        @pl.when(jnp.logical_and(l > 0, l < nk - 1))
        def _():
            acc[...] += p

        @pl.when(l == nk - 1)
        def _():
            finish(acc[...] + p)

    scratch = [pltpu.VMEM((tm, tn), F32)] if nk > 1 else []
    if has_rms:
        x, w, dres = rms_bwd
        vec = pl.BlockSpec((1, n), lambda i, j, l: (0, 0))
        return pl.pallas_call(
            body, name=name, grid=(m // tm, 1, nk), in_specs=[a_spec, b_spec, o_spec, vec, o_spec],
            out_specs=[o_spec, o_spec, vec], scratch_shapes=scratch,
            out_shape=[jax.ShapeDtypeStruct((m, n), F32), jax.ShapeDtypeStruct((m, n), BF16),
                       jax.ShapeDtypeStruct((1, n), F32)],
            compiler_params=_params(("arbitrary", "arbitrary", "arbitrary")))(a, b, x, w, dres)
    ins = [a, b] + ([res] if has_res else [])
    in_specs = [a_spec, b_spec] + ([o_spec] if has_res else [])
    return pl.pallas_call(
        body, name=name, grid=(m // tm, n // tn, nk), in_specs=in_specs, out_specs=o_spec,
        out_shape=jax.ShapeDtypeStruct((m, n), out_dtype), scratch_shapes=scratch,
        compiler_params=_params(("parallel", "parallel", "arbitrary")))(*ins)


def _rowwise(fn, rows, bcasts, outs, accs=(), *, tile, name):
    s = (rows[0][0] if isinstance(rows[0], tuple) else rows[0]).shape[-2]
    tile = min(tile, s)
    n_val = len(rows) + len(bcasts)
    intos = [(k, o) for k, o in enumerate(outs) if len(o) == 4]
    n_in, n_out = n_val + len(intos), len(outs)

    def row_spec(c):
        if isinstance(c, tuple):
            return pl.BlockSpec((c[0], tile, c[1]), lambda i: (0, i, 0))
        return pl.BlockSpec((tile, c), lambda i: (i, 0))

    def row_shape(c):
        return (c[0], s, c[1]) if isinstance(c, tuple) else (s, c)

    def window(width, cb):
        return pl.BlockSpec((tile, width), lambda i: (i, cb))

    def body(*refs):
        vals = fn(*[r[...] for r in refs[:n_val]])
        o_refs = refs[n_in:n_in + n_out]
        a_refs = refs[n_in + n_out:]
        for r, v in zip(o_refs, vals[:n_out]):
            if isinstance(v, list):
                for i, vi in enumerate(v):
                    r[i] = vi.astype(r.dtype)
            else:
                r[...] = v.astype(r.dtype)

        @pl.when(pl.program_id(0) == 0)
        def _():
            for r in a_refs:
                r[...] = jnp.zeros(r.shape, r.dtype)

        for r, v in zip(a_refs, vals[n_out:]):
            r[...] += v

    in_specs = [window(r[1], r[2]) if isinstance(r, tuple)
                else row_spec(r.shape[1] if r.ndim == 2 else (r.shape[0], r.shape[2])) for r in rows]
    in_specs += [pl.BlockSpec(b.shape, lambda i: (0, 0)) for b in bcasts]
    in_specs += [pl.BlockSpec(memory_space=pl.ANY) for _ in intos]
    out_specs = [window(o[0], o[3]) if len(o) == 4 else row_spec(o[0]) for o in outs]
    out_specs += [pl.BlockSpec(sh, lambda i: (0, 0)) for sh, _ in accs]
    out_shape = [jax.ShapeDtypeStruct(o[2].shape, o[2].dtype) if len(o) == 4
                 else jax.ShapeDtypeStruct(row_shape(o[0]), o[1]) for o in outs]
    out_shape += [jax.ShapeDtypeStruct(sh, dt) for sh, dt in accs]
    args = [r[0] if isinstance(r, tuple) else r for r in rows] + list(bcasts) + [o[2] for _, o in intos]
    return pl.pallas_call(body, name=name, grid=(s // tile,), in_specs=in_specs, out_specs=out_specs,
                          out_shape=out_shape, input_output_aliases={n_val + i: k for i, (k, _) in enumerate(intos)},
                          compiler_params=_params(("arbitrary",)))(*args)


def _final_loss(x, w, tgt, name):
    d = x.shape[1]

    def fn(x, t, w):
        r = lax.rsqrt(jnp.mean(x * x, axis=-1, keepdims=True) + RMS_EPS)
        xh = x * r
        err = xh * w - t
        part = jnp.sum(jnp.mean(err * err, axis=-1, keepdims=True), axis=0, keepdims=True) * 0.5
        dy = err * (1.0 / d)
        dxh = dy * w
        dx = r * (dxh - xh * jnp.mean(dxh * xh, axis=-1, keepdims=True))
        return dx, dx, part, jnp.sum(dy * xh, axis=0, keepdims=True)
    dx, dxb, part, dw = _rowwise(fn, [x, tgt], [w], [(d, F32), (d, BF16)], [((1, 1), F32), ((1, d), F32)],
                                 tile=256, name=name)
    return part, dx, dxb, dw


def _softplus_fwd(dtr, bias, name):
    def fn(r, b):
        v = r + b
        return (jnp.maximum(v, 0.0) + jnp.log(1.0 + jnp.exp(-jnp.abs(v))),)
    return _rowwise(fn, [dtr], [bias], [(LANES, F32)], tile=512, name=name)[0]


def _softplus_bwd(ddt, dtr, bias, n_heads, into, name):
    def fn(g, r, b):
        lane = lax.broadcasted_iota(jnp.int32, g.shape, 1)
        d = jnp.where(lane < n_heads, g * jax.nn.sigmoid(r + b), 0.0)
        return d, jnp.sum(d, axis=0, keepdims=True)
    return _rowwise(fn, [ddt, dtr], [bias], [(LANES, BF16, *into)], [((1, LANES), F32)], tile=512, name=name)


def _gnorm_fwd(y, z, w, n_groups, name):
    di = y.shape[1]
    gs = di // n_groups

    def fn(y, z, w):
        y2 = y * (z * jax.nn.sigmoid(z))
        out = []
        for g in range(n_groups):
            sl = y2[:, g * gs:(g + 1) * gs]
            r = lax.rsqrt(jnp.mean(sl * sl, axis=-1, keepdims=True) + GATED_NORM_EPS)
            out.append(sl * r)
        return (jnp.concatenate(out, axis=1) * w,)
    return _rowwise(fn, [y, z], [w], [(di, BF16)], tile=256, name=name)[0]


def _gnorm_bwd(dyn, y, z, w, n_groups, into, name):
    di = y.shape[1]
    gs = di // n_groups

    def fn(dyn, y, z, w):
        sig = jax.nn.sigmoid(z)
        sz = z * sig
        y2 = y * sz
        d2n = dyn * w
        dy2, yhat = [], []
        for g in range(n_groups):
            sl = y2[:, g * gs:(g + 1) * gs]
            dg = d2n[:, g * gs:(g + 1) * gs]
            r = lax.rsqrt(jnp.mean(sl * sl, axis=-1, keepdims=True) + GATED_NORM_EPS)
            yh = sl * r
            dy2.append(r * (dg - yh * jnp.mean(dg * yh, axis=-1, keepdims=True)))
            yhat.append(yh)
        dy2 = jnp.concatenate(dy2, axis=1)
        yhat = jnp.concatenate(yhat, axis=1)
        dz = dy2 * y * (sig * (1.0 + z * (1.0 - sig)))
        return dy2 * sz, dz, jnp.sum(dyn * yhat, axis=0, keepdims=True)
    return _rowwise(fn, [dyn, y, z], [w], [(di, F32), (di, BF16, *into)], [((1, di), F32)], tile=128, name=name)


def _lane_place(cols):
    rows = cols[0].shape[0]
    lane = lax.broadcasted_iota(jnp.int32, (rows, LANES), 1)
    out = jnp.zeros((rows, LANES), F32)
    for j, c in enumerate(cols):
        out = jnp.where(lane == j, c, out)
    return out


def _merge_heads(os_, lses, name):
    n = len(os_)
    n_kv, rep, hd = ATT_KV_HEADS_PER_GROUP, ATT_HEADS_PER_GROUP // ATT_KV_HEADS_PER_GROUP, ATT_HEAD_DIM

    def fn(*v):
        o, l = v[:n], v[n:]
        out, lse = [], []
        for h in range(n_kv):
            cols = []
            for j in range(rep):
                hh = h * rep + j
                lg = [li[h][:, j:j + 1] for li in l]
                m = functools.reduce(jnp.maximum, lg)
                e = [jnp.exp(x - m) for x in lg]
                tot = functools.reduce(jnp.add, e)
                acc = functools.reduce(jnp.add, [ei * oi[hh] for ei, oi in zip(e, o)])
                out.append(acc / tot)
                cols.append(m + jnp.log(tot))
            lse.append(_lane_place(cols))
        merged = jnp.concatenate(out, axis=1)
        return merged, merged, lse
    c = os_[0].shape[0] * hd
    return _rowwise(fn, list(os_) + list(lses), [], [(c, F32), (c, BF16), ((n_kv, LANES), F32)], tile=256, name=name)


def _delta_heads(do, o, name):
    n_kv, rep, hd = ATT_KV_HEADS_PER_GROUP, ATT_HEADS_PER_GROUP // ATT_KV_HEADS_PER_GROUP, ATT_HEAD_DIM

    def fn(do, o):
        p = do * o
        return ([_lane_place([jnp.sum(p[:, (h * rep + j) * hd:(h * rep + j + 1) * hd], axis=-1, keepdims=True)
                              for j in range(rep)]) for h in range(n_kv)],)
    return _rowwise(fn, [do, o], [], [((n_kv, LANES), F32)], tile=256, name=name)[0]


def _sum_slabs(recv, name):
    def body(r_ref, o_ref):
        acc = r_ref[0]
        for k in range(1, NDEV):
            acc = acc + r_ref[k]
        o_ref[...] = acc
    return pl.pallas_call(body, name=name, out_shape=jax.ShapeDtypeStruct(recv.shape[1:], F32),
                          compiler_params=_params())(recv)


def _shift_down(x, k):
    if k == 0:
        return x
    r = pltpu.roll(x, k, 0)
    row = lax.broadcasted_iota(jnp.int32, (SUBLANES, x.shape[1]), 0)
    return jnp.concatenate([jnp.where(row >= k, r[:SUBLANES], 0.0), r[SUBLANES:]], axis=0)


def _shift_up(x, k):
    if k == 0:
        return x
    s = x.shape[0]
    r = pltpu.roll(x, s - k, 0)
    row = lax.broadcasted_iota(jnp.int32, (SUBLANES, x.shape[1]), 0)
    return jnp.concatenate([r[:s - SUBLANES], jnp.where(row < SUBLANES - k, r[s - SUBLANES:], 0.0)], axis=0)


def _conv(x, w):
    kw = w.shape[0]
    return functools.reduce(jnp.add, [w[k:k + 1, :] * _shift_down(x, kw - 1 - k) for k in range(kw)])


def _conv_t(dy, w):
    kw = w.shape[0]
    return functools.reduce(jnp.add, [w[k:k + 1, :] * _shift_up(dy, kw - 1 - k) for k in range(kw)])


def _conv_dw(x, dy, dw_ref):
    kw = dw_ref.shape[0]
    for k in range(kw):
        dw_ref[k:k + 1, :] = jnp.sum(dy * _shift_down(x, kw - 1 - k), axis=0, keepdims=True)


def _dsilu(pre):
    sig = jax.nn.sigmoid(pre)
    return sig * (1.0 + pre * (1.0 - sig))


def _col_specs(s, c, kw, tc):
    return (pl.BlockSpec((s, tc), lambda j: (0, j)), pl.BlockSpec((kw, tc), lambda j: (0, j)),
            pl.BlockSpec((1, tc), lambda j: (0, j)))


def _conv_silu_fwd(x, col0, w, b, name):
    s, c = x.shape[0], w.shape[1]
    tc = LANES
    xs, ws, bs = _col_specs(s, c, w.shape[0], tc)
    xwin = pl.BlockSpec((s, tc), lambda j: (0, j + col0 // tc))

    def body(x_ref, w_ref, b_ref, o_ref):
        pre = _conv(x_ref[...], w_ref[...]) + b_ref[...]
        o_ref[...] = pre * jax.nn.sigmoid(pre)
    return pl.pallas_call(body, name=name, grid=(c // tc,), in_specs=[xwin, ws, bs], out_specs=xs,
                          out_shape=jax.ShapeDtypeStruct((s, c), F32), compiler_params=_params(("parallel",)))(x, w, b)


def _conv_silu_bwd(x, col0, w, b, dy, into, name):
    s, c = x.shape[0], w.shape[1]
    tc = LANES
    xs, ws, bs = _col_specs(s, c, w.shape[0], tc)
    xwin = pl.BlockSpec((s, tc), lambda j: (0, j + col0 // tc))

    def body(x_ref, w_ref, b_ref, dy_ref, _, dx_ref, dw_ref, db_ref):
        xv, wv = x_ref[...], w_ref[...]
        pre = _conv(xv, wv) + b_ref[...]
        dpre = dy_ref[...] * _dsilu(pre)
        dx_ref[...] = _conv_t(dpre, wv).astype(dx_ref.dtype)
        _conv_dw(xv, dpre, dw_ref)
        db_ref[...] = jnp.sum(dpre, axis=0, keepdims=True)
    return pl.pallas_call(
        body, name=name, grid=(c // tc,), in_specs=[xwin, ws, bs, xs, pl.BlockSpec(memory_space=pl.ANY)],
        out_specs=[xwin, ws, bs], input_output_aliases={4: 0},
        out_shape=[jax.ShapeDtypeStruct(into.shape, into.dtype), jax.ShapeDtypeStruct(w.shape, F32),
                   jax.ShapeDtypeStruct((1, c), F32)],
        compiler_params=_params(("parallel",)))(x, w, b, dy, into)


def _gate_specs(s, f, kw):
    nt = f // LANES
    return (pl.BlockSpec((s, LANES), lambda j: (0, j)), pl.BlockSpec((s, LANES), lambda j: (0, j + nt)),
            pl.BlockSpec((kw, LANES), lambda j: (0, j)), pl.BlockSpec((kw, LANES), lambda j: (0, j + nt)))


def _ffn_gate_fwd(u, w, name):
    s, f = u.shape[0], u.shape[1] // 2
    gs, vs, wgs, wvs = _gate_specs(s, f, w.shape[0])

    def body(g_ref, v_ref, wg_ref, wv_ref, o_ref):
        g = _conv(g_ref[...], wg_ref[...])
        v = _conv(v_ref[...], wv_ref[...])
        o_ref[...] = (g * jax.nn.sigmoid(g) * v).astype(o_ref.dtype)
    return pl.pallas_call(body, name=name, grid=(f // LANES,), in_specs=[gs, vs, wgs, wvs], out_specs=gs,
                          out_shape=jax.ShapeDtypeStruct((s, f), BF16),
                          compiler_params=_params(("parallel",)))(u, u, w, w)


def _ffn_gate_bwd(u, w, df, name):
    s, f = u.shape[0], u.shape[1] // 2
    kw = w.shape[0]
    gs, vs, wgs, wvs = _gate_specs(s, f, kw)

    def body(g_ref, v_ref, wg_ref, wv_ref, df_ref, du_ref, dw_ref):
        gp, vp, wgv, wvv = g_ref[...], v_ref[...], wg_ref[...], wv_ref[...]
        g = _conv(gp, wgv)
        v = _conv(vp, wvv)
        dfv = df_ref[...]
        dg = dfv * v * _dsilu(g)
        dv = dfv * (g * jax.nn.sigmoid(g))
        du_ref[0] = _conv_t(dg, wgv).astype(du_ref.dtype)
        du_ref[1] = _conv_t(dv, wvv).astype(du_ref.dtype)
        _conv_dw(gp, dg, dw_ref.at[0])
        _conv_dw(vp, dv, dw_ref.at[1])
    return pl.pallas_call(
        body, name=name, grid=(f // LANES,), in_specs=[gs, vs, wgs, wvs, gs],
        out_specs=[pl.BlockSpec((2, s, LANES), lambda j: (0, 0, j)), pl.BlockSpec((2, kw, LANES), lambda j: (0, 0, j))],
        out_shape=[jax.ShapeDtypeStruct((2, s, f), BF16), jax.ShapeDtypeStruct((2, kw, f), F32)],
        compiler_params=_params(("parallel",)))(u, u, w, w, df)


def _ssd_common(dt, alog, n_heads):
    ln = dt.shape[0]
    lane = lax.broadcasted_iota(jnp.int32, (1, LANES), 1)
    a = jnp.where(lane < n_heads, -jnp.exp(alog), 0.0)
    row = lax.broadcasted_iota(jnp.int32, (ln, ln), 0)
    col = lax.broadcasted_iota(jnp.int32, (ln, ln), 1)
    tril = col <= row
    acs = _dot(tril.astype(F32), dt * a, precision=HIGHEST)
    return a, acs, acs.T, tril


def _split(x, n):
    out = []
    for _ in range(n):
        piece = x.astype(BF16)
        out.append(piece)
        x = x - piece.astype(F32)
    return out


def _spread(x, onehot, n=2):
    return functools.reduce(jnp.add, [_dot(piece, onehot) for piece in _split(x, n)])


def _head_maps(di, p):
    e = (jnp.arange(di, dtype=jnp.int32)[None, :] // p == jnp.arange(LANES, dtype=jnp.int32)[:, None]).astype(BF16)
    return e, e.T


def _ssd_wide(dt, acs, acs_t, dskip, e_ref, et_ref):
    ln = dt.shape[0]
    last = acs[ln - 1:ln, :]
    stack = jnp.concatenate([dt, jnp.exp(acs), jnp.exp(last - acs), jnp.broadcast_to(dskip, (8, LANES))], axis=0)
    wide = _spread(stack, e_ref[...])
    tb = jnp.exp(jnp.broadcast_to(acs_t[:, ln - 1:ln], (LANES, LANES)))
    texp = functools.reduce(jnp.add, [_dot(et_ref[...], piece) for piece in _split(tb, 3)])
    return wide[:ln], wide[ln:2 * ln], wide[2 * ln:3 * ln], wide[3 * ln:3 * ln + 1], texp


def _ssd_fwd(xbc, dt, alog, dskip, di, n_heads, n_groups, name):
    s, convd = xbc.shape
    ln, p, ns = SSM_CHUNK, SSM_HEAD_DIM, SSM_D_STATE
    nc, hg = s // ln, n_heads // n_groups
    gw = hg * p
    e64, e64t = _head_maps(di, p)

    def body(x_ref, dt_ref, alog_ref, d_ref, e_ref, et_ref, y_ref, prev_ref, st):
        @pl.when(pl.program_id(0) == 0)
        def _():
            st[...] = jnp.zeros(st.shape, F32)

        dt = dt_ref[...]
        _, acs, acs_t, tril = _ssd_common(dt, alog_ref[...], n_heads)
        dte, ee, dse, dske, texp = _ssd_wide(dt, acs, acs_t, d_ref[...], e_ref, et_ref)
        x = x_ref[:, :di]
        xdt = x * dte
        xdtb = xdt.astype(BF16)
        xdsb = (xdt * dse).astype(BF16)
        for g in range(n_groups):
            rows = slice(g * gw, (g + 1) * gw)
            bg = x_ref[:, di + g * ns:di + (g + 1) * ns].astype(BF16)
            cg = x_ref[:, di + (n_groups + g) * ns:di + (n_groups + g + 1) * ns].astype(BF16)
            gm = _dot(cg, bg, _NT)
            prev = st[rows, :]
            prev_ref[0, rows, :] = prev
            yo = _dot(cg, prev.astype(BF16), _NT)
            for j in range(hg):
                h = g * hg + j
                seg = acs[:, h:h + 1] - acs_t[h:h + 1, :]
                m = jnp.where(tril, gm * jnp.exp(jnp.where(tril, seg, 0.0)), 0.0)
                y_ref[:, h * p:(h + 1) * p] = _dot(m.astype(BF16), xdtb[:, h * p:(h + 1) * p])
            y_ref[:, rows] = y_ref[:, rows] + yo * ee[:, rows] + x[:, rows] * dske[:, rows]
            st[rows, :] = prev * texp[rows, :] + _dot(xdsb[:, rows], bg, _TN)

    vec = pl.BlockSpec((1, LANES), lambda c: (0, 0))
    return pl.pallas_call(
        body, name=name, grid=(nc,),
        in_specs=[pl.BlockSpec((ln, convd), lambda c: (c, 0)), pl.BlockSpec((ln, LANES), lambda c: (c, 0)), vec, vec,
                  pl.BlockSpec(e64.shape, lambda c: (0, 0)), pl.BlockSpec(e64t.shape, lambda c: (0, 0))],
        out_specs=[pl.BlockSpec((ln, di), lambda c: (c, 0)), pl.BlockSpec((1, di, ns), lambda c: (c, 0, 0))],
        out_shape=[jax.ShapeDtypeStruct((s, di), F32), jax.ShapeDtypeStruct((nc, di, ns), F32)],
        scratch_shapes=[pltpu.VMEM((di, ns), F32)],
        compiler_params=_params(("arbitrary",)))(xbc, dt, alog, dskip, e64, e64t)


def _ssd_bwd(xbc, dt, alog, dskip, prev_all, dy, di, n_heads, n_groups, name):
    s, convd = xbc.shape
    ln, p, ns = SSM_CHUNK, SSM_HEAD_DIM, SSM_D_STATE
    nc, hg = s // ln, n_heads // n_groups
    gw = hg * p
    e64, e64t = _head_maps(di, p)

    def body(x_ref, dt_ref, alog_ref, d_ref, e_ref, et_ref, prev_ref, dy_ref,
             dx_ref, ddt_ref, da_ref, dd_ref, dh, yo_ref, w_ref):
        step = pl.program_id(0)

        @pl.when(step == 0)
        def _():
            dh[...] = jnp.zeros(dh.shape, F32)
            da_ref[...] = jnp.zeros(da_ref.shape, F32)
            dd_ref[...] = jnp.zeros(dd_ref.shape, F32)

        dt = dt_ref[...]
        a, acs, acs_t, tril = _ssd_common(dt, alog_ref[...], n_heads)
        dte, ee, dse, dske, texp = _ssd_wide(dt, acs, acs_t, d_ref[...], e_ref, et_ref)
        row = lax.broadcasted_iota(jnp.int32, (ln, ln), 0)
        col = lax.broadcasted_iota(jnp.int32, (ln, ln), 1)
        triu = col >= row
        x = x_ref[:, :di]
        dy = dy_ref[...]
        xdt = x * dte
        xdtb = xdt.astype(BF16)
        xdsb = (xdt * dse).astype(BF16)
        dyb = dy.astype(BF16)
        dyob = (dy * ee).astype(BF16)
        dhn = dh[...]
        dhb = dhn.astype(BF16)
        per_head = functools.reduce(jnp.add, [_dot(e_ref[...], piece) for piece in _split(dhn * prev_ref[0], 2)])
        ones8 = jnp.ones((8, LANES), BF16)
        dtt = functools.reduce(jnp.add, [_dot(ones8, piece, _NT) for piece in _split(per_head, 2)])[0:1]
        dacs_c = jnp.zeros((ln, LANES), F32)
        dacs_r = jnp.zeros((LANES, ln), F32)
        for g in range(n_groups):
            rows = slice(g * gw, (g + 1) * gw)
            bg = x_ref[:, di + g * ns:di + (g + 1) * ns].astype(BF16)
            cg = x_ref[:, di + (n_groups + g) * ns:di + (n_groups + g + 1) * ns].astype(BF16)
            gmt = _dot(bg, cg, _NT)
            prevb = prev_ref[0, rows, :].astype(BF16)
            dcg = _dot(dyob[:, rows], prevb)
            dh[rows, :] = texp[rows, :] * dhn[rows, :] + _dot(dyob[:, rows], cg, _TN)
            w = _dot(bg, dhb[rows, :], _NT)
            dbg = _dot(xdsb[:, rows], dhb[rows, :])
            yo_ref[:, rows] = _dot(cg, prevb, _NT)
            w_ref[:, rows] = w
            dgmt = jnp.zeros((ln, ln), F32)
            q_hi, q_lo = [], []
            for j in range(hg):
                h = g * hg + j
                segt = acs_t[h:h + 1, :] - acs[:, h:h + 1]
                dect = jnp.where(triu, jnp.exp(jnp.where(triu, segt, 0.0)), 0.0)
                dyh, xh = dyb[:, h * p:(h + 1) * p], xdtb[:, h * p:(h + 1) * p]
                mt = gmt * dect
                dmt = _dot(xh, dyh, _NT)
                dx_ref[:, h * p:(h + 1) * p] = _dot(mt.astype(BF16), dyh)
                dgmt = dgmt + dmt * dect
                hi, lo = _split(dmt * mt, 2)
                q_hi.append(hi)
                q_lo.append(lo)
            sel_c = (lax.broadcasted_iota(jnp.int32, (hg * ln, LANES), 1)
                     == g * hg + lax.broadcasted_iota(jnp.int32, (hg * ln, LANES), 0) // ln).astype(BF16)
            sel_r = (lax.broadcasted_iota(jnp.int32, (LANES, hg * ln), 0)
                     == g * hg + lax.broadcasted_iota(jnp.int32, (LANES, hg * ln), 1) // ln).astype(BF16)
            for pieces in (q_hi, q_lo):
                dacs_c = dacs_c - _dot(jnp.concatenate(pieces, axis=1), sel_c)
                dacs_r = dacs_r + _dot(sel_r, jnp.concatenate(pieces, axis=0))
            dgb = dgmt.astype(BF16)
            dx_ref[:, di + g * ns:di + (g + 1) * ns] = dbg + _dot(dgb, cg)
            dx_ref[:, di + (n_groups + g) * ns:di + (n_groups + g + 1) * ns] = dcg + _dot(dgb, bg, _TN)

        wds = w_ref[...] * dse
        dxdt = dx_ref[:, :di] + wds
        red = _spread(jnp.concatenate([dxdt * x, dy * yo_ref[...] * ee, xdt * wds, dy * x], axis=0), et_ref[...], n=1)
        ddt_x, r_off, r_state, ddr = red[:ln], red[ln:2 * ln], red[2 * ln:3 * ln], red[3 * ln:]
        dx_ref[:, :di] = dxdt * dte + dy * dske
        rowi = lax.broadcasted_iota(jnp.int32, (ln, LANES), 0)
        dlast = jnp.sum(r_state, axis=0, keepdims=True) + dtt * jnp.exp(acs[ln - 1:ln, :])
        dacs = r_off - r_state + dacs_c + dacs_r.T + jnp.where(rowi == ln - 1, dlast, 0.0)
        dadt = _dot(triu.astype(F32), dacs, precision=HIGHEST)
        ddt_ref[...] = dadt * a + ddt_x
        da_ref[...] += jnp.sum(dadt * dt, axis=0, keepdims=True)
        dd_ref[...] += jnp.sum(ddr, axis=0, keepdims=True)

        @pl.when(step == nc - 1)
        def _():
            da_ref[...] = da_ref[...] * a

    vec = pl.BlockSpec((1, LANES), lambda c: (0, 0))
    rev = lambda c: (nc - 1 - c, 0)
    return pl.pallas_call(
        body, name=name, grid=(nc,),
        in_specs=[pl.BlockSpec((ln, convd), rev), pl.BlockSpec((ln, LANES), rev), vec, vec,
                  pl.BlockSpec(e64.shape, lambda c: (0, 0)), pl.BlockSpec(e64t.shape, lambda c: (0, 0)),
                  pl.BlockSpec((1, di, ns), lambda c: (nc - 1 - c, 0, 0)), pl.BlockSpec((ln, di), rev)],
        out_specs=[pl.BlockSpec((ln, convd), rev), pl.BlockSpec((ln, LANES), rev), vec, vec],
        out_shape=[jax.ShapeDtypeStruct((s, convd), F32), jax.ShapeDtypeStruct((s, LANES), F32),
                   jax.ShapeDtypeStruct((1, LANES), F32), jax.ShapeDtypeStruct((1, LANES), F32)],
        scratch_shapes=[pltpu.VMEM((di, ns), F32), pltpu.VMEM((ln, di), F32), pltpu.VMEM((ln, di), F32)],
        compiler_params=_params(("arbitrary",)))(xbc, dt, alog, dskip, e64, e64t, prev_all, dy)


def _rot_tables(s, d):
    half = ROPE_DIM // 2
    inv_freq = jnp.power(jnp.float32(ROPE_THETA), -jnp.arange(0, ROPE_DIM, 2, dtype=F32) / ROPE_DIM)
    v = jnp.arange(s, dtype=jnp.int32)
    pos = (v % (s // d)) * d + v // (s // d)
    ang = pos.astype(F32)[:, None] * inv_freq[None, :]
    cos, sin = jnp.cos(ang), jnp.sin(ang)
    zero = jnp.zeros((s, ATT_HEAD_DIM - ROPE_DIM), F32)
    cf = jnp.concatenate([cos, cos, jnp.ones_like(zero)], axis=1)
    s1 = jnp.concatenate([-sin, jnp.zeros_like(sin), zero], axis=1)
    s2 = jnp.concatenate([jnp.zeros_like(sin), sin, zero], axis=1)
    assert half * 2 == ROPE_DIM
    return cf, s1, s2


def _rot(x, tabs, sign):
    cf, s1, s2 = tabs
    half = ROPE_DIM // 2
    left = pltpu.roll(x, ATT_HEAD_DIM - half, 1)
    right = pltpu.roll(x, half, 1)
    return x * cf + sign * (left * s1 + right * s2)


def _att_masks(n, n_blk, rep):
    b = ATT_BLOCK
    row = lax.broadcasted_iota(jnp.int32, (rep * b, b), 0) & (b - 1)
    col = lax.broadcasted_iota(jnp.int32, (rep * b, b), 1)
    off = jnp.where(n % n_blk != 0, 0, 2 * b)
    return col <= row, col >= row + off


def _rot_heads(x, tabs, width, sign, out_dtype, name):
    s = x.shape[0]
    hd = ATT_HEAD_DIM
    tile = min(512, s)

    def body(x_ref, cf, s1, s2, o_ref):
        t = (cf[...], s1[...], s2[...])
        for j in range(width // hd):
            o_ref[:, j * hd:(j + 1) * hd] = _rot(x_ref[:, j * hd:(j + 1) * hd], t, sign).astype(o_ref.dtype)

    tab = pl.BlockSpec((tile, hd), lambda i: (i, 0))
    return pl.pallas_call(
        body, name=name, grid=(s // tile,), in_specs=[pl.BlockSpec((tile, width), lambda i: (i, 0)), tab, tab, tab],
        out_specs=pl.BlockSpec((tile, width), lambda i: (i, 0)), out_shape=jax.ShapeDtypeStruct((s, width), out_dtype),
        compiler_params=_params(("parallel",)))(x, *tabs)


def _kv_grad(dk_rot, dv, tabs, name):
    s, width = dk_rot.shape
    hd = ATT_HEAD_DIM
    tile = min(512, s)

    def body(k_ref, v_ref, cf, s1, s2, o_ref):
        t = (cf[...], s1[...], s2[...])
        for j in range(width // hd):
            o_ref[:, j * hd:(j + 1) * hd] = _rot(k_ref[:, j * hd:(j + 1) * hd], t, -1.0).astype(o_ref.dtype)
        o_ref[:, width:] = v_ref[...].astype(o_ref.dtype)

    tab = pl.BlockSpec((tile, hd), lambda i: (i, 0))
    half = pl.BlockSpec((tile, width), lambda i: (i, 0))
    return pl.pallas_call(
        body, name=name, grid=(s // tile,), in_specs=[half, half, tab, tab, tab],
        out_specs=pl.BlockSpec((tile, 2 * width), lambda i: (i, 0)),
        out_shape=jax.ShapeDtypeStruct((s, 2 * width), BF16), compiler_params=_params(("parallel",)))(dk_rot, dv, *tabs)


def _rows_of(r, dil):
    return pl.ds(r, ATT_BLOCK, stride=dil) if dil > 1 else slice(None)


def _nat_specs(g, dil, n_kv_all, cur, prv):
    b, hd = ATT_BLOCK * dil, ATT_HEAD_DIM
    n_kv = ATT_KV_HEADS_PER_GROUP
    rep = ATT_HEADS_PER_GROUP // n_kv
    q_all = [pl.BlockSpec((b, hd), lambda h, n, j=j: (cur(n), (g * n_kv + h) * rep + j)) for j in range(rep)]
    q_own = [pl.BlockSpec((b, hd), lambda h, n, j=j: (cur(n), h * rep + j)) for j in range(rep)]
    hm_all = pl.BlockSpec((rep, b, hd), lambda h, n: (g * n_kv + h, cur(n), 0))
    hm_own = pl.BlockSpec((rep, b, hd), lambda h, n: (h, cur(n), 0))
    kc = pl.BlockSpec((b, hd), lambda h, n: (cur(n), g * n_kv + h))
    kp = pl.BlockSpec((b, hd), lambda h, n: (prv(n), g * n_kv + h))
    vc = pl.BlockSpec((b, hd), lambda h, n: (cur(n), n_kv_all + g * n_kv + h))
    vp = pl.BlockSpec((b, hd), lambda h, n: (prv(n), n_kv_all + g * n_kv + h))
    tab = pl.BlockSpec((b, hd), lambda h, n: (cur(n), 0))
    stat = pl.BlockSpec((None, b, LANES), lambda h, n: (h, cur(n), 0))
    return q_all, q_own, hm_all, hm_own, kc, kp, vc, vp, tab, stat


def _head_cols(stat, rep):
    return jnp.concatenate([jnp.broadcast_to(stat[:, j:j + 1], stat.shape) for j in range(rep)], axis=0)


def _attn_fwd_nat(q_all, k_rot, kv, tabs, g, dil, name):
    s = q_all.shape[0]
    b, hd = ATT_BLOCK, ATT_HEAD_DIM
    nbn = s // (b * dil)
    n_kv = ATT_KV_HEADS_PER_GROUP
    rep = ATT_HEADS_PER_GROUP // n_kv
    n_kv_all = k_rot.shape[1] // hd
    scale = hd ** -0.5

    def body(*refs):
        q_refs = refs[:rep]
        kc_ref, kp_ref, vc_ref, vp_ref, cf, s1, s2, o_ref, lse_ref = refs[rep:]
        mc, mp = _att_masks(jnp.where(pl.program_id(1) > 0, 1, 0), 2, rep)
        for r in range(dil):
            sl = _rows_of(r, dil)
            tc = (cf[sl, :], s1[sl, :], s2[sl, :])
            q4 = (jnp.concatenate([_rot(q_ref[sl, :], tc, 1.0) for q_ref in q_refs], axis=0) * scale).astype(BF16)
            kc, kp = kc_ref[sl, :].astype(BF16), kp_ref[sl, :].astype(BF16)
            sc = jnp.where(mc, _dot(q4, kc, _NT), NEG)
            sp = jnp.where(mp, _dot(q4, kp, _NT), NEG)
            m = jnp.maximum(jnp.max(sc, axis=1, keepdims=True), jnp.max(sp, axis=1, keepdims=True))
            pc, pp = jnp.exp(sc - m), jnp.exp(sp - m)
            l = jnp.sum(pc, axis=1, keepdims=True) + jnp.sum(pp, axis=1, keepdims=True)
            o = (_dot(pc.astype(BF16), vc_ref[sl, :].astype(BF16))
                 + _dot(pp.astype(BF16), vp_ref[sl, :].astype(BF16))) / l
            lse = m + jnp.log(l)
            for j in range(rep):
                o_ref[j, sl, :] = o[j * b:(j + 1) * b]
            lse_ref[sl, :] = _lane_place([lse[j * b:(j + 1) * b] for j in range(rep)])

    cur = lambda n: n
    prv = lambda n: jnp.maximum(n - 1, 0)
    q_specs, _, _, hm_own, kc, kp, vc, vp, tab, stat = _nat_specs(g, dil, n_kv_all, cur, prv)
    return pl.pallas_call(
        body, name=name, grid=(n_kv, nbn), in_specs=[*q_specs, kc, kp, vc, vp, tab, tab, tab], out_specs=[hm_own, stat],
        out_shape=[jax.ShapeDtypeStruct((ATT_HEADS_PER_GROUP, s, hd), F32), jax.ShapeDtypeStruct((n_kv, s, LANES), F32)],
        compiler_params=_params(("parallel", "arbitrary")))(*([q_all] * rep), k_rot, k_rot, kv, kv, *tabs)


def _attn_bwd_nat(q_all, k_rot, kv, do, lse, delta, tabs, grads, g, dil, name):
    s = q_all.shape[0]
    b, hd = ATT_BLOCK, ATT_HEAD_DIM
    nbn = s // (b * dil)
    n_kv = ATT_KV_HEADS_PER_GROUP
    rep = ATT_HEADS_PER_GROUP // n_kv
    n_kv_all = k_rot.shape[1] // hd
    scale = hd ** -0.5

    def body(*refs):
        q_refs, do_refs = refs[:rep], refs[rep:2 * rep]
        (lse_ref, dl_ref, kc_ref, kp_ref, vc_ref, vp_ref, cf, s1, s2, _, _, _,
         dq_ref, dk_ref, dv_ref, ck, cv) = refs[2 * rep:]
        n = pl.program_id(1)

        @pl.when(n == 0)
        def _():
            ck[...] = jnp.zeros(ck.shape, F32)
            cv[...] = jnp.zeros(cv.shape, F32)

        @pl.when(n < nbn)
        def _():
            mc, mp = _att_masks(jnp.where(n > 0, 1, 0), 2, rep)
            for r in range(dil):
                sl = _rows_of(r, dil)
                own = slice(r * b, (r + 1) * b)
                tc = (cf[sl, :], s1[sl, :], s2[sl, :])
                q4 = (jnp.concatenate([_rot(q_ref[sl, :], tc, 1.0) for q_ref in q_refs], axis=0) * scale).astype(BF16)
                do4 = jnp.concatenate([do_ref[sl, :] for do_ref in do_refs], axis=0).astype(BF16)
                lse4 = _head_cols(lse_ref[sl, :], rep)
                dl4 = _head_cols(dl_ref[sl, :], rep)
                kc, kp = kc_ref[sl, :].astype(BF16), kp_ref[sl, :].astype(BF16)
                vc, vp = vc_ref[sl, :].astype(BF16), vp_ref[sl, :].astype(BF16)
                pc = jnp.where(mc, jnp.exp(_dot(q4, kc, _NT) - lse4), 0.0)
                pp = jnp.where(mp, jnp.exp(_dot(q4, kp, _NT) - lse4), 0.0)
                dsc = (pc * (_dot(do4, vc, _NT) - dl4)).astype(BF16)
                dsp = (pp * (_dot(do4, vp, _NT) - dl4)).astype(BF16)
                dq4 = (_dot(dsc, kc) + _dot(dsp, kp)) * scale
                for j in range(rep):
                    dq_ref[j, sl, :] = _rot(dq4[j * b:(j + 1) * b], tc, -1.0)
                dk_ref[sl, :] = ck[own, :] + _dot(dsp, q4, _TN)
                dv_ref[sl, :] = cv[own, :] + _dot(pp.astype(BF16), do4, _TN)
                ck[own, :] = _dot(dsc, q4, _TN)
                cv[own, :] = _dot(pc.astype(BF16), do4, _TN)

        @pl.when(n == nbn)
        def _():
            for r in range(dil):
                sl = _rows_of(r, dil)
                dk_ref[sl, :] = ck[r * b:(r + 1) * b, :]
                dv_ref[sl, :] = cv[r * b:(r + 1) * b, :]

    cur = lambda n: jnp.minimum(n, nbn - 1)
    prv = lambda n: jnp.maximum(n - 1, 0)
    q_specs, do_specs, hm_all, _, kc, kp, vc, vp, tab, stat = _nat_specs(g, dil, n_kv_all, cur, prv)
    anyspace = pl.BlockSpec(memory_space=pl.ANY)
    n_in = 2 * rep + 9
    return pl.pallas_call(
        body, name=name, grid=(n_kv, nbn + 1),
        in_specs=[*q_specs, *do_specs, stat, stat, kc, kp, vc, vp, tab, tab, tab, anyspace, anyspace, anyspace],
        out_specs=[hm_all, kp, kp], out_shape=[jax.ShapeDtypeStruct(a.shape, a.dtype) for a in grads],
        input_output_aliases={n_in: 0, n_in + 1: 1, n_in + 2: 2},
        scratch_shapes=[pltpu.VMEM((dil * b, hd), F32), pltpu.VMEM((dil * b, hd), F32)],
        compiler_params=_params(("parallel", "arbitrary"), VMEM_LIMIT_ATTN_BWD_BYTES))(
            *([q_all] * rep), *([do] * rep), lse, delta, k_rot, k_rot, kv, kv, *tabs, *grads)


def _adamw(g_slabs, w, m, v, name, row0=0, prior=None):
    kk, r, c = g_slabs.shape
    tile = r if r <= 256 else _pick_rows(r, 256)
    off = row0 // tile
    assert off * tile == row0

    def body(g_ref, w_ref, m_ref, v_ref, *rest):
        go_ref, d_ref, mo_ref, vo_ref = rest[-4:]
        g = g_ref[0].astype(F32)
        for k in range(1, kk):
            g = g + g_ref[k].astype(F32)
        m2 = ADAM_B1 * m_ref[...] + (1.0 - ADAM_B1) * g
        v2 = ADAM_B2 * v_ref[...] + (1.0 - ADAM_B2) * jnp.square(g)
        m_hat = m2 / (1.0 - ADAM_B1 ** ADAM_STEP)
        v_hat = v2 / (1.0 - ADAM_B2 ** ADAM_STEP)
        go_ref[...] = g
        d_ref[...] = -ADAM_LR * (m_hat / (jnp.sqrt(v_hat) + ADAM_EPS) + ADAM_WD * w_ref[...])
        mo_ref[...] = m2
        vo_ref[...] = v2

    spec = pl.BlockSpec((tile, c), lambda i: (i + off, 0))
    prior = list(prior) if prior is not None else []
    return pl.pallas_call(
        body, name=name, grid=(r // tile,),
        in_specs=[pl.BlockSpec((kk, tile, c), lambda i: (0, i, 0)), spec, spec, spec]
        + [pl.BlockSpec(memory_space=pl.ANY)] * len(prior),
        out_specs=[spec] * 4, out_shape=[jax.ShapeDtypeStruct(w.shape, F32)] * 4,
        input_output_aliases={4 + i: i for i in range(len(prior))},
        compiler_params=_params(("parallel",)))(g_slabs, w, m, v, *prior)


def _pick_rows(r, pref):
    t = (pref // 16) * 16
    while t >= 16:
        if r % t == 0:
            return t
        t -= 16
    return r


def _coords():
    return lax.axis_index("x"), lax.axis_index("y"), lax.axis_index("c")


def _dev_index(px, py, pc):
    return 4 * px + 2 * py + pc


def _all_gather(shards, name):
    na = len(shards)

    def body(*refs):
        ins, outs = refs[:na], refs[na:2 * na]
        send_sems, recv_sems, local_sems = refs[2 * na:]
        x, y, c = _coords()
        me, sibling = (x, y, c), (x, y, 1 - c)
        chips = [(1 - x, y), (x, 1 - y), (1 - x, 1 - y)]

        def copy(a, k, block, to, src=None):
            dst = outs[a].at[_dev_index(*block)]
            return pltpu.make_async_remote_copy(
                src_ref=dst if src is None else src, dst_ref=dst, send_sem=send_sems.at[a * 7 + k],
                recv_sem=recv_sems.at[a * 7 + k], device_id=to, device_id_type=MESH)

        mine = [pltpu.make_async_copy(ins[a], outs[a].at[_dev_index(*me)], local_sems.at[a]) for a in range(na)]
        for cp in mine:
            cp.start()
        first = []
        for a in range(na):
            first.append(copy(a, 0, me, sibling, src=ins[a]))
            first += [copy(a, 1 + j, me, (*chip, c), src=ins[a]) for j, chip in enumerate(chips)]
        for cp in first:
            cp.start()
        passed = []
        for j, chip in enumerate(chips):
            for a in range(na):
                copy(a, 1 + j, (*chip, c), me).wait_recv()
                cp = copy(a, 4 + j, (*chip, c), sibling)
                cp.start()
                passed.append(cp)
        for a in range(na):
            copy(a, 0, sibling, me).wait_recv()
            for j, chip in enumerate(chips):
                copy(a, 4 + j, (*chip, 1 - c), me).wait_recv()
        for cp in first + passed:
            cp.wait_send()
        for cp in mine:
            cp.wait()

    hbm = pl.BlockSpec(memory_space=pl.ANY)
    return pl.pallas_call(
        body, name=name, in_specs=[hbm] * na, out_specs=[hbm] * na,
        out_shape=[jax.ShapeDtypeStruct((NDEV,) + s.shape, s.dtype) for s in shards],
        scratch_shapes=[pltpu.SemaphoreType.DMA((7 * na,)), pltpu.SemaphoreType.DMA((7 * na,)),
                        pltpu.SemaphoreType.DMA((na,))])(*shards)


def _exchange(slabs, whole, name, after=()):
    ns, nw = len(slabs), len(whole)
    na = ns + nw
    nb = len(after)

    def body(*refs):
        ins, outs = refs[:na], refs[na + nb:2 * na + nb]
        send_sems, recv_sems, local_sems = refs[2 * na + nb:]
        x, y, c = _coords()
        me = _dev_index(x, y, c)

        def src_of(a, p):
            return ins[a].at[p] if a < ns else ins[a]

        def copy(a, k, peer):
            p = _dev_index(*peer)
            return pltpu.make_async_remote_copy(
                src_ref=src_of(a, p), dst_ref=outs[a].at[me], send_sem=send_sems.at[a * 7 + k - 1],
                recv_sem=recv_sems.at[a * 7 + k - 1], device_id=peer, device_id_type=MESH)

        def arrival(a, k, peer):
            p = _dev_index(*peer)
            return pltpu.make_async_remote_copy(
                src_ref=src_of(a, p), dst_ref=outs[a].at[p], send_sem=send_sems.at[a * 7 + k - 1],
                recv_sem=recv_sems.at[a * 7 + k - 1], device_id=peer, device_id_type=MESH)

        mine = [pltpu.make_async_copy(src_of(a, me), outs[a].at[me], local_sems.at[a]) for a in range(na)]
        for cp in mine:
            cp.start()
        peers = [(k, (x ^ (k >> 2), y ^ ((k >> 1) & 1), c ^ (k & 1))) for k in range(1, NDEV)]
        sent = [copy(a, k, peer) for k, peer in peers for a in range(na)]
        for cp in sent:
            cp.start()
        for k, peer in peers:
            for a in range(na):
                arrival(a, k, peer).wait_recv()
        for cp in sent:
            cp.wait_send()
        for cp in mine:
            cp.wait()

    hbm = pl.BlockSpec(memory_space=pl.ANY)
    out_shape = [jax.ShapeDtypeStruct(s.shape, s.dtype) for s in slabs]
    out_shape += [jax.ShapeDtypeStruct((NDEV,) + w.shape, w.dtype) for w in whole]
    return pl.pallas_call(
        body, name=name, in_specs=[hbm] * (na + nb), out_specs=[hbm] * na, out_shape=out_shape,
        scratch_shapes=[pltpu.SemaphoreType.DMA((7 * na,)), pltpu.SemaphoreType.DMA((7 * na,)),
                        pltpu.SemaphoreType.DMA((na,))])(*slabs, *whole, *after)


_HBM = pl.BlockSpec(memory_space=pltpu.HBM)
_SEM = pl.BlockSpec(memory_space=pltpu.SEMAPHORE)
_EFFECT = pltpu.SideEffectType.DATAFLOW_SIDE_EFFECTING


def _peers(x, y, c):
    return [(k, (x ^ (k >> 2), y ^ ((k >> 1) & 1), c ^ (k & 1))) for k in range(1, NDEV)]


def _peer_copy(src, land, send_sems, recv_sems, a, k, dst_block, peer):
    return pltpu.make_async_remote_copy(
        src_ref=src, dst_ref=land.at[dst_block], send_sem=send_sems.at[a * 7 + k - 1],
        recv_sem=recv_sems.at[a * 7 + k - 1], device_id=peer, device_id_type=MESH)


def _send_start(arrays, slabs, name):
    na = len(arrays)
    lands = [jax.ShapeDtypeStruct(a.shape if slabs else (NDEV,) + a.shape, a.dtype) for a in arrays]

    def body(*refs):
        ins, zones = refs[:na], refs[na:2 * na]
        send_sems, recv_sems = refs[2 * na], refs[2 * na + 1]
        token = refs[-1]
        x, y, c = _coords()
        me = _dev_index(x, y, c)
        for k, peer in _peers(x, y, c):
            for a in range(na):
                src = ins[a].at[_dev_index(*peer)] if slabs else ins[a]
                _peer_copy(src, zones[a], send_sems, recv_sems, a, k, me, peer).start()
        token[...] = jnp.zeros_like(token)

    outs = pl.pallas_call(
        body, name=name,
        out_shape=(pltpu.SemaphoreType.DMA((7 * na,)), pltpu.SemaphoreType.DMA((7 * na,)),
                   *[pltpu.HBM(a.shape, a.dtype) for a in arrays], *[pltpu.HBM(l.shape, l.dtype) for l in lands],
                   jax.ShapeDtypeStruct((8, LANES), F32)),
        in_specs=[_HBM] * (2 * na), out_specs=(_SEM, _SEM, *([_HBM] * (2 * na)), pl.BlockSpec(memory_space=pltpu.VMEM)),
        input_output_aliases={i: 2 + i for i in range(2 * na)},
        compiler_params=pltpu.CompilerParams(has_side_effects=_EFFECT),
    )(*[pltpu.with_memory_space_constraint(a, pltpu.HBM) for a in arrays],
      *[pltpu.with_memory_space_constraint(lax.empty(l.shape, l.dtype), pltpu.HBM) for l in lands])
    return outs[0], outs[1], list(outs[2:2 + na]), list(outs[2 + na:2 + 2 * na]), outs[-1]


def _send_wait(started, after, slabs, name):
    send_sems, recv_sems, thru, zones, _ = started
    na = len(thru)

    def body(*refs):
        ins, lands = refs[:na], refs[na:2 * na]
        s_sems, r_sems = refs[2 * na], refs[2 * na + 1]
        x, y, c = _coords()
        for k, peer in _peers(x, y, c):
            p = _dev_index(*peer)
            for a in range(na):
                src = ins[a].at[p] if slabs else ins[a]
                cp = _peer_copy(src, lands[a], s_sems, r_sems, a, k, p, peer)
                cp.wait_send()
                cp.wait_recv()

    outs = pl.pallas_call(
        body, name=name, out_shape=tuple(pltpu.HBM(v.shape, v.dtype) for v in thru + zones),
        in_specs=[_HBM] * (2 * na) + [_SEM, _SEM, pl.BlockSpec(memory_space=pl.ANY)], out_specs=tuple([_HBM] * (2 * na)),
        input_output_aliases={i: i for i in range(2 * na)},
        compiler_params=pltpu.CompilerParams(has_side_effects=_EFFECT),
    )(*thru, *zones, send_sems, recv_sems, after)
    me = _dev_index(*_coords())
    filled = []
    for a in range(na):
        own = lax.dynamic_index_in_dim(outs[a], me, 0, keepdims=False) if slabs else outs[a]
        filled.append(lax.dynamic_update_index_in_dim(outs[na + a], own, me, 0))
    return filled


def _pack(vecs):
    parts, spans, off = [], [], 0
    for v in vecs:
        n = v.size
        pad = (-n) % LANES
        parts.append(jnp.pad(v.reshape(-1).astype(F32), (0, pad)))
        spans.append((off, n))
        off += n + pad
    return jnp.concatenate(parts).reshape(-1, LANES), spans


def _pad_lanes(v):
    v = v.reshape(1, -1)
    return jnp.pad(v, ((0, 0), (0, LANES - v.shape[1])))


def _cols_to_slabs(g):
    sh = g.shape
    g = g.reshape(sh[:-1] + (NDEV, sh[-1] // NDEV))
    return jnp.moveaxis(g, -2, 0)


def _rows_to_slabs(g):
    sh = g.shape
    g = g.reshape(sh[:-2] + (NDEV, sh[-2] // NDEV, sh[-1]))
    return jnp.moveaxis(g, -3, 0)


def _slabs_to_cols(a):
    a = jnp.moveaxis(a, 0, -2)
    return a.reshape(a.shape[:-2] + (a.shape[-2] * a.shape[-1],))


def _slabs_to_rows(a):
    a = jnp.moveaxis(a, 0, -3)
    return a.reshape(a.shape[:-3] + (a.shape[-3] * a.shape[-2], a.shape[-1]))


def _ffn_forward(x, norm_w, w_up, fcw, wdown, tag):
    u, h = _mm(x, w_up, rms_fwd=norm_w, name=f"{tag}_up")
    f = _ffn_gate_fwd(u, fcw, f"{tag}_gate")
    return _mm(f, wdown, res=x, name=f"{tag}_down"), (h, u, f)


def _ffn_backward(x, saved, dout, dout_b, norm_w, w_up, fcw, wdown, tag):
    h, u, f = saved
    dwdown = _mm(f, dout_b, ta=True, name=f"{tag}_dwdown")
    df = _mm(dout_b, wdown, tb=True, name=f"{tag}_df")
    du, dfc = _ffn_gate_bwd(u, fcw, df, f"{tag}_gate_bwd")
    dwup = _mm(h, du, ta=True, name=f"{tag}_dwup")
    dx, dxb, dnorm = _mm(du, w_up, tb=True, rms_bwd=(x, norm_w, dout), tm=RMS_BWD_ROWS, name=f"{tag}_dh")
    return dx, dxb, (dwup, jnp.concatenate([dfc[0], dfc[1]], axis=1), dwdown, dnorm)


def kernel(x, a_norm, ssm_w_in, ssm_conv_w, ssm_conv_b, ssm_dt_bias, ssm_a_log, ssm_d, ssm_norm, ssm_w_out, kv_norm, w_kv, b_norm, att_w_q, att_w_o, ffn_norm, ffn_w_up, ffn_conv_w, ffn_w_down, final_norm, loss_target, m_a_norm, m_ssm_w_in, m_ssm_conv_w, m_ssm_conv_b, m_ssm_dt_bias, m_ssm_a_log, m_ssm_d, m_ssm_norm, m_ssm_w_out, m_kv_norm, m_w_kv, m_b_norm, m_att_w_q, m_att_w_o, m_ffn_norm, m_ffn_w_up, m_ffn_conv_w, m_ffn_w_down, m_final_norm, v_a_norm, v_ssm_w_in, v_ssm_conv_w, v_ssm_conv_b, v_ssm_dt_bias, v_ssm_a_log, v_ssm_d, v_ssm_norm, v_ssm_w_out, v_kv_norm, v_w_kv, v_b_norm, v_att_w_q, v_att_w_o, v_ffn_norm, v_ffn_w_up, v_ffn_conv_w, v_ffn_w_down, v_final_norm):
    given = dict(locals())
    xs, tgt = x[0], loss_target[0]
    s, d = xs.shape
    di = ssm_w_out.shape[1] * NDEV
    nh = ssm_dt_bias.shape[1]
    ng = SSM_N_GROUPS
    convd = di + 2 * ng * SSM_D_STATE
    f = ffn_w_down.shape[1] * NDEV
    n_att = len(ATT_PATTERNS)
    qg = ATT_HEADS_PER_GROUP * ATT_HEAD_DIM
    kg = ATT_KV_HEADS_PER_GROUP * ATT_HEAD_DIM
    kvd = n_att * kg
    assert all(w // dil == ATT_BLOCK for w, dil in ATT_PATTERNS)

    small, _ = _pack([a_norm, ssm_conv_w, ssm_conv_b, ssm_norm, ffn_conv_w])
    gat = _all_gather([ssm_w_in[0].astype(BF16), small], "gather_weights")
    first = _send_start([ssm_w_out[0].astype(BF16), ffn_w_up[0].astype(BF16), ffn_w_down[0].astype(BF16)], False,
                        "gather_ffn0_start")
    rest = _send_start([b.astype(BF16) for b in (w_kv, att_w_q[0], att_w_o[0], ffn_w_up[1], ffn_w_down[1])], False,
                       "gather_rest_start")
    w_in = _slabs_to_cols(gat[0])
    in_dim = di + convd + nh
    in_pad = di + convd + LANES
    w_in = jnp.pad(w_in, ((0, 0), (0, in_pad - in_dim)))
    sm = gat[1].reshape(NDEV, -1)
    o0 = 0

    def take(shape):
        nonlocal o0
        n = math.prod(shape)
        out = sm[:, o0:o0 + n].reshape((NDEV,) + shape)
        o0 += n + (-n) % LANES
        return out
    a_norm_f = _slabs_to_cols(take(a_norm.shape)) + (first[-1][0, 0] + rest[-1][0, 0])
    conv_w_f = _slabs_to_cols(take(ssm_conv_w.shape))[0]
    conv_b_f = _slabs_to_cols(take(ssm_conv_b.shape))
    ssm_norm_f = _slabs_to_cols(take(ssm_norm.shape))
    fcw = _slabs_to_cols(take(ffn_conv_w.shape))
    dtb, alog, dsk = _pad_lanes(ssm_dt_bias), _pad_lanes(ssm_a_log), _pad_lanes(ssm_d)
    kvn, fin = kv_norm.reshape(1, d), final_norm.reshape(1, d)

    zx, h0 = _mm(xs, w_in, rms_fwd=a_norm_f, name="in_proj")
    z, dtr = (zx, di, 0), (zx, LANES, (di + convd) // LANES)
    xbc = _conv_silu_fwd(zx, di, conv_w_f, conv_b_f, "ssm_conv")
    dt = _softplus_fwd(dtr, dtb, "ssm_dt")
    y, prevs = _ssd_fwd(xbc, dt, alog, dsk, di, nh, ng, "ssd")
    yn = _gnorm_fwd(y, z, ssm_norm_f, ng, "ssm_gnorm")
    got = _send_wait(first, yn, False, "gather_ffn0_wait")
    w_out = _slabs_to_rows(got[0])
    w_up0, w_down0 = _slabs_to_cols(got[1]), _slabs_to_rows(got[2])
    x1 = _mm(yn, w_out, res=xs, name="ssm_out")
    x2, ffn0 = _ffn_forward(x1, ffn_norm[0:1], w_up0, fcw[0], w_down0, "ffn0")
    got = _send_wait(rest, x2, False, "gather_rest_wait")
    w_kvf = _slabs_to_cols(got[0])
    w_q = _slabs_to_cols(got[1])
    w_o = _slabs_to_rows(got[2])
    w_up1, w_down1 = _slabs_to_cols(got[3]), _slabs_to_rows(got[4])
    kv, hk = _mm(x2, w_kvf, rms_fwd=kvn, name="kv_proj")
    q, h2 = _mm(x2, w_q, rms_fwd=b_norm, name="q_proj")
    tabs = _rot_tables(s, 1)
    k_rot = _rot_heads(kv, tabs, kvd, 1.0, F32, "k_rot")
    att = [_attn_fwd_nat(q, k_rot, kv, tabs, g, dil, f"attn{g}") for g, (_, dil) in enumerate(ATT_PATTERNS)]
    o, ob, lse = _merge_heads([t[0] for t in att], [t[1] for t in att], "attn_merge")
    x3 = _mm(ob, w_o, res=x2, name="attn_out")
    x4, ffn1 = _ffn_forward(x3, ffn_norm[1:2], w_up1, fcw[1], w_down1, "ffn1")
    loss_part, dx4, dx4b, dfin = _final_loss(x4, fin, tgt, "loss_head")

    dx3, dx3b, (dwup1, dfc1, dwdown1, dfn1) = _ffn_backward(
        x3, ffn1, dx4, dx4b, ffn_norm[1:2], w_up1, fcw[1], w_down1, "ffn1")
    dw_o = _mm(ob, dx3b, ta=True, name="attn_dwo")
    do = _mm(dx3b, w_o, tb=True, name="attn_do")
    delta = _delta_heads(do, o, "attn_delta")
    grads = (lax.empty((n_att * qg // LANES, s, LANES), F32), lax.empty((s, kvd), F32), lax.empty((s, kvd), F32))
    for g, (_, dil) in enumerate(ATT_PATTERNS):
        grads = _attn_bwd_nat(q, k_rot, kv, do, lse, delta, tabs, grads, g, dil, f"attn{g}_bwd")
    dq, dk_rot, dv = grads
    dkv = _kv_grad(dk_rot, dv, tabs, "kv_grad")
    dw_q = _mm(h2, dq, ta=True, b_heads=True, name="q_dw")
    dx2, _, db_norm = _mm(dq, w_q, tb=True, a_heads=True, rms_bwd=(x2, b_norm, dx3), tm=RMS_BWD_ROWS, name="q_dh")
    dw_kv = _mm(hk, dkv, ta=True, name="kv_dw")
    dx2, dx2b, dkv_norm = _mm(dkv, w_kvf, tb=True, rms_bwd=(x2, kvn, dx2), tm=RMS_BWD_ROWS, name="kv_dh")
    sent1 = _send_start([_cols_to_slabs(dwup1).astype(BF16), _rows_to_slabs(dwdown1).astype(BF16),
                         _cols_to_slabs(dw_kv).astype(BF16), _cols_to_slabs(dw_q).astype(BF16),
                         _rows_to_slabs(dw_o).astype(BF16)], True, "grads_late_start")
    dx1, dx1b, (dwup0, dfc0, dwdown0, dfn0) = _ffn_backward(
        x1, ffn0, dx2, dx2b, ffn_norm[0:1], w_up0, fcw[0] + sent1[-1][0, 0], w_down0, "ffn0")
    dw_out = _mm(yn, dx1b, ta=True, name="ssm_dwout")
    sent0 = _send_start([_cols_to_slabs(dwup0).astype(BF16), _rows_to_slabs(dwdown0).astype(BF16),
                         _rows_to_slabs(dw_out).astype(BF16)], True, "grads_ffn0_start")
    dyn = _mm(dx1b, w_out, tb=True, name="ssm_dyn")
    dzx = lax.empty((s, in_pad), BF16)
    dy, dzx, dssm_norm = _gnorm_bwd(dyn, y, z, ssm_norm_f + sent0[-1][0, 0], ng, (dzx, 0), "ssm_gnorm_bwd")
    dxbc, ddt, dalog, ddsk = _ssd_bwd(xbc, dt, alog, dsk, prevs, dy, di, nh, ng, "ssd_bwd")
    dzx, ddtb = _softplus_bwd(ddt, dtr, dtb, nh, (dzx, (di + convd) // LANES), "ssm_dt_bwd")
    dzx, dconv_w, dconv_b = _conv_silu_bwd(zx, di, conv_w_f, conv_b_f, dxbc, dzx, "ssm_conv_bwd")
    dw_in = _mm(h0, dzx, ta=True, name="in_dw")
    sent_m = _send_start([_cols_to_slabs(dw_in[:, :in_dim])], True, "grads_mamba_start")
    dx0, _, da_norm = _mm(dzx, w_in, tb=True, rms_bwd=(xs, a_norm_f + sent_m[-1][0, 0], dx1), tm=RMS_BWD_ROWS,
                          name="in_dh")

    small_full = {
        'a_norm': da_norm, 'ssm_conv_w': dconv_w[None], 'ssm_conv_b': dconv_b, 'ssm_dt_bias': ddtb[:, :nh],
        'ssm_a_log': dalog[:, :nh], 'ssm_d': ddsk[:, :nh], 'ssm_norm': dssm_norm, 'kv_norm': dkv_norm.reshape(d),
        'b_norm': db_norm, 'ffn_norm': jnp.concatenate([dfn0, dfn1], axis=0), 'ffn_conv_w': jnp.stack([dfc0, dfc1]),
        'final_norm': dfin.reshape(d),
    }
    small_names = list(small_full)
    packed, spans = _pack([small_full[n] for n in small_names])
    got1 = _send_wait(sent1, dx0, True, "grads_late_wait")
    got0 = _send_wait(sent0, dx0, True, "grads_ffn0_wait")
    recv_big = {'w_kv': [got1[2]], 'att_w_q': [got1[3]], 'att_w_o': [got1[4]], 'ffn_w_up': [got0[0], got1[0]],
                'ffn_w_down': [got0[1], got1[1]]}

    me = _dev_index(*_coords())
    res = {}

    def update_big(n, layers):
        w = given[n]
        c = w.shape[-1]
        outs, row0 = None, 0
        for k, r in enumerate(layers):
            g = r.reshape(NDEV, -1, c)
            outs = _adamw(g, w.reshape(-1, c), given['m_' + n].reshape(-1, c), given['v_' + n].reshape(-1, c),
                          f"adamw_{n}_{k}", row0=row0, prior=outs)
            row0 += g.shape[1]
        res[n] = [o_.reshape(w.shape) for o_ in outs]
    for n, r in recv_big.items():
        update_big(n, r)
    update_big('ssm_w_out', [got0[2]])
    recv = _exchange([], [packed], "exchange_grads", after=[res[n][1] for n in res])
    small_sum = _sum_slabs(recv[-1], "sum_small_grads").reshape(-1)
    gotm = _send_wait(sent_m, recv[-1], True, "grads_mamba_wait")
    update_big('ssm_w_in', [gotm[0]])
    sharded_small = {'a_norm', 'ssm_conv_w', 'ssm_conv_b', 'ssm_norm', 'ffn_conv_w'}
    for n, (off, size) in zip(small_names, spans):
        w = given[n]
        gfull = small_sum[off:off + size].reshape(small_full[n].shape)
        if n in sharded_small:
            c = w.shape[-1]
            gfull = lax.dynamic_slice_in_dim(gfull, me * c, c, axis=gfull.ndim - 1)
        c = w.shape[-1]
        outs = _adamw(gfull.reshape(1, -1, c), w.reshape(-1, c), given['m_' + n].reshape(-1, c),
                      given['v_' + n].reshape(-1, c), f"adamw_{n}")
        res[n] = [o_.reshape(w.shape) for o_ in outs]

    loss = lax.psum(loss_part[0, 0], AXES)
    return (loss, dx0[None], *[res[n][0] for n in WEIGHTS], *[res[n][1] for n in WEIGHTS],
            *[res[n][2] for n in WEIGHTS], *[res[n][3] for n in WEIGHTS])
```

```python
import functools
import math

import jax
import jax.numpy as jnp
from jax import lax
from jax.experimental import pallas as pl
from jax.experimental.pallas import tpu as pltpu

F32, BF16 = jnp.float32, jnp.bfloat16
AXES = ("x", "y", "c")
NDEV = 8
MESH = pl.DeviceIdType.MESH
HIGHEST = lax.Precision.HIGHEST

LANES = 128
SUBLANES = 8
VMEM_LIMIT_BYTES = 48 * 1024 * 1024
VMEM_LIMIT_ATTN_BWD_BYTES = 58 * 1024 * 1024
RMS_BWD_ROWS = 512

RMS_EPS = 1e-6
GATED_NORM_EPS = 1e-5
SSM_HEAD_DIM = 64
SSM_N_GROUPS = 8
SSM_D_STATE = 128
SSM_CONV = 4
SSM_CHUNK = 128
ATT_PATTERNS = ((128, 1), (512, 4), (2048, 16))
ATT_HEAD_DIM = 128
ATT_HEADS_PER_GROUP = 8
ATT_KV_HEADS_PER_GROUP = 2
ATT_BLOCK = 128
ROPE_DIM = ATT_HEAD_DIM // 4
ROPE_THETA = 500000.0
FFN_CONV = 3
ADAM_LR = 0.001
ADAM_B1 = 0.9
ADAM_B2 = 0.999
ADAM_EPS = 1e-08
ADAM_WD = 0.01
ADAM_STEP = 10
NEG = -1e30

WEIGHTS = ['a_norm', 'ssm_w_in', 'ssm_conv_w', 'ssm_conv_b', 'ssm_dt_bias', 'ssm_a_log', 'ssm_d', 'ssm_norm',
           'ssm_w_out', 'kv_norm', 'w_kv', 'b_norm', 'att_w_q', 'att_w_o', 'ffn_norm', 'ffn_w_up', 'ffn_conv_w',
           'ffn_w_down', 'final_norm']


def _params(sem=None, vmem=VMEM_LIMIT_BYTES):
    kw = dict(vmem_limit_bytes=vmem)
    if sem is not None:
        kw["dimension_semantics"] = sem
    return pltpu.CompilerParams(**kw)


def _pick(n, pref):
    if n <= pref:
        return n
    t = (pref // LANES) * LANES
    while t >= LANES:
        if n % t == 0:
            return t
        t -= LANES
    return n


def _dot(a, b, dims=(((1,), (0,)), ((), ())), precision=None):
    return lax.dot_general(a, b, dims, precision=precision, preferred_element_type=F32)


_NT = (((1,), (1,)), ((), ()))
_TN = (((0,), (0,)), ((), ()))


def _mm(a, b, *, ta=False, tb=False, res=None, out_dtype=None, name, tm=1408, tn=1408, tk=2048,
        a_heads=False, b_heads=False, rms_bwd=None, rms_fwd=None):
    assert not (a_heads and ta) and not (b_heads and tb)
    a_parts = a.ndim == 3 and not a_heads
    b_parts = b.ndim == 3 and not b_heads
    assert not (a_parts and ta) and not (b_parts and tb)
    if out_dtype is None:
        out_dtype = BF16 if ta else F32
    if a_heads:
        m, k = a.shape[1], a.shape[0] * LANES
    elif a_parts:
        m, k = a.shape[1], a.shape[0] * a.shape[2]
    else:
        m = a.shape[1] if ta else a.shape[0]
        k = a.shape[0] if ta else a.shape[1]
    if b_heads:
        n, kb = b.shape[0] * LANES, b.shape[1]
    elif b_parts:
        n, kb = b.shape[0] * b.shape[2], b.shape[1]
    else:
        n = b.shape[0] if tb else b.shape[1]
        kb = b.shape[1] if tb else b.shape[0]
    assert k == kb
    tm = _pick(m, tm)
    tn = _pick(b.shape[2], tn) if b_parts else _pick(n, tn)
    tk = _pick(a.shape[2], tk) if a_parts else _pick(k, tk)
    nk = k // tk
    if a_heads:
        a_spec = pl.BlockSpec((tk // LANES, tm, LANES), lambda i, j, l: (l, i, 0))
    elif a_parts:
        per = a.shape[2] // tk
        a_spec = pl.BlockSpec((None, tm, tk), lambda i, j, l: (l // per, i, l % per))
    elif ta:
        a_spec = pl.BlockSpec((tk, tm), lambda i, j, l: (l, i))
    else:
        a_spec = pl.BlockSpec((tm, tk), lambda i, j, l: (i, l))
    if b_heads:
        b_spec = pl.BlockSpec((tn // LANES, tk, LANES), lambda i, j, l: (j, l, 0))
    elif b_parts:
        per_n = b.shape[2] // tn
        b_spec = pl.BlockSpec((None, tk, tn), lambda i, j, l: (j // per_n, l, j % per_n))
    elif tb:
        b_spec = pl.BlockSpec((tn, tk), lambda i, j, l: (j, l))
    else:
        b_spec = pl.BlockSpec((tk, tn), lambda i, j, l: (l, j))
    if rms_fwd is not None:
        assert nk == 1 and a.ndim == 2 and b.ndim == 2 and not (ta or tb or a_heads or b_heads)
        assert res is None and rms_bwd is None

        def nbody(a_ref, b_ref, w_ref, o_ref, h_ref, hs):
            @pl.when(pl.program_id(1) == 0)
            def _():
                xv = a_ref[...]
                rs = lax.rsqrt(jnp.mean(xv * xv, axis=-1, keepdims=True) + RMS_EPS)
                hv = (xv * rs * w_ref[...]).astype(BF16)
                hs[...] = hv
                h_ref[...] = hv

            o_ref[...] = _dot(hs[...], b_ref[...].astype(BF16)).astype(o_ref.dtype)

        rows = pl.BlockSpec((tm, k), lambda i, j: (i, 0))
        return pl.pallas_call(
            nbody, name=name, grid=(m // tm, n // tn),
            in_specs=[rows, pl.BlockSpec((k, tn), lambda i, j: (0, j)), pl.BlockSpec((1, k), lambda i, j: (0, 0))],
            out_specs=[pl.BlockSpec((tm, tn), lambda i, j: (i, j)), rows],
            out_shape=[jax.ShapeDtypeStruct((m, n), out_dtype), jax.ShapeDtypeStruct((m, k), BF16)],
            scratch_shapes=[pltpu.VMEM((tm, k), BF16)],
            compiler_params=_params(("parallel", "arbitrary")))(a, b, rms_fwd)
    o_spec = pl.BlockSpec((tm, tn), lambda i, j, l: (i, j))
    dims = (((0 if ta else 1,), (1 if tb else 0,)), ((), ()))
    has_res = res is not None
    has_rms = rms_bwd is not None
    assert not (has_res and has_rms) and (not has_rms or tn == n)
    n_extra = 3 if has_rms else int(has_res)
    n_out = 3 if has_rms else 1

    def load(ref, heads):
        if not heads:
            return ref[...].astype(BF16)
        return jnp.concatenate([ref[i].astype(BF16) for i in range(ref.shape[0])], axis=1)

    def body(*refs):
        a_ref, b_ref = refs[:2]
        extra = refs[2:2 + n_extra]
        outs = refs[2 + n_extra:2 + n_extra + n_out]
        p = _dot(load(a_ref, a_heads), load(b_ref, b_heads), dims)

        def finish(r):
            if has_res:
                r = r + extra[0][...]
            if not has_rms:
                outs[0][...] = r.astype(outs[0].dtype)
                return
            x_ref, w_ref, dres_ref = extra
            dx_ref, dxb_ref, dw_ref = outs
            xv = x_ref[...]
            rs = lax.rsqrt(jnp.mean(xv * xv, axis=-1, keepdims=True) + RMS_EPS)
            xh = xv * rs
            dxh = r * w_ref[...]
            dx = dres_ref[...] + rs * (dxh - xh * jnp.mean(dxh * xh, axis=-1, keepdims=True))
            dx_ref[...] = dx
            dxb_ref[...] = dx.astype(BF16)

            @pl.when(pl.program_id(0) == 0)
            def _():
                dw_ref[...] = jnp.zeros(dw_ref.shape, F32)

            dw_ref[...] += jnp.sum(r * xh, axis=0, keepdims=True)

        if nk == 1:
            finish(p)
            return
        acc = refs[2 + n_extra + n_out]
        l = pl.program_id(2)

        @pl.when(l == 0)
        def _():
            acc[...] = p

        @pl.when(jnp.logical_and(l > 0, l < nk - 1))
        def _():
            acc[...] += p

        @pl.when(l == nk - 1)
        def _():
            finish(acc[...] + p)

    scratch = [pltpu.VMEM((tm, tn), F32)] if nk > 1 else []
    if has_rms:
        x, w, dres = rms_bwd
        vec = pl.BlockSpec((1, n), lambda i, j, l: (0, 0))
        return pl.pallas_call(
            body, name=name, grid=(m // tm, 1, nk), in_specs=[a_spec, b_spec, o_spec, vec, o_spec],
            out_specs=[o_spec, o_spec, vec], scratch_shapes=scratch,
            out_shape=[jax.ShapeDtypeStruct((m, n), F32), jax.ShapeDtypeStruct((m, n), BF16),
                       jax.ShapeDtypeStruct((1, n), F32)],
            compiler_params=_params(("arbitrary", "arbitrary", "arbitrary")))(a, b, x, w, dres)
    ins = [a, b] + ([res] if has_res else [])
    in_specs = [a_spec, b_spec] + ([o_spec] if has_res else [])
    return pl.pallas_call(
        body, name=name, grid=(m // tm, n // tn, nk), in_specs=in_specs, out_specs=o_spec,
        out_shape=jax.ShapeDtypeStruct((m, n), out_dtype), scratch_shapes=scratch,
        compiler_params=_params(("parallel", "parallel", "arbitrary")))(*ins)


def _rowwise(fn, rows, bcasts, outs, accs=(), *, tile, name):
    s = (rows[0][0] if isinstance(rows[0], tuple) else rows[0]).shape[-2]
    tile = min(tile, s)
    n_val = len(rows) + len(bcasts)
    intos = [(k, o) for k, o in enumerate(outs) if len(o) == 4]
    n_in, n_out = n_val + len(intos), len(outs)

    def row_spec(c):
        if isinstance(c, tuple):
            return pl.BlockSpec((c[0], tile, c[1]), lambda i: (0, i, 0))
        return pl.BlockSpec((tile, c), lambda i: (i, 0))

    def row_shape(c):
        return (c[0], s, c[1]) if isinstance(c, tuple) else (s, c)

    def window(width, cb):
        return pl.BlockSpec((tile, width), lambda i: (i, cb))

    def body(*refs):
        vals = fn(*[r[...] for r in refs[:n_val]])
        o_refs = refs[n_in:n_in + n_out]
        a_refs = refs[n_in + n_out:]
        for r, v in zip(o_refs, vals[:n_out]):
            if isinstance(v, list):
                for i, vi in enumerate(v):
                    r[i] = vi.astype(r.dtype)
            else:
                r[...] = v.astype(r.dtype)

        @pl.when(pl.program_id(0) == 0)
        def _():
            for r in a_refs:
                r[...] = jnp.zeros(r.shape, r.dtype)

        for r, v in zip(a_refs, vals[n_out:]):
            r[...] += v

    in_specs = [window(r[1], r[2]) if isinstance(r, tuple)
                else row_spec(r.shape[1] if r.ndim == 2 else (r.shape[0], r.shape[2])) for r in rows]
    in_specs += [pl.BlockSpec(b.shape, lambda i: (0, 0)) for b in bcasts]
    in_specs += [pl.BlockSpec(memory_space=pl.ANY) for _ in intos]
    out_specs = [window(o[0], o[3]) if len(o) == 4 else row_spec(o[0]) for o in outs]
    out_specs += [pl.BlockSpec(sh, lambda i: (0, 0)) for sh, _ in accs]
    out_shape = [jax.ShapeDtypeStruct(o[2].shape, o[2].dtype) if len(o) == 4
                 else jax.ShapeDtypeStruct(row_shape(o[0]), o[1]) for o in outs]
    out_shape += [jax.ShapeDtypeStruct(sh, dt) for sh, dt in accs]
    args = [r[0] if isinstance(r, tuple) else r for r in rows] + list(bcasts) + [o[2] for _, o in intos]
    return pl.pallas_call(body, name=name, grid=(s // tile,), in_specs=in_specs, out_specs=out_specs,
                          out_shape=out_shape, input_output_aliases={n_val + i: k for i, (k, _) in enumerate(intos)},
                          compiler_params=_params(("arbitrary",)))(*args)


def _final_loss(x, w, tgt, name):
    d = x.shape[1]

    def fn(x, t, w):
        r = lax.rsqrt(jnp.mean(x * x, axis=-1, keepdims=True) + RMS_EPS)
        xh = x * r
        err = xh * w - t
        part = jnp.sum(jnp.mean(err * err, axis=-1, keepdims=True), axis=0, keepdims=True) * 0.5
        dy = err * (1.0 / d)
        dxh = dy * w
        dx = r * (dxh - xh * jnp.mean(dxh * xh, axis=-1, keepdims=True))
        return dx, dx, part, jnp.sum(dy * xh, axis=0, keepdims=True)
    dx, dxb, part, dw = _rowwise(fn, [x, tgt], [w], [(d, F32), (d, BF16)], [((1, 1), F32), ((1, d), F32)],
                                 tile=256, name=name)
    return part, dx, dxb, dw


def _softplus_fwd(dtr, bias, name):
    def fn(r, b):
        v = r + b
        return (jnp.maximum(v, 0.0) + jnp.log(1.0 + jnp.exp(-jnp.abs(v))),)
    return _rowwise(fn, [dtr], [bias], [(LANES, F32)], tile=512, name=name)[0]


def _softplus_bwd(ddt, dtr, bias, n_heads, into, name):
    def fn(g, r, b):
        lane = lax.broadcasted_iota(jnp.int32, g.shape, 1)
        d = jnp.where(lane < n_heads, g * jax.nn.sigmoid(r + b), 0.0)
        return d, jnp.sum(d, axis=0, keepdims=True)
    return _rowwise(fn, [ddt, dtr], [bias], [(LANES, BF16, *into)], [((1, LANES), F32)], tile=512, name=name)


def _gnorm_fwd(y, z, w, n_groups, name):
    di = y.shape[1]
    gs = di // n_groups

    def fn(y, z, w):
        y2 = y * (z * jax.nn.sigmoid(z))
        out = []
        for g in range(n_groups):
            sl = y2[:, g * gs:(g + 1) * gs]
            r = lax.rsqrt(jnp.mean(sl * sl, axis=-1, keepdims=True) + GATED_NORM_EPS)
            out.append(sl * r)
        return (jnp.concatenate(out, axis=1) * w,)
    return _rowwise(fn, [y, z], [w], [(di, BF16)], tile=256, name=name)[0]


def _gnorm_bwd(dyn, y, z, w, n_groups, into, name):
    di = y.shape[1]
    gs = di // n_groups

    def fn(dyn, y, z, w):
        sig = jax.nn.sigmoid(z)
        sz = z * sig
        y2 = y * sz
        d2n = dyn * w
        dy2, yhat = [], []
        for g in range(n_groups):
            sl = y2[:, g * gs:(g + 1) * gs]
            dg = d2n[:, g * gs:(g + 1) * gs]
            r = lax.rsqrt(jnp.mean(sl * sl, axis=-1, keepdims=True) + GATED_NORM_EPS)
            yh = sl * r
            dy2.append(r * (dg - yh * jnp.mean(dg * yh, axis=-1, keepdims=True)))
            yhat.append(yh)
        dy2 = jnp.concatenate(dy2, axis=1)
        yhat = jnp.concatenate(yhat, axis=1)
        dz = dy2 * y * (sig * (1.0 + z * (1.0 - sig)))
        return dy2 * sz, dz, jnp.sum(dyn * yhat, axis=0, keepdims=True)
    return _rowwise(fn, [dyn, y, z], [w], [(di, F32), (di, BF16, *into)], [((1, di), F32)], tile=256, name=name)


def _lane_place(cols):
    rows = cols[0].shape[0]
    lane = lax.broadcasted_iota(jnp.int32, (rows, LANES), 1)
    out = jnp.zeros((rows, LANES), F32)
    for j, c in enumerate(cols):
        out = jnp.where(lane == j, c, out)
    return out


def _merge_heads(os_, lses, name):
    n = len(os_)
    n_kv, rep, hd = ATT_KV_HEADS_PER_GROUP, ATT_HEADS_PER_GROUP // ATT_KV_HEADS_PER_GROUP, ATT_HEAD_DIM

    def fn(*v):
        o, l = v[:n], v[n:]
        out, lse = [], []
        for h in range(n_kv):
            cols = []
            for j in range(rep):
                hh = h * rep + j
                lg = [li[h][:, j:j + 1] for li in l]
                m = functools.reduce(jnp.maximum, lg)
                e = [jnp.exp(x - m) for x in lg]
                tot = functools.reduce(jnp.add, e)
                acc = functools.reduce(jnp.add, [ei * oi[hh] for ei, oi in zip(e, o)])
                out.append(acc / tot)
                cols.append(m + jnp.log(tot))
            lse.append(_lane_place(cols))
        merged = jnp.concatenate(out, axis=1)
        return merged, merged, lse
    c = os_[0].shape[0] * hd
    return _rowwise(fn, list(os_) + list(lses), [], [(c, F32), (c, BF16), ((n_kv, LANES), F32)], tile=256, name=name)


def _delta_heads(do, o, name):
    n_kv, rep, hd = ATT_KV_HEADS_PER_GROUP, ATT_HEADS_PER_GROUP // ATT_KV_HEADS_PER_GROUP, ATT_HEAD_DIM

    def fn(do, o):
        p = do * o
        return ([_lane_place([jnp.sum(p[:, (h * rep + j) * hd:(h * rep + j + 1) * hd], axis=-1, keepdims=True)
                              for j in range(rep)]) for h in range(n_kv)],)
    return _rowwise(fn, [do, o], [], [((n_kv, LANES), F32)], tile=256, name=name)[0]


def _sum_slabs(recv, name):
    def body(r_ref, o_ref):
        acc = r_ref[0]
        for k in range(1, NDEV):
            acc = acc + r_ref[k]
        o_ref[...] = acc
    return pl.pallas_call(body, name=name, out_shape=jax.ShapeDtypeStruct(recv.shape[1:], F32),
                          compiler_params=_params())(recv)


def _shift_down(x, k):
    if k == 0:
        return x
    r = pltpu.roll(x, k, 0)
    row = lax.broadcasted_iota(jnp.int32, (SUBLANES, x.shape[1]), 0)
    return jnp.concatenate([jnp.where(row >= k, r[:SUBLANES], 0.0), r[SUBLANES:]], axis=0)


def _shift_up(x, k):
    if k == 0:
        return x
    s = x.shape[0]
    r = pltpu.roll(x, s - k, 0)
    row = lax.broadcasted_iota(jnp.int32, (SUBLANES, x.shape[1]), 0)
    return jnp.concatenate([r[:s - SUBLANES], jnp.where(row < SUBLANES - k, r[s - SUBLANES:], 0.0)], axis=0)


def _conv(x, w):
    kw = w.shape[0]
    return functools.reduce(jnp.add, [w[k:k + 1, :] * _shift_down(x, kw - 1 - k) for k in range(kw)])


def _conv_t(dy, w):
    kw = w.shape[0]
    return functools.reduce(jnp.add, [w[k:k + 1, :] * _shift_up(dy, kw - 1 - k) for k in range(kw)])


def _conv_dw(x, dy, dw_ref):
    kw = dw_ref.shape[0]
    for k in range(kw):
        dw_ref[k:k + 1, :] = jnp.sum(dy * _shift_down(x, kw - 1 - k), axis=0, keepdims=True)


def _dsilu(pre):
    sig = jax.nn.sigmoid(pre)
    return sig * (1.0 + pre * (1.0 - sig))


def _col_specs(s, c, kw, tc):
    return (pl.BlockSpec((s, tc), lambda j: (0, j)), pl.BlockSpec((kw, tc), lambda j: (0, j)),
            pl.BlockSpec((1, tc), lambda j: (0, j)))


def _conv_silu_fwd(x, col0, w, b, name):
    s, c = x.shape[0], w.shape[1]
    tc = LANES
    xs, ws, bs = _col_specs(s, c, w.shape[0], tc)
    xwin = pl.BlockSpec((s, tc), lambda j: (0, j + col0 // tc))

    def body(x_ref, w_ref, b_ref, o_ref):
        pre = _conv(x_ref[...], w_ref[...]) + b_ref[...]
        o_ref[...] = pre * jax.nn.sigmoid(pre)
    return pl.pallas_call(body, name=name, grid=(c // tc,), in_specs=[xwin, ws, bs], out_specs=xs,
                          out_shape=jax.ShapeDtypeStruct((s, c), F32), compiler_params=_params(("parallel",)))(x, w, b)


def _conv_silu_bwd(x, col0, w, b, dy, into, name):
    s, c = x.shape[0], w.shape[1]
    tc = LANES
    xs, ws, bs = _col_specs(s, c, w.shape[0], tc)
    xwin = pl.BlockSpec((s, tc), lambda j: (0, j + col0 // tc))

    def body(x_ref, w_ref, b_ref, dy_ref, _, dx_ref, dw_ref, db_ref):
        xv, wv = x_ref[...], w_ref[...]
        pre = _conv(xv, wv) + b_ref[...]
        dpre = dy_ref[...] * _dsilu(pre)
        dx_ref[...] = _conv_t(dpre, wv).astype(dx_ref.dtype)
        _conv_dw(xv, dpre, dw_ref)
        db_ref[...] = jnp.sum(dpre, axis=0, keepdims=True)
    return pl.pallas_call(
        body, name=name, grid=(c // tc,), in_specs=[xwin, ws, bs, xs, pl.BlockSpec(memory_space=pl.ANY)],
        out_specs=[xwin, ws, bs], input_output_aliases={4: 0},
        out_shape=[jax.ShapeDtypeStruct(into.shape, into.dtype), jax.ShapeDtypeStruct(w.shape, F32),
                   jax.ShapeDtypeStruct((1, c), F32)],
        compiler_params=_params(("parallel",)))(x, w, b, dy, into)


def _gate_specs(s, f, kw):
    nt = f // LANES
    return (pl.BlockSpec((s, LANES), lambda j: (0, j)), pl.BlockSpec((s, LANES), lambda j: (0, j + nt)),
            pl.BlockSpec((kw, LANES), lambda j: (0, j)), pl.BlockSpec((kw, LANES), lambda j: (0, j + nt)))


def _ffn_gate_fwd(u, w, name):
    s, f = u.shape[0], u.shape[1] // 2
    gs, vs, wgs, wvs = _gate_specs(s, f, w.shape[0])

    def body(g_ref, v_ref, wg_ref, wv_ref, o_ref):
        g = _conv(g_ref[...], wg_ref[...])
        v = _conv(v_ref[...], wv_ref[...])
        o_ref[...] = (g * jax.nn.sigmoid(g) * v).astype(o_ref.dtype)
    return pl.pallas_call(body, name=name, grid=(f // LANES,), in_specs=[gs, vs, wgs, wvs], out_specs=gs,
                          out_shape=jax.ShapeDtypeStruct((s, f), BF16),
                          compiler_params=_params(("parallel",)))(u, u, w, w)


def _ffn_gate_bwd(u, w, df, name):
    s, f = u.shape[0], u.shape[1] // 2
    kw = w.shape[0]
    gs, vs, wgs, wvs = _gate_specs(s, f, kw)

    def body(g_ref, v_ref, wg_ref, wv_ref, df_ref, du_ref, dw_ref):
        gp, vp, wgv, wvv = g_ref[...], v_ref[...], wg_ref[...], wv_ref[...]
        g = _conv(gp, wgv)
        v = _conv(vp, wvv)
        dfv = df_ref[...]
        dg = dfv * v * _dsilu(g)
        dv = dfv * (g * jax.nn.sigmoid(g))
        du_ref[0] = _conv_t(dg, wgv).astype(du_ref.dtype)
        du_ref[1] = _conv_t(dv, wvv).astype(du_ref.dtype)
        _conv_dw(gp, dg, dw_ref.at[0])
        _conv_dw(vp, dv, dw_ref.at[1])
    return pl.pallas_call(
        body, name=name, grid=(f // LANES,), in_specs=[gs, vs, wgs, wvs, gs],
        out_specs=[pl.BlockSpec((2, s, LANES), lambda j: (0, 0, j)), pl.BlockSpec((2, kw, LANES), lambda j: (0, 0, j))],
        out_shape=[jax.ShapeDtypeStruct((2, s, f), BF16), jax.ShapeDtypeStruct((2, kw, f), F32)],
        compiler_params=_params(("parallel",)))(u, u, w, w, df)


def _ssd_common(dt, alog, n_heads):
    ln = dt.shape[0]
    lane = lax.broadcasted_iota(jnp.int32, (1, LANES), 1)
    a = jnp.where(lane < n_heads, -jnp.exp(alog), 0.0)
    row = lax.broadcasted_iota(jnp.int32, (ln, ln), 0)
    col = lax.broadcasted_iota(jnp.int32, (ln, ln), 1)
    tril = col <= row
    acs = _dot(tril.astype(F32), dt * a, precision=HIGHEST)
    return a, acs, acs.T, tril


def _split(x, n):
    out = []
    for _ in range(n):
        piece = x.astype(BF16)
        out.append(piece)
        x = x - piece.astype(F32)
    return out


def _spread(x, onehot, n=2):
    return functools.reduce(jnp.add, [_dot(piece, onehot) for piece in _split(x, n)])


def _head_maps(di, p):
    e = (jnp.arange(di, dtype=jnp.int32)[None, :] // p == jnp.arange(LANES, dtype=jnp.int32)[:, None]).astype(BF16)
    return e, e.T


def _ssd_wide(dt, acs, acs_t, dskip, e_ref, et_ref):
    ln = dt.shape[0]
    last = acs[ln - 1:ln, :]
    stack = jnp.concatenate([dt, jnp.exp(acs), jnp.exp(last - acs), jnp.broadcast_to(dskip, (8, LANES))], axis=0)
    wide = _spread(stack, e_ref[...])
    tb = jnp.exp(jnp.broadcast_to(acs_t[:, ln - 1:ln], (LANES, LANES)))
    texp = functools.reduce(jnp.add, [_dot(et_ref[...], piece) for piece in _split(tb, 3)])
    return wide[:ln], wide[ln:2 * ln], wide[2 * ln:3 * ln], wide[3 * ln:3 * ln + 1], texp


def _ssd_fwd(xbc, dt, alog, dskip, di, n_heads, n_groups, name):
    s, convd = xbc.shape
    ln, p, ns = SSM_CHUNK, SSM_HEAD_DIM, SSM_D_STATE
    nc, hg = s // ln, n_heads // n_groups
    gw = hg * p
    e64, e64t = _head_maps(di, p)

    def body(x_ref, dt_ref, alog_ref, d_ref, e_ref, et_ref, y_ref, prev_ref, st):
        @pl.when(pl.program_id(0) == 0)
        def _():
            st[...] = jnp.zeros(st.shape, F32)

        dt = dt_ref[...]
        _, acs, acs_t, tril = _ssd_common(dt, alog_ref[...], n_heads)
        dte, ee, dse, dske, texp = _ssd_wide(dt, acs, acs_t, d_ref[...], e_ref, et_ref)
        x = x_ref[:, :di]
        xdt = x * dte
        xdtb = xdt.astype(BF16)
        xdsb = (xdt * dse).astype(BF16)
        for g in range(n_groups):
            rows = slice(g * gw, (g + 1) * gw)
            bg = x_ref[:, di + g * ns:di + (g + 1) * ns].astype(BF16)
            cg = x_ref[:, di + (n_groups + g) * ns:di + (n_groups + g + 1) * ns].astype(BF16)
            gm = _dot(cg, bg, _NT)
            prev = st[rows, :]
            prev_ref[0, rows, :] = prev
            yo = _dot(cg, prev.astype(BF16), _NT)
            for j in range(hg):
                h = g * hg + j
                seg = acs[:, h:h + 1] - acs_t[h:h + 1, :]
                m = jnp.where(tril, gm * jnp.exp(jnp.where(tril, seg, 0.0)), 0.0)
                y_ref[:, h * p:(h + 1) * p] = _dot(m.astype(BF16), xdtb[:, h * p:(h + 1) * p])
            y_ref[:, rows] = y_ref[:, rows] + yo * ee[:, rows] + x[:, rows] * dske[:, rows]
            st[rows, :] = prev * texp[rows, :] + _dot(xdsb[:, rows], bg, _TN)

    vec = pl.BlockSpec((1, LANES), lambda c: (0, 0))
    return pl.pallas_call(
        body, name=name, grid=(nc,),
        in_specs=[pl.BlockSpec((ln, convd), lambda c: (c, 0)), pl.BlockSpec((ln, LANES), lambda c: (c, 0)), vec, vec,
                  pl.BlockSpec(e64.shape, lambda c: (0, 0)), pl.BlockSpec(e64t.shape, lambda c: (0, 0))],
        out_specs=[pl.BlockSpec((ln, di), lambda c: (c, 0)), pl.BlockSpec((1, di, ns), lambda c: (c, 0, 0))],
        out_shape=[jax.ShapeDtypeStruct((s, di), F32), jax.ShapeDtypeStruct((nc, di, ns), F32)],
        scratch_shapes=[pltpu.VMEM((di, ns), F32)],
        compiler_params=_params(("arbitrary",)))(xbc, dt, alog, dskip, e64, e64t)


def _ssd_bwd(xbc, dt, alog, dskip, prev_all, dy, di, n_heads, n_groups, name):
    s, convd = xbc.shape
    ln, p, ns = SSM_CHUNK, SSM_HEAD_DIM, SSM_D_STATE
    nc, hg = s // ln, n_heads // n_groups
    gw = hg * p
    e64, e64t = _head_maps(di, p)

    def body(x_ref, dt_ref, alog_ref, d_ref, e_ref, et_ref, prev_ref, dy_ref,
             dx_ref, ddt_ref, da_ref, dd_ref, dh, yo_ref, w_ref):
        step = pl.program_id(0)

        @pl.when(step == 0)
        def _():
            dh[...] = jnp.zeros(dh.shape, F32)
            da_ref[...] = jnp.zeros(da_ref.shape, F32)
            dd_ref[...] = jnp.zeros(dd_ref.shape, F32)

        dt = dt_ref[...]
        a, acs, acs_t, tril = _ssd_common(dt, alog_ref[...], n_heads)
        dte, ee, dse, dske, texp = _ssd_wide(dt, acs, acs_t, d_ref[...], e_ref, et_ref)
        row = lax.broadcasted_iota(jnp.int32, (ln, ln), 0)
        col = lax.broadcasted_iota(jnp.int32, (ln, ln), 1)
        triu = col >= row
        x = x_ref[:, :di]
        dy = dy_ref[...]
        xdt = x * dte
        xdtb = xdt.astype(BF16)
        xdsb = (xdt * dse).astype(BF16)
        dyb = dy.astype(BF16)
        dyob = (dy * ee).astype(BF16)
        dhn = dh[...]
        dhb = dhn.astype(BF16)
        per_head = functools.reduce(jnp.add, [_dot(e_ref[...], piece) for piece in _split(dhn * prev_ref[0], 2)])
        ones8 = jnp.ones((8, LANES), BF16)
        dtt = functools.reduce(jnp.add, [_dot(ones8, piece, _NT) for piece in _split(per_head, 2)])[0:1]
        dacs_c = jnp.zeros((ln, LANES), F32)
        dacs_r = jnp.zeros((LANES, ln), F32)
        for g in range(n_groups):
            rows = slice(g * gw, (g + 1) * gw)
            bg = x_ref[:, di + g * ns:di + (g + 1) * ns].astype(BF16)
            cg = x_ref[:, di + (n_groups + g) * ns:di + (n_groups + g + 1) * ns].astype(BF16)
            gmt = _dot(bg, cg, _NT)
            prevb = prev_ref[0, rows, :].astype(BF16)
            dcg = _dot(dyob[:, rows], prevb)
            dh[rows, :] = texp[rows, :] * dhn[rows, :] + _dot(dyob[:, rows], cg, _TN)
            w = _dot(bg, dhb[rows, :], _NT)
            dbg = _dot(xdsb[:, rows], dhb[rows, :])
            yo_ref[:, rows] = _dot(cg, prevb, _NT)
            w_ref[:, rows] = w
            dgmt = jnp.zeros((ln, ln), F32)
            q_hi, q_lo = [], []
            for j in range(hg):
                h = g * hg + j
                segt = acs_t[h:h + 1, :] - acs[:, h:h + 1]
                dect = jnp.where(triu, jnp.exp(jnp.where(triu, segt, 0.0)), 0.0)
                dyh, xh = dyb[:, h * p:(h + 1) * p], xdtb[:, h * p:(h + 1) * p]
                mt = gmt * dect
                dmt = _dot(xh, dyh, _NT)
                dx_ref[:, h * p:(h + 1) * p] = _dot(mt.astype(BF16), dyh)
                dgmt = dgmt + dmt * dect
                hi, lo = _split(dmt * mt, 2)
                q_hi.append(hi)
                q_lo.append(lo)
            sel_c = (lax.broadcasted_iota(jnp.int32, (hg * ln, LANES), 1)
                     == g * hg + lax.broadcasted_iota(jnp.int32, (hg * ln, LANES), 0) // ln).astype(BF16)
            sel_r = (lax.broadcasted_iota(jnp.int32, (LANES, hg * ln), 0)
                     == g * hg + lax.broadcasted_iota(jnp.int32, (LANES, hg * ln), 1) // ln).astype(BF16)
            for pieces in (q_hi, q_lo):
                dacs_c = dacs_c - _dot(jnp.concatenate(pieces, axis=1), sel_c)
                dacs_r = dacs_r + _dot(sel_r, jnp.concatenate(pieces, axis=0))
            dgb = dgmt.astype(BF16)
            dx_ref[:, di + g * ns:di + (g + 1) * ns] = dbg + _dot(dgb, cg)
            dx_ref[:, di + (n_groups + g) * ns:di + (n_groups + g + 1) * ns] = dcg + _dot(dgb, bg, _TN)

        wds = w_ref[...] * dse
        dxdt = dx_ref[:, :di] + wds
        red = _spread(jnp.concatenate([dxdt * x, dy * yo_ref[...] * ee, xdt * wds, dy * x], axis=0), et_ref[...], n=1)
        ddt_x, r_off, r_state, ddr = red[:ln], red[ln:2 * ln], red[2 * ln:3 * ln], red[3 * ln:]
        dx_ref[:, :di] = dxdt * dte + dy * dske
        rowi = lax.broadcasted_iota(jnp.int32, (ln, LANES), 0)
        dlast = jnp.sum(r_state, axis=0, keepdims=True) + dtt * jnp.exp(acs[ln - 1:ln, :])
        dacs = r_off - r_state + dacs_c + dacs_r.T + jnp.where(rowi == ln - 1, dlast, 0.0)
        dadt = _dot(triu.astype(F32), dacs, precision=HIGHEST)
        ddt_ref[...] = dadt * a + ddt_x
        da_ref[...] += jnp.sum(dadt * dt, axis=0, keepdims=True)
        dd_ref[...] += jnp.sum(ddr, axis=0, keepdims=True)

        @pl.when(step == nc - 1)
        def _():
            da_ref[...] = da_ref[...] * a

    vec = pl.BlockSpec((1, LANES), lambda c: (0, 0))
    rev = lambda c: (nc - 1 - c, 0)
    return pl.pallas_call(
        body, name=name, grid=(nc,),
        in_specs=[pl.BlockSpec((ln, convd), rev), pl.BlockSpec((ln, LANES), rev), vec, vec,
                  pl.BlockSpec(e64.shape, lambda c: (0, 0)), pl.BlockSpec(e64t.shape, lambda c: (0, 0)),
                  pl.BlockSpec((1, di, ns), lambda c: (nc - 1 - c, 0, 0)), pl.BlockSpec((ln, di), rev)],
        out_specs=[pl.BlockSpec((ln, convd), rev), pl.BlockSpec((ln, LANES), rev), vec, vec],
        out_shape=[jax.ShapeDtypeStruct((s, convd), F32), jax.ShapeDtypeStruct((s, LANES), F32),
                   jax.ShapeDtypeStruct((1, LANES), F32), jax.ShapeDtypeStruct((1, LANES), F32)],
        scratch_shapes=[pltpu.VMEM((di, ns), F32), pltpu.VMEM((ln, di), F32), pltpu.VMEM((ln, di), F32)],
        compiler_params=_params(("arbitrary",)))(xbc, dt, alog, dskip, e64, e64t, prev_all, dy)


def _rot_tables(s):
    half = ROPE_DIM // 2
    inv_freq = jnp.power(jnp.float32(ROPE_THETA), -jnp.arange(0, ROPE_DIM, 2, dtype=F32) / ROPE_DIM)
    pos = jnp.arange(s, dtype=jnp.int32)
    ang = pos.astype(F32)[:, None] * inv_freq[None, :]
    cos, sin = jnp.cos(ang), jnp.sin(ang)
    zero = jnp.zeros((s, ATT_HEAD_DIM - ROPE_DIM), F32)
    cf = jnp.concatenate([cos, cos, jnp.ones_like(zero)], axis=1)
    s1 = jnp.concatenate([-sin, jnp.zeros_like(sin), zero], axis=1)
    s2 = jnp.concatenate([jnp.zeros_like(sin), sin, zero], axis=1)
    assert half * 2 == ROPE_DIM
    return cf, s1, s2


def _rot(x, tabs, sign):
    cf, s1, s2 = tabs
    half = ROPE_DIM // 2
    left = pltpu.roll(x, ATT_HEAD_DIM - half, 1)
    right = pltpu.roll(x, half, 1)
    return x * cf + sign * (left * s1 + right * s2)


def _att_masks(n, n_blk, rep):
    b = ATT_BLOCK
    row = lax.broadcasted_iota(jnp.int32, (rep * b, b), 0) & (b - 1)
    col = lax.broadcasted_iota(jnp.int32, (rep * b, b), 1)
    off = jnp.where(n % n_blk != 0, 0, 2 * b)
    return col <= row, col >= row + off


def _rot_heads(x, tabs, width, sign, out_dtype, name):
    s = x.shape[0]
    hd = ATT_HEAD_DIM
    tile = min(512, s)

    def body(x_ref, cf, s1, s2, o_ref):
        t = (cf[...], s1[...], s2[...])
        for j in range(width // hd):
            o_ref[:, j * hd:(j + 1) * hd] = _rot(x_ref[:, j * hd:(j + 1) * hd], t, sign).astype(o_ref.dtype)

    tab = pl.BlockSpec((tile, hd), lambda i: (i, 0))
    return pl.pallas_call(
        body, name=name, grid=(s // tile,), in_specs=[pl.BlockSpec((tile, width), lambda i: (i, 0)), tab, tab, tab],
        out_specs=pl.BlockSpec((tile, width), lambda i: (i, 0)), out_shape=jax.ShapeDtypeStruct((s, width), out_dtype),
        compiler_params=_params(("parallel",)))(x, *tabs)


def _kv_grad(dk_rot, dv, tabs, name):
    s, width = dk_rot.shape
    hd = ATT_HEAD_DIM
    tile = min(512, s)

    def body(k_ref, v_ref, cf, s1, s2, o_ref):
        t = (cf[...], s1[...], s2[...])
        for j in range(width // hd):
            o_ref[:, j * hd:(j + 1) * hd] = _rot(k_ref[:, j * hd:(j + 1) * hd], t, -1.0).astype(o_ref.dtype)
        o_ref[:, width:] = v_ref[...].astype(o_ref.dtype)

    tab = pl.BlockSpec((tile, hd), lambda i: (i, 0))
    half = pl.BlockSpec((tile, width), lambda i: (i, 0))
    return pl.pallas_call(
        body, name=name, grid=(s // tile,), in_specs=[half, half, tab, tab, tab],
        out_specs=pl.BlockSpec((tile, 2 * width), lambda i: (i, 0)),
        out_shape=jax.ShapeDtypeStruct((s, 2 * width), BF16), compiler_params=_params(("parallel",)))(dk_rot, dv, *tabs)


def _rows_of(r, dil):
    return pl.ds(r, ATT_BLOCK, stride=dil) if dil > 1 else slice(None)


def _nat_specs(g, dil, n_kv_all, cur, prv):
    b, hd = ATT_BLOCK * dil, ATT_HEAD_DIM
    n_kv = ATT_KV_HEADS_PER_GROUP
    rep = ATT_HEADS_PER_GROUP // n_kv
    q_all = [pl.BlockSpec((b, hd), lambda h, n, j=j: (cur(n), (g * n_kv + h) * rep + j)) for j in range(rep)]
    q_own = [pl.BlockSpec((b, hd), lambda h, n, j=j: (cur(n), h * rep + j)) for j in range(rep)]
    hm_all = pl.BlockSpec((rep, b, hd), lambda h, n: (g * n_kv + h, cur(n), 0))
    hm_own = pl.BlockSpec((rep, b, hd), lambda h, n: (h, cur(n), 0))
    kc = pl.BlockSpec((b, hd), lambda h, n: (cur(n), g * n_kv + h))
    kp = pl.BlockSpec((b, hd), lambda h, n: (prv(n), g * n_kv + h))
    vc = pl.BlockSpec((b, hd), lambda h, n: (cur(n), n_kv_all + g * n_kv + h))
    vp = pl.BlockSpec((b, hd), lambda h, n: (prv(n), n_kv_all + g * n_kv + h))
    tab = pl.BlockSpec((b, hd), lambda h, n: (cur(n), 0))
    stat = pl.BlockSpec((None, b, LANES), lambda h, n: (h, cur(n), 0))
    return q_all, q_own, hm_all, hm_own, kc, kp, vc, vp, tab, stat


def _head_cols(stat, rep):
    return jnp.concatenate([jnp.broadcast_to(stat[:, j:j + 1], stat.shape) for j in range(rep)], axis=0)


def _attn_fwd_nat(q_all, k_rot, kv, tabs, g, dil, name):
    s = q_all.shape[0]
    b, hd = ATT_BLOCK, ATT_HEAD_DIM
    nbn = s // (b * dil)
    n_kv = ATT_KV_HEADS_PER_GROUP
    rep = ATT_HEADS_PER_GROUP // n_kv
    n_kv_all = k_rot.shape[1] // hd
    scale = hd ** -0.5

    def body(*refs):
        q_refs = refs[:rep]
        kc_ref, kp_ref, vc_ref, vp_ref, cf, s1, s2, o_ref, lse_ref = refs[rep:]
        mc, mp = _att_masks(jnp.where(pl.program_id(1) > 0, 1, 0), 2, rep)
        for r in range(dil):
            sl = _rows_of(r, dil)
            tc = (cf[sl, :], s1[sl, :], s2[sl, :])
            q4 = (jnp.concatenate([_rot(q_ref[sl, :], tc, 1.0) for q_ref in q_refs], axis=0) * scale).astype(BF16)
            kc, kp = kc_ref[sl, :].astype(BF16), kp_ref[sl, :].astype(BF16)
            sc = jnp.where(mc, _dot(q4, kc, _NT), NEG)
            sp = jnp.where(mp, _dot(q4, kp, _NT), NEG)
            m = jnp.maximum(jnp.max(sc, axis=1, keepdims=True), jnp.max(sp, axis=1, keepdims=True))
            pc, pp = jnp.exp(sc - m), jnp.exp(sp - m)
            l = jnp.sum(pc, axis=1, keepdims=True) + jnp.sum(pp, axis=1, keepdims=True)
            o = (_dot(pc.astype(BF16), vc_ref[sl, :].astype(BF16))
                 + _dot(pp.astype(BF16), vp_ref[sl, :].astype(BF16))) / l
            lse = m + jnp.log(l)
            for j in range(rep):
                o_ref[j, sl, :] = o[j * b:(j + 1) * b]
            lse_ref[sl, :] = _lane_place([lse[j * b:(j + 1) * b] for j in range(rep)])

    cur = lambda n: n
    prv = lambda n: jnp.maximum(n - 1, 0)
    q_specs, _, _, hm_own, kc, kp, vc, vp, tab, stat = _nat_specs(g, dil, n_kv_all, cur, prv)
    return pl.pallas_call(
        body, name=name, grid=(n_kv, nbn), in_specs=[*q_specs, kc, kp, vc, vp, tab, tab, tab], out_specs=[hm_own, stat],
        out_shape=[jax.ShapeDtypeStruct((ATT_HEADS_PER_GROUP, s, hd), F32), jax.ShapeDtypeStruct((n_kv, s, LANES), F32)],
        compiler_params=_params(("parallel", "arbitrary")))(*([q_all] * rep), k_rot, k_rot, kv, kv, *tabs)


def _attn_bwd_nat(q_all, k_rot, kv, do, lse, delta, tabs, grads, g, dil, name):
    s = q_all.shape[0]
    b, hd = ATT_BLOCK, ATT_HEAD_DIM
    nbn = s // (b * dil)
    n_kv = ATT_KV_HEADS_PER_GROUP
    rep = ATT_HEADS_PER_GROUP // n_kv
    n_kv_all = k_rot.shape[1] // hd
    scale = hd ** -0.5

    def body(*refs):
        q_refs, do_refs = refs[:rep], refs[rep:2 * rep]
        (lse_ref, dl_ref, kc_ref, kp_ref, vc_ref, vp_ref, cf, s1, s2, _, _, _,
         dq_ref, dk_ref, dv_ref, ck, cv) = refs[2 * rep:]
        n = pl.program_id(1)

        @pl.when(n == 0)
        def _():
            ck[...] = jnp.zeros(ck.shape, F32)
            cv[...] = jnp.zeros(cv.shape, F32)

        @pl.when(n < nbn)
        def _():
            mc, mp = _att_masks(jnp.where(n > 0, 1, 0), 2, rep)
            for r in range(dil):
                sl = _rows_of(r, dil)
                own = slice(r * b, (r + 1) * b)
                tc = (cf[sl, :], s1[sl, :], s2[sl, :])
                q4 = (jnp.concatenate([_rot(q_ref[sl, :], tc, 1.0) for q_ref in q_refs], axis=0) * scale).astype(BF16)
                do4 = jnp.concatenate([do_ref[sl, :] for do_ref in do_refs], axis=0).astype(BF16)
                lse4 = _head_cols(lse_ref[sl, :], rep)
                dl4 = _head_cols(dl_ref[sl, :], rep)
                kc, kp = kc_ref[sl, :].astype(BF16), kp_ref[sl, :].astype(BF16)
                vc, vp = vc_ref[sl, :].astype(BF16), vp_ref[sl, :].astype(BF16)
                pc = jnp.where(mc, jnp.exp(_dot(q4, kc, _NT) - lse4), 0.0)
                pp = jnp.where(mp, jnp.exp(_dot(q4, kp, _NT) - lse4), 0.0)
                dsc = (pc * (_dot(do4, vc, _NT) - dl4)).astype(BF16)
                dsp = (pp * (_dot(do4, vp, _NT) - dl4)).astype(BF16)
                dq4 = (_dot(dsc, kc) + _dot(dsp, kp)) * scale
                for j in range(rep):
                    dq_ref[j, sl, :] = _rot(dq4[j * b:(j + 1) * b], tc, -1.0)
                dk_ref[sl, :] = ck[own, :] + _dot(dsp, q4, _TN)
                dv_ref[sl, :] = cv[own, :] + _dot(pp.astype(BF16), do4, _TN)
                ck[own, :] = _dot(dsc, q4, _TN)
                cv[own, :] = _dot(pc.astype(BF16), do4, _TN)

        @pl.when(n == nbn)
        def _():
            for r in range(dil):
                sl = _rows_of(r, dil)
                dk_ref[sl, :] = ck[r * b:(r + 1) * b, :]
                dv_ref[sl, :] = cv[r * b:(r + 1) * b, :]

    cur = lambda n: jnp.minimum(n, nbn - 1)
    prv = lambda n: jnp.maximum(n - 1, 0)
    q_specs, do_specs, hm_all, _, kc, kp, vc, vp, tab, stat = _nat_specs(g, dil, n_kv_all, cur, prv)
    anyspace = pl.BlockSpec(memory_space=pl.ANY)
    n_in = 2 * rep + 9
    return pl.pallas_call(
        body, name=name, grid=(n_kv, nbn + 1),
        in_specs=[*q_specs, *do_specs, stat, stat, kc, kp, vc, vp, tab, tab, tab, anyspace, anyspace, anyspace],
        out_specs=[hm_all, kp, kp], out_shape=[jax.ShapeDtypeStruct(a.shape, a.dtype) for a in grads],
        input_output_aliases={n_in: 0, n_in + 1: 1, n_in + 2: 2},
        scratch_shapes=[pltpu.VMEM((dil * b, hd), F32), pltpu.VMEM((dil * b, hd), F32)],
        compiler_params=_params(("parallel", "arbitrary"), VMEM_LIMIT_ATTN_BWD_BYTES))(
            *([q_all] * rep), *([do] * rep), lse, delta, k_rot, k_rot, kv, kv, *tabs, *grads)


def _adamw(g_slabs, w, m, v, name, row0=0, prior=None):
    kk, r, c = g_slabs.shape
    tile = r if r <= 256 else _pick_rows(r, 256)
    off = row0 // tile
    assert off * tile == row0

    def body(g_ref, w_ref, m_ref, v_ref, *rest):
        go_ref, d_ref, mo_ref, vo_ref = rest[-4:]
        g = g_ref[0].astype(F32)
        for k in range(1, kk):
            g = g + g_ref[k].astype(F32)
        m2 = ADAM_B1 * m_ref[...] + (1.0 - ADAM_B1) * g
        v2 = ADAM_B2 * v_ref[...] + (1.0 - ADAM_B2) * jnp.square(g)
        m_hat = m2 / (1.0 - ADAM_B1 ** ADAM_STEP)
        v_hat = v2 / (1.0 - ADAM_B2 ** ADAM_STEP)
        go_ref[...] = g
        d_ref[...] = -ADAM_LR * (m_hat / (jnp.sqrt(v_hat) + ADAM_EPS) + ADAM_WD * w_ref[...])
        mo_ref[...] = m2
        vo_ref[...] = v2

    spec = pl.BlockSpec((tile, c), lambda i: (i + off, 0))
    prior = list(prior) if prior is not None else []
    return pl.pallas_call(
        body, name=name, grid=(r // tile,),
        in_specs=[pl.BlockSpec((kk, tile, c), lambda i: (0, i, 0)), spec, spec, spec]
        + [pl.BlockSpec(memory_space=pl.ANY)] * len(prior),
        out_specs=[spec] * 4, out_shape=[jax.ShapeDtypeStruct(w.shape, F32)] * 4,
        input_output_aliases={4 + i: i for i in range(len(prior))},
        compiler_params=_params(("parallel",)))(g_slabs, w, m, v, *prior)


def _pick_rows(r, pref):
    t = (pref // 16) * 16
    while t >= 16:
        if r % t == 0:
            return t
        t -= 16
    return r


def _coords():
    return lax.axis_index("x"), lax.axis_index("y"), lax.axis_index("c")


def _dev_index(px, py, pc):
    return 4 * px + 2 * py + pc


def _all_gather(shards, name):
    na = len(shards)

    def body(*refs):
        ins, outs = refs[:na], refs[na:2 * na]
        send_sems, recv_sems, local_sems = refs[2 * na:]
        x, y, c = _coords()
        me, sibling = (x, y, c), (x, y, 1 - c)
        chips = [(1 - x, y), (x, 1 - y), (1 - x, 1 - y)]

        def copy(a, k, block, to, src=None):
            dst = outs[a].at[_dev_index(*block)]
            return pltpu.make_async_remote_copy(
                src_ref=dst if src is None else src, dst_ref=dst, send_sem=send_sems.at[a * 7 + k],
                recv_sem=recv_sems.at[a * 7 + k], device_id=to, device_id_type=MESH)

        mine = [pltpu.make_async_copy(ins[a], outs[a].at[_dev_index(*me)], local_sems.at[a]) for a in range(na)]
        for cp in mine:
            cp.start()
        first = []
        for a in range(na):
            first.append(copy(a, 0, me, sibling, src=ins[a]))
            first += [copy(a, 1 + j, me, (*chip, c), src=ins[a]) for j, chip in enumerate(chips)]
        for cp in first:
            cp.start()
        passed = []
        for j, chip in enumerate(chips):
            for a in range(na):
                copy(a, 1 + j, (*chip, c), me).wait_recv()
                cp = copy(a, 4 + j, (*chip, c), sibling)
                cp.start()
                passed.append(cp)
        for a in range(na):
            copy(a, 0, sibling, me).wait_recv()
            for j, chip in enumerate(chips):
                copy(a, 4 + j, (*chip, 1 - c), me).wait_recv()
        for cp in first + passed:
            cp.wait_send()
        for cp in mine:
            cp.wait()

    hbm = pl.BlockSpec(memory_space=pl.ANY)
    return pl.pallas_call(
        body, name=name, in_specs=[hbm] * na, out_specs=[hbm] * na,
        out_shape=[jax.ShapeDtypeStruct((NDEV,) + s.shape, s.dtype) for s in shards],
        scratch_shapes=[pltpu.SemaphoreType.DMA((7 * na,)), pltpu.SemaphoreType.DMA((7 * na,)),
                        pltpu.SemaphoreType.DMA((na,))])(*shards)


def _exchange(slabs, whole, name, after=()):
    ns, nw = len(slabs), len(whole)
    na = ns + nw
    nb = len(after)

    def body(*refs):
        ins, outs = refs[:na], refs[na + nb:2 * na + nb]
        send_sems, recv_sems, local_sems = refs[2 * na + nb:]
        x, y, c = _coords()
        me = _dev_index(x, y, c)

        def src_of(a, p):
            return ins[a].at[p] if a < ns else ins[a]

        def copy(a, k, peer):
            p = _dev_index(*peer)
            return pltpu.make_async_remote_copy(
                src_ref=src_of(a, p), dst_ref=outs[a].at[me], send_sem=send_sems.at[a * 7 + k - 1],
                recv_sem=recv_sems.at[a * 7 + k - 1], device_id=peer, device_id_type=MESH)

        def arrival(a, k, peer):
            p = _dev_index(*peer)
            return pltpu.make_async_remote_copy(
                src_ref=src_of(a, p), dst_ref=outs[a].at[p], send_sem=send_sems.at[a * 7 + k - 1],
                recv_sem=recv_sems.at[a * 7 + k - 1], device_id=peer, device_id_type=MESH)

        mine = [pltpu.make_async_copy(src_of(a, me), outs[a].at[me], local_sems.at[a]) for a in range(na)]
        for cp in mine:
            cp.start()
        peers = [(k, (x ^ (k >> 2), y ^ ((k >> 1) & 1), c ^ (k & 1))) for k in range(1, NDEV)]
        sent = [copy(a, k, peer) for k, peer in peers for a in range(na)]
        for cp in sent:
            cp.start()
        for k, peer in peers:
            for a in range(na):
                arrival(a, k, peer).wait_recv()
        for cp in sent:
            cp.wait_send()
        for cp in mine:
            cp.wait()

    hbm = pl.BlockSpec(memory_space=pl.ANY)
    out_shape = [jax.ShapeDtypeStruct(s.shape, s.dtype) for s in slabs]
    out_shape += [jax.ShapeDtypeStruct((NDEV,) + w.shape, w.dtype) for w in whole]
    return pl.pallas_call(
        body, name=name, in_specs=[hbm] * (na + nb), out_specs=[hbm] * na, out_shape=out_shape,
        scratch_shapes=[pltpu.SemaphoreType.DMA((7 * na,)), pltpu.SemaphoreType.DMA((7 * na,)),
                        pltpu.SemaphoreType.DMA((na,))])(*slabs, *whole, *after)


_HBM = pl.BlockSpec(memory_space=pltpu.HBM)
_SEM = pl.BlockSpec(memory_space=pltpu.SEMAPHORE)
_EFFECT = pltpu.SideEffectType.DATAFLOW_SIDE_EFFECTING


def _peers(x, y, c):
    return [(k, (x ^ (k >> 2), y ^ ((k >> 1) & 1), c ^ (k & 1))) for k in range(1, NDEV)]


def _peer_copy(src, land, send_sems, recv_sems, a, k, dst_block, peer):
    return pltpu.make_async_remote_copy(
        src_ref=src, dst_ref=land.at[dst_block], send_sem=send_sems.at[a * 7 + k - 1],
        recv_sem=recv_sems.at[a * 7 + k - 1], device_id=peer, device_id_type=MESH)


def _send_start(arrays, slabs, name):
    na = len(arrays)
    lands = [jax.ShapeDtypeStruct(a.shape if slabs else (NDEV,) + a.shape, a.dtype) for a in arrays]

    def body(*refs):
        ins, zones = refs[:na], refs[na:2 * na]
        send_sems, recv_sems = refs[2 * na], refs[2 * na + 1]
        token = refs[-1]
        x, y, c = _coords()
        me = _dev_index(x, y, c)
        for k, peer in _peers(x, y, c):
            for a in range(na):
                src = ins[a].at[_dev_index(*peer)] if slabs else ins[a]
                _peer_copy(src, zones[a], send_sems, recv_sems, a, k, me, peer).start()
        token[...] = jnp.zeros_like(token)

    outs = pl.pallas_call(
        body, name=name,
        out_shape=(pltpu.SemaphoreType.DMA((7 * na,)), pltpu.SemaphoreType.DMA((7 * na,)),
                   *[pltpu.HBM(a.shape, a.dtype) for a in arrays], *[pltpu.HBM(l.shape, l.dtype) for l in lands],
                   jax.ShapeDtypeStruct((8, LANES), F32)),
        in_specs=[_HBM] * (2 * na), out_specs=(_SEM, _SEM, *([_HBM] * (2 * na)), pl.BlockSpec(memory_space=pltpu.VMEM)),
        input_output_aliases={i: 2 + i for i in range(2 * na)},
        compiler_params=pltpu.CompilerParams(has_side_effects=_EFFECT),
    )(*[pltpu.with_memory_space_constraint(a, pltpu.HBM) for a in arrays],
      *[pltpu.with_memory_space_constraint(lax.empty(l.shape, l.dtype), pltpu.HBM) for l in lands])
    return outs[0], outs[1], list(outs[2:2 + na]), list(outs[2 + na:2 + 2 * na]), outs[-1]


def _send_wait(started, after, slabs, name):
    send_sems, recv_sems, thru, zones, _ = started
    na = len(thru)

    def body(*refs):
        ins, lands = refs[:na], refs[na:2 * na]
        s_sems, r_sems = refs[2 * na], refs[2 * na + 1]
        x, y, c = _coords()
        for k, peer in _peers(x, y, c):
            p = _dev_index(*peer)
            for a in range(na):
                src = ins[a].at[p] if slabs else ins[a]
                cp = _peer_copy(src, lands[a], s_sems, r_sems, a, k, p, peer)
                cp.wait_send()
                cp.wait_recv()

    outs = pl.pallas_call(
        body, name=name, out_shape=tuple(pltpu.HBM(v.shape, v.dtype) for v in thru + zones),
        in_specs=[_HBM] * (2 * na) + [_SEM, _SEM, pl.BlockSpec(memory_space=pl.ANY)], out_specs=tuple([_HBM] * (2 * na)),
        input_output_aliases={i: i for i in range(2 * na)},
        compiler_params=pltpu.CompilerParams(has_side_effects=_EFFECT),
    )(*thru, *zones, send_sems, recv_sems, after)
    me = _dev_index(*_coords())
    filled = []
    for a in range(na):
        own = lax.dynamic_index_in_dim(outs[a], me, 0, keepdims=False) if slabs else outs[a]
        filled.append(lax.dynamic_update_index_in_dim(outs[na + a], own, me, 0))
    return filled


def _pack(vecs):
    parts, spans, off = [], [], 0
    for v in vecs:
        n = v.size
        pad = (-n) % LANES
        parts.append(jnp.pad(v.reshape(-1).astype(F32), (0, pad)))
        spans.append((off, n))
        off += n + pad
    return jnp.concatenate(parts).reshape(-1, LANES), spans


def _pad_lanes(v):
    v = v.reshape(1, -1)
    return jnp.pad(v, ((0, 0), (0, LANES - v.shape[1])))


def _cols_to_slabs(g):
    sh = g.shape
    g = g.reshape(sh[:-1] + (NDEV, sh[-1] // NDEV))
    return jnp.moveaxis(g, -2, 0)


def _rows_to_slabs(g):
    sh = g.shape
    g = g.reshape(sh[:-2] + (NDEV, sh[-2] // NDEV, sh[-1]))
    return jnp.moveaxis(g, -3, 0)


def _slabs_to_cols(a):
    a = jnp.moveaxis(a, 0, -2)
    return a.reshape(a.shape[:-2] + (a.shape[-2] * a.shape[-1],))


def _slabs_to_rows(a):
    a = jnp.moveaxis(a, 0, -3)
    return a.reshape(a.shape[:-3] + (a.shape[-3] * a.shape[-2], a.shape[-1]))


def _ffn_forward(x, norm_w, w_up, fcw, wdown, tag):
    u, h = _mm(x, w_up, rms_fwd=norm_w, name=f"{tag}_up")
    f = _ffn_gate_fwd(u, fcw, f"{tag}_gate")
    return _mm(f, wdown, res=x, name=f"{tag}_down"), (h, u, f)


def _ffn_backward(x, saved, dout, dout_b, norm_w, w_up, fcw, wdown, tag):
    h, u, f = saved
    dwdown = _mm(f, dout_b, ta=True, name=f"{tag}_dwdown")
    df = _mm(dout_b, wdown, tb=True, name=f"{tag}_df")
    du, dfc = _ffn_gate_bwd(u, fcw, df, f"{tag}_gate_bwd")
    dwup = _mm(h, du, ta=True, name=f"{tag}_dwup")
    dx, dxb, dnorm = _mm(du, w_up, tb=True, rms_bwd=(x, norm_w, dout), tm=RMS_BWD_ROWS, name=f"{tag}_dh")
    return dx, dxb, (dwup, jnp.concatenate([dfc[0], dfc[1]], axis=1), dwdown, dnorm)


def kernel(x, a_norm, ssm_w_in, ssm_conv_w, ssm_conv_b, ssm_dt_bias, ssm_a_log, ssm_d, ssm_norm, ssm_w_out, kv_norm, w_kv, b_norm, att_w_q, att_w_o, ffn_norm, ffn_w_up, ffn_conv_w, ffn_w_down, final_norm, loss_target, m_a_norm, m_ssm_w_in, m_ssm_conv_w, m_ssm_conv_b, m_ssm_dt_bias, m_ssm_a_log, m_ssm_d, m_ssm_norm, m_ssm_w_out, m_kv_norm, m_w_kv, m_b_norm, m_att_w_q, m_att_w_o, m_ffn_norm, m_ffn_w_up, m_ffn_conv_w, m_ffn_w_down, m_final_norm, v_a_norm, v_ssm_w_in, v_ssm_conv_w, v_ssm_conv_b, v_ssm_dt_bias, v_ssm_a_log, v_ssm_d, v_ssm_norm, v_ssm_w_out, v_kv_norm, v_w_kv, v_b_norm, v_att_w_q, v_att_w_o, v_ffn_norm, v_ffn_w_up, v_ffn_conv_w, v_ffn_w_down, v_final_norm):
    given = dict(locals())
    xs, tgt = x[0], loss_target[0]
    s, d = xs.shape
    di = ssm_w_out.shape[1] * NDEV
    nh = ssm_dt_bias.shape[1]
    ng = SSM_N_GROUPS
    convd = di + 2 * ng * SSM_D_STATE
    f = ffn_w_down.shape[1] * NDEV
    n_att = len(ATT_PATTERNS)
    qg = ATT_HEADS_PER_GROUP * ATT_HEAD_DIM
    kg = ATT_KV_HEADS_PER_GROUP * ATT_HEAD_DIM
    kvd = n_att * kg
    assert all(w // dil == ATT_BLOCK for w, dil in ATT_PATTERNS)

    small, _ = _pack([a_norm, ssm_conv_w, ssm_conv_b, ssm_norm, ffn_conv_w])
    gat = _all_gather([ssm_w_in[0].astype(BF16), small], "gather_weights")
    first = _send_start([ssm_w_out[0].astype(BF16), ffn_w_up[0].astype(BF16), ffn_w_down[0].astype(BF16)], False,
                        "gather_ffn0_start")
    rest = _send_start([b.astype(BF16) for b in (w_kv, att_w_q[0], att_w_o[0], ffn_w_up[1], ffn_w_down[1])], False,
                       "gather_rest_start")
    w_in = _slabs_to_cols(gat[0])
    in_dim = di + convd + nh
    in_pad = di + convd + LANES
    w_in = jnp.pad(w_in, ((0, 0), (0, in_pad - in_dim)))
    sm = gat[1].reshape(NDEV, -1)
    o0 = 0

    def take(shape):
        nonlocal o0
        n = math.prod(shape)
        out = sm[:, o0:o0 + n].reshape((NDEV,) + shape)
        o0 += n + (-n) % LANES
        return out
    a_norm_f = _slabs_to_cols(take(a_norm.shape)) + (first[-1][0, 0] + rest[-1][0, 0])
    conv_w_f = _slabs_to_cols(take(ssm_conv_w.shape))[0]
    conv_b_f = _slabs_to_cols(take(ssm_conv_b.shape))
    ssm_norm_f = _slabs_to_cols(take(ssm_norm.shape))
    fcw = _slabs_to_cols(take(ffn_conv_w.shape))
    dtb, alog, dsk = _pad_lanes(ssm_dt_bias), _pad_lanes(ssm_a_log), _pad_lanes(ssm_d)
    kvn, fin = kv_norm.reshape(1, d), final_norm.reshape(1, d)

    zx, h0 = _mm(xs, w_in, rms_fwd=a_norm_f, name="in_proj")
    z, dtr = (zx, di, 0), (zx, LANES, (di + convd) // LANES)
    xbc = _conv_silu_fwd(zx, di, conv_w_f, conv_b_f, "ssm_conv")
    dt = _softplus_fwd(dtr, dtb, "ssm_dt")
    y, prevs = _ssd_fwd(xbc, dt, alog, dsk, di, nh, ng, "ssd")
    yn = _gnorm_fwd(y, z, ssm_norm_f, ng, "ssm_gnorm")
    got = _send_wait(first, yn, False, "gather_ffn0_wait")
    w_out = _slabs_to_rows(got[0])
    w_up0, w_down0 = _slabs_to_cols(got[1]), _slabs_to_rows(got[2])
    x1 = _mm(yn, w_out, res=xs, name="ssm_out")
    x2, ffn0 = _ffn_forward(x1, ffn_norm[0:1], w_up0, fcw[0], w_down0, "ffn0")
    got = _send_wait(rest, x2, False, "gather_rest_wait")
    w_kvf = _slabs_to_cols(got[0])
    w_q = _slabs_to_cols(got[1])
    w_o = _slabs_to_rows(got[2])
    w_up1, w_down1 = _slabs_to_cols(got[3]), _slabs_to_rows(got[4])
    kv, hk = _mm(x2, w_kvf, rms_fwd=kvn, name="kv_proj")
    q, h2 = _mm(x2, w_q, rms_fwd=b_norm, name="q_proj")
    tabs = _rot_tables(s)
    k_rot = _rot_heads(kv, tabs, kvd, 1.0, F32, "k_rot")
    att = [_attn_fwd_nat(q, k_rot, kv, tabs, g, dil, f"attn{g}") for g, (_, dil) in enumerate(ATT_PATTERNS)]
    o, ob, lse = _merge_heads([t[0] for t in att], [t[1] for t in att], "attn_merge")
    x3 = _mm(ob, w_o, res=x2, name="attn_out")
    x4, ffn1 = _ffn_forward(x3, ffn_norm[1:2], w_up1, fcw[1], w_down1, "ffn1")
    loss_part, dx4, dx4b, dfin = _final_loss(x4, fin, tgt, "loss_head")

    dx3, dx3b, (dwup1, dfc1, dwdown1, dfn1) = _ffn_backward(
        x3, ffn1, dx4, dx4b, ffn_norm[1:2], w_up1, fcw[1], w_down1, "ffn1")
    dw_o = _mm(ob, dx3b, ta=True, name="attn_dwo")
    do = _mm(dx3b, w_o, tb=True, name="attn_do")
    delta = _delta_heads(do, o, "attn_delta")
    grads = (lax.empty((n_att * qg // LANES, s, LANES), F32), lax.empty((s, kvd), F32), lax.empty((s, kvd), F32))
    for g, (_, dil) in enumerate(ATT_PATTERNS):
        grads = _attn_bwd_nat(q, k_rot, kv, do, lse, delta, tabs, grads, g, dil, f"attn{g}_bwd")
    dq, dk_rot, dv = grads
    dkv = _kv_grad(dk_rot, dv, tabs, "kv_grad")
    dw_q = _mm(h2, dq, ta=True, b_heads=True, name="q_dw")
    dx2, _, db_norm = _mm(dq, w_q, tb=True, a_heads=True, rms_bwd=(x2, b_norm, dx3), tm=RMS_BWD_ROWS, name="q_dh")
    dw_kv = _mm(hk, dkv, ta=True, name="kv_dw")
    dx2, dx2b, dkv_norm = _mm(dkv, w_kvf, tb=True, rms_bwd=(x2, kvn, dx2), tm=RMS_BWD_ROWS, name="kv_dh")
    sent1 = _send_start([_cols_to_slabs(dwup1).astype(BF16), _rows_to_slabs(dwdown1).astype(BF16),
                         _cols_to_slabs(dw_kv).astype(BF16), _cols_to_slabs(dw_q).astype(BF16),
                         _rows_to_slabs(dw_o).astype(BF16)], True, "grads_late_start")
    dx1, dx1b, (dwup0, dfc0, dwdown0, dfn0) = _ffn_backward(
        x1, ffn0, dx2, dx2b, ffn_norm[0:1], w_up0, fcw[0] + sent1[-1][0, 0], w_down0, "ffn0")
    dw_out = _mm(yn, dx1b, ta=True, name="ssm_dwout")
    sent0 = _send_start([_cols_to_slabs(dwup0).astype(BF16), _rows_to_slabs(dwdown0).astype(BF16),
                         _rows_to_slabs(dw_out).astype(BF16)], True, "grads_ffn0_start")
    dyn = _mm(dx1b, w_out, tb=True, name="ssm_dyn")
    dzx = lax.empty((s, in_pad), BF16)
    dy, dzx, dssm_norm = _gnorm_bwd(dyn, y, z, ssm_norm_f + sent0[-1][0, 0], ng, (dzx, 0), "ssm_gnorm_bwd")
    dxbc, ddt, dalog, ddsk = _ssd_bwd(xbc, dt, alog, dsk, prevs, dy, di, nh, ng, "ssd_bwd")
    dzx, ddtb = _softplus_bwd(ddt, dtr, dtb, nh, (dzx, (di + convd) // LANES), "ssm_dt_bwd")
    dzx, dconv_w, dconv_b = _conv_silu_bwd(zx, di, conv_w_f, conv_b_f, dxbc, dzx, "ssm_conv_bwd")
    dw_in = _mm(h0, dzx, ta=True, name="in_dw")
    sent_m = _send_start([_cols_to_slabs(dw_in[:, :in_dim])], True, "grads_mamba_start")
    dx0, _, da_norm = _mm(dzx, w_in, tb=True, rms_bwd=(xs, a_norm_f + sent_m[-1][0, 0], dx1), tm=RMS_BWD_ROWS,
                          name="in_dh")

    small_full = {
        'a_norm': da_norm, 'ssm_conv_w': dconv_w[None], 'ssm_conv_b': dconv_b, 'ssm_dt_bias': ddtb[:, :nh],
        'ssm_a_log': dalog[:, :nh], 'ssm_d': ddsk[:, :nh], 'ssm_norm': dssm_norm, 'kv_norm': dkv_norm.reshape(d),
        'b_norm': db_norm, 'ffn_norm': jnp.concatenate([dfn0, dfn1], axis=0), 'ffn_conv_w': jnp.stack([dfc0, dfc1]),
        'final_norm': dfin.reshape(d),
    }
    small_names = list(small_full)
    packed, spans = _pack([small_full[n] for n in small_names])
    got1 = _send_wait(sent1, dx0, True, "grads_late_wait")
    got0 = _send_wait(sent0, dx0, True, "grads_ffn0_wait")
    recv_big = {'w_kv': [got1[2]], 'att_w_q': [got1[3]], 'att_w_o': [got1[4]], 'ffn_w_up': [got0[0], got1[0]],
                'ffn_w_down': [got0[1], got1[1]]}

    me = _dev_index(*_coords())
    res = {}

    def update_big(n, layers):
        w = given[n]
        c = w.shape[-1]
        outs, row0 = None, 0
        for k, r in enumerate(layers):
            g = r.reshape(NDEV, -1, c)
            outs = _adamw(g, w.reshape(-1, c), given['m_' + n].reshape(-1, c), given['v_' + n].reshape(-1, c),
                          f"adamw_{n}_{k}", row0=row0, prior=outs)
            row0 += g.shape[1]
        res[n] = [o_.reshape(w.shape) for o_ in outs]
    for n, r in recv_big.items():
        update_big(n, r)
    update_big('ssm_w_out', [got0[2]])
    recv = _exchange([], [packed], "exchange_grads", after=[res[n][1] for n in res])
    small_sum = _sum_slabs(recv[-1], "sum_small_grads").reshape(-1)
    gotm = _send_wait(sent_m, recv[-1], True, "grads_mamba_wait")
    update_big('ssm_w_in', [gotm[0]])
    sharded_small = {'a_norm', 'ssm_conv_w', 'ssm_conv_b', 'ssm_norm', 'ffn_conv_w'}
    for n, (off, size) in zip(small_names, spans):
        w = given[n]
        gfull = small_sum[off:off + size].reshape(small_full[n].shape)
        if n in sharded_small:
            c = w.shape[-1]
            gfull = lax.dynamic_slice_in_dim(gfull, me * c, c, axis=gfull.ndim - 1)
        c = w.shape[-1]
        outs = _adamw(gfull.reshape(1, -1, c), w.reshape(-1, c), given['m_' + n].reshape(-1, c),
                      given['v_' + n].reshape(-1, c), f"adamw_{n}")
        res[n] = [o_.reshape(w.shape) for o_ in outs]

    loss = lax.psum(loss_part[0, 0], AXES)
    return (loss, dx0[None], *[res[n][0] for n in WEIGHTS], *[res[n][1] for n in WEIGHTS],
            *[res[n][2] for n in WEIGHTS], *[res[n][3] for n in WEIGHTS])
```

```python
import functools
import math

import jax
import jax.numpy as jnp
from jax import lax
from jax.experimental import pallas as pl
from jax.experimental.pallas import tpu as pltpu

F32, BF16 = jnp.float32, jnp.bfloat16
AXES = ("x", "y", "c")
NDEV = 8
MESH = pl.DeviceIdType.MESH
HIGHEST = lax.Precision.HIGHEST

LANES = 128
SUBLANES = 8
VMEM_LIMIT_BYTES = 48 * 1024 * 1024
VMEM_LIMIT_ATTN_BWD_BYTES = 58 * 1024 * 1024
RMS_BWD_ROWS = 512

RMS_EPS = 1e-6
GATED_NORM_EPS = 1e-5
SSM_HEAD_DIM = 64
SSM_N_GROUPS = 8
SSM_D_STATE = 128
SSM_CONV = 4
SSM_CHUNK = 128
ATT_PATTERNS = ((128, 1), (512, 4), (2048, 16))
ATT_HEAD_DIM = 128
ATT_HEADS_PER_GROUP = 8
ATT_KV_HEADS_PER_GROUP = 2
ATT_BLOCK = 128
ROPE_DIM = ATT_HEAD_DIM // 4
ROPE_THETA = 500000.0
FFN_CONV = 3
ADAM_LR = 0.001
ADAM_B1 = 0.9
ADAM_B2 = 0.999
ADAM_EPS = 1e-08
ADAM_WD = 0.01
ADAM_STEP = 10
NEG = -1e30

WEIGHTS = ['a_norm', 'ssm_w_in', 'ssm_conv_w', 'ssm_conv_b', 'ssm_dt_bias', 'ssm_a_log', 'ssm_d', 'ssm_norm',
           'ssm_w_out', 'kv_norm', 'w_kv', 'b_norm', 'att_w_q', 'att_w_o', 'ffn_norm', 'ffn_w_up', 'ffn_conv_w',
           'ffn_w_down', 'final_norm']


def _params(sem=None, vmem=VMEM_LIMIT_BYTES):
    kw = dict(vmem_limit_bytes=vmem)
    if sem is not None:
        kw["dimension_semantics"] = sem
    return pltpu.CompilerParams(**kw)


def _pick(n, pref):
    if n <= pref:
        return n
    t = (pref // LANES) * LANES
    while t >= LANES:
        if n % t == 0:
            return t
        t -= LANES
    return n


def _dot(a, b, dims=(((1,), (0,)), ((), ())), precision=None):
    return lax.dot_general(a, b, dims, precision=precision, preferred_element_type=F32)


_NT = (((1,), (1,)), ((), ()))
_TN = (((0,), (0,)), ((), ()))


def _mm(a, b, *, ta=False, tb=False, res=None, out_dtype=None, name, tm=1408, tn=1408, tk=2048,
        a_heads=False, b_heads=False, rms_bwd=None, rms_fwd=None):
    assert not (a_heads and ta) and not (b_heads and tb)
    a_parts = a.ndim == 3 and not a_heads
    b_parts = b.ndim == 3 and not b_heads
    assert not (a_parts and ta) and not (b_parts and tb)
    if out_dtype is None:
        out_dtype = BF16 if ta else F32
    if a_heads:
        m, k = a.shape[1], a.shape[0] * LANES
    elif a_parts:
        m, k = a.shape[1], a.shape[0] * a.shape[2]
    else:
        m = a.shape[1] if ta else a.shape[0]
        k = a.shape[0] if ta else a.shape[1]
    if b_heads:
        n, kb = b.shape[0] * LANES, b.shape[1]
    elif b_parts:
        n, kb = b.shape[0] * b.shape[2], b.shape[1]
    else:
        n = b.shape[0] if tb else b.shape[1]
        kb = b.shape[1] if tb else b.shape[0]
    assert k == kb
    tm = _pick(m, tm)
    tn = _pick(b.shape[2], tn) if b_parts else _pick(n, tn)
    tk = _pick(a.shape[2], tk) if a_parts else _pick(k, tk)
    nk = k // tk
    if a_heads:
        a_spec = pl.BlockSpec((tk // LANES, tm, LANES), lambda i, j, l: (l, i, 0))
    elif a_parts:
        per = a.shape[2] // tk
        a_spec = pl.BlockSpec((None, tm, tk), lambda i, j, l: (l // per, i, l % per))
    elif ta:
        a_spec = pl.BlockSpec((tk, tm), lambda i, j, l: (l, i))
    else:
        a_spec = pl.BlockSpec((tm, tk), lambda i, j, l: (i, l))
    if b_heads:
        b_spec = pl.BlockSpec((tn // LANES, tk, LANES), lambda i, j, l: (j, l, 0))
    elif b_parts:
        per_n = b.shape[2] // tn
        b_spec = pl.BlockSpec((None, tk, tn), lambda i, j, l: (j // per_n, l, j % per_n))
    elif tb:
        b_spec = pl.BlockSpec((tn, tk), lambda i, j, l: (j, l))
    else:
        b_spec = pl.BlockSpec((tk, tn), lambda i, j, l: (l, j))
    if rms_fwd is not None:
        assert nk == 1 and a.ndim == 2 and b.ndim == 2 and not (ta or tb or a_heads or b_heads)
        assert res is None and rms_bwd is None

        def nbody(a_ref, b_ref, w_ref, o_ref, h_ref, hs):
            @pl.when(pl.program_id(1) == 0)
            def _():
                xv = a_ref[...]
                rs = lax.rsqrt(jnp.mean(xv * xv, axis=-1, keepdims=True) + RMS_EPS)
                hv = (xv * rs * w_ref[...]).astype(BF16)
                hs[...] = hv
                h_ref[...] = hv

            o_ref[...] = _dot(hs[...], b_ref[...].astype(BF16)).astype(o_ref.dtype)

        rows = pl.BlockSpec((tm, k), lambda i, j: (i, 0))
        return pl.pallas_call(
            nbody, name=name, grid=(m // tm, n // tn),
            in_specs=[rows, pl.BlockSpec((k, tn), lambda i, j: (0, j)), pl.BlockSpec((1, k), lambda i, j: (0, 0))],
            out_specs=[pl.BlockSpec((tm, tn), lambda i, j: (i, j)), rows],
            out_shape=[jax.ShapeDtypeStruct((m, n), out_dtype), jax.ShapeDtypeStruct((m, k), BF16)],
            scratch_shapes=[pltpu.VMEM((tm, k), BF16)],
            compiler_params=_params(("parallel", "arbitrary")))(a, b, rms_fwd)
    o_spec = pl.BlockSpec((tm, tn), lambda i, j, l: (i, j))
    dims = (((0 if ta else 1,), (1 if tb else 0,)), ((), ()))
    has_res = res is not None
    has_rms = rms_bwd is not None
    assert not (has_res and has_rms) and (not has_rms or tn == n)
    n_extra = 3 if has_rms else int(has_res)
    n_out = 3 if has_rms else 1

    def load(ref, heads):
        if not heads:
            return ref[...].astype(BF16)
        return jnp.concatenate([ref[i].astype(BF16) for i in range(ref.shape[0])], axis=1)

    def body(*refs):
        a_ref, b_ref = refs[:2]
        extra = refs[2:2 + n_extra]
        outs = refs[2 + n_extra:2 + n_extra + n_out]
        p = _dot(load(a_ref, a_heads), load(b_ref, b_heads), dims)

        def finish(r):
            if has_res:
                r = r + extra[0][...]
            if not has_rms:
                outs[0][...] = r.astype(outs[0].dtype)
                return
            x_ref, w_ref, dres_ref = extra
            dx_ref, dxb_ref, dw_ref = outs
            xv = x_ref[...]
            rs = lax.rsqrt(jnp.mean(xv * xv, axis=-1, keepdims=True) + RMS_EPS)
            xh = xv * rs
            dxh = r * w_ref[...]
            dx = dres_ref[...] + rs * (dxh - xh * jnp.mean(dxh * xh, axis=-1, keepdims=True))
            dx_ref[...] = dx
            dxb_ref[...] = dx.astype(BF16)

            @pl.when(pl.program_id(0) == 0)
            def _():
                dw_ref[...] = jnp.zeros(dw_ref.shape, F32)

            dw_ref[...] += jnp.sum(r * xh, axis=0, keepdims=True)

        if nk == 1:
            finish(p)
            return
        acc = refs[2 + n_extra + n_out]
        l = pl.program_id(2)

        @pl.when(l == 0)
        def _():
            acc[...] = p

        @pl.when(jnp.logical_and(l > 0, l < nk - 1))
        def _():
            acc[...] += p

        @pl.when(l == nk - 1)
        def _():
            finish(acc[...] + p)

    scratch = [pltpu.VMEM((tm, tn), F32)] if nk > 1 else []
    if has_rms:
        x, w, dres = rms_bwd
        vec = pl.BlockSpec((1, n), lambda i, j, l: (0, 0))
        return pl.pallas_call(
            body, name=name, grid=(m // tm, 1, nk), in_specs=[a_spec, b_spec, o_spec, vec, o_spec],
            out_specs=[o_spec, o_spec, vec], scratch_shapes=scratch,
            out_shape=[jax.ShapeDtypeStruct((m, n), F32), jax.ShapeDtypeStruct((m, n), BF16),
                       jax.ShapeDtypeStruct((1, n), F32)],
            compiler_params=_params(("arbitrary", "arbitrary", "arbitrary")))(a, b, x, w, dres)
    ins = [a, b] + ([res] if has_res else [])
    in_specs = [a_spec, b_spec] + ([o_spec] if has_res else [])
    return pl.pallas_call(
        body, name=name, grid=(m // tm, n // tn, nk), in_specs=in_specs, out_specs=o_spec,
        out_shape=jax.ShapeDtypeStruct((m, n), out_dtype), scratch_shapes=scratch,
        compiler_params=_params(("parallel", "parallel", "arbitrary")))(*ins)


def _rowwise(fn, rows, bcasts, outs, accs=(), *, tile, name):
    s = (rows[0][0] if isinstance(rows[0], tuple) else rows[0]).shape[-2]
    tile = min(tile, s)
    n_val = len(rows) + len(bcasts)
    intos = [(k, o) for k, o in enumerate(outs) if len(o) == 4]
    n_in, n_out = n_val + len(intos), len(outs)

    def row_spec(c):
        if isinstance(c, tuple):
            return pl.BlockSpec((c[0], tile, c[1]), lambda i: (0, i, 0))
        return pl.BlockSpec((tile, c), lambda i: (i, 0))

    def row_shape(c):
        return (c[0], s, c[1]) if isinstance(c, tuple) else (s, c)

    def window(width, cb):
        return pl.BlockSpec((tile, width), lambda i: (i, cb))

    def body(*refs):
        vals = fn(*[r[...] for r in refs[:n_val]])
        o_refs = refs[n_in:n_in + n_out]
        a_refs = refs[n_in + n_out:]
        for r, v in zip(o_refs, vals[:n_out]):
            if isinstance(v, list):
                for i, vi in enumerate(v):
                    r[i] = vi.astype(r.dtype)
            else:
                r[...] = v.astype(r.dtype)

        @pl.when(pl.program_id(0) == 0)
        def _():
            for r in a_refs:
                r[...] = jnp.zeros(r.shape, r.dtype)

        for r, v in zip(a_refs, vals[n_out:]):
            r[...] += v

    in_specs = [window(r[1], r[2]) if isinstance(r, tuple)
                else row_spec(r.shape[1] if r.ndim == 2 else (r.shape[0], r.shape[2])) for r in rows]
    in_specs += [pl.BlockSpec(b.shape, lambda i: (0, 0)) for b in bcasts]
    in_specs += [pl.BlockSpec(memory_space=pl.ANY) for _ in intos]
    out_specs = [window(o[0], o[3]) if len(o) == 4 else row_spec(o[0]) for o in outs]
    out_specs += [pl.BlockSpec(sh, lambda i: (0, 0)) for sh, _ in accs]
    out_shape = [jax.ShapeDtypeStruct(o[2].shape, o[2].dtype) if len(o) == 4
                 else jax.ShapeDtypeStruct(row_shape(o[0]), o[1]) for o in outs]
    out_shape += [jax.ShapeDtypeStruct(sh, dt) for sh, dt in accs]
    args = [r[0] if isinstance(r, tuple) else r for r in rows] + list(bcasts) + [o[2] for _, o in intos]
    return pl.pallas_call(body, name=name, grid=(s // tile,), in_specs=in_specs, out_specs=out_specs,
                          out_shape=out_shape, input_output_aliases={n_val + i: k for i, (k, _) in enumerate(intos)},
                          compiler_params=_params(("arbitrary",)))(*args)


def _final_loss(x, w, tgt, name):
    d = x.shape[1]

    def fn(x, t, w):
        r = lax.rsqrt(jnp.mean(x * x, axis=-1, keepdims=True) + RMS_EPS)
        xh = x * r
        err = xh * w - t
        part = jnp.sum(jnp.mean(err * err, axis=-1, keepdims=True), axis=0, keepdims=True) * 0.5
        dy = err * (1.0 / d)
        dxh = dy * w
        dx = r * (dxh - xh * jnp.mean(dxh * xh, axis=-1, keepdims=True))
        return dx, dx, part, jnp.sum(dy * xh, axis=0, keepdims=True)
    dx, dxb, part, dw = _rowwise(fn, [x, tgt], [w], [(d, F32), (d, BF16)], [((1, 1), F32), ((1, d), F32)],
                                 tile=256, name=name)
    return part, dx, dxb, dw


def _softplus_fwd(dtr, bias, name):
    def fn(r, b):
        v = r + b
        return (jnp.maximum(v, 0.0) + jnp.log(1.0 + jnp.exp(-jnp.abs(v))),)
    return _rowwise(fn, [dtr], [bias], [(LANES, F32)], tile=512, name=name)[0]


def _softplus_bwd(ddt, dtr, bias, n_heads, into, name):
    def fn(g, r, b):
        lane = lax.broadcasted_iota(jnp.int32, g.shape, 1)
        d = jnp.where(lane < n_heads, g * jax.nn.sigmoid(r + b), 0.0)
        return d, jnp.sum(d, axis=0, keepdims=True)
    return _rowwise(fn, [ddt, dtr], [bias], [(LANES, BF16, *into)], [((1, LANES), F32)], tile=512, name=name)


def _gnorm_fwd(y, z, w, n_groups, name):
    di = y.shape[1]
    gs = di // n_groups

    def fn(y, z, w):
        y2 = y * (z * jax.nn.sigmoid(z))
        out = []
        for g in range(n_groups):
            sl = y2[:, g * gs:(g + 1) * gs]
            r = lax.rsqrt(jnp.mean(sl * sl, axis=-1, keepdims=True) + GATED_NORM_EPS)
            out.append(sl * r)
        return (jnp.concatenate(out, axis=1) * w,)
    return _rowwise(fn, [y, z], [w], [(di, BF16)], tile=256, name=name)[0]


def _gnorm_bwd(dyn, y, z, w, n_groups, into, name):
    di = y.shape[1]
    gs = di // n_groups

    def fn(dyn, y, z, w):
        sig = jax.nn.sigmoid(z)
        sz = z * sig
        y2 = y * sz
        d2n = dyn * w
        dy2, yhat = [], []
        for g in range(n_groups):
            sl = y2[:, g * gs:(g + 1) * gs]
            dg = d2n[:, g * gs:(g + 1) * gs]
            r = lax.rsqrt(jnp.mean(sl * sl, axis=-1, keepdims=True) + GATED_NORM_EPS)
            yh = sl * r
            dy2.append(r * (dg - yh * jnp.mean(dg * yh, axis=-1, keepdims=True)))
            yhat.append(yh)
        dy2 = jnp.concatenate(dy2, axis=1)
        yhat = jnp.concatenate(yhat, axis=1)
        dz = dy2 * y * (sig * (1.0 + z * (1.0 - sig)))
        return dy2 * sz, dz, jnp.sum(dyn * yhat, axis=0, keepdims=True)
    return _rowwise(fn, [dyn, y, z], [w], [(di, F32), (di, BF16, *into)], [((1, di), F32)], tile=256, name=name)


def _lane_place(cols):
    rows = cols[0].shape[0]
    lane = lax.broadcasted_iota(jnp.int32, (rows, LANES), 1)
    out = jnp.zeros((rows, LANES), F32)
    for j, c in enumerate(cols):
        out = jnp.where(lane == j, c, out)
    return out


def _merge_heads(os_, lses, name):
    n = len(os_)
    n_kv, rep, hd = ATT_KV_HEADS_PER_GROUP, ATT_HEADS_PER_GROUP // ATT_KV_HEADS_PER_GROUP, ATT_HEAD_DIM

    def fn(*v):
        o, l = v[:n], v[n:]
        out, lse = [], []
        for h in range(n_kv):
            cols = []
            for j in range(rep):
                hh = h * rep + j
                lg = [li[h][:, j:j + 1] for li in l]
                m = functools.reduce(jnp.maximum, lg)
                e = [jnp.exp(x - m) for x in lg]
                tot = functools.reduce(jnp.add, e)
                acc = functools.reduce(jnp.add, [ei * oi[hh] for ei, oi in zip(e, o)])
                out.append(acc / tot)
                cols.append(m + jnp.log(tot))
            lse.append(_lane_place(cols))
        merged = jnp.concatenate(out, axis=1)
        return merged, merged, lse
    c = os_[0].shape[0] * hd
    return _rowwise(fn, list(os_) + list(lses), [], [(c, F32), (c, BF16), ((n_kv, LANES), F32)], tile=256, name=name)


def _delta_heads(do, o, name):
    n_kv, rep, hd = ATT_KV_HEADS_PER_GROUP, ATT_HEADS_PER_GROUP // ATT_KV_HEADS_PER_GROUP, ATT_HEAD_DIM

    def fn(do, o):
        p = do * o
        return ([_lane_place([jnp.sum(p[:, (h * rep + j) * hd:(h * rep + j + 1) * hd], axis=-1, keepdims=True)
                              for j in range(rep)]) for h in range(n_kv)],)
    return _rowwise(fn, [do, o], [], [((n_kv, LANES), F32)], tile=256, name=name)[0]


def _sum_slabs(recv, name):
    def body(r_ref, o_ref):
        acc = r_ref[0]
        for k in range(1, NDEV):
            acc = acc + r_ref[k]
        o_ref[...] = acc
    return pl.pallas_call(body, name=name, out_shape=jax.ShapeDtypeStruct(recv.shape[1:], F32),
                          compiler_params=_params())(recv)


def _shift_down(x, k):
    if k == 0:
        return x
    r = pltpu.roll(x, k, 0)
    row = lax.broadcasted_iota(jnp.int32, (SUBLANES, x.shape[1]), 0)
    return jnp.concatenate([jnp.where(row >= k, r[:SUBLANES], 0.0), r[SUBLANES:]], axis=0)


def _shift_up(x, k):
    if k == 0:
        return x
    s = x.shape[0]
    r = pltpu.roll(x, s - k, 0)
    row = lax.broadcasted_iota(jnp.int32, (SUBLANES, x.shape[1]), 0)
    return jnp.concatenate([r[:s - SUBLANES], jnp.where(row < SUBLANES - k, r[s - SUBLANES:], 0.0)], axis=0)


def _conv(x, w):
    kw = w.shape[0]
    return functools.reduce(jnp.add, [w[k:k + 1, :] * _shift_down(x, kw - 1 - k) for k in range(kw)])


def _conv_t(dy, w):
    kw = w.shape[0]
    return functools.reduce(jnp.add, [w[k:k + 1, :] * _shift_up(dy, kw - 1 - k) for k in range(kw)])


def _conv_dw(x, dy, dw_ref):
    kw = dw_ref.shape[0]
    for k in range(kw):
        dw_ref[k:k + 1, :] = jnp.sum(dy * _shift_down(x, kw - 1 - k), axis=0, keepdims=True)


def _dsilu(pre):
    sig = jax.nn.sigmoid(pre)
    return sig * (1.0 + pre * (1.0 - sig))


def _col_specs(s, c, kw, tc):
    return (pl.BlockSpec((s, tc), lambda j: (0, j)), pl.BlockSpec((kw, tc), lambda j: (0, j)),
            pl.BlockSpec((1, tc), lambda j: (0, j)))


def _conv_silu_fwd(x, col0, w, b, name):
    s, c = x.shape[0], w.shape[1]
    tc = LANES
    xs, ws, bs = _col_specs(s, c, w.shape[0], tc)
    xwin = pl.BlockSpec((s, tc), lambda j: (0, j + col0 // tc))

    def body(x_ref, w_ref, b_ref, o_ref):
        pre = _conv(x_ref[...], w_ref[...]) + b_ref[...]
        o_ref[...] = pre * jax.nn.sigmoid(pre)
    return pl.pallas_call(body, name=name, grid=(c // tc,), in_specs=[xwin, ws, bs], out_specs=xs,
                          out_shape=jax.ShapeDtypeStruct((s, c), F32), compiler_params=_params(("parallel",)))(x, w, b)


def _conv_silu_bwd(x, col0, w, b, dy, into, name):
    s, c = x.shape[0], w.shape[1]
    tc = LANES
    xs, ws, bs = _col_specs(s, c, w.shape[0], tc)
    xwin = pl.BlockSpec((s, tc), lambda j: (0, j + col0 // tc))

    def body(x_ref, w_ref, b_ref, dy_ref, _, dx_ref, dw_ref, db_ref):
        xv, wv = x_ref[...], w_ref[...]
        pre = _conv(xv, wv) + b_ref[...]
        dpre = dy_ref[...] * _dsilu(pre)
        dx_ref[...] = _conv_t(dpre, wv).astype(dx_ref.dtype)
        _conv_dw(xv, dpre, dw_ref)
        db_ref[...] = jnp.sum(dpre, axis=0, keepdims=True)
    return pl.pallas_call(
        body, name=name, grid=(c // tc,), in_specs=[xwin, ws, bs, xs, pl.BlockSpec(memory_space=pl.ANY)],
        out_specs=[xwin, ws, bs], input_output_aliases={4: 0},
        out_shape=[jax.ShapeDtypeStruct(into.shape, into.dtype), jax.ShapeDtypeStruct(w.shape, F32),
                   jax.ShapeDtypeStruct((1, c), F32)],
        compiler_params=_params(("parallel",)))(x, w, b, dy, into)


def _gate_specs(s, f, kw):
    nt = f // LANES
    return (pl.BlockSpec((s, LANES), lambda j: (0, j)), pl.BlockSpec((s, LANES), lambda j: (0, j + nt)),
            pl.BlockSpec((kw, LANES), lambda j: (0, j)), pl.BlockSpec((kw, LANES), lambda j: (0, j + nt)))


def _ffn_gate_fwd(u, w, name):
    s, f = u.shape[0], u.shape[1] // 2
    gs, vs, wgs, wvs = _gate_specs(s, f, w.shape[0])

    def body(g_ref, v_ref, wg_ref, wv_ref, o_ref):
        g = _conv(g_ref[...], wg_ref[...])
        v = _conv(v_ref[...], wv_ref[...])
        o_ref[...] = (g * jax.nn.sigmoid(g) * v).astype(o_ref.dtype)
    return pl.pallas_call(body, name=name, grid=(f // LANES,), in_specs=[gs, vs, wgs, wvs], out_specs=gs,
                          out_shape=jax.ShapeDtypeStruct((s, f), BF16),
                          compiler_params=_params(("parallel",)))(u, u, w, w)


def _ffn_gate_bwd(u, w, df, name):
    s, f = u.shape[0], u.shape[1] // 2
    kw = w.shape[0]
    gs, vs, wgs, wvs = _gate_specs(s, f, kw)

    def body(g_ref, v_ref, wg_ref, wv_ref, df_ref, du_ref, dw_ref):
        gp, vp, wgv, wvv = g_ref[...], v_ref[...], wg_ref[...], wv_ref[...]
        g = _conv(gp, wgv)
        v = _conv(vp, wvv)
        dfv = df_ref[...]
        dg = dfv * v * _dsilu(g)
        dv = dfv * (g * jax.nn.sigmoid(g))
        du_ref[0] = _conv_t(dg, wgv).astype(du_ref.dtype)
        du_ref[1] = _conv_t(dv, wvv).astype(du_ref.dtype)
        _conv_dw(gp, dg, dw_ref.at[0])
        _conv_dw(vp, dv, dw_ref.at[1])
    return pl.pallas_call(
        body, name=name, grid=(f // LANES,), in_specs=[gs, vs, wgs, wvs, gs],
        out_specs=[pl.BlockSpec((2, s, LANES), lambda j: (0, 0, j)), pl.BlockSpec((2, kw, LANES), lambda j: (0, 0, j))],
        out_shape=[jax.ShapeDtypeStruct((2, s, f), BF16), jax.ShapeDtypeStruct((2, kw, f), F32)],
        compiler_params=_params(("parallel",)))(u, u, w, w, df)


def _ssd_common(dt, alog, n_heads):
    ln = dt.shape[0]
    lane = lax.broadcasted_iota(jnp.int32, (1, LANES), 1)
    a = jnp.where(lane < n_heads, -jnp.exp(alog), 0.0)
    row = lax.broadcasted_iota(jnp.int32, (ln, ln), 0)
    col = lax.broadcasted_iota(jnp.int32, (ln, ln), 1)
    tril = col <= row
    acs = _dot(tril.astype(F32), dt * a, precision=HIGHEST)
    return a, acs, acs.T, tril


def _split(x, n):
    out = []
    for _ in range(n):
        piece = x.astype(BF16)
        out.append(piece)
        x = x - piece.astype(F32)
    return out


def _spread(x, onehot, n=2):
    return functools.reduce(jnp.add, [_dot(piece, onehot) for piece in _split(x, n)])


def _head_maps(di, p):
    e = (jnp.arange(di, dtype=jnp.int32)[None, :] // p == jnp.arange(LANES, dtype=jnp.int32)[:, None]).astype(BF16)
    return e, e.T


def _ssd_wide(dt, acs, acs_t, dskip, e_ref, et_ref):
    ln = dt.shape[0]
    last = acs[ln - 1:ln, :]
    stack = jnp.concatenate([dt, jnp.exp(acs), jnp.exp(last - acs), jnp.broadcast_to(dskip, (8, LANES))], axis=0)
    wide = _spread(stack, e_ref[...], n=1)
    tb = jnp.exp(jnp.broadcast_to(acs_t[:, ln - 1:ln], (LANES, LANES)))
    texp = functools.reduce(jnp.add, [_dot(et_ref[...], piece) for piece in _split(tb, 3)])
    return wide[:ln], wide[ln:2 * ln], wide[2 * ln:3 * ln], wide[3 * ln:3 * ln + 1], texp


def _ssd_fwd(xbc, dt, alog, dskip, di, n_heads, n_groups, name):
    s, convd = xbc.shape
    ln, p, ns = SSM_CHUNK, SSM_HEAD_DIM, SSM_D_STATE
    nc, hg = s // ln, n_heads // n_groups
    gw = hg * p
    e64, e64t = _head_maps(di, p)

    def body(x_ref, dt_ref, alog_ref, d_ref, e_ref, et_ref, y_ref, prev_ref, st):
        @pl.when(pl.program_id(0) == 0)
        def _():
            st[...] = jnp.zeros(st.shape, F32)

        dt = dt_ref[...]
        _, acs, acs_t, tril = _ssd_common(dt, alog_ref[...], n_heads)
        dte, ee, dse, dske, texp = _ssd_wide(dt, acs, acs_t, d_ref[...], e_ref, et_ref)
        x = x_ref[:, :di]
        xdt = x * dte
        xdtb = xdt.astype(BF16)
        xdsb = (xdt * dse).astype(BF16)
        for g in range(n_groups):
            rows = slice(g * gw, (g + 1) * gw)
            bg = x_ref[:, di + g * ns:di + (g + 1) * ns].astype(BF16)
            cg = x_ref[:, di + (n_groups + g) * ns:di + (n_groups + g + 1) * ns].astype(BF16)
            gm = _dot(cg, bg, _NT)
            prev = st[rows, :]
            prev_ref[0, rows, :] = prev
            yo = _dot(cg, prev.astype(BF16), _NT)
            for j in range(hg):
                h = g * hg + j
                seg = acs[:, h:h + 1] - acs_t[h:h + 1, :]
                m = jnp.where(tril, gm * jnp.exp(jnp.where(tril, seg, 0.0)), 0.0)
                y_ref[:, h * p:(h + 1) * p] = _dot(m.astype(BF16), xdtb[:, h * p:(h + 1) * p])
            y_ref[:, rows] = y_ref[:, rows] + yo * ee[:, rows] + x[:, rows] * dske[:, rows]
            st[rows, :] = prev * texp[rows, :] + _dot(xdsb[:, rows], bg, _TN)

    vec = pl.BlockSpec((1, LANES), lambda c: (0, 0))
    return pl.pallas_call(
        body, name=name, grid=(nc,),
        in_specs=[pl.BlockSpec((ln, convd), lambda c: (c, 0)), pl.BlockSpec((ln, LANES), lambda c: (c, 0)), vec, vec,
                  pl.BlockSpec(e64.shape, lambda c: (0, 0)), pl.BlockSpec(e64t.shape, lambda c: (0, 0))],
        out_specs=[pl.BlockSpec((ln, di), lambda c: (c, 0)), pl.BlockSpec((1, di, ns), lambda c: (c, 0, 0))],
        out_shape=[jax.ShapeDtypeStruct((s, di), F32), jax.ShapeDtypeStruct((nc, di, ns), F32)],
        scratch_shapes=[pltpu.VMEM((di, ns), F32)],
        compiler_params=_params(("arbitrary",)))(xbc, dt, alog, dskip, e64, e64t)


def _ssd_bwd(xbc, dt, alog, dskip, prev_all, dy, di, n_heads, n_groups, name):
    s, convd = xbc.shape
    ln, p, ns = SSM_CHUNK, SSM_HEAD_DIM, SSM_D_STATE
    nc, hg = s // ln, n_heads // n_groups
    gw = hg * p
    e64, e64t = _head_maps(di, p)

    def body(x_ref, dt_ref, alog_ref, d_ref, e_ref, et_ref, prev_ref, dy_ref,
             dx_ref, ddt_ref, da_ref, dd_ref, dh, yo_ref, w_ref):
        step = pl.program_id(0)

        @pl.when(step == 0)
        def _():
            dh[...] = jnp.zeros(dh.shape, F32)
            da_ref[...] = jnp.zeros(da_ref.shape, F32)
            dd_ref[...] = jnp.zeros(dd_ref.shape, F32)

        dt = dt_ref[...]
        a, acs, acs_t, tril = _ssd_common(dt, alog_ref[...], n_heads)
        dte, ee, dse, dske, texp = _ssd_wide(dt, acs, acs_t, d_ref[...], e_ref, et_ref)
        row = lax.broadcasted_iota(jnp.int32, (ln, ln), 0)
        col = lax.broadcasted_iota(jnp.int32, (ln, ln), 1)
        triu = col >= row
        x = x_ref[:, :di]
        dy = dy_ref[...]
        xdt = x * dte
        xdtb = xdt.astype(BF16)
        xdsb = (xdt * dse).astype(BF16)
        dyb = dy.astype(BF16)
        dyob = (dy * ee).astype(BF16)
        dhn = dh[...]
        dhb = dhn.astype(BF16)
        per_head = functools.reduce(jnp.add, [_dot(e_ref[...], piece) for piece in _split(dhn * prev_ref[0], 2)])
        ones8 = jnp.ones((8, LANES), BF16)
        dtt = functools.reduce(jnp.add, [_dot(ones8, piece, _NT) for piece in _split(per_head, 2)])[0:1]
        dacs_c = jnp.zeros((ln, LANES), F32)
        dacs_r = jnp.zeros((LANES, ln), F32)
        for g in range(n_groups):
            rows = slice(g * gw, (g + 1) * gw)
            bg = x_ref[:, di + g * ns:di + (g + 1) * ns].astype(BF16)
            cg = x_ref[:, di + (n_groups + g) * ns:di + (n_groups + g + 1) * ns].astype(BF16)
            gmt = _dot(bg, cg, _NT)
            prevb = prev_ref[0, rows, :].astype(BF16)
            dcg = _dot(dyob[:, rows], prevb)
            dh[rows, :] = texp[rows, :] * dhn[rows, :] + _dot(dyob[:, rows], cg, _TN)
            w = _dot(bg, dhb[rows, :], _NT)
            dbg = _dot(xdsb[:, rows], dhb[rows, :])
            yo_ref[:, rows] = _dot(cg, prevb, _NT)
            w_ref[:, rows] = w
            dgmt = jnp.zeros((ln, ln), F32)
            q_hi, q_lo = [], []
            for j in range(hg):
                h = g * hg + j
                segt = acs_t[h:h + 1, :] - acs[:, h:h + 1]
                dect = jnp.where(triu, jnp.exp(jnp.where(triu, segt, 0.0)), 0.0)
                dyh, xh = dyb[:, h * p:(h + 1) * p], xdtb[:, h * p:(h + 1) * p]
                mt = gmt * dect
                dmt = _dot(xh, dyh, _NT)
                dx_ref[:, h * p:(h + 1) * p] = _dot(mt.astype(BF16), dyh)
                dgmt = dgmt + dmt * dect
                hi, lo = _split(dmt * mt, 2)
                q_hi.append(hi)
                q_lo.append(lo)
            sel_c = (lax.broadcasted_iota(jnp.int32, (hg * ln, LANES), 1)
                     == g * hg + lax.broadcasted_iota(jnp.int32, (hg * ln, LANES), 0) // ln).astype(BF16)
            sel_r = (lax.broadcasted_iota(jnp.int32, (LANES, hg * ln), 0)
                     == g * hg + lax.broadcasted_iota(jnp.int32, (LANES, hg * ln), 1) // ln).astype(BF16)
            for pieces in (q_hi, q_lo):
                dacs_c = dacs_c - _dot(jnp.concatenate(pieces, axis=1), sel_c)
                dacs_r = dacs_r + _dot(sel_r, jnp.concatenate(pieces, axis=0))
            dgb = dgmt.astype(BF16)
            dx_ref[:, di + g * ns:di + (g + 1) * ns] = dbg + _dot(dgb, cg)
            dx_ref[:, di + (n_groups + g) * ns:di + (n_groups + g + 1) * ns] = dcg + _dot(dgb, bg, _TN)

        wds = w_ref[...] * dse
        dxdt = dx_ref[:, :di] + wds
        red = _spread(jnp.concatenate([dxdt * x, dy * yo_ref[...] * ee, xdt * wds, dy * x], axis=0), et_ref[...], n=1)
        ddt_x, r_off, r_state, ddr = red[:ln], red[ln:2 * ln], red[2 * ln:3 * ln], red[3 * ln:]
        dx_ref[:, :di] = dxdt * dte + dy * dske
        rowi = lax.broadcasted_iota(jnp.int32, (ln, LANES), 0)
        dlast = jnp.sum(r_state, axis=0, keepdims=True) + dtt * jnp.exp(acs[ln - 1:ln, :])
        dacs = r_off - r_state + dacs_c + dacs_r.T + jnp.where(rowi == ln - 1, dlast, 0.0)
        dadt = _dot(triu.astype(F32), dacs, precision=HIGHEST)
        ddt_ref[...] = dadt * a + ddt_x
        da_ref[...] += jnp.sum(dadt * dt, axis=0, keepdims=True)
        dd_ref[...] += jnp.sum(ddr, axis=0, keepdims=True)

        @pl.when(step == nc - 1)
        def _():
            da_ref[...] = da_ref[...] * a

    vec = pl.BlockSpec((1, LANES), lambda c: (0, 0))
    rev = lambda c: (nc - 1 - c, 0)
    return pl.pallas_call(
        body, name=name, grid=(nc,),
        in_specs=[pl.BlockSpec((ln, convd), rev), pl.BlockSpec((ln, LANES), rev), vec, vec,
                  pl.BlockSpec(e64.shape, lambda c: (0, 0)), pl.BlockSpec(e64t.shape, lambda c: (0, 0)),
                  pl.BlockSpec((1, di, ns), lambda c: (nc - 1 - c, 0, 0)), pl.BlockSpec((ln, di), rev)],
        out_specs=[pl.BlockSpec((ln, convd), rev), pl.BlockSpec((ln, LANES), rev), vec, vec],
        out_shape=[jax.ShapeDtypeStruct((s, convd), F32), jax.ShapeDtypeStruct((s, LANES), F32),
                   jax.ShapeDtypeStruct((1, LANES), F32), jax.ShapeDtypeStruct((1, LANES), F32)],
        scratch_shapes=[pltpu.VMEM((di, ns), F32), pltpu.VMEM((ln, di), F32), pltpu.VMEM((ln, di), F32)],
        compiler_params=_params(("arbitrary",)))(xbc, dt, alog, dskip, e64, e64t, prev_all, dy)


def _rot_tables(s):
    half = ROPE_DIM // 2
    inv_freq = jnp.power(jnp.float32(ROPE_THETA), -jnp.arange(0, ROPE_DIM, 2, dtype=F32) / ROPE_DIM)
    pos = jnp.arange(s, dtype=jnp.int32)
    ang = pos.astype(F32)[:, None] * inv_freq[None, :]
    cos, sin = jnp.cos(ang), jnp.sin(ang)
    zero = jnp.zeros((s, ATT_HEAD_DIM - ROPE_DIM), F32)
    cf = jnp.concatenate([cos, cos, jnp.ones_like(zero)], axis=1)
    s1 = jnp.concatenate([-sin, jnp.zeros_like(sin), zero], axis=1)
    s2 = jnp.concatenate([jnp.zeros_like(sin), sin, zero], axis=1)
    assert half * 2 == ROPE_DIM
    return cf, s1, s2


def _rot(x, tabs, sign):
    cf, s1, s2 = tabs
    half = ROPE_DIM // 2
    left = pltpu.roll(x, ATT_HEAD_DIM - half, 1)
    right = pltpu.roll(x, half, 1)
    return x * cf + sign * (left * s1 + right * s2)


def _att_masks(n, n_blk, rep):
    b = ATT_BLOCK
    row = lax.broadcasted_iota(jnp.int32, (rep * b, b), 0) & (b - 1)
    col = lax.broadcasted_iota(jnp.int32, (rep * b, b), 1)
    off = jnp.where(n % n_blk != 0, 0, 2 * b)
    return col <= row, col >= row + off


def _rot_heads(x, tabs, width, sign, out_dtype, name):
    s = x.shape[0]
    hd = ATT_HEAD_DIM
    tile = min(512, s)

    def body(x_ref, cf, s1, s2, o_ref):
        t = (cf[...], s1[...], s2[...])
        for j in range(width // hd):
            o_ref[:, j * hd:(j + 1) * hd] = _rot(x_ref[:, j * hd:(j + 1) * hd], t, sign).astype(o_ref.dtype)

    tab = pl.BlockSpec((tile, hd), lambda i: (i, 0))
    return pl.pallas_call(
        body, name=name, grid=(s // tile,), in_specs=[pl.BlockSpec((tile, width), lambda i: (i, 0)), tab, tab, tab],
        out_specs=pl.BlockSpec((tile, width), lambda i: (i, 0)), out_shape=jax.ShapeDtypeStruct((s, width), out_dtype),
        compiler_params=_params(("parallel",)))(x, *tabs)


def _kv_grad(dk_rot, dv, tabs, name):
    s, width = dk_rot.shape
    hd = ATT_HEAD_DIM
    tile = min(512, s)

    def body(k_ref, v_ref, cf, s1, s2, o_ref):
        t = (cf[...], s1[...], s2[...])
        for j in range(width // hd):
            o_ref[:, j * hd:(j + 1) * hd] = _rot(k_ref[:, j * hd:(j + 1) * hd], t, -1.0).astype(o_ref.dtype)
        o_ref[:, width:] = v_ref[...].astype(o_ref.dtype)

    tab = pl.BlockSpec((tile, hd), lambda i: (i, 0))
    half = pl.BlockSpec((tile, width), lambda i: (i, 0))
    return pl.pallas_call(
        body, name=name, grid=(s // tile,), in_specs=[half, half, tab, tab, tab],
        out_specs=pl.BlockSpec((tile, 2 * width), lambda i: (i, 0)),
        out_shape=jax.ShapeDtypeStruct((s, 2 * width), BF16), compiler_params=_params(("parallel",)))(dk_rot, dv, *tabs)


def _rows_of(r, dil):
    return pl.ds(r, ATT_BLOCK, stride=dil) if dil > 1 else slice(None)


def _nat_specs(g, dil, n_kv_all, cur, prv):
    b, hd = ATT_BLOCK * dil, ATT_HEAD_DIM
    n_kv = ATT_KV_HEADS_PER_GROUP
    rep = ATT_HEADS_PER_GROUP // n_kv
    q_all = [pl.BlockSpec((b, hd), lambda h, n, j=j: (cur(n), (g * n_kv + h) * rep + j)) for j in range(rep)]
    q_own = [pl.BlockSpec((b, hd), lambda h, n, j=j: (cur(n), h * rep + j)) for j in range(rep)]
    hm_all = pl.BlockSpec((rep, b, hd), lambda h, n: (g * n_kv + h, cur(n), 0))
    hm_own = pl.BlockSpec((rep, b, hd), lambda h, n: (h, cur(n), 0))
    kc = pl.BlockSpec((b, hd), lambda h, n: (cur(n), g * n_kv + h))
    kp = pl.BlockSpec((b, hd), lambda h, n: (prv(n), g * n_kv + h))
    vc = pl.BlockSpec((b, hd), lambda h, n: (cur(n), n_kv_all + g * n_kv + h))
    vp = pl.BlockSpec((b, hd), lambda h, n: (prv(n), n_kv_all + g * n_kv + h))
    tab = pl.BlockSpec((b, hd), lambda h, n: (cur(n), 0))
    stat = pl.BlockSpec((None, b, LANES), lambda h, n: (h, cur(n), 0))
    return q_all, q_own, hm_all, hm_own, kc, kp, vc, vp, tab, stat


def _head_cols(stat, rep):
    return jnp.concatenate([jnp.broadcast_to(stat[:, j:j + 1], stat.shape) for j in range(rep)], axis=0)


def _attn_fwd_nat(q_all, k_rot, kv, tabs, g, dil, name):
    s = q_all.shape[0]
    b, hd = ATT_BLOCK, ATT_HEAD_DIM
    nbn = s // (b * dil)
    n_kv = ATT_KV_HEADS_PER_GROUP
    rep = ATT_HEADS_PER_GROUP // n_kv
    n_kv_all = k_rot.shape[1] // hd
    scale = hd ** -0.5

    def body(*refs):
        q_refs = refs[:rep]
        kc_ref, kp_ref, vc_ref, vp_ref, cf, s1, s2, o_ref, lse_ref = refs[rep:]
        mc, mp = _att_masks(jnp.where(pl.program_id(1) > 0, 1, 0), 2, rep)
        for r in range(dil):
            sl = _rows_of(r, dil)
            tc = (cf[sl, :], s1[sl, :], s2[sl, :])
            q4 = (jnp.concatenate([_rot(q_ref[sl, :], tc, 1.0) for q_ref in q_refs], axis=0) * scale).astype(BF16)
            kc, kp = kc_ref[sl, :].astype(BF16), kp_ref[sl, :].astype(BF16)
            sc = jnp.where(mc, _dot(q4, kc, _NT), NEG)
            sp = jnp.where(mp, _dot(q4, kp, _NT), NEG)
            m = jnp.maximum(jnp.max(sc, axis=1, keepdims=True), jnp.max(sp, axis=1, keepdims=True))
            pc, pp = jnp.exp(sc - m), jnp.exp(sp - m)
            l = jnp.sum(pc, axis=1, keepdims=True) + jnp.sum(pp, axis=1, keepdims=True)
            o = (_dot(pc.astype(BF16), vc_ref[sl, :].astype(BF16))
                 + _dot(pp.astype(BF16), vp_ref[sl, :].astype(BF16))) / l
            lse = m + jnp.log(l)
            for j in range(rep):
                o_ref[j, sl, :] = o[j * b:(j + 1) * b]
            lse_ref[sl, :] = _lane_place([lse[j * b:(j + 1) * b] for j in range(rep)])

    cur = lambda n: n
    prv = lambda n: jnp.maximum(n - 1, 0)
    q_specs, _, _, hm_own, kc, kp, vc, vp, tab, stat = _nat_specs(g, dil, n_kv_all, cur, prv)
    return pl.pallas_call(
        body, name=name, grid=(n_kv, nbn), in_specs=[*q_specs, kc, kp, vc, vp, tab, tab, tab], out_specs=[hm_own, stat],
        out_shape=[jax.ShapeDtypeStruct((ATT_HEADS_PER_GROUP, s, hd), F32), jax.ShapeDtypeStruct((n_kv, s, LANES), F32)],
        compiler_params=_params(("parallel", "arbitrary")))(*([q_all] * rep), k_rot, k_rot, kv, kv, *tabs)


def _attn_bwd_nat(q_all, k_rot, kv, do, lse, delta, tabs, grads, g, dil, name):
    s = q_all.shape[0]
    b, hd = ATT_BLOCK, ATT_HEAD_DIM
    nbn = s // (b * dil)
    n_kv = ATT_KV_HEADS_PER_GROUP
    rep = ATT_HEADS_PER_GROUP // n_kv
    n_kv_all = k_rot.shape[1] // hd
    scale = hd ** -0.5

    def body(*refs):
        q_refs, do_refs = refs[:rep], refs[rep:2 * rep]
        (lse_ref, dl_ref, kc_ref, kp_ref, vc_ref, vp_ref, cf, s1, s2, _, _, _,
         dq_ref, dk_ref, dv_ref, ck, cv) = refs[2 * rep:]
        n = pl.program_id(1)

        @pl.when(n == 0)
        def _():
            ck[...] = jnp.zeros(ck.shape, F32)
            cv[...] = jnp.zeros(cv.shape, F32)

        @pl.when(n < nbn)
        def _():
            mc, mp = _att_masks(jnp.where(n > 0, 1, 0), 2, rep)
            for r in range(dil):
                sl = _rows_of(r, dil)
                own = slice(r * b, (r + 1) * b)
                tc = (cf[sl, :], s1[sl, :], s2[sl, :])
                q4 = (jnp.concatenate([_rot(q_ref[sl, :], tc, 1.0) for q_ref in q_refs], axis=0) * scale).astype(BF16)
                do4 = jnp.concatenate([do_ref[sl, :] for do_ref in do_refs], axis=0).astype(BF16)
                lse4 = _head_cols(lse_ref[sl, :], rep)
                dl4 = _head_cols(dl_ref[sl, :], rep)
                kc, kp = kc_ref[sl, :].astype(BF16), kp_ref[sl, :].astype(BF16)
                vc, vp = vc_ref[sl, :].astype(BF16), vp_ref[sl, :].astype(BF16)
                pc = jnp.where(mc, jnp.exp(_dot(q4, kc, _NT) - lse4), 0.0)
                pp = jnp.where(mp, jnp.exp(_dot(q4, kp, _NT) - lse4), 0.0)
                dsc = (pc * (_dot(do4, vc, _NT) - dl4)).astype(BF16)
                dsp = (pp * (_dot(do4, vp, _NT) - dl4)).astype(BF16)
                dq4 = (_dot(dsc, kc) + _dot(dsp, kp)) * scale
                for j in range(rep):
                    dq_ref[j, sl, :] = _rot(dq4[j * b:(j + 1) * b], tc, -1.0)
                dk_ref[sl, :] = ck[own, :] + _dot(dsp, q4, _TN)
                dv_ref[sl, :] = cv[own, :] + _dot(pp.astype(BF16), do4, _TN)
                ck[own, :] = _dot(dsc, q4, _TN)
                cv[own, :] = _dot(pc.astype(BF16), do4, _TN)

        @pl.when(n == nbn)
        def _():
            for r in range(dil):
                sl = _rows_of(r, dil)
                dk_ref[sl, :] = ck[r * b:(r + 1) * b, :]
                dv_ref[sl, :] = cv[r * b:(r + 1) * b, :]

    cur = lambda n: jnp.minimum(n, nbn - 1)
    prv = lambda n: jnp.maximum(n - 1, 0)
    q_specs, do_specs, hm_all, _, kc, kp, vc, vp, tab, stat = _nat_specs(g, dil, n_kv_all, cur, prv)
    anyspace = pl.BlockSpec(memory_space=pl.ANY)
    n_in = 2 * rep + 9
    return pl.pallas_call(
        body, name=name, grid=(n_kv, nbn + 1),
        in_specs=[*q_specs, *do_specs, stat, stat, kc, kp, vc, vp, tab, tab, tab, anyspace, anyspace, anyspace],
        out_specs=[hm_all, kp, kp], out_shape=[jax.ShapeDtypeStruct(a.shape, a.dtype) for a in grads],
        input_output_aliases={n_in: 0, n_in + 1: 1, n_in + 2: 2},
        scratch_shapes=[pltpu.VMEM((dil * b, hd), F32), pltpu.VMEM((dil * b, hd), F32)],
        compiler_params=_params(("parallel", "arbitrary"), VMEM_LIMIT_ATTN_BWD_BYTES))(
            *([q_all] * rep), *([do] * rep), lse, delta, k_rot, k_rot, kv, kv, *tabs, *grads)


def _adamw(g_slabs, w, m, v, name, row0=0, prior=None):
    kk, r, c = g_slabs.shape
    tile = r if r <= 256 else _pick_rows(r, 256)
    off = row0 // tile
    assert off * tile == row0

    def body(g_ref, w_ref, m_ref, v_ref, *rest):
        go_ref, d_ref, mo_ref, vo_ref = rest[-4:]
        g = g_ref[0].astype(F32)
        for k in range(1, kk):
            g = g + g_ref[k].astype(F32)
        m2 = ADAM_B1 * m_ref[...] + (1.0 - ADAM_B1) * g
        v2 = ADAM_B2 * v_ref[...] + (1.0 - ADAM_B2) * jnp.square(g)
        m_hat = m2 / (1.0 - ADAM_B1 ** ADAM_STEP)
        v_hat = v2 / (1.0 - ADAM_B2 ** ADAM_STEP)
        go_ref[...] = g
        d_ref[...] = -ADAM_LR * (m_hat / (jnp.sqrt(v_hat) + ADAM_EPS) + ADAM_WD * w_ref[...])
        mo_ref[...] = m2
        vo_ref[...] = v2

    spec = pl.BlockSpec((tile, c), lambda i: (i + off, 0))
    prior = list(prior) if prior is not None else []
    return pl.pallas_call(
        body, name=name, grid=(r // tile,),
        in_specs=[pl.BlockSpec((kk, tile, c), lambda i: (0, i, 0)), spec, spec, spec]
        + [pl.BlockSpec(memory_space=pl.ANY)] * len(prior),
        out_specs=[spec] * 4, out_shape=[jax.ShapeDtypeStruct(w.shape, F32)] * 4,
        input_output_aliases={4 + i: i for i in range(len(prior))},
        compiler_params=_params(("parallel",)))(g_slabs, w, m, v, *prior)


def _pick_rows(r, pref):
    t = (pref // 16) * 16
    while t >= 16:
        if r % t == 0:
            return t
        t -= 16
    return r


def _coords():
    return lax.axis_index("x"), lax.axis_index("y"), lax.axis_index("c")


def _dev_index(px, py, pc):
    return 4 * px + 2 * py + pc


def _all_gather(shards, name):
    na = len(shards)

    def body(*refs):
        ins, outs = refs[:na], refs[na:2 * na]
        send_sems, recv_sems, local_sems = refs[2 * na:]
        x, y, c = _coords()
        me, sibling = (x, y, c), (x, y, 1 - c)
        chips = [(1 - x, y), (x, 1 - y), (1 - x, 1 - y)]

        def copy(a, k, block, to, src=None):
            dst = outs[a].at[_dev_index(*block)]
            return pltpu.make_async_remote_copy(
                src_ref=dst if src is None else src, dst_ref=dst, send_sem=send_sems.at[a * 7 + k],
                recv_sem=recv_sems.at[a * 7 + k], device_id=to, device_id_type=MESH)

        mine = [pltpu.make_async_copy(ins[a], outs[a].at[_dev_index(*me)], local_sems.at[a]) for a in range(na)]
        for cp in mine:
            cp.start()
        first = []
        for a in range(na):
            first.append(copy(a, 0, me, sibling, src=ins[a]))
            first += [copy(a, 1 + j, me, (*chip, c), src=ins[a]) for j, chip in enumerate(chips)]
        for cp in first:
            cp.start()
        passed = []
        for j, chip in enumerate(chips):
            for a in range(na):
                copy(a, 1 + j, (*chip, c), me).wait_recv()
                cp = copy(a, 4 + j, (*chip, c), sibling)
                cp.start()
                passed.append(cp)
        for a in range(na):
            copy(a, 0, sibling, me).wait_recv()
            for j, chip in enumerate(chips):
                copy(a, 4 + j, (*chip, 1 - c), me).wait_recv()
        for cp in first + passed:
            cp.wait_send()
        for cp in mine:
            cp.wait()

    hbm = pl.BlockSpec(memory_space=pl.ANY)
    return pl.pallas_call(
        body, name=name, in_specs=[hbm] * na, out_specs=[hbm] * na,
        out_shape=[jax.ShapeDtypeStruct((NDEV,) + s.shape, s.dtype) for s in shards],
        scratch_shapes=[pltpu.SemaphoreType.DMA((7 * na,)), pltpu.SemaphoreType.DMA((7 * na,)),
                        pltpu.SemaphoreType.DMA((na,))])(*shards)


def _exchange(slabs, whole, name, after=()):
    ns, nw = len(slabs), len(whole)
    na = ns + nw
    nb = len(after)

    def body(*refs):
        ins, outs = refs[:na], refs[na + nb:2 * na + nb]
        send_sems, recv_sems, local_sems = refs[2 * na + nb:]
        x, y, c = _coords()
        me = _dev_index(x, y, c)

        def src_of(a, p):
            return ins[a].at[p] if a < ns else ins[a]

        def copy(a, k, peer):
            p = _dev_index(*peer)
            return pltpu.make_async_remote_copy(
                src_ref=src_of(a, p), dst_ref=outs[a].at[me], send_sem=send_sems.at[a * 7 + k - 1],
                recv_sem=recv_sems.at[a * 7 + k - 1], device_id=peer, device_id_type=MESH)

        def arrival(a, k, peer):
            p = _dev_index(*peer)
            return pltpu.make_async_remote_copy(
                src_ref=src_of(a, p), dst_ref=outs[a].at[p], send_sem=send_sems.at[a * 7 + k - 1],
                recv_sem=recv_sems.at[a * 7 + k - 1], device_id=peer, device_id_type=MESH)

        mine = [pltpu.make_async_copy(src_of(a, me), outs[a].at[me], local_sems.at[a]) for a in range(na)]
        for cp in mine:
            cp.start()
        peers = [(k, (x ^ (k >> 2), y ^ ((k >> 1) & 1), c ^ (k & 1))) for k in range(1, NDEV)]
        sent = [copy(a, k, peer) for k, peer in peers for a in range(na)]
        for cp in sent:
            cp.start()
        for k, peer in peers:
            for a in range(na):
                arrival(a, k, peer).wait_recv()
        for cp in sent:
            cp.wait_send()
        for cp in mine:
            cp.wait()

    hbm = pl.BlockSpec(memory_space=pl.ANY)
    out_shape = [jax.ShapeDtypeStruct(s.shape, s.dtype) for s in slabs]
    out_shape += [jax.ShapeDtypeStruct((NDEV,) + w.shape, w.dtype) for w in whole]
    return pl.pallas_call(
        body, name=name, in_specs=[hbm] * (na + nb), out_specs=[hbm] * na, out_shape=out_shape,
        scratch_shapes=[pltpu.SemaphoreType.DMA((7 * na,)), pltpu.SemaphoreType.DMA((7 * na,)),
                        pltpu.SemaphoreType.DMA((na,))])(*slabs, *whole, *after)


_HBM = pl.BlockSpec(memory_space=pltpu.HBM)
_SEM = pl.BlockSpec(memory_space=pltpu.SEMAPHORE)
_EFFECT = pltpu.SideEffectType.DATAFLOW_SIDE_EFFECTING


def _peers(x, y, c):
    return [(k, (x ^ (k >> 2), y ^ ((k >> 1) & 1), c ^ (k & 1))) for k in range(1, NDEV)]


def _peer_copy(src, land, send_sems, recv_sems, a, k, dst_block, peer):
    return pltpu.make_async_remote_copy(
        src_ref=src, dst_ref=land.at[dst_block], send_sem=send_sems.at[a * 7 + k - 1],
        recv_sem=recv_sems.at[a * 7 + k - 1], device_id=peer, device_id_type=MESH)


def _send_start(arrays, slabs, name):
    na = len(arrays)
    lands = [jax.ShapeDtypeStruct(a.shape if slabs else (NDEV,) + a.shape, a.dtype) for a in arrays]

    def body(*refs):
        ins, zones = refs[:na], refs[na:2 * na]
        send_sems, recv_sems = refs[2 * na], refs[2 * na + 1]
        token = refs[-1]
        x, y, c = _coords()
        me = _dev_index(x, y, c)
        for k, peer in _peers(x, y, c):
            for a in range(na):
                src = ins[a].at[_dev_index(*peer)] if slabs else ins[a]
                _peer_copy(src, zones[a], send_sems, recv_sems, a, k, me, peer).start()
        token[...] = jnp.zeros_like(token)

    outs = pl.pallas_call(
        body, name=name,
        out_shape=(pltpu.SemaphoreType.DMA((7 * na,)), pltpu.SemaphoreType.DMA((7 * na,)),
                   *[pltpu.HBM(a.shape, a.dtype) for a in arrays], *[pltpu.HBM(l.shape, l.dtype) for l in lands],
                   jax.ShapeDtypeStruct((8, LANES), F32)),
        in_specs=[_HBM] * (2 * na), out_specs=(_SEM, _SEM, *([_HBM] * (2 * na)), pl.BlockSpec(memory_space=pltpu.VMEM)),
        input_output_aliases={i: 2 + i for i in range(2 * na)},
        compiler_params=pltpu.CompilerParams(has_side_effects=_EFFECT),
    )(*[pltpu.with_memory_space_constraint(a, pltpu.HBM) for a in arrays],
      *[pltpu.with_memory_space_constraint(lax.empty(l.shape, l.dtype), pltpu.HBM) for l in lands])
    return outs[0], outs[1], list(outs[2:2 + na]), list(outs[2 + na:2 + 2 * na]), outs[-1]


def _send_wait(started, after, slabs, name):
    send_sems, recv_sems, thru, zones, _ = started
    na = len(thru)

    def body(*refs):
        ins, lands = refs[:na], refs[na:2 * na]
        s_sems, r_sems = refs[2 * na], refs[2 * na + 1]
        x, y, c = _coords()
        for k, peer in _peers(x, y, c):
            p = _dev_index(*peer)
            for a in range(na):
                src = ins[a].at[p] if slabs else ins[a]
                cp = _peer_copy(src, lands[a], s_sems, r_sems, a, k, p, peer)
                cp.wait_send()
                cp.wait_recv()

    outs = pl.pallas_call(
        body, name=name, out_shape=tuple(pltpu.HBM(v.shape, v.dtype) for v in thru + zones),
        in_specs=[_HBM] * (2 * na) + [_SEM, _SEM, pl.BlockSpec(memory_space=pl.ANY)], out_specs=tuple([_HBM] * (2 * na)),
        input_output_aliases={i: i for i in range(2 * na)},
        compiler_params=pltpu.CompilerParams(has_side_effects=_EFFECT),
    )(*thru, *zones, send_sems, recv_sems, after)
    me = _dev_index(*_coords())
    filled = []
    for a in range(na):
        own = lax.dynamic_index_in_dim(outs[a], me, 0, keepdims=False) if slabs else outs[a]
        filled.append(lax.dynamic_update_index_in_dim(outs[na + a], own, me, 0))
    return filled


def _pack(vecs):
    parts, spans, off = [], [], 0
    for v in vecs:
        n = v.size
        pad = (-n) % LANES
        parts.append(jnp.pad(v.reshape(-1).astype(F32), (0, pad)))
        spans.append((off, n))
        off += n + pad
    return jnp.concatenate(parts).reshape(-1, LANES), spans


def _pad_lanes(v):
    v = v.reshape(1, -1)
    return jnp.pad(v, ((0, 0), (0, LANES - v.shape[1])))


def _cols_to_slabs(g):
    sh = g.shape
    g = g.reshape(sh[:-1] + (NDEV, sh[-1] // NDEV))
    return jnp.moveaxis(g, -2, 0)


def _rows_to_slabs(g):
    sh = g.shape
    g = g.reshape(sh[:-2] + (NDEV, sh[-2] // NDEV, sh[-1]))
    return jnp.moveaxis(g, -3, 0)


def _slabs_to_cols(a):
    a = jnp.moveaxis(a, 0, -2)
    return a.reshape(a.shape[:-2] + (a.shape[-2] * a.shape[-1],))


def _slabs_to_rows(a):
    a = jnp.moveaxis(a, 0, -3)
    return a.reshape(a.shape[:-3] + (a.shape[-3] * a.shape[-2], a.shape[-1]))


def _ffn_forward(x, norm_w, w_up, fcw, wdown, tag):
    u, h = _mm(x, w_up, rms_fwd=norm_w, name=f"{tag}_up")
    f = _ffn_gate_fwd(u, fcw, f"{tag}_gate")
    return _mm(f, wdown, res=x, name=f"{tag}_down"), (h, u, f)


def _ffn_backward(x, saved, dout, dout_b, norm_w, w_up, fcw, wdown, tag):
    h, u, f = saved
    dwdown = _mm(f, dout_b, ta=True, name=f"{tag}_dwdown")
    df = _mm(dout_b, wdown, tb=True, name=f"{tag}_df")
    du, dfc = _ffn_gate_bwd(u, fcw, df, f"{tag}_gate_bwd")
    dwup = _mm(h, du, ta=True, name=f"{tag}_dwup")
    dx, dxb, dnorm = _mm(du, w_up, tb=True, rms_bwd=(x, norm_w, dout), tm=RMS_BWD_ROWS, name=f"{tag}_dh")
    return dx, dxb, (dwup, jnp.concatenate([dfc[0], dfc[1]], axis=1), dwdown, dnorm)


def kernel(x, a_norm, ssm_w_in, ssm_conv_w, ssm_conv_b, ssm_dt_bias, ssm_a_log, ssm_d, ssm_norm, ssm_w_out, kv_norm, w_kv, b_norm, att_w_q, att_w_o, ffn_norm, ffn_w_up, ffn_conv_w, ffn_w_down, final_norm, loss_target, m_a_norm, m_ssm_w_in, m_ssm_conv_w, m_ssm_conv_b, m_ssm_dt_bias, m_ssm_a_log, m_ssm_d, m_ssm_norm, m_ssm_w_out, m_kv_norm, m_w_kv, m_b_norm, m_att_w_q, m_att_w_o, m_ffn_norm, m_ffn_w_up, m_ffn_conv_w, m_ffn_w_down, m_final_norm, v_a_norm, v_ssm_w_in, v_ssm_conv_w, v_ssm_conv_b, v_ssm_dt_bias, v_ssm_a_log, v_ssm_d, v_ssm_norm, v_ssm_w_out, v_kv_norm, v_w_kv, v_b_norm, v_att_w_q, v_att_w_o, v_ffn_norm, v_ffn_w_up, v_ffn_conv_w, v_ffn_w_down, v_final_norm):
    given = dict(locals())
    xs, tgt = x[0], loss_target[0]
    s, d = xs.shape
    di = ssm_w_out.shape[1] * NDEV
    nh = ssm_dt_bias.shape[1]
    ng = SSM_N_GROUPS
    convd = di + 2 * ng * SSM_D_STATE
    f = ffn_w_down.shape[1] * NDEV
    n_att = len(ATT_PATTERNS)
    qg = ATT_HEADS_PER_GROUP * ATT_HEAD_DIM
    kg = ATT_KV_HEADS_PER_GROUP * ATT_HEAD_DIM
    kvd = n_att * kg
    assert all(w // dil == ATT_BLOCK for w, dil in ATT_PATTERNS)

    small, _ = _pack([a_norm, ssm_conv_w, ssm_conv_b, ssm_norm, ffn_conv_w])
    gat = _all_gather([ssm_w_in[0].astype(BF16), small], "gather_weights")
    first = _send_start([ssm_w_out[0].astype(BF16), ffn_w_up[0].astype(BF16), ffn_w_down[0].astype(BF16)], False,
                        "gather_ffn0_start")
    rest = _send_start([b.astype(BF16) for b in (w_kv, att_w_q[0], att_w_o[0], ffn_w_up[1], ffn_w_down[1])], False,
                       "gather_rest_start")
    w_in = _slabs_to_cols(gat[0])
    in_dim = di + convd + nh
    in_pad = di + convd + LANES
    w_in = jnp.pad(w_in, ((0, 0), (0, in_pad - in_dim)))
    sm = gat[1].reshape(NDEV, -1)
    o0 = 0

    def take(shape):
        nonlocal o0
        n = math.prod(shape)
        out = sm[:, o0:o0 + n].reshape((NDEV,) + shape)
        o0 += n + (-n) % LANES
        return out
    a_norm_f = _slabs_to_cols(take(a_norm.shape)) + (first[-1][0, 0] + rest[-1][0, 0])
    conv_w_f = _slabs_to_cols(take(ssm_conv_w.shape))[0]
    conv_b_f = _slabs_to_cols(take(ssm_conv_b.shape))
    ssm_norm_f = _slabs_to_cols(take(ssm_norm.shape))
    fcw = _slabs_to_cols(take(ffn_conv_w.shape))
    dtb, alog, dsk = _pad_lanes(ssm_dt_bias), _pad_lanes(ssm_a_log), _pad_lanes(ssm_d)
    kvn, fin = kv_norm.reshape(1, d), final_norm.reshape(1, d)

    zx, h0 = _mm(xs, w_in, rms_fwd=a_norm_f, name="in_proj")
    z, dtr = (zx, di, 0), (zx, LANES, (di + convd) // LANES)
    xbc = _conv_silu_fwd(zx, di, conv_w_f, conv_b_f, "ssm_conv")
    dt = _softplus_fwd(dtr, dtb, "ssm_dt")
    y, prevs = _ssd_fwd(xbc, dt, alog, dsk, di, nh, ng, "ssd")
    yn = _gnorm_fwd(y, z, ssm_norm_f, ng, "ssm_gnorm")
    got = _send_wait(first, yn, False, "gather_ffn0_wait")
    w_out = _slabs_to_rows(got[0])
    w_up0, w_down0 = _slabs_to_cols(got[1]), _slabs_to_rows(got[2])
    x1 = _mm(yn, w_out, res=xs, name="ssm_out")
    x2, ffn0 = _ffn_forward(x1, ffn_norm[0:1], w_up0, fcw[0], w_down0, "ffn0")
    got = _send_wait(rest, x2, False, "gather_rest_wait")
    w_kvf = _slabs_to_cols(got[0])
    w_q = _slabs_to_cols(got[1])
    w_o = _slabs_to_rows(got[2])
    w_up1, w_down1 = _slabs_to_cols(got[3]), _slabs_to_rows(got[4])
    kv, hk = _mm(x2, w_kvf, rms_fwd=kvn, name="kv_proj")
    q, h2 = _mm(x2, w_q, rms_fwd=b_norm, name="q_proj")
    tabs = _rot_tables(s)
    k_rot = _rot_heads(kv, tabs, kvd, 1.0, F32, "k_rot")
    att = [_attn_fwd_nat(q, k_rot, kv, tabs, g, dil, f"attn{g}") for g, (_, dil) in enumerate(ATT_PATTERNS)]
    o, ob, lse = _merge_heads([t[0] for t in att], [t[1] for t in att], "attn_merge")
    x3 = _mm(ob, w_o, res=x2, name="attn_out")
    x4, ffn1 = _ffn_forward(x3, ffn_norm[1:2], w_up1, fcw[1], w_down1, "ffn1")
    loss_part, dx4, dx4b, dfin = _final_loss(x4, fin, tgt, "loss_head")

    dx3, dx3b, (dwup1, dfc1, dwdown1, dfn1) = _ffn_backward(
        x3, ffn1, dx4, dx4b, ffn_norm[1:2], w_up1, fcw[1], w_down1, "ffn1")
    dw_o = _mm(ob, dx3b, ta=True, name="attn_dwo")
    do = _mm(dx3b, w_o, tb=True, name="attn_do")
    delta = _delta_heads(do, o, "attn_delta")
    grads = (lax.empty((n_att * qg // LANES, s, LANES), F32), lax.empty((s, kvd), F32), lax.empty((s, kvd), F32))
    for g, (_, dil) in enumerate(ATT_PATTERNS):
        grads = _attn_bwd_nat(q, k_rot, kv, do, lse, delta, tabs, grads, g, dil, f"attn{g}_bwd")
    dq, dk_rot, dv = grads
    dkv = _kv_grad(dk_rot, dv, tabs, "kv_grad")
    dw_q = _mm(h2, dq, ta=True, b_heads=True, name="q_dw")
    dx2, _, db_norm = _mm(dq, w_q, tb=True, a_heads=True, rms_bwd=(x2, b_norm, dx3), tm=RMS_BWD_ROWS, name="q_dh")
    dw_kv = _mm(hk, dkv, ta=True, name="kv_dw")
    dx2, dx2b, dkv_norm = _mm(dkv, w_kvf, tb=True, rms_bwd=(x2, kvn, dx2), tm=RMS_BWD_ROWS, name="kv_dh")
    sent1 = _send_start([_cols_to_slabs(dwup1).astype(BF16), _rows_to_slabs(dwdown1).astype(BF16),
                         _cols_to_slabs(dw_kv).astype(BF16), _cols_to_slabs(dw_q).astype(BF16),
                         _rows_to_slabs(dw_o).astype(BF16)], True, "grads_late_start")
    dx1, dx1b, (dwup0, dfc0, dwdown0, dfn0) = _ffn_backward(
        x1, ffn0, dx2, dx2b, ffn_norm[0:1], w_up0, fcw[0] + sent1[-1][0, 0], w_down0, "ffn0")
    dw_out = _mm(yn, dx1b, ta=True, name="ssm_dwout")
    sent0 = _send_start([_cols_to_slabs(dwup0).astype(BF16), _rows_to_slabs(dwdown0).astype(BF16),
                         _rows_to_slabs(dw_out).astype(BF16)], True, "grads_ffn0_start")
    dyn = _mm(dx1b, w_out, tb=True, name="ssm_dyn")
    dzx = lax.empty((s, in_pad), BF16)
    dy, dzx, dssm_norm = _gnorm_bwd(dyn, y, z, ssm_norm_f + sent0[-1][0, 0], ng, (dzx, 0), "ssm_gnorm_bwd")
    dxbc, ddt, dalog, ddsk = _ssd_bwd(xbc, dt, alog, dsk, prevs, dy, di, nh, ng, "ssd_bwd")
    dzx, ddtb = _softplus_bwd(ddt, dtr, dtb, nh, (dzx, (di + convd) // LANES), "ssm_dt_bwd")
    dzx, dconv_w, dconv_b = _conv_silu_bwd(zx, di, conv_w_f, conv_b_f, dxbc, dzx, "ssm_conv_bwd")
    dw_in = _mm(h0, dzx, ta=True, name="in_dw")
    sent_m = _send_start([_cols_to_slabs(dw_in[:, :in_dim])], True, "grads_mamba_start")
    dx0, _, da_norm = _mm(dzx, w_in, tb=True, rms_bwd=(xs, a_norm_f + sent_m[-1][0, 0], dx1), tm=RMS_BWD_ROWS,
                          name="in_dh")

    small_full = {
        'a_norm': da_norm, 'ssm_conv_w': dconv_w[None], 'ssm_conv_b': dconv_b, 'ssm_dt_bias': ddtb[:, :nh],
        'ssm_a_log': dalog[:, :nh], 'ssm_d': ddsk[:, :nh], 'ssm_norm': dssm_norm, 'kv_norm': dkv_norm.reshape(d),
        'b_norm': db_norm, 'ffn_norm': jnp.concatenate([dfn0, dfn1], axis=0), 'ffn_conv_w': jnp.stack([dfc0, dfc1]),
        'final_norm': dfin.reshape(d),
    }
    small_names = list(small_full)
    packed, spans = _pack([small_full[n] for n in small_names])
    got1 = _send_wait(sent1, dx0, True, "grads_late_wait")
    got0 = _send_wait(sent0, dx0, True, "grads_ffn0_wait")
    recv_big = {'w_kv': [got1[2]], 'att_w_q': [got1[3]], 'att_w_o': [got1[4]], 'ffn_w_up': [got0[0], got1[0]],
                'ffn_w_down': [got0[1], got1[1]]}

    me = _dev_index(*_coords())
    res = {}

    def update_big(n, layers):
        w = given[n]
        c = w.shape[-1]
        outs, row0 = None, 0
        for k, r in enumerate(layers):
            g = r.reshape(NDEV, -1, c)
            outs = _adamw(g, w.reshape(-1, c), given['m_' + n].reshape(-1, c), given['v_' + n].reshape(-1, c),
                          f"adamw_{n}_{k}", row0=row0, prior=outs)
            row0 += g.shape[1]
        res[n] = [o_.reshape(w.shape) for o_ in outs]
    for n, r in recv_big.items():
        update_big(n, r)
    update_big('ssm_w_out', [got0[2]])
    recv = _exchange([], [packed], "exchange_grads", after=[res[n][1] for n in res])
    small_sum = _sum_slabs(recv[-1], "sum_small_grads").reshape(-1)
    gotm = _send_wait(sent_m, recv[-1], True, "grads_mamba_wait")
    update_big('ssm_w_in', [gotm[0]])
    sharded_small = {'a_norm', 'ssm_conv_w', 'ssm_conv_b', 'ssm_norm', 'ffn_conv_w'}
    for n, (off, size) in zip(small_names, spans):
        w = given[n]
        gfull = small_sum[off:off + size].reshape(small_full[n].shape)
        if n in sharded_small:
            c = w.shape[-1]
            gfull = lax.dynamic_slice_in_dim(gfull, me * c, c, axis=gfull.ndim - 1)
        c = w.shape[-1]
        outs = _adamw(gfull.reshape(1, -1, c), w.reshape(-1, c), given['m_' + n].reshape(-1, c),
                      given['v_' + n].reshape(-1, c), f"adamw_{n}")
        res[n] = [o_.reshape(w.shape) for o_ in outs]

    loss = lax.psum(loss_part[0, 0], AXES)
    return (loss, dx0[None], *[res[n][0] for n in WEIGHTS], *[res[n][1] for n in WEIGHTS],
            *[res[n][2] for n in WEIGHTS], *[res[n][3] for n in WEIGHTS])
```

```python
import functools
import math

import jax
import jax.numpy as jnp
from jax import lax
from jax.experimental import pallas as pl
from jax.experimental.pallas import tpu as pltpu

F32, BF16 = jnp.float32, jnp.bfloat16
AXES = ("x", "y", "c")
NDEV = 8
MESH = pl.DeviceIdType.MESH
HIGHEST = lax.Precision.HIGHEST

LANES = 128
SUBLANES = 8
VMEM_LIMIT_BYTES = 48 * 1024 * 1024
VMEM_LIMIT_ATTN_BWD_BYTES = 58 * 1024 * 1024
RMS_BWD_ROWS = 512

RMS_EPS = 1e-6
GATED_NORM_EPS = 1e-5
SSM_HEAD_DIM = 64
SSM_N_GROUPS = 8
SSM_D_STATE = 128
SSM_CONV = 4
SSM_CHUNK = 128
ATT_PATTERNS = ((128, 1), (512, 4), (2048, 16))
ATT_HEAD_DIM = 128
ATT_HEADS_PER_GROUP = 8
ATT_KV_HEADS_PER_GROUP = 2
ATT_BLOCK = 128
ROPE_DIM = ATT_HEAD_DIM // 4
ROPE_THETA = 500000.0
FFN_CONV = 3
ADAM_LR = 0.001
ADAM_B1 = 0.9
ADAM_B2 = 0.999
ADAM_EPS = 1e-08
ADAM_WD = 0.01
ADAM_STEP = 10
NEG = -1e30

WEIGHTS = ['a_norm', 'ssm_w_in', 'ssm_conv_w', 'ssm_conv_b', 'ssm_dt_bias', 'ssm_a_log', 'ssm_d', 'ssm_norm',
           'ssm_w_out', 'kv_norm', 'w_kv', 'b_norm', 'att_w_q', 'att_w_o', 'ffn_norm', 'ffn_w_up', 'ffn_conv_w',
           'ffn_w_down', 'final_norm']


def _params(sem=None, vmem=VMEM_LIMIT_BYTES):
    kw = dict(vmem_limit_bytes=vmem)
    if sem is not None:
        kw["dimension_semantics"] = sem
    return pltpu.CompilerParams(**kw)


def _pick(n, pref):
    if n <= pref:
        return n
    t = (pref // LANES) * LANES
    while t >= LANES:
        if n % t == 0:
            return t
        t -= LANES
    return n


def _dot(a, b, dims=(((1,), (0,)), ((), ())), precision=None):
    return lax.dot_general(a, b, dims, precision=precision, preferred_element_type=F32)


_NT = (((1,), (1,)), ((), ()))
_TN = (((0,), (0,)), ((), ()))


def _mm(a, b, *, ta=False, tb=False, res=None, out_dtype=None, name, tm=1408, tn=1408, tk=2048,
        a_heads=False, b_heads=False, rms_bwd=None, rms_fwd=None):
    assert not (a_heads and ta) and not (b_heads and tb)
    a_parts = a.ndim == 3 and not a_heads
    b_parts = b.ndim == 3 and not b_heads
    assert not (a_parts and ta) and not (b_parts and tb)
    if out_dtype is None:
        out_dtype = BF16 if ta else F32
    if a_heads:
        m, k = a.shape[1], a.shape[0] * LANES
    elif a_parts:
        m, k = a.shape[1], a.shape[0] * a.shape[2]
    else:
        m = a.shape[1] if ta else a.shape[0]
        k = a.shape[0] if ta else a.shape[1]
    if b_heads:
        n, kb = b.shape[0] * LANES, b.shape[1]
    elif b_parts:
        n, kb = b.shape[0] * b.shape[2], b.shape[1]
    else:
        n = b.shape[0] if tb else b.shape[1]
        kb = b.shape[1] if tb else b.shape[0]
    assert k == kb
    tm = _pick(m, tm)
    tn = _pick(b.shape[2], tn) if b_parts else _pick(n, tn)
    tk = _pick(a.shape[2], tk) if a_parts else _pick(k, tk)
    nk = k // tk
    if a_heads:
        a_spec = pl.BlockSpec((tk // LANES, tm, LANES), lambda i, j, l: (l, i, 0))
    elif a_parts:
        per = a.shape[2] // tk
        a_spec = pl.BlockSpec((None, tm, tk), lambda i, j, l: (l // per, i, l % per))
    elif ta:
        a_spec = pl.BlockSpec((tk, tm), lambda i, j, l: (l, i))
    else:
        a_spec = pl.BlockSpec((tm, tk), lambda i, j, l: (i, l))
    if b_heads:
        b_spec = pl.BlockSpec((tn // LANES, tk, LANES), lambda i, j, l: (j, l, 0))
    elif b_parts:
        per_n = b.shape[2] // tn
        b_spec = pl.BlockSpec((None, tk, tn), lambda i, j, l: (j // per_n, l, j % per_n))
    elif tb:
        b_spec = pl.BlockSpec((tn, tk), lambda i, j, l: (j, l))
    else:
        b_spec = pl.BlockSpec((tk, tn), lambda i, j, l: (l, j))
    if rms_fwd is not None:
        assert nk == 1 and a.ndim == 2 and b.ndim == 2 and not (ta or tb or a_heads or b_heads)
        assert res is None and rms_bwd is None

        def nbody(a_ref, b_ref, w_ref, o_ref, h_ref, hs):
            @pl.when(pl.program_id(1) == 0)
            def _():
                xv = a_ref[...]
                rs = lax.rsqrt(jnp.mean(xv * xv, axis=-1, keepdims=True) + RMS_EPS)
                hv = (xv * rs * w_ref[...]).astype(BF16)
                hs[...] = hv
                h_ref[...] = hv

            o_ref[...] = _dot(hs[...], b_ref[...].astype(BF16)).astype(o_ref.dtype)

        rows = pl.BlockSpec((tm, k), lambda i, j: (i, 0))
        return pl.pallas_call(
            nbody, name=name, grid=(m // tm, n // tn),
            in_specs=[rows, pl.BlockSpec((k, tn), lambda i, j: (0, j)), pl.BlockSpec((1, k), lambda i, j: (0, 0))],
            out_specs=[pl.BlockSpec((tm, tn), lambda i, j: (i, j)), rows],
            out_shape=[jax.ShapeDtypeStruct((m, n), out_dtype), jax.ShapeDtypeStruct((m, k), BF16)],
            scratch_shapes=[pltpu.VMEM((tm, k), BF16)],
            compiler_params=_params(("parallel", "arbitrary")))(a, b, rms_fwd)
    o_spec = pl.BlockSpec((tm, tn), lambda i, j, l: (i, j))
    dims = (((0 if ta else 1,), (1 if tb else 0,)), ((), ()))
    has_res = res is not None
    has_rms = rms_bwd is not None
    assert not (has_res and has_rms) and (not has_rms or tn == n)
    n_extra = 3 if has_rms else int(has_res)
    n_out = 3 if has_rms else 1

    def load(ref, heads):
        if not heads:
            return ref[...].astype(BF16)
        return jnp.concatenate([ref[i].astype(BF16) for i in range(ref.shape[0])], axis=1)

    def body(*refs):
        a_ref, b_ref = refs[:2]
        extra = refs[2:2 + n_extra]
        outs = refs[2 + n_extra:2 + n_extra + n_out]
        p = _dot(load(a_ref, a_heads), load(b_ref, b_heads), dims)

        def finish(r):
            if has_res:
                r = r + extra[0][...]
            if not has_rms:
                outs[0][...] = r.astype(outs[0].dtype)
                return
            x_ref, w_ref, dres_ref = extra
            dx_ref, dxb_ref, dw_ref = outs
            xv = x_ref[...]
            rs = lax.rsqrt(jnp.mean(xv * xv, axis=-1, keepdims=True) + RMS_EPS)
            xh = xv * rs
            dxh = r * w_ref[...]
            dx = dres_ref[...] + rs * (dxh - xh * jnp.mean(dxh * xh, axis=-1, keepdims=True))
            dx_ref[...] = dx
            dxb_ref[...] = dx.astype(BF16)

            @pl.when(pl.program_id(0) == 0)
            def _():
                dw_ref[...] = jnp.zeros(dw_ref.shape, F32)

            dw_ref[...] += jnp.sum(r * xh, axis=0, keepdims=True)

        if nk == 1:
            finish(p)
            return
        acc = refs[2 + n_extra + n_out]
        l = pl.program_id(2)

        @pl.when(l == 0)
        def _():
            acc[...] = p

        @pl.when(jnp.logical_and(l > 0, l < nk - 1))
        def _():
            acc[...] += p

        @pl.when(l == nk - 1)
        def _():
            finish(acc[...] + p)

    scratch = [pltpu.VMEM((tm, tn), F32)] if nk > 1 else []
    if has_rms:
        x, w, dres = rms_bwd
        vec = pl.BlockSpec((1, n), lambda i, j, l: (0, 0))
        return pl.pallas_call(
            body, name=name, grid=(m // tm, 1, nk), in_specs=[a_spec, b_spec, o_spec, vec, o_spec],
            out_specs=[o_spec, o_spec, vec], scratch_shapes=scratch,
            out_shape=[jax.ShapeDtypeStruct((m, n), F32), jax.ShapeDtypeStruct((m, n), BF16),
                       jax.ShapeDtypeStruct((1, n), F32)],
            compiler_params=_params(("arbitrary", "arbitrary", "arbitrary")))(a, b, x, w, dres)
    ins = [a, b] + ([res] if has_res else [])
    in_specs = [a_spec, b_spec] + ([o_spec] if has_res else [])
    return pl.pallas_call(
        body, name=name, grid=(m // tm, n // tn, nk), in_specs=in_specs, out_specs=o_spec,
        out_shape=jax.ShapeDtypeStruct((m, n), out_dtype), scratch_shapes=scratch,
        compiler_params=_params(("parallel", "parallel", "arbitrary")))(*ins)


def _rowwise(fn, rows, bcasts, outs, accs=(), *, tile, name):
    s = (rows[0][0] if isinstance(rows[0], tuple) else rows[0]).shape[-2]
    tile = min(tile, s)
    n_val = len(rows) + len(bcasts)
    intos = [(k, o) for k, o in enumerate(outs) if len(o) == 4]
    n_in, n_out = n_val + len(intos), len(outs)

    def row_spec(c):
        if isinstance(c, tuple):
            return pl.BlockSpec((c[0], tile, c[1]), lambda i: (0, i, 0))
        return pl.BlockSpec((tile, c), lambda i: (i, 0))

    def row_shape(c):
        return (c[0], s, c[1]) if isinstance(c, tuple) else (s, c)

    def window(width, cb):
        return pl.BlockSpec((tile, width), lambda i: (i, cb))

    def body(*refs):
        vals = fn(*[r[...] for r in refs[:n_val]])
        o_refs = refs[n_in:n_in + n_out]
        a_refs = refs[n_in + n_out:]
        for r, v in zip(o_refs, vals[:n_out]):
            if isinstance(v, list):
                for i, vi in enumerate(v):
                    r[i] = vi.astype(r.dtype)
            else:
                r[...] = v.astype(r.dtype)

        @pl.when(pl.program_id(0) == 0)
        def _():
            for r in a_refs:
                r[...] = jnp.zeros(r.shape, r.dtype)

        for r, v in zip(a_refs, vals[n_out:]):
            r[...] += v

    in_specs = [window(r[1], r[2]) if isinstance(r, tuple)
                else row_spec(r.shape[1] if r.ndim == 2 else (r.shape[0], r.shape[2])) for r in rows]
    in_specs += [pl.BlockSpec(b.shape, lambda i: (0, 0)) for b in bcasts]
    in_specs += [pl.BlockSpec(memory_space=pl.ANY) for _ in intos]
    out_specs = [window(o[0], o[3]) if len(o) == 4 else row_spec(o[0]) for o in outs]
    out_specs += [pl.BlockSpec(sh, lambda i: (0, 0)) for sh, _ in accs]
    out_shape = [jax.ShapeDtypeStruct(o[2].shape, o[2].dtype) if len(o) == 4
                 else jax.ShapeDtypeStruct(row_shape(o[0]), o[1]) for o in outs]
    out_shape += [jax.ShapeDtypeStruct(sh, dt) for sh, dt in accs]
    args = [r[0] if isinstance(r, tuple) else r for r in rows] + list(bcasts) + [o[2] for _, o in intos]
    return pl.pallas_call(body, name=name, grid=(s // tile,), in_specs=in_specs, out_specs=out_specs,
                          out_shape=out_shape, input_output_aliases={n_val + i: k for i, (k, _) in enumerate(intos)},
                          compiler_params=_params(("arbitrary",)))(*args)


def _final_loss(x, w, tgt, name):
    d = x.shape[1]

    def fn(x, t, w):
        r = lax.rsqrt(jnp.mean(x * x, axis=-1, keepdims=True) + RMS_EPS)
        xh = x * r
        err = xh * w - t
        part = jnp.sum(jnp.mean(err * err, axis=-1, keepdims=True), axis=0, keepdims=True) * 0.5
        dy = err * (1.0 / d)
        dxh = dy * w
        dx = r * (dxh - xh * jnp.mean(dxh * xh, axis=-1, keepdims=True))
        return dx, dx, part, jnp.sum(dy * xh, axis=0, keepdims=True)
    dx, dxb, part, dw = _rowwise(fn, [x, tgt], [w], [(d, F32), (d, BF16)], [((1, 1), F32), ((1, d), F32)],
                                 tile=256, name=name)
    return part, dx, dxb, dw


def _softplus_fwd(dtr, bias, name):
    def fn(r, b):
        v = r + b
        return (jnp.maximum(v, 0.0) + jnp.log(1.0 + jnp.exp(-jnp.abs(v))),)
    return _rowwise(fn, [dtr], [bias], [(LANES, F32)], tile=512, name=name)[0]


def _softplus_bwd(ddt, dtr, bias, n_heads, into, name):
    def fn(g, r, b):
        lane = lax.broadcasted_iota(jnp.int32, g.shape, 1)
        d = jnp.where(lane < n_heads, g * jax.nn.sigmoid(r + b), 0.0)
        return d, jnp.sum(d, axis=0, keepdims=True)
    return _rowwise(fn, [ddt, dtr], [bias], [(LANES, BF16, *into)], [((1, LANES), F32)], tile=512, name=name)


def _gnorm_fwd(y, z, w, n_groups, name):
    di = y.shape[1]
    gs = di // n_groups

    def fn(y, z, w):
        y2 = y * (z * jax.nn.sigmoid(z))
        out = []
        for g in range(n_groups):
            sl = y2[:, g * gs:(g + 1) * gs]
            r = lax.rsqrt(jnp.mean(sl * sl, axis=-1, keepdims=True) + GATED_NORM_EPS)
            out.append(sl * r)
        return (jnp.concatenate(out, axis=1) * w,)
    return _rowwise(fn, [y, z], [w], [(di, BF16)], tile=256, name=name)[0]


def _gnorm_bwd(dyn, y, z, w, n_groups, into, name):
    di = y.shape[1]
    gs = di // n_groups

    def fn(dyn, y, z, w):
        sig = jax.nn.sigmoid(z)
        sz = z * sig
        y2 = y * sz
        d2n = dyn * w
        dy2, yhat = [], []
        for g in range(n_groups):
            sl = y2[:, g * gs:(g + 1) * gs]
            dg = d2n[:, g * gs:(g + 1) * gs]
            r = lax.rsqrt(jnp.mean(sl * sl, axis=-1, keepdims=True) + GATED_NORM_EPS)
            yh = sl * r
            dy2.append(r * (dg - yh * jnp.mean(dg * yh, axis=-1, keepdims=True)))
            yhat.append(yh)
        dy2 = jnp.concatenate(dy2, axis=1)
        yhat = jnp.concatenate(yhat, axis=1)
        dz = dy2 * y * (sig * (1.0 + z * (1.0 - sig)))
        return dy2 * sz, dz, jnp.sum(dyn * yhat, axis=0, keepdims=True)
    return _rowwise(fn, [dyn, y, z], [w], [(di, F32), (di, BF16, *into)], [((1, di), F32)], tile=256, name=name)


def _lane_place(cols):
    rows = cols[0].shape[0]
    lane = lax.broadcasted_iota(jnp.int32, (rows, LANES), 1)
    out = jnp.zeros((rows, LANES), F32)
    for j, c in enumerate(cols):
        out = jnp.where(lane == j, c, out)
    return out


def _merge_heads(os_, lses, name):
    n = len(os_)
    n_kv, rep, hd = ATT_KV_HEADS_PER_GROUP, ATT_HEADS_PER_GROUP // ATT_KV_HEADS_PER_GROUP, ATT_HEAD_DIM

    def fn(*v):
        o, l = v[:n], v[n:]
        out, lse = [], []
        for h in range(n_kv):
            cols = []
            for j in range(rep):
                hh = h * rep + j
                lg = [li[h][:, j:j + 1] for li in l]
                m = functools.reduce(jnp.maximum, lg)
                e = [jnp.exp(x - m) for x in lg]
                tot = functools.reduce(jnp.add, e)
                acc = functools.reduce(jnp.add, [ei * oi[hh] for ei, oi in zip(e, o)])
                out.append(acc / tot)
                cols.append(m + jnp.log(tot))
            lse.append(_lane_place(cols))
        merged = jnp.concatenate(out, axis=1)
        return merged, merged, lse
    c = os_[0].shape[0] * hd
    return _rowwise(fn, list(os_) + list(lses), [], [(c, F32), (c, BF16), ((n_kv, LANES), F32)], tile=256, name=name)


def _delta_heads(do, o, name):
    n_kv, rep, hd = ATT_KV_HEADS_PER_GROUP, ATT_HEADS_PER_GROUP // ATT_KV_HEADS_PER_GROUP, ATT_HEAD_DIM

    def fn(do, o):
        p = do * o
        return ([_lane_place([jnp.sum(p[:, (h * rep + j) * hd:(h * rep + j + 1) * hd], axis=-1, keepdims=True)
                              for j in range(rep)]) for h in range(n_kv)],)
    return _rowwise(fn, [do, o], [], [((n_kv, LANES), F32)], tile=256, name=name)[0]


def _sum_slabs(recv, name):
    def body(r_ref, o_ref):
        acc = r_ref[0]
        for k in range(1, NDEV):
            acc = acc + r_ref[k]
        o_ref[...] = acc
    return pl.pallas_call(body, name=name, out_shape=jax.ShapeDtypeStruct(recv.shape[1:], F32),
                          compiler_params=_params())(recv)


def _shift_down(x, k):
    if k == 0:
        return x
    r = pltpu.roll(x, k, 0)
    row = lax.broadcasted_iota(jnp.int32, (SUBLANES, x.shape[1]), 0)
    return jnp.concatenate([jnp.where(row >= k, r[:SUBLANES], 0.0), r[SUBLANES:]], axis=0)


def _shift_up(x, k):
    if k == 0:
        return x
    s = x.shape[0]
    r = pltpu.roll(x, s - k, 0)
    row = lax.broadcasted_iota(jnp.int32, (SUBLANES, x.shape[1]), 0)
    return jnp.concatenate([r[:s - SUBLANES], jnp.where(row < SUBLANES - k, r[s - SUBLANES:], 0.0)], axis=0)


def _conv(x, w):
    kw = w.shape[0]
    return functools.reduce(jnp.add, [w[k:k + 1, :] * _shift_down(x, kw - 1 - k) for k in range(kw)])


def _conv_t(dy, w):
    kw = w.shape[0]
    return functools.reduce(jnp.add, [w[k:k + 1, :] * _shift_up(dy, kw - 1 - k) for k in range(kw)])


def _conv_dw(x, dy, dw_ref):
    kw = dw_ref.shape[0]
    for k in range(kw):
        dw_ref[k:k + 1, :] = jnp.sum(dy * _shift_down(x, kw - 1 - k), axis=0, keepdims=True)


def _dsilu(pre):
    sig = jax.nn.sigmoid(pre)
    return sig * (1.0 + pre * (1.0 - sig))


def _col_specs(s, c, kw, tc):
    return (pl.BlockSpec((s, tc), lambda j: (0, j)), pl.BlockSpec((kw, tc), lambda j: (0, j)),
            pl.BlockSpec((1, tc), lambda j: (0, j)))


def _conv_silu_fwd(x, col0, w, b, name):
    s, c = x.shape[0], w.shape[1]
    tc = LANES
    xs, ws, bs = _col_specs(s, c, w.shape[0], tc)
    xwin = pl.BlockSpec((s, tc), lambda j: (0, j + col0 // tc))

    def body(x_ref, w_ref, b_ref, o_ref):
        pre = _conv(x_ref[...], w_ref[...]) + b_ref[...]
        o_ref[...] = pre * jax.nn.sigmoid(pre)
    return pl.pallas_call(body, name=name, grid=(c // tc,), in_specs=[xwin, ws, bs], out_specs=xs,
                          out_shape=jax.ShapeDtypeStruct((s, c), F32), compiler_params=_params(("parallel",)))(x, w, b)


def _conv_silu_bwd(x, col0, w, b, dy, into, name):
    s, c = x.shape[0], w.shape[1]
    tc = LANES
    xs, ws, bs = _col_specs(s, c, w.shape[0], tc)
    xwin = pl.BlockSpec((s, tc), lambda j: (0, j + col0 // tc))

    def body(x_ref, w_ref, b_ref, dy_ref, _, dx_ref, dw_ref, db_ref):
        xv, wv = x_ref[...], w_ref[...]
        pre = _conv(xv, wv) + b_ref[...]
        dpre = dy_ref[...] * _dsilu(pre)
        dx_ref[...] = _conv_t(dpre, wv).astype(dx_ref.dtype)
        _conv_dw(xv, dpre, dw_ref)
        db_ref[...] = jnp.sum(dpre, axis=0, keepdims=True)
    return pl.pallas_call(
        body, name=name, grid=(c // tc,), in_specs=[xwin, ws, bs, xs, pl.BlockSpec(memory_space=pl.ANY)],
        out_specs=[xwin, ws, bs], input_output_aliases={4: 0},
        out_shape=[jax.ShapeDtypeStruct(into.shape, into.dtype), jax.ShapeDtypeStruct(w.shape, F32),
                   jax.ShapeDtypeStruct((1, c), F32)],
        compiler_params=_params(("parallel",)))(x, w, b, dy, into)


def _gate_specs(s, f, kw):
    nt = f // LANES
    return (pl.BlockSpec((s, LANES), lambda j: (0, j)), pl.BlockSpec((s, LANES), lambda j: (0, j + nt)),
            pl.BlockSpec((kw, LANES), lambda j: (0, j)), pl.BlockSpec((kw, LANES), lambda j: (0, j + nt)))


def _ffn_gate_fwd(u, w, name):
    s, f = u.shape[0], u.shape[1] // 2
    gs, vs, wgs, wvs = _gate_specs(s, f, w.shape[0])

    def body(g_ref, v_ref, wg_ref, wv_ref, o_ref):
        g = _conv(g_ref[...], wg_ref[...])
        v = _conv(v_ref[...], wv_ref[...])
        o_ref[...] = (g * jax.nn.sigmoid(g) * v).astype(o_ref.dtype)
    return pl.pallas_call(body, name=name, grid=(f // LANES,), in_specs=[gs, vs, wgs, wvs], out_specs=gs,
                          out_shape=jax.ShapeDtypeStruct((s, f), BF16),
                          compiler_params=_params(("parallel",)))(u, u, w, w)


def _ffn_gate_bwd(u, w, df, name):
    s, f = u.shape[0], u.shape[1] // 2
    kw = w.shape[0]
    gs, vs, wgs, wvs = _gate_specs(s, f, kw)

    def body(g_ref, v_ref, wg_ref, wv_ref, df_ref, du_ref, dw_ref):
        gp, vp, wgv, wvv = g_ref[...], v_ref[...], wg_ref[...], wv_ref[...]
        g = _conv(gp, wgv)
        v = _conv(vp, wvv)
        dfv = df_ref[...]
        dg = dfv * v * _dsilu(g)
        dv = dfv * (g * jax.nn.sigmoid(g))
        du_ref[0] = _conv_t(dg, wgv).astype(du_ref.dtype)
        du_ref[1] = _conv_t(dv, wvv).astype(du_ref.dtype)
        _conv_dw(gp, dg, dw_ref.at[0])
        _conv_dw(vp, dv, dw_ref.at[1])
    return pl.pallas_call(
        body, name=name, grid=(f // LANES,), in_specs=[gs, vs, wgs, wvs, gs],
        out_specs=[pl.BlockSpec((2, s, LANES), lambda j: (0, 0, j)), pl.BlockSpec((2, kw, LANES), lambda j: (0, 0, j))],
        out_shape=[jax.ShapeDtypeStruct((2, s, f), BF16), jax.ShapeDtypeStruct((2, kw, f), F32)],
        compiler_params=_params(("parallel",)))(u, u, w, w, df)


def _ssd_common(dt, alog, n_heads):
    ln = dt.shape[0]
    lane = lax.broadcasted_iota(jnp.int32, (1, LANES), 1)
    a = jnp.where(lane < n_heads, -jnp.exp(alog), 0.0)
    row = lax.broadcasted_iota(jnp.int32, (ln, ln), 0)
    col = lax.broadcasted_iota(jnp.int32, (ln, ln), 1)
    tril = col <= row
    acs = _dot(tril.astype(F32), dt * a, precision=HIGHEST)
    return a, acs, acs.T, tril


def _split(x, n):
    out = []
    for _ in range(n):
        piece = x.astype(BF16)
        out.append(piece)
        x = x - piece.astype(F32)
    return out


def _spread(x, onehot, n=2):
    return functools.reduce(jnp.add, [_dot(piece, onehot) for piece in _split(x, n)])


def _head_maps(di, p):
    e = (jnp.arange(di, dtype=jnp.int32)[None, :] // p == jnp.arange(LANES, dtype=jnp.int32)[:, None]).astype(BF16)
    return e, e.T


def _ssd_wide(dt, acs, acs_t, dskip, e_ref, et_ref):
    ln = dt.shape[0]
    last = acs[ln - 1:ln, :]
    stack = jnp.concatenate([dt, jnp.exp(acs), jnp.exp(last - acs), jnp.broadcast_to(dskip, (8, LANES))], axis=0)
    wide = _spread(stack, e_ref[...], n=1)
    tb = jnp.exp(jnp.broadcast_to(acs_t[:, ln - 1:ln], (LANES, LANES)))
    texp = functools.reduce(jnp.add, [_dot(et_ref[...], piece) for piece in _split(tb, 3)])
    return wide[:ln], wide[ln:2 * ln], wide[2 * ln:3 * ln], wide[3 * ln:3 * ln + 1], texp


def _ssd_fwd(xbc, dt, alog, dskip, di, n_heads, n_groups, name):
    s, convd = xbc.shape
    ln, p, ns = SSM_CHUNK, SSM_HEAD_DIM, SSM_D_STATE
    nc, hg = s // ln, n_heads // n_groups
    gw = hg * p
    e64, e64t = _head_maps(di, p)

    def body(x_ref, dt_ref, alog_ref, d_ref, e_ref, et_ref, y_ref, prev_ref, st):
        @pl.when(pl.program_id(0) == 0)
        def _():
            st[...] = jnp.zeros(st.shape, F32)

        dt = dt_ref[...]
        _, acs, acs_t, tril = _ssd_common(dt, alog_ref[...], n_heads)
        dte, ee, dse, dske, texp = _ssd_wide(dt, acs, acs_t, d_ref[...], e_ref, et_ref)
        x = x_ref[:, :di]
        xdt = x * dte
        xdtb = xdt.astype(BF16)
        xdsb = (xdt * dse).astype(BF16)
        for g in range(n_groups):
            rows = slice(g * gw, (g + 1) * gw)
            bg = x_ref[:, di + g * ns:di + (g + 1) * ns].astype(BF16)
            cg = x_ref[:, di + (n_groups + g) * ns:di + (n_groups + g + 1) * ns].astype(BF16)
            gm = _dot(cg, bg, _NT)
            prev = st[rows, :]
            prev_ref[0, rows, :] = prev
            yo = _dot(cg, prev.astype(BF16), _NT)
            for j in range(hg):
                h = g * hg + j
                seg = acs[:, h:h + 1] - acs_t[h:h + 1, :]
                m = jnp.where(tril, gm * jnp.exp(jnp.where(tril, seg, 0.0)), 0.0)
                y_ref[:, h * p:(h + 1) * p] = _dot(m.astype(BF16), xdtb[:, h * p:(h + 1) * p])
            y_ref[:, rows] = y_ref[:, rows] + yo * ee[:, rows] + x[:, rows] * dske[:, rows]
            st[rows, :] = prev * texp[rows, :] + _dot(xdsb[:, rows], bg, _TN)

    vec = pl.BlockSpec((1, LANES), lambda c: (0, 0))
    return pl.pallas_call(
        body, name=name, grid=(nc,),
        in_specs=[pl.BlockSpec((ln, convd), lambda c: (c, 0)), pl.BlockSpec((ln, LANES), lambda c: (c, 0)), vec, vec,
                  pl.BlockSpec(e64.shape, lambda c: (0, 0)), pl.BlockSpec(e64t.shape, lambda c: (0, 0))],
        out_specs=[pl.BlockSpec((ln, di), lambda c: (c, 0)), pl.BlockSpec((1, di, ns), lambda c: (c, 0, 0))],
        out_shape=[jax.ShapeDtypeStruct((s, di), F32), jax.ShapeDtypeStruct((nc, di, ns), F32)],
        scratch_shapes=[pltpu.VMEM((di, ns), F32)],
        compiler_params=_params(("arbitrary",)))(xbc, dt, alog, dskip, e64, e64t)


def _ssd_bwd(xbc, dt, alog, dskip, prev_all, dy, di, n_heads, n_groups, name):
    s, convd = xbc.shape
    ln, p, ns = SSM_CHUNK, SSM_HEAD_DIM, SSM_D_STATE
    nc, hg = s // ln, n_heads // n_groups
    gw = hg * p
    e64, e64t = _head_maps(di, p)

    def body(x_ref, dt_ref, alog_ref, d_ref, e_ref, et_ref, prev_ref, dy_ref,
             dx_ref, ddt_ref, da_ref, dd_ref, dh, yo_ref, w_ref):
        step = pl.program_id(0)

        @pl.when(step == 0)
        def _():
            dh[...] = jnp.zeros(dh.shape, F32)
            da_ref[...] = jnp.zeros(da_ref.shape, F32)
            dd_ref[...] = jnp.zeros(dd_ref.shape, F32)

        dt = dt_ref[...]
        a, acs, acs_t, tril = _ssd_common(dt, alog_ref[...], n_heads)
        dte, ee, dse, dske, texp = _ssd_wide(dt, acs, acs_t, d_ref[...], e_ref, et_ref)
        row = lax.broadcasted_iota(jnp.int32, (ln, ln), 0)
        col = lax.broadcasted_iota(jnp.int32, (ln, ln), 1)
        triu = col >= row
        x = x_ref[:, :di]
        dy = dy_ref[...]
        xdt = x * dte
        xdtb = xdt.astype(BF16)
        xdsb = (xdt * dse).astype(BF16)
        dyb = dy.astype(BF16)
        dyob = (dy * ee).astype(BF16)
        dhn = dh[...]
        dhb = dhn.astype(BF16)
        per_head = functools.reduce(jnp.add, [_dot(e_ref[...], piece) for piece in _split(dhn * prev_ref[0], 2)])
        ones8 = jnp.ones((8, LANES), BF16)
        dtt = functools.reduce(jnp.add, [_dot(ones8, piece, _NT) for piece in _split(per_head, 2)])[0:1]
        dacs_c = jnp.zeros((ln, LANES), F32)
        dacs_r = jnp.zeros((LANES, ln), F32)
        for g in range(n_groups):
            rows = slice(g * gw, (g + 1) * gw)
            bg = x_ref[:, di + g * ns:di + (g + 1) * ns].astype(BF16)
            cg = x_ref[:, di + (n_groups + g) * ns:di + (n_groups + g + 1) * ns].astype(BF16)
            gmt = _dot(bg, cg, _NT)
            prevb = prev_ref[0, rows, :].astype(BF16)
            dcg = _dot(dyob[:, rows], prevb)
            dh[rows, :] = texp[rows, :] * dhn[rows, :] + _dot(dyob[:, rows], cg, _TN)
            w = _dot(bg, dhb[rows, :], _NT)
            dbg = _dot(xdsb[:, rows], dhb[rows, :])
            yo_ref[:, rows] = _dot(cg, prevb, _NT)
            w_ref[:, rows] = w
            dgmt = jnp.zeros((ln, ln), F32)
            q_hi = []
            for j in range(hg):
                h = g * hg + j
                segt = acs_t[h:h + 1, :] - acs[:, h:h + 1]
                dect = jnp.where(triu, jnp.exp(jnp.where(triu, segt, 0.0)), 0.0)
                dyh, xh = dyb[:, h * p:(h + 1) * p], xdtb[:, h * p:(h + 1) * p]
                mt = gmt * dect
                dmt = _dot(xh, dyh, _NT)
                dx_ref[:, h * p:(h + 1) * p] = _dot(mt.astype(BF16), dyh)
                dgmt = dgmt + dmt * dect
                q_hi.append((dmt * mt).astype(BF16))
            sel_c = (lax.broadcasted_iota(jnp.int32, (hg * ln, LANES), 1)
                     == g * hg + lax.broadcasted_iota(jnp.int32, (hg * ln, LANES), 0) // ln).astype(BF16)
            sel_r = (lax.broadcasted_iota(jnp.int32, (LANES, hg * ln), 0)
                     == g * hg + lax.broadcasted_iota(jnp.int32, (LANES, hg * ln), 1) // ln).astype(BF16)
            dacs_c = dacs_c - _dot(jnp.concatenate(q_hi, axis=1), sel_c)
            dacs_r = dacs_r + _dot(sel_r, jnp.concatenate(q_hi, axis=0))
            dgb = dgmt.astype(BF16)
            dx_ref[:, di + g * ns:di + (g + 1) * ns] = dbg + _dot(dgb, cg)
            dx_ref[:, di + (n_groups + g) * ns:di + (n_groups + g + 1) * ns] = dcg + _dot(dgb, bg, _TN)

        wds = w_ref[...] * dse
        dxdt = dx_ref[:, :di] + wds
        red = _spread(jnp.concatenate([dxdt * x, dy * yo_ref[...] * ee, xdt * wds, dy * x], axis=0), et_ref[...], n=1)
        ddt_x, r_off, r_state, ddr = red[:ln], red[ln:2 * ln], red[2 * ln:3 * ln], red[3 * ln:]
        dx_ref[:, :di] = dxdt * dte + dy * dske
        rowi = lax.broadcasted_iota(jnp.int32, (ln, LANES), 0)
        dlast = jnp.sum(r_state, axis=0, keepdims=True) + dtt * jnp.exp(acs[ln - 1:ln, :])
        dacs = r_off - r_state + dacs_c + dacs_r.T + jnp.where(rowi == ln - 1, dlast, 0.0)
        dadt = _dot(triu.astype(F32), dacs, precision=HIGHEST)
        ddt_ref[...] = dadt * a + ddt_x
        da_ref[...] += jnp.sum(dadt * dt, axis=0, keepdims=True)
        dd_ref[...] += jnp.sum(ddr, axis=0, keepdims=True)

        @pl.when(step == nc - 1)
        def _():
            da_ref[...] = da_ref[...] * a

    vec = pl.BlockSpec((1, LANES), lambda c: (0, 0))
    rev = lambda c: (nc - 1 - c, 0)
    return pl.pallas_call(
        body, name=name, grid=(nc,),
        in_specs=[pl.BlockSpec((ln, convd), rev), pl.BlockSpec((ln, LANES), rev), vec, vec,
                  pl.BlockSpec(e64.shape, lambda c: (0, 0)), pl.BlockSpec(e64t.shape, lambda c: (0, 0)),
                  pl.BlockSpec((1, di, ns), lambda c: (nc - 1 - c, 0, 0)), pl.BlockSpec((ln, di), rev)],
        out_specs=[pl.BlockSpec((ln, convd), rev), pl.BlockSpec((ln, LANES), rev), vec, vec],
        out_shape=[jax.ShapeDtypeStruct((s, convd), F32), jax.ShapeDtypeStruct((s, LANES), F32),
                   jax.ShapeDtypeStruct((1, LANES), F32), jax.ShapeDtypeStruct((1, LANES), F32)],
        scratch_shapes=[pltpu.VMEM((di, ns), F32), pltpu.VMEM((ln, di), F32), pltpu.VMEM((ln, di), F32)],
        compiler_params=_params(("arbitrary",)))(xbc, dt, alog, dskip, e64, e64t, prev_all, dy)


def _rot_tables(s):
    half = ROPE_DIM // 2
    inv_freq = jnp.power(jnp.float32(ROPE_THETA), -jnp.arange(0, ROPE_DIM, 2, dtype=F32) / ROPE_DIM)
    pos = jnp.arange(s, dtype=jnp.int32)
    ang = pos.astype(F32)[:, None] * inv_freq[None, :]
    cos, sin = jnp.cos(ang), jnp.sin(ang)
    zero = jnp.zeros((s, ATT_HEAD_DIM - ROPE_DIM), F32)
    cf = jnp.concatenate([cos, cos, jnp.ones_like(zero)], axis=1)
    s1 = jnp.concatenate([-sin, jnp.zeros_like(sin), zero], axis=1)
    s2 = jnp.concatenate([jnp.zeros_like(sin), sin, zero], axis=1)
    assert half * 2 == ROPE_DIM
    return cf, s1, s2


def _rot(x, tabs, sign):
    cf, s1, s2 = tabs
    half = ROPE_DIM // 2
    left = pltpu.roll(x, ATT_HEAD_DIM - half, 1)
    right = pltpu.roll(x, half, 1)
    return x * cf + sign * (left * s1 + right * s2)


def _att_masks(n, n_blk, rep):
    b = ATT_BLOCK
    row = lax.broadcasted_iota(jnp.int32, (rep * b, b), 0) & (b - 1)
    col = lax.broadcasted_iota(jnp.int32, (rep * b, b), 1)
    off = jnp.where(n % n_blk != 0, 0, 2 * b)
    return col <= row, col >= row + off


def _rot_heads(x, tabs, width, sign, out_dtype, name):
    s = x.shape[0]
    hd = ATT_HEAD_DIM
    tile = min(512, s)

    def body(x_ref, cf, s1, s2, o_ref):
        t = (cf[...], s1[...], s2[...])
        for j in range(width // hd):
            o_ref[:, j * hd:(j + 1) * hd] = _rot(x_ref[:, j * hd:(j + 1) * hd], t, sign).astype(o_ref.dtype)

    tab = pl.BlockSpec((tile, hd), lambda i: (i, 0))
    return pl.pallas_call(
        body, name=name, grid=(s // tile,), in_specs=[pl.BlockSpec((tile, width), lambda i: (i, 0)), tab, tab, tab],
        out_specs=pl.BlockSpec((tile, width), lambda i: (i, 0)), out_shape=jax.ShapeDtypeStruct((s, width), out_dtype),
        compiler_params=_params(("parallel",)))(x, *tabs)


def _kv_grad(dk_rot, dv, tabs, name):
    s, width = dk_rot.shape
    hd = ATT_HEAD_DIM
    tile = min(512, s)

    def body(k_ref, v_ref, cf, s1, s2, o_ref):
        t = (cf[...], s1[...], s2[...])
        for j in range(width // hd):
            o_ref[:, j * hd:(j + 1) * hd] = _rot(k_ref[:, j * hd:(j + 1) * hd], t, -1.0).astype(o_ref.dtype)
        o_ref[:, width:] = v_ref[...].astype(o_ref.dtype)

    tab = pl.BlockSpec((tile, hd), lambda i: (i, 0))
    half = pl.BlockSpec((tile, width), lambda i: (i, 0))
    return pl.pallas_call(
        body, name=name, grid=(s // tile,), in_specs=[half, half, tab, tab, tab],
        out_specs=pl.BlockSpec((tile, 2 * width), lambda i: (i, 0)),
        out_shape=jax.ShapeDtypeStruct((s, 2 * width), BF16), compiler_params=_params(("parallel",)))(dk_rot, dv, *tabs)


def _rows_of(r, dil):
    return pl.ds(r, ATT_BLOCK, stride=dil) if dil > 1 else slice(None)


def _nat_specs(g, dil, n_kv_all, cur, prv):
    b, hd = ATT_BLOCK * dil, ATT_HEAD_DIM
    n_kv = ATT_KV_HEADS_PER_GROUP
    rep = ATT_HEADS_PER_GROUP // n_kv
    q_all = [pl.BlockSpec((b, hd), lambda h, n, j=j: (cur(n), (g * n_kv + h) * rep + j)) for j in range(rep)]
    q_own = [pl.BlockSpec((b, hd), lambda h, n, j=j: (cur(n), h * rep + j)) for j in range(rep)]
    hm_all = pl.BlockSpec((rep, b, hd), lambda h, n: (g * n_kv + h, cur(n), 0))
    hm_own = pl.BlockSpec((rep, b, hd), lambda h, n: (h, cur(n), 0))
    kc = pl.BlockSpec((b, hd), lambda h, n: (cur(n), g * n_kv + h))
    kp = pl.BlockSpec((b, hd), lambda h, n: (prv(n), g * n_kv + h))
    vc = pl.BlockSpec((b, hd), lambda h, n: (cur(n), n_kv_all + g * n_kv + h))
    vp = pl.BlockSpec((b, hd), lambda h, n: (prv(n), n_kv_all + g * n_kv + h))
    tab = pl.BlockSpec((b, hd), lambda h, n: (cur(n), 0))
    stat = pl.BlockSpec((None, b, LANES), lambda h, n: (h, cur(n), 0))
    return q_all, q_own, hm_all, hm_own, kc, kp, vc, vp, tab, stat


def _head_cols(stat, rep):
    return jnp.concatenate([jnp.broadcast_to(stat[:, j:j + 1], stat.shape) for j in range(rep)], axis=0)


def _attn_fwd_nat(q_all, k_rot, kv, tabs, g, dil, name):
    s = q_all.shape[0]
    b, hd = ATT_BLOCK, ATT_HEAD_DIM
    nbn = s // (b * dil)
    n_kv = ATT_KV_HEADS_PER_GROUP
    rep = ATT_HEADS_PER_GROUP // n_kv
    n_kv_all = k_rot.shape[1] // hd
    scale = hd ** -0.5

    def body(*refs):
        q_refs = refs[:rep]
        kc_ref, kp_ref, vc_ref, vp_ref, cf, s1, s2, o_ref, lse_ref = refs[rep:]
        mc, mp = _att_masks(jnp.where(pl.program_id(1) > 0, 1, 0), 2, rep)
        for r in range(dil):
            sl = _rows_of(r, dil)
            tc = (cf[sl, :], s1[sl, :], s2[sl, :])
            q4 = (jnp.concatenate([_rot(q_ref[sl, :], tc, 1.0) for q_ref in q_refs], axis=0) * scale).astype(BF16)
            kc, kp = kc_ref[sl, :].astype(BF16), kp_ref[sl, :].astype(BF16)
            sc = jnp.where(mc, _dot(q4, kc, _NT), NEG)
            sp = jnp.where(mp, _dot(q4, kp, _NT), NEG)
            m = jnp.maximum(jnp.max(sc, axis=1, keepdims=True), jnp.max(sp, axis=1, keepdims=True))
            pc, pp = jnp.exp(sc - m), jnp.exp(sp - m)
            l = jnp.sum(pc, axis=1, keepdims=True) + jnp.sum(pp, axis=1, keepdims=True)
            o = (_dot(pc.astype(BF16), vc_ref[sl, :].astype(BF16))
                 + _dot(pp.astype(BF16), vp_ref[sl, :].astype(BF16))) / l
            lse = m + jnp.log(l)
            for j in range(rep):
                o_ref[j, sl, :] = o[j * b:(j + 1) * b]
            lse_ref[sl, :] = _lane_place([lse[j * b:(j + 1) * b] for j in range(rep)])

    cur = lambda n: n
    prv = lambda n: jnp.maximum(n - 1, 0)
    q_specs, _, _, hm_own, kc, kp, vc, vp, tab, stat = _nat_specs(g, dil, n_kv_all, cur, prv)
    return pl.pallas_call(
        body, name=name, grid=(n_kv, nbn), in_specs=[*q_specs, kc, kp, vc, vp, tab, tab, tab], out_specs=[hm_own, stat],
        out_shape=[jax.ShapeDtypeStruct((ATT_HEADS_PER_GROUP, s, hd), F32), jax.ShapeDtypeStruct((n_kv, s, LANES), F32)],
        compiler_params=_params(("parallel", "arbitrary")))(*([q_all] * rep), k_rot, k_rot, kv, kv, *tabs)


def _attn_bwd_nat(q_all, k_rot, kv, do, lse, delta, tabs, grads, g, dil, name):
    s = q_all.shape[0]
    b, hd = ATT_BLOCK, ATT_HEAD_DIM
    nbn = s // (b * dil)
    n_kv = ATT_KV_HEADS_PER_GROUP
    rep = ATT_HEADS_PER_GROUP // n_kv
    n_kv_all = k_rot.shape[1] // hd
    scale = hd ** -0.5

    def body(*refs):
        q_refs, do_refs = refs[:rep], refs[rep:2 * rep]
        (lse_ref, dl_ref, kc_ref, kp_ref, vc_ref, vp_ref, cf, s1, s2, _, _, _,
         dq_ref, dk_ref, dv_ref, ck, cv) = refs[2 * rep:]
        n = pl.program_id(1)

        @pl.when(n == 0)
        def _():
            ck[...] = jnp.zeros(ck.shape, F32)
            cv[...] = jnp.zeros(cv.shape, F32)

        @pl.when(n < nbn)
        def _():
            mc, mp = _att_masks(jnp.where(n > 0, 1, 0), 2, rep)
            for r in range(dil):
                sl = _rows_of(r, dil)
                own = slice(r * b, (r + 1) * b)
                tc = (cf[sl, :], s1[sl, :], s2[sl, :])
                q4 = (jnp.concatenate([_rot(q_ref[sl, :], tc, 1.0) for q_ref in q_refs], axis=0) * scale).astype(BF16)
                do4 = jnp.concatenate([do_ref[sl, :] for do_ref in do_refs], axis=0).astype(BF16)
                lse4 = _head_cols(lse_ref[sl, :], rep)
                dl4 = _head_cols(dl_ref[sl, :], rep)
                kc, kp = kc_ref[sl, :].astype(BF16), kp_ref[sl, :].astype(BF16)
                vc, vp = vc_ref[sl, :].astype(BF16), vp_ref[sl, :].astype(BF16)
                pc = jnp.where(mc, jnp.exp(_dot(q4, kc, _NT) - lse4), 0.0)
                pp = jnp.where(mp, jnp.exp(_dot(q4, kp, _NT) - lse4), 0.0)
                dsc = (pc * (_dot(do4, vc, _NT) - dl4)).astype(BF16)
                dsp = (pp * (_dot(do4, vp, _NT) - dl4)).astype(BF16)
                dq4 = (_dot(dsc, kc) + _dot(dsp, kp)) * scale
                for j in range(rep):
                    dq_ref[j, sl, :] = _rot(dq4[j * b:(j + 1) * b], tc, -1.0)
                dk_ref[sl, :] = ck[own, :] + _dot(dsp, q4, _TN)
                dv_ref[sl, :] = cv[own, :] + _dot(pp.astype(BF16), do4, _TN)
                ck[own, :] = _dot(dsc, q4, _TN)
                cv[own, :] = _dot(pc.astype(BF16), do4, _TN)

        @pl.when(n == nbn)
        def _():
            for r in range(dil):
                sl = _rows_of(r, dil)
                dk_ref[sl, :] = ck[r * b:(r + 1) * b, :]
                dv_ref[sl, :] = cv[r * b:(r + 1) * b, :]

    cur = lambda n: jnp.minimum(n, nbn - 1)
    prv = lambda n: jnp.maximum(n - 1, 0)
    q_specs, do_specs, hm_all, _, kc, kp, vc, vp, tab, stat = _nat_specs(g, dil, n_kv_all, cur, prv)
    anyspace = pl.BlockSpec(memory_space=pl.ANY)
    n_in = 2 * rep + 9
    return pl.pallas_call(
        body, name=name, grid=(n_kv, nbn + 1),
        in_specs=[*q_specs, *do_specs, stat, stat, kc, kp, vc, vp, tab, tab, tab, anyspace, anyspace, anyspace],
        out_specs=[hm_all, kp, kp], out_shape=[jax.ShapeDtypeStruct(a.shape, a.dtype) for a in grads],
        input_output_aliases={n_in: 0, n_in + 1: 1, n_in + 2: 2},
        scratch_shapes=[pltpu.VMEM((dil * b, hd), F32), pltpu.VMEM((dil * b, hd), F32)],
        compiler_params=_params(("parallel", "arbitrary"), VMEM_LIMIT_ATTN_BWD_BYTES))(
            *([q_all] * rep), *([do] * rep), lse, delta, k_rot, k_rot, kv, kv, *tabs, *grads)


def _adamw(g_slabs, w, m, v, name, row0=0, prior=None):
    kk, r, c = g_slabs.shape
    tile = r if r <= 256 else _pick_rows(r, 256)
    off = row0 // tile
    assert off * tile == row0

    def body(g_ref, w_ref, m_ref, v_ref, *rest):
        go_ref, d_ref, mo_ref, vo_ref = rest[-4:]
        g = g_ref[0].astype(F32)
        for k in range(1, kk):
            g = g + g_ref[k].astype(F32)
        m2 = ADAM_B1 * m_ref[...] + (1.0 - ADAM_B1) * g
        v2 = ADAM_B2 * v_ref[...] + (1.0 - ADAM_B2) * jnp.square(g)
        m_hat = m2 / (1.0 - ADAM_B1 ** ADAM_STEP)
        v_hat = v2 / (1.0 - ADAM_B2 ** ADAM_STEP)
        go_ref[...] = g
        d_ref[...] = -ADAM_LR * (m_hat / (jnp.sqrt(v_hat) + ADAM_EPS) + ADAM_WD * w_ref[...])
        mo_ref[...] = m2
        vo_ref[...] = v2

    spec = pl.BlockSpec((tile, c), lambda i: (i + off, 0))
    prior = list(prior) if prior is not None else []
    return pl.pallas_call(
        body, name=name, grid=(r // tile,),
        in_specs=[pl.BlockSpec((kk, tile, c), lambda i: (0, i, 0)), spec, spec, spec]
        + [pl.BlockSpec(memory_space=pl.ANY)] * len(prior),
        out_specs=[spec] * 4, out_shape=[jax.ShapeDtypeStruct(w.shape, F32)] * 4,
        input_output_aliases={4 + i: i for i in range(len(prior))},
        compiler_params=_params(("parallel",)))(g_slabs, w, m, v, *prior)


def _pick_rows(r, pref):
    t = (pref // 16) * 16
    while t >= 16:
        if r % t == 0:
            return t
        t -= 16
    return r


def _coords():
    return lax.axis_index("x"), lax.axis_index("y"), lax.axis_index("c")


def _dev_index(px, py, pc):
    return 4 * px + 2 * py + pc


def _all_gather(shards, name):
    na = len(shards)

    def body(*refs):
        ins, outs = refs[:na], refs[na:2 * na]
        send_sems, recv_sems, local_sems = refs[2 * na:]
        x, y, c = _coords()
        me, sibling = (x, y, c), (x, y, 1 - c)
        chips = [(1 - x, y), (x, 1 - y), (1 - x, 1 - y)]

        def copy(a, k, block, to, src=None):
            dst = outs[a].at[_dev_index(*block)]
            return pltpu.make_async_remote_copy(
                src_ref=dst if src is None else src, dst_ref=dst, send_sem=send_sems.at[a * 7 + k],
                recv_sem=recv_sems.at[a * 7 + k], device_id=to, device_id_type=MESH)

        mine = [pltpu.make_async_copy(ins[a], outs[a].at[_dev_index(*me)], local_sems.at[a]) for a in range(na)]
        for cp in mine:
            cp.start()
        first = []
        for a in range(na):
            first.append(copy(a, 0, me, sibling, src=ins[a]))
            first += [copy(a, 1 + j, me, (*chip, c), src=ins[a]) for j, chip in enumerate(chips)]
        for cp in first:
            cp.start()
        passed = []
        for j, chip in enumerate(chips):
            for a in range(na):
                copy(a, 1 + j, (*chip, c), me).wait_recv()
                cp = copy(a, 4 + j, (*chip, c), sibling)
                cp.start()
                passed.append(cp)
        for a in range(na):
            copy(a, 0, sibling, me).wait_recv()
            for j, chip in enumerate(chips):
                copy(a, 4 + j, (*chip, 1 - c), me).wait_recv()
        for cp in first + passed:
            cp.wait_send()
        for cp in mine:
            cp.wait()

    hbm = pl.BlockSpec(memory_space=pl.ANY)
    return pl.pallas_call(
        body, name=name, in_specs=[hbm] * na, out_specs=[hbm] * na,
        out_shape=[jax.ShapeDtypeStruct((NDEV,) + s.shape, s.dtype) for s in shards],
        scratch_shapes=[pltpu.SemaphoreType.DMA((7 * na,)), pltpu.SemaphoreType.DMA((7 * na,)),
                        pltpu.SemaphoreType.DMA((na,))])(*shards)


def _exchange(slabs, whole, name, after=()):
    ns, nw = len(slabs), len(whole)
    na = ns + nw
    nb = len(after)

    def body(*refs):
        ins, outs = refs[:na], refs[na + nb:2 * na + nb]
        send_sems, recv_sems, local_sems = refs[2 * na + nb:]
        x, y, c = _coords()
        me = _dev_index(x, y, c)

        def src_of(a, p):
            return ins[a].at[p] if a < ns else ins[a]

        def copy(a, k, peer):
            p = _dev_index(*peer)
            return pltpu.make_async_remote_copy(
                src_ref=src_of(a, p), dst_ref=outs[a].at[me], send_sem=send_sems.at[a * 7 + k - 1],
                recv_sem=recv_sems.at[a * 7 + k - 1], device_id=peer, device_id_type=MESH)

        def arrival(a, k, peer):
            p = _dev_index(*peer)
            return pltpu.make_async_remote_copy(
                src_ref=src_of(a, p), dst_ref=outs[a].at[p], send_sem=send_sems.at[a * 7 + k - 1],
                recv_sem=recv_sems.at[a * 7 + k - 1], device_id=peer, device_id_type=MESH)

        mine = [pltpu.make_async_copy(src_of(a, me), outs[a].at[me], local_sems.at[a]) for a in range(na)]
        for cp in mine:
            cp.start()
        peers = [(k, (x ^ (k >> 2), y ^ ((k >> 1) & 1), c ^ (k & 1))) for k in range(1, NDEV)]
        sent = [copy(a, k, peer) for k, peer in peers for a in range(na)]
        for cp in sent:
            cp.start()
        for k, peer in peers:
            for a in range(na):
                arrival(a, k, peer).wait_recv()
        for cp in sent:
            cp.wait_send()
        for cp in mine:
            cp.wait()

    hbm = pl.BlockSpec(memory_space=pl.ANY)
    out_shape = [jax.ShapeDtypeStruct(s.shape, s.dtype) for s in slabs]
    out_shape += [jax.ShapeDtypeStruct((NDEV,) + w.shape, w.dtype) for w in whole]
    return pl.pallas_call(
        body, name=name, in_specs=[hbm] * (na + nb), out_specs=[hbm] * na, out_shape=out_shape,
        scratch_shapes=[pltpu.SemaphoreType.DMA((7 * na,)), pltpu.SemaphoreType.DMA((7 * na,)),
                        pltpu.SemaphoreType.DMA((na,))])(*slabs, *whole, *after)


_HBM = pl.BlockSpec(memory_space=pltpu.HBM)
_SEM = pl.BlockSpec(memory_space=pltpu.SEMAPHORE)
_EFFECT = pltpu.SideEffectType.DATAFLOW_SIDE_EFFECTING


def _peers(x, y, c):
    return [(k, (x ^ (k >> 2), y ^ ((k >> 1) & 1), c ^ (k & 1))) for k in range(1, NDEV)]


def _peer_copy(src, land, send_sems, recv_sems, a, k, dst_block, peer):
    return pltpu.make_async_remote_copy(
        src_ref=src, dst_ref=land.at[dst_block], send_sem=send_sems.at[a * 7 + k - 1],
        recv_sem=recv_sems.at[a * 7 + k - 1], device_id=peer, device_id_type=MESH)


def _send_start(arrays, slabs, name):
    na = len(arrays)
    lands = [jax.ShapeDtypeStruct(a.shape if slabs else (NDEV,) + a.shape, a.dtype) for a in arrays]

    def body(*refs):
        ins, zones = refs[:na], refs[na:2 * na]
        send_sems, recv_sems = refs[2 * na], refs[2 * na + 1]
        token = refs[-1]
        x, y, c = _coords()
        me = _dev_index(x, y, c)
        for k, peer in _peers(x, y, c):
            for a in range(na):
                src = ins[a].at[_dev_index(*peer)] if slabs else ins[a]
                _peer_copy(src, zones[a], send_sems, recv_sems, a, k, me, peer).start()
        token[...] = jnp.zeros_like(token)

    outs = pl.pallas_call(
        body, name=name,
        out_shape=(pltpu.SemaphoreType.DMA((7 * na,)), pltpu.SemaphoreType.DMA((7 * na,)),
                   *[pltpu.HBM(a.shape, a.dtype) for a in arrays], *[pltpu.HBM(l.shape, l.dtype) for l in lands],
                   jax.ShapeDtypeStruct((8, LANES), F32)),
        in_specs=[_HBM] * (2 * na), out_specs=(_SEM, _SEM, *([_HBM] * (2 * na)), pl.BlockSpec(memory_space=pltpu.VMEM)),
        input_output_aliases={i: 2 + i for i in range(2 * na)},
        compiler_params=pltpu.CompilerParams(has_side_effects=_EFFECT),
    )(*[pltpu.with_memory_space_constraint(a, pltpu.HBM) for a in arrays],
      *[pltpu.with_memory_space_constraint(lax.empty(l.shape, l.dtype), pltpu.HBM) for l in lands])
    return outs[0], outs[1], list(outs[2:2 + na]), list(outs[2 + na:2 + 2 * na]), outs[-1]


def _send_wait(started, after, slabs, name):
    send_sems, recv_sems, thru, zones, _ = started
    na = len(thru)

    def body(*refs):
        ins, lands = refs[:na], refs[na:2 * na]
        s_sems, r_sems = refs[2 * na], refs[2 * na + 1]
        x, y, c = _coords()
        for k, peer in _peers(x, y, c):
            p = _dev_index(*peer)
            for a in range(na):
                src = ins[a].at[p] if slabs else ins[a]
                cp = _peer_copy(src, lands[a], s_sems, r_sems, a, k, p, peer)
                cp.wait_send()
                cp.wait_recv()

    outs = pl.pallas_call(
        body, name=name, out_shape=tuple(pltpu.HBM(v.shape, v.dtype) for v in thru + zones),
        in_specs=[_HBM] * (2 * na) + [_SEM, _SEM, pl.BlockSpec(memory_space=pl.ANY)], out_specs=tuple([_HBM] * (2 * na)),
        input_output_aliases={i: i for i in range(2 * na)},
        compiler_params=pltpu.CompilerParams(has_side_effects=_EFFECT),
    )(*thru, *zones, send_sems, recv_sems, after)
    me = _dev_index(*_coords())
    filled = []
    for a in range(na):
        own = lax.dynamic_index_in_dim(outs[a], me, 0, keepdims=False) if slabs else outs[a]
        filled.append(lax.dynamic_update_index_in_dim(outs[na + a], own, me, 0))
    return filled


def _pack(vecs):
    parts, spans, off = [], [], 0
    for v in vecs:
        n = v.size
        pad = (-n) % LANES
        parts.append(jnp.pad(v.reshape(-1).astype(F32), (0, pad)))
        spans.append((off, n))
        off += n + pad
    return jnp.concatenate(parts).reshape(-1, LANES), spans


def _pad_lanes(v):
    v = v.reshape(1, -1)
    return jnp.pad(v, ((0, 0), (0, LANES - v.shape[1])))


def _cols_to_slabs(g):
    sh = g.shape
    g = g.reshape(sh[:-1] + (NDEV, sh[-1] // NDEV))
    return jnp.moveaxis(g, -2, 0)


def _rows_to_slabs(g):
    sh = g.shape
    g = g.reshape(sh[:-2] + (NDEV, sh[-2] // NDEV, sh[-1]))
    return jnp.moveaxis(g, -3, 0)


def _slabs_to_cols(a):
    a = jnp.moveaxis(a, 0, -2)
    return a.reshape(a.shape[:-2] + (a.shape[-2] * a.shape[-1],))


def _slabs_to_rows(a):
    a = jnp.moveaxis(a, 0, -3)
    return a.reshape(a.shape[:-3] + (a.shape[-3] * a.shape[-2], a.shape[-1]))


def _ffn_forward(x, norm_w, w_up, fcw, wdown, tag):
    u, h = _mm(x, w_up, rms_fwd=norm_w, name=f"{tag}_up")
    f = _ffn_gate_fwd(u, fcw, f"{tag}_gate")
    return _mm(f, wdown, res=x, name=f"{tag}_down"), (h, u, f)


def _ffn_backward(x, saved, dout, dout_b, norm_w, w_up, fcw, wdown, tag):
    h, u, f = saved
    dwdown = _mm(f, dout_b, ta=True, name=f"{tag}_dwdown")
    df = _mm(dout_b, wdown, tb=True, name=f"{tag}_df")
    du, dfc = _ffn_gate_bwd(u, fcw, df, f"{tag}_gate_bwd")
    dwup = _mm(h, du, ta=True, name=f"{tag}_dwup")
    dx, dxb, dnorm = _mm(du, w_up, tb=True, rms_bwd=(x, norm_w, dout), tm=RMS_BWD_ROWS, name=f"{tag}_dh")
    return dx, dxb, (dwup, jnp.concatenate([dfc[0], dfc[1]], axis=1), dwdown, dnorm)


def kernel(x, a_norm, ssm_w_in, ssm_conv_w, ssm_conv_b, ssm_dt_bias, ssm_a_log, ssm_d, ssm_norm, ssm_w_out, kv_norm, w_kv, b_norm, att_w_q, att_w_o, ffn_norm, ffn_w_up, ffn_conv_w, ffn_w_down, final_norm, loss_target, m_a_norm, m_ssm_w_in, m_ssm_conv_w, m_ssm_conv_b, m_ssm_dt_bias, m_ssm_a_log, m_ssm_d, m_ssm_norm, m_ssm_w_out, m_kv_norm, m_w_kv, m_b_norm, m_att_w_q, m_att_w_o, m_ffn_norm, m_ffn_w_up, m_ffn_conv_w, m_ffn_w_down, m_final_norm, v_a_norm, v_ssm_w_in, v_ssm_conv_w, v_ssm_conv_b, v_ssm_dt_bias, v_ssm_a_log, v_ssm_d, v_ssm_norm, v_ssm_w_out, v_kv_norm, v_w_kv, v_b_norm, v_att_w_q, v_att_w_o, v_ffn_norm, v_ffn_w_up, v_ffn_conv_w, v_ffn_w_down, v_final_norm):
    given = dict(locals())
    xs, tgt = x[0], loss_target[0]
    s, d = xs.shape
    di = ssm_w_out.shape[1] * NDEV
    nh = ssm_dt_bias.shape[1]
    ng = SSM_N_GROUPS
    convd = di + 2 * ng * SSM_D_STATE
    f = ffn_w_down.shape[1] * NDEV
    n_att = len(ATT_PATTERNS)
    qg = ATT_HEADS_PER_GROUP * ATT_HEAD_DIM
    kg = ATT_KV_HEADS_PER_GROUP * ATT_HEAD_DIM
    kvd = n_att * kg
    assert all(w // dil == ATT_BLOCK for w, dil in ATT_PATTERNS)

    small, _ = _pack([a_norm, ssm_conv_w, ssm_conv_b, ssm_norm, ffn_conv_w])
    gat = _all_gather([ssm_w_in[0].astype(BF16), small], "gather_weights")
    first = _send_start([ssm_w_out[0].astype(BF16), ffn_w_up[0].astype(BF16), ffn_w_down[0].astype(BF16)], False,
                        "gather_ffn0_start")
    rest = _send_start([b.astype(BF16) for b in (w_kv, att_w_q[0], att_w_o[0], ffn_w_up[1], ffn_w_down[1])], False,
                       "gather_rest_start")
    w_in = _slabs_to_cols(gat[0])
    in_dim = di + convd + nh
    in_pad = di + convd + LANES
    w_in = jnp.pad(w_in, ((0, 0), (0, in_pad - in_dim)))
    sm = gat[1].reshape(NDEV, -1)
    o0 = 0

    def take(shape):
        nonlocal o0
        n = math.prod(shape)
        out = sm[:, o0:o0 + n].reshape((NDEV,) + shape)
        o0 += n + (-n) % LANES
        return out
    a_norm_f = _slabs_to_cols(take(a_norm.shape)) + (first[-1][0, 0] + rest[-1][0, 0])
    conv_w_f = _slabs_to_cols(take(ssm_conv_w.shape))[0]
    conv_b_f = _slabs_to_cols(take(ssm_conv_b.shape))
    ssm_norm_f = _slabs_to_cols(take(ssm_norm.shape))
    fcw = _slabs_to_cols(take(ffn_conv_w.shape))
    dtb, alog, dsk = _pad_lanes(ssm_dt_bias), _pad_lanes(ssm_a_log), _pad_lanes(ssm_d)
    kvn, fin = kv_norm.reshape(1, d), final_norm.reshape(1, d)

    zx, h0 = _mm(xs, w_in, rms_fwd=a_norm_f, name="in_proj")
    z, dtr = (zx, di, 0), (zx, LANES, (di + convd) // LANES)
    xbc = _conv_silu_fwd(zx, di, conv_w_f, conv_b_f, "ssm_conv")
    dt = _softplus_fwd(dtr, dtb, "ssm_dt")
    y, prevs = _ssd_fwd(xbc, dt, alog, dsk, di, nh, ng, "ssd")
    yn = _gnorm_fwd(y, z, ssm_norm_f, ng, "ssm_gnorm")
    got = _send_wait(first, yn, False, "gather_ffn0_wait")
    w_out = _slabs_to_rows(got[0])
    w_up0, w_down0 = _slabs_to_cols(got[1]), _slabs_to_rows(got[2])
    x1 = _mm(yn, w_out, res=xs, name="ssm_out")
    x2, ffn0 = _ffn_forward(x1, ffn_norm[0:1], w_up0, fcw[0], w_down0, "ffn0")
    got = _send_wait(rest, x2, False, "gather_rest_wait")
    w_kvf = _slabs_to_cols(got[0])
    w_q = _slabs_to_cols(got[1])
    w_o = _slabs_to_rows(got[2])
    w_up1, w_down1 = _slabs_to_cols(got[3]), _slabs_to_rows(got[4])
    kv, hk = _mm(x2, w_kvf, rms_fwd=kvn, name="kv_proj")
    q, h2 = _mm(x2, w_q, rms_fwd=b_norm, name="q_proj")
    tabs = _rot_tables(s)
    k_rot = _rot_heads(kv, tabs, kvd, 1.0, F32, "k_rot")
    att = [_attn_fwd_nat(q, k_rot, kv, tabs, g, dil, f"attn{g}") for g, (_, dil) in enumerate(ATT_PATTERNS)]
    o, ob, lse = _merge_heads([t[0] for t in att], [t[1] for t in att], "attn_merge")
    x3 = _mm(ob, w_o, res=x2, name="attn_out")
    x4, ffn1 = _ffn_forward(x3, ffn_norm[1:2], w_up1, fcw[1], w_down1, "ffn1")
    loss_part, dx4, dx4b, dfin = _final_loss(x4, fin, tgt, "loss_head")

    dx3, dx3b, (dwup1, dfc1, dwdown1, dfn1) = _ffn_backward(
        x3, ffn1, dx4, dx4b, ffn_norm[1:2], w_up1, fcw[1], w_down1, "ffn1")
    dw_o = _mm(ob, dx3b, ta=True, name="attn_dwo")
    do = _mm(dx3b, w_o, tb=True, name="attn_do")
    delta = _delta_heads(do, o, "attn_delta")
    grads = (lax.empty((n_att * qg // LANES, s, LANES), F32), lax.empty((s, kvd), F32), lax.empty((s, kvd), F32))
    for g, (_, dil) in enumerate(ATT_PATTERNS):
        grads = _attn_bwd_nat(q, k_rot, kv, do, lse, delta, tabs, grads, g, dil, f"attn{g}_bwd")
    dq, dk_rot, dv = grads
    dkv = _kv_grad(dk_rot, dv, tabs, "kv_grad")
    dw_q = _mm(h2, dq, ta=True, b_heads=True, name="q_dw")
    dx2, _, db_norm = _mm(dq, w_q, tb=True, a_heads=True, rms_bwd=(x2, b_norm, dx3), tm=RMS_BWD_ROWS, name="q_dh")
    dw_kv = _mm(hk, dkv, ta=True, name="kv_dw")
    dx2, dx2b, dkv_norm = _mm(dkv, w_kvf, tb=True, rms_bwd=(x2, kvn, dx2), tm=RMS_BWD_ROWS, name="kv_dh")
    sent1 = _send_start([_cols_to_slabs(dwup1).astype(BF16), _rows_to_slabs(dwdown1).astype(BF16),
                         _cols_to_slabs(dw_kv).astype(BF16), _cols_to_slabs(dw_q).astype(BF16),
                         _rows_to_slabs(dw_o).astype(BF16)], True, "grads_late_start")
    dx1, dx1b, (dwup0, dfc0, dwdown0, dfn0) = _ffn_backward(
        x1, ffn0, dx2, dx2b, ffn_norm[0:1], w_up0, fcw[0] + sent1[-1][0, 0], w_down0, "ffn0")
    dw_out = _mm(yn, dx1b, ta=True, name="ssm_dwout")
    sent0 = _send_start([_cols_to_slabs(dwup0).astype(BF16), _rows_to_slabs(dwdown0).astype(BF16),
                         _rows_to_slabs(dw_out).astype(BF16)], True, "grads_ffn0_start")
    dyn = _mm(dx1b, w_out, tb=True, name="ssm_dyn")
    dzx = lax.empty((s, in_pad), BF16)
    dy, dzx, dssm_norm = _gnorm_bwd(dyn, y, z, ssm_norm_f + sent0[-1][0, 0], ng, (dzx, 0), "ssm_gnorm_bwd")
    dxbc, ddt, dalog, ddsk = _ssd_bwd(xbc, dt, alog, dsk, prevs, dy, di, nh, ng, "ssd_bwd")
    dzx, ddtb = _softplus_bwd(ddt, dtr, dtb, nh, (dzx, (di + convd) // LANES), "ssm_dt_bwd")
    dzx, dconv_w, dconv_b = _conv_silu_bwd(zx, di, conv_w_f, conv_b_f, dxbc, dzx, "ssm_conv_bwd")
    dw_in = _mm(h0, dzx, ta=True, name="in_dw")
    sent_m = _send_start([_cols_to_slabs(dw_in[:, :in_dim])], True, "grads_mamba_start")
    dx0, _, da_norm = _mm(dzx, w_in, tb=True, rms_bwd=(xs, a_norm_f + sent_m[-1][0, 0], dx1), tm=RMS_BWD_ROWS,
                          name="in_dh")

    small_full = {
        'a_norm': da_norm, 'ssm_conv_w': dconv_w[None], 'ssm_conv_b': dconv_b, 'ssm_dt_bias': ddtb[:, :nh],
        'ssm_a_log': dalog[:, :nh], 'ssm_d': ddsk[:, :nh], 'ssm_norm': dssm_norm, 'kv_norm': dkv_norm.reshape(d),
        'b_norm': db_norm, 'ffn_norm': jnp.concatenate([dfn0, dfn1], axis=0), 'ffn_conv_w': jnp.stack([dfc0, dfc1]),
        'final_norm': dfin.reshape(d),
    }
    small_names = list(small_full)
    packed, spans = _pack([small_full[n] for n in small_names])
    got1 = _send_wait(sent1, dx0, True, "grads_late_wait")
    got0 = _send_wait(sent0, dx0, True, "grads_ffn0_wait")
    recv_big = {'w_kv': [got1[2]], 'att_w_q': [got1[3]], 'att_w_o': [got1[4]], 'ffn_w_up': [got0[0], got1[0]],
                'ffn_w_down': [got0[1], got1[1]]}

    me = _dev_index(*_coords())
    res = {}

    def update_big(n, layers):
        w = given[n]
        c = w.shape[-1]
        outs, row0 = None, 0
        for k, r in enumerate(layers):
            g = r.reshape(NDEV, -1, c)
            outs = _adamw(g, w.reshape(-1, c), given['m_' + n].reshape(-1, c), given['v_' + n].reshape(-1, c),
                          f"adamw_{n}_{k}", row0=row0, prior=outs)
            row0 += g.shape[1]
        res[n] = [o_.reshape(w.shape) for o_ in outs]
    for n, r in recv_big.items():
        update_big(n, r)
    update_big('ssm_w_out', [got0[2]])
    recv = _exchange([], [packed], "exchange_grads", after=[res[n][1] for n in res])
    small_sum = _sum_slabs(recv[-1], "sum_small_grads").reshape(-1)
    gotm = _send_wait(sent_m, recv[-1], True, "grads_mamba_wait")
    update_big('ssm_w_in', [gotm[0]])
    sharded_small = {'a_norm', 'ssm_conv_w', 'ssm_conv_b', 'ssm_norm', 'ffn_conv_w'}
    for n, (off, size) in zip(small_names, spans):
        w = given[n]
        gfull = small_sum[off:off + size].reshape(small_full[n].shape)
        if n in sharded_small:
            c = w.shape[-1]
            gfull = lax.dynamic_slice_in_dim(gfull, me * c, c, axis=gfull.ndim - 1)
        c = w.shape[-1]
        outs = _adamw(gfull.reshape(1, -1, c), w.reshape(-1, c), given['m_' + n].reshape(-1, c),
                      given['v_' + n].reshape(-1, c), f"adamw_{n}")
        res[n] = [o_.reshape(w.shape) for o_ in outs]

    loss = lax.psum(loss_part[0, 0], AXES)
    return (loss, dx0[None], *[res[n][0] for n in WEIGHTS], *[res[n][1] for n in WEIGHTS],
            *[res[n][2] for n in WEIGHTS], *[res[n][3] for n in WEIGHTS])
```
